```python
import jax, jax.numpy as jnp
from jax import lax
import numpy as np

D_MODEL = 2048
BATCH = 8
SEQ = 8192
DEPTH = 4

CHUNK = 64
LEFT_CHUNKS = 8
BAND = (LEFT_CHUNKS + 1) * CHUNK
MAX_REL = 128
N_MEM = 256

MIX_WIDTH = D_MODEL
A_WIDTH = MIX_WIDTH // 2
A_HEADS = 8
A_HEAD_DIM = A_WIDTH // A_HEADS
B_WIDTH = MIX_WIDTH // 4
B_HEADS = 4
B_DV = B_WIDTH // B_HEADS
B_DK = B_DV // 2
B_KEY_WIDTH = B_HEADS * B_DK
GATE_RANK = 16
GATE_TAU = 16.0
M_WIDTH = MIX_WIDTH // 4
M_HEADS = 4
M_HEAD_DIM = M_WIDTH // M_HEADS

IN_SPLITS = (A_WIDTH, A_WIDTH, A_WIDTH, A_WIDTH,
             B_KEY_WIDTH, B_KEY_WIDTH, B_WIDTH, B_WIDTH, GATE_RANK,
             M_WIDTH, M_WIDTH)
IN_WIDTH = sum(IN_SPLITS)
VALUE_SPLITS = (2, 6)

DEEPNORM_ALPHA = (2.0 * DEPTH) ** 0.25
DEEPNORM_BETA = (8.0 * DEPTH) ** -0.25
LN_EPS = 1e-5
RMS_EPS = 1e-6
NEG_INF = -1e30

kernel_name = "hybrid_chunk_attn_gla_mem_deepnorm"


def layer_norm(x, g, b):
    xf = x.astype(jnp.float32)
    mu = jnp.mean(xf, axis=-1, keepdims=True)
    var = jnp.mean(jnp.square(xf - mu), axis=-1, keepdims=True)
    y = (xf - mu) * lax.rsqrt(var + LN_EPS) * g.astype(jnp.float32) + b.astype(jnp.float32)
    return y.astype(x.dtype)


def chunk_band_attention(q, k, v, rel_table):
    B, S, H, Dh = q.shape
    nc = S // CHUNK
    qc = q.reshape(B, nc, CHUNK, H, Dh)
    pad = ((0, 0), (LEFT_CHUNKS * CHUNK, 0), (0, 0), (0, 0))
    kp = jnp.pad(k, pad).reshape(B, nc + LEFT_CHUNKS, CHUNK, H, Dh)
    vp = jnp.pad(v, pad).reshape(B, nc + LEFT_CHUNKS, CHUNK, H, Dh)
    k_band = jnp.concatenate([kp[:, i:i + nc] for i in range(LEFT_CHUNKS + 1)], axis=2)
    v_band = jnp.concatenate([vp[:, i:i + nc] for i in range(LEFT_CHUNKS + 1)], axis=2)
    scores = jnp.einsum('bnqhd,bnkhd->bnhqk', qc, k_band).astype(jnp.float32) * (Dh ** -0.5)
    dist = jnp.arange(CHUNK)[:, None] + LEFT_CHUNKS * CHUNK - jnp.arange(BAND)[None, :]
    rel_idx = jnp.clip(dist, -MAX_REL, MAX_REL) + MAX_REL
    bias = rel_table[:, rel_idx].astype(jnp.float32)
    key_chunk = jnp.arange(nc)[:, None] - LEFT_CHUNKS + (jnp.arange(BAND) // CHUNK)[None, :]
    valid = key_chunk >= 0
    scores = jnp.where(valid[None, :, None, None, :], scores + bias[None, None], NEG_INF)
    p = jax.nn.softmax(scores, axis=-1).astype(v.dtype)
    out = jnp.einsum('bnhqk,bnkhd->bnqhd', p, v_band)
    return out.reshape(B, S, H, Dh)


def gla_chunk_recurrence(q, k, v, log_g):
    B, S, H, DK = q.shape
    DV = v.shape[-1]
    nc = S // CHUNK
    f32 = jnp.float32

    def to_chunks(t):
        return jnp.moveaxis(t.astype(f32).reshape(B, nc, CHUNK, H, t.shape[-1]), 1, 0)

    qs = to_chunks(q * (DK ** -0.5))
    ks, vs, gs = to_chunks(k), to_chunks(v), to_chunks(log_g)

    def step(state, inp):
        qc, kc, vc, gc = inp
        b = jnp.cumsum(gc, axis=1)
        decay = jnp.exp(-jnp.abs(b[:, :, None] - b[:, None, :]))
        attn = jnp.einsum('bihd,bjhd,bijhd->bhij', qc, kc, decay)
        o_intra = jnp.einsum('bhij,bjhv->bihv', attn, vc)
        o_inter = jnp.einsum('bihd,bhdv->bihv', qc * jnp.exp(b), state)
        b_last = b[:, -1]
        k_dec = kc * jnp.exp(b_last[:, None] - b)
        new_state = jnp.exp(b_last)[..., None] * state + jnp.einsum('bjhd,bjhv->bhdv', k_dec, vc)
        return new_state, o_intra + o_inter

    state0 = jnp.zeros((B, H, DK, DV), f32)
    _, out = lax.scan(step, state0, (qs, ks, vs, gs))
    return jnp.moveaxis(out, 0, 1).reshape(B, S, H, DV)


def memory_attention(q, mk, mv):
    s = jnp.einsum('bshd,bmhd->bhsm', q, mk).astype(jnp.float32) * (q.shape[-1] ** -0.5)
    p = jax.nn.softmax(s, axis=-1).astype(mv.dtype)
    return jnp.einsum('bhsm,bmhd->bshd', p, mv)


def hybrid_layer(x, mem, w_in, rel_table, gate_w, gate_b, gla_norm_g, w_mem_kv, w_out, ln_g, ln_b):
    B, S, _ = x.shape
    h = x @ w_in
    cuts = [int(c) for c in np.cumsum(IN_SPLITS)[:-1]]
    a_q, a_k, a_v, a_z, b_q, b_k, b_v, b_z, b_lr, m_q, m_z = jnp.split(h, cuts, axis=-1)

    heads_a = lambda t: t.reshape(B, S, A_HEADS, A_HEAD_DIM)
    y_a = chunk_band_attention(heads_a(a_q), heads_a(a_k), heads_a(a_v), rel_table).reshape(B, S, A_WIDTH)

    gate_logit = (b_lr @ gate_w + gate_b).astype(jnp.float32)
    log_g = jax.nn.log_sigmoid(gate_logit) / GATE_TAU
    heads_k = lambda t: t.reshape(B, S, B_HEADS, B_DK)
    o_b = gla_chunk_recurrence(heads_k(b_q), heads_k(b_k), b_v.reshape(B, S, B_HEADS, B_DV), heads_k(log_g))
    o_b = o_b * lax.rsqrt(jnp.mean(jnp.square(o_b), axis=-1, keepdims=True) + RMS_EPS) * gla_norm_g.astype(jnp.float32)
    y_b = o_b.reshape(B, S, B_WIDTH).astype(x.dtype)

    mkv = mem @ w_mem_kv
    mk, mv = jnp.split(mkv, 2, axis=-1)
    heads_m = lambda t: t.reshape(t.shape[0], t.shape[1], M_HEADS, M_HEAD_DIM)
    y_m = memory_attention(heads_m(m_q), heads_m(mk), heads_m(mv)).reshape(B, S, M_WIDTH)

    y = jnp.concatenate([y_a * jax.nn.silu(a_z), y_b * jax.nn.silu(b_z), y_m * jax.nn.silu(m_z)], axis=-1)
    out = y @ w_out
    return layer_norm(DEEPNORM_ALPHA * x + out, ln_g, ln_b)


def _fwd_setup_inputs(seed: int = 0) -> dict:
    key = jax.random.key(seed)
    ks = jax.random.split(key, 12)
    f32 = jnp.float32
    x = jax.random.normal(ks[0], (BATCH, SEQ, D_MODEL), f32)
    mem = jax.random.normal(ks[1], (N_MEM, D_MODEL), f32)[None].repeat(BATCH, axis=0) \
        + 0.1 * jax.random.normal(ks[2], (BATCH, N_MEM, D_MODEL), f32)
    col_scale = jnp.concatenate([
        jnp.full((n,), DEEPNORM_BETA if i in VALUE_SPLITS else 1.0, f32) for i, n in enumerate(IN_SPLITS)])
    w_in = jax.random.normal(ks[3], (DEPTH, D_MODEL, IN_WIDTH), f32) * (D_MODEL ** -0.5) * col_scale
    a_rel_bias = 0.1 * jax.random.normal(ks[4], (DEPTH, A_HEADS, 2 * MAX_REL + 1), f32)
    b_gate_w = jax.random.normal(ks[5], (DEPTH, GATE_RANK, B_KEY_WIDTH), f32) * (GATE_RANK ** -0.5)
    b_gate_b = 0.1 * jax.random.normal(ks[6], (DEPTH, B_KEY_WIDTH), f32)
    b_norm_g = 1.0 + 0.02 * jax.random.normal(ks[7], (DEPTH, B_DV), f32)
    kv_scale = jnp.concatenate([jnp.ones((M_WIDTH,), f32), jnp.full((M_WIDTH,), DEEPNORM_BETA, f32)])
    w_mem_kv = jax.random.normal(ks[8], (DEPTH, D_MODEL, 2 * M_WIDTH), f32) * (D_MODEL ** -0.5) * kv_scale
    w_out = jax.random.normal(ks[9], (DEPTH, MIX_WIDTH, D_MODEL), f32) * (MIX_WIDTH ** -0.5) * DEEPNORM_BETA
    ln_g = 1.0 + 0.02 * jax.random.normal(ks[10], (DEPTH, D_MODEL), f32)
    ln_b = 0.02 * jax.random.normal(ks[11], (DEPTH, D_MODEL), f32)
    return {"x": x, "mem": mem, "w_in": w_in, "a_rel_bias": a_rel_bias, "b_gate_w": b_gate_w,
            "b_gate_b": b_gate_b, "b_norm_g": b_norm_g, "w_mem_kv": w_mem_kv, "w_out": w_out,
            "ln_g": ln_g, "ln_b": ln_b}


def _fwd_reference(x, mem, w_in, a_rel_bias, b_gate_w, b_gate_b, b_norm_g, w_mem_kv, w_out, ln_g, ln_b):
    h = x
    for l in range(DEPTH):
        h = hybrid_layer(h, mem, w_in[l], a_rel_bias[l], b_gate_w[l], b_gate_b[l], b_norm_g[l],
                         w_mem_kv[l], w_out[l], ln_g[l], ln_b[l])
    return h


import jax as _jax
import jax.numpy as _jnp

TWIN_FORMAT = 'train_step'
FWD_PARAMS = ['x', 'mem', 'w_in', 'a_rel_bias', 'b_gate_w', 'b_gate_b', 'b_norm_g', 'w_mem_kv', 'w_out', 'ln_g', 'ln_b']
TWIN_WEIGHTS = ['w_in', 'a_rel_bias', 'b_gate_w', 'b_gate_b', 'b_norm_g', 'w_mem_kv', 'w_out', 'ln_g', 'ln_b']
TWIN_DIFF_INPUT = 'x'
TWIN_INPUTS = ['x', 'mem', 'w_in', 'a_rel_bias', 'b_gate_w', 'b_gate_b', 'b_norm_g', 'w_mem_kv', 'w_out', 'ln_g', 'ln_b', 'loss_target', 'm_w_in', 'm_a_rel_bias', 'm_b_gate_w', 'm_b_gate_b', 'm_b_norm_g', 'm_w_mem_kv', 'm_w_out', 'm_ln_g', 'm_ln_b', 'v_w_in', 'v_a_rel_bias', 'v_b_gate_w', 'v_b_gate_b', 'v_b_norm_g', 'v_w_mem_kv', 'v_w_out', 'v_ln_g', 'v_ln_b']
TWIN_OUTPUTS = ['loss', 'grad_x', 'grad_w_in', 'grad_a_rel_bias', 'grad_b_gate_w', 'grad_b_gate_b', 'grad_b_norm_g', 'grad_w_mem_kv', 'grad_w_out', 'grad_ln_g', 'grad_ln_b', 'delta_w_in', 'delta_a_rel_bias', 'delta_b_gate_w', 'delta_b_gate_b', 'delta_b_norm_g', 'delta_w_mem_kv', 'delta_w_out', 'delta_ln_g', 'delta_ln_b', 'new_m_w_in', 'new_m_a_rel_bias', 'new_m_b_gate_w', 'new_m_b_gate_b', 'new_m_b_norm_g', 'new_m_w_mem_kv', 'new_m_w_out', 'new_m_ln_g', 'new_m_ln_b', 'new_v_w_in', 'new_v_a_rel_bias', 'new_v_b_gate_w', 'new_v_b_gate_b', 'new_v_b_norm_g', 'new_v_w_mem_kv', 'new_v_w_out', 'new_v_ln_g', 'new_v_ln_b']
TWIN_LEAF_KINDS = {'loss': 'loss', 'grad_x': 'grad_x', 'grad_w_in': 'grad_w', 'grad_a_rel_bias': 'grad_w', 'grad_b_gate_w': 'grad_w', 'grad_b_gate_b': 'grad_w', 'grad_b_norm_g': 'grad_w', 'grad_w_mem_kv': 'grad_w', 'grad_w_out': 'grad_w', 'grad_ln_g': 'grad_w', 'grad_ln_b': 'grad_w', 'delta_w_in': 'delta_w', 'delta_a_rel_bias': 'delta_w', 'delta_b_gate_w': 'delta_w', 'delta_b_gate_b': 'delta_w', 'delta_b_norm_g': 'delta_w', 'delta_w_mem_kv': 'delta_w', 'delta_w_out': 'delta_w', 'delta_ln_g': 'delta_w', 'delta_ln_b': 'delta_w', 'new_m_w_in': 'new_m', 'new_m_a_rel_bias': 'new_m', 'new_m_b_gate_w': 'new_m', 'new_m_b_gate_b': 'new_m', 'new_m_b_norm_g': 'new_m', 'new_m_w_mem_kv': 'new_m', 'new_m_w_out': 'new_m', 'new_m_ln_g': 'new_m', 'new_m_ln_b': 'new_m', 'new_v_w_in': 'new_v', 'new_v_a_rel_bias': 'new_v', 'new_v_b_gate_w': 'new_v', 'new_v_b_gate_b': 'new_v', 'new_v_b_norm_g': 'new_v', 'new_v_w_mem_kv': 'new_v', 'new_v_w_out': 'new_v', 'new_v_ln_g': 'new_v', 'new_v_ln_b': 'new_v'}


def _forward(args):
    return _fwd_reference(*[args[k] for k in FWD_PARAMS])


def _output_shape():
    def fwd():
        inp = _fwd_setup_inputs(0)
        return _fwd_reference(*[inp[k] for k in FWD_PARAMS])
    out = _jax.eval_shape(fwd)
    return out.shape, out.dtype

N_MICROBATCH = 1
ADAM_LR = 0.001
ADAM_B1 = 0.9
ADAM_B2 = 0.999
ADAM_EPS = 1e-08
ADAM_WD = 0.01
ADAM_STEP = 10
PER_EXAMPLE_BATCH_AXIS = {'x': 0, 'mem': 0, 'loss_target': 0}
SHARED_INPUTS = []
_WEIGHT_DTYPES = {'w_in': _jnp.float32, 'a_rel_bias': _jnp.float32, 'b_gate_w': _jnp.float32, 'b_gate_b': _jnp.float32, 'b_norm_g': _jnp.float32, 'w_mem_kv': _jnp.float32, 'w_out': _jnp.float32, 'ln_g': _jnp.float32, 'ln_b': _jnp.float32}
MOMENT_SCALE = {'w_in': 1.563319e-02, 'a_rel_bias': 7.692383e-04, 'b_gate_w': 2.961618e-03, 'b_gate_b': 1.221086e-02, 'b_norm_g': 4.376684e-02, 'w_mem_kv': 1.729361e-03, 'w_out': 2.257188e-02, 'ln_g': 1.605857e+01, 'ln_b': 6.639695e-01}


def _to_microbatches(a, axis):
    t = _jnp.moveaxis(a, axis, 0)
    t = t.reshape((N_MICROBATCH, t.shape[0] // N_MICROBATCH) + t.shape[1:])
    return _jnp.moveaxis(t, 1, axis + 1)


def setup_inputs(seed: int = 0) -> dict:
    inp = _fwd_setup_inputs(seed)
    key = _jax.random.fold_in(_jax.random.key(seed), 7919)
    shape, _ = _output_shape()
    out = dict(inp)
    out["loss_target"] = _jax.random.normal(_jax.random.fold_in(key, 0), shape, _jnp.float32)
    for i, name in enumerate(TWIN_WEIGHTS):
        w = inp[name].astype(_jnp.float32)
        if MOMENT_SCALE is None:
            s = _jnp.sqrt(_jnp.mean(_jnp.square(w)) + 1e-30)
        else:
            s = MOMENT_SCALE[name]
        km, kv = _jax.random.split(_jax.random.fold_in(key, i + 1))
        out[name] = w
        out["m_" + name] = s * _jax.random.normal(km, w.shape, _jnp.float32)
        out["v_" + name] = (s * s) * _jax.random.uniform(kv, w.shape, _jnp.float32, 0.5, 1.5)
    if N_MICROBATCH > 1:
        for name, axis in PER_EXAMPLE_BATCH_AXIS.items():
            out[name] = _to_microbatches(out[name], axis)
    return {'x': out['x'], 'mem': out['mem'], 'w_in': out['w_in'], 'a_rel_bias': out['a_rel_bias'], 'b_gate_w': out['b_gate_w'], 'b_gate_b': out['b_gate_b'], 'b_norm_g': out['b_norm_g'], 'w_mem_kv': out['w_mem_kv'], 'w_out': out['w_out'], 'ln_g': out['ln_g'], 'ln_b': out['ln_b'], 'loss_target': out['loss_target'], 'm_w_in': out['m_w_in'], 'm_a_rel_bias': out['m_a_rel_bias'], 'm_b_gate_w': out['m_b_gate_w'], 'm_b_gate_b': out['m_b_gate_b'], 'm_b_norm_g': out['m_b_norm_g'], 'm_w_mem_kv': out['m_w_mem_kv'], 'm_w_out': out['m_w_out'], 'm_ln_g': out['m_ln_g'], 'm_ln_b': out['m_ln_b'], 'v_w_in': out['v_w_in'], 'v_a_rel_bias': out['v_a_rel_bias'], 'v_b_gate_w': out['v_b_gate_w'], 'v_b_gate_b': out['v_b_gate_b'], 'v_b_norm_g': out['v_b_norm_g'], 'v_w_mem_kv': out['v_w_mem_kv'], 'v_w_out': out['v_w_out'], 'v_ln_g': out['v_ln_g'], 'v_ln_b': out['v_ln_b']}


def _loss(weights, diff, rest, loss_target):
    with _jax.named_scope("forward"):
        args = {**rest, TWIN_DIFF_INPUT: diff, **{k: w.astype(_WEIGHT_DTYPES[k]) for k, w in weights.items()}}
        y = _forward(args)
    with _jax.named_scope("loss_head"):
        err = _jnp.square(y.astype(_jnp.float32) - loss_target)
        return 0.5 * _jnp.sum(_jnp.mean(err, axis=-1)) if err.ndim else 0.5 * err


def _adamw(w, g, m, v):
    m = ADAM_B1 * m + (1.0 - ADAM_B1) * g
    v = ADAM_B2 * v + (1.0 - ADAM_B2) * _jnp.square(g)
    m_hat = m / (1.0 - ADAM_B1 ** ADAM_STEP)
    v_hat = v / (1.0 - ADAM_B2 ** ADAM_STEP)
    delta = -ADAM_LR * (m_hat / (_jnp.sqrt(v_hat) + ADAM_EPS) + ADAM_WD * w)
    return delta, m, v


def reference(x, mem, w_in, a_rel_bias, b_gate_w, b_gate_b, b_norm_g, w_mem_kv, w_out, ln_g, ln_b, loss_target, m_w_in, m_a_rel_bias, m_b_gate_w, m_b_gate_b, m_b_norm_g, m_w_mem_kv, m_w_out, m_ln_g, m_ln_b, v_w_in, v_a_rel_bias, v_b_gate_w, v_b_gate_b, v_b_norm_g, v_w_mem_kv, v_w_out, v_ln_g, v_ln_b):
    given = dict(x=x, mem=mem, w_in=w_in, a_rel_bias=a_rel_bias, b_gate_w=b_gate_w, b_gate_b=b_gate_b, b_norm_g=b_norm_g, w_mem_kv=w_mem_kv, w_out=w_out, ln_g=ln_g, ln_b=ln_b, loss_target=loss_target, m_w_in=m_w_in, m_a_rel_bias=m_a_rel_bias, m_b_gate_w=m_b_gate_w, m_b_gate_b=m_b_gate_b, m_b_norm_g=m_b_norm_g, m_w_mem_kv=m_w_mem_kv, m_w_out=m_w_out, m_ln_g=m_ln_g, m_ln_b=m_ln_b, v_w_in=v_w_in, v_a_rel_bias=v_a_rel_bias, v_b_gate_w=v_b_gate_w, v_b_gate_b=v_b_gate_b, v_b_norm_g=v_b_norm_g, v_w_mem_kv=v_w_mem_kv, v_w_out=v_w_out, v_ln_g=v_ln_g, v_ln_b=v_ln_b)
    weights = {n: given[n] for n in TWIN_WEIGHTS}
    shared = {n: given[n] for n in SHARED_INPUTS}
    per_example = {n: given[n] for n in ['x', 'mem']}
    grad_fn = _jax.value_and_grad(_loss, argnums=(0, 1))

    def one_microbatch(ex, loss_target):
        ex = dict(ex)
        diff = ex.pop(TWIN_DIFF_INPUT)
        return grad_fn(weights, diff, {**shared, **ex}, loss_target)

    if N_MICROBATCH == 1:
        loss, (grad_w, grad_x) = one_microbatch(per_example, given["loss_target"])
    else:
        def body(carry, xs):
            loss_sum, grad_sum = carry
            l_k, (gw_k, gx_k) = one_microbatch(xs[0], xs[1])
            with _jax.named_scope("update"):
                return (loss_sum + l_k, _jax.tree.map(_jnp.add, grad_sum, gw_k)), gx_k

        init = (_jnp.zeros((), _jnp.float32), _jax.tree.map(_jnp.zeros_like, weights))
        (loss, grad_w), grad_x = _jax.lax.scan(body, init, (per_example, given["loss_target"]))
    with _jax.named_scope("update"):
        delta_w, new_m, new_v = {}, {}, {}
        for n in TWIN_WEIGHTS:
            delta_w[n], new_m[n], new_v[n] = _adamw(weights[n], grad_w[n], given["m_" + n], given["v_" + n])
    return (loss, grad_x, *[grad_w[n] for n in TWIN_WEIGHTS], *[delta_w[n] for n in TWIN_WEIGHTS],
            *[new_m[n] for n in TWIN_WEIGHTS], *[new_v[n] for n in TWIN_WEIGHTS])
```

```python
import functools

import numpy as np
import jax
import jax.numpy as jnp
from jax import lax
from jax.experimental import pallas as pl
from jax.experimental.pallas import tpu as pltpu

F32 = jnp.float32
BF16 = jnp.bfloat16
MESH = pl.DeviceIdType.MESH

D_MODEL = 2048
DEPTH = 4
CHUNK = 64
LEFT_CHUNKS = 8
MAX_REL = 128
N_REL = 2 * MAX_REL + 1
A_HEADS = 8
HEAD_DIM = 128
B_HEADS = 4
B_DK = 64
M_HEADS = 4
GATE_RANK = 16
GATE_TAU = 16.0
A_WIDTH = A_HEADS * HEAD_DIM
B_WIDTH = B_HEADS * HEAD_DIM
B_KEY_WIDTH = B_HEADS * B_DK
M_WIDTH = M_HEADS * HEAD_DIM
IN_WIDTH = 4 * A_WIDTH + 2 * B_KEY_WIDTH + 2 * B_WIDTH + GATE_RANK + 2 * M_WIDTH
ALPHA = (2.0 * DEPTH) ** 0.25
LN_EPS = 1e-5
RMS_EPS = 1e-6
NEG_INF = -1e30
ADAM_LR = 0.001
ADAM_B1 = 0.9
ADAM_B2 = 0.999
ADAM_EPS = 1e-08
ADAM_WD = 0.01
ADAM_STEP = 10

LANES = 128
VMEM_LIMIT = 56 * 1024 * 1024

C_AQ, C_AK, C_AV, C_AZ = 0, 1024, 2048, 3072
C_BQ, C_BK, C_BV, C_BZ = 4096, 4608, 5120, 5632
C_MQ, C_MZ, C_LR = 6144, 6656, 7168
HP = 7680
TQ = 512
CPB = TQ // CHUNK
N_CHIPS = 4
N_DEV = 8


def _params(sem, vmem=VMEM_LIMIT):
    return pltpu.CompilerParams(dimension_semantics=sem, vmem_limit_bytes=vmem)


def _dot(a, b):
    return jnp.dot(a, b, preferred_element_type=F32)


def _dot_nt(a, b):
    return lax.dot_general(a, b, (((1,), (1,)), ((), ())), preferred_element_type=F32)


def _dot_tn(a, b):
    return lax.dot_general(a, b, (((0,), (0,)), ((), ())), preferred_element_type=F32)


def _sigmoid(x):
    return 1.0 / (1.0 + jnp.exp(-x))


def _split3(x):
    hi = x.astype(BF16)
    r = x - hi.astype(F32)
    mid = r.astype(BF16)
    lo = (r - mid.astype(F32)).astype(BF16)
    return hi, mid, lo


def _dot3(m_bf, x):
    hi, mid, lo = _split3(x)
    return _dot(m_bf, hi) + _dot(m_bf, mid) + _dot(m_bf, lo)


def _matmul(a, b, *, mode, out_dtype, tm, tn, tk, name, add=None, add_scale=1.0):
    if mode == "nn":
        (M, K), (K2, N) = a.shape, b.shape
        a_spec = pl.BlockSpec((tm, tk), lambda i, j, k: (i, k))
        b_spec = pl.BlockSpec((tk, tn), lambda i, j, k: (k, j))
        dot = _dot
    elif mode == "nt":
        (M, K), (N, K2) = a.shape, b.shape
        a_spec = pl.BlockSpec((tm, tk), lambda i, j, k: (i, k))
        b_spec = pl.BlockSpec((tn, tk), lambda i, j, k: (j, k))
        dot = _dot_nt
    else:
        (K, M), (K2, N) = a.shape, b.shape
        a_spec = pl.BlockSpec((tk, tm), lambda i, j, k: (k, i))
        b_spec = pl.BlockSpec((tk, tn), lambda i, j, k: (k, j))
        dot = _dot_tn
    assert K == K2 and M % tm == 0 and N % tn == 0 and K % tk == 0, (a.shape, b.shape, mode)
    nk = K // tk
    has_add = add is not None

    def body(*refs):
        if has_add:
            a_ref, b_ref, add_ref, o_ref, acc = refs
        else:
            a_ref, b_ref, o_ref, acc = refs
        k = pl.program_id(2)
        part = dot(a_ref[...].astype(BF16), b_ref[...].astype(BF16))

        @pl.when(k == 0)
        def _():
            acc[...] = part

        @pl.when(k > 0)
        def _():
            acc[...] += part

        @pl.when(k == nk - 1)
        def _():
            r = acc[...]
            if has_add:
                r = r + add_scale * add_ref[...]
            o_ref[...] = r.astype(out_dtype)

    in_specs = [a_spec, b_spec]
    args = [a, b]
    if has_add:
        in_specs.append(pl.BlockSpec((tm, tn), lambda i, j, k: (i, j)))
        args.append(add)
    return pl.pallas_call(
        body,
        name=name,
        grid=(M // tm, N // tn, nk),
        in_specs=in_specs,
        out_specs=pl.BlockSpec((tm, tn), lambda i, j, k: (i, j)),
        out_shape=jax.ShapeDtypeStruct((M, N), out_dtype),
        scratch_shapes=[pltpu.VMEM((tm, tn), F32)],
        compiler_params=_params(("parallel", "parallel", "arbitrary")),
    )(*args)


def _chunk_of(rows):
    return lax.shift_right_logical(rows, CHUNK.bit_length() - 1)


def _band_masks(i):
    qc = _chunk_of(lax.broadcasted_iota(jnp.int32, (TQ, TQ), 0))
    kc = _chunk_of(lax.broadcasted_iota(jnp.int32, (TQ, TQ), 1))
    prev_ok = jnp.logical_and(kc >= qc, i > 0)
    cur_ok = kc <= qc
    return prev_ok, cur_ok


def _band_probs(q, kp, kc, bias, i):
    scale = HEAD_DIM ** -0.5
    prev_ok, cur_ok = _band_masks(i)
    sp = jnp.where(prev_ok, _dot_nt(q, kp) * scale + bias[:, :TQ], NEG_INF)
    sc = jnp.where(cur_ok, _dot_nt(q, kc) * scale + bias[:, TQ:], NEG_INF)
    m = jnp.maximum(jnp.max(sp, axis=1, keepdims=True), jnp.max(sc, axis=1, keepdims=True))
    pp = jnp.exp(sp - m)
    pc = jnp.exp(sc - m)
    inv = 1.0 / (jnp.sum(pp, axis=1, keepdims=True) + jnp.sum(pc, axis=1, keepdims=True))
    return pp * inv, pc * inv


def _toeplitz(u_row):
    return pltpu.roll(jnp.broadcast_to(u_row, (TQ, 2 * TQ)), 0, 1, stride=1, stride_axis=0)


def _band_specs(nq):
    def col(c0):
        return c0 // LANES

    q_spec = pl.BlockSpec((TQ, LANES), lambda h, i: (jnp.minimum(i, nq - 1), col(C_AQ) + h))
    kp_spec = pl.BlockSpec((TQ, LANES), lambda h, i: (jnp.clip(i - 1, 0, nq - 1), col(C_AK) + h))
    kc_spec = pl.BlockSpec((TQ, LANES), lambda h, i: (jnp.minimum(i, nq - 1), col(C_AK) + h))
    vp_spec = pl.BlockSpec((TQ, LANES), lambda h, i: (jnp.clip(i - 1, 0, nq - 1), col(C_AV) + h))
    vc_spec = pl.BlockSpec((TQ, LANES), lambda h, i: (jnp.minimum(i, nq - 1), col(C_AV) + h))
    z_spec = pl.BlockSpec((TQ, LANES), lambda h, i: (jnp.minimum(i, nq - 1), col(C_AZ) + h))
    u_spec = pl.BlockSpec((1, 1, 2 * TQ), lambda h, i: (h, 0, 0))
    return q_spec, kp_spec, kc_spec, vp_spec, vc_spec, z_spec, u_spec


def _band_fwd(h, u):
    S = h.shape[0]
    nq = S // TQ

    def body(q_ref, kp_ref, kc_ref, vp_ref, vc_ref, z_ref, u_ref, y_ref, bias_scr):
        i = pl.program_id(1)

        @pl.when(i == 0)
        def _():
            bias_scr[...] = _toeplitz(u_ref[0])

        pp, pc = _band_probs(q_ref[...], kp_ref[...], kc_ref[...], bias_scr[...], i)
        o = _dot(pp.astype(BF16), vp_ref[...]) + _dot(pc.astype(BF16), vc_ref[...])
        z = z_ref[...].astype(F32)
        y_ref[...] = (o * (z * _sigmoid(z))).astype(BF16)

    specs = _band_specs(nq)
    return pl.pallas_call(
        body,
        name="band_fwd",
        grid=(A_HEADS, nq),
        in_specs=[specs[0], specs[1], specs[2], specs[3], specs[4], specs[5], specs[6]],
        out_specs=pl.BlockSpec((TQ, LANES), lambda h, i: (i, h)),
        out_shape=jax.ShapeDtypeStruct((S, A_WIDTH), BF16),
        scratch_shapes=[pltpu.VMEM((TQ, 2 * TQ), F32)],
        compiler_params=_params(("parallel", "arbitrary")),
    )(h, h, h, h, h, h, u)


def _band_bwd(h, u, dycat):
    S = h.shape[0]
    nq = S // TQ
    scale = HEAD_DIM ** -0.5

    def body(q_ref, kp_ref, kc_ref, vp_ref, vc_ref, z_ref, u_ref, dy_ref,
             dq_ref, dk_ref, dv_ref, dz_ref, du_ref, bias_scr, db_scr, ck_scr, cv_scr):
        i = pl.program_id(1)

        @pl.when(i == 0)
        def _():
            bias_scr[...] = _toeplitz(u_ref[0])
            db_scr[...] = jnp.zeros_like(db_scr)
            ck_scr[...] = jnp.zeros_like(ck_scr)
            cv_scr[...] = jnp.zeros_like(cv_scr)

        @pl.when(i < nq)
        def _():
            q, kp, kc, vp, vc = q_ref[...], kp_ref[...], kc_ref[...], vp_ref[...], vc_ref[...]
            pp, pc = _band_probs(q, kp, kc, bias_scr[...], i)
            ppb, pcb = pp.astype(BF16), pc.astype(BF16)
            o = _dot(ppb, vp) + _dot(pcb, vc)
            z = z_ref[...].astype(F32)
            sg = _sigmoid(z)
            dy = dy_ref[...].astype(F32)
            do = dy * (z * sg)
            dz_ref[...] = (dy * o * (sg * (1.0 + z * (1.0 - sg)))).astype(BF16)
            dob = do.astype(BF16)
            delta = jnp.sum(do * o, axis=1, keepdims=True)
            dsp = pp * (_dot_nt(dob, vp) - delta)
            dsc = pc * (_dot_nt(dob, vc) - delta)
            db_scr[:, :TQ] += dsp
            db_scr[:, TQ:] += dsc
            dspb, dscb = dsp.astype(BF16), dsc.astype(BF16)
            dq_ref[...] = (scale * (_dot(dspb, kp) + _dot(dscb, kc))).astype(BF16)
            dk_ref[...] = (ck_scr[...] + scale * _dot_tn(dspb, q)).astype(BF16)
            dv_ref[...] = (cv_scr[...] + _dot_tn(ppb, dob)).astype(BF16)
            ck_scr[...] = scale * _dot_tn(dscb, q)
            cv_scr[...] = _dot_tn(pcb, dob)

        @pl.when(i == nq)
        def _():
            dk_ref[...] = ck_scr[...].astype(BF16)
            dv_ref[...] = cv_scr[...].astype(BF16)
            r0 = lax.broadcasted_iota(jnp.int32, (TQ, TQ), 0)
            r1 = lax.broadcasted_iota(jnp.int32, (TQ, TQ), 1)
            flip = (r0 + r1 == TQ - 1).astype(BF16)
            fl = _dot3(flip, db_scr[...])
            rolled = pltpu.roll(fl, 0, 1, stride=1, stride_axis=0)
            du_ref[0] = jnp.sum(rolled, axis=0, keepdims=True)

    specs = _band_specs(nq)
    dy_spec = pl.BlockSpec((TQ, LANES), lambda h, i: (jnp.minimum(i, nq - 1), h))
    row_spec = pl.BlockSpec((TQ, LANES), lambda h, i: (jnp.minimum(i, nq - 1), h))
    key_spec = pl.BlockSpec((TQ, LANES), lambda h, i: (jnp.maximum(i - 1, 0), h))
    out_sd = jax.ShapeDtypeStruct((S, A_WIDTH), BF16)
    return pl.pallas_call(
        body,
        name="band_bwd",
        grid=(A_HEADS, nq + 1),
        in_specs=[specs[0], specs[1], specs[2], specs[3], specs[4], specs[5], specs[6], dy_spec],
        out_specs=[row_spec, key_spec, key_spec, row_spec, pl.BlockSpec((1, 1, 2 * TQ), lambda h, i: (h, 0, 0))],
        out_shape=[out_sd, out_sd, out_sd, out_sd, jax.ShapeDtypeStruct((A_HEADS, 1, 2 * TQ), F32)],
        scratch_shapes=[pltpu.VMEM((TQ, 2 * TQ), F32), pltpu.VMEM((TQ, 2 * TQ), F32),
                        pltpu.VMEM((TQ, LANES), F32), pltpu.VMEM((TQ, LANES), F32)],
        compiler_params=_params(("parallel", "arbitrary")),
    )(h, h, h, h, h, h, u, dycat)


def _bias_by_offset(table):
    far = jnp.broadcast_to(table[:, N_REL - 1:], (A_HEADS, TQ - MAX_REL))
    ramp = jnp.flip(table, axis=1)
    rest = jnp.broadcast_to(table[:, :1], (A_HEADS, 2 * TQ - CHUNK - (TQ + MAX_REL + 1)))
    wrap = jnp.broadcast_to(table[:, N_REL - 1:], (A_HEADS, CHUNK))
    return jnp.concatenate([far, ramp, rest, wrap], axis=1)[:, None, :]


def _bias_grad_from_offset(du):
    g = jnp.roll(du[:, 0, :], -(TQ - 1), axis=1)
    far = jnp.sum(g[:, :TQ - MAX_REL], axis=1) + jnp.sum(g[:, 2 * TQ - CHUNK:], axis=1)
    ramp = jnp.flip(g[:, TQ - MAX_REL:TQ + MAX_REL + 1], axis=1)
    return ramp.at[:, N_REL - 1].add(far)


def _mem_probs(q, mk):
    s = _dot_nt(q, mk) * (HEAD_DIM ** -0.5)
    p = jnp.exp(s - jnp.max(s, axis=1, keepdims=True))
    return p * (1.0 / jnp.sum(p, axis=1, keepdims=True))


def _mem_fwd(h, mkv):
    S = h.shape[0]
    nm = mkv.shape[0]

    def body(q_ref, z_ref, mk_ref, mv_ref, y_ref):
        p = _mem_probs(q_ref[...], mk_ref[...])
        o = _dot(p.astype(BF16), mv_ref[...])
        z = z_ref[...].astype(F32)
        y_ref[...] = (o * (z * _sigmoid(z))).astype(BF16)

    return pl.pallas_call(
        body,
        name="mem_fwd",
        grid=(M_HEADS, S // TQ),
        in_specs=[pl.BlockSpec((TQ, LANES), lambda h, i: (i, C_MQ // LANES + h)),
                  pl.BlockSpec((TQ, LANES), lambda h, i: (i, C_MZ // LANES + h)),
                  pl.BlockSpec((nm, LANES), lambda h, i: (0, h)),
                  pl.BlockSpec((nm, LANES), lambda h, i: (0, M_HEADS + h))],
        out_specs=pl.BlockSpec((TQ, LANES), lambda h, i: (i, h)),
        out_shape=jax.ShapeDtypeStruct((S, M_WIDTH), BF16),
        compiler_params=_params(("parallel", "arbitrary")),
    )(h, h, mkv, mkv)


def _mem_bwd(h, mkv, dycat):
    S = h.shape[0]
    nm = mkv.shape[0]
    scale = HEAD_DIM ** -0.5

    def body(q_ref, z_ref, mk_ref, mv_ref, dy_ref, dq_ref, dz_ref, dmk_ref, dmv_ref):
        i = pl.program_id(1)
        q, mk, mv = q_ref[...], mk_ref[...], mv_ref[...]
        p = _mem_probs(q, mk)
        pb = p.astype(BF16)
        o = _dot(pb, mv)
        z = z_ref[...].astype(F32)
        sg = _sigmoid(z)
        dy = dy_ref[...].astype(F32)
        do = dy * (z * sg)
        dz_ref[...] = (dy * o * (sg * (1.0 + z * (1.0 - sg)))).astype(BF16)
        dob = do.astype(BF16)
        ds = p * (_dot_nt(dob, mv) - jnp.sum(do * o, axis=1, keepdims=True))
        dsb = ds.astype(BF16)
        dq_ref[...] = (scale * _dot(dsb, mk)).astype(BF16)
        dmk = scale * _dot_tn(dsb, q)
        dmv = _dot_tn(pb, dob)

        @pl.when(i == 0)
        def _():
            dmk_ref[...] = dmk
            dmv_ref[...] = dmv

        @pl.when(i > 0)
        def _():
            dmk_ref[...] += dmk
            dmv_ref[...] += dmv

    row = pl.BlockSpec((TQ, LANES), lambda h, i: (i, h))
    out_sd = jax.ShapeDtypeStruct((S, M_WIDTH), BF16)
    dq, dz, dmkv, dmkv2 = pl.pallas_call(
        body,
        name="mem_bwd",
        grid=(M_HEADS, S // TQ),
        in_specs=[pl.BlockSpec((TQ, LANES), lambda h, i: (i, C_MQ // LANES + h)),
                  pl.BlockSpec((TQ, LANES), lambda h, i: (i, C_MZ // LANES + h)),
                  pl.BlockSpec((nm, LANES), lambda h, i: (0, h)),
                  pl.BlockSpec((nm, LANES), lambda h, i: (0, M_HEADS + h)),
                  pl.BlockSpec((TQ, LANES), lambda h, i: (i, (A_WIDTH + B_WIDTH) // LANES + h))],
        out_specs=[row, row,
                   pl.BlockSpec((nm, LANES), lambda h, i: (0, h)),
                   pl.BlockSpec((nm, LANES), lambda h, i: (0, h))],
        out_shape=[out_sd, out_sd, jax.ShapeDtypeStruct((nm, M_WIDTH), F32), jax.ShapeDtypeStruct((nm, M_WIDTH), F32)],
        compiler_params=_params(("parallel", "arbitrary")),
    )(h, h, mkv, mkv, dycat)
    return dq, dz, jnp.concatenate([dmkv, dmkv2], axis=1)


def _chunk_masks():
    r = lax.broadcasted_iota(jnp.int32, (TQ, TQ), 0)
    c = lax.broadcasted_iota(jnp.int32, (TQ, TQ), 1)
    same = _chunk_of(r) == _chunk_of(c)
    return jnp.logical_and(same, c <= r), jnp.logical_and(same, c > r)


def _gla_gates(lr, gw, gb):
    logit = _dot(lr, gw) + gb
    sg = _sigmoid(logit)
    g = (jnp.minimum(logit, 0.0) - jnp.log(1.0 + jnp.exp(-jnp.abs(logit)))) * (1.0 / GATE_TAU)
    lo, _ = _chunk_masks()
    return sg, _dot3(lo.astype(BF16), g)


def _gla_factors(q, k, b):
    eb = jnp.exp(b)
    enb = jnp.exp(-b)
    return eb, enb, q * eb, q * enb, k * eb, k * enb


def _gla_intra(qp, qn, kp, kn):
    lo, up = _chunk_masks()
    return (jnp.where(lo, _dot_nt(qp.astype(BF16), kn.astype(BF16)), 0.0)
            + jnp.where(up, _dot_nt(qn.astype(BF16), kp.astype(BF16)), 0.0))


def _gla_specs(nb, rev):
    blk = (lambda i: nb - 1 - i) if rev else (lambda i: i)

    def at(c0):
        return pl.BlockSpec((TQ, LANES), lambda i, h: (blk(i), c0 // LANES + h))

    lr_spec = pl.BlockSpec((TQ, LANES), lambda i, h: (blk(i), C_LR // LANES))
    gw_spec = pl.BlockSpec((LANES, LANES), lambda i, h: (0, h))
    gb_spec = pl.BlockSpec((1, LANES), lambda i, h: (0, h))
    gn_spec = pl.BlockSpec((1, LANES), lambda i, h: (0, 0))
    return at(C_BQ), at(C_BK), at(C_BV), at(C_BZ), lr_spec, gw_spec, gb_spec, gn_spec, blk


def _gla_fwd(h, gw, gb, gn):
    S = h.shape[0]
    nb = S // TQ

    def body(q_ref, k_ref, v_ref, z_ref, lr_ref, gw_ref, gb_ref, gn_ref, y_ref, o_ref, st_ref, st_scr):
        i, hd = pl.program_id(0), pl.program_id(1)

        @pl.when(i == 0)
        def _():
            st_scr[hd] = jnp.zeros((LANES, LANES), F32)

        q = q_ref[...].astype(F32) * (B_DK ** -0.5)
        k = k_ref[...].astype(F32)
        v = v_ref[...]
        _, b = _gla_gates(lr_ref[...], gw_ref[...], gb_ref[...])
        _, _, qp, qn, kp, kn = _gla_factors(q, k, b)
        o_intra = _dot(_gla_intra(qp, qn, kp, kn).astype(BF16), v)
        qpb, knb = qp.astype(BF16), kn.astype(BF16)
        st = st_scr[hd]
        outs = []
        for c in range(CPB):
            rows = slice(c * CHUNK, (c + 1) * CHUNK)
            st_ref[0, c] = st
            outs.append(_dot_nt(qpb[rows], st.astype(BF16)))
            e_last = jnp.exp(b[(c + 1) * CHUNK - 1:(c + 1) * CHUNK, :])
            st = (st + _dot_tn(v[rows], knb[rows])) * e_last
        st_scr[hd] = st
        o = o_intra + jnp.concatenate(outs, axis=0)
        o_ref[...] = o
        r = lax.rsqrt(jnp.mean(o * o, axis=1, keepdims=True) + RMS_EPS)
        z = z_ref[...].astype(F32)
        y_ref[...] = (o * r * gn_ref[...] * (z * _sigmoid(z))).astype(BF16)

    q_s, k_s, v_s, z_s, lr_s, gw_s, gb_s, gn_s, _ = _gla_specs(nb, False)
    row = pl.BlockSpec((TQ, LANES), lambda i, h: (i, h))
    return pl.pallas_call(
        body,
        name="gla_fwd",
        grid=(nb, B_HEADS),
        in_specs=[q_s, k_s, v_s, z_s, lr_s, gw_s, gb_s, gn_s],
        out_specs=[row, row, pl.BlockSpec((1, CPB, LANES, LANES), lambda i, h: (h, i, 0, 0))],
        out_shape=[jax.ShapeDtypeStruct((S, B_WIDTH), BF16), jax.ShapeDtypeStruct((S, B_WIDTH), F32),
                   jax.ShapeDtypeStruct((B_HEADS, S // CHUNK, LANES, LANES), F32)],
        scratch_shapes=[pltpu.VMEM((B_HEADS, LANES, LANES), F32)],
        compiler_params=_params(("arbitrary", "arbitrary")),
    )(h, h, h, h, h, gw, gb, gn)


def _gla_bwd(h, gw, gb, gn, o_pre, states, dycat):
    S = h.shape[0]
    nb = S // TQ

    def body(q_ref, k_ref, v_ref, z_ref, lr_ref, gw_ref, gb_ref, gn_ref, o_ref, st_ref, dy_ref,
             dq_ref, dk_ref, dv_ref, dz_ref, dlr_ref, dgw_ref, dgb_ref, dgn_ref,
             dst_scr, dgw_scr, dgb_scr, dgn_scr):
        i, hd = pl.program_id(0), pl.program_id(1)

        @pl.when(i == 0)
        def _():
            dst_scr[hd] = jnp.zeros((LANES, LANES), F32)
            dgw_scr[hd] = jnp.zeros((LANES, LANES), F32)
            dgb_scr[hd] = jnp.zeros((1, LANES), F32)

        @pl.when(jnp.logical_and(i == 0, hd == 0))
        def _():
            dgn_scr[...] = jnp.zeros_like(dgn_scr)

        q = q_ref[...].astype(F32) * (B_DK ** -0.5)
        k = k_ref[...].astype(F32)
        v = v_ref[...]
        lr, gwv = lr_ref[...], gw_ref[...]
        sg, b = _gla_gates(lr, gwv, gb_ref[...])
        eb, enb, qp, qn, kp, kn = _gla_factors(q, k, b)
        a = _gla_intra(qp, qn, kp, kn)
        qpb, qnb, kpb, knb = qp.astype(BF16), qn.astype(BF16), kp.astype(BF16), kn.astype(BF16)

        o = o_ref[...]
        gn = gn_ref[...]
        r = lax.rsqrt(jnp.mean(o * o, axis=1, keepdims=True) + RMS_EPS)
        z = z_ref[...].astype(F32)
        sz = _sigmoid(z)
        dy = dy_ref[...].astype(F32)
        d_on = dy * (z * sz)
        dz_ref[...] = (dy * (o * r * gn) * (sz * (1.0 + z * (1.0 - sz)))).astype(BF16)
        dgn_scr[...] += jnp.sum(d_on * o * r, axis=0, keepdims=True)
        t = d_on * gn
        do = r * t - o * (r * r * r) * jnp.mean(t * o, axis=1, keepdims=True)
        dob = do.astype(BF16)

        lo, up = _chunk_masks()
        da = _dot_nt(dob, v)
        dalo = jnp.where(lo, da, 0.0).astype(BF16)
        daup = jnp.where(up, da, 0.0).astype(BF16)
        dqp = _dot(dalo, knb)
        dkn = _dot_tn(dalo, qpb)
        dqn = _dot(daup, kpb)
        dkp = _dot_tn(daup, qnb)
        dv = _dot_tn(a.astype(BF16), dob)

        dst = dst_scr[hd]
        dqp_c, dkn_c, dv_c, dbl_c = [None] * CPB, [None] * CPB, [None] * CPB, [None] * CPB
        for c in reversed(range(CPB)):
            rows = slice(c * CHUNK, (c + 1) * CHUNK)
            st = st_ref[0, c]
            e_last = jnp.exp(b[(c + 1) * CHUNK - 1:(c + 1) * CHUNK, :])
            if c == CPB - 1:
                st_next = (st + _dot_tn(v[rows], knb[rows])) * e_last
            else:
                st_next = st_ref[0, c + 1]
            dbl_c[c] = jnp.sum(dst * st_next, axis=0, keepdims=True)
            dtt = (dst * e_last).astype(BF16)
            dv_c[c] = _dot_nt(knb[rows], dtt)
            dkn_c[c] = _dot(v[rows], dtt)
            dqp_c[c] = _dot(dob[rows], st.astype(BF16))
            dst = _dot_tn(dob[rows], qpb[rows]) + dst * e_last
        dst_scr[hd] = dst
        dqp = dqp + jnp.concatenate(dqp_c, axis=0)
        dkn = dkn + jnp.concatenate(dkn_c, axis=0)
        dv = dv + jnp.concatenate(dv_c, axis=0)
        dv_ref[...] = dv.astype(BF16)
        dq_ref[...] = ((dqp * eb + dqn * enb) * (B_DK ** -0.5)).astype(BF16)
        dk_ref[...] = (dkp * eb + dkn * enb).astype(BF16)

        last = jnp.bitwise_and(lax.broadcasted_iota(jnp.int32, (TQ, 1), 0), CHUNK - 1) == CHUNK - 1
        dbl = jnp.concatenate([jnp.broadcast_to(x, (CHUNK, LANES)) for x in dbl_c], axis=0)
        db = dqp * qp - dqn * qn + dkp * kp - dkn * kn + jnp.where(last, dbl, 0.0)
        r0 = lax.broadcasted_iota(jnp.int32, (TQ, TQ), 0)
        r1 = lax.broadcasted_iota(jnp.int32, (TQ, TQ), 1)
        upper = jnp.logical_and(_chunk_of(r0) == _chunk_of(r1), r1 >= r0).astype(BF16)
        dlogit = _dot3(upper, db) * (1.0 / GATE_TAU) * (1.0 - sg)
        dlb = dlogit.astype(BF16)
        dlr = _dot_nt(dlb, gwv)

        @pl.when(hd == 0)
        def _():
            dlr_ref[...] = dlr

        @pl.when(hd > 0)
        def _():
            dlr_ref[...] += dlr

        dgw_scr[hd] += _dot_tn(lr, dlb)
        dgb_scr[hd] += jnp.sum(dlogit, axis=0, keepdims=True)

        @pl.when(i == nb - 1)
        def _():
            dgw_ref[...] = dgw_scr[hd]
            dgb_ref[...] = dgb_scr[hd]
            dgn_ref[...] = dgn_scr[...]

    q_s, k_s, v_s, z_s, lr_s, gw_s, gb_s, gn_s, blk = _gla_specs(nb, True)
    row = pl.BlockSpec((TQ, LANES), lambda i, h: (blk(i), h))
    dy_spec = pl.BlockSpec((TQ, LANES), lambda i, h: (blk(i), A_WIDTH // LANES + h))
    st_spec = pl.BlockSpec((1, CPB, LANES, LANES), lambda i, h: (h, blk(i), 0, 0))
    out_sd = jax.ShapeDtypeStruct((S, B_WIDTH), BF16)
    return pl.pallas_call(
        body,
        name="gla_bwd",
        grid=(nb, B_HEADS),
        in_specs=[q_s, k_s, v_s, z_s, lr_s, gw_s, gb_s, gn_s, row, st_spec, dy_spec],
        out_specs=[row, row, row, row,
                   pl.BlockSpec((TQ, LANES), lambda i, h: (blk(i), 0)),
                   pl.BlockSpec((LANES, LANES), lambda i, h: (0, jnp.where(i == nb - 1, h, 0))),
                   pl.BlockSpec((1, LANES), lambda i, h: (0, jnp.where(i == nb - 1, h, 0))),
                   pl.BlockSpec((1, LANES), lambda i, h: (0, 0))],
        out_shape=[out_sd, out_sd, out_sd, out_sd,
                   jax.ShapeDtypeStruct((S, LANES), F32),
                   jax.ShapeDtypeStruct((LANES, B_HEADS * LANES), F32),
                   jax.ShapeDtypeStruct((1, B_HEADS * LANES), F32),
                   jax.ShapeDtypeStruct((1, LANES), F32)],
        scratch_shapes=[pltpu.VMEM((B_HEADS, LANES, LANES), F32), pltpu.VMEM((B_HEADS, LANES, LANES), F32),
                        pltpu.VMEM((B_HEADS, 1, LANES), F32), pltpu.VMEM((1, LANES), F32)],
        compiler_params=_params(("arbitrary", "arbitrary")),
    )(h, h, h, h, h, gw, gb, gn, o_pre, states, dycat)


LN_ROWS = 256


def _outproj_ln(ycat, w_out, x, g, b):
    S = x.shape[0]

    def body(yc_ref, w_ref, x_ref, g_ref, b_ref, y_ref, yb_ref, xh_ref, rs_ref):
        u = ALPHA * x_ref[...] + _dot(yc_ref[...], w_ref[...])
        mu = jnp.mean(u, axis=1, keepdims=True)
        d = u - mu
        rstd = lax.rsqrt(jnp.mean(d * d, axis=1, keepdims=True) + LN_EPS)
        xh = d * rstd
        y = xh * g_ref[...] + b_ref[...]
        y_ref[...] = y
        yb_ref[...] = y.astype(BF16)
        xh_ref[...] = xh
        rs_ref[...] = rstd

    row = lambda w: pl.BlockSpec((LN_ROWS, w), lambda i: (i, 0))
    vec = pl.BlockSpec((1, D_MODEL), lambda i: (0, 0))
    return pl.pallas_call(
        body,
        name="outproj_ln",
        grid=(S // LN_ROWS,),
        in_specs=[row(D_MODEL), pl.BlockSpec((D_MODEL, D_MODEL), lambda i: (0, 0)), row(D_MODEL), vec, vec],
        out_specs=[row(D_MODEL), row(D_MODEL), row(D_MODEL), row(1)],
        out_shape=[jax.ShapeDtypeStruct((S, D_MODEL), F32), jax.ShapeDtypeStruct((S, D_MODEL), BF16),
                   jax.ShapeDtypeStruct((S, D_MODEL), F32), jax.ShapeDtypeStruct((S, 1), F32)],
        compiler_params=_params(("parallel",)),
    )(ycat, w_out, x, g, b)


def _ln_bwd(dy, xhat, rstd, g):
    S = dy.shape[0]

    def body(dy_ref, xh_ref, rs_ref, g_ref, du_ref, dub_ref, dg_ref, db_ref):
        i = pl.program_id(0)
        dy_, xh = dy_ref[...], xh_ref[...]
        dyg = dy_ * g_ref[...]
        m1 = jnp.mean(dyg, axis=1, keepdims=True)
        m2 = jnp.mean(dyg * xh, axis=1, keepdims=True)
        du = rs_ref[...] * (dyg - m1 - xh * m2)
        du_ref[...] = du
        dub_ref[...] = du.astype(BF16)
        dg = jnp.sum(dy_ * xh, axis=0, keepdims=True)
        db = jnp.sum(dy_, axis=0, keepdims=True)

        @pl.when(i == 0)
        def _():
            dg_ref[...] = dg
            db_ref[...] = db

        @pl.when(i > 0)
        def _():
            dg_ref[...] += dg
            db_ref[...] += db

    row = lambda w: pl.BlockSpec((TQ, w), lambda i: (i, 0))
    vec = pl.BlockSpec((1, D_MODEL), lambda i: (0, 0))
    return pl.pallas_call(
        body,
        name="ln_bwd",
        grid=(S // TQ,),
        in_specs=[row(D_MODEL), row(D_MODEL), row(1), vec],
        out_specs=[row(D_MODEL), row(D_MODEL), vec, vec],
        out_shape=[jax.ShapeDtypeStruct((S, D_MODEL), F32), jax.ShapeDtypeStruct((S, D_MODEL), BF16),
                   jax.ShapeDtypeStruct((1, D_MODEL), F32), jax.ShapeDtypeStruct((1, D_MODEL), F32)],
        compiler_params=_params(("arbitrary",)),
    )(dy, xhat, rstd, g)


def _loss_head(y, target):
    S = y.shape[0]

    def body(y_ref, t_ref, l_ref, dy_ref):
        i = pl.program_id(0)
        err = y_ref[...] - t_ref[...]
        dy_ref[...] = err * (1.0 / D_MODEL)
        part = (0.5 / D_MODEL) * jnp.sum(jnp.sum(err * err, axis=1, keepdims=True), axis=0, keepdims=True)

        @pl.when(i == 0)
        def _():
            l_ref[...] = part

        @pl.when(i > 0)
        def _():
            l_ref[...] += part

    row = pl.BlockSpec((TQ, D_MODEL), lambda i: (i, 0))
    return pl.pallas_call(
        body,
        name="loss_head",
        grid=(S // TQ,),
        in_specs=[row, row],
        out_specs=[pl.BlockSpec((1, 1), lambda i: (0, 0)), row],
        out_shape=[jax.ShapeDtypeStruct((1, 1), F32), jax.ShapeDtypeStruct((S, D_MODEL), F32)],
        compiler_params=_params(("arbitrary",)),
    )(y, target)


def _pad_gate(gate_w, gate_b):
    gw = gate_w.reshape(GATE_RANK, B_HEADS, B_DK)
    gw = jnp.pad(gw, ((0, LANES - GATE_RANK), (0, 0), (0, LANES - B_DK))).reshape(LANES, B_HEADS * LANES)
    gb = jnp.pad(gate_b.reshape(B_HEADS, B_DK), ((0, 0), (0, LANES - B_DK))).reshape(1, B_HEADS * LANES)
    return gw.astype(BF16), gb.astype(F32)


def _layer_fwd(x, xb, mem_b, w_in, w_kv, w_out, u, gw, gb, gn, ln_g, ln_b):
    h = _matmul(xb, w_in, mode="nn", out_dtype=BF16, tm=1024, tn=768, tk=D_MODEL, name="in_proj")
    mkv = _matmul(mem_b, w_kv, mode="nn", out_dtype=BF16, tm=mem_b.shape[0], tn=1024, tk=D_MODEL, name="mem_kv")
    ya = _band_fwd(h, u)
    yb_, o_pre, states = _gla_fwd(h, gw, gb, gn)
    ym = _mem_fwd(h, mkv)
    ycat = jnp.concatenate([ya, yb_, ym], axis=1)
    y, ybf, xhat, rstd = _outproj_ln(ycat, w_out, x, ln_g, ln_b)
    return y, ybf, (xb, h, mkv, ycat, o_pre, states, xhat, rstd)


def _layer_bwd(dy, saved, mem_b, w_in, w_out, u, gw, gb, gn, ln_g):
    xb, h, mkv, ycat, o_pre, states, xhat, rstd = saved
    S = dy.shape[0]
    du, dub, d_ln_g, d_ln_b = _ln_bwd(dy, xhat, rstd, ln_g)
    dycat = _matmul(dub, w_out, mode="nt", out_dtype=BF16, tm=1024, tn=1024, tk=D_MODEL, name="dycat")
    d_w_out = _matmul(ycat, dub, mode="tn", out_dtype=F32, tm=1024, tn=1024, tk=512, name="d_w_out")
    daq, dak, dav, daz, d_u = _band_bwd(h, u, dycat)
    dbq, dbk, dbv, dbz, dlr, dgw, dgb, dgn = _gla_bwd(h, gw, gb, gn, o_pre, states, dycat)
    dmq, dmz, dmkv = _mem_bwd(h, mkv, dycat)
    d_w_kv = _matmul(mem_b, dmkv, mode="tn", out_dtype=F32, tm=1024, tn=1024, tk=mem_b.shape[0], name="d_w_kv")
    dh = jnp.concatenate([daq, dak, dav, daz, dbq, dbk, dbv, dbz, dmq, dmz, dlr.astype(BF16),
                          jnp.zeros((S, HP - C_LR - LANES), BF16)], axis=1)
    dx = _matmul(dh, w_in, mode="nt", out_dtype=F32, tm=1024, tn=1024, tk=768, name="dx", add=du, add_scale=ALPHA)
    d_w_in = _matmul(xb, dh, mode="tn", out_dtype=F32, tm=1024, tn=768, tk=512, name="d_w_in")
    return dx, (d_w_in, d_u, dgw, dgb, dgn, d_w_kv, d_w_out, d_ln_g, d_ln_b)


def _pad_heads(w):
    r = w.shape[0]
    return jnp.pad(w.reshape(r, B_HEADS, B_DK), ((0, 0), (0, 0), (0, LANES - B_DK))).reshape(r, B_HEADS * LANES)


def _unpad_heads(w):
    r = w.shape[0]
    return w.reshape(r, B_HEADS, LANES)[:, :, :B_DK].reshape(r, B_KEY_WIDTH)


O_BQ = 4 * A_WIDTH
O_BK = O_BQ + B_KEY_WIDTH
O_BV = O_BK + B_KEY_WIDTH
O_LR = O_BV + 2 * B_WIDTH
O_MQ = O_LR + GATE_RANK


def _to_padded(w):
    r = w.shape[0]
    return jnp.concatenate([
        w[:, :O_BQ], _pad_heads(w[:, O_BQ:O_BK]), _pad_heads(w[:, O_BK:O_BV]), w[:, O_BV:O_LR], w[:, O_MQ:],
        w[:, O_LR:O_MQ], jnp.zeros((r, HP - C_LR - GATE_RANK), w.dtype)], axis=1)


def _from_padded(g):
    return jnp.concatenate([
        g[:, :C_BQ], _unpad_heads(g[:, C_BQ:C_BK]), _unpad_heads(g[:, C_BK:C_BV]), g[:, C_BV:C_MQ],
        g[:, C_LR:C_LR + GATE_RANK], g[:, C_MQ:C_LR]], axis=1)


def _adamw(w, g, m, v, name):
    R, C = w.shape
    tr = R
    for cand in (256, 128, 64, 32, 16, 8):
        if R % cand == 0 and R > cand:
            tr = cand
            break

    def body(w_ref, g_ref, m_ref, v_ref, d_ref, nm_ref, nv_ref):
        g_ = g_ref[...]
        nm = ADAM_B1 * m_ref[...] + (1.0 - ADAM_B1) * g_
        nv = ADAM_B2 * v_ref[...] + (1.0 - ADAM_B2) * (g_ * g_)
        m_hat = nm / (1.0 - ADAM_B1 ** ADAM_STEP)
        v_hat = nv / (1.0 - ADAM_B2 ** ADAM_STEP)
        d_ref[...] = -ADAM_LR * (m_hat / (jnp.sqrt(v_hat) + ADAM_EPS) + ADAM_WD * w_ref[...])
        nm_ref[...] = nm
        nv_ref[...] = nv

    spec = pl.BlockSpec((tr, C), lambda i: (i, 0))
    sd = jax.ShapeDtypeStruct((R, C), F32)
    return pl.pallas_call(
        body, name=name, grid=(R // tr,), in_specs=[spec] * 4, out_specs=[spec] * 3, out_shape=[sd] * 3,
        compiler_params=_params(("parallel",)),
    )(w, g, m, v)


def _adamw_nd(w, g, m, v, name):
    shape = w.shape
    f = lambda a: a.reshape(-1, shape[-1])
    return tuple(o.reshape(shape) for o in _adamw(f(w), f(g), f(m), f(v), name))


ANY = pl.BlockSpec(memory_space=pl.ANY)


def _place():
    x, y, c = lax.axis_index("x"), lax.axis_index("y"), lax.axis_index("c")
    chips = [(1 - x, y), (x, 1 - y), (1 - x, 1 - y)]
    return x, y, c, chips


def _gather_chips(shard, name):
    R, C = shard.shape
    half = R // 2
    assert half % 16 == 0

    def body(s_ref, o_ref, send, recv, lsem):
        x, y, c, chips = _place()
        mine = pl.ds(pl.multiple_of(c * half, 16), half)
        other = pl.ds(pl.multiple_of((1 - c) * half, 16), half)

        def cp(k, src, chip, rows, to):
            dst = o_ref.at[2 * chip[0] + chip[1], rows]
            return pltpu.make_async_remote_copy(src_ref=dst if src is None else src, dst_ref=dst, send_sem=send.at[k],
                                                recv_sem=recv.at[k], device_id=to, device_id_type=MESH)

        local = pltpu.make_async_copy(s_ref, o_ref.at[2 * x + y], lsem)
        local.start()
        first = [cp(k, s_ref.at[mine], (x, y), mine, (*chip, c)) for k, chip in enumerate(chips)]
        for f in first:
            f.start()
        passed = [cp(3 + k, None, chip, mine, (x, y, 1 - c)) for k, chip in enumerate(chips)]
        for k, chip in enumerate(chips):
            cp(k, None, chip, mine, (x, y, c)).wait_recv()
            passed[k].start()
        for k, chip in enumerate(chips):
            cp(3 + k, None, chip, other, (x, y, c)).wait_recv()
        for f in first + passed:
            f.wait_send()
        local.wait()

    return pl.pallas_call(
        body, name=name, in_specs=[ANY], out_specs=ANY,
        out_shape=jax.ShapeDtypeStruct((N_CHIPS, R, C), shard.dtype),
        scratch_shapes=[pltpu.SemaphoreType.DMA((6,)), pltpu.SemaphoreType.DMA((6,)), pltpu.SemaphoreType.DMA],
    )(shard)


def _pair_exchange(g4, name):
    n, _, R, C = g4.shape

    def body(g_ref, o_ref, send, recv):
        x, y, c, _ = _place()
        cps = [pltpu.make_async_remote_copy(src_ref=g_ref.at[j, 1 - c], dst_ref=o_ref.at[j], send_sem=send.at[j],
                                            recv_sem=recv.at[j], device_id=(x, y, 1 - c), device_id_type=MESH)
               for j in range(n)]
        for cp in cps:
            cp.start()
        for cp in cps:
            cp.wait()

    return pl.pallas_call(
        body, name=name, in_specs=[ANY], out_specs=ANY, out_shape=jax.ShapeDtypeStruct((n, R, C), g4.dtype),
        scratch_shapes=[pltpu.SemaphoreType.DMA((n,)), pltpu.SemaphoreType.DMA((n,))],
    )(g4)


def _chip_exchange(p, name):
    n, R, C = p.shape

    def body(p_ref, o_ref, send, recv, lsem):
        x, y, c, chips = _place()
        me = 2 * x + y
        local = pltpu.make_async_copy(p_ref.at[me], o_ref.at[me], lsem)
        local.start()
        cps = [pltpu.make_async_remote_copy(src_ref=p_ref.at[2 * chip[0] + chip[1]], dst_ref=o_ref.at[me],
                                            send_sem=send.at[k], recv_sem=recv.at[k], device_id=(*chip, c),
                                            device_id_type=MESH) for k, chip in enumerate(chips)]
        for cp in cps:
            cp.start()
        for k, chip in enumerate(chips):
            pltpu.make_async_remote_copy(src_ref=p_ref.at[me], dst_ref=o_ref.at[2 * chip[0] + chip[1]],
                                         send_sem=send.at[k], recv_sem=recv.at[k], device_id=(*chip, c),
                                         device_id_type=MESH).wait_recv()
        for cp in cps:
            cp.wait_send()
        local.wait()

    return pl.pallas_call(
        body, name=name, in_specs=[ANY], out_specs=ANY, out_shape=jax.ShapeDtypeStruct((n, R, C), p.dtype),
        scratch_shapes=[pltpu.SemaphoreType.DMA((3,)), pltpu.SemaphoreType.DMA((3,)), pltpu.SemaphoreType.DMA],
    )(p)


def _pair_gather(t, name):
    R, C = t.shape

    def body(t_ref, o_ref, send, recv, lsem):
        x, y, c, _ = _place()
        local = pltpu.make_async_copy(t_ref, o_ref.at[c], lsem)
        local.start()
        cp = pltpu.make_async_remote_copy(src_ref=t_ref, dst_ref=o_ref.at[c], send_sem=send, recv_sem=recv,
                                          device_id=(x, y, 1 - c), device_id_type=MESH)
        cp.start()
        pltpu.make_async_remote_copy(src_ref=t_ref, dst_ref=o_ref.at[1 - c], send_sem=send, recv_sem=recv,
                                     device_id=(x, y, 1 - c), device_id_type=MESH).wait_recv()
        cp.wait_send()
        local.wait()

    return pl.pallas_call(
        body, name=name, in_specs=[ANY], out_specs=ANY, out_shape=jax.ShapeDtypeStruct((2, R, C), t.dtype),
        scratch_shapes=[pltpu.SemaphoreType.DMA, pltpu.SemaphoreType.DMA, pltpu.SemaphoreType.DMA],
    )(t)


def _add_halves(g4, recv, c_idx, name):
    n, _, R, C = g4.shape
    tr = 256

    def body(c_ref, a_ref, b_ref, o_ref):
        o_ref[...] = a_ref[0] + b_ref[...]

    return pl.pallas_call(
        body, name=name,
        grid_spec=pltpu.PrefetchScalarGridSpec(
            num_scalar_prefetch=1, grid=(n, R // tr),
            in_specs=[pl.BlockSpec((1, 1, tr, C), lambda j, i, c: (j, c[0], i, 0)),
                      pl.BlockSpec((1, tr, C), lambda j, i, c: (j, i, 0))],
            out_specs=pl.BlockSpec((1, tr, C), lambda j, i, c: (j, i, 0))),
        out_shape=jax.ShapeDtypeStruct((n, R, C), F32),
        compiler_params=_params(("parallel", "parallel")),
    )(c_idx, g4, recv)


def _add_slots(r, name):
    n, R, C = r.shape
    tr = 256

    def body(r_ref, o_ref):
        acc = r_ref[0]
        for j in range(1, n):
            acc = acc + r_ref[j]
        o_ref[...] = acc

    return pl.pallas_call(
        body, name=name, grid=(R // tr,),
        in_specs=[pl.BlockSpec((n, tr, C), lambda i: (0, i, 0))],
        out_specs=pl.BlockSpec((tr, C), lambda i: (i, 0)),
        out_shape=jax.ShapeDtypeStruct((R, C), F32),
        compiler_params=_params(("parallel",)),
    )(r)


def _reduce_scatter(g4, c_idx, tag):
    got = _pair_exchange(g4, "rs_pair_" + tag)
    p = _add_halves(g4, got, c_idx, "rs_add2_" + tag)
    q = _chip_exchange(p, "rs_chip_" + tag)
    t = _add_slots(q, "rs_add4_" + tag)
    return _pair_gather(t, "rs_gather_" + tag)


def _all_reduce_small(buf, name):
    R = buf.shape[0]

    def flipped(k, x, y, c):
        return ((1 - x) if k & 4 else x, (1 - y) if k & 2 else y, (1 - c) if k & 1 else c)

    def body(b_ref, o_ref, land, send, recv):
        x, y, c, _ = _place()
        me = 4 * x + 2 * y + c
        land[me] = b_ref[...]
        cps = []
        for k in range(1, N_DEV):
            peer = flipped(k, x, y, c)
            cps.append(pltpu.make_async_remote_copy(src_ref=b_ref, dst_ref=land.at[me], send_sem=send.at[k - 1],
                                                    recv_sem=recv.at[k - 1], device_id=peer, device_id_type=MESH))
        for cp in cps:
            cp.start()
        for k in range(1, N_DEV):
            peer = flipped(k, x, y, c)
            slot = 4 * peer[0] + 2 * peer[1] + peer[2]
            pltpu.make_async_remote_copy(src_ref=b_ref, dst_ref=land.at[slot], send_sem=send.at[k - 1],
                                         recv_sem=recv.at[k - 1], device_id=peer, device_id_type=MESH).wait_recv()
        for cp in cps:
            cp.wait_send()
        acc = land[0]
        for j in range(1, N_DEV):
            acc = acc + land[j]
        o_ref[...] = acc

    vm = pl.BlockSpec(memory_space=pltpu.VMEM)
    return pl.pallas_call(
        body, name=name, in_specs=[vm], out_specs=vm,
        out_shape=jax.ShapeDtypeStruct((R, LANES), F32),
        scratch_shapes=[pltpu.VMEM((N_DEV, R, LANES), F32), pltpu.SemaphoreType.DMA((N_DEV - 1,)),
                        pltpu.SemaphoreType.DMA((N_DEV - 1,))],
    )(buf)


def _by_chip_and_half(g, axis):
    L = g.shape[0]
    if axis == 2:
        n = g.shape[2] // N_CHIPS
        t = g.reshape(L, g.shape[1], N_CHIPS, n).transpose(2, 0, 1, 3)
    else:
        n = g.shape[1] // N_CHIPS
        t = g.reshape(L, N_CHIPS, n, g.shape[2]).transpose(1, 0, 2, 3)
    return t.reshape(N_CHIPS, 2, (L // 2) * t.shape[2], t.shape[3])


def kernel(x, mem, w_in, a_rel_bias, b_gate_w, b_gate_b, b_norm_g, w_mem_kv, w_out, ln_g, ln_b, loss_target, m_w_in, m_a_rel_bias, m_b_gate_w, m_b_gate_b, m_b_norm_g, m_w_mem_kv, m_w_out, m_ln_g, m_ln_b, v_w_in, v_a_rel_bias, v_b_gate_w, v_b_gate_b, v_b_norm_g, v_w_mem_kv, v_w_out, v_ln_g, v_ln_b):
    L = w_in.shape[0]
    S = x.shape[1]
    cx, cy, cc = lax.axis_index("x"), lax.axis_index("y"), lax.axis_index("c")
    chip = 2 * cx + cy
    c_idx = jnp.reshape(cc, (1,)).astype(jnp.int32)

    n_in = w_in.shape[2]
    win_all = _gather_chips(w_in.astype(BF16).reshape(L * D_MODEL, n_in), "gather_w_in")
    r_kv = w_mem_kv.shape[1]
    wkv_all = _gather_chips(w_mem_kv.astype(BF16).reshape(L * r_kv, w_mem_kv.shape[2]), "gather_w_kv")
    r_out = w_out.shape[1]
    wout_all = _gather_chips(w_out.astype(BF16).reshape(L * r_out, D_MODEL), "gather_w_out")
    win_all = win_all.reshape(N_CHIPS, L, D_MODEL, n_in)
    wkv_all = wkv_all.reshape(N_CHIPS, L, r_kv, w_mem_kv.shape[2])
    wout_all = wout_all.reshape(N_CHIPS, L, r_out, D_MODEL)

    gw_cols = b_gate_w.shape[2]
    gw_slot = jnp.zeros((N_CHIPS, L, GATE_RANK, gw_cols), F32)
    gw_slot = lax.dynamic_update_slice(gw_slot, (0.5 * b_gate_w)[None], (chip, 0, 0, 0))
    gw_flat = gw_slot.reshape(-1)
    n_gw = gw_flat.shape[0]
    pad = (-n_gw) % (8 * LANES)
    gw_full = _all_reduce_small(jnp.pad(gw_flat, (0, pad)).reshape(-1, LANES), "gather_gate_w").reshape(-1)[:n_gw]
    gw_full = gw_full.reshape(N_CHIPS, L, GATE_RANK, gw_cols).transpose(1, 2, 0, 3).reshape(L, GATE_RANK, B_KEY_WIDTH)

    xs = x.reshape(S, D_MODEL)
    mem_b = mem.reshape(mem.shape[1], D_MODEL).astype(BF16)
    target = loss_target.reshape(S, D_MODEL)

    layer_w = []
    for l in range(L):
        w_in_l = _to_padded(jnp.concatenate([win_all[j, l] for j in range(N_CHIPS)], axis=1))
        w_kv_l = jnp.concatenate([wkv_all[j, l] for j in range(N_CHIPS)], axis=0)
        w_out_l = jnp.concatenate([wout_all[j, l] for j in range(N_CHIPS)], axis=0)
        gw_l, gb_l = _pad_gate(gw_full[l], b_gate_b[l])
        layer_w.append((w_in_l, w_kv_l, w_out_l, _bias_by_offset(a_rel_bias[l]), gw_l, gb_l,
                        b_norm_g[l].reshape(1, LANES), ln_g[l].reshape(1, D_MODEL), ln_b[l].reshape(1, D_MODEL)))

    y, yb = xs, xs.astype(BF16)
    saved = []
    for l in range(L):
        w_in_l, w_kv_l, w_out_l, u_l, gw_l, gb_l, gn_l, lg_l, lb_l = layer_w[l]
        y, yb, sv = _layer_fwd(y, yb, mem_b, w_in_l, w_kv_l, w_out_l, u_l, gw_l, gb_l, gn_l, lg_l, lb_l)
        saved.append(sv)
    loss_part, dy = _loss_head(y, target)

    grads = [None] * L
    for l in reversed(range(L)):
        w_in_l, w_kv_l, w_out_l, u_l, gw_l, gb_l, gn_l, lg_l, lb_l = layer_w[l]
        dy, grads[l] = _layer_bwd(dy, saved[l], mem_b, w_in_l, w_out_l, u_l, gw_l, gb_l, gn_l, lg_l)
    grad_x = dy.reshape(x.shape)

    g_w_in = jnp.stack([_from_padded(g[0]) for g in grads])
    g_rel = jnp.stack([_bias_grad_from_offset(g[1]) for g in grads])
    g_gw = jnp.stack([_unpad_heads(g[2][:GATE_RANK]) for g in grads])
    g_gb = jnp.stack([_unpad_heads(g[3])[0] for g in grads])
    g_gn = jnp.stack([g[4][0] for g in grads])
    g_w_kv = jnp.stack([g[5] for g in grads])
    g_w_out = jnp.stack([g[6] for g in grads])
    g_lg = jnp.stack([g[7][0] for g in grads])
    g_lb = jnp.stack([g[8][0] for g in grads])

    r_w_in = _reduce_scatter(_by_chip_and_half(g_w_in, 2), c_idx, "w_in").reshape(L, D_MODEL, n_in)
    r_w_kv = _reduce_scatter(_by_chip_and_half(g_w_kv, 1), c_idx, "w_kv").reshape(L, r_kv, w_mem_kv.shape[2])
    r_w_out = _reduce_scatter(_by_chip_and_half(g_w_out, 1), c_idx, "w_out").reshape(L, r_out, D_MODEL)

    small = [g_rel, g_gw, g_gb, g_gn, g_lg, g_lb, loss_part]
    flat = jnp.concatenate([s.reshape(-1) for s in small])
    n_small = flat.shape[0]
    pad = (-n_small) % (8 * LANES)
    red = _all_reduce_small(jnp.pad(flat, (0, pad)).reshape(-1, LANES), "all_reduce_small").reshape(-1)
    outs, off = [], 0
    for s in small:
        outs.append(red[off:off + s.size].reshape(s.shape))
        off += s.size
    g_rel, g_gw, g_gb, g_gn, g_lg, g_lb, loss = outs
    loss = loss.reshape(())
    g_gw = lax.dynamic_slice_in_dim(g_gw.reshape(L, GATE_RANK, N_CHIPS, gw_cols), chip, 1, axis=2).reshape(L, GATE_RANK, gw_cols)

    g_list = [r_w_in, g_rel, g_gw, g_gb, g_gn, r_w_kv, r_w_out, g_lg, g_lb]
    w_list = [w_in, a_rel_bias, b_gate_w, b_gate_b, b_norm_g, w_mem_kv, w_out, ln_g, ln_b]
    m_list = [m_w_in, m_a_rel_bias, m_b_gate_w, m_b_gate_b, m_b_norm_g, m_w_mem_kv, m_w_out, m_ln_g, m_ln_b]
    v_list = [v_w_in, v_a_rel_bias, v_b_gate_w, v_b_gate_b, v_b_norm_g, v_w_mem_kv, v_w_out, v_ln_g, v_ln_b]
    names = ["w_in", "rel", "gate_w", "gate_b", "norm_g", "w_kv", "w_out", "ln_g", "ln_b"]
    upd = [_adamw_nd(w, g, m, v, "adamw_" + n) for w, g, m, v, n in zip(w_list, g_list, m_list, v_list, names)]
    deltas = [u_[0] for u_ in upd]
    new_m = [u_[1] for u_ in upd]
    new_v = [u_[2] for u_ in upd]
    return (loss, grad_x, *g_list, *deltas, *new_m, *new_v)
```

```python
import functools

import numpy as np
import jax
import jax.numpy as jnp
from jax import lax
from jax.experimental import pallas as pl
from jax.experimental.pallas import tpu as pltpu

F32 = jnp.float32
BF16 = jnp.bfloat16
MESH = pl.DeviceIdType.MESH

D_MODEL = 2048
DEPTH = 4
CHUNK = 64
LEFT_CHUNKS = 8
MAX_REL = 128
N_REL = 2 * MAX_REL + 1
A_HEADS = 8
HEAD_DIM = 128
B_HEADS = 4
B_DK = 64
M_HEADS = 4
GATE_RANK = 16
GATE_TAU = 16.0
A_WIDTH = A_HEADS * HEAD_DIM
B_WIDTH = B_HEADS * HEAD_DIM
B_KEY_WIDTH = B_HEADS * B_DK
M_WIDTH = M_HEADS * HEAD_DIM
IN_WIDTH = 4 * A_WIDTH + 2 * B_KEY_WIDTH + 2 * B_WIDTH + GATE_RANK + 2 * M_WIDTH
ALPHA = (2.0 * DEPTH) ** 0.25
LN_EPS = 1e-5
RMS_EPS = 1e-6
NEG_INF = -1e30
ADAM_LR = 0.001
ADAM_B1 = 0.9
ADAM_B2 = 0.999
ADAM_EPS = 1e-08
ADAM_WD = 0.01
ADAM_STEP = 10

LANES = 128
VMEM_LIMIT = 56 * 1024 * 1024

C_AQ, C_AK, C_AV, C_AZ = 0, 1024, 2048, 3072
C_BQ, C_BK, C_BV, C_BZ = 4096, 4608, 5120, 5632
C_MQ, C_MZ, C_LR = 6144, 6656, 7168
HP = 7680
TQ = 512
CPB = TQ // CHUNK
N_CHIPS = 4
N_DEV = 8


def _params(sem, vmem=VMEM_LIMIT):
    return pltpu.CompilerParams(dimension_semantics=sem, vmem_limit_bytes=vmem)


def _dot(a, b):
    return jnp.dot(a, b, preferred_element_type=F32)


def _dot_nt(a, b):
    return lax.dot_general(a, b, (((1,), (1,)), ((), ())), preferred_element_type=F32)


def _dot_tn(a, b):
    return lax.dot_general(a, b, (((0,), (0,)), ((), ())), preferred_element_type=F32)


def _sigmoid(x):
    return 1.0 / (1.0 + jnp.exp(-x))


def _split3(x):
    hi = x.astype(BF16)
    r = x - hi.astype(F32)
    mid = r.astype(BF16)
    lo = (r - mid.astype(F32)).astype(BF16)
    return hi, mid, lo


def _dot3(m_bf, x):
    hi, mid, lo = _split3(x)
    return _dot(m_bf, hi) + _dot(m_bf, mid) + _dot(m_bf, lo)


def _matmul(a, b, *, mode, out_dtype, tm, tn, tk, name, add=None, add_scale=1.0):
    if mode == "nn":
        (M, K), (K2, N) = a.shape, b.shape
        a_spec = pl.BlockSpec((tm, tk), lambda i, j, k: (i, k))
        b_spec = pl.BlockSpec((tk, tn), lambda i, j, k: (k, j))
        dot = _dot
    elif mode == "nt":
        (M, K), (N, K2) = a.shape, b.shape
        a_spec = pl.BlockSpec((tm, tk), lambda i, j, k: (i, k))
        b_spec = pl.BlockSpec((tn, tk), lambda i, j, k: (j, k))
        dot = _dot_nt
    else:
        (K, M), (K2, N) = a.shape, b.shape
        a_spec = pl.BlockSpec((tk, tm), lambda i, j, k: (k, i))
        b_spec = pl.BlockSpec((tk, tn), lambda i, j, k: (k, j))
        dot = _dot_tn
    assert K == K2 and M % tm == 0 and N % tn == 0 and K % tk == 0, (a.shape, b.shape, mode)
    nk = K // tk
    has_add = add is not None

    def body(*refs):
        if has_add:
            a_ref, b_ref, add_ref, o_ref, acc = refs
        else:
            a_ref, b_ref, o_ref, acc = refs
        k = pl.program_id(2)
        part = dot(a_ref[...].astype(BF16), b_ref[...].astype(BF16))

        @pl.when(k == 0)
        def _():
            acc[...] = part

        @pl.when(k > 0)
        def _():
            acc[...] += part

        @pl.when(k == nk - 1)
        def _():
            r = acc[...]
            if has_add:
                r = r + add_scale * add_ref[...]
            o_ref[...] = r.astype(out_dtype)

    in_specs = [a_spec, b_spec]
    args = [a, b]
    if has_add:
        in_specs.append(pl.BlockSpec((tm, tn), lambda i, j, k: (i, j)))
        args.append(add)
    return pl.pallas_call(
        body,
        name=name,
        grid=(M // tm, N // tn, nk),
        in_specs=in_specs,
        out_specs=pl.BlockSpec((tm, tn), lambda i, j, k: (i, j)),
        out_shape=jax.ShapeDtypeStruct((M, N), out_dtype),
        scratch_shapes=[pltpu.VMEM((tm, tn), F32)],
        compiler_params=_params(("parallel", "parallel", "arbitrary")),
    )(*args)


def _chunk_of(rows):
    return lax.shift_right_logical(rows, CHUNK.bit_length() - 1)


A_HPS = 2
A_HW = A_HPS * LANES


def _band_bias(u_row):
    bias = pltpu.roll(jnp.broadcast_to(u_row, (TQ, 2 * TQ)), 0, 1, stride=1, stride_axis=0)
    qc = _chunk_of(lax.broadcasted_iota(jnp.int32, (TQ, 2 * TQ), 0))
    col = lax.broadcasted_iota(jnp.int32, (TQ, 2 * TQ), 1)
    kc = _chunk_of(jnp.bitwise_and(col, TQ - 1))
    ok = jnp.logical_or(jnp.logical_and(col < TQ, kc >= qc), jnp.logical_and(col >= TQ, kc <= qc))
    return jnp.where(ok, bias, NEG_INF)


def _band_probs(q, kp, kc, bias, first):
    scale = HEAD_DIM ** -0.5
    sp = _dot_nt(q, kp) * scale + (bias[:, :TQ] + first * NEG_INF)
    sc = _dot_nt(q, kc) * scale + bias[:, TQ:]
    m = jnp.maximum(jnp.max(sp, axis=1, keepdims=True), jnp.max(sc, axis=1, keepdims=True))
    pp = jnp.exp(sp - m)
    pc = jnp.exp(sc - m)
    inv = 1.0 / (jnp.sum(pp, axis=1, keepdims=True) + jnp.sum(pc, axis=1, keepdims=True))
    return pp * inv, pc * inv


def _band_specs(nq):
    def col(c0):
        return c0 // A_HW

    q_spec = pl.BlockSpec((TQ, A_HW), lambda h, i: (jnp.minimum(i, nq - 1), col(C_AQ) + h))
    kp_spec = pl.BlockSpec((TQ, A_HW), lambda h, i: (jnp.clip(i - 1, 0, nq - 1), col(C_AK) + h))
    kc_spec = pl.BlockSpec((TQ, A_HW), lambda h, i: (jnp.minimum(i, nq - 1), col(C_AK) + h))
    vp_spec = pl.BlockSpec((TQ, A_HW), lambda h, i: (jnp.clip(i - 1, 0, nq - 1), col(C_AV) + h))
    vc_spec = pl.BlockSpec((TQ, A_HW), lambda h, i: (jnp.minimum(i, nq - 1), col(C_AV) + h))
    z_spec = pl.BlockSpec((TQ, A_HW), lambda h, i: (jnp.minimum(i, nq - 1), col(C_AZ) + h))
    u_spec = pl.BlockSpec((A_HPS, 1, 2 * TQ), lambda h, i: (h, 0, 0))
    return q_spec, kp_spec, kc_spec, vp_spec, vc_spec, z_spec, u_spec


def _band_fwd(h, u):
    S = h.shape[0]
    nq = S // TQ

    def body(q_ref, kp_ref, kc_ref, vp_ref, vc_ref, z_ref, u_ref, y_ref, bias_scr):
        i = pl.program_id(1)

        @pl.when(i == 0)
        def _():
            for hh in range(A_HPS):
                bias_scr[hh] = _band_bias(u_ref[hh])

        first = (i == 0).astype(F32)
        for hh in range(A_HPS):
            cs = slice(hh * LANES, (hh + 1) * LANES)
            pp, pc = _band_probs(q_ref[:, cs], kp_ref[:, cs], kc_ref[:, cs], bias_scr[hh], first)
            o = _dot(pp.astype(BF16), vp_ref[:, cs]) + _dot(pc.astype(BF16), vc_ref[:, cs])
            z = z_ref[:, cs].astype(F32)
            y_ref[:, cs] = (o * (z * _sigmoid(z))).astype(BF16)

    specs = _band_specs(nq)
    return pl.pallas_call(
        body,
        name="band_fwd",
        grid=(A_HEADS // A_HPS, nq),
        in_specs=[specs[0], specs[1], specs[2], specs[3], specs[4], specs[5], specs[6]],
        out_specs=pl.BlockSpec((TQ, A_HW), lambda h, i: (i, h)),
        out_shape=jax.ShapeDtypeStruct((S, A_WIDTH), BF16),
        scratch_shapes=[pltpu.VMEM((A_HPS, TQ, 2 * TQ), F32)],
        compiler_params=_params(("parallel", "arbitrary")),
    )(h, h, h, h, h, h, u)


def _band_bwd(h, u, dycat):
    S = h.shape[0]
    nq = S // TQ
    scale = HEAD_DIM ** -0.5

    def body(q_ref, kp_ref, kc_ref, vp_ref, vc_ref, z_ref, u_ref, dy_ref,
             dq_ref, dk_ref, dv_ref, dz_ref, du_ref, bias_scr, db_scr, ck_scr, cv_scr):
        i = pl.program_id(1)

        @pl.when(i == 0)
        def _():
            for hh in range(A_HPS):
                bias_scr[hh] = _band_bias(u_ref[hh])
            db_scr[...] = jnp.zeros_like(db_scr)
            ck_scr[...] = jnp.zeros_like(ck_scr)
            cv_scr[...] = jnp.zeros_like(cv_scr)

        @pl.when(i < nq)
        def _():
            first = (i == 0).astype(F32)
            for hh in range(A_HPS):
                cs = slice(hh * LANES, (hh + 1) * LANES)
                q, kp, kc, vp, vc = q_ref[:, cs], kp_ref[:, cs], kc_ref[:, cs], vp_ref[:, cs], vc_ref[:, cs]
                pp, pc = _band_probs(q, kp, kc, bias_scr[hh], first)
                ppb, pcb = pp.astype(BF16), pc.astype(BF16)
                o = _dot(ppb, vp) + _dot(pcb, vc)
                z = z_ref[:, cs].astype(F32)
                sg = _sigmoid(z)
                dy = dy_ref[:, cs].astype(F32)
                do = dy * (z * sg)
                dz_ref[:, cs] = (dy * o * (sg * (1.0 + z * (1.0 - sg)))).astype(BF16)
                dob = do.astype(BF16)
                delta = jnp.sum(do * o, axis=1, keepdims=True)
                dsp = pp * (_dot_nt(dob, vp) - delta)
                dsc = pc * (_dot_nt(dob, vc) - delta)
                db_scr[hh, :, :TQ] += dsp
                db_scr[hh, :, TQ:] += dsc
                dspb, dscb = dsp.astype(BF16), dsc.astype(BF16)
                dq_ref[:, cs] = (scale * (_dot(dspb, kp) + _dot(dscb, kc))).astype(BF16)
                dk_ref[:, cs] = (ck_scr[:, cs] + scale * _dot_tn(dspb, q)).astype(BF16)
                dv_ref[:, cs] = (cv_scr[:, cs] + _dot_tn(ppb, dob)).astype(BF16)
                ck_scr[:, cs] = scale * _dot_tn(dscb, q)
                cv_scr[:, cs] = _dot_tn(pcb, dob)

        @pl.when(i == nq)
        def _():
            dk_ref[...] = ck_scr[...].astype(BF16)
            dv_ref[...] = cv_scr[...].astype(BF16)
            r0 = lax.broadcasted_iota(jnp.int32, (TQ, TQ), 0)
            r1 = lax.broadcasted_iota(jnp.int32, (TQ, TQ), 1)
            flip = (r0 + r1 == TQ - 1).astype(BF16)
            for hh in range(A_HPS):
                fl = _dot3(flip, db_scr[hh])
                rolled = pltpu.roll(fl, 0, 1, stride=1, stride_axis=0)
                du_ref[hh] = jnp.sum(rolled, axis=0, keepdims=True)

    specs = _band_specs(nq)
    row_spec = pl.BlockSpec((TQ, A_HW), lambda h, i: (jnp.minimum(i, nq - 1), h))
    key_spec = pl.BlockSpec((TQ, A_HW), lambda h, i: (jnp.maximum(i - 1, 0), h))
    out_sd = jax.ShapeDtypeStruct((S, A_WIDTH), BF16)
    return pl.pallas_call(
        body,
        name="band_bwd",
        grid=(A_HEADS // A_HPS, nq + 1),
        in_specs=[specs[0], specs[1], specs[2], specs[3], specs[4], specs[5], specs[6], row_spec],
        out_specs=[row_spec, key_spec, key_spec, row_spec, pl.BlockSpec((A_HPS, 1, 2 * TQ), lambda h, i: (h, 0, 0))],
        out_shape=[out_sd, out_sd, out_sd, out_sd, jax.ShapeDtypeStruct((A_HEADS, 1, 2 * TQ), F32)],
        scratch_shapes=[pltpu.VMEM((A_HPS, TQ, 2 * TQ), F32), pltpu.VMEM((A_HPS, TQ, 2 * TQ), F32),
                        pltpu.VMEM((TQ, A_HW), F32), pltpu.VMEM((TQ, A_HW), F32)],
        compiler_params=_params(("parallel", "arbitrary")),
    )(h, h, h, h, h, h, u, dycat)


def _bias_by_offset(table):
    far = jnp.broadcast_to(table[:, N_REL - 1:], (A_HEADS, TQ - MAX_REL))
    ramp = jnp.flip(table, axis=1)
    rest = jnp.broadcast_to(table[:, :1], (A_HEADS, 2 * TQ - CHUNK - (TQ + MAX_REL + 1)))
    wrap = jnp.broadcast_to(table[:, N_REL - 1:], (A_HEADS, CHUNK))
    return jnp.concatenate([far, ramp, rest, wrap], axis=1)[:, None, :]


def _bias_grad_from_offset(du):
    g = jnp.roll(du[:, 0, :], -(TQ - 1), axis=1)
    far = jnp.sum(g[:, :TQ - MAX_REL], axis=1) + jnp.sum(g[:, 2 * TQ - CHUNK:], axis=1)
    ramp = jnp.flip(g[:, TQ - MAX_REL:TQ + MAX_REL + 1], axis=1)
    return ramp.at[:, N_REL - 1].add(far)


def _mem_probs(q, mk):
    s = _dot_nt(q, mk) * (HEAD_DIM ** -0.5)
    p = jnp.exp(s - jnp.max(s, axis=1, keepdims=True))
    return p * (1.0 / jnp.sum(p, axis=1, keepdims=True))


def _mem_fwd(h, mkv):
    S = h.shape[0]
    nm = mkv.shape[0]

    def body(q_ref, z_ref, mk_ref, mv_ref, y_ref):
        p = _mem_probs(q_ref[...], mk_ref[...])
        o = _dot(p.astype(BF16), mv_ref[...])
        z = z_ref[...].astype(F32)
        y_ref[...] = (o * (z * _sigmoid(z))).astype(BF16)

    return pl.pallas_call(
        body,
        name="mem_fwd",
        grid=(M_HEADS, S // TQ),
        in_specs=[pl.BlockSpec((TQ, LANES), lambda h, i: (i, C_MQ // LANES + h)),
                  pl.BlockSpec((TQ, LANES), lambda h, i: (i, C_MZ // LANES + h)),
                  pl.BlockSpec((nm, LANES), lambda h, i: (0, h)),
                  pl.BlockSpec((nm, LANES), lambda h, i: (0, M_HEADS + h))],
        out_specs=pl.BlockSpec((TQ, LANES), lambda h, i: (i, h)),
        out_shape=jax.ShapeDtypeStruct((S, M_WIDTH), BF16),
        compiler_params=_params(("parallel", "arbitrary")),
    )(h, h, mkv, mkv)


def _mem_bwd(h, mkv, dycat):
    S = h.shape[0]
    nm = mkv.shape[0]
    scale = HEAD_DIM ** -0.5

    def body(q_ref, z_ref, mk_ref, mv_ref, dy_ref, dq_ref, dz_ref, dmk_ref, dmv_ref):
        i = pl.program_id(1)
        q, mk, mv = q_ref[...], mk_ref[...], mv_ref[...]
        p = _mem_probs(q, mk)
        pb = p.astype(BF16)
        o = _dot(pb, mv)
        z = z_ref[...].astype(F32)
        sg = _sigmoid(z)
        dy = dy_ref[...].astype(F32)
        do = dy * (z * sg)
        dz_ref[...] = (dy * o * (sg * (1.0 + z * (1.0 - sg)))).astype(BF16)
        dob = do.astype(BF16)
        ds = p * (_dot_nt(dob, mv) - jnp.sum(do * o, axis=1, keepdims=True))
        dsb = ds.astype(BF16)
        dq_ref[...] = (scale * _dot(dsb, mk)).astype(BF16)
        dmk = scale * _dot_tn(dsb, q)
        dmv = _dot_tn(pb, dob)

        @pl.when(i == 0)
        def _():
            dmk_ref[...] = dmk
            dmv_ref[...] = dmv

        @pl.when(i > 0)
        def _():
            dmk_ref[...] += dmk
            dmv_ref[...] += dmv

    row = pl.BlockSpec((TQ, LANES), lambda h, i: (i, h))
    out_sd = jax.ShapeDtypeStruct((S, M_WIDTH), BF16)
    dq, dz, dmkv, dmkv2 = pl.pallas_call(
        body,
        name="mem_bwd",
        grid=(M_HEADS, S // TQ),
        in_specs=[pl.BlockSpec((TQ, LANES), lambda h, i: (i, C_MQ // LANES + h)),
                  pl.BlockSpec((TQ, LANES), lambda h, i: (i, C_MZ // LANES + h)),
                  pl.BlockSpec((nm, LANES), lambda h, i: (0, h)),
                  pl.BlockSpec((nm, LANES), lambda h, i: (0, M_HEADS + h)),
                  pl.BlockSpec((TQ, LANES), lambda h, i: (i, (A_WIDTH + B_WIDTH) // LANES + h))],
        out_specs=[row, row,
                   pl.BlockSpec((nm, LANES), lambda h, i: (0, h)),
                   pl.BlockSpec((nm, LANES), lambda h, i: (0, h))],
        out_shape=[out_sd, out_sd, jax.ShapeDtypeStruct((nm, M_WIDTH), F32), jax.ShapeDtypeStruct((nm, M_WIDTH), F32)],
        compiler_params=_params(("parallel", "arbitrary")),
    )(h, h, mkv, mkv, dycat)
    return dq, dz, jnp.concatenate([dmkv, dmkv2], axis=1)


def _chunk_masks():
    r = lax.broadcasted_iota(jnp.int32, (TQ, TQ), 0)
    c = lax.broadcasted_iota(jnp.int32, (TQ, TQ), 1)
    same = _chunk_of(r) == _chunk_of(c)
    return jnp.logical_and(same, c <= r), jnp.logical_and(same, c > r)


def _gla_gates(lr, gw, gb):
    logit = _dot(lr, gw) + gb
    sg = _sigmoid(logit)
    g = (jnp.minimum(logit, 0.0) - jnp.log(1.0 + jnp.exp(-jnp.abs(logit)))) * (1.0 / GATE_TAU)
    lo, _ = _chunk_masks()
    return sg, _dot3(lo.astype(BF16), g)


def _gla_factors(q, k, b):
    eb = jnp.exp(b)
    enb = jnp.exp(-b)
    return eb, enb, q * eb, q * enb, k * eb, k * enb


def _gla_intra(qp, qn, kp, kn):
    lo, up = _chunk_masks()
    return (jnp.where(lo, _dot_nt(qp.astype(BF16), kn.astype(BF16)), 0.0)
            + jnp.where(up, _dot_nt(qn.astype(BF16), kp.astype(BF16)), 0.0))


def _gla_specs(nb, rev):
    blk = (lambda i: nb - 1 - i) if rev else (lambda i: i)

    def at(c0):
        return pl.BlockSpec((TQ, LANES), lambda i, h: (blk(i), c0 // LANES + h))

    lr_spec = pl.BlockSpec((TQ, LANES), lambda i, h: (blk(i), C_LR // LANES))
    gw_spec = pl.BlockSpec((LANES, LANES), lambda i, h: (0, h))
    gb_spec = pl.BlockSpec((1, LANES), lambda i, h: (0, h))
    gn_spec = pl.BlockSpec((1, LANES), lambda i, h: (0, 0))
    return at(C_BQ), at(C_BK), at(C_BV), at(C_BZ), lr_spec, gw_spec, gb_spec, gn_spec, blk


def _gla_fwd(h, gw, gb, gn):
    S = h.shape[0]
    nb = S // TQ

    def body(q_ref, k_ref, v_ref, z_ref, lr_ref, gw_ref, gb_ref, gn_ref, y_ref, o_ref, st_ref, st_scr):
        i, hd = pl.program_id(0), pl.program_id(1)

        @pl.when(i == 0)
        def _():
            st_scr[hd] = jnp.zeros((LANES, LANES), F32)

        q = q_ref[...].astype(F32) * (B_DK ** -0.5)
        k = k_ref[...].astype(F32)
        v = v_ref[...]
        _, b = _gla_gates(lr_ref[...], gw_ref[...], gb_ref[...])
        _, _, qp, qn, kp, kn = _gla_factors(q, k, b)
        o_intra = _dot(_gla_intra(qp, qn, kp, kn).astype(BF16), v)
        qpb, knb = qp.astype(BF16), kn.astype(BF16)
        st = st_scr[hd]
        outs = []
        for c in range(CPB):
            rows = slice(c * CHUNK, (c + 1) * CHUNK)
            st_ref[0, c] = st
            outs.append(_dot_nt(qpb[rows], st.astype(BF16)))
            e_last = jnp.exp(b[(c + 1) * CHUNK - 1:(c + 1) * CHUNK, :])
            st = (st + _dot_tn(v[rows], knb[rows])) * e_last
        st_scr[hd] = st
        o = o_intra + jnp.concatenate(outs, axis=0)
        o_ref[...] = o
        r = lax.rsqrt(jnp.mean(o * o, axis=1, keepdims=True) + RMS_EPS)
        z = z_ref[...].astype(F32)
        y_ref[...] = (o * r * gn_ref[...] * (z * _sigmoid(z))).astype(BF16)

    q_s, k_s, v_s, z_s, lr_s, gw_s, gb_s, gn_s, _ = _gla_specs(nb, False)
    row = pl.BlockSpec((TQ, LANES), lambda i, h: (i, h))
    return pl.pallas_call(
        body,
        name="gla_fwd",
        grid=(nb, B_HEADS),
        in_specs=[q_s, k_s, v_s, z_s, lr_s, gw_s, gb_s, gn_s],
        out_specs=[row, row, pl.BlockSpec((1, CPB, LANES, LANES), lambda i, h: (h, i, 0, 0))],
        out_shape=[jax.ShapeDtypeStruct((S, B_WIDTH), BF16), jax.ShapeDtypeStruct((S, B_WIDTH), F32),
                   jax.ShapeDtypeStruct((B_HEADS, S // CHUNK, LANES, LANES), F32)],
        scratch_shapes=[pltpu.VMEM((B_HEADS, LANES, LANES), F32)],
        compiler_params=_params(("arbitrary", "arbitrary")),
    )(h, h, h, h, h, gw, gb, gn)


def _gla_bwd(h, gw, gb, gn, o_pre, states, dycat):
    S = h.shape[0]
    nb = S // TQ

    def body(q_ref, k_ref, v_ref, z_ref, lr_ref, gw_ref, gb_ref, gn_ref, o_ref, st_ref, dy_ref,
             dq_ref, dk_ref, dv_ref, dz_ref, dlr_ref, dgw_ref, dgb_ref, dgn_ref,
             dst_scr, dgw_scr, dgb_scr, dgn_scr):
        i, hd = pl.program_id(0), pl.program_id(1)

        @pl.when(i == 0)
        def _():
            dst_scr[hd] = jnp.zeros((LANES, LANES), F32)
            dgw_scr[hd] = jnp.zeros((LANES, LANES), F32)
            dgb_scr[hd] = jnp.zeros((1, LANES), F32)

        @pl.when(jnp.logical_and(i == 0, hd == 0))
        def _():
            dgn_scr[...] = jnp.zeros_like(dgn_scr)

        q = q_ref[...].astype(F32) * (B_DK ** -0.5)
        k = k_ref[...].astype(F32)
        v = v_ref[...]
        lr, gwv = lr_ref[...], gw_ref[...]
        sg, b = _gla_gates(lr, gwv, gb_ref[...])
        eb, enb, qp, qn, kp, kn = _gla_factors(q, k, b)
        a = _gla_intra(qp, qn, kp, kn)
        qpb, qnb, kpb, knb = qp.astype(BF16), qn.astype(BF16), kp.astype(BF16), kn.astype(BF16)

        o = o_ref[...]
        gn = gn_ref[...]
        r = lax.rsqrt(jnp.mean(o * o, axis=1, keepdims=True) + RMS_EPS)
        z = z_ref[...].astype(F32)
        sz = _sigmoid(z)
        dy = dy_ref[...].astype(F32)
        d_on = dy * (z * sz)
        dz_ref[...] = (dy * (o * r * gn) * (sz * (1.0 + z * (1.0 - sz)))).astype(BF16)
        dgn_scr[...] += jnp.sum(d_on * o * r, axis=0, keepdims=True)
        t = d_on * gn
        do = r * t - o * (r * r * r) * jnp.mean(t * o, axis=1, keepdims=True)
        dob = do.astype(BF16)

        lo, up = _chunk_masks()
        da = _dot_nt(dob, v)
        dalo = jnp.where(lo, da, 0.0).astype(BF16)
        daup = jnp.where(up, da, 0.0).astype(BF16)
        dqp = _dot(dalo, knb)
        dkn = _dot_tn(dalo, qpb)
        dqn = _dot(daup, kpb)
        dkp = _dot_tn(daup, qnb)
        dv = _dot_tn(a.astype(BF16), dob)

        dst = dst_scr[hd]
        dqp_c, dkn_c, dv_c, dbl_c = [None] * CPB, [None] * CPB, [None] * CPB, [None] * CPB
        for c in reversed(range(CPB)):
            rows = slice(c * CHUNK, (c + 1) * CHUNK)
            st = st_ref[0, c]
            e_last = jnp.exp(b[(c + 1) * CHUNK - 1:(c + 1) * CHUNK, :])
            if c == CPB - 1:
                st_next = (st + _dot_tn(v[rows], knb[rows])) * e_last
            else:
                st_next = st_ref[0, c + 1]
            dbl_c[c] = jnp.sum(dst * st_next, axis=0, keepdims=True)
            dtt = (dst * e_last).astype(BF16)
            dv_c[c] = _dot_nt(knb[rows], dtt)
            dkn_c[c] = _dot(v[rows], dtt)
            dqp_c[c] = _dot(dob[rows], st.astype(BF16))
            dst = _dot_tn(dob[rows], qpb[rows]) + dst * e_last
        dst_scr[hd] = dst
        dqp = dqp + jnp.concatenate(dqp_c, axis=0)
        dkn = dkn + jnp.concatenate(dkn_c, axis=0)
        dv = dv + jnp.concatenate(dv_c, axis=0)
        dv_ref[...] = dv.astype(BF16)
        dq_ref[...] = ((dqp * eb + dqn * enb) * (B_DK ** -0.5)).astype(BF16)
        dk_ref[...] = (dkp * eb + dkn * enb).astype(BF16)

        last = jnp.bitwise_and(lax.broadcasted_iota(jnp.int32, (TQ, 1), 0), CHUNK - 1) == CHUNK - 1
        dbl = jnp.concatenate([jnp.broadcast_to(x, (CHUNK, LANES)) for x in dbl_c], axis=0)
        db = dqp * qp - dqn * qn + dkp * kp - dkn * kn + jnp.where(last, dbl, 0.0)
        r0 = lax.broadcasted_iota(jnp.int32, (TQ, TQ), 0)
        r1 = lax.broadcasted_iota(jnp.int32, (TQ, TQ), 1)
        upper = jnp.logical_and(_chunk_of(r0) == _chunk_of(r1), r1 >= r0).astype(BF16)
        dlogit = _dot3(upper, db) * (1.0 / GATE_TAU) * (1.0 - sg)
        dlb = dlogit.astype(BF16)
        dlr = _dot_nt(dlb, gwv)

        @pl.when(hd == 0)
        def _():
            dlr_ref[...] = dlr

        @pl.when(hd > 0)
        def _():
            dlr_ref[...] += dlr

        dgw_scr[hd] += _dot_tn(lr, dlb)
        dgb_scr[hd] += jnp.sum(dlogit, axis=0, keepdims=True)

        @pl.when(i == nb - 1)
        def _():
            dgw_ref[...] = dgw_scr[hd]
            dgb_ref[...] = dgb_scr[hd]
            dgn_ref[...] = dgn_scr[...]

    q_s, k_s, v_s, z_s, lr_s, gw_s, gb_s, gn_s, blk = _gla_specs(nb, True)
    row = pl.BlockSpec((TQ, LANES), lambda i, h: (blk(i), h))
    dy_spec = pl.BlockSpec((TQ, LANES), lambda i, h: (blk(i), A_WIDTH // LANES + h))
    st_spec = pl.BlockSpec((1, CPB, LANES, LANES), lambda i, h: (h, blk(i), 0, 0))
    out_sd = jax.ShapeDtypeStruct((S, B_WIDTH), BF16)
    return pl.pallas_call(
        body,
        name="gla_bwd",
        grid=(nb, B_HEADS),
        in_specs=[q_s, k_s, v_s, z_s, lr_s, gw_s, gb_s, gn_s, row, st_spec, dy_spec],
        out_specs=[row, row, row, row,
                   pl.BlockSpec((TQ, LANES), lambda i, h: (blk(i), 0)),
                   pl.BlockSpec((LANES, LANES), lambda i, h: (0, jnp.where(i == nb - 1, h, 0))),
                   pl.BlockSpec((1, LANES), lambda i, h: (0, jnp.where(i == nb - 1, h, 0))),
                   pl.BlockSpec((1, LANES), lambda i, h: (0, 0))],
        out_shape=[out_sd, out_sd, out_sd, out_sd,
                   jax.ShapeDtypeStruct((S, LANES), F32),
                   jax.ShapeDtypeStruct((LANES, B_HEADS * LANES), F32),
                   jax.ShapeDtypeStruct((1, B_HEADS * LANES), F32),
                   jax.ShapeDtypeStruct((1, LANES), F32)],
        scratch_shapes=[pltpu.VMEM((B_HEADS, LANES, LANES), F32), pltpu.VMEM((B_HEADS, LANES, LANES), F32),
                        pltpu.VMEM((B_HEADS, 1, LANES), F32), pltpu.VMEM((1, LANES), F32)],
        compiler_params=_params(("arbitrary", "arbitrary")),
    )(h, h, h, h, h, gw, gb, gn, o_pre, states, dycat)


LN_ROWS = 256


def _outproj_ln(ycat, w_out, x, g, b):
    S = x.shape[0]

    def body(yc_ref, w_ref, x_ref, g_ref, b_ref, y_ref, yb_ref, xh_ref, rs_ref):
        u = ALPHA * x_ref[...] + _dot(yc_ref[...], w_ref[...])
        mu = jnp.mean(u, axis=1, keepdims=True)
        d = u - mu
        rstd = lax.rsqrt(jnp.mean(d * d, axis=1, keepdims=True) + LN_EPS)
        xh = d * rstd
        y = xh * g_ref[...] + b_ref[...]
        y_ref[...] = y
        yb_ref[...] = y.astype(BF16)
        xh_ref[...] = xh
        rs_ref[...] = rstd

    row = lambda w: pl.BlockSpec((LN_ROWS, w), lambda i: (i, 0))
    vec = pl.BlockSpec((1, D_MODEL), lambda i: (0, 0))
    return pl.pallas_call(
        body,
        name="outproj_ln",
        grid=(S // LN_ROWS,),
        in_specs=[row(D_MODEL), pl.BlockSpec((D_MODEL, D_MODEL), lambda i: (0, 0)), row(D_MODEL), vec, vec],
        out_specs=[row(D_MODEL), row(D_MODEL), row(D_MODEL), row(1)],
        out_shape=[jax.ShapeDtypeStruct((S, D_MODEL), F32), jax.ShapeDtypeStruct((S, D_MODEL), BF16),
                   jax.ShapeDtypeStruct((S, D_MODEL), F32), jax.ShapeDtypeStruct((S, 1), F32)],
        compiler_params=_params(("parallel",)),
    )(ycat, w_out, x, g, b)


def _ln_bwd(dy, xhat, rstd, g):
    S = dy.shape[0]

    def body(dy_ref, xh_ref, rs_ref, g_ref, du_ref, dub_ref, dg_ref, db_ref):
        i = pl.program_id(0)
        dy_, xh = dy_ref[...], xh_ref[...]
        dyg = dy_ * g_ref[...]
        m1 = jnp.mean(dyg, axis=1, keepdims=True)
        m2 = jnp.mean(dyg * xh, axis=1, keepdims=True)
        du = rs_ref[...] * (dyg - m1 - xh * m2)
        du_ref[...] = du
        dub_ref[...] = du.astype(BF16)
        dg = jnp.sum(dy_ * xh, axis=0, keepdims=True)
        db = jnp.sum(dy_, axis=0, keepdims=True)

        @pl.when(i == 0)
        def _():
            dg_ref[...] = dg
            db_ref[...] = db

        @pl.when(i > 0)
        def _():
            dg_ref[...] += dg
            db_ref[...] += db

    row = lambda w: pl.BlockSpec((TQ, w), lambda i: (i, 0))
    vec = pl.BlockSpec((1, D_MODEL), lambda i: (0, 0))
    return pl.pallas_call(
        body,
        name="ln_bwd",
        grid=(S // TQ,),
        in_specs=[row(D_MODEL), row(D_MODEL), row(1), vec],
        out_specs=[row(D_MODEL), row(D_MODEL), vec, vec],
        out_shape=[jax.ShapeDtypeStruct((S, D_MODEL), F32), jax.ShapeDtypeStruct((S, D_MODEL), BF16),
                   jax.ShapeDtypeStruct((1, D_MODEL), F32), jax.ShapeDtypeStruct((1, D_MODEL), F32)],
        compiler_params=_params(("arbitrary",)),
    )(dy, xhat, rstd, g)


def _loss_head(y, target):
    S = y.shape[0]

    def body(y_ref, t_ref, l_ref, dy_ref):
        i = pl.program_id(0)
        err = y_ref[...] - t_ref[...]
        dy_ref[...] = err * (1.0 / D_MODEL)
        part = (0.5 / D_MODEL) * jnp.sum(jnp.sum(err * err, axis=1, keepdims=True), axis=0, keepdims=True)

        @pl.when(i == 0)
        def _():
            l_ref[...] = part

        @pl.when(i > 0)
        def _():
            l_ref[...] += part

    row = pl.BlockSpec((TQ, D_MODEL), lambda i: (i, 0))
    return pl.pallas_call(
        body,
        name="loss_head",
        grid=(S // TQ,),
        in_specs=[row, row],
        out_specs=[pl.BlockSpec((1, 1), lambda i: (0, 0)), row],
        out_shape=[jax.ShapeDtypeStruct((1, 1), F32), jax.ShapeDtypeStruct((S, D_MODEL), F32)],
        compiler_params=_params(("arbitrary",)),
    )(y, target)


def _pad_gate(gate_w, gate_b):
    gw = gate_w.reshape(GATE_RANK, B_HEADS, B_DK)
    gw = jnp.pad(gw, ((0, LANES - GATE_RANK), (0, 0), (0, LANES - B_DK))).reshape(LANES, B_HEADS * LANES)
    gb = jnp.pad(gate_b.reshape(B_HEADS, B_DK), ((0, 0), (0, LANES - B_DK))).reshape(1, B_HEADS * LANES)
    return gw.astype(BF16), gb.astype(F32)


def _layer_fwd(x, xb, mem_b, w_in, w_kv, w_out, u, gw, gb, gn, ln_g, ln_b):
    h = _matmul(xb, w_in, mode="nn", out_dtype=BF16, tm=1024, tn=768, tk=D_MODEL, name="in_proj")
    mkv = _matmul(mem_b, w_kv, mode="nn", out_dtype=BF16, tm=mem_b.shape[0], tn=1024, tk=D_MODEL, name="mem_kv")
    ya = _band_fwd(h, u)
    yb_, o_pre, states = _gla_fwd(h, gw, gb, gn)
    ym = _mem_fwd(h, mkv)
    ycat = jnp.concatenate([ya, yb_, ym], axis=1)
    y, ybf, xhat, rstd = _outproj_ln(ycat, w_out, x, ln_g, ln_b)
    return y, ybf, (xb, h, mkv, ycat, o_pre, states, xhat, rstd)


def _layer_bwd(dy, saved, mem_b, w_in, w_out, u, gw, gb, gn, ln_g):
    xb, h, mkv, ycat, o_pre, states, xhat, rstd = saved
    S = dy.shape[0]
    du, dub, d_ln_g, d_ln_b = _ln_bwd(dy, xhat, rstd, ln_g)
    dycat = _matmul(dub, w_out, mode="nt", out_dtype=BF16, tm=1024, tn=1024, tk=D_MODEL, name="dycat")
    d_w_out = _matmul(ycat, dub, mode="tn", out_dtype=F32, tm=1024, tn=1024, tk=512, name="d_w_out")
    daq, dak, dav, daz, d_u = _band_bwd(h, u, dycat)
    dbq, dbk, dbv, dbz, dlr, dgw, dgb, dgn = _gla_bwd(h, gw, gb, gn, o_pre, states, dycat)
    dmq, dmz, dmkv = _mem_bwd(h, mkv, dycat)
    d_w_kv = _matmul(mem_b, dmkv, mode="tn", out_dtype=F32, tm=1024, tn=1024, tk=mem_b.shape[0], name="d_w_kv")
    dh = jnp.concatenate([daq, dak, dav, daz, dbq, dbk, dbv, dbz, dmq, dmz, dlr.astype(BF16),
                          jnp.zeros((S, HP - C_LR - LANES), BF16)], axis=1)
    dx = _matmul(dh, w_in, mode="nt", out_dtype=F32, tm=1024, tn=1024, tk=768, name="dx", add=du, add_scale=ALPHA)
    d_w_in = _matmul(xb, dh, mode="tn", out_dtype=F32, tm=1024, tn=768, tk=512, name="d_w_in")
    return dx, (d_w_in, d_u, dgw, dgb, dgn, d_w_kv, d_w_out, d_ln_g, d_ln_b)


def _pad_heads(w):
    r = w.shape[0]
    return jnp.pad(w.reshape(r, B_HEADS, B_DK), ((0, 0), (0, 0), (0, LANES - B_DK))).reshape(r, B_HEADS * LANES)


def _unpad_heads(w):
    r = w.shape[0]
    return w.reshape(r, B_HEADS, LANES)[:, :, :B_DK].reshape(r, B_KEY_WIDTH)


O_BQ = 4 * A_WIDTH
O_BK = O_BQ + B_KEY_WIDTH
O_BV = O_BK + B_KEY_WIDTH
O_LR = O_BV + 2 * B_WIDTH
O_MQ = O_LR + GATE_RANK


def _to_padded(w):
    r = w.shape[0]
    return jnp.concatenate([
        w[:, :O_BQ], _pad_heads(w[:, O_BQ:O_BK]), _pad_heads(w[:, O_BK:O_BV]), w[:, O_BV:O_LR], w[:, O_MQ:],
        w[:, O_LR:O_MQ], jnp.zeros((r, HP - C_LR - GATE_RANK), w.dtype)], axis=1)


def _from_padded(g):
    return jnp.concatenate([
        g[:, :C_BQ], _unpad_heads(g[:, C_BQ:C_BK]), _unpad_heads(g[:, C_BK:C_BV]), g[:, C_BV:C_MQ],
        g[:, C_LR:C_LR + GATE_RANK], g[:, C_MQ:C_LR]], axis=1)


def _adamw(w, g, m, v, name):
    R, C = w.shape
    tr = R
    for cand in (256, 128, 64, 32, 16, 8):
        if R % cand == 0 and R > cand:
            tr = cand
            break

    def body(w_ref, g_ref, m_ref, v_ref, d_ref, nm_ref, nv_ref):
        g_ = g_ref[...]
        nm = ADAM_B1 * m_ref[...] + (1.0 - ADAM_B1) * g_
        nv = ADAM_B2 * v_ref[...] + (1.0 - ADAM_B2) * (g_ * g_)
        m_hat = nm / (1.0 - ADAM_B1 ** ADAM_STEP)
        v_hat = nv / (1.0 - ADAM_B2 ** ADAM_STEP)
        d_ref[...] = -ADAM_LR * (m_hat / (jnp.sqrt(v_hat) + ADAM_EPS) + ADAM_WD * w_ref[...])
        nm_ref[...] = nm
        nv_ref[...] = nv

    spec = pl.BlockSpec((tr, C), lambda i: (i, 0))
    sd = jax.ShapeDtypeStruct((R, C), F32)
    return pl.pallas_call(
        body, name=name, grid=(R // tr,), in_specs=[spec] * 4, out_specs=[spec] * 3, out_shape=[sd] * 3,
        compiler_params=_params(("parallel",)),
    )(w, g, m, v)


def _adamw_nd(w, g, m, v, name):
    shape = w.shape
    f = lambda a: a.reshape(-1, shape[-1])
    return tuple(o.reshape(shape) for o in _adamw(f(w), f(g), f(m), f(v), name))


ANY = pl.BlockSpec(memory_space=pl.ANY)


def _place():
    x, y, c = lax.axis_index("x"), lax.axis_index("y"), lax.axis_index("c")
    chips = [(1 - x, y), (x, 1 - y), (1 - x, 1 - y)]
    return x, y, c, chips


def _gather_chips(shard, name):
    R, C = shard.shape
    half = R // 2
    assert half % 16 == 0

    def body(s_ref, o_ref, send, recv):
        x, y, c, chips = _place()
        mine = pl.ds(pl.multiple_of(c * half, 16), half)
        other = pl.ds(pl.multiple_of((1 - c) * half, 16), half)

        def cp(k, src, chip, rows, to):
            dst = o_ref.at[2 * chip[0] + chip[1], rows]
            return pltpu.make_async_remote_copy(src_ref=dst if src is None else src, dst_ref=dst, send_sem=send.at[k],
                                                recv_sem=recv.at[k], device_id=to, device_id_type=MESH)

        first = [cp(k, s_ref.at[mine], (x, y), mine, (*chip, c)) for k, chip in enumerate(chips)]
        for f in first:
            f.start()
        passed = [cp(3 + k, None, chip, mine, (x, y, 1 - c)) for k, chip in enumerate(chips)]
        for k, chip in enumerate(chips):
            cp(k, None, chip, mine, (x, y, c)).wait_recv()
            passed[k].start()
        for k, chip in enumerate(chips):
            cp(3 + k, None, chip, other, (x, y, c)).wait_recv()
        for f in first + passed:
            f.wait_send()

    return pl.pallas_call(
        body, name=name, in_specs=[ANY], out_specs=ANY,
        out_shape=jax.ShapeDtypeStruct((N_CHIPS, R, C), shard.dtype),
        scratch_shapes=[pltpu.SemaphoreType.DMA((6,)), pltpu.SemaphoreType.DMA((6,))],
    )(shard)


def _pair_exchange(g4, name):
    n, _, R, C = g4.shape

    def body(g_ref, o_ref, send, recv):
        x, y, c, _ = _place()
        cps = [pltpu.make_async_remote_copy(src_ref=g_ref.at[j, 1 - c], dst_ref=o_ref.at[j], send_sem=send.at[j],
                                            recv_sem=recv.at[j], device_id=(x, y, 1 - c), device_id_type=MESH)
               for j in range(n)]
        for cp in cps:
            cp.start()
        for cp in cps:
            cp.wait()

    return pl.pallas_call(
        body, name=name, in_specs=[ANY], out_specs=ANY, out_shape=jax.ShapeDtypeStruct((n, R, C), g4.dtype),
        scratch_shapes=[pltpu.SemaphoreType.DMA((n,)), pltpu.SemaphoreType.DMA((n,))],
    )(g4)


def _chip_exchange(p, name):
    n, R, C = p.shape

    def body(p_ref, o_ref, send, recv):
        x, y, c, chips = _place()
        me = 2 * x + y
        cps = [pltpu.make_async_remote_copy(src_ref=p_ref.at[2 * chip[0] + chip[1]], dst_ref=o_ref.at[me],
                                            send_sem=send.at[k], recv_sem=recv.at[k], device_id=(*chip, c),
                                            device_id_type=MESH) for k, chip in enumerate(chips)]
        for cp in cps:
            cp.start()
        for k, chip in enumerate(chips):
            pltpu.make_async_remote_copy(src_ref=p_ref.at[me], dst_ref=o_ref.at[2 * chip[0] + chip[1]],
                                         send_sem=send.at[k], recv_sem=recv.at[k], device_id=(*chip, c),
                                         device_id_type=MESH).wait_recv()
        for cp in cps:
            cp.wait_send()

    return pl.pallas_call(
        body, name=name, in_specs=[ANY], out_specs=ANY, out_shape=jax.ShapeDtypeStruct((n, R, C), p.dtype),
        scratch_shapes=[pltpu.SemaphoreType.DMA((3,)), pltpu.SemaphoreType.DMA((3,))],
    )(p)


def _pair_gather(t2, name):
    def body(t_ref, o_ref, send, recv):
        del t_ref
        x, y, c, _ = _place()
        cp = pltpu.make_async_remote_copy(src_ref=o_ref.at[c], dst_ref=o_ref.at[c], send_sem=send, recv_sem=recv,
                                          device_id=(x, y, 1 - c), device_id_type=MESH)
        cp.start()
        pltpu.make_async_remote_copy(src_ref=o_ref.at[c], dst_ref=o_ref.at[1 - c], send_sem=send, recv_sem=recv,
                                     device_id=(x, y, 1 - c), device_id_type=MESH).wait_recv()
        cp.wait_send()

    return pl.pallas_call(
        body, name=name, in_specs=[ANY], out_specs=ANY, out_shape=jax.ShapeDtypeStruct(t2.shape, t2.dtype),
        input_output_aliases={0: 0},
        scratch_shapes=[pltpu.SemaphoreType.DMA, pltpu.SemaphoreType.DMA],
    )(t2)


def _add_halves(g4, recv, c_idx, name):
    n, _, R, C = g4.shape
    tr = 256

    def body(c_ref, a_ref, b_ref, o_ref):
        o_ref[...] = (a_ref[0] + b_ref[...]).astype(BF16)

    return pl.pallas_call(
        body, name=name,
        grid_spec=pltpu.PrefetchScalarGridSpec(
            num_scalar_prefetch=1, grid=(n, R // tr),
            in_specs=[pl.BlockSpec((1, 1, tr, C), lambda j, i, c: (j, c[0], i, 0)),
                      pl.BlockSpec((1, tr, C), lambda j, i, c: (j, i, 0))],
            out_specs=pl.BlockSpec((1, tr, C), lambda j, i, c: (j, i, 0))),
        out_shape=jax.ShapeDtypeStruct((n, R, C), BF16),
        compiler_params=_params(("parallel", "parallel")),
    )(c_idx, g4, recv)


def _add_slots(r, c_idx, name):
    n, R, C = r.shape
    tr = 256

    def body(c_ref, r_ref, o_ref):
        acc = r_ref[0].astype(F32)
        for j in range(1, n):
            acc = acc + r_ref[j].astype(F32)
        o_ref[0] = acc

    return pl.pallas_call(
        body, name=name,
        grid_spec=pltpu.PrefetchScalarGridSpec(
            num_scalar_prefetch=1, grid=(R // tr,),
            in_specs=[pl.BlockSpec((n, tr, C), lambda i, c: (0, i, 0))],
            out_specs=pl.BlockSpec((1, tr, C), lambda i, c: (c[0], i, 0))),
        out_shape=jax.ShapeDtypeStruct((2, R, C), F32),
        compiler_params=_params(("parallel",)),
    )(c_idx, r)


def _reduce_scatter(g4, c_idx, chip, tag):
    got = _pair_exchange(g4, "rs_pair_" + tag)
    p = _add_halves(g4, got, c_idx, "rs_add2_" + tag)
    q = _chip_exchange(p, "rs_chip_" + tag)
    q = lax.dynamic_update_slice_in_dim(q, lax.dynamic_slice_in_dim(p, chip, 1, axis=0), chip, axis=0)
    t2 = _add_slots(q, c_idx, "rs_add4_" + tag)
    return _pair_gather(t2, "rs_gather_" + tag)


def _all_reduce_small(buf, name):
    R = buf.shape[0]

    def flipped(k, x, y, c):
        return ((1 - x) if k & 4 else x, (1 - y) if k & 2 else y, (1 - c) if k & 1 else c)

    def body(b_ref, o_ref, land, send, recv):
        x, y, c, _ = _place()
        me = 4 * x + 2 * y + c
        land[me] = b_ref[...]
        cps = []
        for k in range(1, N_DEV):
            peer = flipped(k, x, y, c)
            cps.append(pltpu.make_async_remote_copy(src_ref=b_ref, dst_ref=land.at[me], send_sem=send.at[k - 1],
                                                    recv_sem=recv.at[k - 1], device_id=peer, device_id_type=MESH))
        for cp in cps:
            cp.start()
        for k in range(1, N_DEV):
            peer = flipped(k, x, y, c)
            slot = 4 * peer[0] + 2 * peer[1] + peer[2]
            pltpu.make_async_remote_copy(src_ref=b_ref, dst_ref=land.at[slot], send_sem=send.at[k - 1],
                                         recv_sem=recv.at[k - 1], device_id=peer, device_id_type=MESH).wait_recv()
        for cp in cps:
            cp.wait_send()
        acc = land[0]
        for j in range(1, N_DEV):
            acc = acc + land[j]
        o_ref[...] = acc

    vm = pl.BlockSpec(memory_space=pltpu.VMEM)
    return pl.pallas_call(
        body, name=name, in_specs=[vm], out_specs=vm,
        out_shape=jax.ShapeDtypeStruct((R, LANES), F32),
        scratch_shapes=[pltpu.VMEM((N_DEV, R, LANES), F32), pltpu.SemaphoreType.DMA((N_DEV - 1,)),
                        pltpu.SemaphoreType.DMA((N_DEV - 1,))],
    )(buf)


def _by_chip_and_half(g, axis):
    L = g.shape[0]
    if axis == 2:
        n = g.shape[2] // N_CHIPS
        t = g.reshape(L, g.shape[1], N_CHIPS, n).transpose(2, 0, 1, 3)
    else:
        n = g.shape[1] // N_CHIPS
        t = g.reshape(L, N_CHIPS, n, g.shape[2]).transpose(1, 0, 2, 3)
    return t.reshape(N_CHIPS, 2, (L // 2) * t.shape[2], t.shape[3])


def kernel(x, mem, w_in, a_rel_bias, b_gate_w, b_gate_b, b_norm_g, w_mem_kv, w_out, ln_g, ln_b, loss_target, m_w_in, m_a_rel_bias, m_b_gate_w, m_b_gate_b, m_b_norm_g, m_w_mem_kv, m_w_out, m_ln_g, m_ln_b, v_w_in, v_a_rel_bias, v_b_gate_w, v_b_gate_b, v_b_norm_g, v_w_mem_kv, v_w_out, v_ln_g, v_ln_b):
    L = w_in.shape[0]
    S = x.shape[1]
    cx, cy, cc = lax.axis_index("x"), lax.axis_index("y"), lax.axis_index("c")
    chip = 2 * cx + cy
    c_idx = jnp.reshape(cc, (1,)).astype(jnp.int32)

    def gathered(w, name):
        shard = w.astype(BF16).reshape(-1, w.shape[2])
        return lax.dynamic_update_slice_in_dim(_gather_chips(shard, name), shard[None], chip, axis=0)

    n_in = w_in.shape[2]
    win_all = gathered(w_in, "gather_w_in")
    r_kv = w_mem_kv.shape[1]
    wkv_all = gathered(w_mem_kv, "gather_w_kv")
    r_out = w_out.shape[1]
    wout_all = gathered(w_out, "gather_w_out")
    win_all = win_all.reshape(N_CHIPS, L, D_MODEL, n_in)
    wkv_all = wkv_all.reshape(N_CHIPS, L, r_kv, w_mem_kv.shape[2])
    wout_all = wout_all.reshape(N_CHIPS, L, r_out, D_MODEL)

    gw_cols = b_gate_w.shape[2]
    gw_slot = jnp.zeros((N_CHIPS, L, GATE_RANK, gw_cols), F32)
    gw_slot = lax.dynamic_update_slice(gw_slot, (0.5 * b_gate_w)[None], (chip, 0, 0, 0))
    gw_flat = gw_slot.reshape(-1)
    n_gw = gw_flat.shape[0]
    pad = (-n_gw) % (8 * LANES)
    gw_full = _all_reduce_small(jnp.pad(gw_flat, (0, pad)).reshape(-1, LANES), "gather_gate_w").reshape(-1)[:n_gw]
    gw_full = gw_full.reshape(N_CHIPS, L, GATE_RANK, gw_cols).transpose(1, 2, 0, 3).reshape(L, GATE_RANK, B_KEY_WIDTH)

    xs = x.reshape(S, D_MODEL)
    mem_b = mem.reshape(mem.shape[1], D_MODEL).astype(BF16)
    target = loss_target.reshape(S, D_MODEL)

    layer_w = []
    for l in range(L):
        w_in_l = _to_padded(jnp.concatenate([win_all[j, l] for j in range(N_CHIPS)], axis=1))
        w_kv_l = jnp.concatenate([wkv_all[j, l] for j in range(N_CHIPS)], axis=0)
        w_out_l = jnp.concatenate([wout_all[j, l] for j in range(N_CHIPS)], axis=0)
        gw_l, gb_l = _pad_gate(gw_full[l], b_gate_b[l])
        layer_w.append((w_in_l, w_kv_l, w_out_l, _bias_by_offset(a_rel_bias[l]), gw_l, gb_l,
                        b_norm_g[l].reshape(1, LANES), ln_g[l].reshape(1, D_MODEL), ln_b[l].reshape(1, D_MODEL)))

    y, yb = xs, xs.astype(BF16)
    saved = []
    for l in range(L):
        w_in_l, w_kv_l, w_out_l, u_l, gw_l, gb_l, gn_l, lg_l, lb_l = layer_w[l]
        y, yb, sv = _layer_fwd(y, yb, mem_b, w_in_l, w_kv_l, w_out_l, u_l, gw_l, gb_l, gn_l, lg_l, lb_l)
        saved.append(sv)
    loss_part, dy = _loss_head(y, target)

    grads = [None] * L
    for l in reversed(range(L)):
        w_in_l, w_kv_l, w_out_l, u_l, gw_l, gb_l, gn_l, lg_l, lb_l = layer_w[l]
        dy, grads[l] = _layer_bwd(dy, saved[l], mem_b, w_in_l, w_out_l, u_l, gw_l, gb_l, gn_l, lg_l)
    grad_x = dy.reshape(x.shape)

    g_w_in = jnp.stack([_from_padded(g[0]) for g in grads])
    g_rel = jnp.stack([_bias_grad_from_offset(g[1]) for g in grads])
    g_gw = jnp.stack([_unpad_heads(g[2][:GATE_RANK]) for g in grads])
    g_gb = jnp.stack([_unpad_heads(g[3])[0] for g in grads])
    g_gn = jnp.stack([g[4][0] for g in grads])
    g_w_kv = jnp.stack([g[5] for g in grads])
    g_w_out = jnp.stack([g[6] for g in grads])
    g_lg = jnp.stack([g[7][0] for g in grads])
    g_lb = jnp.stack([g[8][0] for g in grads])

    r_w_in = _reduce_scatter(_by_chip_and_half(g_w_in, 2), c_idx, chip, "w_in").reshape(L, D_MODEL, n_in)
    r_w_kv = _reduce_scatter(_by_chip_and_half(g_w_kv, 1), c_idx, chip, "w_kv").reshape(L, r_kv, w_mem_kv.shape[2])
    r_w_out = _reduce_scatter(_by_chip_and_half(g_w_out, 1), c_idx, chip, "w_out").reshape(L, r_out, D_MODEL)

    small = [g_rel, g_gw, g_gb, g_gn, g_lg, g_lb, loss_part]
    flat = jnp.concatenate([s.reshape(-1) for s in small])
    n_small = flat.shape[0]
    pad = (-n_small) % (8 * LANES)
    red = _all_reduce_small(jnp.pad(flat, (0, pad)).reshape(-1, LANES), "all_reduce_small").reshape(-1)
    outs, off = [], 0
    for s in small:
        outs.append(red[off:off + s.size].reshape(s.shape))
        off += s.size
    g_rel, g_gw, g_gb, g_gn, g_lg, g_lb, loss = outs
    loss = loss.reshape(())
    g_gw = lax.dynamic_slice_in_dim(g_gw.reshape(L, GATE_RANK, N_CHIPS, gw_cols), chip, 1, axis=2).reshape(L, GATE_RANK, gw_cols)

    g_list = [r_w_in, g_rel, g_gw, g_gb, g_gn, r_w_kv, r_w_out, g_lg, g_lb]
    w_list = [w_in, a_rel_bias, b_gate_w, b_gate_b, b_norm_g, w_mem_kv, w_out, ln_g, ln_b]
    m_list = [m_w_in, m_a_rel_bias, m_b_gate_w, m_b_gate_b, m_b_norm_g, m_w_mem_kv, m_w_out, m_ln_g, m_ln_b]
    v_list = [v_w_in, v_a_rel_bias, v_b_gate_w, v_b_gate_b, v_b_norm_g, v_w_mem_kv, v_w_out, v_ln_g, v_ln_b]
    names = ["w_in", "rel", "gate_w", "gate_b", "norm_g", "w_kv", "w_out", "ln_g", "ln_b"]
    upd = [_adamw_nd(w, g, m, v, "adamw_" + n) for w, g, m, v, n in zip(w_list, g_list, m_list, v_list, names)]
    deltas = [u_[0] for u_ in upd]
    new_m = [u_[1] for u_ in upd]
    new_v = [u_[2] for u_ in upd]
    return (loss, grad_x, *g_list, *deltas, *new_m, *new_v)
```

```python
import functools

import numpy as np
import jax
import jax.numpy as jnp
from jax import lax
from jax.experimental import pallas as pl
from jax.experimental.pallas import tpu as pltpu

F32 = jnp.float32
BF16 = jnp.bfloat16
MESH = pl.DeviceIdType.MESH

D_MODEL = 2048
DEPTH = 4
CHUNK = 64
LEFT_CHUNKS = 8
MAX_REL = 128
N_REL = 2 * MAX_REL + 1
A_HEADS = 8
HEAD_DIM = 128
B_HEADS = 4
B_DK = 64
M_HEADS = 4
GATE_RANK = 16
GATE_TAU = 16.0
A_WIDTH = A_HEADS * HEAD_DIM
B_WIDTH = B_HEADS * HEAD_DIM
B_KEY_WIDTH = B_HEADS * B_DK
M_WIDTH = M_HEADS * HEAD_DIM
IN_WIDTH = 4 * A_WIDTH + 2 * B_KEY_WIDTH + 2 * B_WIDTH + GATE_RANK + 2 * M_WIDTH
ALPHA = (2.0 * DEPTH) ** 0.25
LN_EPS = 1e-5
RMS_EPS = 1e-6
NEG_INF = -1e30
ADAM_LR = 0.001
ADAM_B1 = 0.9
ADAM_B2 = 0.999
ADAM_EPS = 1e-08
ADAM_WD = 0.01
ADAM_STEP = 10

LANES = 128
VMEM_LIMIT = 56 * 1024 * 1024

C_AQ, C_AK, C_AV, C_AZ = 0, 1024, 2048, 3072
C_BQ, C_BK, C_BV, C_BZ = 4096, 4608, 5120, 5632
C_MQ, C_MZ, C_LR = 6144, 6656, 7168
HP = 7680
TQ = 512
CPB = TQ // CHUNK
N_CHIPS = 4
N_DEV = 8


def _params(sem, vmem=VMEM_LIMIT):
    return pltpu.CompilerParams(dimension_semantics=sem, vmem_limit_bytes=vmem)


def _dot(a, b):
    return jnp.dot(a, b, preferred_element_type=F32)


def _dot_nt(a, b):
    return lax.dot_general(a, b, (((1,), (1,)), ((), ())), preferred_element_type=F32)


def _dot_tn(a, b):
    return lax.dot_general(a, b, (((0,), (0,)), ((), ())), preferred_element_type=F32)


def _sigmoid(x):
    return 1.0 / (1.0 + jnp.exp(-x))


def _split3(x):
    hi = x.astype(BF16)
    r = x - hi.astype(F32)
    mid = r.astype(BF16)
    lo = (r - mid.astype(F32)).astype(BF16)
    return hi, mid, lo


def _dot3(m_bf, x):
    hi, mid, lo = _split3(x)
    return _dot(m_bf, hi) + _dot(m_bf, mid) + _dot(m_bf, lo)


def _matmul(a, b, *, mode, out_dtype, tm, tn, tk, name, add=None, add_scale=1.0):
    if mode == "nn":
        (M, K), (K2, N) = a.shape, b.shape
        a_spec = pl.BlockSpec((tm, tk), lambda i, j, k: (i, k))
        b_spec = pl.BlockSpec((tk, tn), lambda i, j, k: (k, j))
        dot = _dot
    elif mode == "nt":
        (M, K), (N, K2) = a.shape, b.shape
        a_spec = pl.BlockSpec((tm, tk), lambda i, j, k: (i, k))
        b_spec = pl.BlockSpec((tn, tk), lambda i, j, k: (j, k))
        dot = _dot_nt
    else:
        (K, M), (K2, N) = a.shape, b.shape
        a_spec = pl.BlockSpec((tk, tm), lambda i, j, k: (k, i))
        b_spec = pl.BlockSpec((tk, tn), lambda i, j, k: (k, j))
        dot = _dot_tn
    assert K == K2 and M % tm == 0 and N % tn == 0 and K % tk == 0, (a.shape, b.shape, mode)
    nk = K // tk
    has_add = add is not None
    assert nk == 1 or out_dtype == F32

    def body(*refs):
        if has_add:
            a_ref, b_ref, add_ref, o_ref = refs
        else:
            a_ref, b_ref, o_ref = refs
        k = pl.program_id(2)
        part = dot(a_ref[...].astype(BF16), b_ref[...].astype(BF16))

        @pl.when(k == 0)
        def _():
            first = part + add_scale * add_ref[...] if has_add else part
            o_ref[...] = first.astype(out_dtype)

        if nk > 1:
            @pl.when(k > 0)
            def _():
                o_ref[...] += part

    in_specs = [a_spec, b_spec]
    args = [a, b]
    if has_add:
        in_specs.append(pl.BlockSpec((tm, tn), lambda i, j, k: (i, j)))
        args.append(add)
    return pl.pallas_call(
        body,
        name=name,
        grid=(M // tm, N // tn, nk),
        in_specs=in_specs,
        out_specs=pl.BlockSpec((tm, tn), lambda i, j, k: (i, j)),
        out_shape=jax.ShapeDtypeStruct((M, N), out_dtype),
        compiler_params=_params(("parallel", "parallel", "arbitrary")),
    )(*args)


def _transpose(a, name):
    R, C = a.shape
    t = 512

    def body(a_ref, o_ref):
        o_ref[...] = a_ref[...].T

    return pl.pallas_call(
        body, name=name, grid=(R // t, C // t),
        in_specs=[pl.BlockSpec((t, t), lambda i, j: (i, j))],
        out_specs=pl.BlockSpec((t, t), lambda i, j: (j, i)),
        out_shape=jax.ShapeDtypeStruct((C, R), a.dtype),
        compiler_params=_params(("parallel", "parallel")),
    )(a)


def _chunk_of(rows):
    return lax.shift_right_logical(rows, CHUNK.bit_length() - 1)


A_HPS = 2
A_HW = A_HPS * LANES


def _band_bias(u_row):
    bias = pltpu.roll(jnp.broadcast_to(u_row, (TQ, 2 * TQ)), 0, 1, stride=1, stride_axis=0)
    qc = _chunk_of(lax.broadcasted_iota(jnp.int32, (TQ, 2 * TQ), 0))
    col = lax.broadcasted_iota(jnp.int32, (TQ, 2 * TQ), 1)
    kc = _chunk_of(jnp.bitwise_and(col, TQ - 1))
    ok = jnp.logical_or(jnp.logical_and(col < TQ, kc >= qc), jnp.logical_and(col >= TQ, kc <= qc))
    return jnp.where(ok, bias, NEG_INF)


def _band_probs(q, kp, kc, bias, first):
    scale = HEAD_DIM ** -0.5
    sp = _dot_nt(q, kp) * scale + (bias[:, :TQ] + first * NEG_INF)
    sc = _dot_nt(q, kc) * scale + bias[:, TQ:]
    m = jnp.maximum(jnp.max(sp, axis=1, keepdims=True), jnp.max(sc, axis=1, keepdims=True))
    pp = jnp.exp(sp - m)
    pc = jnp.exp(sc - m)
    inv = 1.0 / (jnp.sum(pp, axis=1, keepdims=True) + jnp.sum(pc, axis=1, keepdims=True))
    return pp * inv, pc * inv


def _band_specs(nq):
    def col(c0):
        return c0 // A_HW

    q_spec = pl.BlockSpec((TQ, A_HW), lambda h, i: (jnp.minimum(i, nq - 1), col(C_AQ) + h))
    kp_spec = pl.BlockSpec((TQ, A_HW), lambda h, i: (jnp.clip(i - 1, 0, nq - 1), col(C_AK) + h))
    kc_spec = pl.BlockSpec((TQ, A_HW), lambda h, i: (jnp.minimum(i, nq - 1), col(C_AK) + h))
    vp_spec = pl.BlockSpec((TQ, A_HW), lambda h, i: (jnp.clip(i - 1, 0, nq - 1), col(C_AV) + h))
    vc_spec = pl.BlockSpec((TQ, A_HW), lambda h, i: (jnp.minimum(i, nq - 1), col(C_AV) + h))
    z_spec = pl.BlockSpec((TQ, A_HW), lambda h, i: (jnp.minimum(i, nq - 1), col(C_AZ) + h))
    u_spec = pl.BlockSpec((A_HPS, 1, 2 * TQ), lambda h, i: (h, 0, 0))
    return q_spec, kp_spec, kc_spec, vp_spec, vc_spec, z_spec, u_spec


def _band_fwd(h, u):
    S = h.shape[0]
    nq = S // TQ

    def body(q_ref, kp_ref, kc_ref, vp_ref, vc_ref, z_ref, u_ref, y_ref, bias_scr):
        i = pl.program_id(1)

        @pl.when(i == 0)
        def _():
            for hh in range(A_HPS):
                bias_scr[hh] = _band_bias(u_ref[hh])

        first = (i == 0).astype(F32)
        for hh in range(A_HPS):
            cs = slice(hh * LANES, (hh + 1) * LANES)
            pp, pc = _band_probs(q_ref[:, cs], kp_ref[:, cs], kc_ref[:, cs], bias_scr[hh], first)
            o = _dot(pp.astype(BF16), vp_ref[:, cs]) + _dot(pc.astype(BF16), vc_ref[:, cs])
            z = z_ref[:, cs].astype(F32)
            y_ref[:, cs] = (o * (z * _sigmoid(z))).astype(BF16)

    specs = _band_specs(nq)
    return pl.pallas_call(
        body,
        name="band_fwd",
        grid=(A_HEADS // A_HPS, nq),
        in_specs=[specs[0], specs[1], specs[2], specs[3], specs[4], specs[5], specs[6]],
        out_specs=pl.BlockSpec((TQ, A_HW), lambda h, i: (i, h)),
        out_shape=jax.ShapeDtypeStruct((S, A_WIDTH), BF16),
        scratch_shapes=[pltpu.VMEM((A_HPS, TQ, 2 * TQ), F32)],
        compiler_params=_params(("parallel", "arbitrary")),
    )(h, h, h, h, h, h, u)


def _band_bwd(h, u, dycat):
    S = h.shape[0]
    nq = S // TQ
    scale = HEAD_DIM ** -0.5

    def body(q_ref, kp_ref, kc_ref, vp_ref, vc_ref, z_ref, u_ref, dy_ref,
             dq_ref, dk_ref, dv_ref, dz_ref, du_ref, bias_scr, db_scr, ck_scr, cv_scr):
        i = pl.program_id(1)

        @pl.when(i == 0)
        def _():
            for hh in range(A_HPS):
                bias_scr[hh] = _band_bias(u_ref[hh])
            db_scr[...] = jnp.zeros_like(db_scr)
            ck_scr[...] = jnp.zeros_like(ck_scr)
            cv_scr[...] = jnp.zeros_like(cv_scr)

        @pl.when(i < nq)
        def _():
            first = (i == 0).astype(F32)
            for hh in range(A_HPS):
                cs = slice(hh * LANES, (hh + 1) * LANES)
                q, kp, kc, vp, vc = q_ref[:, cs], kp_ref[:, cs], kc_ref[:, cs], vp_ref[:, cs], vc_ref[:, cs]
                pp, pc = _band_probs(q, kp, kc, bias_scr[hh], first)
                ppb, pcb = pp.astype(BF16), pc.astype(BF16)
                o = _dot(ppb, vp) + _dot(pcb, vc)
                z = z_ref[:, cs].astype(F32)
                sg = _sigmoid(z)
                dy = dy_ref[:, cs].astype(F32)
                do = dy * (z * sg)
                dz_ref[:, cs] = (dy * o * (sg * (1.0 + z * (1.0 - sg)))).astype(BF16)
                dob = do.astype(BF16)
                delta = jnp.sum(do * o, axis=1, keepdims=True)
                dsp = pp * (_dot_nt(dob, vp) - delta)
                dsc = pc * (_dot_nt(dob, vc) - delta)
                db_scr[hh, :, :TQ] += dsp
                db_scr[hh, :, TQ:] += dsc
                dspb, dscb = dsp.astype(BF16), dsc.astype(BF16)
                dq_ref[:, cs] = (scale * (_dot(dspb, kp) + _dot(dscb, kc))).astype(BF16)
                dk_ref[:, cs] = (ck_scr[:, cs] + scale * _dot_tn(dspb, q)).astype(BF16)
                dv_ref[:, cs] = (cv_scr[:, cs] + _dot_tn(ppb, dob)).astype(BF16)
                ck_scr[:, cs] = scale * _dot_tn(dscb, q)
                cv_scr[:, cs] = _dot_tn(pcb, dob)

        @pl.when(i == nq)
        def _():
            dk_ref[...] = ck_scr[...].astype(BF16)
            dv_ref[...] = cv_scr[...].astype(BF16)
            r0 = lax.broadcasted_iota(jnp.int32, (TQ, TQ), 0)
            r1 = lax.broadcasted_iota(jnp.int32, (TQ, TQ), 1)
            flip = (r0 + r1 == TQ - 1).astype(BF16)
            for hh in range(A_HPS):
                fl = _dot3(flip, db_scr[hh])
                rolled = pltpu.roll(fl, 0, 1, stride=1, stride_axis=0)
                du_ref[hh] = jnp.sum(rolled, axis=0, keepdims=True)

    specs = _band_specs(nq)
    row_spec = pl.BlockSpec((TQ, A_HW), lambda h, i: (jnp.minimum(i, nq - 1), h))
    key_spec = pl.BlockSpec((TQ, A_HW), lambda h, i: (jnp.maximum(i - 1, 0), h))
    out_sd = jax.ShapeDtypeStruct((S, A_WIDTH), BF16)
    return pl.pallas_call(
        body,
        name="band_bwd",
        grid=(A_HEADS // A_HPS, nq + 1),
        in_specs=[specs[0], specs[1], specs[2], specs[3], specs[4], specs[5], specs[6], row_spec],
        out_specs=[row_spec, key_spec, key_spec, row_spec, pl.BlockSpec((A_HPS, 1, 2 * TQ), lambda h, i: (h, 0, 0))],
        out_shape=[out_sd, out_sd, out_sd, out_sd, jax.ShapeDtypeStruct((A_HEADS, 1, 2 * TQ), F32)],
        scratch_shapes=[pltpu.VMEM((A_HPS, TQ, 2 * TQ), F32), pltpu.VMEM((A_HPS, TQ, 2 * TQ), F32),
                        pltpu.VMEM((TQ, A_HW), F32), pltpu.VMEM((TQ, A_HW), F32)],
        compiler_params=_params(("parallel", "arbitrary")),
    )(h, h, h, h, h, h, u, dycat)


def _bias_by_offset(table):
    far = jnp.broadcast_to(table[:, N_REL - 1:], (A_HEADS, TQ - MAX_REL))
    ramp = jnp.flip(table, axis=1)
    rest = jnp.broadcast_to(table[:, :1], (A_HEADS, 2 * TQ - CHUNK - (TQ + MAX_REL + 1)))
    wrap = jnp.broadcast_to(table[:, N_REL - 1:], (A_HEADS, CHUNK))
    return jnp.concatenate([far, ramp, rest, wrap], axis=1)[:, None, :]


def _bias_grad_from_offset(du):
    g = jnp.roll(du[:, 0, :], -(TQ - 1), axis=1)
    far = jnp.sum(g[:, :TQ - MAX_REL], axis=1) + jnp.sum(g[:, 2 * TQ - CHUNK:], axis=1)
    ramp = jnp.flip(g[:, TQ - MAX_REL:TQ + MAX_REL + 1], axis=1)
    return ramp.at[:, N_REL - 1].add(far)


def _mem_probs(q, mk):
    s = _dot_nt(q, mk) * (HEAD_DIM ** -0.5)
    p = jnp.exp(s - jnp.max(s, axis=1, keepdims=True))
    return p * (1.0 / jnp.sum(p, axis=1, keepdims=True))


def _mem_fwd(h, mkv):
    S = h.shape[0]
    nm = mkv.shape[0]

    def body(q_ref, z_ref, mk_ref, mv_ref, y_ref):
        p = _mem_probs(q_ref[...], mk_ref[...])
        o = _dot(p.astype(BF16), mv_ref[...])
        z = z_ref[...].astype(F32)
        y_ref[...] = (o * (z * _sigmoid(z))).astype(BF16)

    return pl.pallas_call(
        body,
        name="mem_fwd",
        grid=(M_HEADS, S // TQ),
        in_specs=[pl.BlockSpec((TQ, LANES), lambda h, i: (i, C_MQ // LANES + h)),
                  pl.BlockSpec((TQ, LANES), lambda h, i: (i, C_MZ // LANES + h)),
                  pl.BlockSpec((nm, LANES), lambda h, i: (0, h)),
                  pl.BlockSpec((nm, LANES), lambda h, i: (0, M_HEADS + h))],
        out_specs=pl.BlockSpec((TQ, LANES), lambda h, i: (i, h)),
        out_shape=jax.ShapeDtypeStruct((S, M_WIDTH), BF16),
        compiler_params=_params(("parallel", "arbitrary")),
    )(h, h, mkv, mkv)


def _mem_bwd(h, mkv, dycat):
    S = h.shape[0]
    nm = mkv.shape[0]
    scale = HEAD_DIM ** -0.5

    def body(q_ref, z_ref, mk_ref, mv_ref, dy_ref, dq_ref, dz_ref, dmk_ref, dmv_ref):
        i = pl.program_id(1)
        q, mk, mv = q_ref[...], mk_ref[...], mv_ref[...]
        p = _mem_probs(q, mk)
        pb = p.astype(BF16)
        o = _dot(pb, mv)
        z = z_ref[...].astype(F32)
        sg = _sigmoid(z)
        dy = dy_ref[...].astype(F32)
        do = dy * (z * sg)
        dz_ref[...] = (dy * o * (sg * (1.0 + z * (1.0 - sg)))).astype(BF16)
        dob = do.astype(BF16)
        ds = p * (_dot_nt(dob, mv) - jnp.sum(do * o, axis=1, keepdims=True))
        dsb = ds.astype(BF16)
        dq_ref[...] = (scale * _dot(dsb, mk)).astype(BF16)
        dmk = scale * _dot_tn(dsb, q)
        dmv = _dot_tn(pb, dob)

        @pl.when(i == 0)
        def _():
            dmk_ref[...] = dmk
            dmv_ref[...] = dmv

        @pl.when(i > 0)
        def _():
            dmk_ref[...] += dmk
            dmv_ref[...] += dmv

    row = pl.BlockSpec((TQ, LANES), lambda h, i: (i, h))
    out_sd = jax.ShapeDtypeStruct((S, M_WIDTH), BF16)
    dq, dz, dmkv, dmkv2 = pl.pallas_call(
        body,
        name="mem_bwd",
        grid=(M_HEADS, S // TQ),
        in_specs=[pl.BlockSpec((TQ, LANES), lambda h, i: (i, C_MQ // LANES + h)),
                  pl.BlockSpec((TQ, LANES), lambda h, i: (i, C_MZ // LANES + h)),
                  pl.BlockSpec((nm, LANES), lambda h, i: (0, h)),
                  pl.BlockSpec((nm, LANES), lambda h, i: (0, M_HEADS + h)),
                  pl.BlockSpec((TQ, LANES), lambda h, i: (i, (A_WIDTH + B_WIDTH) // LANES + h))],
        out_specs=[row, row,
                   pl.BlockSpec((nm, LANES), lambda h, i: (0, h)),
                   pl.BlockSpec((nm, LANES), lambda h, i: (0, h))],
        out_shape=[out_sd, out_sd, jax.ShapeDtypeStruct((nm, M_WIDTH), F32), jax.ShapeDtypeStruct((nm, M_WIDTH), F32)],
        compiler_params=_params(("parallel", "arbitrary")),
    )(h, h, mkv, mkv, dycat)
    return dq, dz, jnp.concatenate([dmkv, dmkv2], axis=1)


def _chunk_masks():
    r = lax.broadcasted_iota(jnp.int32, (TQ, TQ), 0)
    c = lax.broadcasted_iota(jnp.int32, (TQ, TQ), 1)
    same = _chunk_of(r) == _chunk_of(c)
    return jnp.logical_and(same, c <= r), jnp.logical_and(same, c > r)


def _gla_gates(lr, gw, gb):
    logit = _dot(lr, gw) + gb
    sg = _sigmoid(logit)
    g = (jnp.minimum(logit, 0.0) - jnp.log(1.0 + jnp.exp(-jnp.abs(logit)))) * (1.0 / GATE_TAU)
    lo, _ = _chunk_masks()
    return sg, _dot3(lo.astype(BF16), g)


def _gla_factors(q, k, b):
    eb = jnp.exp(b)
    enb = jnp.exp(-b)
    return eb, enb, q * eb, q * enb, k * eb, k * enb


def _gla_intra(qp, qn, kp, kn):
    lo, up = _chunk_masks()
    return (jnp.where(lo, _dot_nt(qp.astype(BF16), kn.astype(BF16)), 0.0)
            + jnp.where(up, _dot_nt(qn.astype(BF16), kp.astype(BF16)), 0.0))


def _gla_specs(nb, rev):
    blk = (lambda i: nb - 1 - i) if rev else (lambda i: i)

    def at(c0):
        return pl.BlockSpec((TQ, LANES), lambda i, h: (blk(i), c0 // LANES + h))

    lr_spec = pl.BlockSpec((TQ, LANES), lambda i, h: (blk(i), C_LR // LANES))
    gw_spec = pl.BlockSpec((LANES, LANES), lambda i, h: (0, h))
    gb_spec = pl.BlockSpec((1, LANES), lambda i, h: (0, h))
    gn_spec = pl.BlockSpec((1, LANES), lambda i, h: (0, 0))
    return at(C_BQ), at(C_BK), at(C_BV), at(C_BZ), lr_spec, gw_spec, gb_spec, gn_spec, blk


def _gla_fwd(h, gw, gb, gn):
    S = h.shape[0]
    nb = S // TQ

    def body(q_ref, k_ref, v_ref, z_ref, lr_ref, gw_ref, gb_ref, gn_ref, y_ref, o_ref, st_ref, st_scr):
        i, hd = pl.program_id(0), pl.program_id(1)

        @pl.when(i == 0)
        def _():
            st_scr[hd] = jnp.zeros((LANES, LANES), F32)

        q = q_ref[...].astype(F32) * (B_DK ** -0.5)
        k = k_ref[...].astype(F32)
        v = v_ref[...]
        _, b = _gla_gates(lr_ref[...], gw_ref[...], gb_ref[...])
        _, _, qp, qn, kp, kn = _gla_factors(q, k, b)
        o_intra = _dot(_gla_intra(qp, qn, kp, kn).astype(BF16), v)
        qpb, knb = qp.astype(BF16), kn.astype(BF16)
        st = st_scr[hd]
        outs = []
        for c in range(CPB):
            rows = slice(c * CHUNK, (c + 1) * CHUNK)
            st_ref[0, c] = st
            outs.append(_dot_nt(qpb[rows], st.astype(BF16)))
            e_last = jnp.exp(b[(c + 1) * CHUNK - 1:(c + 1) * CHUNK, :])
            st = (st + _dot_tn(v[rows], knb[rows])) * e_last
        st_scr[hd] = st
        o = o_intra + jnp.concatenate(outs, axis=0)
        o_ref[...] = o
        r = lax.rsqrt(jnp.mean(o * o, axis=1, keepdims=True) + RMS_EPS)
        z = z_ref[...].astype(F32)
        y_ref[...] = (o * r * gn_ref[...] * (z * _sigmoid(z))).astype(BF16)

    q_s, k_s, v_s, z_s, lr_s, gw_s, gb_s, gn_s, _ = _gla_specs(nb, False)
    row = pl.BlockSpec((TQ, LANES), lambda i, h: (i, h))
    return pl.pallas_call(
        body,
        name="gla_fwd",
        grid=(nb, B_HEADS),
        in_specs=[q_s, k_s, v_s, z_s, lr_s, gw_s, gb_s, gn_s],
        out_specs=[row, row, pl.BlockSpec((1, CPB, LANES, LANES), lambda i, h: (h, i, 0, 0))],
        out_shape=[jax.ShapeDtypeStruct((S, B_WIDTH), BF16), jax.ShapeDtypeStruct((S, B_WIDTH), F32),
                   jax.ShapeDtypeStruct((B_HEADS, S // CHUNK, LANES, LANES), F32)],
        scratch_shapes=[pltpu.VMEM((B_HEADS, LANES, LANES), F32)],
        compiler_params=_params(("arbitrary", "arbitrary")),
    )(h, h, h, h, h, gw, gb, gn)


def _gla_bwd(h, gw, gb, gn, o_pre, states, dycat):
    S = h.shape[0]
    nb = S // TQ

    def body(q_ref, k_ref, v_ref, z_ref, lr_ref, gw_ref, gb_ref, gn_ref, o_ref, st_ref, dy_ref,
             dq_ref, dk_ref, dv_ref, dz_ref, dlr_ref, dgw_ref, dgb_ref, dgn_ref,
             dst_scr, dgw_scr, dgb_scr, dgn_scr):
        i, hd = pl.program_id(0), pl.program_id(1)

        @pl.when(i == 0)
        def _():
            dst_scr[hd] = jnp.zeros((LANES, LANES), F32)
            dgw_scr[hd] = jnp.zeros((LANES, LANES), F32)
            dgb_scr[hd] = jnp.zeros((1, LANES), F32)

        @pl.when(jnp.logical_and(i == 0, hd == 0))
        def _():
            dgn_scr[...] = jnp.zeros_like(dgn_scr)

        q = q_ref[...].astype(F32) * (B_DK ** -0.5)
        k = k_ref[...].astype(F32)
        v = v_ref[...]
        lr, gwv = lr_ref[...], gw_ref[...]
        sg, b = _gla_gates(lr, gwv, gb_ref[...])
        eb, enb, qp, qn, kp, kn = _gla_factors(q, k, b)
        a = _gla_intra(qp, qn, kp, kn)
        qpb, qnb, kpb, knb = qp.astype(BF16), qn.astype(BF16), kp.astype(BF16), kn.astype(BF16)

        o = o_ref[...]
        gn = gn_ref[...]
        r = lax.rsqrt(jnp.mean(o * o, axis=1, keepdims=True) + RMS_EPS)
        z = z_ref[...].astype(F32)
        sz = _sigmoid(z)
        dy = dy_ref[...].astype(F32)
        d_on = dy * (z * sz)
        dz_ref[...] = (dy * (o * r * gn) * (sz * (1.0 + z * (1.0 - sz)))).astype(BF16)
        dgn_scr[...] += jnp.sum(d_on * o * r, axis=0, keepdims=True)
        t = d_on * gn
        do = r * t - o * (r * r * r) * jnp.mean(t * o, axis=1, keepdims=True)
        dob = do.astype(BF16)

        lo, up = _chunk_masks()
        da = _dot_nt(dob, v)
        dalo = jnp.where(lo, da, 0.0).astype(BF16)
        daup = jnp.where(up, da, 0.0).astype(BF16)
        dqp = _dot(dalo, knb)
        dkn = _dot_tn(dalo, qpb)
        dqn = _dot(daup, kpb)
        dkp = _dot_tn(daup, qnb)
        dv = _dot_tn(a.astype(BF16), dob)

        dst = dst_scr[hd]
        dqp_c, dkn_c, dv_c, dbl_c = [None] * CPB, [None] * CPB, [None] * CPB, [None] * CPB
        for c in reversed(range(CPB)):
            rows = slice(c * CHUNK, (c + 1) * CHUNK)
            st = st_ref[0, c]
            e_last = jnp.exp(b[(c + 1) * CHUNK - 1:(c + 1) * CHUNK, :])
            if c == CPB - 1:
                st_next = (st + _dot_tn(v[rows], knb[rows])) * e_last
            else:
                st_next = st_ref[0, c + 1]
            dbl_c[c] = jnp.sum(dst * st_next, axis=0, keepdims=True)
            dtt = (dst * e_last).astype(BF16)
            dv_c[c] = _dot_nt(knb[rows], dtt)
            dkn_c[c] = _dot(v[rows], dtt)
            dqp_c[c] = _dot(dob[rows], st.astype(BF16))
            dst = _dot_tn(dob[rows], qpb[rows]) + dst * e_last
        dst_scr[hd] = dst
        dqp = dqp + jnp.concatenate(dqp_c, axis=0)
        dkn = dkn + jnp.concatenate(dkn_c, axis=0)
        dv = dv + jnp.concatenate(dv_c, axis=0)
        dv_ref[...] = dv.astype(BF16)
        dq_ref[...] = ((dqp * eb + dqn * enb) * (B_DK ** -0.5)).astype(BF16)
        dk_ref[...] = (dkp * eb + dkn * enb).astype(BF16)

        last = jnp.bitwise_and(lax.broadcasted_iota(jnp.int32, (TQ, 1), 0), CHUNK - 1) == CHUNK - 1
        dbl = jnp.concatenate([jnp.broadcast_to(x, (CHUNK, LANES)) for x in dbl_c], axis=0)
        db = dqp * qp - dqn * qn + dkp * kp - dkn * kn + jnp.where(last, dbl, 0.0)
        r0 = lax.broadcasted_iota(jnp.int32, (TQ, TQ), 0)
        r1 = lax.broadcasted_iota(jnp.int32, (TQ, TQ), 1)
        upper = jnp.logical_and(_chunk_of(r0) == _chunk_of(r1), r1 >= r0).astype(BF16)
        dlogit = _dot3(upper, db) * (1.0 / GATE_TAU) * (1.0 - sg)
        dlb = dlogit.astype(BF16)
        dlr = _dot_nt(dlb, gwv)

        @pl.when(hd == 0)
        def _():
            dlr_ref[...] = dlr

        @pl.when(hd > 0)
        def _():
            dlr_ref[...] += dlr

        dgw_scr[hd] += _dot_tn(lr, dlb)
        dgb_scr[hd] += jnp.sum(dlogit, axis=0, keepdims=True)

        @pl.when(i == nb - 1)
        def _():
            dgw_ref[...] = dgw_scr[hd]
            dgb_ref[...] = dgb_scr[hd]
            dgn_ref[...] = dgn_scr[...]

    q_s, k_s, v_s, z_s, lr_s, gw_s, gb_s, gn_s, blk = _gla_specs(nb, True)
    row = pl.BlockSpec((TQ, LANES), lambda i, h: (blk(i), h))
    dy_spec = pl.BlockSpec((TQ, LANES), lambda i, h: (blk(i), A_WIDTH // LANES + h))
    st_spec = pl.BlockSpec((1, CPB, LANES, LANES), lambda i, h: (h, blk(i), 0, 0))
    out_sd = jax.ShapeDtypeStruct((S, B_WIDTH), BF16)
    return pl.pallas_call(
        body,
        name="gla_bwd",
        grid=(nb, B_HEADS),
        in_specs=[q_s, k_s, v_s, z_s, lr_s, gw_s, gb_s, gn_s, row, st_spec, dy_spec],
        out_specs=[row, row, row, row,
                   pl.BlockSpec((TQ, LANES), lambda i, h: (blk(i), 0)),
                   pl.BlockSpec((LANES, LANES), lambda i, h: (0, jnp.where(i == nb - 1, h, 0))),
                   pl.BlockSpec((1, LANES), lambda i, h: (0, jnp.where(i == nb - 1, h, 0))),
                   pl.BlockSpec((1, LANES), lambda i, h: (0, 0))],
        out_shape=[out_sd, out_sd, out_sd, out_sd,
                   jax.ShapeDtypeStruct((S, LANES), F32),
                   jax.ShapeDtypeStruct((LANES, B_HEADS * LANES), F32),
                   jax.ShapeDtypeStruct((1, B_HEADS * LANES), F32),
                   jax.ShapeDtypeStruct((1, LANES), F32)],
        scratch_shapes=[pltpu.VMEM((B_HEADS, LANES, LANES), F32), pltpu.VMEM((B_HEADS, LANES, LANES), F32),
                        pltpu.VMEM((B_HEADS, 1, LANES), F32), pltpu.VMEM((1, LANES), F32)],
        compiler_params=_params(("arbitrary", "arbitrary")),
    )(h, h, h, h, h, gw, gb, gn, o_pre, states, dycat)


LN_ROWS = 256


def _outproj_ln(ycat, w_out, x, g, b):
    S = x.shape[0]

    def body(yc_ref, w_ref, x_ref, g_ref, b_ref, y_ref, yb_ref, xh_ref, rs_ref):
        u = ALPHA * x_ref[...] + _dot(yc_ref[...], w_ref[...])
        mu = jnp.mean(u, axis=1, keepdims=True)
        d = u - mu
        rstd = lax.rsqrt(jnp.mean(d * d, axis=1, keepdims=True) + LN_EPS)
        xh = d * rstd
        y = xh * g_ref[...] + b_ref[...]
        y_ref[...] = y
        yb_ref[...] = y.astype(BF16)
        xh_ref[...] = xh
        rs_ref[...] = rstd

    row = lambda w: pl.BlockSpec((LN_ROWS, w), lambda i: (i, 0))
    vec = pl.BlockSpec((1, D_MODEL), lambda i: (0, 0))
    return pl.pallas_call(
        body,
        name="outproj_ln",
        grid=(S // LN_ROWS,),
        in_specs=[row(D_MODEL), pl.BlockSpec((D_MODEL, D_MODEL), lambda i: (0, 0)), row(D_MODEL), vec, vec],
        out_specs=[row(D_MODEL), row(D_MODEL), row(D_MODEL), row(1)],
        out_shape=[jax.ShapeDtypeStruct((S, D_MODEL), F32), jax.ShapeDtypeStruct((S, D_MODEL), BF16),
                   jax.ShapeDtypeStruct((S, D_MODEL), F32), jax.ShapeDtypeStruct((S, 1), F32)],
        compiler_params=_params(("parallel",)),
    )(ycat, w_out, x, g, b)


def _ln_bwd(dy, xhat, rstd, g):
    S = dy.shape[0]

    def body(dy_ref, xh_ref, rs_ref, g_ref, du_ref, dub_ref, dg_ref, db_ref):
        i = pl.program_id(0)
        dy_, xh = dy_ref[...], xh_ref[...]
        dyg = dy_ * g_ref[...]
        m1 = jnp.mean(dyg, axis=1, keepdims=True)
        m2 = jnp.mean(dyg * xh, axis=1, keepdims=True)
        du = rs_ref[...] * (dyg - m1 - xh * m2)
        du_ref[...] = du
        dub_ref[...] = du.astype(BF16)
        dg = jnp.sum(dy_ * xh, axis=0, keepdims=True)
        db = jnp.sum(dy_, axis=0, keepdims=True)

        @pl.when(i == 0)
        def _():
            dg_ref[...] = dg
            db_ref[...] = db

        @pl.when(i > 0)
        def _():
            dg_ref[...] += dg
            db_ref[...] += db

    row = lambda w: pl.BlockSpec((TQ, w), lambda i: (i, 0))
    vec = pl.BlockSpec((1, D_MODEL), lambda i: (0, 0))
    return pl.pallas_call(
        body,
        name="ln_bwd",
        grid=(S // TQ,),
        in_specs=[row(D_MODEL), row(D_MODEL), row(1), vec],
        out_specs=[row(D_MODEL), row(D_MODEL), vec, vec],
        out_shape=[jax.ShapeDtypeStruct((S, D_MODEL), F32), jax.ShapeDtypeStruct((S, D_MODEL), BF16),
                   jax.ShapeDtypeStruct((1, D_MODEL), F32), jax.ShapeDtypeStruct((1, D_MODEL), F32)],
        compiler_params=_params(("arbitrary",)),
    )(dy, xhat, rstd, g)


def _loss_head(y, target):
    S = y.shape[0]

    def body(y_ref, t_ref, l_ref, dy_ref):
        i = pl.program_id(0)
        err = y_ref[...] - t_ref[...]
        dy_ref[...] = err * (1.0 / D_MODEL)
        part = (0.5 / D_MODEL) * jnp.sum(jnp.sum(err * err, axis=1, keepdims=True), axis=0, keepdims=True)

        @pl.when(i == 0)
        def _():
            l_ref[...] = part

        @pl.when(i > 0)
        def _():
            l_ref[...] += part

    row = pl.BlockSpec((TQ, D_MODEL), lambda i: (i, 0))
    return pl.pallas_call(
        body,
        name="loss_head",
        grid=(S // TQ,),
        in_specs=[row, row],
        out_specs=[pl.BlockSpec((1, 1), lambda i: (0, 0)), row],
        out_shape=[jax.ShapeDtypeStruct((1, 1), F32), jax.ShapeDtypeStruct((S, D_MODEL), F32)],
        compiler_params=_params(("arbitrary",)),
    )(y, target)


def _pad_gate(gate_w, gate_b):
    gw = gate_w.reshape(GATE_RANK, B_HEADS, B_DK)
    gw = jnp.pad(gw, ((0, LANES - GATE_RANK), (0, 0), (0, LANES - B_DK))).reshape(LANES, B_HEADS * LANES)
    gb = jnp.pad(gate_b.reshape(B_HEADS, B_DK), ((0, 0), (0, LANES - B_DK))).reshape(1, B_HEADS * LANES)
    return gw.astype(BF16), gb.astype(F32)


def _layer_fwd(x, xb, mem_b, w_in, w_kv, w_out, u, gw, gb, gn, ln_g, ln_b):
    h = _matmul(xb, w_in, mode="nn", out_dtype=BF16, tm=1024, tn=768, tk=D_MODEL, name="in_proj")
    mkv = _matmul(mem_b, w_kv, mode="nn", out_dtype=BF16, tm=mem_b.shape[0], tn=1024, tk=D_MODEL, name="mem_kv")
    ya = _band_fwd(h, u)
    yb_, o_pre, states = _gla_fwd(h, gw, gb, gn)
    ym = _mem_fwd(h, mkv)
    ycat = jnp.concatenate([ya, yb_, ym], axis=1)
    y, ybf, xhat, rstd = _outproj_ln(ycat, w_out, x, ln_g, ln_b)
    return y, ybf, (xb, h, mkv, ycat, o_pre, states, xhat, rstd)


def _layer_bwd(dy, saved, mem_b, w_in, w_out, u, gw, gb, gn, ln_g):
    xb, h, mkv, ycat, o_pre, states, xhat, rstd = saved
    S = dy.shape[0]
    du, dub, d_ln_g, d_ln_b = _ln_bwd(dy, xhat, rstd, ln_g)
    dycat = _matmul(dub, w_out, mode="nt", out_dtype=BF16, tm=1024, tn=1024, tk=D_MODEL, name="dycat")
    d_w_out = _matmul(_transpose(ycat, "ycat_t"), dub, mode="nn", out_dtype=F32, tm=1024, tn=1024, tk=2048, name="d_w_out")
    daq, dak, dav, daz, d_u = _band_bwd(h, u, dycat)
    dbq, dbk, dbv, dbz, dlr, dgw, dgb, dgn = _gla_bwd(h, gw, gb, gn, o_pre, states, dycat)
    dmq, dmz, dmkv = _mem_bwd(h, mkv, dycat)
    d_w_kv = _matmul(mem_b, dmkv, mode="tn", out_dtype=F32, tm=1024, tn=1024, tk=mem_b.shape[0], name="d_w_kv")
    dh = jnp.concatenate([daq, dak, dav, daz, dbq, dbk, dbv, dbz, dmq, dmz, dlr.astype(BF16),
                          jnp.zeros((S, HP - C_LR - LANES), BF16)], axis=1)
    dx = _matmul(dh, w_in, mode="nt", out_dtype=F32, tm=1024, tn=1024, tk=2560, name="dx", add=du, add_scale=ALPHA)
    d_w_in = _matmul(_transpose(xb, "x_t"), dh, mode="nn", out_dtype=F32, tm=1024, tn=768, tk=2048, name="d_w_in")
    return dx, (d_w_in, d_u, dgw, dgb, dgn, d_w_kv, d_w_out, d_ln_g, d_ln_b)


def _pad_heads(w):
    r = w.shape[0]
    return jnp.pad(w.reshape(r, B_HEADS, B_DK), ((0, 0), (0, 0), (0, LANES - B_DK))).reshape(r, B_HEADS * LANES)


def _unpad_heads(w):
    r = w.shape[0]
    return w.reshape(r, B_HEADS, LANES)[:, :, :B_DK].reshape(r, B_KEY_WIDTH)


O_BQ = 4 * A_WIDTH
O_BK = O_BQ + B_KEY_WIDTH
O_BV = O_BK + B_KEY_WIDTH
O_LR = O_BV + 2 * B_WIDTH
O_MQ = O_LR + GATE_RANK


def _to_padded(w):
    r = w.shape[0]
    return jnp.concatenate([
        w[:, :O_BQ], _pad_heads(w[:, O_BQ:O_BK]), _pad_heads(w[:, O_BK:O_BV]), w[:, O_BV:O_LR], w[:, O_MQ:],
        w[:, O_LR:O_MQ], jnp.zeros((r, HP - C_LR - GATE_RANK), w.dtype)], axis=1)


def _from_padded(g):
    return jnp.concatenate([
        g[:, :C_BQ], _unpad_heads(g[:, C_BQ:C_BK]), _unpad_heads(g[:, C_BK:C_BV]), g[:, C_BV:C_MQ],
        g[:, C_LR:C_LR + GATE_RANK], g[:, C_MQ:C_LR]], axis=1)


def _adamw(w, g, m, v, name):
    R, C = w.shape
    tr = R
    for cand in (256, 128, 64, 32, 16, 8):
        if R % cand == 0 and R > cand:
            tr = cand
            break

    def body(w_ref, g_ref, m_ref, v_ref, d_ref, nm_ref, nv_ref):
        g_ = g_ref[...]
        nm = ADAM_B1 * m_ref[...] + (1.0 - ADAM_B1) * g_
        nv = ADAM_B2 * v_ref[...] + (1.0 - ADAM_B2) * (g_ * g_)
        m_hat = nm / (1.0 - ADAM_B1 ** ADAM_STEP)
        v_hat = nv / (1.0 - ADAM_B2 ** ADAM_STEP)
        d_ref[...] = -ADAM_LR * (m_hat / (jnp.sqrt(v_hat) + ADAM_EPS) + ADAM_WD * w_ref[...])
        nm_ref[...] = nm
        nv_ref[...] = nv

    spec = pl.BlockSpec((tr, C), lambda i: (i, 0))
    sd = jax.ShapeDtypeStruct((R, C), F32)
    return pl.pallas_call(
        body, name=name, grid=(R // tr,), in_specs=[spec] * 4, out_specs=[spec] * 3, out_shape=[sd] * 3,
        compiler_params=_params(("parallel",)),
    )(w, g, m, v)


def _adamw_nd(w, g, m, v, name):
    shape = w.shape
    f = lambda a: a.reshape(-1, shape[-1])
    return tuple(o.reshape(shape) for o in _adamw(f(w), f(g), f(m), f(v), name))


ANY = pl.BlockSpec(memory_space=pl.ANY)


def _place():
    x, y, c = lax.axis_index("x"), lax.axis_index("y"), lax.axis_index("c")
    chips = [(1 - x, y), (x, 1 - y), (1 - x, 1 - y)]
    return x, y, c, chips


def _gather_chips(shard, name):
    R, C = shard.shape
    half = R // 2
    assert half % 16 == 0

    def body(s_ref, o_ref, send, recv):
        x, y, c, chips = _place()
        mine = pl.ds(pl.multiple_of(c * half, 16), half)
        other = pl.ds(pl.multiple_of((1 - c) * half, 16), half)

        def cp(k, src, chip, rows, to):
            dst = o_ref.at[2 * chip[0] + chip[1], rows]
            return pltpu.make_async_remote_copy(src_ref=dst if src is None else src, dst_ref=dst, send_sem=send.at[k],
                                                recv_sem=recv.at[k], device_id=to, device_id_type=MESH)

        first = [cp(k, s_ref.at[mine], (x, y), mine, (*chip, c)) for k, chip in enumerate(chips)]
        for f in first:
            f.start()
        passed = [cp(3 + k, None, chip, mine, (x, y, 1 - c)) for k, chip in enumerate(chips)]
        for k, chip in enumerate(chips):
            cp(k, None, chip, mine, (x, y, c)).wait_recv()
            passed[k].start()
        for k, chip in enumerate(chips):
            cp(3 + k, None, chip, other, (x, y, c)).wait_recv()
        for f in first + passed:
            f.wait_send()

    return pl.pallas_call(
        body, name=name, in_specs=[ANY], out_specs=ANY,
        out_shape=jax.ShapeDtypeStruct((N_CHIPS, R, C), shard.dtype),
        scratch_shapes=[pltpu.SemaphoreType.DMA((6,)), pltpu.SemaphoreType.DMA((6,))],
    )(shard)


def _pair_exchange(g4, name):
    n, _, R, C = g4.shape

    def body(g_ref, o_ref, send, recv):
        x, y, c, _ = _place()
        cps = [pltpu.make_async_remote_copy(src_ref=g_ref.at[j, 1 - c], dst_ref=o_ref.at[j], send_sem=send.at[j],
                                            recv_sem=recv.at[j], device_id=(x, y, 1 - c), device_id_type=MESH)
               for j in range(n)]
        for cp in cps:
            cp.start()
        for cp in cps:
            cp.wait()

    return pl.pallas_call(
        body, name=name, in_specs=[ANY], out_specs=ANY, out_shape=jax.ShapeDtypeStruct((n, R, C), g4.dtype),
        scratch_shapes=[pltpu.SemaphoreType.DMA((n,)), pltpu.SemaphoreType.DMA((n,))],
    )(g4)


def _chip_exchange(p, name):
    n, R, C = p.shape

    def body(p_ref, o_ref, send, recv):
        x, y, c, chips = _place()
        me = 2 * x + y
        cps = [pltpu.make_async_remote_copy(src_ref=p_ref.at[2 * chip[0] + chip[1]], dst_ref=o_ref.at[me],
                                            send_sem=send.at[k], recv_sem=recv.at[k], device_id=(*chip, c),
                                            device_id_type=MESH) for k, chip in enumerate(chips)]
        for cp in cps:
            cp.start()
        for k, chip in enumerate(chips):
            pltpu.make_async_remote_copy(src_ref=p_ref.at[me], dst_ref=o_ref.at[2 * chip[0] + chip[1]],
                                         send_sem=send.at[k], recv_sem=recv.at[k], device_id=(*chip, c),
                                         device_id_type=MESH).wait_recv()
        for cp in cps:
            cp.wait_send()

    return pl.pallas_call(
        body, name=name, in_specs=[ANY], out_specs=ANY, out_shape=jax.ShapeDtypeStruct((n, R, C), p.dtype),
        scratch_shapes=[pltpu.SemaphoreType.DMA((3,)), pltpu.SemaphoreType.DMA((3,))],
    )(p)


def _pair_gather(t2, name):
    def body(t_ref, o_ref, send, recv):
        del t_ref
        x, y, c, _ = _place()
        cp = pltpu.make_async_remote_copy(src_ref=o_ref.at[c], dst_ref=o_ref.at[c], send_sem=send, recv_sem=recv,
                                          device_id=(x, y, 1 - c), device_id_type=MESH)
        cp.start()
        pltpu.make_async_remote_copy(src_ref=o_ref.at[c], dst_ref=o_ref.at[1 - c], send_sem=send, recv_sem=recv,
                                     device_id=(x, y, 1 - c), device_id_type=MESH).wait_recv()
        cp.wait_send()

    return pl.pallas_call(
        body, name=name, in_specs=[ANY], out_specs=ANY, out_shape=jax.ShapeDtypeStruct(t2.shape, t2.dtype),
        input_output_aliases={0: 0},
        scratch_shapes=[pltpu.SemaphoreType.DMA, pltpu.SemaphoreType.DMA],
    )(t2)


def _add_halves(g4, recv, c_idx, name):
    n, _, R, C = g4.shape
    tr = 256

    def body(c_ref, a_ref, b_ref, o_ref):
        o_ref[...] = (a_ref[0] + b_ref[...]).astype(BF16)

    return pl.pallas_call(
        body, name=name,
        grid_spec=pltpu.PrefetchScalarGridSpec(
            num_scalar_prefetch=1, grid=(n, R // tr),
            in_specs=[pl.BlockSpec((1, 1, tr, C), lambda j, i, c: (j, c[0], i, 0)),
                      pl.BlockSpec((1, tr, C), lambda j, i, c: (j, i, 0))],
            out_specs=pl.BlockSpec((1, tr, C), lambda j, i, c: (j, i, 0))),
        out_shape=jax.ShapeDtypeStruct((n, R, C), BF16),
        compiler_params=_params(("parallel", "parallel")),
    )(c_idx, g4, recv)


def _add_slots(r, c_idx, name):
    n, R, C = r.shape
    tr = 256

    def body(c_ref, r_ref, o_ref):
        acc = r_ref[0].astype(F32)
        for j in range(1, n):
            acc = acc + r_ref[j].astype(F32)
        o_ref[0] = acc

    return pl.pallas_call(
        body, name=name,
        grid_spec=pltpu.PrefetchScalarGridSpec(
            num_scalar_prefetch=1, grid=(R // tr,),
            in_specs=[pl.BlockSpec((n, tr, C), lambda i, c: (0, i, 0))],
            out_specs=pl.BlockSpec((1, tr, C), lambda i, c: (c[0], i, 0))),
        out_shape=jax.ShapeDtypeStruct((2, R, C), F32),
        compiler_params=_params(("parallel",)),
    )(c_idx, r)


def _reduce_scatter(g4, c_idx, chip, tag):
    got = _pair_exchange(g4, "rs_pair_" + tag)
    p = _add_halves(g4, got, c_idx, "rs_add2_" + tag)
    q = _chip_exchange(p, "rs_chip_" + tag)
    q = lax.dynamic_update_slice_in_dim(q, lax.dynamic_slice_in_dim(p, chip, 1, axis=0), chip, axis=0)
    t2 = _add_slots(q, c_idx, "rs_add4_" + tag)
    return _pair_gather(t2, "rs_gather_" + tag)


def _all_reduce_small(buf, name):
    R = buf.shape[0]

    def flipped(k, x, y, c):
        return ((1 - x) if k & 4 else x, (1 - y) if k & 2 else y, (1 - c) if k & 1 else c)

    def body(b_ref, o_ref, land, send, recv):
        x, y, c, _ = _place()
        me = 4 * x + 2 * y + c
        land[me] = b_ref[...]
        cps = []
        for k in range(1, N_DEV):
            peer = flipped(k, x, y, c)
            cps.append(pltpu.make_async_remote_copy(src_ref=b_ref, dst_ref=land.at[me], send_sem=send.at[k - 1],
                                                    recv_sem=recv.at[k - 1], device_id=peer, device_id_type=MESH))
        for cp in cps:
            cp.start()
        for k in range(1, N_DEV):
            peer = flipped(k, x, y, c)
            slot = 4 * peer[0] + 2 * peer[1] + peer[2]
            pltpu.make_async_remote_copy(src_ref=b_ref, dst_ref=land.at[slot], send_sem=send.at[k - 1],
                                         recv_sem=recv.at[k - 1], device_id=peer, device_id_type=MESH).wait_recv()
        for cp in cps:
            cp.wait_send()
        acc = land[0]
        for j in range(1, N_DEV):
            acc = acc + land[j]
        o_ref[...] = acc

    vm = pl.BlockSpec(memory_space=pltpu.VMEM)
    return pl.pallas_call(
        body, name=name, in_specs=[vm], out_specs=vm,
        out_shape=jax.ShapeDtypeStruct((R, LANES), F32),
        scratch_shapes=[pltpu.VMEM((N_DEV, R, LANES), F32), pltpu.SemaphoreType.DMA((N_DEV - 1,)),
                        pltpu.SemaphoreType.DMA((N_DEV - 1,))],
    )(buf)


def _by_chip_and_half(g, axis):
    L = g.shape[0]
    if axis == 2:
        n = g.shape[2] // N_CHIPS
        t = g.reshape(L, g.shape[1], N_CHIPS, n).transpose(2, 0, 1, 3)
    else:
        n = g.shape[1] // N_CHIPS
        t = g.reshape(L, N_CHIPS, n, g.shape[2]).transpose(1, 0, 2, 3)
    return t.reshape(N_CHIPS, 2, (L // 2) * t.shape[2], t.shape[3])


def kernel(x, mem, w_in, a_rel_bias, b_gate_w, b_gate_b, b_norm_g, w_mem_kv, w_out, ln_g, ln_b, loss_target, m_w_in, m_a_rel_bias, m_b_gate_w, m_b_gate_b, m_b_norm_g, m_w_mem_kv, m_w_out, m_ln_g, m_ln_b, v_w_in, v_a_rel_bias, v_b_gate_w, v_b_gate_b, v_b_norm_g, v_w_mem_kv, v_w_out, v_ln_g, v_ln_b):
    L = w_in.shape[0]
    S = x.shape[1]
    cx, cy, cc = lax.axis_index("x"), lax.axis_index("y"), lax.axis_index("c")
    chip = 2 * cx + cy
    c_idx = jnp.reshape(cc, (1,)).astype(jnp.int32)

    def gathered(w, name):
        shard = w.astype(BF16).reshape(-1, w.shape[2])
        return lax.dynamic_update_slice_in_dim(_gather_chips(shard, name), shard[None], chip, axis=0)

    n_in = w_in.shape[2]
    win_all = gathered(w_in, "gather_w_in")
    r_kv = w_mem_kv.shape[1]
    wkv_all = gathered(w_mem_kv, "gather_w_kv")
    r_out = w_out.shape[1]
    wout_all = gathered(w_out, "gather_w_out")
    win_all = win_all.reshape(N_CHIPS, L, D_MODEL, n_in)
    wkv_all = wkv_all.reshape(N_CHIPS, L, r_kv, w_mem_kv.shape[2])
    wout_all = wout_all.reshape(N_CHIPS, L, r_out, D_MODEL)

    gw_cols = b_gate_w.shape[2]
    gw_slot = jnp.zeros((N_CHIPS, L, GATE_RANK, gw_cols), F32)
    gw_slot = lax.dynamic_update_slice(gw_slot, (0.5 * b_gate_w)[None], (chip, 0, 0, 0))
    gw_flat = gw_slot.reshape(-1)
    n_gw = gw_flat.shape[0]
    pad = (-n_gw) % (8 * LANES)
    gw_full = _all_reduce_small(jnp.pad(gw_flat, (0, pad)).reshape(-1, LANES), "gather_gate_w").reshape(-1)[:n_gw]
    gw_full = gw_full.reshape(N_CHIPS, L, GATE_RANK, gw_cols).transpose(1, 2, 0, 3).reshape(L, GATE_RANK, B_KEY_WIDTH)

    xs = x.reshape(S, D_MODEL)
    mem_b = mem.reshape(mem.shape[1], D_MODEL).astype(BF16)
    target = loss_target.reshape(S, D_MODEL)

    layer_w = []
    for l in range(L):
        w_in_l = _to_padded(jnp.concatenate([win_all[j, l] for j in range(N_CHIPS)], axis=1))
        w_kv_l = jnp.concatenate([wkv_all[j, l] for j in range(N_CHIPS)], axis=0)
        w_out_l = jnp.concatenate([wout_all[j, l] for j in range(N_CHIPS)], axis=0)
        gw_l, gb_l = _pad_gate(gw_full[l], b_gate_b[l])
        layer_w.append((w_in_l, w_kv_l, w_out_l, _bias_by_offset(a_rel_bias[l]), gw_l, gb_l,
                        b_norm_g[l].reshape(1, LANES), ln_g[l].reshape(1, D_MODEL), ln_b[l].reshape(1, D_MODEL)))

    y, yb = xs, xs.astype(BF16)
    saved = []
    for l in range(L):
        w_in_l, w_kv_l, w_out_l, u_l, gw_l, gb_l, gn_l, lg_l, lb_l = layer_w[l]
        y, yb, sv = _layer_fwd(y, yb, mem_b, w_in_l, w_kv_l, w_out_l, u_l, gw_l, gb_l, gn_l, lg_l, lb_l)
        saved.append(sv)
    loss_part, dy = _loss_head(y, target)

    grads = [None] * L
    for l in reversed(range(L)):
        w_in_l, w_kv_l, w_out_l, u_l, gw_l, gb_l, gn_l, lg_l, lb_l = layer_w[l]
        dy, grads[l] = _layer_bwd(dy, saved[l], mem_b, w_in_l, w_out_l, u_l, gw_l, gb_l, gn_l, lg_l)
    grad_x = dy.reshape(x.shape)

    g_w_in = jnp.stack([_from_padded(g[0]) for g in grads])
    g_rel = jnp.stack([_bias_grad_from_offset(g[1]) for g in grads])
    g_gw = jnp.stack([_unpad_heads(g[2][:GATE_RANK]) for g in grads])
    g_gb = jnp.stack([_unpad_heads(g[3])[0] for g in grads])
    g_gn = jnp.stack([g[4][0] for g in grads])
    g_w_kv = jnp.stack([g[5] for g in grads])
    g_w_out = jnp.stack([g[6] for g in grads])
    g_lg = jnp.stack([g[7][0] for g in grads])
    g_lb = jnp.stack([g[8][0] for g in grads])

    r_w_in = _reduce_scatter(_by_chip_and_half(g_w_in, 2), c_idx, chip, "w_in").reshape(L, D_MODEL, n_in)
    r_w_kv = _reduce_scatter(_by_chip_and_half(g_w_kv, 1), c_idx, chip, "w_kv").reshape(L, r_kv, w_mem_kv.shape[2])
    r_w_out = _reduce_scatter(_by_chip_and_half(g_w_out, 1), c_idx, chip, "w_out").reshape(L, r_out, D_MODEL)

    small = [g_rel, g_gw, g_gb, g_gn, g_lg, g_lb, loss_part]
    flat = jnp.concatenate([s.reshape(-1) for s in small])
    n_small = flat.shape[0]
    pad = (-n_small) % (8 * LANES)
    red = _all_reduce_small(jnp.pad(flat, (0, pad)).reshape(-1, LANES), "all_reduce_small").reshape(-1)
    outs, off = [], 0
    for s in small:
        outs.append(red[off:off + s.size].reshape(s.shape))
        off += s.size
    g_rel, g_gw, g_gb, g_gn, g_lg, g_lb, loss = outs
    loss = loss.reshape(())
    g_gw = lax.dynamic_slice_in_dim(g_gw.reshape(L, GATE_RANK, N_CHIPS, gw_cols), chip, 1, axis=2).reshape(L, GATE_RANK, gw_cols)

    g_list = [r_w_in, g_rel, g_gw, g_gb, g_gn, r_w_kv, r_w_out, g_lg, g_lb]
    w_list = [w_in, a_rel_bias, b_gate_w, b_gate_b, b_norm_g, w_mem_kv, w_out, ln_g, ln_b]
    m_list = [m_w_in, m_a_rel_bias, m_b_gate_w, m_b_gate_b, m_b_norm_g, m_w_mem_kv, m_w_out, m_ln_g, m_ln_b]
    v_list = [v_w_in, v_a_rel_bias, v_b_gate_w, v_b_gate_b, v_b_norm_g, v_w_mem_kv, v_w_out, v_ln_g, v_ln_b]
    names = ["w_in", "rel", "gate_w", "gate_b", "norm_g", "w_kv", "w_out", "ln_g", "ln_b"]
    upd = [_adamw_nd(w, g, m, v, "adamw_" + n) for w, g, m, v, n in zip(w_list, g_list, m_list, v_list, names)]
    deltas = [u_[0] for u_ in upd]
    new_m = [u_[1] for u_ in upd]
    new_v = [u_[2] for u_ in upd]
    return (loss, grad_x, *g_list, *deltas, *new_m, *new_v)
```

```python
import functools

import numpy as np
import jax
import jax.numpy as jnp
from jax import lax
from jax.experimental import pallas as pl
from jax.experimental.pallas import tpu as pltpu

F32 = jnp.float32
BF16 = jnp.bfloat16
MESH = pl.DeviceIdType.MESH

D_MODEL = 2048
DEPTH = 4
CHUNK = 64
LEFT_CHUNKS = 8
MAX_REL = 128
N_REL = 2 * MAX_REL + 1
A_HEADS = 8
HEAD_DIM = 128
B_HEADS = 4
B_DK = 64
M_HEADS = 4
GATE_RANK = 16
GATE_TAU = 16.0
A_WIDTH = A_HEADS * HEAD_DIM
B_WIDTH = B_HEADS * HEAD_DIM
B_KEY_WIDTH = B_HEADS * B_DK
M_WIDTH = M_HEADS * HEAD_DIM
IN_WIDTH = 4 * A_WIDTH + 2 * B_KEY_WIDTH + 2 * B_WIDTH + GATE_RANK + 2 * M_WIDTH
ALPHA = (2.0 * DEPTH) ** 0.25
LN_EPS = 1e-5
RMS_EPS = 1e-6
NEG_INF = -1e30
ADAM_LR = 0.001
ADAM_B1 = 0.9
ADAM_B2 = 0.999
ADAM_EPS = 1e-08
ADAM_WD = 0.01
ADAM_STEP = 10

LANES = 128
VMEM_LIMIT = 56 * 1024 * 1024

C_AQ, C_AK, C_AV, C_AZ = 0, 1024, 2048, 3072
C_BQ, C_BK, C_BV, C_BZ = 4096, 4608, 5120, 5632
C_MQ, C_MZ, C_LR = 6144, 6656, 7168
HP = 7680
TQ = 512
CPB = TQ // CHUNK
N_CHIPS = 4
N_DEV = 8


def _params(sem, vmem=VMEM_LIMIT):
    return pltpu.CompilerParams(dimension_semantics=sem, vmem_limit_bytes=vmem)


def _dot(a, b):
    return jnp.dot(a, b, preferred_element_type=F32)


def _dot_nt(a, b):
    return lax.dot_general(a, b, (((1,), (1,)), ((), ())), preferred_element_type=F32)


def _dot_tn(a, b):
    return lax.dot_general(a, b, (((0,), (0,)), ((), ())), preferred_element_type=F32)


def _sigmoid(x):
    return 1.0 / (1.0 + jnp.exp(-x))


def _split3(x):
    hi = x.astype(BF16)
    r = x - hi.astype(F32)
    mid = r.astype(BF16)
    lo = (r - mid.astype(F32)).astype(BF16)
    return hi, mid, lo


def _dot3(m_bf, x):
    hi, mid, lo = _split3(x)
    return _dot(m_bf, hi) + _dot(m_bf, mid) + _dot(m_bf, lo)


def _matmul(a, b, *, mode, out_dtype, tm, tn, tk, name, add=None, add_scale=1.0):
    if mode == "nn":
        (M, K), (K2, N) = a.shape, b.shape
        a_spec = pl.BlockSpec((tm, tk), lambda i, j, k: (i, k))
        b_spec = pl.BlockSpec((tk, tn), lambda i, j, k: (k, j))
        dot = _dot
    elif mode == "nt":
        (M, K), (N, K2) = a.shape, b.shape
        a_spec = pl.BlockSpec((tm, tk), lambda i, j, k: (i, k))
        b_spec = pl.BlockSpec((tn, tk), lambda i, j, k: (j, k))
        dot = _dot_nt
    else:
        (K, M), (K2, N) = a.shape, b.shape
        a_spec = pl.BlockSpec((tk, tm), lambda i, j, k: (k, i))
        b_spec = pl.BlockSpec((tk, tn), lambda i, j, k: (k, j))
        dot = _dot_tn
    assert K == K2 and M % tm == 0 and N % tn == 0 and K % tk == 0, (a.shape, b.shape, mode)
    nk = K // tk
    has_add = add is not None
    assert nk == 1 or out_dtype == F32

    def body(*refs):
        if has_add:
            a_ref, b_ref, add_ref, o_ref = refs
        else:
            a_ref, b_ref, o_ref = refs
        k = pl.program_id(2)
        part = dot(a_ref[...].astype(BF16), b_ref[...].astype(BF16))

        @pl.when(k == 0)
        def _():
            first = part + add_scale * add_ref[...] if has_add else part
            o_ref[...] = first.astype(out_dtype)

        if nk > 1:
            @pl.when(k > 0)
            def _():
                o_ref[...] += part

    in_specs = [a_spec, b_spec]
    args = [a, b]
    if has_add:
        in_specs.append(pl.BlockSpec((tm, tn), lambda i, j, k: (i, j)))
        args.append(add)
    return pl.pallas_call(
        body,
        name=name,
        grid=(M // tm, N // tn, nk),
        in_specs=in_specs,
        out_specs=pl.BlockSpec((tm, tn), lambda i, j, k: (i, j)),
        out_shape=jax.ShapeDtypeStruct((M, N), out_dtype),
        compiler_params=_params(("parallel", "parallel", "arbitrary")),
    )(*args)


def _transpose(a, name):
    R, C = a.shape
    t = 512

    def body(a_ref, o_ref):
        o_ref[...] = a_ref[...].T

    return pl.pallas_call(
        body, name=name, grid=(R // t, C // t),
        in_specs=[pl.BlockSpec((t, t), lambda i, j: (i, j))],
        out_specs=pl.BlockSpec((t, t), lambda i, j: (j, i)),
        out_shape=jax.ShapeDtypeStruct((C, R), a.dtype),
        compiler_params=_params(("parallel", "parallel")),
    )(a)


def _chunk_of(rows):
    return lax.shift_right_logical(rows, CHUNK.bit_length() - 1)


A_HPS = 2
A_HW = A_HPS * LANES


def _band_bias(u_row):
    bias = pltpu.roll(jnp.broadcast_to(u_row, (TQ, 2 * TQ)), 0, 1, stride=1, stride_axis=0)
    qc = _chunk_of(lax.broadcasted_iota(jnp.int32, (TQ, 2 * TQ), 0))
    col = lax.broadcasted_iota(jnp.int32, (TQ, 2 * TQ), 1)
    kc = _chunk_of(jnp.bitwise_and(col, TQ - 1))
    ok = jnp.logical_or(jnp.logical_and(col < TQ, kc >= qc), jnp.logical_and(col >= TQ, kc <= qc))
    return jnp.where(ok, bias, NEG_INF)


def _band_probs(q, kp, kc, bias, first):
    scale = HEAD_DIM ** -0.5
    sp = _dot_nt(q, kp) * scale + (bias[:, :TQ] + first * NEG_INF)
    sc = _dot_nt(q, kc) * scale + bias[:, TQ:]
    m = jnp.maximum(jnp.max(sp, axis=1, keepdims=True), jnp.max(sc, axis=1, keepdims=True))
    pp = jnp.exp(sp - m)
    pc = jnp.exp(sc - m)
    inv = 1.0 / (jnp.sum(pp, axis=1, keepdims=True) + jnp.sum(pc, axis=1, keepdims=True))
    return pp * inv, pc * inv


def _band_specs(nq):
    def col(c0):
        return c0 // A_HW

    q_spec = pl.BlockSpec((TQ, A_HW), lambda h, i: (jnp.minimum(i, nq - 1), col(C_AQ) + h))
    kp_spec = pl.BlockSpec((TQ, A_HW), lambda h, i: (jnp.clip(i - 1, 0, nq - 1), col(C_AK) + h))
    kc_spec = pl.BlockSpec((TQ, A_HW), lambda h, i: (jnp.minimum(i, nq - 1), col(C_AK) + h))
    vp_spec = pl.BlockSpec((TQ, A_HW), lambda h, i: (jnp.clip(i - 1, 0, nq - 1), col(C_AV) + h))
    vc_spec = pl.BlockSpec((TQ, A_HW), lambda h, i: (jnp.minimum(i, nq - 1), col(C_AV) + h))
    z_spec = pl.BlockSpec((TQ, A_HW), lambda h, i: (jnp.minimum(i, nq - 1), col(C_AZ) + h))
    u_spec = pl.BlockSpec((A_HPS, 1, 2 * TQ), lambda h, i: (h, 0, 0))
    return q_spec, kp_spec, kc_spec, vp_spec, vc_spec, z_spec, u_spec


def _band_fwd(h, u):
    S = h.shape[0]
    nq = S // TQ

    def body(q_ref, kp_ref, kc_ref, vp_ref, vc_ref, z_ref, u_ref, y_ref, yt_ref, bias_scr):
        i = pl.program_id(1)

        @pl.when(i == 0)
        def _():
            for hh in range(A_HPS):
                bias_scr[hh] = _band_bias(u_ref[hh])

        first = (i == 0).astype(F32)
        for hh in range(A_HPS):
            cs = slice(hh * LANES, (hh + 1) * LANES)
            pp, pc = _band_probs(q_ref[:, cs], kp_ref[:, cs], kc_ref[:, cs], bias_scr[hh], first)
            o = _dot(pp.astype(BF16), vp_ref[:, cs]) + _dot(pc.astype(BF16), vc_ref[:, cs])
            z = z_ref[:, cs].astype(F32)
            y = o * (z * _sigmoid(z))
            y_ref[:, cs] = y.astype(BF16)
            yt_ref[cs, :] = y.T.astype(BF16)

    specs = _band_specs(nq)
    return pl.pallas_call(
        body,
        name="band_fwd",
        grid=(A_HEADS // A_HPS, nq),
        in_specs=[specs[0], specs[1], specs[2], specs[3], specs[4], specs[5], specs[6]],
        out_specs=[pl.BlockSpec((TQ, A_HW), lambda h, i: (i, h)), pl.BlockSpec((A_HW, TQ), lambda h, i: (h, i))],
        out_shape=[jax.ShapeDtypeStruct((S, D_MODEL), BF16), jax.ShapeDtypeStruct((D_MODEL, S), BF16)],
        scratch_shapes=[pltpu.VMEM((A_HPS, TQ, 2 * TQ), F32)],
        compiler_params=_params(("parallel", "arbitrary")),
    )(h, h, h, h, h, h, u)


def _band_bwd(h, u, dycat):
    S = h.shape[0]
    nq = S // TQ
    scale = HEAD_DIM ** -0.5

    def body(q_ref, kp_ref, kc_ref, vp_ref, vc_ref, z_ref, u_ref, dy_ref,
             dq_ref, dk_ref, dv_ref, dz_ref, du_ref, bias_scr, db_scr, ck_scr, cv_scr):
        i = pl.program_id(1)

        @pl.when(i == 0)
        def _():
            for hh in range(A_HPS):
                bias_scr[hh] = _band_bias(u_ref[hh])
            db_scr[...] = jnp.zeros_like(db_scr)
            ck_scr[...] = jnp.zeros_like(ck_scr)
            cv_scr[...] = jnp.zeros_like(cv_scr)

        @pl.when(i < nq)
        def _():
            first = (i == 0).astype(F32)
            for hh in range(A_HPS):
                cs = slice(hh * LANES, (hh + 1) * LANES)
                q, kp, kc, vp, vc = q_ref[:, cs], kp_ref[:, cs], kc_ref[:, cs], vp_ref[:, cs], vc_ref[:, cs]
                pp, pc = _band_probs(q, kp, kc, bias_scr[hh], first)
                ppb, pcb = pp.astype(BF16), pc.astype(BF16)
                o = _dot(ppb, vp) + _dot(pcb, vc)
                z = z_ref[:, cs].astype(F32)
                sg = _sigmoid(z)
                dy = dy_ref[:, cs].astype(F32)
                do = dy * (z * sg)
                dz_ref[:, cs] = (dy * o * (sg * (1.0 + z * (1.0 - sg)))).astype(BF16)
                dob = do.astype(BF16)
                delta = jnp.sum(do * o, axis=1, keepdims=True)
                dsp = pp * (_dot_nt(dob, vp) - delta)
                dsc = pc * (_dot_nt(dob, vc) - delta)
                db_scr[hh, :, :TQ] += dsp
                db_scr[hh, :, TQ:] += dsc
                dspb, dscb = dsp.astype(BF16), dsc.astype(BF16)
                dq_ref[:, cs] = (scale * (_dot(dspb, kp) + _dot(dscb, kc))).astype(BF16)
                dk_ref[:, cs] = (ck_scr[:, cs] + scale * _dot_tn(dspb, q)).astype(BF16)
                dv_ref[:, cs] = (cv_scr[:, cs] + _dot_tn(ppb, dob)).astype(BF16)
                ck_scr[:, cs] = scale * _dot_tn(dscb, q)
                cv_scr[:, cs] = _dot_tn(pcb, dob)

        @pl.when(i == nq)
        def _():
            dk_ref[...] = ck_scr[...].astype(BF16)
            dv_ref[...] = cv_scr[...].astype(BF16)
            r0 = lax.broadcasted_iota(jnp.int32, (TQ, TQ), 0)
            r1 = lax.broadcasted_iota(jnp.int32, (TQ, TQ), 1)
            flip = (r0 + r1 == TQ - 1).astype(BF16)
            for hh in range(A_HPS):
                fl = _dot3(flip, db_scr[hh])
                rolled = pltpu.roll(fl, 0, 1, stride=1, stride_axis=0)
                du_ref[hh] = jnp.sum(rolled, axis=0, keepdims=True)

    specs = _band_specs(nq)
    row_spec = pl.BlockSpec((TQ, A_HW), lambda h, i: (jnp.minimum(i, nq - 1), h))
    key_spec = pl.BlockSpec((TQ, A_HW), lambda h, i: (jnp.maximum(i - 1, 0), h))
    out_sd = jax.ShapeDtypeStruct((S, A_WIDTH), BF16)
    return pl.pallas_call(
        body,
        name="band_bwd",
        grid=(A_HEADS // A_HPS, nq + 1),
        in_specs=[specs[0], specs[1], specs[2], specs[3], specs[4], specs[5], specs[6], row_spec],
        out_specs=[row_spec, key_spec, key_spec, row_spec, pl.BlockSpec((A_HPS, 1, 2 * TQ), lambda h, i: (h, 0, 0))],
        out_shape=[out_sd, out_sd, out_sd, out_sd, jax.ShapeDtypeStruct((A_HEADS, 1, 2 * TQ), F32)],
        scratch_shapes=[pltpu.VMEM((A_HPS, TQ, 2 * TQ), F32), pltpu.VMEM((A_HPS, TQ, 2 * TQ), F32),
                        pltpu.VMEM((TQ, A_HW), F32), pltpu.VMEM((TQ, A_HW), F32)],
        compiler_params=_params(("parallel", "arbitrary")),
    )(h, h, h, h, h, h, u, dycat)


def _bias_by_offset(table):
    far = jnp.broadcast_to(table[:, N_REL - 1:], (A_HEADS, TQ - MAX_REL))
    ramp = jnp.flip(table, axis=1)
    rest = jnp.broadcast_to(table[:, :1], (A_HEADS, 2 * TQ - CHUNK - (TQ + MAX_REL + 1)))
    wrap = jnp.broadcast_to(table[:, N_REL - 1:], (A_HEADS, CHUNK))
    return jnp.concatenate([far, ramp, rest, wrap], axis=1)[:, None, :]


def _bias_grad_from_offset(du):
    g = jnp.roll(du[:, 0, :], -(TQ - 1), axis=1)
    far = jnp.sum(g[:, :TQ - MAX_REL], axis=1) + jnp.sum(g[:, 2 * TQ - CHUNK:], axis=1)
    ramp = jnp.flip(g[:, TQ - MAX_REL:TQ + MAX_REL + 1], axis=1)
    return ramp.at[:, N_REL - 1].add(far)


def _mem_probs(q, mk):
    s = _dot_nt(q, mk) * (HEAD_DIM ** -0.5)
    p = jnp.exp(s - jnp.max(s, axis=1, keepdims=True))
    return p * (1.0 / jnp.sum(p, axis=1, keepdims=True))


def _mem_fwd(h, mkv, ycat, ycat_t):
    S = h.shape[0]
    nm = mkv.shape[0]
    c0 = (A_WIDTH + B_WIDTH) // LANES

    def body(q_ref, z_ref, mk_ref, mv_ref, yin_ref, ytin_ref, y_ref, yt_ref):
        del yin_ref, ytin_ref
        p = _mem_probs(q_ref[...], mk_ref[...])
        o = _dot(p.astype(BF16), mv_ref[...])
        z = z_ref[...].astype(F32)
        y = o * (z * _sigmoid(z))
        y_ref[...] = y.astype(BF16)
        yt_ref[...] = y.T.astype(BF16)

    return pl.pallas_call(
        body,
        name="mem_fwd",
        grid=(M_HEADS, S // TQ),
        in_specs=[pl.BlockSpec((TQ, LANES), lambda h, i: (i, C_MQ // LANES + h)),
                  pl.BlockSpec((TQ, LANES), lambda h, i: (i, C_MZ // LANES + h)),
                  pl.BlockSpec((nm, LANES), lambda h, i: (0, h)),
                  pl.BlockSpec((nm, LANES), lambda h, i: (0, M_HEADS + h)), ANY, ANY],
        out_specs=[pl.BlockSpec((TQ, LANES), lambda h, i: (i, c0 + h)), pl.BlockSpec((LANES, TQ), lambda h, i: (c0 + h, i))],
        out_shape=[jax.ShapeDtypeStruct(ycat.shape, BF16), jax.ShapeDtypeStruct(ycat_t.shape, BF16)],
        input_output_aliases={4: 0, 5: 1},
        compiler_params=_params(("parallel", "arbitrary")),
    )(h, h, mkv, mkv, ycat, ycat_t)


def _mem_bwd(h, mkv, dycat):
    S = h.shape[0]
    nm = mkv.shape[0]
    scale = HEAD_DIM ** -0.5

    def body(q_ref, z_ref, mk_ref, mv_ref, dy_ref, dq_ref, dz_ref, dmk_ref, dmv_ref):
        i = pl.program_id(1)
        q, mk, mv = q_ref[...], mk_ref[...], mv_ref[...]
        p = _mem_probs(q, mk)
        pb = p.astype(BF16)
        o = _dot(pb, mv)
        z = z_ref[...].astype(F32)
        sg = _sigmoid(z)
        dy = dy_ref[...].astype(F32)
        do = dy * (z * sg)
        dz_ref[...] = (dy * o * (sg * (1.0 + z * (1.0 - sg)))).astype(BF16)
        dob = do.astype(BF16)
        ds = p * (_dot_nt(dob, mv) - jnp.sum(do * o, axis=1, keepdims=True))
        dsb = ds.astype(BF16)
        dq_ref[...] = (scale * _dot(dsb, mk)).astype(BF16)
        dmk = scale * _dot_tn(dsb, q)
        dmv = _dot_tn(pb, dob)

        @pl.when(i == 0)
        def _():
            dmk_ref[...] = dmk
            dmv_ref[...] = dmv

        @pl.when(i > 0)
        def _():
            dmk_ref[...] += dmk
            dmv_ref[...] += dmv

    row = pl.BlockSpec((TQ, LANES), lambda h, i: (i, h))
    out_sd = jax.ShapeDtypeStruct((S, M_WIDTH), BF16)
    dq, dz, dmkv, dmkv2 = pl.pallas_call(
        body,
        name="mem_bwd",
        grid=(M_HEADS, S // TQ),
        in_specs=[pl.BlockSpec((TQ, LANES), lambda h, i: (i, C_MQ // LANES + h)),
                  pl.BlockSpec((TQ, LANES), lambda h, i: (i, C_MZ // LANES + h)),
                  pl.BlockSpec((nm, LANES), lambda h, i: (0, h)),
                  pl.BlockSpec((nm, LANES), lambda h, i: (0, M_HEADS + h)),
                  pl.BlockSpec((TQ, LANES), lambda h, i: (i, (A_WIDTH + B_WIDTH) // LANES + h))],
        out_specs=[row, row,
                   pl.BlockSpec((nm, LANES), lambda h, i: (0, h)),
                   pl.BlockSpec((nm, LANES), lambda h, i: (0, h))],
        out_shape=[out_sd, out_sd, jax.ShapeDtypeStruct((nm, M_WIDTH), F32), jax.ShapeDtypeStruct((nm, M_WIDTH), F32)],
        compiler_params=_params(("parallel", "arbitrary")),
    )(h, h, mkv, mkv, dycat)
    return dq, dz, jnp.concatenate([dmkv, dmkv2], axis=1)


def _chunk_masks():
    r = lax.broadcasted_iota(jnp.int32, (TQ, TQ), 0)
    c = lax.broadcasted_iota(jnp.int32, (TQ, TQ), 1)
    same = _chunk_of(r) == _chunk_of(c)
    return jnp.logical_and(same, c <= r), jnp.logical_and(same, c > r)


def _gla_gates(lr, gw, gb):
    logit = _dot(lr, gw) + gb
    sg = _sigmoid(logit)
    g = (jnp.minimum(logit, 0.0) - jnp.log(1.0 + jnp.exp(-jnp.abs(logit)))) * (1.0 / GATE_TAU)
    lo, _ = _chunk_masks()
    return sg, _dot3(lo.astype(BF16), g)


def _gla_factors(q, k, b):
    eb = jnp.exp(b)
    enb = jnp.exp(-b)
    return eb, enb, q * eb, q * enb, k * eb, k * enb


def _gla_intra(qp, qn, kp, kn):
    lo, up = _chunk_masks()
    return (jnp.where(lo, _dot_nt(qp.astype(BF16), kn.astype(BF16)), 0.0)
            + jnp.where(up, _dot_nt(qn.astype(BF16), kp.astype(BF16)), 0.0))


def _gla_specs(nb, rev):
    blk = (lambda i: nb - 1 - i) if rev else (lambda i: i)

    def at(c0):
        return pl.BlockSpec((TQ, LANES), lambda i, h: (blk(i), c0 // LANES + h))

    lr_spec = pl.BlockSpec((TQ, LANES), lambda i, h: (blk(i), C_LR // LANES))
    gw_spec = pl.BlockSpec((LANES, LANES), lambda i, h: (0, h))
    gb_spec = pl.BlockSpec((1, LANES), lambda i, h: (0, h))
    gn_spec = pl.BlockSpec((1, LANES), lambda i, h: (0, 0))
    return at(C_BQ), at(C_BK), at(C_BV), at(C_BZ), lr_spec, gw_spec, gb_spec, gn_spec, blk


def _gla_fwd(h, gw, gb, gn, ycat, ycat_t):
    S = h.shape[0]
    nb = S // TQ
    c0 = A_WIDTH // LANES

    def body(q_ref, k_ref, v_ref, z_ref, lr_ref, gw_ref, gb_ref, gn_ref, yin_ref, ytin_ref,
             y_ref, yt_ref, o_ref, st_ref, st_scr):
        del yin_ref, ytin_ref
        i, hd = pl.program_id(0), pl.program_id(1)

        @pl.when(i == 0)
        def _():
            st_scr[hd] = jnp.zeros((LANES, LANES), F32)

        q = q_ref[...].astype(F32) * (B_DK ** -0.5)
        k = k_ref[...].astype(F32)
        v = v_ref[...]
        _, b = _gla_gates(lr_ref[...], gw_ref[...], gb_ref[...])
        _, _, qp, qn, kp, kn = _gla_factors(q, k, b)
        o_intra = _dot(_gla_intra(qp, qn, kp, kn).astype(BF16), v)
        qpb, knb = qp.astype(BF16), kn.astype(BF16)
        st = st_scr[hd]
        outs = []
        for c in range(CPB):
            rows = slice(c * CHUNK, (c + 1) * CHUNK)
            st_ref[0, c] = st
            outs.append(_dot_nt(qpb[rows], st.astype(BF16)))
            e_last = jnp.exp(b[(c + 1) * CHUNK - 1:(c + 1) * CHUNK, :])
            st = (st + _dot_tn(v[rows], knb[rows])) * e_last
        st_scr[hd] = st
        o = o_intra + jnp.concatenate(outs, axis=0)
        o_ref[...] = o
        r = lax.rsqrt(jnp.mean(o * o, axis=1, keepdims=True) + RMS_EPS)
        z = z_ref[...].astype(F32)
        y = o * r * gn_ref[...] * (z * _sigmoid(z))
        y_ref[...] = y.astype(BF16)
        yt_ref[...] = y.T.astype(BF16)

    q_s, k_s, v_s, z_s, lr_s, gw_s, gb_s, gn_s, _ = _gla_specs(nb, False)
    row = pl.BlockSpec((TQ, LANES), lambda i, h: (i, h))
    return pl.pallas_call(
        body,
        name="gla_fwd",
        grid=(nb, B_HEADS),
        in_specs=[q_s, k_s, v_s, z_s, lr_s, gw_s, gb_s, gn_s, ANY, ANY],
        out_specs=[pl.BlockSpec((TQ, LANES), lambda i, h: (i, c0 + h)), pl.BlockSpec((LANES, TQ), lambda i, h: (c0 + h, i)),
                   row, pl.BlockSpec((1, CPB, LANES, LANES), lambda i, h: (h, i, 0, 0))],
        out_shape=[jax.ShapeDtypeStruct(ycat.shape, BF16), jax.ShapeDtypeStruct(ycat_t.shape, BF16),
                   jax.ShapeDtypeStruct((S, B_WIDTH), F32),
                   jax.ShapeDtypeStruct((B_HEADS, S // CHUNK, LANES, LANES), F32)],
        input_output_aliases={8: 0, 9: 1},
        scratch_shapes=[pltpu.VMEM((B_HEADS, LANES, LANES), F32)],
        compiler_params=_params(("arbitrary", "arbitrary")),
    )(h, h, h, h, h, gw, gb, gn, ycat, ycat_t)


def _gla_bwd(h, gw, gb, gn, o_pre, states, dycat):
    S = h.shape[0]
    nb = S // TQ

    def body(q_ref, k_ref, v_ref, z_ref, lr_ref, gw_ref, gb_ref, gn_ref, o_ref, st_ref, dy_ref,
             dq_ref, dk_ref, dv_ref, dz_ref, dlr_ref, dgw_ref, dgb_ref, dgn_ref,
             dst_scr, dgw_scr, dgb_scr, dgn_scr):
        i, hd = pl.program_id(0), pl.program_id(1)

        @pl.when(i == 0)
        def _():
            dst_scr[hd] = jnp.zeros((LANES, LANES), F32)
            dgw_scr[hd] = jnp.zeros((LANES, LANES), F32)
            dgb_scr[hd] = jnp.zeros((1, LANES), F32)

        @pl.when(jnp.logical_and(i == 0, hd == 0))
        def _():
            dgn_scr[...] = jnp.zeros_like(dgn_scr)

        q = q_ref[...].astype(F32) * (B_DK ** -0.5)
        k = k_ref[...].astype(F32)
        v = v_ref[...]
        lr, gwv = lr_ref[...], gw_ref[...]
        sg, b = _gla_gates(lr, gwv, gb_ref[...])
        eb, enb, qp, qn, kp, kn = _gla_factors(q, k, b)
        a = _gla_intra(qp, qn, kp, kn)
        qpb, qnb, kpb, knb = qp.astype(BF16), qn.astype(BF16), kp.astype(BF16), kn.astype(BF16)

        o = o_ref[...]
        gn = gn_ref[...]
        r = lax.rsqrt(jnp.mean(o * o, axis=1, keepdims=True) + RMS_EPS)
        z = z_ref[...].astype(F32)
        sz = _sigmoid(z)
        dy = dy_ref[...].astype(F32)
        d_on = dy * (z * sz)
        dz_ref[...] = (dy * (o * r * gn) * (sz * (1.0 + z * (1.0 - sz)))).astype(BF16)
        dgn_scr[...] += jnp.sum(d_on * o * r, axis=0, keepdims=True)
        t = d_on * gn
        do = r * t - o * (r * r * r) * jnp.mean(t * o, axis=1, keepdims=True)
        dob = do.astype(BF16)

        lo, up = _chunk_masks()
        da = _dot_nt(dob, v)
        dalo = jnp.where(lo, da, 0.0).astype(BF16)
        daup = jnp.where(up, da, 0.0).astype(BF16)
        dqp = _dot(dalo, knb)
        dkn = _dot_tn(dalo, qpb)
        dqn = _dot(daup, kpb)
        dkp = _dot_tn(daup, qnb)
        dv = _dot_tn(a.astype(BF16), dob)

        dst = dst_scr[hd]
        dqp_c, dkn_c, dv_c, dbl_c = [None] * CPB, [None] * CPB, [None] * CPB, [None] * CPB
        for c in reversed(range(CPB)):
            rows = slice(c * CHUNK, (c + 1) * CHUNK)
            st = st_ref[0, c]
            e_last = jnp.exp(b[(c + 1) * CHUNK - 1:(c + 1) * CHUNK, :])
            if c == CPB - 1:
                st_next = (st + _dot_tn(v[rows], knb[rows])) * e_last
            else:
                st_next = st_ref[0, c + 1]
            dbl_c[c] = jnp.sum(dst * st_next, axis=0, keepdims=True)
            dtt = (dst * e_last).astype(BF16)
            dv_c[c] = _dot_nt(knb[rows], dtt)
            dkn_c[c] = _dot(v[rows], dtt)
            dqp_c[c] = _dot(dob[rows], st.astype(BF16))
            dst = _dot_tn(dob[rows], qpb[rows]) + dst * e_last
        dst_scr[hd] = dst
        dqp = dqp + jnp.concatenate(dqp_c, axis=0)
        dkn = dkn + jnp.concatenate(dkn_c, axis=0)
        dv = dv + jnp.concatenate(dv_c, axis=0)
        dv_ref[...] = dv.astype(BF16)
        dq_ref[...] = ((dqp * eb + dqn * enb) * (B_DK ** -0.5)).astype(BF16)
        dk_ref[...] = (dkp * eb + dkn * enb).astype(BF16)

        last = jnp.bitwise_and(lax.broadcasted_iota(jnp.int32, (TQ, 1), 0), CHUNK - 1) == CHUNK - 1
        dbl = jnp.concatenate([jnp.broadcast_to(x, (CHUNK, LANES)) for x in dbl_c], axis=0)
        db = dqp * qp - dqn * qn + dkp * kp - dkn * kn + jnp.where(last, dbl, 0.0)
        r0 = lax.broadcasted_iota(jnp.int32, (TQ, TQ), 0)
        r1 = lax.broadcasted_iota(jnp.int32, (TQ, TQ), 1)
        upper = jnp.logical_and(_chunk_of(r0) == _chunk_of(r1), r1 >= r0).astype(BF16)
        dlogit = _dot3(upper, db) * (1.0 / GATE_TAU) * (1.0 - sg)
        dlb = dlogit.astype(BF16)
        dlr = _dot_nt(dlb, gwv)

        @pl.when(hd == 0)
        def _():
            dlr_ref[...] = dlr

        @pl.when(hd > 0)
        def _():
            dlr_ref[...] += dlr

        dgw_scr[hd] += _dot_tn(lr, dlb)
        dgb_scr[hd] += jnp.sum(dlogit, axis=0, keepdims=True)

        @pl.when(i == nb - 1)
        def _():
            dgw_ref[...] = dgw_scr[hd]
            dgb_ref[...] = dgb_scr[hd]
            dgn_ref[...] = dgn_scr[...]

    q_s, k_s, v_s, z_s, lr_s, gw_s, gb_s, gn_s, blk = _gla_specs(nb, True)
    row = pl.BlockSpec((TQ, LANES), lambda i, h: (blk(i), h))
    dy_spec = pl.BlockSpec((TQ, LANES), lambda i, h: (blk(i), A_WIDTH // LANES + h))
    st_spec = pl.BlockSpec((1, CPB, LANES, LANES), lambda i, h: (h, blk(i), 0, 0))
    out_sd = jax.ShapeDtypeStruct((S, B_WIDTH), BF16)
    return pl.pallas_call(
        body,
        name="gla_bwd",
        grid=(nb, B_HEADS),
        in_specs=[q_s, k_s, v_s, z_s, lr_s, gw_s, gb_s, gn_s, row, st_spec, dy_spec],
        out_specs=[row, row, row, row,
                   pl.BlockSpec((TQ, LANES), lambda i, h: (blk(i), 0)),
                   pl.BlockSpec((LANES, LANES), lambda i, h: (0, jnp.where(i == nb - 1, h, 0))),
                   pl.BlockSpec((1, LANES), lambda i, h: (0, jnp.where(i == nb - 1, h, 0))),
                   pl.BlockSpec((1, LANES), lambda i, h: (0, 0))],
        out_shape=[out_sd, out_sd, out_sd, out_sd,
                   jax.ShapeDtypeStruct((S, LANES), F32),
                   jax.ShapeDtypeStruct((LANES, B_HEADS * LANES), F32),
                   jax.ShapeDtypeStruct((1, B_HEADS * LANES), F32),
                   jax.ShapeDtypeStruct((1, LANES), F32)],
        scratch_shapes=[pltpu.VMEM((B_HEADS, LANES, LANES), F32), pltpu.VMEM((B_HEADS, LANES, LANES), F32),
                        pltpu.VMEM((B_HEADS, 1, LANES), F32), pltpu.VMEM((1, LANES), F32)],
        compiler_params=_params(("arbitrary", "arbitrary")),
    )(h, h, h, h, h, gw, gb, gn, o_pre, states, dycat)


LN_ROWS = 256


def _outproj_ln(ycat, w_out, x, g, b):
    S = x.shape[0]

    def body(yc_ref, w_ref, x_ref, g_ref, b_ref, y_ref, yb_ref, yt_ref, xh_ref, rs_ref):
        u = ALPHA * x_ref[...] + _dot(yc_ref[...], w_ref[...])
        mu = jnp.mean(u, axis=1, keepdims=True)
        d = u - mu
        rstd = lax.rsqrt(jnp.mean(d * d, axis=1, keepdims=True) + LN_EPS)
        xh = d * rstd
        y = xh * g_ref[...] + b_ref[...]
        y_ref[...] = y
        yb_ref[...] = y.astype(BF16)
        yt_ref[...] = y.T.astype(BF16)
        xh_ref[...] = xh
        rs_ref[...] = rstd

    row = lambda w: pl.BlockSpec((LN_ROWS, w), lambda i: (i, 0))
    vec = pl.BlockSpec((1, D_MODEL), lambda i: (0, 0))
    return pl.pallas_call(
        body,
        name="outproj_ln",
        grid=(S // LN_ROWS,),
        in_specs=[row(D_MODEL), pl.BlockSpec((D_MODEL, D_MODEL), lambda i: (0, 0)), row(D_MODEL), vec, vec],
        out_specs=[row(D_MODEL), row(D_MODEL), pl.BlockSpec((D_MODEL, LN_ROWS), lambda i: (0, i)), row(D_MODEL), row(1)],
        out_shape=[jax.ShapeDtypeStruct((S, D_MODEL), F32), jax.ShapeDtypeStruct((S, D_MODEL), BF16),
                   jax.ShapeDtypeStruct((D_MODEL, S), BF16),
                   jax.ShapeDtypeStruct((S, D_MODEL), F32), jax.ShapeDtypeStruct((S, 1), F32)],
        compiler_params=_params(("parallel",)),
    )(ycat, w_out, x, g, b)


def _ln_bwd(dy, xhat, rstd, g):
    S = dy.shape[0]

    def body(dy_ref, xh_ref, rs_ref, g_ref, du_ref, dub_ref, dg_ref, db_ref):
        i = pl.program_id(0)
        dy_, xh = dy_ref[...], xh_ref[...]
        dyg = dy_ * g_ref[...]
        m1 = jnp.mean(dyg, axis=1, keepdims=True)
        m2 = jnp.mean(dyg * xh, axis=1, keepdims=True)
        du = rs_ref[...] * (dyg - m1 - xh * m2)
        du_ref[...] = du
        dub_ref[...] = du.astype(BF16)
        dg = jnp.sum(dy_ * xh, axis=0, keepdims=True)
        db = jnp.sum(dy_, axis=0, keepdims=True)

        @pl.when(i == 0)
        def _():
            dg_ref[...] = dg
            db_ref[...] = db

        @pl.when(i > 0)
        def _():
            dg_ref[...] += dg
            db_ref[...] += db

    row = lambda w: pl.BlockSpec((TQ, w), lambda i: (i, 0))
    vec = pl.BlockSpec((1, D_MODEL), lambda i: (0, 0))
    return pl.pallas_call(
        body,
        name="ln_bwd",
        grid=(S // TQ,),
        in_specs=[row(D_MODEL), row(D_MODEL), row(1), vec],
        out_specs=[row(D_MODEL), row(D_MODEL), vec, vec],
        out_shape=[jax.ShapeDtypeStruct((S, D_MODEL), F32), jax.ShapeDtypeStruct((S, D_MODEL), BF16),
                   jax.ShapeDtypeStruct((1, D_MODEL), F32), jax.ShapeDtypeStruct((1, D_MODEL), F32)],
        compiler_params=_params(("arbitrary",)),
    )(dy, xhat, rstd, g)


def _loss_head(y, target):
    S = y.shape[0]

    def body(y_ref, t_ref, l_ref, dy_ref):
        i = pl.program_id(0)
        err = y_ref[...] - t_ref[...]
        dy_ref[...] = err * (1.0 / D_MODEL)
        part = (0.5 / D_MODEL) * jnp.sum(jnp.sum(err * err, axis=1, keepdims=True), axis=0, keepdims=True)

        @pl.when(i == 0)
        def _():
            l_ref[...] = part

        @pl.when(i > 0)
        def _():
            l_ref[...] += part

    row = pl.BlockSpec((TQ, D_MODEL), lambda i: (i, 0))
    return pl.pallas_call(
        body,
        name="loss_head",
        grid=(S // TQ,),
        in_specs=[row, row],
        out_specs=[pl.BlockSpec((1, 1), lambda i: (0, 0)), row],
        out_shape=[jax.ShapeDtypeStruct((1, 1), F32), jax.ShapeDtypeStruct((S, D_MODEL), F32)],
        compiler_params=_params(("arbitrary",)),
    )(y, target)


def _pad_gate(gate_w, gate_b):
    gw = gate_w.reshape(GATE_RANK, B_HEADS, B_DK)
    gw = jnp.pad(gw, ((0, LANES - GATE_RANK), (0, 0), (0, LANES - B_DK))).reshape(LANES, B_HEADS * LANES)
    gb = jnp.pad(gate_b.reshape(B_HEADS, B_DK), ((0, 0), (0, LANES - B_DK))).reshape(1, B_HEADS * LANES)
    return gw.astype(BF16), gb.astype(F32)


def _layer_fwd(x, xb, xt, mem_b, w_in, w_kv, w_out, u, gw, gb, gn, ln_g, ln_b):
    h = _matmul(xb, w_in, mode="nn", out_dtype=BF16, tm=1024, tn=768, tk=D_MODEL, name="in_proj")
    mkv = _matmul(mem_b, w_kv, mode="nn", out_dtype=BF16, tm=mem_b.shape[0], tn=1024, tk=D_MODEL, name="mem_kv")
    ycat, ycat_t = _band_fwd(h, u)
    ycat, ycat_t, o_pre, states = _gla_fwd(h, gw, gb, gn, ycat, ycat_t)
    ycat, ycat_t = _mem_fwd(h, mkv, ycat, ycat_t)
    y, ybf, yt, xhat, rstd = _outproj_ln(ycat, w_out, x, ln_g, ln_b)
    return y, ybf, yt, (xt, h, mkv, ycat_t, o_pre, states, xhat, rstd)


def _layer_bwd(dy, saved, mem_b, w_in, w_out, u, gw, gb, gn, ln_g):
    xt, h, mkv, ycat_t, o_pre, states, xhat, rstd = saved
    S = dy.shape[0]
    du, dub, d_ln_g, d_ln_b = _ln_bwd(dy, xhat, rstd, ln_g)
    dycat = _matmul(dub, w_out, mode="nt", out_dtype=BF16, tm=1024, tn=1024, tk=D_MODEL, name="dycat")
    d_w_out = _matmul(ycat_t, dub, mode="nn", out_dtype=F32, tm=1024, tn=1024, tk=2048, name="d_w_out")
    daq, dak, dav, daz, d_u = _band_bwd(h, u, dycat)
    dbq, dbk, dbv, dbz, dlr, dgw, dgb, dgn = _gla_bwd(h, gw, gb, gn, o_pre, states, dycat)
    dmq, dmz, dmkv = _mem_bwd(h, mkv, dycat)
    d_w_kv = _matmul(mem_b, dmkv, mode="tn", out_dtype=F32, tm=1024, tn=1024, tk=mem_b.shape[0], name="d_w_kv")
    dh = jnp.concatenate([daq, dak, dav, daz, dbq, dbk, dbv, dbz, dmq, dmz, dlr.astype(BF16),
                          jnp.zeros((S, HP - C_LR - LANES), BF16)], axis=1)
    dx = _matmul(dh, w_in, mode="nt", out_dtype=F32, tm=1024, tn=1024, tk=2560, name="dx", add=du, add_scale=ALPHA)
    d_w_in = _matmul(xt, dh, mode="nn", out_dtype=F32, tm=1024, tn=768, tk=2048, name="d_w_in")
    return dx, (d_w_in, d_u, dgw, dgb, dgn, d_w_kv, d_w_out, d_ln_g, d_ln_b)


def _pad_heads(w):
    r = w.shape[0]
    return jnp.pad(w.reshape(r, B_HEADS, B_DK), ((0, 0), (0, 0), (0, LANES - B_DK))).reshape(r, B_HEADS * LANES)


def _unpad_heads(w):
    r = w.shape[0]
    return w.reshape(r, B_HEADS, LANES)[:, :, :B_DK].reshape(r, B_KEY_WIDTH)


O_BQ = 4 * A_WIDTH
O_BK = O_BQ + B_KEY_WIDTH
O_BV = O_BK + B_KEY_WIDTH
O_LR = O_BV + 2 * B_WIDTH
O_MQ = O_LR + GATE_RANK


def _to_padded(w):
    r = w.shape[0]
    return jnp.concatenate([
        w[:, :O_BQ], _pad_heads(w[:, O_BQ:O_BK]), _pad_heads(w[:, O_BK:O_BV]), w[:, O_BV:O_LR], w[:, O_MQ:],
        w[:, O_LR:O_MQ], jnp.zeros((r, HP - C_LR - GATE_RANK), w.dtype)], axis=1)


def _from_padded(g):
    return jnp.concatenate([
        g[:, :C_BQ], _unpad_heads(g[:, C_BQ:C_BK]), _unpad_heads(g[:, C_BK:C_BV]), g[:, C_BV:C_MQ],
        g[:, C_LR:C_LR + GATE_RANK], g[:, C_MQ:C_LR]], axis=1)


def _adamw(w, g, m, v, name):
    R, C = w.shape
    tr = R
    for cand in (256, 128, 64, 32, 16, 8):
        if R % cand == 0 and R > cand:
            tr = cand
            break

    def body(w_ref, g_ref, m_ref, v_ref, d_ref, nm_ref, nv_ref):
        g_ = g_ref[...]
        nm = ADAM_B1 * m_ref[...] + (1.0 - ADAM_B1) * g_
        nv = ADAM_B2 * v_ref[...] + (1.0 - ADAM_B2) * (g_ * g_)
        m_hat = nm / (1.0 - ADAM_B1 ** ADAM_STEP)
        v_hat = nv / (1.0 - ADAM_B2 ** ADAM_STEP)
        d_ref[...] = -ADAM_LR * (m_hat / (jnp.sqrt(v_hat) + ADAM_EPS) + ADAM_WD * w_ref[...])
        nm_ref[...] = nm
        nv_ref[...] = nv

    spec = pl.BlockSpec((tr, C), lambda i: (i, 0))
    sd = jax.ShapeDtypeStruct((R, C), F32)
    return pl.pallas_call(
        body, name=name, grid=(R // tr,), in_specs=[spec] * 4, out_specs=[spec] * 3, out_shape=[sd] * 3,
        compiler_params=_params(("parallel",)),
    )(w, g, m, v)


def _adamw_nd(w, g, m, v, name):
    shape = w.shape
    f = lambda a: a.reshape(-1, shape[-1])
    return tuple(o.reshape(shape) for o in _adamw(f(w), f(g), f(m), f(v), name))


ANY = pl.BlockSpec(memory_space=pl.ANY)


def _place():
    x, y, c = lax.axis_index("x"), lax.axis_index("y"), lax.axis_index("c")
    chips = [(1 - x, y), (x, 1 - y), (1 - x, 1 - y)]
    return x, y, c, chips


def _gather_chips(shard, name):
    R, C = shard.shape
    half = R // 2
    assert half % 16 == 0

    def body(s_ref, o_ref, send, recv):
        x, y, c, chips = _place()
        mine = pl.ds(pl.multiple_of(c * half, 16), half)
        other = pl.ds(pl.multiple_of((1 - c) * half, 16), half)

        def cp(k, src, chip, rows, to):
            dst = o_ref.at[2 * chip[0] + chip[1], rows]
            return pltpu.make_async_remote_copy(src_ref=dst if src is None else src, dst_ref=dst, send_sem=send.at[k],
                                                recv_sem=recv.at[k], device_id=to, device_id_type=MESH)

        first = [cp(k, s_ref.at[mine], (x, y), mine, (*chip, c)) for k, chip in enumerate(chips)]
        for f in first:
            f.start()
        passed = [cp(3 + k, None, chip, mine, (x, y, 1 - c)) for k, chip in enumerate(chips)]
        for k, chip in enumerate(chips):
            cp(k, None, chip, mine, (x, y, c)).wait_recv()
            passed[k].start()
        for k, chip in enumerate(chips):
            cp(3 + k, None, chip, other, (x, y, c)).wait_recv()
        for f in first + passed:
            f.wait_send()

    return pl.pallas_call(
        body, name=name, in_specs=[ANY], out_specs=ANY,
        out_shape=jax.ShapeDtypeStruct((N_CHIPS, R, C), shard.dtype),
        scratch_shapes=[pltpu.SemaphoreType.DMA((6,)), pltpu.SemaphoreType.DMA((6,))],
    )(shard)


def _pair_exchange(g4, name):
    n, _, R, C = g4.shape

    def body(g_ref, o_ref, send, recv):
        x, y, c, _ = _place()
        cps = [pltpu.make_async_remote_copy(src_ref=g_ref.at[j, 1 - c], dst_ref=o_ref.at[j], send_sem=send.at[j],
                                            recv_sem=recv.at[j], device_id=(x, y, 1 - c), device_id_type=MESH)
               for j in range(n)]
        for cp in cps:
            cp.start()
        for cp in cps:
            cp.wait()

    return pl.pallas_call(
        body, name=name, in_specs=[ANY], out_specs=ANY, out_shape=jax.ShapeDtypeStruct((n, R, C), g4.dtype),
        scratch_shapes=[pltpu.SemaphoreType.DMA((n,)), pltpu.SemaphoreType.DMA((n,))],
    )(g4)


def _chip_exchange(p, name):
    n, R, C = p.shape

    def body(p_ref, o_ref, send, recv):
        x, y, c, chips = _place()
        me = 2 * x + y
        cps = [pltpu.make_async_remote_copy(src_ref=p_ref.at[2 * chip[0] + chip[1]], dst_ref=o_ref.at[me],
                                            send_sem=send.at[k], recv_sem=recv.at[k], device_id=(*chip, c),
                                            device_id_type=MESH) for k, chip in enumerate(chips)]
        for cp in cps:
            cp.start()
        for k, chip in enumerate(chips):
            pltpu.make_async_remote_copy(src_ref=p_ref.at[me], dst_ref=o_ref.at[2 * chip[0] + chip[1]],
                                         send_sem=send.at[k], recv_sem=recv.at[k], device_id=(*chip, c),
                                         device_id_type=MESH).wait_recv()
        for cp in cps:
            cp.wait_send()

    return pl.pallas_call(
        body, name=name, in_specs=[ANY], out_specs=ANY, out_shape=jax.ShapeDtypeStruct((n, R, C), p.dtype),
        scratch_shapes=[pltpu.SemaphoreType.DMA((3,)), pltpu.SemaphoreType.DMA((3,))],
    )(p)


def _pair_gather(t2, name):
    def body(t_ref, o_ref, send, recv):
        del t_ref
        x, y, c, _ = _place()
        cp = pltpu.make_async_remote_copy(src_ref=o_ref.at[c], dst_ref=o_ref.at[c], send_sem=send, recv_sem=recv,
                                          device_id=(x, y, 1 - c), device_id_type=MESH)
        cp.start()
        pltpu.make_async_remote_copy(src_ref=o_ref.at[c], dst_ref=o_ref.at[1 - c], send_sem=send, recv_sem=recv,
                                     device_id=(x, y, 1 - c), device_id_type=MESH).wait_recv()
        cp.wait_send()

    return pl.pallas_call(
        body, name=name, in_specs=[ANY], out_specs=ANY, out_shape=jax.ShapeDtypeStruct(t2.shape, t2.dtype),
        input_output_aliases={0: 0},
        scratch_shapes=[pltpu.SemaphoreType.DMA, pltpu.SemaphoreType.DMA],
    )(t2)


def _add_halves(g4, recv, c_idx, name):
    n, _, R, C = g4.shape
    tr = 256

    def body(c_ref, a_ref, b_ref, o_ref):
        o_ref[...] = (a_ref[0] + b_ref[...]).astype(BF16)

    return pl.pallas_call(
        body, name=name,
        grid_spec=pltpu.PrefetchScalarGridSpec(
            num_scalar_prefetch=1, grid=(n, R // tr),
            in_specs=[pl.BlockSpec((1, 1, tr, C), lambda j, i, c: (j, c[0], i, 0)),
                      pl.BlockSpec((1, tr, C), lambda j, i, c: (j, i, 0))],
            out_specs=pl.BlockSpec((1, tr, C), lambda j, i, c: (j, i, 0))),
        out_shape=jax.ShapeDtypeStruct((n, R, C), BF16),
        compiler_params=_params(("parallel", "parallel")),
    )(c_idx, g4, recv)


def _add_slots(r, c_idx, name):
    n, R, C = r.shape
    tr = 256

    def body(c_ref, r_ref, o_ref):
        acc = r_ref[0].astype(F32)
        for j in range(1, n):
            acc = acc + r_ref[j].astype(F32)
        o_ref[0] = acc

    return pl.pallas_call(
        body, name=name,
        grid_spec=pltpu.PrefetchScalarGridSpec(
            num_scalar_prefetch=1, grid=(R // tr,),
            in_specs=[pl.BlockSpec((n, tr, C), lambda i, c: (0, i, 0))],
            out_specs=pl.BlockSpec((1, tr, C), lambda i, c: (c[0], i, 0))),
        out_shape=jax.ShapeDtypeStruct((2, R, C), F32),
        compiler_params=_params(("parallel",)),
    )(c_idx, r)


def _reduce_scatter(g4, c_idx, chip, tag):
    got = _pair_exchange(g4, "rs_pair_" + tag)
    p = _add_halves(g4, got, c_idx, "rs_add2_" + tag)
    q = _chip_exchange(p, "rs_chip_" + tag)
    q = lax.dynamic_update_slice_in_dim(q, lax.dynamic_slice_in_dim(p, chip, 1, axis=0), chip, axis=0)
    t2 = _add_slots(q, c_idx, "rs_add4_" + tag)
    return _pair_gather(t2, "rs_gather_" + tag)


def _all_reduce_small(buf, name):
    R = buf.shape[0]

    def flipped(k, x, y, c):
        return ((1 - x) if k & 4 else x, (1 - y) if k & 2 else y, (1 - c) if k & 1 else c)

    def body(b_ref, o_ref, land, send, recv):
        x, y, c, _ = _place()
        me = 4 * x + 2 * y + c
        land[me] = b_ref[...]
        cps = []
        for k in range(1, N_DEV):
            peer = flipped(k, x, y, c)
            cps.append(pltpu.make_async_remote_copy(src_ref=b_ref, dst_ref=land.at[me], send_sem=send.at[k - 1],
                                                    recv_sem=recv.at[k - 1], device_id=peer, device_id_type=MESH))
        for cp in cps:
            cp.start()
        for k in range(1, N_DEV):
            peer = flipped(k, x, y, c)
            slot = 4 * peer[0] + 2 * peer[1] + peer[2]
            pltpu.make_async_remote_copy(src_ref=b_ref, dst_ref=land.at[slot], send_sem=send.at[k - 1],
                                         recv_sem=recv.at[k - 1], device_id=peer, device_id_type=MESH).wait_recv()
        for cp in cps:
            cp.wait_send()
        acc = land[0]
        for j in range(1, N_DEV):
            acc = acc + land[j]
        o_ref[...] = acc

    vm = pl.BlockSpec(memory_space=pltpu.VMEM)
    return pl.pallas_call(
        body, name=name, in_specs=[vm], out_specs=vm,
        out_shape=jax.ShapeDtypeStruct((R, LANES), F32),
        scratch_shapes=[pltpu.VMEM((N_DEV, R, LANES), F32), pltpu.SemaphoreType.DMA((N_DEV - 1,)),
                        pltpu.SemaphoreType.DMA((N_DEV - 1,))],
    )(buf)


def _by_chip_and_half(g, axis):
    L = g.shape[0]
    if axis == 2:
        n = g.shape[2] // N_CHIPS
        t = g.reshape(L, g.shape[1], N_CHIPS, n).transpose(2, 0, 1, 3)
    else:
        n = g.shape[1] // N_CHIPS
        t = g.reshape(L, N_CHIPS, n, g.shape[2]).transpose(1, 0, 2, 3)
    return t.reshape(N_CHIPS, 2, (L // 2) * t.shape[2], t.shape[3])


def kernel(x, mem, w_in, a_rel_bias, b_gate_w, b_gate_b, b_norm_g, w_mem_kv, w_out, ln_g, ln_b, loss_target, m_w_in, m_a_rel_bias, m_b_gate_w, m_b_gate_b, m_b_norm_g, m_w_mem_kv, m_w_out, m_ln_g, m_ln_b, v_w_in, v_a_rel_bias, v_b_gate_w, v_b_gate_b, v_b_norm_g, v_w_mem_kv, v_w_out, v_ln_g, v_ln_b):
    L = w_in.shape[0]
    S = x.shape[1]
    cx, cy, cc = lax.axis_index("x"), lax.axis_index("y"), lax.axis_index("c")
    chip = 2 * cx + cy
    c_idx = jnp.reshape(cc, (1,)).astype(jnp.int32)

    def gathered(w, name):
        shard = w.astype(BF16).reshape(-1, w.shape[2])
        return lax.dynamic_update_slice_in_dim(_gather_chips(shard, name), shard[None], chip, axis=0)

    n_in = w_in.shape[2]
    win_all = gathered(w_in, "gather_w_in")
    r_kv = w_mem_kv.shape[1]
    wkv_all = gathered(w_mem_kv, "gather_w_kv")
    r_out = w_out.shape[1]
    wout_all = gathered(w_out, "gather_w_out")
    win_all = win_all.reshape(N_CHIPS, L, D_MODEL, n_in)
    wkv_all = wkv_all.reshape(N_CHIPS, L, r_kv, w_mem_kv.shape[2])
    wout_all = wout_all.reshape(N_CHIPS, L, r_out, D_MODEL)

    gw_cols = b_gate_w.shape[2]
    gw_slot = jnp.zeros((N_CHIPS, L, GATE_RANK, gw_cols), F32)
    gw_slot = lax.dynamic_update_slice(gw_slot, (0.5 * b_gate_w)[None], (chip, 0, 0, 0))
    gw_flat = gw_slot.reshape(-1)
    n_gw = gw_flat.shape[0]
    pad = (-n_gw) % (8 * LANES)
    gw_full = _all_reduce_small(jnp.pad(gw_flat, (0, pad)).reshape(-1, LANES), "gather_gate_w").reshape(-1)[:n_gw]
    gw_full = gw_full.reshape(N_CHIPS, L, GATE_RANK, gw_cols).transpose(1, 2, 0, 3).reshape(L, GATE_RANK, B_KEY_WIDTH)

    xs = x.reshape(S, D_MODEL)
    mem_b = mem.reshape(mem.shape[1], D_MODEL).astype(BF16)
    target = loss_target.reshape(S, D_MODEL)

    layer_w = []
    for l in range(L):
        w_in_l = _to_padded(jnp.concatenate([win_all[j, l] for j in range(N_CHIPS)], axis=1))
        w_kv_l = jnp.concatenate([wkv_all[j, l] for j in range(N_CHIPS)], axis=0)
        w_out_l = jnp.concatenate([wout_all[j, l] for j in range(N_CHIPS)], axis=0)
        gw_l, gb_l = _pad_gate(gw_full[l], b_gate_b[l])
        layer_w.append((w_in_l, w_kv_l, w_out_l, _bias_by_offset(a_rel_bias[l]), gw_l, gb_l,
                        b_norm_g[l].reshape(1, LANES), ln_g[l].reshape(1, D_MODEL), ln_b[l].reshape(1, D_MODEL)))

    y, yb = xs, xs.astype(BF16)
    yt = _transpose(yb, "x_t")
    saved = []
    for l in range(L):
        w_in_l, w_kv_l, w_out_l, u_l, gw_l, gb_l, gn_l, lg_l, lb_l = layer_w[l]
        y, yb, yt, sv = _layer_fwd(y, yb, yt, mem_b, w_in_l, w_kv_l, w_out_l, u_l, gw_l, gb_l, gn_l, lg_l, lb_l)
        saved.append(sv)
    loss_part, dy = _loss_head(y, target)

    grads = [None] * L
    for l in reversed(range(L)):
        w_in_l, w_kv_l, w_out_l, u_l, gw_l, gb_l, gn_l, lg_l, lb_l = layer_w[l]
        dy, grads[l] = _layer_bwd(dy, saved[l], mem_b, w_in_l, w_out_l, u_l, gw_l, gb_l, gn_l, lg_l)
    grad_x = dy.reshape(x.shape)

    g_w_in = jnp.stack([_from_padded(g[0]) for g in grads])
    g_rel = jnp.stack([_bias_grad_from_offset(g[1]) for g in grads])
    g_gw = jnp.stack([_unpad_heads(g[2][:GATE_RANK]) for g in grads])
    g_gb = jnp.stack([_unpad_heads(g[3])[0] for g in grads])
    g_gn = jnp.stack([g[4][0] for g in grads])
    g_w_kv = jnp.stack([g[5] for g in grads])
    g_w_out = jnp.stack([g[6] for g in grads])
    g_lg = jnp.stack([g[7][0] for g in grads])
    g_lb = jnp.stack([g[8][0] for g in grads])

    r_w_in = _reduce_scatter(_by_chip_and_half(g_w_in, 2), c_idx, chip, "w_in").reshape(L, D_MODEL, n_in)
    r_w_kv = _reduce_scatter(_by_chip_and_half(g_w_kv, 1), c_idx, chip, "w_kv").reshape(L, r_kv, w_mem_kv.shape[2])
    r_w_out = _reduce_scatter(_by_chip_and_half(g_w_out, 1), c_idx, chip, "w_out").reshape(L, r_out, D_MODEL)

    small = [g_rel, g_gw, g_gb, g_gn, g_lg, g_lb, loss_part]
    flat = jnp.concatenate([s.reshape(-1) for s in small])
    n_small = flat.shape[0]
    pad = (-n_small) % (8 * LANES)
    red = _all_reduce_small(jnp.pad(flat, (0, pad)).reshape(-1, LANES), "all_reduce_small").reshape(-1)
    outs, off = [], 0
    for s in small:
        outs.append(red[off:off + s.size].reshape(s.shape))
        off += s.size
    g_rel, g_gw, g_gb, g_gn, g_lg, g_lb, loss = outs
    loss = loss.reshape(())
    g_gw = lax.dynamic_slice_in_dim(g_gw.reshape(L, GATE_RANK, N_CHIPS, gw_cols), chip, 1, axis=2).reshape(L, GATE_RANK, gw_cols)

    g_list = [r_w_in, g_rel, g_gw, g_gb, g_gn, r_w_kv, r_w_out, g_lg, g_lb]
    w_list = [w_in, a_rel_bias, b_gate_w, b_gate_b, b_norm_g, w_mem_kv, w_out, ln_g, ln_b]
    m_list = [m_w_in, m_a_rel_bias, m_b_gate_w, m_b_gate_b, m_b_norm_g, m_w_mem_kv, m_w_out, m_ln_g, m_ln_b]
    v_list = [v_w_in, v_a_rel_bias, v_b_gate_w, v_b_gate_b, v_b_norm_g, v_w_mem_kv, v_w_out, v_ln_g, v_ln_b]
    names = ["w_in", "rel", "gate_w", "gate_b", "norm_g", "w_kv", "w_out", "ln_g", "ln_b"]
    upd = [_adamw_nd(w, g, m, v, "adamw_" + n) for w, g, m, v, n in zip(w_list, g_list, m_list, v_list, names)]
    deltas = [u_[0] for u_ in upd]
    new_m = [u_[1] for u_ in upd]
    new_v = [u_[2] for u_ in upd]
    return (loss, grad_x, *g_list, *deltas, *new_m, *new_v)
```

```python
import functools

import numpy as np
import jax
import jax.numpy as jnp
from jax import lax
from jax.experimental import pallas as pl
from jax.experimental.pallas import tpu as pltpu

F32 = jnp.float32
BF16 = jnp.bfloat16
MESH = pl.DeviceIdType.MESH

D_MODEL = 2048
DEPTH = 4
CHUNK = 64
LEFT_CHUNKS = 8
MAX_REL = 128
N_REL = 2 * MAX_REL + 1
A_HEADS = 8
HEAD_DIM = 128
B_HEADS = 4
B_DK = 64
M_HEADS = 4
GATE_RANK = 16
GATE_TAU = 16.0
A_WIDTH = A_HEADS * HEAD_DIM
B_WIDTH = B_HEADS * HEAD_DIM
B_KEY_WIDTH = B_HEADS * B_DK
M_WIDTH = M_HEADS * HEAD_DIM
IN_WIDTH = 4 * A_WIDTH + 2 * B_KEY_WIDTH + 2 * B_WIDTH + GATE_RANK + 2 * M_WIDTH
ALPHA = (2.0 * DEPTH) ** 0.25
LN_EPS = 1e-5
RMS_EPS = 1e-6
NEG_INF = -1e30
ADAM_LR = 0.001
ADAM_B1 = 0.9
ADAM_B2 = 0.999
ADAM_EPS = 1e-08
ADAM_WD = 0.01
ADAM_STEP = 10

LANES = 128
VMEM_LIMIT = 56 * 1024 * 1024

C_AQ, C_AK, C_AV, C_AZ = 0, 1024, 2048, 3072
C_BQ, C_BK, C_BV, C_BZ = 4096, 4608, 5120, 5632
C_MQ, C_MZ, C_LR = 6144, 6656, 7168
HP = 7680
TQ = 512
CPB = TQ // CHUNK
N_CHIPS = 4
N_DEV = 8


def _params(sem, vmem=VMEM_LIMIT):
    return pltpu.CompilerParams(dimension_semantics=sem, vmem_limit_bytes=vmem)


def _dot(a, b):
    return jnp.dot(a, b, preferred_element_type=F32)


def _dot_nt(a, b):
    return lax.dot_general(a, b, (((1,), (1,)), ((), ())), preferred_element_type=F32)


def _dot_tn(a, b):
    return lax.dot_general(a, b, (((0,), (0,)), ((), ())), preferred_element_type=F32)


def _sigmoid(x):
    return 1.0 / (1.0 + jnp.exp(-x))


def _split3(x):
    hi = x.astype(BF16)
    r = x - hi.astype(F32)
    mid = r.astype(BF16)
    lo = (r - mid.astype(F32)).astype(BF16)
    return hi, mid, lo


def _dot3(m_bf, x):
    hi, mid, lo = _split3(x)
    return _dot(m_bf, hi) + _dot(m_bf, mid) + _dot(m_bf, lo)


def _matmul(a, b, *, mode, out_dtype, tm, tn, tk, name, add=None, add_scale=1.0):
    if mode == "nn":
        (M, K), (K2, N) = a.shape, b.shape
        a_spec = pl.BlockSpec((tm, tk), lambda i, j, k: (i, k))
        b_spec = pl.BlockSpec((tk, tn), lambda i, j, k: (k, j))
        dot = _dot
    elif mode == "nt":
        (M, K), (N, K2) = a.shape, b.shape
        a_spec = pl.BlockSpec((tm, tk), lambda i, j, k: (i, k))
        b_spec = pl.BlockSpec((tn, tk), lambda i, j, k: (j, k))
        dot = _dot_nt
    else:
        (K, M), (K2, N) = a.shape, b.shape
        a_spec = pl.BlockSpec((tk, tm), lambda i, j, k: (k, i))
        b_spec = pl.BlockSpec((tk, tn), lambda i, j, k: (k, j))
        dot = _dot_tn
    assert K == K2 and M % tm == 0 and N % tn == 0 and K % tk == 0, (a.shape, b.shape, mode)
    nk = K // tk
    has_add = add is not None
    assert nk == 1 or out_dtype == F32

    def body(*refs):
        if has_add:
            a_ref, b_ref, add_ref, o_ref = refs
        else:
            a_ref, b_ref, o_ref = refs
        k = pl.program_id(2)
        part = dot(a_ref[...].astype(BF16), b_ref[...].astype(BF16))

        @pl.when(k == 0)
        def _():
            first = part + add_scale * add_ref[...] if has_add else part
            o_ref[...] = first.astype(out_dtype)

        if nk > 1:
            @pl.when(k > 0)
            def _():
                o_ref[...] += part

    in_specs = [a_spec, b_spec]
    args = [a, b]
    if has_add:
        in_specs.append(pl.BlockSpec((tm, tn), lambda i, j, k: (i, j)))
        args.append(add)
    return pl.pallas_call(
        body,
        name=name,
        grid=(M // tm, N // tn, nk),
        in_specs=in_specs,
        out_specs=pl.BlockSpec((tm, tn), lambda i, j, k: (i, j)),
        out_shape=jax.ShapeDtypeStruct((M, N), out_dtype),
        compiler_params=_params(("parallel", "parallel", "arbitrary")),
    )(*args)


def _transpose(a, name):
    R, C = a.shape
    t = 512

    def body(a_ref, o_ref):
        o_ref[...] = a_ref[...].T

    return pl.pallas_call(
        body, name=name, grid=(R // t, C // t),
        in_specs=[pl.BlockSpec((t, t), lambda i, j: (i, j))],
        out_specs=pl.BlockSpec((t, t), lambda i, j: (j, i)),
        out_shape=jax.ShapeDtypeStruct((C, R), a.dtype),
        compiler_params=_params(("parallel", "parallel")),
    )(a)


def _chunk_of(rows):
    return lax.shift_right_logical(rows, CHUNK.bit_length() - 1)


A_HPS = 2
A_HW = A_HPS * LANES


def _band_bias(u_row, first):
    bias = pltpu.roll(jnp.broadcast_to(u_row, (TQ, 2 * TQ)), 0, 1, stride=1, stride_axis=0)
    qc = _chunk_of(lax.broadcasted_iota(jnp.int32, (TQ, 2 * TQ), 0))
    col = lax.broadcasted_iota(jnp.int32, (TQ, 2 * TQ), 1)
    kc = _chunk_of(jnp.bitwise_and(col, TQ - 1))
    ok = jnp.logical_or(jnp.logical_and(col < TQ, kc >= qc), jnp.logical_and(col >= TQ, kc <= qc))
    return jnp.where(ok, bias, NEG_INF) + jnp.where(col < TQ, first * NEG_INF, 0.0)


def _band_probs(q, kp, kc, bias):
    scale = HEAD_DIM ** -0.5
    sp = _dot_nt(q, kp) * scale + bias[:, :TQ]
    sc = _dot_nt(q, kc) * scale + bias[:, TQ:]
    m = jnp.maximum(jnp.max(sp, axis=1, keepdims=True), jnp.max(sc, axis=1, keepdims=True))
    pp = jnp.exp(sp - m)
    pc = jnp.exp(sc - m)
    inv = 1.0 / (jnp.sum(pp, axis=1, keepdims=True) + jnp.sum(pc, axis=1, keepdims=True))
    return pp, pc, inv


def _band_specs(nq):
    def col(c0):
        return c0 // A_HW

    q_spec = pl.BlockSpec((TQ, A_HW), lambda h, i: (jnp.minimum(i, nq - 1), col(C_AQ) + h))
    kp_spec = pl.BlockSpec((TQ, A_HW), lambda h, i: (jnp.clip(i - 1, 0, nq - 1), col(C_AK) + h))
    kc_spec = pl.BlockSpec((TQ, A_HW), lambda h, i: (jnp.minimum(i, nq - 1), col(C_AK) + h))
    vp_spec = pl.BlockSpec((TQ, A_HW), lambda h, i: (jnp.clip(i - 1, 0, nq - 1), col(C_AV) + h))
    vc_spec = pl.BlockSpec((TQ, A_HW), lambda h, i: (jnp.minimum(i, nq - 1), col(C_AV) + h))
    z_spec = pl.BlockSpec((TQ, A_HW), lambda h, i: (jnp.minimum(i, nq - 1), col(C_AZ) + h))
    u_spec = pl.BlockSpec((A_HPS, 1, 2 * TQ), lambda h, i: (h, 0, 0))
    return q_spec, kp_spec, kc_spec, vp_spec, vc_spec, z_spec, u_spec


def _band_fwd(h, u):
    S = h.shape[0]
    nq = S // TQ

    def body(q_ref, kp_ref, kc_ref, vp_ref, vc_ref, z_ref, u_ref, y_ref, yt_ref, bias_scr):
        i = pl.program_id(1)

        @pl.when(i <= 1)
        def _():
            for hh in range(A_HPS):
                bias_scr[hh] = _band_bias(u_ref[hh], (i == 0).astype(F32))

        for hh in range(A_HPS):
            cs = slice(hh * LANES, (hh + 1) * LANES)
            pp, pc, inv = _band_probs(q_ref[:, cs], kp_ref[:, cs], kc_ref[:, cs], bias_scr[hh])
            o = (_dot(pp.astype(BF16), vp_ref[:, cs]) + _dot(pc.astype(BF16), vc_ref[:, cs])) * inv
            z = z_ref[:, cs].astype(F32)
            y = o * (z * _sigmoid(z))
            y_ref[:, cs] = y.astype(BF16)
            yt_ref[cs, :] = y.T.astype(BF16)

    specs = _band_specs(nq)
    return pl.pallas_call(
        body,
        name="band_fwd",
        grid=(A_HEADS // A_HPS, nq),
        in_specs=[specs[0], specs[1], specs[2], specs[3], specs[4], specs[5], specs[6]],
        out_specs=[pl.BlockSpec((TQ, A_HW), lambda h, i: (i, h)), pl.BlockSpec((A_HW, TQ), lambda h, i: (h, i))],
        out_shape=[jax.ShapeDtypeStruct((S, D_MODEL), BF16), jax.ShapeDtypeStruct((D_MODEL, S), BF16)],
        scratch_shapes=[pltpu.VMEM((A_HPS, TQ, 2 * TQ), F32)],
        compiler_params=_params(("parallel", "arbitrary")),
    )(h, h, h, h, h, h, u)


def _band_bwd(h, u, dycat):
    S = h.shape[0]
    nq = S // TQ
    scale = HEAD_DIM ** -0.5

    def body(q_ref, kp_ref, kc_ref, vp_ref, vc_ref, z_ref, u_ref, dy_ref,
             dq_ref, dk_ref, dv_ref, dz_ref, du_ref, bias_scr, db_scr, ckt_scr, cvt_scr):
        i = pl.program_id(1)

        @pl.when(i <= 1)
        def _():
            for hh in range(A_HPS):
                bias_scr[hh] = _band_bias(u_ref[hh], (i == 0).astype(F32))

        @pl.when(i == 0)
        def _():
            db_scr[...] = jnp.zeros_like(db_scr)
            ckt_scr[...] = jnp.zeros_like(ckt_scr)
            cvt_scr[...] = jnp.zeros_like(cvt_scr)

        @pl.when(i < nq)
        def _():
            for hh in range(A_HPS):
                cs = slice(hh * LANES, (hh + 1) * LANES)
                q, kp, kc, vp, vc = q_ref[:, cs], kp_ref[:, cs], kc_ref[:, cs], vp_ref[:, cs], vc_ref[:, cs]
                pp, pc, inv = _band_probs(q, kp, kc, bias_scr[hh])
                pp, pc = pp * inv, pc * inv
                ppb, pcb = pp.astype(BF16), pc.astype(BF16)
                o = _dot(ppb, vp) + _dot(pcb, vc)
                z = z_ref[:, cs].astype(F32)
                sg = _sigmoid(z)
                dy = dy_ref[:, cs].astype(F32)
                do = dy * (z * sg)
                dz_ref[:, cs] = (dy * o * (sg * (1.0 + z * (1.0 - sg)))).astype(BF16)
                dob = do.astype(BF16)
                delta = jnp.sum(do * o, axis=1, keepdims=True)
                dsp = pp * (_dot_nt(dob, vp) - delta)
                dsc = pc * (_dot_nt(dob, vc) - delta)
                db_scr[hh, :, :TQ] += dsp
                db_scr[hh, :, TQ:] += dsc
                dspb, dscb = dsp.astype(BF16), dsc.astype(BF16)
                dq_ref[:, cs] = (scale * (_dot(dspb, kp) + _dot(dscb, kc))).astype(BF16)
                qt, dot_ = q.T, dob.T
                dk_ref[:, cs] = (ckt_scr[cs, :] + scale * _dot(qt, dspb)).T.astype(BF16)
                dv_ref[:, cs] = (cvt_scr[cs, :] + _dot(dot_, ppb)).T.astype(BF16)
                ckt_scr[cs, :] = scale * _dot(qt, dscb)
                cvt_scr[cs, :] = _dot(dot_, pcb)

        @pl.when(i == nq)
        def _():
            dk_ref[...] = ckt_scr[...].T.astype(BF16)
            dv_ref[...] = cvt_scr[...].T.astype(BF16)
            r0 = lax.broadcasted_iota(jnp.int32, (TQ, TQ), 0)
            r1 = lax.broadcasted_iota(jnp.int32, (TQ, TQ), 1)
            flip = (r0 + r1 == TQ - 1).astype(BF16)
            for hh in range(A_HPS):
                fl = _dot3(flip, db_scr[hh])
                rolled = pltpu.roll(fl, 0, 1, stride=1, stride_axis=0)
                du_ref[hh] = jnp.sum(rolled, axis=0, keepdims=True)

    specs = _band_specs(nq)
    row_spec = pl.BlockSpec((TQ, A_HW), lambda h, i: (jnp.minimum(i, nq - 1), h))
    key_spec = pl.BlockSpec((TQ, A_HW), lambda h, i: (jnp.maximum(i - 1, 0), h))
    out_sd = jax.ShapeDtypeStruct((S, A_WIDTH), BF16)
    return pl.pallas_call(
        body,
        name="band_bwd",
        grid=(A_HEADS // A_HPS, nq + 1),
        in_specs=[specs[0], specs[1], specs[2], specs[3], specs[4], specs[5], specs[6], row_spec],
        out_specs=[row_spec, key_spec, key_spec, row_spec, pl.BlockSpec((A_HPS, 1, 2 * TQ), lambda h, i: (h, 0, 0))],
        out_shape=[out_sd, out_sd, out_sd, out_sd, jax.ShapeDtypeStruct((A_HEADS, 1, 2 * TQ), F32)],
        scratch_shapes=[pltpu.VMEM((A_HPS, TQ, 2 * TQ), F32), pltpu.VMEM((A_HPS, TQ, 2 * TQ), F32),
                        pltpu.VMEM((A_HW, TQ), F32), pltpu.VMEM((A_HW, TQ), F32)],
        compiler_params=_params(("parallel", "arbitrary")),
    )(h, h, h, h, h, h, u, dycat)


def _bias_by_offset(table):
    far = jnp.broadcast_to(table[:, N_REL - 1:], (A_HEADS, TQ - MAX_REL))
    ramp = jnp.flip(table, axis=1)
    rest = jnp.broadcast_to(table[:, :1], (A_HEADS, 2 * TQ - CHUNK - (TQ + MAX_REL + 1)))
    wrap = jnp.broadcast_to(table[:, N_REL - 1:], (A_HEADS, CHUNK))
    return jnp.concatenate([far, ramp, rest, wrap], axis=1)[:, None, :]


def _bias_grad_from_offset(du):
    g = jnp.roll(du[:, 0, :], -(TQ - 1), axis=1)
    far = jnp.sum(g[:, :TQ - MAX_REL], axis=1) + jnp.sum(g[:, 2 * TQ - CHUNK:], axis=1)
    ramp = jnp.flip(g[:, TQ - MAX_REL:TQ + MAX_REL + 1], axis=1)
    return ramp.at[:, N_REL - 1].add(far)


def _mem_probs(q, mk):
    s = _dot_nt(q, mk) * (HEAD_DIM ** -0.5)
    p = jnp.exp(s - jnp.max(s, axis=1, keepdims=True))
    return p * (1.0 / jnp.sum(p, axis=1, keepdims=True))


def _mem_fwd(h, mkv, ycat, ycat_t):
    S = h.shape[0]
    nm = mkv.shape[0]
    c0 = (A_WIDTH + B_WIDTH) // LANES

    def body(q_ref, z_ref, mk_ref, mv_ref, yin_ref, ytin_ref, y_ref, yt_ref):
        del yin_ref, ytin_ref
        p = _mem_probs(q_ref[...], mk_ref[...])
        o = _dot(p.astype(BF16), mv_ref[...])
        z = z_ref[...].astype(F32)
        y = o * (z * _sigmoid(z))
        y_ref[...] = y.astype(BF16)
        yt_ref[...] = y.T.astype(BF16)

    return pl.pallas_call(
        body,
        name="mem_fwd",
        grid=(M_HEADS, S // TQ),
        in_specs=[pl.BlockSpec((TQ, LANES), lambda h, i: (i, C_MQ // LANES + h)),
                  pl.BlockSpec((TQ, LANES), lambda h, i: (i, C_MZ // LANES + h)),
                  pl.BlockSpec((nm, LANES), lambda h, i: (0, h)),
                  pl.BlockSpec((nm, LANES), lambda h, i: (0, M_HEADS + h)), ANY, ANY],
        out_specs=[pl.BlockSpec((TQ, LANES), lambda h, i: (i, c0 + h)), pl.BlockSpec((LANES, TQ), lambda h, i: (c0 + h, i))],
        out_shape=[jax.ShapeDtypeStruct(ycat.shape, BF16), jax.ShapeDtypeStruct(ycat_t.shape, BF16)],
        input_output_aliases={4: 0, 5: 1},
        compiler_params=_params(("parallel", "arbitrary")),
    )(h, h, mkv, mkv, ycat, ycat_t)


def _mem_bwd(h, mkv, dycat):
    S = h.shape[0]
    nm = mkv.shape[0]
    scale = HEAD_DIM ** -0.5

    def body(q_ref, z_ref, mk_ref, mv_ref, dy_ref, dq_ref, dz_ref, dmk_ref, dmv_ref):
        i = pl.program_id(1)
        q, mk, mv = q_ref[...], mk_ref[...], mv_ref[...]
        p = _mem_probs(q, mk)
        pb = p.astype(BF16)
        o = _dot(pb, mv)
        z = z_ref[...].astype(F32)
        sg = _sigmoid(z)
        dy = dy_ref[...].astype(F32)
        do = dy * (z * sg)
        dz_ref[...] = (dy * o * (sg * (1.0 + z * (1.0 - sg)))).astype(BF16)
        dob = do.astype(BF16)
        ds = p * (_dot_nt(dob, mv) - jnp.sum(do * o, axis=1, keepdims=True))
        dsb = ds.astype(BF16)
        dq_ref[...] = (scale * _dot(dsb, mk)).astype(BF16)
        dmk = scale * _dot_tn(dsb, q)
        dmv = _dot_tn(pb, dob)

        @pl.when(i == 0)
        def _():
            dmk_ref[...] = dmk
            dmv_ref[...] = dmv

        @pl.when(i > 0)
        def _():
            dmk_ref[...] += dmk
            dmv_ref[...] += dmv

    row = pl.BlockSpec((TQ, LANES), lambda h, i: (i, h))
    out_sd = jax.ShapeDtypeStruct((S, M_WIDTH), BF16)
    dq, dz, dmkv, dmkv2 = pl.pallas_call(
        body,
        name="mem_bwd",
        grid=(M_HEADS, S // TQ),
        in_specs=[pl.BlockSpec((TQ, LANES), lambda h, i: (i, C_MQ // LANES + h)),
                  pl.BlockSpec((TQ, LANES), lambda h, i: (i, C_MZ // LANES + h)),
                  pl.BlockSpec((nm, LANES), lambda h, i: (0, h)),
                  pl.BlockSpec((nm, LANES), lambda h, i: (0, M_HEADS + h)),
                  pl.BlockSpec((TQ, LANES), lambda h, i: (i, (A_WIDTH + B_WIDTH) // LANES + h))],
        out_specs=[row, row,
                   pl.BlockSpec((nm, LANES), lambda h, i: (0, h)),
                   pl.BlockSpec((nm, LANES), lambda h, i: (0, h))],
        out_shape=[out_sd, out_sd, jax.ShapeDtypeStruct((nm, M_WIDTH), F32), jax.ShapeDtypeStruct((nm, M_WIDTH), F32)],
        compiler_params=_params(("parallel", "arbitrary")),
    )(h, h, mkv, mkv, dycat)
    return dq, dz, jnp.concatenate([dmkv, dmkv2], axis=1)


def _chunk_masks():
    r = lax.broadcasted_iota(jnp.int32, (TQ, TQ), 0)
    c = lax.broadcasted_iota(jnp.int32, (TQ, TQ), 1)
    same = _chunk_of(r) == _chunk_of(c)
    return jnp.logical_and(same, c <= r), jnp.logical_and(same, c > r)


def _gla_gates(lr, gw, gb):
    logit = _dot(lr, gw) + gb
    sg = _sigmoid(logit)
    g = (jnp.minimum(logit, 0.0) - jnp.log(1.0 + jnp.exp(-jnp.abs(logit)))) * (1.0 / GATE_TAU)
    lo, _ = _chunk_masks()
    return sg, _dot3(lo.astype(BF16), g)


def _gla_factors(q, k, b):
    eb = jnp.exp(b)
    enb = jnp.exp(-b)
    return eb, enb, q * eb, q * enb, k * eb, k * enb


def _gla_intra(qp, qn, kp, kn):
    lo, up = _chunk_masks()
    return (jnp.where(lo, _dot_nt(qp.astype(BF16), kn.astype(BF16)), 0.0)
            + jnp.where(up, _dot_nt(qn.astype(BF16), kp.astype(BF16)), 0.0))


def _gla_specs(nb, rev):
    blk = (lambda i: nb - 1 - i) if rev else (lambda i: i)

    def at(c0):
        return pl.BlockSpec((TQ, LANES), lambda i, h: (blk(i), c0 // LANES + h))

    lr_spec = pl.BlockSpec((TQ, LANES), lambda i, h: (blk(i), C_LR // LANES))
    gw_spec = pl.BlockSpec((LANES, LANES), lambda i, h: (0, h))
    gb_spec = pl.BlockSpec((1, LANES), lambda i, h: (0, h))
    gn_spec = pl.BlockSpec((1, LANES), lambda i, h: (0, 0))
    return at(C_BQ), at(C_BK), at(C_BV), at(C_BZ), lr_spec, gw_spec, gb_spec, gn_spec, blk


def _gla_fwd(h, gw, gb, gn, ycat, ycat_t):
    S = h.shape[0]
    nb = S // TQ
    c0 = A_WIDTH // LANES

    def body(q_ref, k_ref, v_ref, z_ref, lr_ref, gw_ref, gb_ref, gn_ref, yin_ref, ytin_ref,
             y_ref, yt_ref, o_ref, st_ref, st_scr):
        del yin_ref, ytin_ref
        i, hd = pl.program_id(0), pl.program_id(1)

        @pl.when(i == 0)
        def _():
            st_scr[hd] = jnp.zeros((LANES, LANES), F32)

        q = q_ref[...].astype(F32) * (B_DK ** -0.5)
        k = k_ref[...].astype(F32)
        v = v_ref[...]
        _, b = _gla_gates(lr_ref[...], gw_ref[...], gb_ref[...])
        _, _, qp, qn, kp, kn = _gla_factors(q, k, b)
        o_intra = _dot(_gla_intra(qp, qn, kp, kn).astype(BF16), v)
        qpb, knb = qp.astype(BF16), kn.astype(BF16)
        st = st_scr[hd]
        outs = []
        for c in range(CPB):
            rows = slice(c * CHUNK, (c + 1) * CHUNK)
            st_ref[0, c] = st
            outs.append(_dot_nt(qpb[rows], st.astype(BF16)))
            e_last = jnp.exp(b[(c + 1) * CHUNK - 1:(c + 1) * CHUNK, :])
            st = (st + _dot_tn(v[rows], knb[rows])) * e_last
        st_scr[hd] = st
        o = o_intra + jnp.concatenate(outs, axis=0)
        o_ref[...] = o
        r = lax.rsqrt(jnp.mean(o * o, axis=1, keepdims=True) + RMS_EPS)
        z = z_ref[...].astype(F32)
        y = o * r * gn_ref[...] * (z * _sigmoid(z))
        y_ref[...] = y.astype(BF16)
        yt_ref[...] = y.T.astype(BF16)

    q_s, k_s, v_s, z_s, lr_s, gw_s, gb_s, gn_s, _ = _gla_specs(nb, False)
    row = pl.BlockSpec((TQ, LANES), lambda i, h: (i, h))
    return pl.pallas_call(
        body,
        name="gla_fwd",
        grid=(nb, B_HEADS),
        in_specs=[q_s, k_s, v_s, z_s, lr_s, gw_s, gb_s, gn_s, ANY, ANY],
        out_specs=[pl.BlockSpec((TQ, LANES), lambda i, h: (i, c0 + h)), pl.BlockSpec((LANES, TQ), lambda i, h: (c0 + h, i)),
                   row, pl.BlockSpec((1, CPB, LANES, LANES), lambda i, h: (h, i, 0, 0))],
        out_shape=[jax.ShapeDtypeStruct(ycat.shape, BF16), jax.ShapeDtypeStruct(ycat_t.shape, BF16),
                   jax.ShapeDtypeStruct((S, B_WIDTH), F32),
                   jax.ShapeDtypeStruct((B_HEADS, S // CHUNK, LANES, LANES), F32)],
        input_output_aliases={8: 0, 9: 1},
        scratch_shapes=[pltpu.VMEM((B_HEADS, LANES, LANES), F32)],
        compiler_params=_params(("arbitrary", "arbitrary")),
    )(h, h, h, h, h, gw, gb, gn, ycat, ycat_t)


def _gla_bwd(h, gw, gb, gn, o_pre, states, dycat):
    S = h.shape[0]
    nb = S // TQ

    def body(q_ref, k_ref, v_ref, z_ref, lr_ref, gw_ref, gb_ref, gn_ref, o_ref, st_ref, dy_ref,
             dq_ref, dk_ref, dv_ref, dz_ref, dlr_ref, dgw_ref, dgb_ref, dgn_ref,
             dst_scr, dgw_scr, dgb_scr, dgn_scr):
        i, hd = pl.program_id(0), pl.program_id(1)

        @pl.when(i == 0)
        def _():
            dst_scr[hd] = jnp.zeros((LANES, LANES), F32)
            dgw_scr[hd] = jnp.zeros((LANES, LANES), F32)
            dgb_scr[hd] = jnp.zeros((1, LANES), F32)

        @pl.when(jnp.logical_and(i == 0, hd == 0))
        def _():
            dgn_scr[...] = jnp.zeros_like(dgn_scr)

        q = q_ref[...].astype(F32) * (B_DK ** -0.5)
        k = k_ref[...].astype(F32)
        v = v_ref[...]
        lr, gwv = lr_ref[...], gw_ref[...]
        sg, b = _gla_gates(lr, gwv, gb_ref[...])
        eb, enb, qp, qn, kp, kn = _gla_factors(q, k, b)
        a = _gla_intra(qp, qn, kp, kn)
        qpb, qnb, kpb, knb = qp.astype(BF16), qn.astype(BF16), kp.astype(BF16), kn.astype(BF16)

        o = o_ref[...]
        gn = gn_ref[...]
        r = lax.rsqrt(jnp.mean(o * o, axis=1, keepdims=True) + RMS_EPS)
        z = z_ref[...].astype(F32)
        sz = _sigmoid(z)
        dy = dy_ref[...].astype(F32)
        d_on = dy * (z * sz)
        dz_ref[...] = (dy * (o * r * gn) * (sz * (1.0 + z * (1.0 - sz)))).astype(BF16)
        dgn_scr[...] += jnp.sum(d_on * o * r, axis=0, keepdims=True)
        t = d_on * gn
        do = r * t - o * (r * r * r) * jnp.mean(t * o, axis=1, keepdims=True)
        dob = do.astype(BF16)

        lo, up = _chunk_masks()
        da = _dot_nt(dob, v)
        dalo = jnp.where(lo, da, 0.0).astype(BF16)
        daup = jnp.where(up, da, 0.0).astype(BF16)
        dqp = _dot(dalo, knb)
        dkn = _dot_tn(dalo, qpb)
        dqn = _dot(daup, kpb)
        dkp = _dot_tn(daup, qnb)
        dv = _dot_tn(a.astype(BF16), dob)

        dst = dst_scr[hd]
        dqp_c, dkn_c, dv_c, dbl_c = [None] * CPB, [None] * CPB, [None] * CPB, [None] * CPB
        for c in reversed(range(CPB)):
            rows = slice(c * CHUNK, (c + 1) * CHUNK)
            st = st_ref[0, c]
            e_last = jnp.exp(b[(c + 1) * CHUNK - 1:(c + 1) * CHUNK, :])
            if c == CPB - 1:
                st_next = (st + _dot_tn(v[rows], knb[rows])) * e_last
            else:
                st_next = st_ref[0, c + 1]
            dbl_c[c] = jnp.sum(dst * st_next, axis=0, keepdims=True)
            dtt = (dst * e_last).astype(BF16)
            dv_c[c] = _dot_nt(knb[rows], dtt)
            dkn_c[c] = _dot(v[rows], dtt)
            dqp_c[c] = _dot(dob[rows], st.astype(BF16))
            dst = _dot_tn(dob[rows], qpb[rows]) + dst * e_last
        dst_scr[hd] = dst
        dqp = dqp + jnp.concatenate(dqp_c, axis=0)
        dkn = dkn + jnp.concatenate(dkn_c, axis=0)
        dv = dv + jnp.concatenate(dv_c, axis=0)
        dv_ref[...] = dv.astype(BF16)
        dq_ref[...] = ((dqp * eb + dqn * enb) * (B_DK ** -0.5)).astype(BF16)
        dk_ref[...] = (dkp * eb + dkn * enb).astype(BF16)

        last = jnp.bitwise_and(lax.broadcasted_iota(jnp.int32, (TQ, 1), 0), CHUNK - 1) == CHUNK - 1
        dbl = jnp.concatenate([jnp.broadcast_to(x, (CHUNK, LANES)) for x in dbl_c], axis=0)
        db = dqp * qp - dqn * qn + dkp * kp - dkn * kn + jnp.where(last, dbl, 0.0)
        r0 = lax.broadcasted_iota(jnp.int32, (TQ, TQ), 0)
        r1 = lax.broadcasted_iota(jnp.int32, (TQ, TQ), 1)
        upper = jnp.logical_and(_chunk_of(r0) == _chunk_of(r1), r1 >= r0).astype(BF16)
        dlogit = _dot3(upper, db) * (1.0 / GATE_TAU) * (1.0 - sg)
        dlb = dlogit.astype(BF16)
        dlr = _dot_nt(dlb, gwv)

        @pl.when(hd == 0)
        def _():
            dlr_ref[...] = dlr

        @pl.when(hd > 0)
        def _():
            dlr_ref[...] += dlr

        dgw_scr[hd] += _dot_tn(lr, dlb)
        dgb_scr[hd] += jnp.sum(dlogit, axis=0, keepdims=True)

        @pl.when(i == nb - 1)
        def _():
            dgw_ref[...] = dgw_scr[hd]
            dgb_ref[...] = dgb_scr[hd]
            dgn_ref[...] = dgn_scr[...]

    q_s, k_s, v_s, z_s, lr_s, gw_s, gb_s, gn_s, blk = _gla_specs(nb, True)
    row = pl.BlockSpec((TQ, LANES), lambda i, h: (blk(i), h))
    dy_spec = pl.BlockSpec((TQ, LANES), lambda i, h: (blk(i), A_WIDTH // LANES + h))
    st_spec = pl.BlockSpec((1, CPB, LANES, LANES), lambda i, h: (h, blk(i), 0, 0))
    out_sd = jax.ShapeDtypeStruct((S, B_WIDTH), BF16)
    return pl.pallas_call(
        body,
        name="gla_bwd",
        grid=(nb, B_HEADS),
        in_specs=[q_s, k_s, v_s, z_s, lr_s, gw_s, gb_s, gn_s, row, st_spec, dy_spec],
        out_specs=[row, row, row, row,
                   pl.BlockSpec((TQ, LANES), lambda i, h: (blk(i), 0)),
                   pl.BlockSpec((LANES, LANES), lambda i, h: (0, jnp.where(i == nb - 1, h, 0))),
                   pl.BlockSpec((1, LANES), lambda i, h: (0, jnp.where(i == nb - 1, h, 0))),
                   pl.BlockSpec((1, LANES), lambda i, h: (0, 0))],
        out_shape=[out_sd, out_sd, out_sd, out_sd,
                   jax.ShapeDtypeStruct((S, LANES), F32),
                   jax.ShapeDtypeStruct((LANES, B_HEADS * LANES), F32),
                   jax.ShapeDtypeStruct((1, B_HEADS * LANES), F32),
                   jax.ShapeDtypeStruct((1, LANES), F32)],
        scratch_shapes=[pltpu.VMEM((B_HEADS, LANES, LANES), F32), pltpu.VMEM((B_HEADS, LANES, LANES), F32),
                        pltpu.VMEM((B_HEADS, 1, LANES), F32), pltpu.VMEM((1, LANES), F32)],
        compiler_params=_params(("arbitrary", "arbitrary")),
    )(h, h, h, h, h, gw, gb, gn, o_pre, states, dycat)


LN_ROWS = 256


def _outproj_ln(ycat, w_out, x, g, b):
    S = x.shape[0]

    def body(yc_ref, w_ref, x_ref, g_ref, b_ref, y_ref, yb_ref, yt_ref, xh_ref, rs_ref):
        u = ALPHA * x_ref[...] + _dot(yc_ref[...], w_ref[...])
        mu = jnp.mean(u, axis=1, keepdims=True)
        d = u - mu
        rstd = lax.rsqrt(jnp.mean(d * d, axis=1, keepdims=True) + LN_EPS)
        xh = d * rstd
        y = xh * g_ref[...] + b_ref[...]
        y_ref[...] = y
        yb_ref[...] = y.astype(BF16)
        yt_ref[...] = y.T.astype(BF16)
        xh_ref[...] = xh
        rs_ref[...] = rstd

    row = lambda w: pl.BlockSpec((LN_ROWS, w), lambda i: (i, 0))
    vec = pl.BlockSpec((1, D_MODEL), lambda i: (0, 0))
    return pl.pallas_call(
        body,
        name="outproj_ln",
        grid=(S // LN_ROWS,),
        in_specs=[row(D_MODEL), pl.BlockSpec((D_MODEL, D_MODEL), lambda i: (0, 0)), row(D_MODEL), vec, vec],
        out_specs=[row(D_MODEL), row(D_MODEL), pl.BlockSpec((D_MODEL, LN_ROWS), lambda i: (0, i)), row(D_MODEL), row(1)],
        out_shape=[jax.ShapeDtypeStruct((S, D_MODEL), F32), jax.ShapeDtypeStruct((S, D_MODEL), BF16),
                   jax.ShapeDtypeStruct((D_MODEL, S), BF16),
                   jax.ShapeDtypeStruct((S, D_MODEL), F32), jax.ShapeDtypeStruct((S, 1), F32)],
        compiler_params=_params(("parallel",)),
    )(ycat, w_out, x, g, b)


def _ln_bwd(dy, xhat, rstd, g):
    S = dy.shape[0]

    def body(dy_ref, xh_ref, rs_ref, g_ref, du_ref, dub_ref, dg_ref, db_ref):
        i = pl.program_id(0)
        dy_, xh = dy_ref[...], xh_ref[...]
        dyg = dy_ * g_ref[...]
        m1 = jnp.mean(dyg, axis=1, keepdims=True)
        m2 = jnp.mean(dyg * xh, axis=1, keepdims=True)
        du = rs_ref[...] * (dyg - m1 - xh * m2)
        du_ref[...] = du
        dub_ref[...] = du.astype(BF16)
        dg = jnp.sum(dy_ * xh, axis=0, keepdims=True)
        db = jnp.sum(dy_, axis=0, keepdims=True)

        @pl.when(i == 0)
        def _():
            dg_ref[...] = dg
            db_ref[...] = db

        @pl.when(i > 0)
        def _():
            dg_ref[...] += dg
            db_ref[...] += db

    row = lambda w: pl.BlockSpec((TQ, w), lambda i: (i, 0))
    vec = pl.BlockSpec((1, D_MODEL), lambda i: (0, 0))
    return pl.pallas_call(
        body,
        name="ln_bwd",
        grid=(S // TQ,),
        in_specs=[row(D_MODEL), row(D_MODEL), row(1), vec],
        out_specs=[row(D_MODEL), row(D_MODEL), vec, vec],
        out_shape=[jax.ShapeDtypeStruct((S, D_MODEL), F32), jax.ShapeDtypeStruct((S, D_MODEL), BF16),
                   jax.ShapeDtypeStruct((1, D_MODEL), F32), jax.ShapeDtypeStruct((1, D_MODEL), F32)],
        compiler_params=_params(("arbitrary",)),
    )(dy, xhat, rstd, g)


def _loss_head(y, target):
    S = y.shape[0]

    def body(y_ref, t_ref, l_ref, dy_ref):
        i = pl.program_id(0)
        err = y_ref[...] - t_ref[...]
        dy_ref[...] = err * (1.0 / D_MODEL)
        part = (0.5 / D_MODEL) * jnp.sum(jnp.sum(err * err, axis=1, keepdims=True), axis=0, keepdims=True)

        @pl.when(i == 0)
        def _():
            l_ref[...] = part

        @pl.when(i > 0)
        def _():
            l_ref[...] += part

    row = pl.BlockSpec((TQ, D_MODEL), lambda i: (i, 0))
    return pl.pallas_call(
        body,
        name="loss_head",
        grid=(S // TQ,),
        in_specs=[row, row],
        out_specs=[pl.BlockSpec((1, 1), lambda i: (0, 0)), row],
        out_shape=[jax.ShapeDtypeStruct((1, 1), F32), jax.ShapeDtypeStruct((S, D_MODEL), F32)],
        compiler_params=_params(("arbitrary",)),
    )(y, target)


def _pad_gate(gate_w, gate_b):
    gw = gate_w.reshape(GATE_RANK, B_HEADS, B_DK)
    gw = jnp.pad(gw, ((0, LANES - GATE_RANK), (0, 0), (0, LANES - B_DK))).reshape(LANES, B_HEADS * LANES)
    gb = jnp.pad(gate_b.reshape(B_HEADS, B_DK), ((0, 0), (0, LANES - B_DK))).reshape(1, B_HEADS * LANES)
    return gw.astype(BF16), gb.astype(F32)


def _layer_fwd(x, xb, xt, mem_b, w_in, w_kv, w_out, u, gw, gb, gn, ln_g, ln_b):
    h = _matmul(xb, w_in, mode="nn", out_dtype=BF16, tm=1024, tn=768, tk=D_MODEL, name="in_proj")
    mkv = _matmul(mem_b, w_kv, mode="nn", out_dtype=BF16, tm=mem_b.shape[0], tn=1024, tk=D_MODEL, name="mem_kv")
    ycat, ycat_t = _band_fwd(h, u)
    ycat, ycat_t, o_pre, states = _gla_fwd(h, gw, gb, gn, ycat, ycat_t)
    ycat, ycat_t = _mem_fwd(h, mkv, ycat, ycat_t)
    y, ybf, yt, xhat, rstd = _outproj_ln(ycat, w_out, x, ln_g, ln_b)
    return y, ybf, yt, (xt, h, mkv, ycat_t, o_pre, states, xhat, rstd)


def _layer_bwd(dy, saved, mem_b, w_in, w_out, u, gw, gb, gn, ln_g):
    xt, h, mkv, ycat_t, o_pre, states, xhat, rstd = saved
    S = dy.shape[0]
    du, dub, d_ln_g, d_ln_b = _ln_bwd(dy, xhat, rstd, ln_g)
    dycat = _matmul(dub, w_out, mode="nt", out_dtype=BF16, tm=1024, tn=1024, tk=D_MODEL, name="dycat")
    d_w_out = _matmul(ycat_t, dub, mode="nn", out_dtype=F32, tm=1024, tn=1024, tk=2048, name="d_w_out")
    daq, dak, dav, daz, d_u = _band_bwd(h, u, dycat)
    dbq, dbk, dbv, dbz, dlr, dgw, dgb, dgn = _gla_bwd(h, gw, gb, gn, o_pre, states, dycat)
    dmq, dmz, dmkv = _mem_bwd(h, mkv, dycat)
    d_w_kv = _matmul(mem_b, dmkv, mode="tn", out_dtype=F32, tm=1024, tn=1024, tk=mem_b.shape[0], name="d_w_kv")
    dh = jnp.concatenate([daq, dak, dav, daz, dbq, dbk, dbv, dbz, dmq, dmz, dlr.astype(BF16),
                          jnp.zeros((S, HP - C_LR - LANES), BF16)], axis=1)
    dx = _matmul(dh, w_in, mode="nt", out_dtype=F32, tm=1024, tn=1024, tk=2560, name="dx", add=du, add_scale=ALPHA)
    d_w_in = _matmul(xt, dh, mode="nn", out_dtype=F32, tm=1024, tn=768, tk=2048, name="d_w_in")
    return dx, (d_w_in, d_u, dgw, dgb, dgn, d_w_kv, d_w_out, d_ln_g, d_ln_b)


def _pad_heads(w):
    r = w.shape[0]
    return jnp.pad(w.reshape(r, B_HEADS, B_DK), ((0, 0), (0, 0), (0, LANES - B_DK))).reshape(r, B_HEADS * LANES)


def _unpad_heads(w):
    r = w.shape[0]
    return w.reshape(r, B_HEADS, LANES)[:, :, :B_DK].reshape(r, B_KEY_WIDTH)


O_BQ = 4 * A_WIDTH
O_BK = O_BQ + B_KEY_WIDTH
O_BV = O_BK + B_KEY_WIDTH
O_LR = O_BV + 2 * B_WIDTH
O_MQ = O_LR + GATE_RANK


def _to_padded(w):
    r = w.shape[0]
    return jnp.concatenate([
        w[:, :O_BQ], _pad_heads(w[:, O_BQ:O_BK]), _pad_heads(w[:, O_BK:O_BV]), w[:, O_BV:O_LR], w[:, O_MQ:],
        w[:, O_LR:O_MQ], jnp.zeros((r, HP - C_LR - GATE_RANK), w.dtype)], axis=1)


def _from_padded(g):
    return jnp.concatenate([
        g[:, :C_BQ], _unpad_heads(g[:, C_BQ:C_BK]), _unpad_heads(g[:, C_BK:C_BV]), g[:, C_BV:C_MQ],
        g[:, C_LR:C_LR + GATE_RANK], g[:, C_MQ:C_LR]], axis=1)


def _padded_col_of():
    col = np.zeros(IN_WIDTH, np.int64)
    col[:O_BQ] = np.arange(O_BQ)
    for o0, c0 in ((O_BQ, C_BQ), (O_BK, C_BK)):
        for hd in range(B_HEADS):
            col[o0 + hd * B_DK:o0 + (hd + 1) * B_DK] = c0 + hd * LANES + np.arange(B_DK)
    col[O_BV:O_LR] = C_BV + np.arange(O_LR - O_BV)
    col[O_LR:O_MQ] = C_LR + np.arange(GATE_RANK)
    col[O_MQ:] = C_MQ + np.arange(IN_WIDTH - O_MQ)
    return col


def _runs(idx):
    out, start = [], 0
    for k in range(1, len(idx) + 1):
        if k == len(idx) or idx[k] != idx[k - 1] + 1:
            out.append((int(idx[start]), k - start))
            start = k
    return out


def _chip_columns(g, j, n):
    runs = _runs(_padded_col_of()[j * n:(j + 1) * n])
    return jnp.concatenate([g[:, a:a + ln] for a, ln in runs], axis=1)


def _padded_from_shards(shards):
    n = shards[0].shape[1]
    src = np.full(HP, -1, np.int64)
    src[_padded_col_of()] = np.arange(IN_WIDTH)
    parts, k = [], 0
    while k < HP:
        e = k + 1
        if src[k] < 0:
            while e < HP and src[e] < 0:
                e += 1
            parts.append(jnp.zeros((shards[0].shape[0], e - k), shards[0].dtype))
        else:
            while e < HP and src[e] == src[e - 1] + 1 and src[e] // n == src[k] // n:
                e += 1
            parts.append(shards[src[k] // n][:, src[k] % n:src[k] % n + e - k])
        k = e
    return jnp.concatenate(parts, axis=1)


def _adamw(w, g, m, v, name):
    L, R, C = w.shape
    tr = R
    for cand in (256, 128, 64, 32, 16, 8):
        if R % cand == 0 and R > cand:
            tr = cand
            break

    def body(w_ref, g_ref, m_ref, v_ref, d_ref, nm_ref, nv_ref):
        g_ = g_ref[...]
        nm = ADAM_B1 * m_ref[...] + (1.0 - ADAM_B1) * g_
        nv = ADAM_B2 * v_ref[...] + (1.0 - ADAM_B2) * (g_ * g_)
        m_hat = nm / (1.0 - ADAM_B1 ** ADAM_STEP)
        v_hat = nv / (1.0 - ADAM_B2 ** ADAM_STEP)
        d_ref[...] = -ADAM_LR * (m_hat / (jnp.sqrt(v_hat) + ADAM_EPS) + ADAM_WD * w_ref[...])
        nm_ref[...] = nm
        nv_ref[...] = nv

    spec = pl.BlockSpec((1, tr, C), lambda l, i: (l, i, 0))
    sd = jax.ShapeDtypeStruct((L, R, C), F32)
    return pl.pallas_call(
        body, name=name, grid=(L, R // tr), in_specs=[spec] * 4, out_specs=[spec] * 3, out_shape=[sd] * 3,
        compiler_params=_params(("parallel", "parallel")),
    )(w, g, m, v)


def _adamw_nd(w, g, m, v, name):
    shape = w.shape
    f = (lambda a: a) if w.ndim == 3 else (lambda a: a.reshape(1, shape[0], shape[1]))
    return tuple(o.reshape(shape) for o in _adamw(f(w), f(g), f(m), f(v), name))


ANY = pl.BlockSpec(memory_space=pl.ANY)


def _place():
    x, y, c = lax.axis_index("x"), lax.axis_index("y"), lax.axis_index("c")
    chips = [(1 - x, y), (x, 1 - y), (1 - x, 1 - y)]
    return x, y, c, chips


def _gather_chips(shard, name):
    R, C = shard.shape
    half = R // 2
    assert half % 16 == 0

    def body(s_ref, o_ref, send, recv):
        x, y, c, chips = _place()
        mine = pl.ds(pl.multiple_of(c * half, 16), half)
        other = pl.ds(pl.multiple_of((1 - c) * half, 16), half)

        def cp(k, src, chip, rows, to):
            dst = o_ref.at[2 * chip[0] + chip[1], rows]
            return pltpu.make_async_remote_copy(src_ref=dst if src is None else src, dst_ref=dst, send_sem=send.at[k],
                                                recv_sem=recv.at[k], device_id=to, device_id_type=MESH)

        first = [cp(k, s_ref.at[mine], (x, y), mine, (*chip, c)) for k, chip in enumerate(chips)]
        for f in first:
            f.start()
        passed = [cp(3 + k, None, chip, mine, (x, y, 1 - c)) for k, chip in enumerate(chips)]
        for k, chip in enumerate(chips):
            cp(k, None, chip, mine, (x, y, c)).wait_recv()
            passed[k].start()
        for k, chip in enumerate(chips):
            cp(3 + k, None, chip, other, (x, y, c)).wait_recv()
        for f in first + passed:
            f.wait_send()

    return pl.pallas_call(
        body, name=name, in_specs=[ANY], out_specs=ANY,
        out_shape=jax.ShapeDtypeStruct((N_CHIPS, R, C), shard.dtype),
        scratch_shapes=[pltpu.SemaphoreType.DMA((6,)), pltpu.SemaphoreType.DMA((6,))],
    )(shard)


def _pair_exchange(parts, half, L, name):
    n = len(parts) // L
    C = parts[0][0].shape[1]

    def body(*refs):
        ins, (o_ref, send, recv) = refs[:len(parts)], refs[len(parts):]
        x, y, c, _ = _place()
        cps = []
        for k, (_, first) in enumerate(parts):
            rows = pl.ds(pl.multiple_of(first + (1 - c) * half, 8), half)
            cps.append(pltpu.make_async_remote_copy(
                src_ref=ins[k].at[rows], dst_ref=o_ref.at[k // L, k % L], send_sem=send.at[k], recv_sem=recv.at[k],
                device_id=(x, y, 1 - c), device_id_type=MESH))
        for cp in cps:
            cp.start()
        for cp in cps:
            cp.wait()

    return pl.pallas_call(
        body, name=name, in_specs=[ANY] * len(parts), out_specs=ANY,
        out_shape=jax.ShapeDtypeStruct((n, L, half, C), F32),
        scratch_shapes=[pltpu.SemaphoreType.DMA((len(parts),)), pltpu.SemaphoreType.DMA((len(parts),))],
    )(*[a for a, _ in parts])


def _chip_exchange(p, name):
    n, R, C = p.shape

    def body(p_ref, o_ref, send, recv):
        x, y, c, chips = _place()
        me = 2 * x + y
        cps = [pltpu.make_async_remote_copy(src_ref=p_ref.at[2 * chip[0] + chip[1]], dst_ref=o_ref.at[me],
                                            send_sem=send.at[k], recv_sem=recv.at[k], device_id=(*chip, c),
                                            device_id_type=MESH) for k, chip in enumerate(chips)]
        for cp in cps:
            cp.start()
        for k, chip in enumerate(chips):
            pltpu.make_async_remote_copy(src_ref=p_ref.at[me], dst_ref=o_ref.at[2 * chip[0] + chip[1]],
                                         send_sem=send.at[k], recv_sem=recv.at[k], device_id=(*chip, c),
                                         device_id_type=MESH).wait_recv()
        for cp in cps:
            cp.wait_send()

    return pl.pallas_call(
        body, name=name, in_specs=[ANY], out_specs=ANY, out_shape=jax.ShapeDtypeStruct((n, R, C), p.dtype),
        scratch_shapes=[pltpu.SemaphoreType.DMA((3,)), pltpu.SemaphoreType.DMA((3,))],
    )(p)


def _pair_gather(t2, name):
    def body(t_ref, o_ref, send, recv):
        del t_ref
        x, y, c, _ = _place()
        cp = pltpu.make_async_remote_copy(src_ref=o_ref.at[c], dst_ref=o_ref.at[c], send_sem=send, recv_sem=recv,
                                          device_id=(x, y, 1 - c), device_id_type=MESH)
        cp.start()
        pltpu.make_async_remote_copy(src_ref=o_ref.at[c], dst_ref=o_ref.at[1 - c], send_sem=send, recv_sem=recv,
                                     device_id=(x, y, 1 - c), device_id_type=MESH).wait_recv()
        cp.wait_send()

    return pl.pallas_call(
        body, name=name, in_specs=[ANY], out_specs=ANY, out_shape=jax.ShapeDtypeStruct(t2.shape, t2.dtype),
        input_output_aliases={0: 0},
        scratch_shapes=[pltpu.SemaphoreType.DMA, pltpu.SemaphoreType.DMA],
    )(t2)


def _add_halves(parts, got, c_idx, name):
    n, L, half, C = got.shape
    tr = 64

    def body(*refs):
        ins, (got_ref, o_ref) = refs[1:1 + len(parts)], refs[1 + len(parts):]
        for k in range(len(parts)):
            o_ref[k // L, k % L] = (ins[k][...] + got_ref[k // L, k % L]).astype(BF16)

    def rows_of(first):
        assert first % tr == 0 and half % tr == 0
        return lambda i, c: (first // tr + c[0] * (half // tr) + i, 0)

    whole = pl.BlockSpec((n, L, tr, C), lambda i, c: (0, 0, i, 0))
    return pl.pallas_call(
        body, name=name,
        grid_spec=pltpu.PrefetchScalarGridSpec(
            num_scalar_prefetch=1, grid=(half // tr,),
            in_specs=[pl.BlockSpec((tr, C), rows_of(first)) for _, first in parts] + [whole],
            out_specs=whole),
        out_shape=jax.ShapeDtypeStruct((n, L, half, C), BF16),
        compiler_params=_params(("parallel",)),
    )(c_idx, *[a for a, _ in parts], got)


def _add_slots(r, c_idx, name):
    n, R, C = r.shape
    tr = 256

    def body(c_ref, r_ref, o_ref):
        acc = r_ref[0].astype(F32)
        for j in range(1, n):
            acc = acc + r_ref[j].astype(F32)
        o_ref[0] = acc

    return pl.pallas_call(
        body, name=name,
        grid_spec=pltpu.PrefetchScalarGridSpec(
            num_scalar_prefetch=1, grid=(R // tr,),
            in_specs=[pl.BlockSpec((n, tr, C), lambda i, c: (0, i, 0))],
            out_specs=pl.BlockSpec((1, tr, C), lambda i, c: (c[0], i, 0))),
        out_shape=jax.ShapeDtypeStruct((2, R, C), F32),
        compiler_params=_params(("parallel",)),
    )(c_idx, r)


def _reduce_scatter(parts, half, L, c_idx, chip, tag):
    C = parts[0][0].shape[1]
    got = _pair_exchange(parts, half, L, "rs_pair_" + tag)
    p = _add_halves(parts, got, c_idx, "rs_add2_" + tag).reshape(N_CHIPS, L * half, C)
    q = _chip_exchange(p, "rs_chip_" + tag)
    q = lax.dynamic_update_slice_in_dim(q, lax.dynamic_slice_in_dim(p, chip, 1, axis=0), chip, axis=0)
    t2 = _add_slots(q, c_idx, "rs_add4_" + tag)
    both = _pair_gather(t2, "rs_gather_" + tag).reshape(2, L, half, C)
    return both.transpose(1, 0, 2, 3).reshape(L, 2 * half, C)


def _all_reduce_small(buf, name):
    R = buf.shape[0]

    def flipped(k, x, y, c):
        return ((1 - x) if k & 4 else x, (1 - y) if k & 2 else y, (1 - c) if k & 1 else c)

    def body(b_ref, o_ref, land, send, recv):
        x, y, c, _ = _place()
        me = 4 * x + 2 * y + c
        land[me] = b_ref[...]
        cps = []
        for k in range(1, N_DEV):
            peer = flipped(k, x, y, c)
            cps.append(pltpu.make_async_remote_copy(src_ref=b_ref, dst_ref=land.at[me], send_sem=send.at[k - 1],
                                                    recv_sem=recv.at[k - 1], device_id=peer, device_id_type=MESH))
        for cp in cps:
            cp.start()
        for k in range(1, N_DEV):
            peer = flipped(k, x, y, c)
            slot = 4 * peer[0] + 2 * peer[1] + peer[2]
            pltpu.make_async_remote_copy(src_ref=b_ref, dst_ref=land.at[slot], send_sem=send.at[k - 1],
                                         recv_sem=recv.at[k - 1], device_id=peer, device_id_type=MESH).wait_recv()
        for cp in cps:
            cp.wait_send()
        acc = land[0]
        for j in range(1, N_DEV):
            acc = acc + land[j]
        o_ref[...] = acc

    vm = pl.BlockSpec(memory_space=pltpu.VMEM)
    return pl.pallas_call(
        body, name=name, in_specs=[vm], out_specs=vm,
        out_shape=jax.ShapeDtypeStruct((R, LANES), F32),
        scratch_shapes=[pltpu.VMEM((N_DEV, R, LANES), F32), pltpu.SemaphoreType.DMA((N_DEV - 1,)),
                        pltpu.SemaphoreType.DMA((N_DEV - 1,))],
    )(buf)


def kernel(x, mem, w_in, a_rel_bias, b_gate_w, b_gate_b, b_norm_g, w_mem_kv, w_out, ln_g, ln_b, loss_target, m_w_in, m_a_rel_bias, m_b_gate_w, m_b_gate_b, m_b_norm_g, m_w_mem_kv, m_w_out, m_ln_g, m_ln_b, v_w_in, v_a_rel_bias, v_b_gate_w, v_b_gate_b, v_b_norm_g, v_w_mem_kv, v_w_out, v_ln_g, v_ln_b):
    L = w_in.shape[0]
    S = x.shape[1]
    cx, cy, cc = lax.axis_index("x"), lax.axis_index("y"), lax.axis_index("c")
    chip = 2 * cx + cy
    c_idx = jnp.reshape(cc, (1,)).astype(jnp.int32)

    def gathered(w, name):
        shard = w.astype(BF16).reshape(-1, w.shape[2])
        return lax.dynamic_update_slice_in_dim(_gather_chips(shard, name), shard[None], chip, axis=0)

    n_in = w_in.shape[2]
    win_all = gathered(w_in, "gather_w_in")
    r_kv = w_mem_kv.shape[1]
    wkv_all = gathered(w_mem_kv, "gather_w_kv")
    r_out = w_out.shape[1]
    wout_all = gathered(w_out, "gather_w_out")
    win_all = win_all.reshape(N_CHIPS, L, D_MODEL, n_in)
    wkv_all = wkv_all.reshape(N_CHIPS, L, r_kv, w_mem_kv.shape[2])
    wout_all = wout_all.reshape(N_CHIPS, L, r_out, D_MODEL)

    gw_cols = b_gate_w.shape[2]
    gw_slot = jnp.zeros((N_CHIPS, L, GATE_RANK, gw_cols), F32)
    gw_slot = lax.dynamic_update_slice(gw_slot, (0.5 * b_gate_w)[None], (chip, 0, 0, 0))
    gw_flat = gw_slot.reshape(-1)
    n_gw = gw_flat.shape[0]
    pad = (-n_gw) % (8 * LANES)
    gw_full = _all_reduce_small(jnp.pad(gw_flat, (0, pad)).reshape(-1, LANES), "gather_gate_w").reshape(-1)[:n_gw]
    gw_full = gw_full.reshape(N_CHIPS, L, GATE_RANK, gw_cols).transpose(1, 2, 0, 3).reshape(L, GATE_RANK, B_KEY_WIDTH)

    xs = x.reshape(S, D_MODEL)
    mem_b = mem.reshape(mem.shape[1], D_MODEL).astype(BF16)
    target = loss_target.reshape(S, D_MODEL)

    layer_w = []
    for l in range(L):
        w_in_l = _padded_from_shards([win_all[j, l] for j in range(N_CHIPS)])
        w_kv_l = jnp.concatenate([wkv_all[j, l] for j in range(N_CHIPS)], axis=0)
        w_out_l = jnp.concatenate([wout_all[j, l] for j in range(N_CHIPS)], axis=0)
        gw_l, gb_l = _pad_gate(gw_full[l], b_gate_b[l])
        layer_w.append((w_in_l, w_kv_l, w_out_l, _bias_by_offset(a_rel_bias[l]), gw_l, gb_l,
                        b_norm_g[l].reshape(1, LANES), ln_g[l].reshape(1, D_MODEL), ln_b[l].reshape(1, D_MODEL)))

    y, yb = xs, xs.astype(BF16)
    yt = _transpose(yb, "x_t")
    saved = []
    for l in range(L):
        w_in_l, w_kv_l, w_out_l, u_l, gw_l, gb_l, gn_l, lg_l, lb_l = layer_w[l]
        y, yb, yt, sv = _layer_fwd(y, yb, yt, mem_b, w_in_l, w_kv_l, w_out_l, u_l, gw_l, gb_l, gn_l, lg_l, lb_l)
        saved.append(sv)
    loss_part, dy = _loss_head(y, target)

    grads = [None] * L
    for l in reversed(range(L)):
        w_in_l, w_kv_l, w_out_l, u_l, gw_l, gb_l, gn_l, lg_l, lb_l = layer_w[l]
        dy, grads[l] = _layer_bwd(dy, saved[l], mem_b, w_in_l, w_out_l, u_l, gw_l, gb_l, gn_l, lg_l)
    grad_x = dy.reshape(x.shape)

    g_rel = jnp.stack([_bias_grad_from_offset(g[1]) for g in grads])
    g_gw = jnp.stack([_unpad_heads(g[2][:GATE_RANK]) for g in grads])
    g_gb = jnp.stack([_unpad_heads(g[3])[0] for g in grads])
    g_gn = jnp.stack([g[4][0] for g in grads])
    g_lg = jnp.stack([g[7][0] for g in grads])
    g_lb = jnp.stack([g[8][0] for g in grads])

    in_parts = [(_chip_columns(grads[l][0], j, n_in), 0) for j in range(N_CHIPS) for l in range(L)]
    r_w_in = _reduce_scatter(in_parts, D_MODEL // 2, L, c_idx, chip, "w_in")
    kv_parts = [(grads[l][5], j * r_kv) for j in range(N_CHIPS) for l in range(L)]
    r_w_kv = _reduce_scatter(kv_parts, r_kv // 2, L, c_idx, chip, "w_kv")
    out_parts = [(grads[l][6], j * r_out) for j in range(N_CHIPS) for l in range(L)]
    r_w_out = _reduce_scatter(out_parts, r_out // 2, L, c_idx, chip, "w_out")

    small = [g_rel, g_gw, g_gb, g_gn, g_lg, g_lb, loss_part]
    flat = jnp.concatenate([s.reshape(-1) for s in small])
    n_small = flat.shape[0]
    pad = (-n_small) % (8 * LANES)
    red = _all_reduce_small(jnp.pad(flat, (0, pad)).reshape(-1, LANES), "all_reduce_small").reshape(-1)
    outs, off = [], 0
    for s in small:
        outs.append(red[off:off + s.size].reshape(s.shape))
        off += s.size
    g_rel, g_gw, g_gb, g_gn, g_lg, g_lb, loss = outs
    loss = loss.reshape(())
    g_gw = lax.dynamic_slice_in_dim(g_gw.reshape(L, GATE_RANK, N_CHIPS, gw_cols), chip, 1, axis=2).reshape(L, GATE_RANK, gw_cols)

    g_list = [r_w_in, g_rel, g_gw, g_gb, g_gn, r_w_kv, r_w_out, g_lg, g_lb]
    w_list = [w_in, a_rel_bias, b_gate_w, b_gate_b, b_norm_g, w_mem_kv, w_out, ln_g, ln_b]
    m_list = [m_w_in, m_a_rel_bias, m_b_gate_w, m_b_gate_b, m_b_norm_g, m_w_mem_kv, m_w_out, m_ln_g, m_ln_b]
    v_list = [v_w_in, v_a_rel_bias, v_b_gate_w, v_b_gate_b, v_b_norm_g, v_w_mem_kv, v_w_out, v_ln_g, v_ln_b]
    names = ["w_in", "rel", "gate_w", "gate_b", "norm_g", "w_kv", "w_out", "ln_g", "ln_b"]
    upd = [_adamw_nd(w, g, m, v, "adamw_" + n) for w, g, m, v, n in zip(w_list, g_list, m_list, v_list, names)]
    deltas = [u_[0] for u_ in upd]
    new_m = [u_[1] for u_ in upd]
    new_v = [u_[2] for u_ in upd]
    return (loss, grad_x, *g_list, *deltas, *new_m, *new_v)
```

```python
import functools

import numpy as np
import jax
import jax.numpy as jnp
from jax import lax
from jax.experimental import pallas as pl
from jax.experimental.pallas import tpu as pltpu

F32 = jnp.float32
BF16 = jnp.bfloat16
MESH = pl.DeviceIdType.MESH

D_MODEL = 2048
DEPTH = 4
CHUNK = 64
LEFT_CHUNKS = 8
MAX_REL = 128
N_REL = 2 * MAX_REL + 1
A_HEADS = 8
HEAD_DIM = 128
B_HEADS = 4
B_DK = 64
M_HEADS = 4
GATE_RANK = 16
GATE_TAU = 16.0
A_WIDTH = A_HEADS * HEAD_DIM
B_WIDTH = B_HEADS * HEAD_DIM
B_KEY_WIDTH = B_HEADS * B_DK
M_WIDTH = M_HEADS * HEAD_DIM
IN_WIDTH = 4 * A_WIDTH + 2 * B_KEY_WIDTH + 2 * B_WIDTH + GATE_RANK + 2 * M_WIDTH
ALPHA = (2.0 * DEPTH) ** 0.25
LN_EPS = 1e-5
RMS_EPS = 1e-6
NEG_INF = -1e30
ADAM_LR = 0.001
ADAM_B1 = 0.9
ADAM_B2 = 0.999
ADAM_EPS = 1e-08
ADAM_WD = 0.01
ADAM_STEP = 10

LANES = 128
VMEM_LIMIT = 56 * 1024 * 1024

C_A, C_B, C_M, C_LR = 0, 4096, 6144, 7168
HP = 7680
SEG_Q, SEG_K, SEG_V, SEG_Z = 0, 1, 2, 3
TQ = 512
CPB = TQ // CHUNK
N_CHIPS = 4
N_DEV = 8


def _params(sem, vmem=VMEM_LIMIT):
    return pltpu.CompilerParams(dimension_semantics=sem, vmem_limit_bytes=vmem)


def _dot(a, b):
    return jnp.dot(a, b, preferred_element_type=F32)


def _dot_nt(a, b):
    return lax.dot_general(a, b, (((1,), (1,)), ((), ())), preferred_element_type=F32)


def _dot_tn(a, b):
    return lax.dot_general(a, b, (((0,), (0,)), ((), ())), preferred_element_type=F32)


def _sigmoid(x):
    return 1.0 / (1.0 + jnp.exp(-x))


def _split3(x):
    hi = x.astype(BF16)
    r = x - hi.astype(F32)
    mid = r.astype(BF16)
    lo = (r - mid.astype(F32)).astype(BF16)
    return hi, mid, lo


def _dot3(m_bf, x):
    hi, mid, lo = _split3(x)
    return _dot(m_bf, hi) + _dot(m_bf, mid) + _dot(m_bf, lo)


def _matmul(a, b, *, mode, out_dtype, tm, tn, tk, name, add=None, add_scale=1.0):
    if mode == "nn":
        (M, K), (K2, N) = a.shape, b.shape
        a_spec = pl.BlockSpec((tm, tk), lambda i, j, k: (i, k))
        b_spec = pl.BlockSpec((tk, tn), lambda i, j, k: (k, j))
        dot = _dot
    elif mode == "nt":
        (M, K), (N, K2) = a.shape, b.shape
        a_spec = pl.BlockSpec((tm, tk), lambda i, j, k: (i, k))
        b_spec = pl.BlockSpec((tn, tk), lambda i, j, k: (j, k))
        dot = _dot_nt
    else:
        (K, M), (K2, N) = a.shape, b.shape
        a_spec = pl.BlockSpec((tk, tm), lambda i, j, k: (k, i))
        b_spec = pl.BlockSpec((tk, tn), lambda i, j, k: (k, j))
        dot = _dot_tn
    assert K == K2 and M % tm == 0 and N % tn == 0 and K % tk == 0, (a.shape, b.shape, mode)
    nk = K // tk
    has_add = add is not None
    assert nk == 1 or out_dtype == F32

    def body(*refs):
        if has_add:
            a_ref, b_ref, add_ref, o_ref = refs
        else:
            a_ref, b_ref, o_ref = refs
        k = pl.program_id(2)
        part = dot(a_ref[...].astype(BF16), b_ref[...].astype(BF16))

        @pl.when(k == 0)
        def _():
            first = part + add_scale * add_ref[...] if has_add else part
            o_ref[...] = first.astype(out_dtype)

        if nk > 1:
            @pl.when(k > 0)
            def _():
                o_ref[...] += part

    in_specs = [a_spec, b_spec]
    args = [a, b]
    if has_add:
        in_specs.append(pl.BlockSpec((tm, tn), lambda i, j, k: (i, j)))
        args.append(add)
    return pl.pallas_call(
        body,
        name=name,
        grid=(M // tm, N // tn, nk),
        in_specs=in_specs,
        out_specs=pl.BlockSpec((tm, tn), lambda i, j, k: (i, j)),
        out_shape=jax.ShapeDtypeStruct((M, N), out_dtype),
        compiler_params=_params(("parallel", "parallel", "arbitrary")),
    )(*args)


def _transpose(a, name):
    R, C = a.shape
    t = 512

    def body(a_ref, o_ref):
        o_ref[...] = a_ref[...].T

    return pl.pallas_call(
        body, name=name, grid=(R // t, C // t),
        in_specs=[pl.BlockSpec((t, t), lambda i, j: (i, j))],
        out_specs=pl.BlockSpec((t, t), lambda i, j: (j, i)),
        out_shape=jax.ShapeDtypeStruct((C, R), a.dtype),
        compiler_params=_params(("parallel", "parallel")),
    )(a)


def _chunk_of(rows):
    return lax.shift_right_logical(rows, CHUNK.bit_length() - 1)


A_HPS = 2
A_HW = A_HPS * LANES


def _band_bias(u_row, first):
    bias = pltpu.roll(jnp.broadcast_to(u_row, (TQ, 2 * TQ)), 0, 1, stride=1, stride_axis=0)
    qc = _chunk_of(lax.broadcasted_iota(jnp.int32, (TQ, 2 * TQ), 0))
    col = lax.broadcasted_iota(jnp.int32, (TQ, 2 * TQ), 1)
    kc = _chunk_of(jnp.bitwise_and(col, TQ - 1))
    ok = jnp.logical_or(jnp.logical_and(col < TQ, kc >= qc), jnp.logical_and(col >= TQ, kc <= qc))
    return jnp.where(ok, bias, NEG_INF) + jnp.where(col < TQ, first * NEG_INF, 0.0)


def _band_probs(q, kp, kc, bias):
    scale = HEAD_DIM ** -0.5
    sp = _dot_nt(q, kp) * scale + bias[:, :TQ]
    sc = _dot_nt(q, kc) * scale + bias[:, TQ:]
    m = jnp.maximum(jnp.max(sp, axis=1, keepdims=True), jnp.max(sc, axis=1, keepdims=True))
    pp = jnp.exp(sp - m)
    pc = jnp.exp(sc - m)
    inv = 1.0 / (jnp.sum(pp, axis=1, keepdims=True) + jnp.sum(pc, axis=1, keepdims=True))
    return pp, pc, inv


def _band_specs(nq):
    def col(seg, h):
        return C_A // A_HW + 4 * h + seg

    q_spec = pl.BlockSpec((TQ, A_HW), lambda h, i: (jnp.minimum(i, nq - 1), col(SEG_Q, h)))
    kp_spec = pl.BlockSpec((TQ, A_HW), lambda h, i: (jnp.clip(i - 1, 0, nq - 1), col(SEG_K, h)))
    kc_spec = pl.BlockSpec((TQ, A_HW), lambda h, i: (jnp.minimum(i, nq - 1), col(SEG_K, h)))
    vp_spec = pl.BlockSpec((TQ, A_HW), lambda h, i: (jnp.clip(i - 1, 0, nq - 1), col(SEG_V, h)))
    vc_spec = pl.BlockSpec((TQ, A_HW), lambda h, i: (jnp.minimum(i, nq - 1), col(SEG_V, h)))
    z_spec = pl.BlockSpec((TQ, A_HW), lambda h, i: (jnp.minimum(i, nq - 1), col(SEG_Z, h)))
    u_spec = pl.BlockSpec((A_HPS, 1, 2 * TQ), lambda h, i: (h, 0, 0))
    return q_spec, kp_spec, kc_spec, vp_spec, vc_spec, z_spec, u_spec


def _band_fwd(h, u):
    S = h.shape[0]
    nq = S // TQ

    def body(q_ref, kp_ref, kc_ref, vp_ref, vc_ref, z_ref, u_ref, y_ref, yt_ref, bias_scr):
        i = pl.program_id(1)

        @pl.when(i <= 1)
        def _():
            for hh in range(A_HPS):
                bias_scr[hh] = _band_bias(u_ref[hh], (i == 0).astype(F32))

        for hh in range(A_HPS):
            cs = slice(hh * LANES, (hh + 1) * LANES)
            pp, pc, inv = _band_probs(q_ref[:, cs], kp_ref[:, cs], kc_ref[:, cs], bias_scr[hh])
            o = (_dot(pp.astype(BF16), vp_ref[:, cs]) + _dot(pc.astype(BF16), vc_ref[:, cs])) * inv
            z = z_ref[:, cs].astype(F32)
            y = o * (z * _sigmoid(z))
            y_ref[:, cs] = y.astype(BF16)
            yt_ref[cs, :] = y.T.astype(BF16)

    specs = _band_specs(nq)
    return pl.pallas_call(
        body,
        name="band_fwd",
        grid=(A_HEADS // A_HPS, nq),
        in_specs=[specs[0], specs[1], specs[2], specs[3], specs[4], specs[5], specs[6]],
        out_specs=[pl.BlockSpec((TQ, A_HW), lambda h, i: (i, h)), pl.BlockSpec((A_HW, TQ), lambda h, i: (h, i))],
        out_shape=[jax.ShapeDtypeStruct((S, D_MODEL), BF16), jax.ShapeDtypeStruct((D_MODEL, S), BF16)],
        scratch_shapes=[pltpu.VMEM((A_HPS, TQ, 2 * TQ), F32)],
        compiler_params=_params(("parallel", "arbitrary")),
    )(h, h, h, h, h, h, u)


def _band_bwd(h, u, dycat):
    S = h.shape[0]
    nq = S // TQ
    scale = HEAD_DIM ** -0.5
    qs, ks, vs, zs = (slice(s * A_HW, (s + 1) * A_HW) for s in (SEG_Q, SEG_K, SEG_V, SEG_Z))

    def body(q_ref, kp_ref, kc_ref, vp_ref, vc_ref, z_ref, u_ref, dy_ref,
             dh_ref, du_ref, bias_scr, db_scr, ckt_scr, cvt_scr, cq_scr, cz_scr):
        i = pl.program_id(1)

        @pl.when(i <= 1)
        def _():
            for hh in range(A_HPS):
                bias_scr[hh] = _band_bias(u_ref[hh], (i == 0).astype(F32))

        @pl.when(i == 0)
        def _():
            db_scr[...] = jnp.zeros_like(db_scr)
            ckt_scr[...] = jnp.zeros_like(ckt_scr)
            cvt_scr[...] = jnp.zeros_like(cvt_scr)
            cq_scr[...] = jnp.zeros_like(cq_scr)
            cz_scr[...] = jnp.zeros_like(cz_scr)

        @pl.when(i < nq)
        def _():
            dh_ref[:, qs] = cq_scr[...]
            dh_ref[:, zs] = cz_scr[...]
            for hh in range(A_HPS):
                cs = slice(hh * LANES, (hh + 1) * LANES)
                q, kp, kc, vp, vc = q_ref[:, cs], kp_ref[:, cs], kc_ref[:, cs], vp_ref[:, cs], vc_ref[:, cs]
                pp, pc, inv = _band_probs(q, kp, kc, bias_scr[hh])
                pp, pc = pp * inv, pc * inv
                ppb, pcb = pp.astype(BF16), pc.astype(BF16)
                o = _dot(ppb, vp) + _dot(pcb, vc)
                z = z_ref[:, cs].astype(F32)
                sg = _sigmoid(z)
                dy = dy_ref[:, cs].astype(F32)
                do = dy * (z * sg)
                cz_scr[:, cs] = (dy * o * (sg * (1.0 + z * (1.0 - sg)))).astype(BF16)
                dob = do.astype(BF16)
                delta = jnp.sum(do * o, axis=1, keepdims=True)
                dsp = pp * (_dot_nt(dob, vp) - delta)
                dsc = pc * (_dot_nt(dob, vc) - delta)
                db_scr[hh, :, :TQ] += dsp
                db_scr[hh, :, TQ:] += dsc
                dspb, dscb = dsp.astype(BF16), dsc.astype(BF16)
                cq_scr[:, cs] = (scale * (_dot(dspb, kp) + _dot(dscb, kc))).astype(BF16)
                qt, dot_ = q.T, dob.T
                dh_ref[:, SEG_K * A_HW + hh * LANES:SEG_K * A_HW + (hh + 1) * LANES] = (
                    ckt_scr[cs, :] + scale * _dot(qt, dspb)).T.astype(BF16)
                dh_ref[:, SEG_V * A_HW + hh * LANES:SEG_V * A_HW + (hh + 1) * LANES] = (
                    cvt_scr[cs, :] + _dot(dot_, ppb)).T.astype(BF16)
                ckt_scr[cs, :] = scale * _dot(qt, dscb)
                cvt_scr[cs, :] = _dot(dot_, pcb)

        @pl.when(i == nq)
        def _():
            dh_ref[:, qs] = cq_scr[...]
            dh_ref[:, zs] = cz_scr[...]
            dh_ref[:, ks] = ckt_scr[...].T.astype(BF16)
            dh_ref[:, vs] = cvt_scr[...].T.astype(BF16)
            r0 = lax.broadcasted_iota(jnp.int32, (TQ, TQ), 0)
            r1 = lax.broadcasted_iota(jnp.int32, (TQ, TQ), 1)
            flip = (r0 + r1 == TQ - 1).astype(BF16)
            for hh in range(A_HPS):
                fl = _dot3(flip, db_scr[hh])
                rolled = pltpu.roll(fl, 0, 1, stride=1, stride_axis=0)
                du_ref[hh] = jnp.sum(rolled, axis=0, keepdims=True)

    specs = _band_specs(nq)
    dy_spec = pl.BlockSpec((TQ, A_HW), lambda h, i: (jnp.minimum(i, nq - 1), h))
    return pl.pallas_call(
        body,
        name="band_bwd",
        grid=(A_HEADS // A_HPS, nq + 1),
        in_specs=[specs[0], specs[1], specs[2], specs[3], specs[4], specs[5], specs[6], dy_spec],
        out_specs=[pl.BlockSpec((TQ, 4 * A_HW), lambda h, i: (jnp.maximum(i - 1, 0), C_A // (4 * A_HW) + h)),
                   pl.BlockSpec((A_HPS, 1, 2 * TQ), lambda h, i: (h, 0, 0))],
        out_shape=[jax.ShapeDtypeStruct((S, HP), BF16), jax.ShapeDtypeStruct((A_HEADS, 1, 2 * TQ), F32)],
        scratch_shapes=[pltpu.VMEM((A_HPS, TQ, 2 * TQ), F32), pltpu.VMEM((A_HPS, TQ, 2 * TQ), F32),
                        pltpu.VMEM((A_HW, TQ), F32), pltpu.VMEM((A_HW, TQ), F32),
                        pltpu.VMEM((TQ, A_HW), BF16), pltpu.VMEM((TQ, A_HW), BF16)],
        compiler_params=_params(("parallel", "arbitrary")),
    )(h, h, h, h, h, h, u, dycat)


def _bias_by_offset(table):
    far = jnp.broadcast_to(table[:, N_REL - 1:], (A_HEADS, TQ - MAX_REL))
    ramp = jnp.flip(table, axis=1)
    rest = jnp.broadcast_to(table[:, :1], (A_HEADS, 2 * TQ - CHUNK - (TQ + MAX_REL + 1)))
    wrap = jnp.broadcast_to(table[:, N_REL - 1:], (A_HEADS, CHUNK))
    return jnp.concatenate([far, ramp, rest, wrap], axis=1)[:, None, :]


def _bias_grad_from_offset(du):
    g = jnp.roll(du[:, 0, :], -(TQ - 1), axis=1)
    far = jnp.sum(g[:, :TQ - MAX_REL], axis=1) + jnp.sum(g[:, 2 * TQ - CHUNK:], axis=1)
    ramp = jnp.flip(g[:, TQ - MAX_REL:TQ + MAX_REL + 1], axis=1)
    return ramp.at[:, N_REL - 1].add(far)


def _mem_probs(q, mk):
    s = _dot_nt(q, mk) * (HEAD_DIM ** -0.5)
    p = jnp.exp(s - jnp.max(s, axis=1, keepdims=True))
    return p * (1.0 / jnp.sum(p, axis=1, keepdims=True))


def _mem_fwd(h, mkv, ycat, ycat_t):
    S = h.shape[0]
    nm = mkv.shape[0]
    c0 = (A_WIDTH + B_WIDTH) // LANES

    def body(q_ref, z_ref, mk_ref, mv_ref, yin_ref, ytin_ref, y_ref, yt_ref):
        del yin_ref, ytin_ref
        p = _mem_probs(q_ref[...], mk_ref[...])
        o = _dot(p.astype(BF16), mv_ref[...])
        z = z_ref[...].astype(F32)
        y = o * (z * _sigmoid(z))
        y_ref[...] = y.astype(BF16)
        yt_ref[...] = y.T.astype(BF16)

    return pl.pallas_call(
        body,
        name="mem_fwd",
        grid=(M_HEADS, S // TQ),
        in_specs=[pl.BlockSpec((TQ, LANES), lambda h, i: (i, C_M // LANES + 2 * h)),
                  pl.BlockSpec((TQ, LANES), lambda h, i: (i, C_M // LANES + 2 * h + 1)),
                  pl.BlockSpec((nm, LANES), lambda h, i: (0, h)),
                  pl.BlockSpec((nm, LANES), lambda h, i: (0, M_HEADS + h)), ANY, ANY],
        out_specs=[pl.BlockSpec((TQ, LANES), lambda h, i: (i, c0 + h)), pl.BlockSpec((LANES, TQ), lambda h, i: (c0 + h, i))],
        out_shape=[jax.ShapeDtypeStruct(ycat.shape, BF16), jax.ShapeDtypeStruct(ycat_t.shape, BF16)],
        input_output_aliases={4: 0, 5: 1},
        compiler_params=_params(("parallel", "arbitrary")),
    )(h, h, mkv, mkv, ycat, ycat_t)


def _mem_bwd(h, mkv, dycat, dh):
    S = h.shape[0]
    nm = mkv.shape[0]
    scale = HEAD_DIM ** -0.5

    def body(q_ref, z_ref, mk_ref, mv_ref, dy_ref, dhin_ref, dh_ref, dmk_ref, dmv_ref):
        del dhin_ref
        i = pl.program_id(1)
        q, mk, mv = q_ref[...], mk_ref[...], mv_ref[...]
        p = _mem_probs(q, mk)
        pb = p.astype(BF16)
        o = _dot(pb, mv)
        z = z_ref[...].astype(F32)
        sg = _sigmoid(z)
        dy = dy_ref[...].astype(F32)
        do = dy * (z * sg)
        dh_ref[:, LANES:] = (dy * o * (sg * (1.0 + z * (1.0 - sg)))).astype(BF16)
        dob = do.astype(BF16)
        ds = p * (_dot_nt(dob, mv) - jnp.sum(do * o, axis=1, keepdims=True))
        dsb = ds.astype(BF16)
        dh_ref[:, :LANES] = (scale * _dot(dsb, mk)).astype(BF16)
        dmk = scale * _dot_tn(dsb, q)
        dmv = _dot_tn(pb, dob)

        @pl.when(i == 0)
        def _():
            dmk_ref[...] = dmk
            dmv_ref[...] = dmv

        @pl.when(i > 0)
        def _():
            dmk_ref[...] += dmk
            dmv_ref[...] += dmv

    dh, dmk, dmv = pl.pallas_call(
        body,
        name="mem_bwd",
        grid=(M_HEADS, S // TQ),
        in_specs=[pl.BlockSpec((TQ, LANES), lambda h, i: (i, C_M // LANES + 2 * h)),
                  pl.BlockSpec((TQ, LANES), lambda h, i: (i, C_M // LANES + 2 * h + 1)),
                  pl.BlockSpec((nm, LANES), lambda h, i: (0, h)),
                  pl.BlockSpec((nm, LANES), lambda h, i: (0, M_HEADS + h)),
                  pl.BlockSpec((TQ, LANES), lambda h, i: (i, (A_WIDTH + B_WIDTH) // LANES + h)), ANY],
        out_specs=[pl.BlockSpec((TQ, 2 * LANES), lambda h, i: (i, C_M // (2 * LANES) + h)),
                   pl.BlockSpec((nm, LANES), lambda h, i: (0, h)),
                   pl.BlockSpec((nm, LANES), lambda h, i: (0, h))],
        out_shape=[jax.ShapeDtypeStruct(dh.shape, BF16),
                   jax.ShapeDtypeStruct((nm, M_WIDTH), F32), jax.ShapeDtypeStruct((nm, M_WIDTH), F32)],
        input_output_aliases={5: 0},
        compiler_params=_params(("parallel", "arbitrary")),
    )(h, h, mkv, mkv, dycat, dh)
    return dh, jnp.concatenate([dmk, dmv], axis=1)


def _chunk_masks():
    r = lax.broadcasted_iota(jnp.int32, (TQ, TQ), 0)
    c = lax.broadcasted_iota(jnp.int32, (TQ, TQ), 1)
    same = _chunk_of(r) == _chunk_of(c)
    return jnp.logical_and(same, c <= r), jnp.logical_and(same, c > r)


def _gla_gates(lr, gw, gb):
    logit = _dot(lr, gw) + gb
    sg = _sigmoid(logit)
    g = (jnp.minimum(logit, 0.0) - jnp.log(1.0 + jnp.exp(-jnp.abs(logit)))) * (1.0 / GATE_TAU)
    lo, _ = _chunk_masks()
    return sg, _dot3(lo.astype(BF16), g)


def _gla_factors(q, k, b):
    eb = jnp.exp(b)
    enb = jnp.exp(-b)
    return eb, enb, q * eb, q * enb, k * eb, k * enb


def _gla_intra(qp, qn, kp, kn):
    lo, up = _chunk_masks()
    return (jnp.where(lo, _dot_nt(qp.astype(BF16), kn.astype(BF16)), 0.0)
            + jnp.where(up, _dot_nt(qn.astype(BF16), kp.astype(BF16)), 0.0))


def _gla_specs(nb, rev):
    blk = (lambda i: nb - 1 - i) if rev else (lambda i: i)

    def at(seg):
        return pl.BlockSpec((TQ, LANES), lambda i, h: (blk(i), C_B // LANES + 4 * h + seg))

    lr_spec = pl.BlockSpec((TQ, LANES), lambda i, h: (blk(i), C_LR // LANES))
    gw_spec = pl.BlockSpec((LANES, LANES), lambda i, h: (0, h))
    gb_spec = pl.BlockSpec((1, LANES), lambda i, h: (0, h))
    gn_spec = pl.BlockSpec((1, LANES), lambda i, h: (0, 0))
    return at(SEG_Q), at(SEG_K), at(SEG_V), at(SEG_Z), lr_spec, gw_spec, gb_spec, gn_spec, blk


def _gla_fwd(h, gw, gb, gn, ycat, ycat_t):
    S = h.shape[0]
    nb = S // TQ
    c0 = A_WIDTH // LANES

    def body(q_ref, k_ref, v_ref, z_ref, lr_ref, gw_ref, gb_ref, gn_ref, yin_ref, ytin_ref,
             y_ref, yt_ref, o_ref, st_ref, st_scr):
        del yin_ref, ytin_ref
        i, hd = pl.program_id(0), pl.program_id(1)

        @pl.when(i == 0)
        def _():
            st_scr[hd] = jnp.zeros((LANES, LANES), F32)

        q = q_ref[...].astype(F32) * (B_DK ** -0.5)
        k = k_ref[...].astype(F32)
        v = v_ref[...]
        _, b = _gla_gates(lr_ref[...], gw_ref[...], gb_ref[...])
        _, _, qp, qn, kp, kn = _gla_factors(q, k, b)
        o_intra = _dot(_gla_intra(qp, qn, kp, kn).astype(BF16), v)
        qpb, knb = qp.astype(BF16), kn.astype(BF16)
        st = st_scr[hd]
        outs = []
        for c in range(CPB):
            rows = slice(c * CHUNK, (c + 1) * CHUNK)
            st_ref[0, c] = st
            outs.append(_dot_nt(qpb[rows], st.astype(BF16)))
            e_last = jnp.exp(b[(c + 1) * CHUNK - 1:(c + 1) * CHUNK, :])
            st = (st + _dot_tn(v[rows], knb[rows])) * e_last
        st_scr[hd] = st
        o = o_intra + jnp.concatenate(outs, axis=0)
        o_ref[...] = o
        r = lax.rsqrt(jnp.mean(o * o, axis=1, keepdims=True) + RMS_EPS)
        z = z_ref[...].astype(F32)
        y = o * r * gn_ref[...] * (z * _sigmoid(z))
        y_ref[...] = y.astype(BF16)
        yt_ref[...] = y.T.astype(BF16)

    q_s, k_s, v_s, z_s, lr_s, gw_s, gb_s, gn_s, _ = _gla_specs(nb, False)
    row = pl.BlockSpec((TQ, LANES), lambda i, h: (i, h))
    return pl.pallas_call(
        body,
        name="gla_fwd",
        grid=(nb, B_HEADS),
        in_specs=[q_s, k_s, v_s, z_s, lr_s, gw_s, gb_s, gn_s, ANY, ANY],
        out_specs=[pl.BlockSpec((TQ, LANES), lambda i, h: (i, c0 + h)), pl.BlockSpec((LANES, TQ), lambda i, h: (c0 + h, i)),
                   row, pl.BlockSpec((1, CPB, LANES, LANES), lambda i, h: (h, i, 0, 0))],
        out_shape=[jax.ShapeDtypeStruct(ycat.shape, BF16), jax.ShapeDtypeStruct(ycat_t.shape, BF16),
                   jax.ShapeDtypeStruct((S, B_WIDTH), F32),
                   jax.ShapeDtypeStruct((B_HEADS, S // CHUNK, LANES, LANES), F32)],
        input_output_aliases={8: 0, 9: 1},
        scratch_shapes=[pltpu.VMEM((B_HEADS, LANES, LANES), F32)],
        compiler_params=_params(("arbitrary", "arbitrary")),
    )(h, h, h, h, h, gw, gb, gn, ycat, ycat_t)


def _gla_bwd(h, gw, gb, gn, o_pre, states, dycat, dh):
    S = h.shape[0]
    nb = S // TQ

    def body(q_ref, k_ref, v_ref, z_ref, lr_ref, gw_ref, gb_ref, gn_ref, o_ref, st_ref, dy_ref, dhin_ref,
             dh_ref, dlr_ref, dgw_ref, dgb_ref, dgn_ref,
             dst_scr, dgw_scr, dgb_scr, dgn_scr):
        del dhin_ref
        i, hd = pl.program_id(0), pl.program_id(1)

        @pl.when(i == 0)
        def _():
            dst_scr[hd] = jnp.zeros((LANES, LANES), F32)
            dgw_scr[hd] = jnp.zeros((LANES, LANES), F32)
            dgb_scr[hd] = jnp.zeros((1, LANES), F32)

        @pl.when(jnp.logical_and(i == 0, hd == 0))
        def _():
            dgn_scr[...] = jnp.zeros_like(dgn_scr)

        q = q_ref[...].astype(F32) * (B_DK ** -0.5)
        k = k_ref[...].astype(F32)
        v = v_ref[...]
        lr, gwv = lr_ref[...], gw_ref[...]
        sg, b = _gla_gates(lr, gwv, gb_ref[...])
        eb, enb, qp, qn, kp, kn = _gla_factors(q, k, b)
        a = _gla_intra(qp, qn, kp, kn)
        qpb, qnb, kpb, knb = qp.astype(BF16), qn.astype(BF16), kp.astype(BF16), kn.astype(BF16)

        o = o_ref[...]
        gn = gn_ref[...]
        r = lax.rsqrt(jnp.mean(o * o, axis=1, keepdims=True) + RMS_EPS)
        z = z_ref[...].astype(F32)
        sz = _sigmoid(z)
        dy = dy_ref[...].astype(F32)
        d_on = dy * (z * sz)
        dh_ref[:, SEG_Z * LANES:(SEG_Z + 1) * LANES] = (dy * (o * r * gn) * (sz * (1.0 + z * (1.0 - sz)))).astype(BF16)
        dgn_scr[...] += jnp.sum(d_on * o * r, axis=0, keepdims=True)
        t = d_on * gn
        do = r * t - o * (r * r * r) * jnp.mean(t * o, axis=1, keepdims=True)
        dob = do.astype(BF16)

        lo, up = _chunk_masks()
        da = _dot_nt(dob, v)
        dalo = jnp.where(lo, da, 0.0).astype(BF16)
        daup = jnp.where(up, da, 0.0).astype(BF16)
        dqp = _dot(dalo, knb)
        dkn = _dot_tn(dalo, qpb)
        dqn = _dot(daup, kpb)
        dkp = _dot_tn(daup, qnb)
        dv = _dot_tn(a.astype(BF16), dob)

        dst = dst_scr[hd]
        dqp_c, dkn_c, dv_c, dbl_c = [None] * CPB, [None] * CPB, [None] * CPB, [None] * CPB
        for c in reversed(range(CPB)):
            rows = slice(c * CHUNK, (c + 1) * CHUNK)
            st = st_ref[0, c]
            e_last = jnp.exp(b[(c + 1) * CHUNK - 1:(c + 1) * CHUNK, :])
            if c == CPB - 1:
                st_next = (st + _dot_tn(v[rows], knb[rows])) * e_last
            else:
                st_next = st_ref[0, c + 1]
            dbl_c[c] = jnp.sum(dst * st_next, axis=0, keepdims=True)
            dtt = (dst * e_last).astype(BF16)
            dv_c[c] = _dot_nt(knb[rows], dtt)
            dkn_c[c] = _dot(v[rows], dtt)
            dqp_c[c] = _dot(dob[rows], st.astype(BF16))
            dst = _dot_tn(dob[rows], qpb[rows]) + dst * e_last
        dst_scr[hd] = dst
        dqp = dqp + jnp.concatenate(dqp_c, axis=0)
        dkn = dkn + jnp.concatenate(dkn_c, axis=0)
        dv = dv + jnp.concatenate(dv_c, axis=0)
        dh_ref[:, SEG_V * LANES:(SEG_V + 1) * LANES] = dv.astype(BF16)
        dh_ref[:, SEG_Q * LANES:(SEG_Q + 1) * LANES] = ((dqp * eb + dqn * enb) * (B_DK ** -0.5)).astype(BF16)
        dh_ref[:, SEG_K * LANES:(SEG_K + 1) * LANES] = (dkp * eb + dkn * enb).astype(BF16)

        last = jnp.bitwise_and(lax.broadcasted_iota(jnp.int32, (TQ, 1), 0), CHUNK - 1) == CHUNK - 1
        dbl = jnp.concatenate([jnp.broadcast_to(x, (CHUNK, LANES)) for x in dbl_c], axis=0)
        db = dqp * qp - dqn * qn + dkp * kp - dkn * kn + jnp.where(last, dbl, 0.0)
        r0 = lax.broadcasted_iota(jnp.int32, (TQ, TQ), 0)
        r1 = lax.broadcasted_iota(jnp.int32, (TQ, TQ), 1)
        upper = jnp.logical_and(_chunk_of(r0) == _chunk_of(r1), r1 >= r0).astype(BF16)
        dlogit = _dot3(upper, db) * (1.0 / GATE_TAU) * (1.0 - sg)
        dlb = dlogit.astype(BF16)
        dlr = _dot_nt(dlb, gwv)

        @pl.when(hd == 0)
        def _():
            dlr_ref[...] = dlr

        @pl.when(hd > 0)
        def _():
            dlr_ref[...] += dlr

        dgw_scr[hd] += _dot_tn(lr, dlb)
        dgb_scr[hd] += jnp.sum(dlogit, axis=0, keepdims=True)

        @pl.when(i == nb - 1)
        def _():
            dgw_ref[...] = dgw_scr[hd]
            dgb_ref[...] = dgb_scr[hd]
            dgn_ref[...] = dgn_scr[...]

    q_s, k_s, v_s, z_s, lr_s, gw_s, gb_s, gn_s, blk = _gla_specs(nb, True)
    row = pl.BlockSpec((TQ, LANES), lambda i, h: (blk(i), h))
    dy_spec = pl.BlockSpec((TQ, LANES), lambda i, h: (blk(i), A_WIDTH // LANES + h))
    st_spec = pl.BlockSpec((1, CPB, LANES, LANES), lambda i, h: (h, blk(i), 0, 0))
    return pl.pallas_call(
        body,
        name="gla_bwd",
        grid=(nb, B_HEADS),
        in_specs=[q_s, k_s, v_s, z_s, lr_s, gw_s, gb_s, gn_s, row, st_spec, dy_spec, ANY],
        out_specs=[pl.BlockSpec((TQ, 4 * LANES), lambda i, h: (blk(i), C_B // (4 * LANES) + h)),
                   pl.BlockSpec((TQ, LANES), lambda i, h: (blk(i), 0)),
                   pl.BlockSpec((LANES, LANES), lambda i, h: (0, jnp.where(i == nb - 1, h, 0))),
                   pl.BlockSpec((1, LANES), lambda i, h: (0, jnp.where(i == nb - 1, h, 0))),
                   pl.BlockSpec((1, LANES), lambda i, h: (0, 0))],
        out_shape=[jax.ShapeDtypeStruct(dh.shape, BF16),
                   jax.ShapeDtypeStruct((S, LANES), F32),
                   jax.ShapeDtypeStruct((LANES, B_HEADS * LANES), F32),
                   jax.ShapeDtypeStruct((1, B_HEADS * LANES), F32),
                   jax.ShapeDtypeStruct((1, LANES), F32)],
        input_output_aliases={11: 0},
        scratch_shapes=[pltpu.VMEM((B_HEADS, LANES, LANES), F32), pltpu.VMEM((B_HEADS, LANES, LANES), F32),
                        pltpu.VMEM((B_HEADS, 1, LANES), F32), pltpu.VMEM((1, LANES), F32)],
        compiler_params=_params(("arbitrary", "arbitrary")),
    )(h, h, h, h, h, gw, gb, gn, o_pre, states, dycat, dh)


def _lr_fill(dlr, dh):
    S = dlr.shape[0]
    w = HP - C_LR

    def body(dlr_ref, dhin_ref, dh_ref):
        del dhin_ref
        dh_ref[:, :LANES] = dlr_ref[...].astype(BF16)
        dh_ref[:, LANES:] = jnp.zeros((TQ, w - LANES), BF16)

    return pl.pallas_call(
        body, name="lr_fill", grid=(S // TQ,),
        in_specs=[pl.BlockSpec((TQ, LANES), lambda i: (i, 0)), ANY],
        out_specs=pl.BlockSpec((TQ, w), lambda i: (i, C_LR // w)),
        out_shape=jax.ShapeDtypeStruct(dh.shape, BF16),
        input_output_aliases={1: 0},
        compiler_params=_params(("parallel",)),
    )(dlr, dh)


LN_ROWS = 256


def _outproj_ln(ycat, w_out, x, g, b):
    S = x.shape[0]

    def body(yc_ref, w_ref, x_ref, g_ref, b_ref, y_ref, yb_ref, yt_ref, xh_ref, rs_ref):
        u = ALPHA * x_ref[...] + _dot(yc_ref[...], w_ref[...])
        mu = jnp.mean(u, axis=1, keepdims=True)
        d = u - mu
        rstd = lax.rsqrt(jnp.mean(d * d, axis=1, keepdims=True) + LN_EPS)
        xh = d * rstd
        y = xh * g_ref[...] + b_ref[...]
        y_ref[...] = y
        yb_ref[...] = y.astype(BF16)
        yt_ref[...] = y.T.astype(BF16)
        xh_ref[...] = xh
        rs_ref[...] = rstd

    row = lambda w: pl.BlockSpec((LN_ROWS, w), lambda i: (i, 0))
    vec = pl.BlockSpec((1, D_MODEL), lambda i: (0, 0))
    return pl.pallas_call(
        body,
        name="outproj_ln",
        grid=(S // LN_ROWS,),
        in_specs=[row(D_MODEL), pl.BlockSpec((D_MODEL, D_MODEL), lambda i: (0, 0)), row(D_MODEL), vec, vec],
        out_specs=[row(D_MODEL), row(D_MODEL), pl.BlockSpec((D_MODEL, LN_ROWS), lambda i: (0, i)), row(D_MODEL), row(1)],
        out_shape=[jax.ShapeDtypeStruct((S, D_MODEL), F32), jax.ShapeDtypeStruct((S, D_MODEL), BF16),
                   jax.ShapeDtypeStruct((D_MODEL, S), BF16),
                   jax.ShapeDtypeStruct((S, D_MODEL), F32), jax.ShapeDtypeStruct((S, 1), F32)],
        compiler_params=_params(("parallel",)),
    )(ycat, w_out, x, g, b)


def _ln_bwd(dy, xhat, rstd, g):
    S = dy.shape[0]

    def body(dy_ref, xh_ref, rs_ref, g_ref, du_ref, dub_ref, dg_ref, db_ref):
        i = pl.program_id(0)
        dy_, xh = dy_ref[...], xh_ref[...]
        dyg = dy_ * g_ref[...]
        m1 = jnp.mean(dyg, axis=1, keepdims=True)
        m2 = jnp.mean(dyg * xh, axis=1, keepdims=True)
        du = rs_ref[...] * (dyg - m1 - xh * m2)
        du_ref[...] = du
        dub_ref[...] = du.astype(BF16)
        dg = jnp.sum(dy_ * xh, axis=0, keepdims=True)
        db = jnp.sum(dy_, axis=0, keepdims=True)

        @pl.when(i == 0)
        def _():
            dg_ref[...] = dg
            db_ref[...] = db

        @pl.when(i > 0)
        def _():
            dg_ref[...] += dg
            db_ref[...] += db

    row = lambda w: pl.BlockSpec((TQ, w), lambda i: (i, 0))
    vec = pl.BlockSpec((1, D_MODEL), lambda i: (0, 0))
    return pl.pallas_call(
        body,
        name="ln_bwd",
        grid=(S // TQ,),
        in_specs=[row(D_MODEL), row(D_MODEL), row(1), vec],
        out_specs=[row(D_MODEL), row(D_MODEL), vec, vec],
        out_shape=[jax.ShapeDtypeStruct((S, D_MODEL), F32), jax.ShapeDtypeStruct((S, D_MODEL), BF16),
                   jax.ShapeDtypeStruct((1, D_MODEL), F32), jax.ShapeDtypeStruct((1, D_MODEL), F32)],
        compiler_params=_params(("arbitrary",)),
    )(dy, xhat, rstd, g)


def _loss_head(y, target):
    S = y.shape[0]

    def body(y_ref, t_ref, l_ref, dy_ref):
        i = pl.program_id(0)
        err = y_ref[...] - t_ref[...]
        dy_ref[...] = err * (1.0 / D_MODEL)
        part = (0.5 / D_MODEL) * jnp.sum(jnp.sum(err * err, axis=1, keepdims=True), axis=0, keepdims=True)

        @pl.when(i == 0)
        def _():
            l_ref[...] = part

        @pl.when(i > 0)
        def _():
            l_ref[...] += part

    row = pl.BlockSpec((TQ, D_MODEL), lambda i: (i, 0))
    return pl.pallas_call(
        body,
        name="loss_head",
        grid=(S // TQ,),
        in_specs=[row, row],
        out_specs=[pl.BlockSpec((1, 1), lambda i: (0, 0)), row],
        out_shape=[jax.ShapeDtypeStruct((1, 1), F32), jax.ShapeDtypeStruct((S, D_MODEL), F32)],
        compiler_params=_params(("arbitrary",)),
    )(y, target)


def _pad_gate(gate_w, gate_b):
    gw = gate_w.reshape(GATE_RANK, B_HEADS, B_DK)
    gw = jnp.pad(gw, ((0, LANES - GATE_RANK), (0, 0), (0, LANES - B_DK))).reshape(LANES, B_HEADS * LANES)
    gb = jnp.pad(gate_b.reshape(B_HEADS, B_DK), ((0, 0), (0, LANES - B_DK))).reshape(1, B_HEADS * LANES)
    return gw.astype(BF16), gb.astype(F32)


def _layer_fwd(x, xb, xt, mem_b, w_in, w_kv, w_out, u, gw, gb, gn, ln_g, ln_b):
    h = _matmul(xb, w_in, mode="nn", out_dtype=BF16, tm=1024, tn=768, tk=D_MODEL, name="in_proj")
    mkv = _matmul(mem_b, w_kv, mode="nn", out_dtype=BF16, tm=mem_b.shape[0], tn=1024, tk=D_MODEL, name="mem_kv")
    ycat, ycat_t = _band_fwd(h, u)
    ycat, ycat_t, o_pre, states = _gla_fwd(h, gw, gb, gn, ycat, ycat_t)
    ycat, ycat_t = _mem_fwd(h, mkv, ycat, ycat_t)
    y, ybf, yt, xhat, rstd = _outproj_ln(ycat, w_out, x, ln_g, ln_b)
    return y, ybf, yt, (xt, h, mkv, ycat_t, o_pre, states, xhat, rstd)


def _layer_bwd(dy, saved, mem_b, w_in, w_out, u, gw, gb, gn, ln_g):
    xt, h, mkv, ycat_t, o_pre, states, xhat, rstd = saved
    S = dy.shape[0]
    du, dub, d_ln_g, d_ln_b = _ln_bwd(dy, xhat, rstd, ln_g)
    dycat = _matmul(dub, w_out, mode="nt", out_dtype=BF16, tm=1024, tn=1024, tk=D_MODEL, name="dycat")
    d_w_out = _matmul(ycat_t, dub, mode="nn", out_dtype=F32, tm=1024, tn=1024, tk=2048, name="d_w_out")
    dh, d_u = _band_bwd(h, u, dycat)
    dh, dlr, dgw, dgb, dgn = _gla_bwd(h, gw, gb, gn, o_pre, states, dycat, dh)
    dh, dmkv = _mem_bwd(h, mkv, dycat, dh)
    dh = _lr_fill(dlr, dh)
    d_w_kv = _matmul(mem_b, dmkv, mode="tn", out_dtype=F32, tm=1024, tn=1024, tk=mem_b.shape[0], name="d_w_kv")
    dx = _matmul(dh, w_in, mode="nt", out_dtype=F32, tm=1024, tn=1024, tk=2560, name="dx", add=du, add_scale=ALPHA)
    d_w_in = _matmul(xt, dh, mode="nn", out_dtype=F32, tm=1024, tn=768, tk=2048, name="d_w_in")
    return dx, (d_w_in, d_u, dgw, dgb, dgn, d_w_kv, d_w_out, d_ln_g, d_ln_b)


def _unpad_heads(w):
    r = w.shape[0]
    return w.reshape(r, B_HEADS, LANES)[:, :, :B_DK].reshape(r, B_KEY_WIDTH)


def _padded_col_of():
    col, o = np.zeros(IN_WIDTH, np.int64), 0
    for seg in (SEG_Q, SEG_K, SEG_V, SEG_Z):
        for hd in range(A_HEADS):
            col[o:o + LANES] = C_A + (hd // A_HPS) * 4 * A_HW + seg * A_HW + (hd % A_HPS) * LANES + np.arange(LANES)
            o += LANES
    for seg, width in ((SEG_Q, B_DK), (SEG_K, B_DK), (SEG_V, LANES), (SEG_Z, LANES)):
        for hd in range(B_HEADS):
            col[o:o + width] = C_B + hd * 4 * LANES + seg * LANES + np.arange(width)
            o += width
    col[o:o + GATE_RANK] = C_LR + np.arange(GATE_RANK)
    o += GATE_RANK
    for seg in (0, 1):
        for hd in range(M_HEADS):
            col[o:o + LANES] = C_M + hd * 2 * LANES + seg * LANES + np.arange(LANES)
            o += LANES
    assert o == IN_WIDTH
    return col


def _runs(idx):
    out, start = [], 0
    for k in range(1, len(idx) + 1):
        if k == len(idx) or idx[k] != idx[k - 1] + 1:
            out.append((int(idx[start]), k - start))
            start = k
    return out


def _chip_columns(g, j, n):
    runs = _runs(_padded_col_of()[j * n:(j + 1) * n])
    return jnp.concatenate([g[:, a:a + ln] for a, ln in runs], axis=1)


def _padded_from_shards(shards):
    n = shards[0].shape[1]
    src = np.full(HP, -1, np.int64)
    src[_padded_col_of()] = np.arange(IN_WIDTH)
    parts, k = [], 0
    while k < HP:
        e = k + 1
        if src[k] < 0:
            while e < HP and src[e] < 0:
                e += 1
            parts.append(jnp.zeros((shards[0].shape[0], e - k), shards[0].dtype))
        else:
            while e < HP and src[e] == src[e - 1] + 1 and src[e] // n == src[k] // n:
                e += 1
            parts.append(shards[src[k] // n][:, src[k] % n:src[k] % n + e - k])
        k = e
    return jnp.concatenate(parts, axis=1)


ADAMW_BLOCK_BYTES = 1 << 20


def _adamw(w, g, m, v, name):
    L, R, C = w.shape
    tl, tr = 1, R
    if R * C * 4 <= ADAMW_BLOCK_BYTES:
        tl = max(d for d in range(1, L + 1) if L % d == 0 and d * R * C * 4 <= ADAMW_BLOCK_BYTES)
    else:
        for cand in (256, 128, 64, 32, 16, 8):
            if R % cand == 0 and R > cand:
                tr = cand
                break

    def body(w_ref, g_ref, m_ref, v_ref, d_ref, nm_ref, nv_ref):
        g_ = g_ref[...]
        nm = ADAM_B1 * m_ref[...] + (1.0 - ADAM_B1) * g_
        nv = ADAM_B2 * v_ref[...] + (1.0 - ADAM_B2) * (g_ * g_)
        m_hat = nm / (1.0 - ADAM_B1 ** ADAM_STEP)
        v_hat = nv / (1.0 - ADAM_B2 ** ADAM_STEP)
        d_ref[...] = -ADAM_LR * (m_hat / (jnp.sqrt(v_hat) + ADAM_EPS) + ADAM_WD * w_ref[...])
        nm_ref[...] = nm
        nv_ref[...] = nv

    spec = pl.BlockSpec((tl, tr, C), lambda l, i: (l, i, 0))
    sd = jax.ShapeDtypeStruct((L, R, C), F32)
    return pl.pallas_call(
        body, name=name, grid=(L // tl, R // tr), in_specs=[spec] * 4, out_specs=[spec] * 3, out_shape=[sd] * 3,
        compiler_params=_params(("parallel", "parallel")),
    )(w, g, m, v)


def _adamw_nd(w, g, m, v, name):
    shape = w.shape
    f = (lambda a: a) if w.ndim == 3 else (lambda a: a.reshape(1, shape[0], shape[1]))
    return tuple(o.reshape(shape) for o in _adamw(f(w), f(g), f(m), f(v), name))


ANY = pl.BlockSpec(memory_space=pl.ANY)


def _place():
    x, y, c = lax.axis_index("x"), lax.axis_index("y"), lax.axis_index("c")
    chips = [(1 - x, y), (x, 1 - y), (1 - x, 1 - y)]
    return x, y, c, chips


def _gather_chips(shard, name):
    R, C = shard.shape
    half = R // 2
    assert half % 16 == 0

    def body(s_ref, o_ref, send, recv):
        x, y, c, chips = _place()
        mine = pl.ds(pl.multiple_of(c * half, 16), half)
        other = pl.ds(pl.multiple_of((1 - c) * half, 16), half)

        def cp(k, src, chip, rows, to):
            dst = o_ref.at[2 * chip[0] + chip[1], rows]
            return pltpu.make_async_remote_copy(src_ref=dst if src is None else src, dst_ref=dst, send_sem=send.at[k],
                                                recv_sem=recv.at[k], device_id=to, device_id_type=MESH)

        first = [cp(k, s_ref.at[mine], (x, y), mine, (*chip, c)) for k, chip in enumerate(chips)]
        for f in first:
            f.start()
        passed = [cp(3 + k, None, chip, mine, (x, y, 1 - c)) for k, chip in enumerate(chips)]
        for k, chip in enumerate(chips):
            cp(k, None, chip, mine, (x, y, c)).wait_recv()
            passed[k].start()
        for k, chip in enumerate(chips):
            cp(3 + k, None, chip, other, (x, y, c)).wait_recv()
        for f in first + passed:
            f.wait_send()

    return pl.pallas_call(
        body, name=name, in_specs=[ANY], out_specs=ANY,
        out_shape=jax.ShapeDtypeStruct((N_CHIPS, R, C), shard.dtype),
        scratch_shapes=[pltpu.SemaphoreType.DMA((6,)), pltpu.SemaphoreType.DMA((6,))],
    )(shard)


def _pair_exchange(parts, half, L, name):
    n = len(parts) // L
    C = parts[0][0].shape[1]

    def body(*refs):
        ins, (o_ref, send, recv) = refs[:len(parts)], refs[len(parts):]
        x, y, c, _ = _place()
        cps = []
        for k, (_, first) in enumerate(parts):
            rows = pl.ds(pl.multiple_of(first + (1 - c) * half, 8), half)
            cps.append(pltpu.make_async_remote_copy(
                src_ref=ins[k].at[rows], dst_ref=o_ref.at[k // L, k % L], send_sem=send.at[k], recv_sem=recv.at[k],
                device_id=(x, y, 1 - c), device_id_type=MESH))
        for cp in cps:
            cp.start()
        for cp in cps:
            cp.wait()

    return pl.pallas_call(
        body, name=name, in_specs=[ANY] * len(parts), out_specs=ANY,
        out_shape=jax.ShapeDtypeStruct((n, L, half, C), F32),
        scratch_shapes=[pltpu.SemaphoreType.DMA((len(parts),)), pltpu.SemaphoreType.DMA((len(parts),))],
    )(*[a for a, _ in parts])


def _chip_exchange(p, name):
    n, R, C = p.shape

    def body(p_ref, o_ref, send, recv):
        x, y, c, chips = _place()
        me = 2 * x + y
        cps = [pltpu.make_async_remote_copy(src_ref=p_ref.at[2 * chip[0] + chip[1]], dst_ref=o_ref.at[me],
                                            send_sem=send.at[k], recv_sem=recv.at[k], device_id=(*chip, c),
                                            device_id_type=MESH) for k, chip in enumerate(chips)]
        for cp in cps:
            cp.start()
        for k, chip in enumerate(chips):
            pltpu.make_async_remote_copy(src_ref=p_ref.at[me], dst_ref=o_ref.at[2 * chip[0] + chip[1]],
                                         send_sem=send.at[k], recv_sem=recv.at[k], device_id=(*chip, c),
                                         device_id_type=MESH).wait_recv()
        for cp in cps:
            cp.wait_send()

    return pl.pallas_call(
        body, name=name, in_specs=[ANY], out_specs=ANY, out_shape=jax.ShapeDtypeStruct((n, R, C), p.dtype),
        scratch_shapes=[pltpu.SemaphoreType.DMA((3,)), pltpu.SemaphoreType.DMA((3,))],
    )(p)


def _pair_gather(t2, name):
    def body(t_ref, o_ref, send, recv):
        del t_ref
        x, y, c, _ = _place()
        cp = pltpu.make_async_remote_copy(src_ref=o_ref.at[c], dst_ref=o_ref.at[c], send_sem=send, recv_sem=recv,
                                          device_id=(x, y, 1 - c), device_id_type=MESH)
        cp.start()
        pltpu.make_async_remote_copy(src_ref=o_ref.at[c], dst_ref=o_ref.at[1 - c], send_sem=send, recv_sem=recv,
                                     device_id=(x, y, 1 - c), device_id_type=MESH).wait_recv()
        cp.wait_send()

    return pl.pallas_call(
        body, name=name, in_specs=[ANY], out_specs=ANY, out_shape=jax.ShapeDtypeStruct(t2.shape, t2.dtype),
        input_output_aliases={0: 0},
        scratch_shapes=[pltpu.SemaphoreType.DMA, pltpu.SemaphoreType.DMA],
    )(t2)


def _add_halves(parts, got, c_idx, name):
    n, L, half, C = got.shape
    tr = 64

    def body(*refs):
        ins, (got_ref, o_ref) = refs[1:1 + len(parts)], refs[1 + len(parts):]
        for k in range(len(parts)):
            o_ref[k // L, k % L] = (ins[k][...] + got_ref[k // L, k % L]).astype(BF16)

    def rows_of(first):
        assert first % tr == 0 and half % tr == 0
        return lambda i, c: (first // tr + c[0] * (half // tr) + i, 0)

    whole = pl.BlockSpec((n, L, tr, C), lambda i, c: (0, 0, i, 0))
    return pl.pallas_call(
        body, name=name,
        grid_spec=pltpu.PrefetchScalarGridSpec(
            num_scalar_prefetch=1, grid=(half // tr,),
            in_specs=[pl.BlockSpec((tr, C), rows_of(first)) for _, first in parts] + [whole],
            out_specs=whole),
        out_shape=jax.ShapeDtypeStruct((n, L, half, C), BF16),
        compiler_params=_params(("parallel",)),
    )(c_idx, *[a for a, _ in parts], got)


def _add_slots(r, c_idx, name):
    n, R, C = r.shape
    tr = 256

    def body(c_ref, r_ref, o_ref):
        acc = r_ref[0].astype(F32)
        for j in range(1, n):
            acc = acc + r_ref[j].astype(F32)
        o_ref[0] = acc

    return pl.pallas_call(
        body, name=name,
        grid_spec=pltpu.PrefetchScalarGridSpec(
            num_scalar_prefetch=1, grid=(R // tr,),
            in_specs=[pl.BlockSpec((n, tr, C), lambda i, c: (0, i, 0))],
            out_specs=pl.BlockSpec((1, tr, C), lambda i, c: (c[0], i, 0))),
        out_shape=jax.ShapeDtypeStruct((2, R, C), F32),
        compiler_params=_params(("parallel",)),
    )(c_idx, r)


def _reduce_scatter(parts, half, L, c_idx, chip, tag):
    C = parts[0][0].shape[1]
    got = _pair_exchange(parts, half, L, "rs_pair_" + tag)
    p = _add_halves(parts, got, c_idx, "rs_add2_" + tag).reshape(N_CHIPS, L * half, C)
    q = _chip_exchange(p, "rs_chip_" + tag)
    q = lax.dynamic_update_slice_in_dim(q, lax.dynamic_slice_in_dim(p, chip, 1, axis=0), chip, axis=0)
    t2 = _add_slots(q, c_idx, "rs_add4_" + tag)
    both = _pair_gather(t2, "rs_gather_" + tag).reshape(2, L, half, C)
    return both.transpose(1, 0, 2, 3).reshape(L, 2 * half, C)


def _all_reduce_small(buf, name):
    R = buf.shape[0]

    def flipped(k, x, y, c):
        return ((1 - x) if k & 4 else x, (1 - y) if k & 2 else y, (1 - c) if k & 1 else c)

    def body(b_ref, o_ref, land, send, recv):
        x, y, c, _ = _place()
        me = 4 * x + 2 * y + c
        land[me] = b_ref[...]
        cps = []
        for k in range(1, N_DEV):
            peer = flipped(k, x, y, c)
            cps.append(pltpu.make_async_remote_copy(src_ref=b_ref, dst_ref=land.at[me], send_sem=send.at[k - 1],
                                                    recv_sem=recv.at[k - 1], device_id=peer, device_id_type=MESH))
        for cp in cps:
            cp.start()
        for k in range(1, N_DEV):
            peer = flipped(k, x, y, c)
            slot = 4 * peer[0] + 2 * peer[1] + peer[2]
            pltpu.make_async_remote_copy(src_ref=b_ref, dst_ref=land.at[slot], send_sem=send.at[k - 1],
                                         recv_sem=recv.at[k - 1], device_id=peer, device_id_type=MESH).wait_recv()
        for cp in cps:
            cp.wait_send()
        acc = land[0]
        for j in range(1, N_DEV):
            acc = acc + land[j]
        o_ref[...] = acc

    vm = pl.BlockSpec(memory_space=pltpu.VMEM)
    return pl.pallas_call(
        body, name=name, in_specs=[vm], out_specs=vm,
        out_shape=jax.ShapeDtypeStruct((R, LANES), F32),
        scratch_shapes=[pltpu.VMEM((N_DEV, R, LANES), F32), pltpu.SemaphoreType.DMA((N_DEV - 1,)),
                        pltpu.SemaphoreType.DMA((N_DEV - 1,))],
    )(buf)


def kernel(x, mem, w_in, a_rel_bias, b_gate_w, b_gate_b, b_norm_g, w_mem_kv, w_out, ln_g, ln_b, loss_target, m_w_in, m_a_rel_bias, m_b_gate_w, m_b_gate_b, m_b_norm_g, m_w_mem_kv, m_w_out, m_ln_g, m_ln_b, v_w_in, v_a_rel_bias, v_b_gate_w, v_b_gate_b, v_b_norm_g, v_w_mem_kv, v_w_out, v_ln_g, v_ln_b):
    L = w_in.shape[0]
    S = x.shape[1]
    cx, cy, cc = lax.axis_index("x"), lax.axis_index("y"), lax.axis_index("c")
    chip = 2 * cx + cy
    c_idx = jnp.reshape(cc, (1,)).astype(jnp.int32)

    def gathered(w, name):
        shard = w.astype(BF16).reshape(-1, w.shape[2])
        return lax.dynamic_update_slice_in_dim(_gather_chips(shard, name), shard[None], chip, axis=0)

    n_in = w_in.shape[2]
    win_all = gathered(w_in, "gather_w_in")
    r_kv = w_mem_kv.shape[1]
    wkv_all = gathered(w_mem_kv, "gather_w_kv")
    r_out = w_out.shape[1]
    wout_all = gathered(w_out, "gather_w_out")
    win_all = win_all.reshape(N_CHIPS, L, D_MODEL, n_in)
    wkv_all = wkv_all.reshape(N_CHIPS, L, r_kv, w_mem_kv.shape[2])
    wout_all = wout_all.reshape(N_CHIPS, L, r_out, D_MODEL)

    gw_cols = b_gate_w.shape[2]
    gw_slot = jnp.zeros((N_CHIPS, L, GATE_RANK, gw_cols), F32)
    gw_slot = lax.dynamic_update_slice(gw_slot, (0.5 * b_gate_w)[None], (chip, 0, 0, 0))
    gw_flat = gw_slot.reshape(-1)
    n_gw = gw_flat.shape[0]
    pad = (-n_gw) % (8 * LANES)
    gw_full = _all_reduce_small(jnp.pad(gw_flat, (0, pad)).reshape(-1, LANES), "gather_gate_w").reshape(-1)[:n_gw]
    gw_full = gw_full.reshape(N_CHIPS, L, GATE_RANK, gw_cols).transpose(1, 2, 0, 3).reshape(L, GATE_RANK, B_KEY_WIDTH)

    xs = x.reshape(S, D_MODEL)
    mem_b = mem.reshape(mem.shape[1], D_MODEL).astype(BF16)
    target = loss_target.reshape(S, D_MODEL)

    layer_w = []
    for l in range(L):
        w_in_l = _padded_from_shards([win_all[j, l] for j in range(N_CHIPS)])
        w_kv_l = jnp.concatenate([wkv_all[j, l] for j in range(N_CHIPS)], axis=0)
        w_out_l = jnp.concatenate([wout_all[j, l] for j in range(N_CHIPS)], axis=0)
        gw_l, gb_l = _pad_gate(gw_full[l], b_gate_b[l])
        layer_w.append((w_in_l, w_kv_l, w_out_l, _bias_by_offset(a_rel_bias[l]), gw_l, gb_l,
                        b_norm_g[l].reshape(1, LANES), ln_g[l].reshape(1, D_MODEL), ln_b[l].reshape(1, D_MODEL)))

    y, yb = xs, xs.astype(BF16)
    yt = _transpose(yb, "x_t")
    saved = []
    for l in range(L):
        w_in_l, w_kv_l, w_out_l, u_l, gw_l, gb_l, gn_l, lg_l, lb_l = layer_w[l]
        y, yb, yt, sv = _layer_fwd(y, yb, yt, mem_b, w_in_l, w_kv_l, w_out_l, u_l, gw_l, gb_l, gn_l, lg_l, lb_l)
        saved.append(sv)
    loss_part, dy = _loss_head(y, target)

    grads = [None] * L
    for l in reversed(range(L)):
        w_in_l, w_kv_l, w_out_l, u_l, gw_l, gb_l, gn_l, lg_l, lb_l = layer_w[l]
        dy, grads[l] = _layer_bwd(dy, saved[l], mem_b, w_in_l, w_out_l, u_l, gw_l, gb_l, gn_l, lg_l)
    grad_x = dy.reshape(x.shape)

    g_rel = jnp.stack([_bias_grad_from_offset(g[1]) for g in grads])
    g_gw = jnp.stack([_unpad_heads(g[2][:GATE_RANK]) for g in grads])
    g_gb = jnp.stack([_unpad_heads(g[3])[0] for g in grads])
    g_gn = jnp.stack([g[4][0] for g in grads])
    g_lg = jnp.stack([g[7][0] for g in grads])
    g_lb = jnp.stack([g[8][0] for g in grads])

    in_parts = [(_chip_columns(grads[l][0], j, n_in), 0) for j in range(N_CHIPS) for l in range(L)]
    r_w_in = _reduce_scatter(in_parts, D_MODEL // 2, L, c_idx, chip, "w_in")
    kv_parts = [(grads[l][5], j * r_kv) for j in range(N_CHIPS) for l in range(L)]
    r_w_kv = _reduce_scatter(kv_parts, r_kv // 2, L, c_idx, chip, "w_kv")
    out_parts = [(grads[l][6], j * r_out) for j in range(N_CHIPS) for l in range(L)]
    r_w_out = _reduce_scatter(out_parts, r_out // 2, L, c_idx, chip, "w_out")

    small = [g_rel, g_gw, g_gb, g_gn, g_lg, g_lb, loss_part]
    flat = jnp.concatenate([s.reshape(-1) for s in small])
    n_small = flat.shape[0]
    pad = (-n_small) % (8 * LANES)
    red = _all_reduce_small(jnp.pad(flat, (0, pad)).reshape(-1, LANES), "all_reduce_small").reshape(-1)
    outs, off = [], 0
    for s in small:
        outs.append(red[off:off + s.size].reshape(s.shape))
        off += s.size
    g_rel, g_gw, g_gb, g_gn, g_lg, g_lb, loss = outs
    loss = loss.reshape(())
    g_gw = lax.dynamic_slice_in_dim(g_gw.reshape(L, GATE_RANK, N_CHIPS, gw_cols), chip, 1, axis=2).reshape(L, GATE_RANK, gw_cols)

    g_list = [r_w_in, g_rel, g_gw, g_gb, g_gn, r_w_kv, r_w_out, g_lg, g_lb]
    w_list = [w_in, a_rel_bias, b_gate_w, b_gate_b, b_norm_g, w_mem_kv, w_out, ln_g, ln_b]
    m_list = [m_w_in, m_a_rel_bias, m_b_gate_w, m_b_gate_b, m_b_norm_g, m_w_mem_kv, m_w_out, m_ln_g, m_ln_b]
    v_list = [v_w_in, v_a_rel_bias, v_b_gate_w, v_b_gate_b, v_b_norm_g, v_w_mem_kv, v_w_out, v_ln_g, v_ln_b]
    names = ["w_in", "rel", "gate_w", "gate_b", "norm_g", "w_kv", "w_out", "ln_g", "ln_b"]
    to_cols = lambda a: jnp.transpose(a, (2, 0, 1))
    upd = [tuple(jnp.transpose(o, (1, 2, 0)) for o in
                 _adamw(to_cols(w_in), to_cols(r_w_in), to_cols(m_w_in), to_cols(v_w_in), "adamw_w_in"))]
    upd += [_adamw_nd(w, g, m, v, "adamw_" + n)
            for w, g, m, v, n in list(zip(w_list, g_list, m_list, v_list, names))[1:]]
    deltas = [u_[0] for u_ in upd]
    new_m = [u_[1] for u_ in upd]
    new_v = [u_[2] for u_ in upd]
    return (loss, grad_x, *g_list, *deltas, *new_m, *new_v)
```

```python
import functools

import numpy as np
import jax
import jax.numpy as jnp
from jax import lax
from jax.experimental import pallas as pl
from jax.experimental.pallas import tpu as pltpu

F32 = jnp.float32
BF16 = jnp.bfloat16
MESH = pl.DeviceIdType.MESH

D_MODEL = 2048
DEPTH = 4
CHUNK = 64
LEFT_CHUNKS = 8
MAX_REL = 128
N_REL = 2 * MAX_REL + 1
A_HEADS = 8
HEAD_DIM = 128
B_HEADS = 4
B_DK = 64
M_HEADS = 4
GATE_RANK = 16
GATE_TAU = 16.0
A_WIDTH = A_HEADS * HEAD_DIM
B_WIDTH = B_HEADS * HEAD_DIM
B_KEY_WIDTH = B_HEADS * B_DK
M_WIDTH = M_HEADS * HEAD_DIM
IN_WIDTH = 4 * A_WIDTH + 2 * B_KEY_WIDTH + 2 * B_WIDTH + GATE_RANK + 2 * M_WIDTH
ALPHA = (2.0 * DEPTH) ** 0.25
LN_EPS = 1e-5
RMS_EPS = 1e-6
NEG_INF = -1e30
ADAM_LR = 0.001
ADAM_B1 = 0.9
ADAM_B2 = 0.999
ADAM_EPS = 1e-08
ADAM_WD = 0.01
ADAM_STEP = 10

LANES = 128
VMEM_LIMIT = 56 * 1024 * 1024

C_A, C_B, C_M, C_LR = 0, 4096, 6144, 7168
HP = 7680
SEG_Q, SEG_K, SEG_V, SEG_Z = 0, 1, 2, 3
TQ = 512
CPB = TQ // CHUNK
N_CHIPS = 4
N_DEV = 8


def _params(sem, vmem=VMEM_LIMIT):
    return pltpu.CompilerParams(dimension_semantics=sem, vmem_limit_bytes=vmem)


def _dot(a, b):
    return jnp.dot(a, b, preferred_element_type=F32)


def _dot_nt(a, b):
    return lax.dot_general(a, b, (((1,), (1,)), ((), ())), preferred_element_type=F32)


def _dot_tn(a, b):
    return lax.dot_general(a, b, (((0,), (0,)), ((), ())), preferred_element_type=F32)


def _sigmoid(x):
    return 1.0 / (1.0 + jnp.exp(-x))


def _split3(x):
    hi = x.astype(BF16)
    r = x - hi.astype(F32)
    mid = r.astype(BF16)
    lo = (r - mid.astype(F32)).astype(BF16)
    return hi, mid, lo


def _dot3(m_bf, x):
    hi, mid, lo = _split3(x)
    return _dot(m_bf, hi) + _dot(m_bf, mid) + _dot(m_bf, lo)


def _matmul(a, b, *, mode, out_dtype, tm, tn, tk, name, add=None, add_scale=1.0, rider=None):
    if mode == "nn":
        (M, K), (K2, N) = a.shape, b.shape
        a_spec = pl.BlockSpec((tm, tk), lambda i, j, k: (i, k))
        b_spec = pl.BlockSpec((tk, tn), lambda i, j, k: (k, j))
        dot = _dot
    elif mode == "nt":
        (M, K), (N, K2) = a.shape, b.shape
        a_spec = pl.BlockSpec((tm, tk), lambda i, j, k: (i, k))
        b_spec = pl.BlockSpec((tn, tk), lambda i, j, k: (j, k))
        dot = _dot_nt
    else:
        (K, M), (K2, N) = a.shape, b.shape
        a_spec = pl.BlockSpec((tk, tm), lambda i, j, k: (k, i))
        b_spec = pl.BlockSpec((tk, tn), lambda i, j, k: (k, j))
        dot = _dot_tn
    assert K == K2 and M % tm == 0 and N % tn == 0 and K % tk == 0, (a.shape, b.shape, mode)
    nk = K // tk
    has_add = add is not None
    assert nk == 1 or out_dtype == F32
    grid = (M // tm, N // tn, nk)
    hop, srcs, bufs = rider if rider is not None else (None, [], [])
    n_in = 2 + has_add

    def body(*refs):
        a_ref, b_ref = refs[:2]
        add_ref = refs[2] if has_add else None
        src_refs = refs[n_in:n_in + len(srcs)]
        o_ref = refs[n_in + len(srcs) + len(bufs)]
        buf_refs = refs[n_in + len(srcs) + len(bufs) + 1:n_in + len(srcs) + 2 * len(bufs) + 1]
        sems = refs[n_in + len(srcs) + 2 * len(bufs) + 1:]
        i, j, k = pl.program_id(0), pl.program_id(1), pl.program_id(2)
        if hop is not None:
            @pl.when(jnp.logical_and(jnp.logical_and(i == 0, j == 0), k == 0))
            def _():
                hop.start(src_refs, buf_refs, *sems)

        part = dot(a_ref[...].astype(BF16), b_ref[...].astype(BF16))

        @pl.when(k == 0)
        def _():
            first = part + add_scale * add_ref[...] if has_add else part
            o_ref[...] = first.astype(out_dtype)

        if nk > 1:
            @pl.when(k > 0)
            def _():
                o_ref[...] += part

        if hop is not None:
            @pl.when(jnp.logical_and(jnp.logical_and(i == grid[0] - 1, j == grid[1] - 1), k == nk - 1))
            def _():
                hop.wait(src_refs, buf_refs, *sems)

    in_specs = [a_spec, b_spec]
    args = [a, b]
    if has_add:
        in_specs.append(pl.BlockSpec((tm, tn), lambda i, j, k: (i, j)))
        args.append(add)
    out = pl.pallas_call(
        body,
        name=name,
        grid=grid,
        in_specs=in_specs + [ANY] * (len(srcs) + len(bufs)),
        out_specs=[pl.BlockSpec((tm, tn), lambda i, j, k: (i, j))] + [ANY] * len(bufs),
        out_shape=[jax.ShapeDtypeStruct((M, N), out_dtype)] + [jax.ShapeDtypeStruct(x.shape, x.dtype) for x in bufs],
        input_output_aliases={n_in + len(srcs) + t: 1 + t for t in range(len(bufs))},
        scratch_shapes=hop.sems() if hop is not None else [],
        compiler_params=_params(("parallel", "parallel", "arbitrary") if hop is None
                                else ("arbitrary", "arbitrary", "arbitrary")),
    )(*args, *srcs, *bufs)
    return out[0] if hop is None else (out[0], list(out[1:]))


def _transpose(a, name):
    R, C = a.shape
    t = 512

    def body(a_ref, o_ref):
        o_ref[...] = a_ref[...].T

    return pl.pallas_call(
        body, name=name, grid=(R // t, C // t),
        in_specs=[pl.BlockSpec((t, t), lambda i, j: (i, j))],
        out_specs=pl.BlockSpec((t, t), lambda i, j: (j, i)),
        out_shape=jax.ShapeDtypeStruct((C, R), a.dtype),
        compiler_params=_params(("parallel", "parallel")),
    )(a)


def _chunk_of(rows):
    return lax.shift_right_logical(rows, CHUNK.bit_length() - 1)


A_HPS = 2
A_HW = A_HPS * LANES


def _band_bias(u_row, first):
    bias = pltpu.roll(jnp.broadcast_to(u_row, (TQ, 2 * TQ)), 0, 1, stride=1, stride_axis=0)
    qc = _chunk_of(lax.broadcasted_iota(jnp.int32, (TQ, 2 * TQ), 0))
    col = lax.broadcasted_iota(jnp.int32, (TQ, 2 * TQ), 1)
    kc = _chunk_of(jnp.bitwise_and(col, TQ - 1))
    ok = jnp.logical_or(jnp.logical_and(col < TQ, kc >= qc), jnp.logical_and(col >= TQ, kc <= qc))
    return jnp.where(ok, bias, NEG_INF) + jnp.where(col < TQ, first * NEG_INF, 0.0)


def _band_probs(q, kp, kc, bias):
    scale = HEAD_DIM ** -0.5
    sp = _dot_nt(q, kp) * scale + bias[:, :TQ]
    sc = _dot_nt(q, kc) * scale + bias[:, TQ:]
    m = jnp.maximum(jnp.max(sp, axis=1, keepdims=True), jnp.max(sc, axis=1, keepdims=True))
    pp = jnp.exp(sp - m)
    pc = jnp.exp(sc - m)
    inv = 1.0 / (jnp.sum(pp, axis=1, keepdims=True) + jnp.sum(pc, axis=1, keepdims=True))
    return pp, pc, inv


def _band_specs(nq):
    def col(seg, h):
        return C_A // A_HW + 4 * h + seg

    q_spec = pl.BlockSpec((TQ, A_HW), lambda h, i: (jnp.minimum(i, nq - 1), col(SEG_Q, h)))
    kp_spec = pl.BlockSpec((TQ, A_HW), lambda h, i: (jnp.clip(i - 1, 0, nq - 1), col(SEG_K, h)))
    kc_spec = pl.BlockSpec((TQ, A_HW), lambda h, i: (jnp.minimum(i, nq - 1), col(SEG_K, h)))
    vp_spec = pl.BlockSpec((TQ, A_HW), lambda h, i: (jnp.clip(i - 1, 0, nq - 1), col(SEG_V, h)))
    vc_spec = pl.BlockSpec((TQ, A_HW), lambda h, i: (jnp.minimum(i, nq - 1), col(SEG_V, h)))
    z_spec = pl.BlockSpec((TQ, A_HW), lambda h, i: (jnp.minimum(i, nq - 1), col(SEG_Z, h)))
    u_spec = pl.BlockSpec((A_HPS, 1, 2 * TQ), lambda h, i: (h, 0, 0))
    return q_spec, kp_spec, kc_spec, vp_spec, vc_spec, z_spec, u_spec


def _band_fwd(h, u):
    S = h.shape[0]
    nq = S // TQ

    def body(q_ref, kp_ref, kc_ref, vp_ref, vc_ref, z_ref, u_ref, y_ref, yt_ref, bias_scr):
        i = pl.program_id(1)

        @pl.when(i <= 1)
        def _():
            for hh in range(A_HPS):
                bias_scr[hh] = _band_bias(u_ref[hh], (i == 0).astype(F32))

        for hh in range(A_HPS):
            cs = slice(hh * LANES, (hh + 1) * LANES)
            pp, pc, inv = _band_probs(q_ref[:, cs], kp_ref[:, cs], kc_ref[:, cs], bias_scr[hh])
            o = (_dot(pp.astype(BF16), vp_ref[:, cs]) + _dot(pc.astype(BF16), vc_ref[:, cs])) * inv
            z = z_ref[:, cs].astype(F32)
            y = o * (z * _sigmoid(z))
            y_ref[:, cs] = y.astype(BF16)
            yt_ref[cs, :] = y.T.astype(BF16)

    specs = _band_specs(nq)
    return pl.pallas_call(
        body,
        name="band_fwd",
        grid=(A_HEADS // A_HPS, nq),
        in_specs=[specs[0], specs[1], specs[2], specs[3], specs[4], specs[5], specs[6]],
        out_specs=[pl.BlockSpec((TQ, A_HW), lambda h, i: (i, h)), pl.BlockSpec((A_HW, TQ), lambda h, i: (h, i))],
        out_shape=[jax.ShapeDtypeStruct((S, D_MODEL), BF16), jax.ShapeDtypeStruct((D_MODEL, S), BF16)],
        scratch_shapes=[pltpu.VMEM((A_HPS, TQ, 2 * TQ), F32)],
        compiler_params=_params(("parallel", "arbitrary")),
    )(h, h, h, h, h, h, u)


def _band_bwd(h, u, dycat):
    S = h.shape[0]
    nq = S // TQ
    scale = HEAD_DIM ** -0.5
    qs, ks, vs, zs = (slice(s * A_HW, (s + 1) * A_HW) for s in (SEG_Q, SEG_K, SEG_V, SEG_Z))

    def body(q_ref, kp_ref, kc_ref, vp_ref, vc_ref, z_ref, u_ref, dy_ref,
             dh_ref, du_ref, bias_scr, db_scr, ckt_scr, cvt_scr, cq_scr, cz_scr):
        i = pl.program_id(1)

        @pl.when(i <= 1)
        def _():
            for hh in range(A_HPS):
                bias_scr[hh] = _band_bias(u_ref[hh], (i == 0).astype(F32))

        @pl.when(i == 0)
        def _():
            db_scr[...] = jnp.zeros_like(db_scr)
            ckt_scr[...] = jnp.zeros_like(ckt_scr)
            cvt_scr[...] = jnp.zeros_like(cvt_scr)
            cq_scr[...] = jnp.zeros_like(cq_scr)
            cz_scr[...] = jnp.zeros_like(cz_scr)

        @pl.when(i < nq)
        def _():
            dh_ref[:, qs] = cq_scr[...]
            dh_ref[:, zs] = cz_scr[...]
            for hh in range(A_HPS):
                cs = slice(hh * LANES, (hh + 1) * LANES)
                q, kp, kc, vp, vc = q_ref[:, cs], kp_ref[:, cs], kc_ref[:, cs], vp_ref[:, cs], vc_ref[:, cs]
                pp, pc, inv = _band_probs(q, kp, kc, bias_scr[hh])
                pp, pc = pp * inv, pc * inv
                ppb, pcb = pp.astype(BF16), pc.astype(BF16)
                o = _dot(ppb, vp) + _dot(pcb, vc)
                z = z_ref[:, cs].astype(F32)
                sg = _sigmoid(z)
                dy = dy_ref[:, cs].astype(F32)
                do = dy * (z * sg)
                cz_scr[:, cs] = (dy * o * (sg * (1.0 + z * (1.0 - sg)))).astype(BF16)
                dob = do.astype(BF16)
                delta = jnp.sum(do * o, axis=1, keepdims=True)
                dsp = pp * (_dot_nt(dob, vp) - delta)
                dsc = pc * (_dot_nt(dob, vc) - delta)
                db_scr[hh, :, :TQ] += dsp
                db_scr[hh, :, TQ:] += dsc
                dspb, dscb = dsp.astype(BF16), dsc.astype(BF16)
                cq_scr[:, cs] = (scale * (_dot(dspb, kp) + _dot(dscb, kc))).astype(BF16)
                qt, dot_ = q.T, dob.T
                dh_ref[:, SEG_K * A_HW + hh * LANES:SEG_K * A_HW + (hh + 1) * LANES] = (
                    ckt_scr[cs, :] + scale * _dot(qt, dspb)).T.astype(BF16)
                dh_ref[:, SEG_V * A_HW + hh * LANES:SEG_V * A_HW + (hh + 1) * LANES] = (
                    cvt_scr[cs, :] + _dot(dot_, ppb)).T.astype(BF16)
                ckt_scr[cs, :] = scale * _dot(qt, dscb)
                cvt_scr[cs, :] = _dot(dot_, pcb)

        @pl.when(i == nq)
        def _():
            dh_ref[:, qs] = cq_scr[...]
            dh_ref[:, zs] = cz_scr[...]
            dh_ref[:, ks] = ckt_scr[...].T.astype(BF16)
            dh_ref[:, vs] = cvt_scr[...].T.astype(BF16)
            r0 = lax.broadcasted_iota(jnp.int32, (TQ, TQ), 0)
            r1 = lax.broadcasted_iota(jnp.int32, (TQ, TQ), 1)
            flip = (r0 + r1 == TQ - 1).astype(BF16)
            for hh in range(A_HPS):
                fl = _dot3(flip, db_scr[hh])
                rolled = pltpu.roll(fl, 0, 1, stride=1, stride_axis=0)
                du_ref[hh] = jnp.sum(rolled, axis=0, keepdims=True)

    specs = _band_specs(nq)
    dy_spec = pl.BlockSpec((TQ, A_HW), lambda h, i: (jnp.minimum(i, nq - 1), h))
    return pl.pallas_call(
        body,
        name="band_bwd",
        grid=(A_HEADS // A_HPS, nq + 1),
        in_specs=[specs[0], specs[1], specs[2], specs[3], specs[4], specs[5], specs[6], dy_spec],
        out_specs=[pl.BlockSpec((TQ, 4 * A_HW), lambda h, i: (jnp.maximum(i - 1, 0), C_A // (4 * A_HW) + h)),
                   pl.BlockSpec((A_HPS, 1, 2 * TQ), lambda h, i: (h, 0, 0))],
        out_shape=[jax.ShapeDtypeStruct((S, HP), BF16), jax.ShapeDtypeStruct((A_HEADS, 1, 2 * TQ), F32)],
        scratch_shapes=[pltpu.VMEM((A_HPS, TQ, 2 * TQ), F32), pltpu.VMEM((A_HPS, TQ, 2 * TQ), F32),
                        pltpu.VMEM((A_HW, TQ), F32), pltpu.VMEM((A_HW, TQ), F32),
                        pltpu.VMEM((TQ, A_HW), BF16), pltpu.VMEM((TQ, A_HW), BF16)],
        compiler_params=_params(("parallel", "arbitrary")),
    )(h, h, h, h, h, h, u, dycat)


def _bias_by_offset(table):
    far = jnp.broadcast_to(table[:, N_REL - 1:], (A_HEADS, TQ - MAX_REL))
    ramp = jnp.flip(table, axis=1)
    rest = jnp.broadcast_to(table[:, :1], (A_HEADS, 2 * TQ - CHUNK - (TQ + MAX_REL + 1)))
    wrap = jnp.broadcast_to(table[:, N_REL - 1:], (A_HEADS, CHUNK))
    return jnp.concatenate([far, ramp, rest, wrap], axis=1)[:, None, :]


def _bias_grad_from_offset(du):
    g = jnp.roll(du[:, 0, :], -(TQ - 1), axis=1)
    far = jnp.sum(g[:, :TQ - MAX_REL], axis=1) + jnp.sum(g[:, 2 * TQ - CHUNK:], axis=1)
    ramp = jnp.flip(g[:, TQ - MAX_REL:TQ + MAX_REL + 1], axis=1)
    return ramp.at[:, N_REL - 1].add(far)


def _mem_probs(q, mk):
    s = _dot_nt(q, mk) * (HEAD_DIM ** -0.5)
    p = jnp.exp(s - jnp.max(s, axis=1, keepdims=True))
    return p * (1.0 / jnp.sum(p, axis=1, keepdims=True))


def _mem_fwd(h, mkv, ycat, ycat_t):
    S = h.shape[0]
    nm = mkv.shape[0]
    c0 = (A_WIDTH + B_WIDTH) // LANES

    def body(q_ref, z_ref, mk_ref, mv_ref, yin_ref, ytin_ref, y_ref, yt_ref):
        del yin_ref, ytin_ref
        p = _mem_probs(q_ref[...], mk_ref[...])
        o = _dot(p.astype(BF16), mv_ref[...])
        z = z_ref[...].astype(F32)
        y = o * (z * _sigmoid(z))
        y_ref[...] = y.astype(BF16)
        yt_ref[...] = y.T.astype(BF16)

    return pl.pallas_call(
        body,
        name="mem_fwd",
        grid=(M_HEADS, S // TQ),
        in_specs=[pl.BlockSpec((TQ, LANES), lambda h, i: (i, C_M // LANES + 2 * h)),
                  pl.BlockSpec((TQ, LANES), lambda h, i: (i, C_M // LANES + 2 * h + 1)),
                  pl.BlockSpec((nm, LANES), lambda h, i: (0, h)),
                  pl.BlockSpec((nm, LANES), lambda h, i: (0, M_HEADS + h)), ANY, ANY],
        out_specs=[pl.BlockSpec((TQ, LANES), lambda h, i: (i, c0 + h)), pl.BlockSpec((LANES, TQ), lambda h, i: (c0 + h, i))],
        out_shape=[jax.ShapeDtypeStruct(ycat.shape, BF16), jax.ShapeDtypeStruct(ycat_t.shape, BF16)],
        input_output_aliases={4: 0, 5: 1},
        compiler_params=_params(("parallel", "arbitrary")),
    )(h, h, mkv, mkv, ycat, ycat_t)


def _mem_bwd(h, mkv, dycat, dh):
    S = h.shape[0]
    nm = mkv.shape[0]
    scale = HEAD_DIM ** -0.5

    def body(q_ref, z_ref, mk_ref, mv_ref, dy_ref, dhin_ref, dh_ref, dmk_ref, dmv_ref):
        del dhin_ref
        i = pl.program_id(1)
        q, mk, mv = q_ref[...], mk_ref[...], mv_ref[...]
        p = _mem_probs(q, mk)
        pb = p.astype(BF16)
        o = _dot(pb, mv)
        z = z_ref[...].astype(F32)
        sg = _sigmoid(z)
        dy = dy_ref[...].astype(F32)
        do = dy * (z * sg)
        dh_ref[:, LANES:] = (dy * o * (sg * (1.0 + z * (1.0 - sg)))).astype(BF16)
        dob = do.astype(BF16)
        ds = p * (_dot_nt(dob, mv) - jnp.sum(do * o, axis=1, keepdims=True))
        dsb = ds.astype(BF16)
        dh_ref[:, :LANES] = (scale * _dot(dsb, mk)).astype(BF16)
        dmk = scale * _dot_tn(dsb, q)
        dmv = _dot_tn(pb, dob)

        @pl.when(i == 0)
        def _():
            dmk_ref[...] = dmk
            dmv_ref[...] = dmv

        @pl.when(i > 0)
        def _():
            dmk_ref[...] += dmk
            dmv_ref[...] += dmv

    dh, dmk, dmv = pl.pallas_call(
        body,
        name="mem_bwd",
        grid=(M_HEADS, S // TQ),
        in_specs=[pl.BlockSpec((TQ, LANES), lambda h, i: (i, C_M // LANES + 2 * h)),
                  pl.BlockSpec((TQ, LANES), lambda h, i: (i, C_M // LANES + 2 * h + 1)),
                  pl.BlockSpec((nm, LANES), lambda h, i: (0, h)),
                  pl.BlockSpec((nm, LANES), lambda h, i: (0, M_HEADS + h)),
                  pl.BlockSpec((TQ, LANES), lambda h, i: (i, (A_WIDTH + B_WIDTH) // LANES + h)), ANY],
        out_specs=[pl.BlockSpec((TQ, 2 * LANES), lambda h, i: (i, C_M // (2 * LANES) + h)),
                   pl.BlockSpec((nm, LANES), lambda h, i: (0, h)),
                   pl.BlockSpec((nm, LANES), lambda h, i: (0, h))],
        out_shape=[jax.ShapeDtypeStruct(dh.shape, BF16),
                   jax.ShapeDtypeStruct((nm, M_WIDTH), F32), jax.ShapeDtypeStruct((nm, M_WIDTH), F32)],
        input_output_aliases={5: 0},
        compiler_params=_params(("parallel", "arbitrary")),
    )(h, h, mkv, mkv, dycat, dh)
    return dh, jnp.concatenate([dmk, dmv], axis=1)


def _chunk_masks():
    r = lax.broadcasted_iota(jnp.int32, (TQ, TQ), 0)
    c = lax.broadcasted_iota(jnp.int32, (TQ, TQ), 1)
    same = _chunk_of(r) == _chunk_of(c)
    return jnp.logical_and(same, c <= r), jnp.logical_and(same, c > r)


def _gla_gates(lr, gw, gb):
    logit = _dot(lr, gw) + gb
    sg = _sigmoid(logit)
    g = (jnp.minimum(logit, 0.0) - jnp.log(1.0 + jnp.exp(-jnp.abs(logit)))) * (1.0 / GATE_TAU)
    lo, _ = _chunk_masks()
    return sg, _dot3(lo.astype(BF16), g)


def _gla_factors(q, k, b):
    eb = jnp.exp(b)
    enb = jnp.exp(-b)
    return eb, enb, q * eb, q * enb, k * eb, k * enb


def _gla_intra(qp, qn, kp, kn):
    lo, up = _chunk_masks()
    return (jnp.where(lo, _dot_nt(qp.astype(BF16), kn.astype(BF16)), 0.0)
            + jnp.where(up, _dot_nt(qn.astype(BF16), kp.astype(BF16)), 0.0))


def _gla_specs(nb, rev):
    blk = (lambda i: nb - 1 - i) if rev else (lambda i: i)

    def at(seg):
        return pl.BlockSpec((TQ, LANES), lambda i, h: (blk(i), C_B // LANES + 4 * h + seg))

    lr_spec = pl.BlockSpec((TQ, LANES), lambda i, h: (blk(i), C_LR // LANES))
    gw_spec = pl.BlockSpec((LANES, LANES), lambda i, h: (0, h))
    gb_spec = pl.BlockSpec((1, LANES), lambda i, h: (0, h))
    gn_spec = pl.BlockSpec((1, LANES), lambda i, h: (0, 0))
    return at(SEG_Q), at(SEG_K), at(SEG_V), at(SEG_Z), lr_spec, gw_spec, gb_spec, gn_spec, blk


def _gla_fwd(h, gw, gb, gn, ycat, ycat_t):
    S = h.shape[0]
    nb = S // TQ
    c0 = A_WIDTH // LANES

    def body(q_ref, k_ref, v_ref, z_ref, lr_ref, gw_ref, gb_ref, gn_ref, yin_ref, ytin_ref,
             y_ref, yt_ref, o_ref, st_ref, st_scr):
        del yin_ref, ytin_ref
        i, hd = pl.program_id(0), pl.program_id(1)

        @pl.when(i == 0)
        def _():
            st_scr[hd] = jnp.zeros((LANES, LANES), F32)

        q = q_ref[...].astype(F32) * (B_DK ** -0.5)
        k = k_ref[...].astype(F32)
        v = v_ref[...]
        _, b = _gla_gates(lr_ref[...], gw_ref[...], gb_ref[...])
        _, _, qp, qn, kp, kn = _gla_factors(q, k, b)
        o_intra = _dot(_gla_intra(qp, qn, kp, kn).astype(BF16), v)
        qpb, knb = qp.astype(BF16), kn.astype(BF16)
        st = st_scr[hd]
        outs = []
        for c in range(CPB):
            rows = slice(c * CHUNK, (c + 1) * CHUNK)
            st_ref[0, c] = st
            outs.append(_dot_nt(qpb[rows], st.astype(BF16)))
            e_last = jnp.exp(b[(c + 1) * CHUNK - 1:(c + 1) * CHUNK, :])
            st = (st + _dot_tn(v[rows], knb[rows])) * e_last
        st_scr[hd] = st
        o = o_intra + jnp.concatenate(outs, axis=0)
        o_ref[...] = o
        r = lax.rsqrt(jnp.mean(o * o, axis=1, keepdims=True) + RMS_EPS)
        z = z_ref[...].astype(F32)
        y = o * r * gn_ref[...] * (z * _sigmoid(z))
        y_ref[...] = y.astype(BF16)
        yt_ref[...] = y.T.astype(BF16)

    q_s, k_s, v_s, z_s, lr_s, gw_s, gb_s, gn_s, _ = _gla_specs(nb, False)
    row = pl.BlockSpec((TQ, LANES), lambda i, h: (i, h))
    return pl.pallas_call(
        body,
        name="gla_fwd",
        grid=(nb, B_HEADS),
        in_specs=[q_s, k_s, v_s, z_s, lr_s, gw_s, gb_s, gn_s, ANY, ANY],
        out_specs=[pl.BlockSpec((TQ, LANES), lambda i, h: (i, c0 + h)), pl.BlockSpec((LANES, TQ), lambda i, h: (c0 + h, i)),
                   row, pl.BlockSpec((1, CPB, LANES, LANES), lambda i, h: (h, i, 0, 0))],
        out_shape=[jax.ShapeDtypeStruct(ycat.shape, BF16), jax.ShapeDtypeStruct(ycat_t.shape, BF16),
                   jax.ShapeDtypeStruct((S, B_WIDTH), F32),
                   jax.ShapeDtypeStruct((B_HEADS, S // CHUNK, LANES, LANES), F32)],
        input_output_aliases={8: 0, 9: 1},
        scratch_shapes=[pltpu.VMEM((B_HEADS, LANES, LANES), F32)],
        compiler_params=_params(("arbitrary", "arbitrary")),
    )(h, h, h, h, h, gw, gb, gn, ycat, ycat_t)


def _gla_bwd(h, gw, gb, gn, o_pre, states, dycat, dh):
    S = h.shape[0]
    nb = S // TQ

    def body(q_ref, k_ref, v_ref, z_ref, lr_ref, gw_ref, gb_ref, gn_ref, o_ref, st_ref, dy_ref, dhin_ref,
             dh_ref, dlr_ref, dgw_ref, dgb_ref, dgn_ref,
             dst_scr, dgw_scr, dgb_scr, dgn_scr):
        del dhin_ref
        i, hd = pl.program_id(0), pl.program_id(1)

        @pl.when(i == 0)
        def _():
            dst_scr[hd] = jnp.zeros((LANES, LANES), F32)
            dgw_scr[hd] = jnp.zeros((LANES, LANES), F32)
            dgb_scr[hd] = jnp.zeros((1, LANES), F32)

        @pl.when(jnp.logical_and(i == 0, hd == 0))
        def _():
            dgn_scr[...] = jnp.zeros_like(dgn_scr)

        q = q_ref[...].astype(F32) * (B_DK ** -0.5)
        k = k_ref[...].astype(F32)
        v = v_ref[...]
        lr, gwv = lr_ref[...], gw_ref[...]
        sg, b = _gla_gates(lr, gwv, gb_ref[...])
        eb, enb, qp, qn, kp, kn = _gla_factors(q, k, b)
        a = _gla_intra(qp, qn, kp, kn)
        qpb, qnb, kpb, knb = qp.astype(BF16), qn.astype(BF16), kp.astype(BF16), kn.astype(BF16)

        o = o_ref[...]
        gn = gn_ref[...]
        r = lax.rsqrt(jnp.mean(o * o, axis=1, keepdims=True) + RMS_EPS)
        z = z_ref[...].astype(F32)
        sz = _sigmoid(z)
        dy = dy_ref[...].astype(F32)
        d_on = dy * (z * sz)
        dh_ref[:, SEG_Z * LANES:(SEG_Z + 1) * LANES] = (dy * (o * r * gn) * (sz * (1.0 + z * (1.0 - sz)))).astype(BF16)
        dgn_scr[...] += jnp.sum(d_on * o * r, axis=0, keepdims=True)
        t = d_on * gn
        do = r * t - o * (r * r * r) * jnp.mean(t * o, axis=1, keepdims=True)
        dob = do.astype(BF16)

        lo, up = _chunk_masks()
        da = _dot_nt(dob, v)
        dalo = jnp.where(lo, da, 0.0).astype(BF16)
        daup = jnp.where(up, da, 0.0).astype(BF16)
        dqp = _dot(dalo, knb)
        dkn = _dot_tn(dalo, qpb)
        dqn = _dot(daup, kpb)
        dkp = _dot_tn(daup, qnb)
        dv = _dot_tn(a.astype(BF16), dob)

        dst = dst_scr[hd]
        dqp_c, dkn_c, dv_c, dbl_c = [None] * CPB, [None] * CPB, [None] * CPB, [None] * CPB
        for c in reversed(range(CPB)):
            rows = slice(c * CHUNK, (c + 1) * CHUNK)
            st = st_ref[0, c]
            e_last = jnp.exp(b[(c + 1) * CHUNK - 1:(c + 1) * CHUNK, :])
            if c == CPB - 1:
                st_next = (st + _dot_tn(v[rows], knb[rows])) * e_last
            else:
                st_next = st_ref[0, c + 1]
            dbl_c[c] = jnp.sum(dst * st_next, axis=0, keepdims=True)
            dtt = (dst * e_last).astype(BF16)
            dv_c[c] = _dot_nt(knb[rows], dtt)
            dkn_c[c] = _dot(v[rows], dtt)
            dqp_c[c] = _dot(dob[rows], st.astype(BF16))
            dst = _dot_tn(dob[rows], qpb[rows]) + dst * e_last
        dst_scr[hd] = dst
        dqp = dqp + jnp.concatenate(dqp_c, axis=0)
        dkn = dkn + jnp.concatenate(dkn_c, axis=0)
        dv = dv + jnp.concatenate(dv_c, axis=0)
        dh_ref[:, SEG_V * LANES:(SEG_V + 1) * LANES] = dv.astype(BF16)
        dh_ref[:, SEG_Q * LANES:(SEG_Q + 1) * LANES] = ((dqp * eb + dqn * enb) * (B_DK ** -0.5)).astype(BF16)
        dh_ref[:, SEG_K * LANES:(SEG_K + 1) * LANES] = (dkp * eb + dkn * enb).astype(BF16)

        last = jnp.bitwise_and(lax.broadcasted_iota(jnp.int32, (TQ, 1), 0), CHUNK - 1) == CHUNK - 1
        dbl = jnp.concatenate([jnp.broadcast_to(x, (CHUNK, LANES)) for x in dbl_c], axis=0)
        db = dqp * qp - dqn * qn + dkp * kp - dkn * kn + jnp.where(last, dbl, 0.0)
        r0 = lax.broadcasted_iota(jnp.int32, (TQ, TQ), 0)
        r1 = lax.broadcasted_iota(jnp.int32, (TQ, TQ), 1)
        upper = jnp.logical_and(_chunk_of(r0) == _chunk_of(r1), r1 >= r0).astype(BF16)
        dlogit = _dot3(upper, db) * (1.0 / GATE_TAU) * (1.0 - sg)
        dlb = dlogit.astype(BF16)
        dlr = _dot_nt(dlb, gwv)

        @pl.when(hd == 0)
        def _():
            dlr_ref[...] = dlr

        @pl.when(hd > 0)
        def _():
            dlr_ref[...] += dlr

        dgw_scr[hd] += _dot_tn(lr, dlb)
        dgb_scr[hd] += jnp.sum(dlogit, axis=0, keepdims=True)

        @pl.when(i == nb - 1)
        def _():
            dgw_ref[...] = dgw_scr[hd]
            dgb_ref[...] = dgb_scr[hd]
            dgn_ref[...] = dgn_scr[...]

    q_s, k_s, v_s, z_s, lr_s, gw_s, gb_s, gn_s, blk = _gla_specs(nb, True)
    row = pl.BlockSpec((TQ, LANES), lambda i, h: (blk(i), h))
    dy_spec = pl.BlockSpec((TQ, LANES), lambda i, h: (blk(i), A_WIDTH // LANES + h))
    st_spec = pl.BlockSpec((1, CPB, LANES, LANES), lambda i, h: (h, blk(i), 0, 0))
    return pl.pallas_call(
        body,
        name="gla_bwd",
        grid=(nb, B_HEADS),
        in_specs=[q_s, k_s, v_s, z_s, lr_s, gw_s, gb_s, gn_s, row, st_spec, dy_spec, ANY],
        out_specs=[pl.BlockSpec((TQ, 4 * LANES), lambda i, h: (blk(i), C_B // (4 * LANES) + h)),
                   pl.BlockSpec((TQ, LANES), lambda i, h: (blk(i), 0)),
                   pl.BlockSpec((LANES, LANES), lambda i, h: (0, jnp.where(i == nb - 1, h, 0))),
                   pl.BlockSpec((1, LANES), lambda i, h: (0, jnp.where(i == nb - 1, h, 0))),
                   pl.BlockSpec((1, LANES), lambda i, h: (0, 0))],
        out_shape=[jax.ShapeDtypeStruct(dh.shape, BF16),
                   jax.ShapeDtypeStruct((S, LANES), F32),
                   jax.ShapeDtypeStruct((LANES, B_HEADS * LANES), F32),
                   jax.ShapeDtypeStruct((1, B_HEADS * LANES), F32),
                   jax.ShapeDtypeStruct((1, LANES), F32)],
        input_output_aliases={11: 0},
        scratch_shapes=[pltpu.VMEM((B_HEADS, LANES, LANES), F32), pltpu.VMEM((B_HEADS, LANES, LANES), F32),
                        pltpu.VMEM((B_HEADS, 1, LANES), F32), pltpu.VMEM((1, LANES), F32)],
        compiler_params=_params(("arbitrary", "arbitrary")),
    )(h, h, h, h, h, gw, gb, gn, o_pre, states, dycat, dh)


def _lr_fill(dlr, dh):
    S = dlr.shape[0]
    w = HP - C_LR

    def body(dlr_ref, dhin_ref, dh_ref):
        del dhin_ref
        dh_ref[:, :LANES] = dlr_ref[...].astype(BF16)
        dh_ref[:, LANES:] = jnp.zeros((TQ, w - LANES), BF16)

    return pl.pallas_call(
        body, name="lr_fill", grid=(S // TQ,),
        in_specs=[pl.BlockSpec((TQ, LANES), lambda i: (i, 0)), ANY],
        out_specs=pl.BlockSpec((TQ, w), lambda i: (i, C_LR // w)),
        out_shape=jax.ShapeDtypeStruct(dh.shape, BF16),
        input_output_aliases={1: 0},
        compiler_params=_params(("parallel",)),
    )(dlr, dh)


LN_ROWS = 256


def _outproj_ln(ycat, w_out, x, g, b):
    S = x.shape[0]

    def body(yc_ref, w_ref, x_ref, g_ref, b_ref, y_ref, yb_ref, yt_ref, xh_ref, rs_ref):
        u = ALPHA * x_ref[...] + _dot(yc_ref[...], w_ref[...])
        mu = jnp.mean(u, axis=1, keepdims=True)
        d = u - mu
        rstd = lax.rsqrt(jnp.mean(d * d, axis=1, keepdims=True) + LN_EPS)
        xh = d * rstd
        y = xh * g_ref[...] + b_ref[...]
        y_ref[...] = y
        yb_ref[...] = y.astype(BF16)
        yt_ref[...] = y.T.astype(BF16)
        xh_ref[...] = xh
        rs_ref[...] = rstd

    row = lambda w: pl.BlockSpec((LN_ROWS, w), lambda i: (i, 0))
    vec = pl.BlockSpec((1, D_MODEL), lambda i: (0, 0))
    return pl.pallas_call(
        body,
        name="outproj_ln",
        grid=(S // LN_ROWS,),
        in_specs=[row(D_MODEL), pl.BlockSpec((D_MODEL, D_MODEL), lambda i: (0, 0)), row(D_MODEL), vec, vec],
        out_specs=[row(D_MODEL), row(D_MODEL), pl.BlockSpec((D_MODEL, LN_ROWS), lambda i: (0, i)), row(D_MODEL), row(1)],
        out_shape=[jax.ShapeDtypeStruct((S, D_MODEL), F32), jax.ShapeDtypeStruct((S, D_MODEL), BF16),
                   jax.ShapeDtypeStruct((D_MODEL, S), BF16),
                   jax.ShapeDtypeStruct((S, D_MODEL), F32), jax.ShapeDtypeStruct((S, 1), F32)],
        compiler_params=_params(("parallel",)),
    )(ycat, w_out, x, g, b)


def _ln_bwd(dy, xhat, rstd, g):
    S = dy.shape[0]

    def body(dy_ref, xh_ref, rs_ref, g_ref, du_ref, dub_ref, dg_ref, db_ref):
        i = pl.program_id(0)
        dy_, xh = dy_ref[...], xh_ref[...]
        dyg = dy_ * g_ref[...]
        m1 = jnp.mean(dyg, axis=1, keepdims=True)
        m2 = jnp.mean(dyg * xh, axis=1, keepdims=True)
        du = rs_ref[...] * (dyg - m1 - xh * m2)
        du_ref[...] = du
        dub_ref[...] = du.astype(BF16)
        dg = jnp.sum(dy_ * xh, axis=0, keepdims=True)
        db = jnp.sum(dy_, axis=0, keepdims=True)

        @pl.when(i == 0)
        def _():
            dg_ref[...] = dg
            db_ref[...] = db

        @pl.when(i > 0)
        def _():
            dg_ref[...] += dg
            db_ref[...] += db

    row = lambda w: pl.BlockSpec((TQ, w), lambda i: (i, 0))
    vec = pl.BlockSpec((1, D_MODEL), lambda i: (0, 0))
    return pl.pallas_call(
        body,
        name="ln_bwd",
        grid=(S // TQ,),
        in_specs=[row(D_MODEL), row(D_MODEL), row(1), vec],
        out_specs=[row(D_MODEL), row(D_MODEL), vec, vec],
        out_shape=[jax.ShapeDtypeStruct((S, D_MODEL), F32), jax.ShapeDtypeStruct((S, D_MODEL), BF16),
                   jax.ShapeDtypeStruct((1, D_MODEL), F32), jax.ShapeDtypeStruct((1, D_MODEL), F32)],
        compiler_params=_params(("arbitrary",)),
    )(dy, xhat, rstd, g)


def _loss_head(y, target):
    S = y.shape[0]

    def body(y_ref, t_ref, l_ref, dy_ref):
        i = pl.program_id(0)
        err = y_ref[...] - t_ref[...]
        dy_ref[...] = err * (1.0 / D_MODEL)
        part = (0.5 / D_MODEL) * jnp.sum(jnp.sum(err * err, axis=1, keepdims=True), axis=0, keepdims=True)

        @pl.when(i == 0)
        def _():
            l_ref[...] = part

        @pl.when(i > 0)
        def _():
            l_ref[...] += part

    row = pl.BlockSpec((TQ, D_MODEL), lambda i: (i, 0))
    return pl.pallas_call(
        body,
        name="loss_head",
        grid=(S // TQ,),
        in_specs=[row, row],
        out_specs=[pl.BlockSpec((1, 1), lambda i: (0, 0)), row],
        out_shape=[jax.ShapeDtypeStruct((1, 1), F32), jax.ShapeDtypeStruct((S, D_MODEL), F32)],
        compiler_params=_params(("arbitrary",)),
    )(y, target)


def _pad_gate(gate_w, gate_b):
    gw = gate_w.reshape(GATE_RANK, B_HEADS, B_DK)
    gw = jnp.pad(gw, ((0, LANES - GATE_RANK), (0, 0), (0, LANES - B_DK))).reshape(LANES, B_HEADS * LANES)
    gb = jnp.pad(gate_b.reshape(B_HEADS, B_DK), ((0, 0), (0, LANES - B_DK))).reshape(1, B_HEADS * LANES)
    return gw.astype(BF16), gb.astype(F32)


def _layer_fwd(x, xb, xt, mem_b, w_in, w_kv, w_out, u, gw, gb, gn, ln_g, ln_b, rider=None):
    h = _matmul(xb, w_in, mode="nn", out_dtype=BF16, tm=1024, tn=768, tk=D_MODEL, name="in_proj", rider=rider)
    if rider is not None:
        h, rode = h
    mkv = _matmul(mem_b, w_kv, mode="nn", out_dtype=BF16, tm=mem_b.shape[0], tn=1024, tk=D_MODEL, name="mem_kv")
    ycat, ycat_t = _band_fwd(h, u)
    ycat, ycat_t, o_pre, states = _gla_fwd(h, gw, gb, gn, ycat, ycat_t)
    ycat, ycat_t = _mem_fwd(h, mkv, ycat, ycat_t)
    y, ybf, yt, xhat, rstd = _outproj_ln(ycat, w_out, x, ln_g, ln_b)
    saved = (xt, h, mkv, ycat_t, o_pre, states, xhat, rstd)
    return (y, ybf, yt, saved) if rider is None else (y, ybf, yt, saved, rode)


def _layer_bwd(dy, saved, mem_b, w_in, w_out, u, gw, gb, gn, ln_g):
    xt, h, mkv, ycat_t, o_pre, states, xhat, rstd = saved
    S = dy.shape[0]
    du, dub, d_ln_g, d_ln_b = _ln_bwd(dy, xhat, rstd, ln_g)
    dycat = _matmul(dub, w_out, mode="nt", out_dtype=BF16, tm=1024, tn=1024, tk=D_MODEL, name="dycat")
    d_w_out = _matmul(ycat_t, dub, mode="nn", out_dtype=F32, tm=1024, tn=1024, tk=2048, name="d_w_out")
    dh, d_u = _band_bwd(h, u, dycat)
    dh, dlr, dgw, dgb, dgn = _gla_bwd(h, gw, gb, gn, o_pre, states, dycat, dh)
    dh, dmkv = _mem_bwd(h, mkv, dycat, dh)
    dh = _lr_fill(dlr, dh)
    d_w_kv = _matmul(mem_b, dmkv, mode="tn", out_dtype=F32, tm=1024, tn=1024, tk=mem_b.shape[0], name="d_w_kv")
    dx = _matmul(dh, w_in, mode="nt", out_dtype=F32, tm=1024, tn=1024, tk=2560, name="dx", add=du, add_scale=ALPHA)
    d_w_in = _matmul(xt, dh, mode="nn", out_dtype=F32, tm=1024, tn=768, tk=2048, name="d_w_in")
    return dx, (d_w_in, d_u, dgw, dgb, dgn, d_w_kv, d_w_out, d_ln_g, d_ln_b)


def _unpad_heads(w):
    r = w.shape[0]
    return w.reshape(r, B_HEADS, LANES)[:, :, :B_DK].reshape(r, B_KEY_WIDTH)


def _padded_col_of():
    col, o = np.zeros(IN_WIDTH, np.int64), 0
    for seg in (SEG_Q, SEG_K, SEG_V, SEG_Z):
        for hd in range(A_HEADS):
            col[o:o + LANES] = C_A + (hd // A_HPS) * 4 * A_HW + seg * A_HW + (hd % A_HPS) * LANES + np.arange(LANES)
            o += LANES
    for seg, width in ((SEG_Q, B_DK), (SEG_K, B_DK), (SEG_V, LANES), (SEG_Z, LANES)):
        for hd in range(B_HEADS):
            col[o:o + width] = C_B + hd * 4 * LANES + seg * LANES + np.arange(width)
            o += width
    col[o:o + GATE_RANK] = C_LR + np.arange(GATE_RANK)
    o += GATE_RANK
    for seg in (0, 1):
        for hd in range(M_HEADS):
            col[o:o + LANES] = C_M + hd * 2 * LANES + seg * LANES + np.arange(LANES)
            o += LANES
    assert o == IN_WIDTH
    return col


def _runs(idx):
    out, start = [], 0
    for k in range(1, len(idx) + 1):
        if k == len(idx) or idx[k] != idx[k - 1] + 1:
            out.append((int(idx[start]), k - start))
            start = k
    return out


def _chip_columns(g, j, n):
    runs = _runs(_padded_col_of()[j * n:(j + 1) * n])
    return jnp.concatenate([g[:, a:a + ln] for a, ln in runs], axis=1)


def _padded_from_shards(shards):
    n = shards[0].shape[1]
    src = np.full(HP, -1, np.int64)
    src[_padded_col_of()] = np.arange(IN_WIDTH)
    parts, k = [], 0
    while k < HP:
        e = k + 1
        if src[k] < 0:
            while e < HP and src[e] < 0:
                e += 1
            parts.append(jnp.zeros((shards[0].shape[0], e - k), shards[0].dtype))
        else:
            while e < HP and src[e] == src[e - 1] + 1 and src[e] // n == src[k] // n:
                e += 1
            parts.append(shards[src[k] // n][:, src[k] % n:src[k] % n + e - k])
        k = e
    return jnp.concatenate(parts, axis=1)


ADAMW_BLOCK_BYTES = 1 << 20


def _adamw(w, g, m, v, name):
    L, R, C = w.shape
    tl, tr = 1, R
    if R * C * 4 <= ADAMW_BLOCK_BYTES:
        tl = max(d for d in range(1, L + 1) if L % d == 0 and d * R * C * 4 <= ADAMW_BLOCK_BYTES)
    else:
        for cand in (256, 128, 64, 32, 16, 8):
            if R % cand == 0 and R > cand:
                tr = cand
                break

    def body(w_ref, g_ref, m_ref, v_ref, d_ref, nm_ref, nv_ref):
        g_ = g_ref[...]
        nm = ADAM_B1 * m_ref[...] + (1.0 - ADAM_B1) * g_
        nv = ADAM_B2 * v_ref[...] + (1.0 - ADAM_B2) * (g_ * g_)
        m_hat = nm / (1.0 - ADAM_B1 ** ADAM_STEP)
        v_hat = nv / (1.0 - ADAM_B2 ** ADAM_STEP)
        d_ref[...] = -ADAM_LR * (m_hat / (jnp.sqrt(v_hat) + ADAM_EPS) + ADAM_WD * w_ref[...])
        nm_ref[...] = nm
        nv_ref[...] = nv

    spec = pl.BlockSpec((tl, tr, C), lambda l, i: (l, i, 0))
    sd = jax.ShapeDtypeStruct((L, R, C), F32)
    return pl.pallas_call(
        body, name=name, grid=(L // tl, R // tr), in_specs=[spec] * 4, out_specs=[spec] * 3, out_shape=[sd] * 3,
        compiler_params=_params(("parallel", "parallel")),
    )(w, g, m, v)


def _adamw_nd(w, g, m, v, name):
    shape = w.shape
    f = (lambda a: a) if w.ndim == 3 else (lambda a: a.reshape(1, shape[0], shape[1]))
    return tuple(o.reshape(shape) for o in _adamw(f(w), f(g), f(m), f(v), name))


ANY = pl.BlockSpec(memory_space=pl.ANY)


def _place():
    x, y, c = lax.axis_index("x"), lax.axis_index("y"), lax.axis_index("c")
    chips = [(1 - x, y), (x, 1 - y), (1 - x, 1 - y)]
    return x, y, c, chips


class _WeightGather:
    def __init__(self, hop, layer, rows):
        self.hop, self.layer, self.rows = hop, layer, rows
        self.n_sem = 3 * len(rows)

    def _copies(self, shard_refs, buf_refs, send, recv, received):
        x, y, c, chips = _place()
        out = []
        for t, R in enumerate(self.rows):
            half = R // 2
            assert half % 16 == 0
            mine = pl.ds(pl.multiple_of(c * half, 16), half)
            other = pl.ds(pl.multiple_of((1 - c) * half, 16), half)
            mine_of_shard = pl.ds(pl.multiple_of(self.layer * R + c * half, 16), half)
            for k, chip in enumerate(chips):
                theirs = buf_refs[t].at[2 * chip[0] + chip[1]]
                if self.hop == "chips":
                    src, dst, to = shard_refs[t].at[mine_of_shard], buf_refs[t].at[2 * x + y, mine], (*chip, c)
                    landed = theirs.at[mine]
                else:
                    src, dst, to = theirs.at[mine], theirs.at[mine], (x, y, 1 - c)
                    landed = theirs.at[other]
                out.append(pltpu.make_async_remote_copy(
                    src_ref=src, dst_ref=landed if received else dst, send_sem=send.at[3 * t + k],
                    recv_sem=recv.at[3 * t + k], device_id=to, device_id_type=MESH))
        return out

    def start(self, shard_refs, buf_refs, send, recv):
        for cp in self._copies(shard_refs, buf_refs, send, recv, False):
            cp.start()

    def wait(self, shard_refs, buf_refs, send, recv):
        for cp in self._copies(shard_refs, buf_refs, send, recv, True):
            cp.wait_recv()
        for cp in self._copies(shard_refs, buf_refs, send, recv, False):
            cp.wait_send()

    def sems(self):
        return [pltpu.SemaphoreType.DMA((self.n_sem,)), pltpu.SemaphoreType.DMA((self.n_sem,))]

    def call(self, shards, bufs, name):
        n = len(shards)

        def body(*refs):
            shard_refs, buf_refs, (send, recv) = refs[:n], refs[2 * n:3 * n], refs[3 * n:]
            self.start(shard_refs, buf_refs, send, recv)
            self.wait(shard_refs, buf_refs, send, recv)

        return pl.pallas_call(
            body, name=name, in_specs=[ANY] * (2 * n), out_specs=[ANY] * n,
            out_shape=[jax.ShapeDtypeStruct(b.shape, b.dtype) for b in bufs],
            input_output_aliases={n + t: t for t in range(n)},
            scratch_shapes=self.sems(),
        )(*shards, *bufs)


def _pair_exchange(parts, half, L, name):
    n = len(parts) // L
    C = parts[0][0].shape[1]

    def body(*refs):
        ins, (o_ref, send, recv) = refs[:len(parts)], refs[len(parts):]
        x, y, c, _ = _place()
        cps = []
        for k, (_, first) in enumerate(parts):
            rows = pl.ds(pl.multiple_of(first + (1 - c) * half, 8), half)
            cps.append(pltpu.make_async_remote_copy(
                src_ref=ins[k].at[rows], dst_ref=o_ref.at[k // L, k % L], send_sem=send.at[k], recv_sem=recv.at[k],
                device_id=(x, y, 1 - c), device_id_type=MESH))
        for cp in cps:
            cp.start()
        for cp in cps:
            cp.wait()

    return pl.pallas_call(
        body, name=name, in_specs=[ANY] * len(parts), out_specs=ANY,
        out_shape=jax.ShapeDtypeStruct((n, L, half, C), F32),
        scratch_shapes=[pltpu.SemaphoreType.DMA((len(parts),)), pltpu.SemaphoreType.DMA((len(parts),))],
    )(*[a for a, _ in parts])


def _chip_exchange(p, name):
    n, R, C = p.shape

    def body(p_ref, o_ref, send, recv):
        x, y, c, chips = _place()
        me = 2 * x + y
        cps = [pltpu.make_async_remote_copy(src_ref=p_ref.at[2 * chip[0] + chip[1]], dst_ref=o_ref.at[me],
                                            send_sem=send.at[k], recv_sem=recv.at[k], device_id=(*chip, c),
                                            device_id_type=MESH) for k, chip in enumerate(chips)]
        for cp in cps:
            cp.start()
        for k, chip in enumerate(chips):
            pltpu.make_async_remote_copy(src_ref=p_ref.at[me], dst_ref=o_ref.at[2 * chip[0] + chip[1]],
                                         send_sem=send.at[k], recv_sem=recv.at[k], device_id=(*chip, c),
                                         device_id_type=MESH).wait_recv()
        for cp in cps:
            cp.wait_send()

    return pl.pallas_call(
        body, name=name, in_specs=[ANY], out_specs=ANY, out_shape=jax.ShapeDtypeStruct((n, R, C), p.dtype),
        scratch_shapes=[pltpu.SemaphoreType.DMA((3,)), pltpu.SemaphoreType.DMA((3,))],
    )(p)


def _pair_gather(t2, name):
    def body(t_ref, o_ref, send, recv):
        del t_ref
        x, y, c, _ = _place()
        cp = pltpu.make_async_remote_copy(src_ref=o_ref.at[c], dst_ref=o_ref.at[c], send_sem=send, recv_sem=recv,
                                          device_id=(x, y, 1 - c), device_id_type=MESH)
        cp.start()
        pltpu.make_async_remote_copy(src_ref=o_ref.at[c], dst_ref=o_ref.at[1 - c], send_sem=send, recv_sem=recv,
                                     device_id=(x, y, 1 - c), device_id_type=MESH).wait_recv()
        cp.wait_send()

    return pl.pallas_call(
        body, name=name, in_specs=[ANY], out_specs=ANY, out_shape=jax.ShapeDtypeStruct(t2.shape, t2.dtype),
        input_output_aliases={0: 0},
        scratch_shapes=[pltpu.SemaphoreType.DMA, pltpu.SemaphoreType.DMA],
    )(t2)


def _add_halves(parts, got, c_idx, name):
    n, L, half, C = got.shape
    tr = 64

    def body(*refs):
        ins, (got_ref, o_ref) = refs[1:1 + len(parts)], refs[1 + len(parts):]
        for k in range(len(parts)):
            o_ref[k // L, k % L] = (ins[k][...] + got_ref[k // L, k % L]).astype(BF16)

    def rows_of(first):
        assert first % tr == 0 and half % tr == 0
        return lambda i, c: (first // tr + c[0] * (half // tr) + i, 0)

    whole = pl.BlockSpec((n, L, tr, C), lambda i, c: (0, 0, i, 0))
    return pl.pallas_call(
        body, name=name,
        grid_spec=pltpu.PrefetchScalarGridSpec(
            num_scalar_prefetch=1, grid=(half // tr,),
            in_specs=[pl.BlockSpec((tr, C), rows_of(first)) for _, first in parts] + [whole],
            out_specs=whole),
        out_shape=jax.ShapeDtypeStruct((n, L, half, C), BF16),
        compiler_params=_params(("parallel",)),
    )(c_idx, *[a for a, _ in parts], got)


def _add_slots(r, c_idx, name):
    n, R, C = r.shape
    tr = 256

    def body(c_ref, r_ref, o_ref):
        acc = r_ref[0].astype(F32)
        for j in range(1, n):
            acc = acc + r_ref[j].astype(F32)
        o_ref[0] = acc

    return pl.pallas_call(
        body, name=name,
        grid_spec=pltpu.PrefetchScalarGridSpec(
            num_scalar_prefetch=1, grid=(R // tr,),
            in_specs=[pl.BlockSpec((n, tr, C), lambda i, c: (0, i, 0))],
            out_specs=pl.BlockSpec((1, tr, C), lambda i, c: (c[0], i, 0))),
        out_shape=jax.ShapeDtypeStruct((2, R, C), F32),
        compiler_params=_params(("parallel",)),
    )(c_idx, r)


def _reduce_scatter(parts, half, L, c_idx, chip, tag):
    C = parts[0][0].shape[1]
    got = _pair_exchange(parts, half, L, "rs_pair_" + tag)
    p = _add_halves(parts, got, c_idx, "rs_add2_" + tag).reshape(N_CHIPS, L * half, C)
    q = _chip_exchange(p, "rs_chip_" + tag)
    q = lax.dynamic_update_slice_in_dim(q, lax.dynamic_slice_in_dim(p, chip, 1, axis=0), chip, axis=0)
    t2 = _add_slots(q, c_idx, "rs_add4_" + tag)
    both = _pair_gather(t2, "rs_gather_" + tag).reshape(2, L, half, C)
    return both.transpose(1, 0, 2, 3).reshape(L, 2 * half, C)


def _all_reduce_small(buf, name):
    R = buf.shape[0]

    def flipped(k, x, y, c):
        return ((1 - x) if k & 4 else x, (1 - y) if k & 2 else y, (1 - c) if k & 1 else c)

    def body(b_ref, o_ref, land, send, recv):
        x, y, c, _ = _place()
        me = 4 * x + 2 * y + c
        land[me] = b_ref[...]
        cps = []
        for k in range(1, N_DEV):
            peer = flipped(k, x, y, c)
            cps.append(pltpu.make_async_remote_copy(src_ref=b_ref, dst_ref=land.at[me], send_sem=send.at[k - 1],
                                                    recv_sem=recv.at[k - 1], device_id=peer, device_id_type=MESH))
        for cp in cps:
            cp.start()
        for k in range(1, N_DEV):
            peer = flipped(k, x, y, c)
            slot = 4 * peer[0] + 2 * peer[1] + peer[2]
            pltpu.make_async_remote_copy(src_ref=b_ref, dst_ref=land.at[slot], send_sem=send.at[k - 1],
                                         recv_sem=recv.at[k - 1], device_id=peer, device_id_type=MESH).wait_recv()
        for cp in cps:
            cp.wait_send()
        acc = land[0]
        for j in range(1, N_DEV):
            acc = acc + land[j]
        o_ref[...] = acc

    vm = pl.BlockSpec(memory_space=pltpu.VMEM)
    return pl.pallas_call(
        body, name=name, in_specs=[vm], out_specs=vm,
        out_shape=jax.ShapeDtypeStruct((R, LANES), F32),
        scratch_shapes=[pltpu.VMEM((N_DEV, R, LANES), F32), pltpu.SemaphoreType.DMA((N_DEV - 1,)),
                        pltpu.SemaphoreType.DMA((N_DEV - 1,))],
    )(buf)


def kernel(x, mem, w_in, a_rel_bias, b_gate_w, b_gate_b, b_norm_g, w_mem_kv, w_out, ln_g, ln_b, loss_target, m_w_in, m_a_rel_bias, m_b_gate_w, m_b_gate_b, m_b_norm_g, m_w_mem_kv, m_w_out, m_ln_g, m_ln_b, v_w_in, v_a_rel_bias, v_b_gate_w, v_b_gate_b, v_b_norm_g, v_w_mem_kv, v_w_out, v_ln_g, v_ln_b):
    L = w_in.shape[0]
    S = x.shape[1]
    cx, cy, cc = lax.axis_index("x"), lax.axis_index("y"), lax.axis_index("c")
    chip = 2 * cx + cy
    c_idx = jnp.reshape(cc, (1,)).astype(jnp.int32)

    n_in, r_kv, r_out = w_in.shape[2], w_mem_kv.shape[1], w_out.shape[1]
    shards = [w.astype(BF16).reshape(-1, w.shape[2]) for w in (w_in, w_mem_kv, w_out)]
    rows = [D_MODEL, r_kv, r_out]

    def landing(l):
        return [lax.dynamic_update_slice_in_dim(lax.empty((N_CHIPS, r, s.shape[1]), BF16),
                                                s[l * r:(l + 1) * r][None], chip, axis=0)
                for s, r in zip(shards, rows)]

    def assembled(bufs):
        return (_padded_from_shards([bufs[0][j] for j in range(N_CHIPS)]),
                bufs[1].reshape(D_MODEL, bufs[1].shape[2]), bufs[2].reshape(D_MODEL, D_MODEL))

    bufs0 = _WeightGather("chips", 0, rows).call(shards, landing(0), "gather_chips_0")
    weights = [assembled(_WeightGather("pair", 0, rows).call(shards, bufs0, "gather_pair_0"))]

    gw_cols = b_gate_w.shape[2]
    gw_slot = jnp.zeros((N_CHIPS, L, GATE_RANK, gw_cols), F32)
    gw_slot = lax.dynamic_update_slice(gw_slot, (0.5 * b_gate_w)[None], (chip, 0, 0, 0))
    gw_flat = gw_slot.reshape(-1)
    n_gw = gw_flat.shape[0]
    pad = (-n_gw) % (8 * LANES)
    gw_full = _all_reduce_small(jnp.pad(gw_flat, (0, pad)).reshape(-1, LANES), "gather_gate_w").reshape(-1)[:n_gw]
    gw_full = gw_full.reshape(N_CHIPS, L, GATE_RANK, gw_cols).transpose(1, 2, 0, 3).reshape(L, GATE_RANK, B_KEY_WIDTH)

    xs = x.reshape(S, D_MODEL)
    mem_b = mem.reshape(mem.shape[1], D_MODEL).astype(BF16)
    target = loss_target.reshape(S, D_MODEL)

    small_w = []
    for l in range(L):
        gw_l, gb_l = _pad_gate(gw_full[l], b_gate_b[l])
        small_w.append((_bias_by_offset(a_rel_bias[l]), gw_l, gb_l,
                        b_norm_g[l].reshape(1, LANES), ln_g[l].reshape(1, D_MODEL), ln_b[l].reshape(1, D_MODEL)))

    y, yb = xs, xs.astype(BF16)
    yt = _transpose(yb, "x_t")
    saved = []
    for l in range(L):
        if l + 1 < L:
            rider = (_WeightGather("chips", l + 1, rows), shards, landing(l + 1))
            y, yb, yt, sv, bufs = _layer_fwd(y, yb, yt, mem_b, *weights[l], *small_w[l], rider=rider)
            weights.append(assembled(_WeightGather("pair", l + 1, rows).call(shards, bufs, f"gather_pair_{l + 1}")))
        else:
            y, yb, yt, sv = _layer_fwd(y, yb, yt, mem_b, *weights[l], *small_w[l])
        saved.append(sv)
    layer_w = [(*weights[l], *small_w[l]) for l in range(L)]
    loss_part, dy = _loss_head(y, target)

    grads = [None] * L
    for l in reversed(range(L)):
        w_in_l, w_kv_l, w_out_l, u_l, gw_l, gb_l, gn_l, lg_l, lb_l = layer_w[l]
        dy, grads[l] = _layer_bwd(dy, saved[l], mem_b, w_in_l, w_out_l, u_l, gw_l, gb_l, gn_l, lg_l)
    grad_x = dy.reshape(x.shape)

    g_rel = jnp.stack([_bias_grad_from_offset(g[1]) for g in grads])
    g_gw = jnp.stack([_unpad_heads(g[2][:GATE_RANK]) for g in grads])
    g_gb = jnp.stack([_unpad_heads(g[3])[0] for g in grads])
    g_gn = jnp.stack([g[4][0] for g in grads])
    g_lg = jnp.stack([g[7][0] for g in grads])
    g_lb = jnp.stack([g[8][0] for g in grads])

    in_parts = [(_chip_columns(grads[l][0], j, n_in), 0) for j in range(N_CHIPS) for l in range(L)]
    r_w_in = _reduce_scatter(in_parts, D_MODEL // 2, L, c_idx, chip, "w_in")
    kv_parts = [(grads[l][5], j * r_kv) for j in range(N_CHIPS) for l in range(L)]
    r_w_kv = _reduce_scatter(kv_parts, r_kv // 2, L, c_idx, chip, "w_kv")
    out_parts = [(grads[l][6], j * r_out) for j in range(N_CHIPS) for l in range(L)]
    r_w_out = _reduce_scatter(out_parts, r_out // 2, L, c_idx, chip, "w_out")

    small = [g_rel, g_gw, g_gb, g_gn, g_lg, g_lb, loss_part]
    flat = jnp.concatenate([s.reshape(-1) for s in small])
    n_small = flat.shape[0]
    pad = (-n_small) % (8 * LANES)
    red = _all_reduce_small(jnp.pad(flat, (0, pad)).reshape(-1, LANES), "all_reduce_small").reshape(-1)
    outs, off = [], 0
    for s in small:
        outs.append(red[off:off + s.size].reshape(s.shape))
        off += s.size
    g_rel, g_gw, g_gb, g_gn, g_lg, g_lb, loss = outs
    loss = loss.reshape(())
    g_gw = lax.dynamic_slice_in_dim(g_gw.reshape(L, GATE_RANK, N_CHIPS, gw_cols), chip, 1, axis=2).reshape(L, GATE_RANK, gw_cols)

    g_list = [r_w_in, g_rel, g_gw, g_gb, g_gn, r_w_kv, r_w_out, g_lg, g_lb]
    w_list = [w_in, a_rel_bias, b_gate_w, b_gate_b, b_norm_g, w_mem_kv, w_out, ln_g, ln_b]
    m_list = [m_w_in, m_a_rel_bias, m_b_gate_w, m_b_gate_b, m_b_norm_g, m_w_mem_kv, m_w_out, m_ln_g, m_ln_b]
    v_list = [v_w_in, v_a_rel_bias, v_b_gate_w, v_b_gate_b, v_b_norm_g, v_w_mem_kv, v_w_out, v_ln_g, v_ln_b]
    names = ["w_in", "rel", "gate_w", "gate_b", "norm_g", "w_kv", "w_out", "ln_g", "ln_b"]
    to_cols = lambda a: jnp.transpose(a, (2, 0, 1))
    upd = [tuple(jnp.transpose(o, (1, 2, 0)) for o in
                 _adamw(to_cols(w_in), to_cols(r_w_in), to_cols(m_w_in), to_cols(v_w_in), "adamw_w_in"))]
    upd += [_adamw_nd(w, g, m, v, "adamw_" + n)
            for w, g, m, v, n in list(zip(w_list, g_list, m_list, v_list, names))[1:]]
    deltas = [u_[0] for u_ in upd]
    new_m = [u_[1] for u_ in upd]
    new_v = [u_[2] for u_ in upd]
    return (loss, grad_x, *g_list, *deltas, *new_m, *new_v)
```

```python
import functools

import numpy as np
import jax
import jax.numpy as jnp
from jax import lax
from jax.experimental import pallas as pl
from jax.experimental.pallas import tpu as pltpu

F32 = jnp.float32
BF16 = jnp.bfloat16
MESH = pl.DeviceIdType.MESH

D_MODEL = 2048
DEPTH = 4
CHUNK = 64
LEFT_CHUNKS = 8
MAX_REL = 128
N_REL = 2 * MAX_REL + 1
A_HEADS = 8
HEAD_DIM = 128
B_HEADS = 4
B_DK = 64
M_HEADS = 4
GATE_RANK = 16
GATE_TAU = 16.0
A_WIDTH = A_HEADS * HEAD_DIM
B_WIDTH = B_HEADS * HEAD_DIM
B_KEY_WIDTH = B_HEADS * B_DK
M_WIDTH = M_HEADS * HEAD_DIM
IN_WIDTH = 4 * A_WIDTH + 2 * B_KEY_WIDTH + 2 * B_WIDTH + GATE_RANK + 2 * M_WIDTH
ALPHA = (2.0 * DEPTH) ** 0.25
LN_EPS = 1e-5
RMS_EPS = 1e-6
NEG_INF = -1e30
ADAM_LR = 0.001
ADAM_B1 = 0.9
ADAM_B2 = 0.999
ADAM_EPS = 1e-08
ADAM_WD = 0.01
ADAM_STEP = 10

LANES = 128
VMEM_LIMIT = 56 * 1024 * 1024

C_A, C_B, C_M, C_LR = 0, 4096, 6144, 7168
HP = 7680
SEG_Q, SEG_K, SEG_V, SEG_Z = 0, 1, 2, 3
TQ = 512
CPB = TQ // CHUNK
N_CHIPS = 4
N_DEV = 8


def _params(sem, vmem=VMEM_LIMIT):
    return pltpu.CompilerParams(dimension_semantics=sem, vmem_limit_bytes=vmem)


def _dot(a, b):
    return jnp.dot(a, b, preferred_element_type=F32)


def _dot_nt(a, b):
    return lax.dot_general(a, b, (((1,), (1,)), ((), ())), preferred_element_type=F32)


def _dot_tn(a, b):
    return lax.dot_general(a, b, (((0,), (0,)), ((), ())), preferred_element_type=F32)


def _sigmoid(x):
    return 1.0 / (1.0 + jnp.exp(-x))


def _split3(x):
    hi = x.astype(BF16)
    r = x - hi.astype(F32)
    mid = r.astype(BF16)
    lo = (r - mid.astype(F32)).astype(BF16)
    return hi, mid, lo


def _dot3(m_bf, x):
    hi, mid, lo = _split3(x)
    return _dot(m_bf, hi) + _dot(m_bf, mid) + _dot(m_bf, lo)


def _matmul(a, b, *, mode, out_dtype, tm, tn, tk, name, add=None, add_scale=1.0, rider=None):
    if mode == "nn":
        (M, K), (K2, N) = a.shape, b.shape
        a_spec = pl.BlockSpec((tm, tk), lambda i, j, k: (i, k))
        b_spec = pl.BlockSpec((tk, tn), lambda i, j, k: (k, j))
        dot = _dot
    elif mode == "nt":
        (M, K), (N, K2) = a.shape, b.shape
        a_spec = pl.BlockSpec((tm, tk), lambda i, j, k: (i, k))
        b_spec = pl.BlockSpec((tn, tk), lambda i, j, k: (j, k))
        dot = _dot_nt
    else:
        (K, M), (K2, N) = a.shape, b.shape
        a_spec = pl.BlockSpec((tk, tm), lambda i, j, k: (k, i))
        b_spec = pl.BlockSpec((tk, tn), lambda i, j, k: (k, j))
        dot = _dot_tn
    assert K == K2 and M % tm == 0 and N % tn == 0 and K % tk == 0, (a.shape, b.shape, mode)
    nk = K // tk
    has_add = add is not None
    assert nk == 1 or out_dtype == F32
    grid = (M // tm, N // tn, nk)
    hop, srcs, bufs = rider if rider is not None else (None, [], [])
    n_in = 2 + has_add

    def body(*refs):
        a_ref, b_ref = refs[:2]
        add_ref = refs[2] if has_add else None
        src_refs = refs[n_in:n_in + len(srcs)]
        o_ref = refs[n_in + len(srcs) + len(bufs)]
        buf_refs = refs[n_in + len(srcs) + len(bufs) + 1:n_in + len(srcs) + 2 * len(bufs) + 1]
        sems = refs[n_in + len(srcs) + 2 * len(bufs) + 1:]
        i, j, k = pl.program_id(0), pl.program_id(1), pl.program_id(2)
        if hop is not None:
            @pl.when(jnp.logical_and(jnp.logical_and(i == 0, j == 0), k == 0))
            def _():
                hop.start(src_refs, buf_refs, *sems)

        part = dot(a_ref[...].astype(BF16), b_ref[...].astype(BF16))

        @pl.when(k == 0)
        def _():
            first = part + add_scale * add_ref[...] if has_add else part
            o_ref[...] = first.astype(out_dtype)

        if nk > 1:
            @pl.when(k > 0)
            def _():
                o_ref[...] += part

        if hop is not None:
            @pl.when(jnp.logical_and(jnp.logical_and(i == grid[0] - 1, j == grid[1] - 1), k == nk - 1))
            def _():
                hop.wait(src_refs, buf_refs, *sems)

    in_specs = [a_spec, b_spec]
    args = [a, b]
    if has_add:
        in_specs.append(pl.BlockSpec((tm, tn), lambda i, j, k: (i, j)))
        args.append(add)
    out = pl.pallas_call(
        body,
        name=name,
        grid=grid,
        in_specs=in_specs + [ANY] * (len(srcs) + len(bufs)),
        out_specs=[pl.BlockSpec((tm, tn), lambda i, j, k: (i, j))] + [ANY] * len(bufs),
        out_shape=[jax.ShapeDtypeStruct((M, N), out_dtype)] + [jax.ShapeDtypeStruct(x.shape, x.dtype) for x in bufs],
        input_output_aliases={n_in + len(srcs) + t: 1 + t for t in range(len(bufs))},
        scratch_shapes=hop.sems() if hop is not None else [],
        compiler_params=_params(("parallel", "parallel", "arbitrary") if hop is None
                                else ("arbitrary", "arbitrary", "arbitrary")),
    )(*args, *srcs, *bufs)
    return out[0] if hop is None else (out[0], list(out[1:]))


def _transpose(a, name):
    R, C = a.shape
    t = 512

    def body(a_ref, o_ref):
        o_ref[...] = a_ref[...].T

    return pl.pallas_call(
        body, name=name, grid=(R // t, C // t),
        in_specs=[pl.BlockSpec((t, t), lambda i, j: (i, j))],
        out_specs=pl.BlockSpec((t, t), lambda i, j: (j, i)),
        out_shape=jax.ShapeDtypeStruct((C, R), a.dtype),
        compiler_params=_params(("parallel", "parallel")),
    )(a)


def _chunk_of(rows):
    return lax.shift_right_logical(rows, CHUNK.bit_length() - 1)


A_HPS = 2
A_HW = A_HPS * LANES


def _band_bias(u_row, first):
    bias = pltpu.roll(jnp.broadcast_to(u_row, (TQ, 2 * TQ)), 0, 1, stride=1, stride_axis=0)
    qc = _chunk_of(lax.broadcasted_iota(jnp.int32, (TQ, 2 * TQ), 0))
    col = lax.broadcasted_iota(jnp.int32, (TQ, 2 * TQ), 1)
    kc = _chunk_of(jnp.bitwise_and(col, TQ - 1))
    ok = jnp.logical_or(jnp.logical_and(col < TQ, kc >= qc), jnp.logical_and(col >= TQ, kc <= qc))
    return jnp.where(ok, bias, NEG_INF) + jnp.where(col < TQ, first * NEG_INF, 0.0)


def _band_probs(q, kp, kc, bias):
    scale = HEAD_DIM ** -0.5
    sp = _dot_nt(q, kp) * scale + bias[:, :TQ]
    sc = _dot_nt(q, kc) * scale + bias[:, TQ:]
    m = jnp.maximum(jnp.max(sp, axis=1, keepdims=True), jnp.max(sc, axis=1, keepdims=True))
    pp = jnp.exp(sp - m)
    pc = jnp.exp(sc - m)
    inv = 1.0 / (jnp.sum(pp, axis=1, keepdims=True) + jnp.sum(pc, axis=1, keepdims=True))
    return pp, pc, inv


def _band_specs(nq):
    def col(seg, h):
        return C_A // A_HW + 4 * h + seg

    q_spec = pl.BlockSpec((TQ, A_HW), lambda h, i: (jnp.minimum(i, nq - 1), col(SEG_Q, h)))
    kp_spec = pl.BlockSpec((TQ, A_HW), lambda h, i: (jnp.clip(i - 1, 0, nq - 1), col(SEG_K, h)))
    kc_spec = pl.BlockSpec((TQ, A_HW), lambda h, i: (jnp.minimum(i, nq - 1), col(SEG_K, h)))
    vp_spec = pl.BlockSpec((TQ, A_HW), lambda h, i: (jnp.clip(i - 1, 0, nq - 1), col(SEG_V, h)))
    vc_spec = pl.BlockSpec((TQ, A_HW), lambda h, i: (jnp.minimum(i, nq - 1), col(SEG_V, h)))
    z_spec = pl.BlockSpec((TQ, A_HW), lambda h, i: (jnp.minimum(i, nq - 1), col(SEG_Z, h)))
    u_spec = pl.BlockSpec((A_HPS, 1, 2 * TQ), lambda h, i: (h, 0, 0))
    return q_spec, kp_spec, kc_spec, vp_spec, vc_spec, z_spec, u_spec


def _band_fwd(h, u):
    S = h.shape[0]
    nq = S // TQ

    def body(q_ref, kp_ref, kc_ref, vp_ref, vc_ref, z_ref, u_ref, y_ref, yt_ref, bias_scr):
        i = pl.program_id(1)

        @pl.when(i <= 1)
        def _():
            for hh in range(A_HPS):
                bias_scr[hh] = _band_bias(u_ref[hh], (i == 0).astype(F32))

        for hh in range(A_HPS):
            cs = slice(hh * LANES, (hh + 1) * LANES)
            pp, pc, inv = _band_probs(q_ref[:, cs], kp_ref[:, cs], kc_ref[:, cs], bias_scr[hh])
            o = (_dot(pp.astype(BF16), vp_ref[:, cs]) + _dot(pc.astype(BF16), vc_ref[:, cs])) * inv
            z = z_ref[:, cs].astype(F32)
            y = o * (z * _sigmoid(z))
            y_ref[:, cs] = y.astype(BF16)
            yt_ref[cs, :] = y.T.astype(BF16)

    specs = _band_specs(nq)
    return pl.pallas_call(
        body,
        name="band_fwd",
        grid=(A_HEADS // A_HPS, nq),
        in_specs=[specs[0], specs[1], specs[2], specs[3], specs[4], specs[5], specs[6]],
        out_specs=[pl.BlockSpec((TQ, A_HW), lambda h, i: (i, h)), pl.BlockSpec((A_HW, TQ), lambda h, i: (h, i))],
        out_shape=[jax.ShapeDtypeStruct((S, D_MODEL), BF16), jax.ShapeDtypeStruct((D_MODEL, S), BF16)],
        scratch_shapes=[pltpu.VMEM((A_HPS, TQ, 2 * TQ), F32)],
        compiler_params=_params(("parallel", "arbitrary")),
    )(h, h, h, h, h, h, u)


def _band_bwd(h, u, dycat):
    S = h.shape[0]
    nq = S // TQ
    scale = HEAD_DIM ** -0.5
    qs, ks, vs, zs = (slice(s * A_HW, (s + 1) * A_HW) for s in (SEG_Q, SEG_K, SEG_V, SEG_Z))

    def body(q_ref, kp_ref, kc_ref, vp_ref, vc_ref, z_ref, u_ref, dy_ref,
             dh_ref, du_ref, bias_scr, db_scr, ckt_scr, cvt_scr, cq_scr, cz_scr):
        i = pl.program_id(1)

        @pl.when(i <= 1)
        def _():
            for hh in range(A_HPS):
                bias_scr[hh] = _band_bias(u_ref[hh], (i == 0).astype(F32))

        @pl.when(i == 0)
        def _():
            db_scr[...] = jnp.zeros_like(db_scr)
            ckt_scr[...] = jnp.zeros_like(ckt_scr)
            cvt_scr[...] = jnp.zeros_like(cvt_scr)
            cq_scr[...] = jnp.zeros_like(cq_scr)
            cz_scr[...] = jnp.zeros_like(cz_scr)

        @pl.when(i < nq)
        def _():
            dh_ref[:, qs] = cq_scr[...]
            dh_ref[:, zs] = cz_scr[...]
            for hh in range(A_HPS):
                cs = slice(hh * LANES, (hh + 1) * LANES)
                q, kp, kc, vp, vc = q_ref[:, cs], kp_ref[:, cs], kc_ref[:, cs], vp_ref[:, cs], vc_ref[:, cs]
                pp, pc, inv = _band_probs(q, kp, kc, bias_scr[hh])
                pp, pc = pp * inv, pc * inv
                ppb, pcb = pp.astype(BF16), pc.astype(BF16)
                o = _dot(ppb, vp) + _dot(pcb, vc)
                z = z_ref[:, cs].astype(F32)
                sg = _sigmoid(z)
                dy = dy_ref[:, cs].astype(F32)
                do = dy * (z * sg)
                cz_scr[:, cs] = (dy * o * (sg * (1.0 + z * (1.0 - sg)))).astype(BF16)
                dob = do.astype(BF16)
                delta = jnp.sum(do * o, axis=1, keepdims=True)
                dsp = pp * (_dot_nt(dob, vp) - delta)
                dsc = pc * (_dot_nt(dob, vc) - delta)
                db_scr[hh, :, :TQ] += dsp
                db_scr[hh, :, TQ:] += dsc
                dspb, dscb = dsp.astype(BF16), dsc.astype(BF16)
                cq_scr[:, cs] = (scale * (_dot(dspb, kp) + _dot(dscb, kc))).astype(BF16)
                qt, dot_ = q.T, dob.T
                dh_ref[:, SEG_K * A_HW + hh * LANES:SEG_K * A_HW + (hh + 1) * LANES] = (
                    ckt_scr[cs, :] + scale * _dot(qt, dspb)).T.astype(BF16)
                dh_ref[:, SEG_V * A_HW + hh * LANES:SEG_V * A_HW + (hh + 1) * LANES] = (
                    cvt_scr[cs, :] + _dot(dot_, ppb)).T.astype(BF16)
                ckt_scr[cs, :] = scale * _dot(qt, dscb)
                cvt_scr[cs, :] = _dot(dot_, pcb)

        @pl.when(i == nq)
        def _():
            dh_ref[:, qs] = cq_scr[...]
            dh_ref[:, zs] = cz_scr[...]
            dh_ref[:, ks] = ckt_scr[...].T.astype(BF16)
            dh_ref[:, vs] = cvt_scr[...].T.astype(BF16)
            r0 = lax.broadcasted_iota(jnp.int32, (TQ, TQ), 0)
            r1 = lax.broadcasted_iota(jnp.int32, (TQ, TQ), 1)
            flip = (r0 + r1 == TQ - 1).astype(BF16)
            for hh in range(A_HPS):
                fl = _dot3(flip, db_scr[hh])
                rolled = pltpu.roll(fl, 0, 1, stride=1, stride_axis=0)
                du_ref[hh] = jnp.sum(rolled, axis=0, keepdims=True)

    specs = _band_specs(nq)
    dy_spec = pl.BlockSpec((TQ, A_HW), lambda h, i: (jnp.minimum(i, nq - 1), h))
    return pl.pallas_call(
        body,
        name="band_bwd",
        grid=(A_HEADS // A_HPS, nq + 1),
        in_specs=[specs[0], specs[1], specs[2], specs[3], specs[4], specs[5], specs[6], dy_spec],
        out_specs=[pl.BlockSpec((TQ, 4 * A_HW), lambda h, i: (jnp.maximum(i - 1, 0), C_A // (4 * A_HW) + h)),
                   pl.BlockSpec((A_HPS, 1, 2 * TQ), lambda h, i: (h, 0, 0))],
        out_shape=[jax.ShapeDtypeStruct((S, HP), BF16), jax.ShapeDtypeStruct((A_HEADS, 1, 2 * TQ), F32)],
        scratch_shapes=[pltpu.VMEM((A_HPS, TQ, 2 * TQ), F32), pltpu.VMEM((A_HPS, TQ, 2 * TQ), F32),
                        pltpu.VMEM((A_HW, TQ), F32), pltpu.VMEM((A_HW, TQ), F32),
                        pltpu.VMEM((TQ, A_HW), BF16), pltpu.VMEM((TQ, A_HW), BF16)],
        compiler_params=_params(("parallel", "arbitrary")),
    )(h, h, h, h, h, h, u, dycat)


def _bias_by_offset(table):
    far = jnp.broadcast_to(table[:, N_REL - 1:], (A_HEADS, TQ - MAX_REL))
    ramp = jnp.flip(table, axis=1)
    rest = jnp.broadcast_to(table[:, :1], (A_HEADS, 2 * TQ - CHUNK - (TQ + MAX_REL + 1)))
    wrap = jnp.broadcast_to(table[:, N_REL - 1:], (A_HEADS, CHUNK))
    return jnp.concatenate([far, ramp, rest, wrap], axis=1)[:, None, :]


def _bias_grad_from_offset(du):
    g = jnp.roll(du[:, 0, :], -(TQ - 1), axis=1)
    far = jnp.sum(g[:, :TQ - MAX_REL], axis=1) + jnp.sum(g[:, 2 * TQ - CHUNK:], axis=1)
    ramp = jnp.flip(g[:, TQ - MAX_REL:TQ + MAX_REL + 1], axis=1)
    return ramp.at[:, N_REL - 1].add(far)


def _mem_probs(q, mk):
    s = _dot_nt(q, mk) * (HEAD_DIM ** -0.5)
    p = jnp.exp(s - jnp.max(s, axis=1, keepdims=True))
    return p * (1.0 / jnp.sum(p, axis=1, keepdims=True))


def _mem_fwd(h, mkv, ycat, ycat_t):
    S = h.shape[0]
    nm = mkv.shape[0]
    c0 = (A_WIDTH + B_WIDTH) // LANES

    def body(q_ref, z_ref, mk_ref, mv_ref, yin_ref, ytin_ref, y_ref, yt_ref):
        del yin_ref, ytin_ref
        p = _mem_probs(q_ref[...], mk_ref[...])
        o = _dot(p.astype(BF16), mv_ref[...])
        z = z_ref[...].astype(F32)
        y = o * (z * _sigmoid(z))
        y_ref[...] = y.astype(BF16)
        yt_ref[...] = y.T.astype(BF16)

    return pl.pallas_call(
        body,
        name="mem_fwd",
        grid=(M_HEADS, S // TQ),
        in_specs=[pl.BlockSpec((TQ, LANES), lambda h, i: (i, C_M // LANES + 2 * h)),
                  pl.BlockSpec((TQ, LANES), lambda h, i: (i, C_M // LANES + 2 * h + 1)),
                  pl.BlockSpec((nm, LANES), lambda h, i: (0, h)),
                  pl.BlockSpec((nm, LANES), lambda h, i: (0, M_HEADS + h)), ANY, ANY],
        out_specs=[pl.BlockSpec((TQ, LANES), lambda h, i: (i, c0 + h)), pl.BlockSpec((LANES, TQ), lambda h, i: (c0 + h, i))],
        out_shape=[jax.ShapeDtypeStruct(ycat.shape, BF16), jax.ShapeDtypeStruct(ycat_t.shape, BF16)],
        input_output_aliases={4: 0, 5: 1},
        compiler_params=_params(("parallel", "arbitrary")),
    )(h, h, mkv, mkv, ycat, ycat_t)


def _mem_bwd(h, mkv, dycat, dh):
    S = h.shape[0]
    nm = mkv.shape[0]
    scale = HEAD_DIM ** -0.5

    def body(q_ref, z_ref, mk_ref, mv_ref, dy_ref, dhin_ref, dh_ref, dmk_ref, dmv_ref):
        del dhin_ref
        i = pl.program_id(1)
        q, mk, mv = q_ref[...], mk_ref[...], mv_ref[...]
        p = _mem_probs(q, mk)
        pb = p.astype(BF16)
        o = _dot(pb, mv)
        z = z_ref[...].astype(F32)
        sg = _sigmoid(z)
        dy = dy_ref[...].astype(F32)
        do = dy * (z * sg)
        dh_ref[:, LANES:] = (dy * o * (sg * (1.0 + z * (1.0 - sg)))).astype(BF16)
        dob = do.astype(BF16)
        ds = p * (_dot_nt(dob, mv) - jnp.sum(do * o, axis=1, keepdims=True))
        dsb = ds.astype(BF16)
        dh_ref[:, :LANES] = (scale * _dot(dsb, mk)).astype(BF16)
        dmk = scale * _dot_tn(dsb, q)
        dmv = _dot_tn(pb, dob)

        @pl.when(i == 0)
        def _():
            dmk_ref[...] = dmk
            dmv_ref[...] = dmv

        @pl.when(i > 0)
        def _():
            dmk_ref[...] += dmk
            dmv_ref[...] += dmv

    dh, dmk, dmv = pl.pallas_call(
        body,
        name="mem_bwd",
        grid=(M_HEADS, S // TQ),
        in_specs=[pl.BlockSpec((TQ, LANES), lambda h, i: (i, C_M // LANES + 2 * h)),
                  pl.BlockSpec((TQ, LANES), lambda h, i: (i, C_M // LANES + 2 * h + 1)),
                  pl.BlockSpec((nm, LANES), lambda h, i: (0, h)),
                  pl.BlockSpec((nm, LANES), lambda h, i: (0, M_HEADS + h)),
                  pl.BlockSpec((TQ, LANES), lambda h, i: (i, (A_WIDTH + B_WIDTH) // LANES + h)), ANY],
        out_specs=[pl.BlockSpec((TQ, 2 * LANES), lambda h, i: (i, C_M // (2 * LANES) + h)),
                   pl.BlockSpec((nm, LANES), lambda h, i: (0, h)),
                   pl.BlockSpec((nm, LANES), lambda h, i: (0, h))],
        out_shape=[jax.ShapeDtypeStruct(dh.shape, BF16),
                   jax.ShapeDtypeStruct((nm, M_WIDTH), F32), jax.ShapeDtypeStruct((nm, M_WIDTH), F32)],
        input_output_aliases={5: 0},
        compiler_params=_params(("parallel", "arbitrary")),
    )(h, h, mkv, mkv, dycat, dh)
    return dh, jnp.concatenate([dmk, dmv], axis=1)


def _chunk_masks():
    r = lax.broadcasted_iota(jnp.int32, (TQ, TQ), 0)
    c = lax.broadcasted_iota(jnp.int32, (TQ, TQ), 1)
    same = _chunk_of(r) == _chunk_of(c)
    return jnp.logical_and(same, c <= r), jnp.logical_and(same, c > r)


def _gla_gates(lr, gw, gb):
    logit = _dot(lr, gw) + gb
    sg = _sigmoid(logit)
    g = (jnp.minimum(logit, 0.0) - jnp.log(1.0 + jnp.exp(-jnp.abs(logit)))) * (1.0 / GATE_TAU)
    lo, _ = _chunk_masks()
    return sg, _dot3(lo.astype(BF16), g)


def _gla_factors(q, k, b):
    eb = jnp.exp(b)
    enb = jnp.exp(-b)
    return eb, enb, q * eb, q * enb, k * eb, k * enb


def _gla_intra(qp, qn, kp, kn):
    lo, up = _chunk_masks()
    return (jnp.where(lo, _dot_nt(qp.astype(BF16), kn.astype(BF16)), 0.0)
            + jnp.where(up, _dot_nt(qn.astype(BF16), kp.astype(BF16)), 0.0))


def _gla_specs(nb, rev):
    blk = (lambda i: nb - 1 - i) if rev else (lambda i: i)

    def at(seg):
        return pl.BlockSpec((TQ, LANES), lambda i, h: (blk(i), C_B // LANES + 4 * h + seg))

    lr_spec = pl.BlockSpec((TQ, LANES), lambda i, h: (blk(i), C_LR // LANES))
    gw_spec = pl.BlockSpec((LANES, LANES), lambda i, h: (0, h))
    gb_spec = pl.BlockSpec((1, LANES), lambda i, h: (0, h))
    gn_spec = pl.BlockSpec((1, LANES), lambda i, h: (0, 0))
    return at(SEG_Q), at(SEG_K), at(SEG_V), at(SEG_Z), lr_spec, gw_spec, gb_spec, gn_spec, blk


def _gla_fwd(h, gw, gb, gn, ycat, ycat_t):
    S = h.shape[0]
    nb = S // TQ
    c0 = A_WIDTH // LANES

    def body(q_ref, k_ref, v_ref, z_ref, lr_ref, gw_ref, gb_ref, gn_ref, yin_ref, ytin_ref,
             y_ref, yt_ref, o_ref, st_ref, st_scr):
        del yin_ref, ytin_ref
        i, hd = pl.program_id(0), pl.program_id(1)

        @pl.when(i == 0)
        def _():
            st_scr[hd] = jnp.zeros((LANES, LANES), F32)

        q = q_ref[...].astype(F32) * (B_DK ** -0.5)
        k = k_ref[...].astype(F32)
        v = v_ref[...]
        _, b = _gla_gates(lr_ref[...], gw_ref[...], gb_ref[...])
        _, _, qp, qn, kp, kn = _gla_factors(q, k, b)
        o_intra = _dot(_gla_intra(qp, qn, kp, kn).astype(BF16), v)
        qpb, knb = qp.astype(BF16), kn.astype(BF16)
        st = st_scr[hd]
        outs = []
        for c in range(CPB):
            rows = slice(c * CHUNK, (c + 1) * CHUNK)
            st_ref[0, c] = st
            outs.append(_dot_nt(qpb[rows], st.astype(BF16)))
            e_last = jnp.exp(b[(c + 1) * CHUNK - 1:(c + 1) * CHUNK, :])
            st = (st + _dot_tn(v[rows], knb[rows])) * e_last
        st_scr[hd] = st
        o = o_intra + jnp.concatenate(outs, axis=0)
        o_ref[...] = o
        r = lax.rsqrt(jnp.mean(o * o, axis=1, keepdims=True) + RMS_EPS)
        z = z_ref[...].astype(F32)
        y = o * r * gn_ref[...] * (z * _sigmoid(z))
        y_ref[...] = y.astype(BF16)
        yt_ref[...] = y.T.astype(BF16)

    q_s, k_s, v_s, z_s, lr_s, gw_s, gb_s, gn_s, _ = _gla_specs(nb, False)
    row = pl.BlockSpec((TQ, LANES), lambda i, h: (i, h))
    return pl.pallas_call(
        body,
        name="gla_fwd",
        grid=(nb, B_HEADS),
        in_specs=[q_s, k_s, v_s, z_s, lr_s, gw_s, gb_s, gn_s, ANY, ANY],
        out_specs=[pl.BlockSpec((TQ, LANES), lambda i, h: (i, c0 + h)), pl.BlockSpec((LANES, TQ), lambda i, h: (c0 + h, i)),
                   row, pl.BlockSpec((1, CPB, LANES, LANES), lambda i, h: (h, i, 0, 0))],
        out_shape=[jax.ShapeDtypeStruct(ycat.shape, BF16), jax.ShapeDtypeStruct(ycat_t.shape, BF16),
                   jax.ShapeDtypeStruct((S, B_WIDTH), F32),
                   jax.ShapeDtypeStruct((B_HEADS, S // CHUNK, LANES, LANES), F32)],
        input_output_aliases={8: 0, 9: 1},
        scratch_shapes=[pltpu.VMEM((B_HEADS, LANES, LANES), F32)],
        compiler_params=_params(("arbitrary", "arbitrary")),
    )(h, h, h, h, h, gw, gb, gn, ycat, ycat_t)


def _gla_bwd(h, gw, gb, gn, o_pre, states, dycat, dh):
    S = h.shape[0]
    nb = S // TQ

    def body(q_ref, k_ref, v_ref, z_ref, lr_ref, gw_ref, gb_ref, gn_ref, o_ref, st_ref, dy_ref, dhin_ref,
             dh_ref, dlr_ref, dgw_ref, dgb_ref, dgn_ref,
             dst_scr, dgw_scr, dgb_scr, dgn_scr):
        del dhin_ref
        i, hd = pl.program_id(0), pl.program_id(1)

        @pl.when(i == 0)
        def _():
            dst_scr[hd] = jnp.zeros((LANES, LANES), F32)
            dgw_scr[hd] = jnp.zeros((LANES, LANES), F32)
            dgb_scr[hd] = jnp.zeros((1, LANES), F32)

        @pl.when(jnp.logical_and(i == 0, hd == 0))
        def _():
            dgn_scr[...] = jnp.zeros_like(dgn_scr)

        q = q_ref[...].astype(F32) * (B_DK ** -0.5)
        k = k_ref[...].astype(F32)
        v = v_ref[...]
        lr, gwv = lr_ref[...], gw_ref[...]
        sg, b = _gla_gates(lr, gwv, gb_ref[...])
        eb, enb, qp, qn, kp, kn = _gla_factors(q, k, b)
        a = _gla_intra(qp, qn, kp, kn)
        qpb, qnb, kpb, knb = qp.astype(BF16), qn.astype(BF16), kp.astype(BF16), kn.astype(BF16)

        o = o_ref[...]
        gn = gn_ref[...]
        r = lax.rsqrt(jnp.mean(o * o, axis=1, keepdims=True) + RMS_EPS)
        z = z_ref[...].astype(F32)
        sz = _sigmoid(z)
        dy = dy_ref[...].astype(F32)
        d_on = dy * (z * sz)
        dh_ref[:, SEG_Z * LANES:(SEG_Z + 1) * LANES] = (dy * (o * r * gn) * (sz * (1.0 + z * (1.0 - sz)))).astype(BF16)
        dgn_scr[...] += jnp.sum(d_on * o * r, axis=0, keepdims=True)
        t = d_on * gn
        do = r * t - o * (r * r * r) * jnp.mean(t * o, axis=1, keepdims=True)
        dob = do.astype(BF16)

        lo, up = _chunk_masks()
        da = _dot_nt(dob, v)
        dalo = jnp.where(lo, da, 0.0).astype(BF16)
        daup = jnp.where(up, da, 0.0).astype(BF16)
        dqp = _dot(dalo, knb)
        dkn = _dot_tn(dalo, qpb)
        dqn = _dot(daup, kpb)
        dkp = _dot_tn(daup, qnb)
        dv = _dot_tn(a.astype(BF16), dob)

        dst = dst_scr[hd]
        dqp_c, dkn_c, dv_c, dbl_c = [None] * CPB, [None] * CPB, [None] * CPB, [None] * CPB
        for c in reversed(range(CPB)):
            rows = slice(c * CHUNK, (c + 1) * CHUNK)
            st = st_ref[0, c]
            e_last = jnp.exp(b[(c + 1) * CHUNK - 1:(c + 1) * CHUNK, :])
            if c == CPB - 1:
                st_next = (st + _dot_tn(v[rows], knb[rows])) * e_last
            else:
                st_next = st_ref[0, c + 1]
            dbl_c[c] = jnp.sum(dst * st_next, axis=0, keepdims=True)
            dtt = (dst * e_last).astype(BF16)
            dv_c[c] = _dot_nt(knb[rows], dtt)
            dkn_c[c] = _dot(v[rows], dtt)
            dqp_c[c] = _dot(dob[rows], st.astype(BF16))
            dst = _dot_tn(dob[rows], qpb[rows]) + dst * e_last
        dst_scr[hd] = dst
        dqp = dqp + jnp.concatenate(dqp_c, axis=0)
        dkn = dkn + jnp.concatenate(dkn_c, axis=0)
        dv = dv + jnp.concatenate(dv_c, axis=0)
        dh_ref[:, SEG_V * LANES:(SEG_V + 1) * LANES] = dv.astype(BF16)
        dh_ref[:, SEG_Q * LANES:(SEG_Q + 1) * LANES] = ((dqp * eb + dqn * enb) * (B_DK ** -0.5)).astype(BF16)
        dh_ref[:, SEG_K * LANES:(SEG_K + 1) * LANES] = (dkp * eb + dkn * enb).astype(BF16)

        last = jnp.bitwise_and(lax.broadcasted_iota(jnp.int32, (TQ, 1), 0), CHUNK - 1) == CHUNK - 1
        dbl = jnp.concatenate([jnp.broadcast_to(x, (CHUNK, LANES)) for x in dbl_c], axis=0)
        db = dqp * qp - dqn * qn + dkp * kp - dkn * kn + jnp.where(last, dbl, 0.0)
        r0 = lax.broadcasted_iota(jnp.int32, (TQ, TQ), 0)
        r1 = lax.broadcasted_iota(jnp.int32, (TQ, TQ), 1)
        upper = jnp.logical_and(_chunk_of(r0) == _chunk_of(r1), r1 >= r0).astype(BF16)
        dlogit = _dot3(upper, db) * (1.0 / GATE_TAU) * (1.0 - sg)
        dlb = dlogit.astype(BF16)
        dlr = _dot_nt(dlb, gwv)

        @pl.when(hd == 0)
        def _():
            dlr_ref[...] = dlr

        @pl.when(hd > 0)
        def _():
            dlr_ref[...] += dlr

        dgw_scr[hd] += _dot_tn(lr, dlb)
        dgb_scr[hd] += jnp.sum(dlogit, axis=0, keepdims=True)

        @pl.when(i == nb - 1)
        def _():
            dgw_ref[...] = dgw_scr[hd]
            dgb_ref[...] = dgb_scr[hd]
            dgn_ref[...] = dgn_scr[...]

    q_s, k_s, v_s, z_s, lr_s, gw_s, gb_s, gn_s, blk = _gla_specs(nb, True)
    row = pl.BlockSpec((TQ, LANES), lambda i, h: (blk(i), h))
    dy_spec = pl.BlockSpec((TQ, LANES), lambda i, h: (blk(i), A_WIDTH // LANES + h))
    st_spec = pl.BlockSpec((1, CPB, LANES, LANES), lambda i, h: (h, blk(i), 0, 0))
    return pl.pallas_call(
        body,
        name="gla_bwd",
        grid=(nb, B_HEADS),
        in_specs=[q_s, k_s, v_s, z_s, lr_s, gw_s, gb_s, gn_s, row, st_spec, dy_spec, ANY],
        out_specs=[pl.BlockSpec((TQ, 4 * LANES), lambda i, h: (blk(i), C_B // (4 * LANES) + h)),
                   pl.BlockSpec((TQ, LANES), lambda i, h: (blk(i), 0)),
                   pl.BlockSpec((LANES, LANES), lambda i, h: (0, jnp.where(i == nb - 1, h, 0))),
                   pl.BlockSpec((1, LANES), lambda i, h: (0, jnp.where(i == nb - 1, h, 0))),
                   pl.BlockSpec((1, LANES), lambda i, h: (0, 0))],
        out_shape=[jax.ShapeDtypeStruct(dh.shape, BF16),
                   jax.ShapeDtypeStruct((S, LANES), F32),
                   jax.ShapeDtypeStruct((LANES, B_HEADS * LANES), F32),
                   jax.ShapeDtypeStruct((1, B_HEADS * LANES), F32),
                   jax.ShapeDtypeStruct((1, LANES), F32)],
        input_output_aliases={11: 0},
        scratch_shapes=[pltpu.VMEM((B_HEADS, LANES, LANES), F32), pltpu.VMEM((B_HEADS, LANES, LANES), F32),
                        pltpu.VMEM((B_HEADS, 1, LANES), F32), pltpu.VMEM((1, LANES), F32)],
        compiler_params=_params(("arbitrary", "arbitrary")),
    )(h, h, h, h, h, gw, gb, gn, o_pre, states, dycat, dh)


def _lr_fill(dlr, dh):
    S = dlr.shape[0]
    w = HP - C_LR

    def body(dlr_ref, dhin_ref, dh_ref):
        del dhin_ref
        dh_ref[:, :LANES] = dlr_ref[...].astype(BF16)
        dh_ref[:, LANES:] = jnp.zeros((TQ, w - LANES), BF16)

    return pl.pallas_call(
        body, name="lr_fill", grid=(S // TQ,),
        in_specs=[pl.BlockSpec((TQ, LANES), lambda i: (i, 0)), ANY],
        out_specs=pl.BlockSpec((TQ, w), lambda i: (i, C_LR // w)),
        out_shape=jax.ShapeDtypeStruct(dh.shape, BF16),
        input_output_aliases={1: 0},
        compiler_params=_params(("parallel",)),
    )(dlr, dh)


LN_ROWS = 256


def _outproj_ln(ycat, w_out, x, g, b):
    S = x.shape[0]

    def body(yc_ref, w_ref, x_ref, g_ref, b_ref, y_ref, yb_ref, yt_ref, xh_ref, rs_ref):
        u = ALPHA * x_ref[...] + _dot(yc_ref[...], w_ref[...])
        mu = jnp.mean(u, axis=1, keepdims=True)
        d = u - mu
        rstd = lax.rsqrt(jnp.mean(d * d, axis=1, keepdims=True) + LN_EPS)
        xh = d * rstd
        y = xh * g_ref[...] + b_ref[...]
        y_ref[...] = y
        yb_ref[...] = y.astype(BF16)
        yt_ref[...] = y.T.astype(BF16)
        xh_ref[...] = xh
        rs_ref[...] = rstd

    row = lambda w: pl.BlockSpec((LN_ROWS, w), lambda i: (i, 0))
    vec = pl.BlockSpec((1, D_MODEL), lambda i: (0, 0))
    return pl.pallas_call(
        body,
        name="outproj_ln",
        grid=(S // LN_ROWS,),
        in_specs=[row(D_MODEL), pl.BlockSpec((D_MODEL, D_MODEL), lambda i: (0, 0)), row(D_MODEL), vec, vec],
        out_specs=[row(D_MODEL), row(D_MODEL), pl.BlockSpec((D_MODEL, LN_ROWS), lambda i: (0, i)), row(D_MODEL), row(1)],
        out_shape=[jax.ShapeDtypeStruct((S, D_MODEL), F32), jax.ShapeDtypeStruct((S, D_MODEL), BF16),
                   jax.ShapeDtypeStruct((D_MODEL, S), BF16),
                   jax.ShapeDtypeStruct((S, D_MODEL), F32), jax.ShapeDtypeStruct((S, 1), F32)],
        compiler_params=_params(("parallel",)),
    )(ycat, w_out, x, g, b)


def _ln_bwd(dy, xhat, rstd, g):
    S = dy.shape[0]

    def body(dy_ref, xh_ref, rs_ref, g_ref, du_ref, dub_ref, dg_ref, db_ref):
        i = pl.program_id(0)
        dy_, xh = dy_ref[...], xh_ref[...]
        dyg = dy_ * g_ref[...]
        m1 = jnp.mean(dyg, axis=1, keepdims=True)
        m2 = jnp.mean(dyg * xh, axis=1, keepdims=True)
        du = rs_ref[...] * (dyg - m1 - xh * m2)
        du_ref[...] = du
        dub_ref[...] = du.astype(BF16)
        dg = jnp.sum(dy_ * xh, axis=0, keepdims=True)
        db = jnp.sum(dy_, axis=0, keepdims=True)

        @pl.when(i == 0)
        def _():
            dg_ref[...] = dg
            db_ref[...] = db

        @pl.when(i > 0)
        def _():
            dg_ref[...] += dg
            db_ref[...] += db

    row = lambda w: pl.BlockSpec((TQ, w), lambda i: (i, 0))
    vec = pl.BlockSpec((1, D_MODEL), lambda i: (0, 0))
    return pl.pallas_call(
        body,
        name="ln_bwd",
        grid=(S // TQ,),
        in_specs=[row(D_MODEL), row(D_MODEL), row(1), vec],
        out_specs=[row(D_MODEL), row(D_MODEL), vec, vec],
        out_shape=[jax.ShapeDtypeStruct((S, D_MODEL), F32), jax.ShapeDtypeStruct((S, D_MODEL), BF16),
                   jax.ShapeDtypeStruct((1, D_MODEL), F32), jax.ShapeDtypeStruct((1, D_MODEL), F32)],
        compiler_params=_params(("arbitrary",)),
    )(dy, xhat, rstd, g)


def _loss_head(y, target):
    S = y.shape[0]

    def body(y_ref, t_ref, l_ref, dy_ref):
        i = pl.program_id(0)
        err = y_ref[...] - t_ref[...]
        dy_ref[...] = err * (1.0 / D_MODEL)
        part = (0.5 / D_MODEL) * jnp.sum(jnp.sum(err * err, axis=1, keepdims=True), axis=0, keepdims=True)

        @pl.when(i == 0)
        def _():
            l_ref[...] = part

        @pl.when(i > 0)
        def _():
            l_ref[...] += part

    row = pl.BlockSpec((TQ, D_MODEL), lambda i: (i, 0))
    return pl.pallas_call(
        body,
        name="loss_head",
        grid=(S // TQ,),
        in_specs=[row, row],
        out_specs=[pl.BlockSpec((1, 1), lambda i: (0, 0)), row],
        out_shape=[jax.ShapeDtypeStruct((1, 1), F32), jax.ShapeDtypeStruct((S, D_MODEL), F32)],
        compiler_params=_params(("arbitrary",)),
    )(y, target)


def _pad_gate(gate_w, gate_b):
    gw = gate_w.reshape(GATE_RANK, B_HEADS, B_DK)
    gw = jnp.pad(gw, ((0, LANES - GATE_RANK), (0, 0), (0, LANES - B_DK))).reshape(LANES, B_HEADS * LANES)
    gb = jnp.pad(gate_b.reshape(B_HEADS, B_DK), ((0, 0), (0, LANES - B_DK))).reshape(1, B_HEADS * LANES)
    return gw.astype(BF16), gb.astype(F32)


def _layer_fwd(x, xb, xt, mem_b, w_in, w_kv, w_out, u, gw, gb, gn, ln_g, ln_b, rider=None):
    h = _matmul(xb, w_in, mode="nn", out_dtype=BF16, tm=1024, tn=768, tk=D_MODEL, name="in_proj", rider=rider)
    if rider is not None:
        h, rode = h
    mkv = _matmul(mem_b, w_kv, mode="nn", out_dtype=BF16, tm=mem_b.shape[0], tn=1024, tk=D_MODEL, name="mem_kv")
    ycat, ycat_t = _band_fwd(h, u)
    ycat, ycat_t, o_pre, states = _gla_fwd(h, gw, gb, gn, ycat, ycat_t)
    ycat, ycat_t = _mem_fwd(h, mkv, ycat, ycat_t)
    y, ybf, yt, xhat, rstd = _outproj_ln(ycat, w_out, x, ln_g, ln_b)
    saved = (xt, h, mkv, ycat_t, o_pre, states, xhat, rstd)
    return (y, ybf, yt, saved) if rider is None else (y, ybf, yt, saved, rode)


def _layer_bwd(dy, saved, mem_b, w_in, w_out, u, gw, gb, gn, ln_g, reduce=None):
    xt, h, mkv, ycat_t, o_pre, states, xhat, rstd = saved

    def riding(**kw):
        if reduce is None:
            return _matmul(**kw)
        out, bufs = _matmul(rider=reduce.rider(), **kw)
        reduce.landed(bufs)
        return out

    du, dub, d_ln_g, d_ln_b = _ln_bwd(dy, xhat, rstd, ln_g)
    dycat = riding(a=dub, b=w_out, mode="nt", out_dtype=BF16, tm=1024, tn=1024, tk=D_MODEL, name="dycat")
    d_w_out = _matmul(ycat_t, dub, mode="nn", out_dtype=F32, tm=1024, tn=1024, tk=2048, name="d_w_out")
    dh, d_u = _band_bwd(h, u, dycat)
    dh, dlr, dgw, dgb, dgn = _gla_bwd(h, gw, gb, gn, o_pre, states, dycat, dh)
    dh, dmkv = _mem_bwd(h, mkv, dycat, dh)
    dh = _lr_fill(dlr, dh)
    d_w_kv = _matmul(mem_b, dmkv, mode="tn", out_dtype=F32, tm=1024, tn=1024, tk=mem_b.shape[0], name="d_w_kv")
    dx = riding(a=dh, b=w_in, mode="nt", out_dtype=F32, tm=1024, tn=1024, tk=2560, name="dx", add=du, add_scale=ALPHA)
    d_w_in = riding(a=xt, b=dh, mode="nn", out_dtype=F32, tm=1024, tn=768, tk=2048, name="d_w_in")
    return dx, (d_w_in, d_u, dgw, dgb, dgn, d_w_kv, d_w_out, d_ln_g, d_ln_b)


def _unpad_heads(w):
    r = w.shape[0]
    return w.reshape(r, B_HEADS, LANES)[:, :, :B_DK].reshape(r, B_KEY_WIDTH)


def _padded_col_of():
    col, o = np.zeros(IN_WIDTH, np.int64), 0
    for seg in (SEG_Q, SEG_K, SEG_V, SEG_Z):
        for hd in range(A_HEADS):
            col[o:o + LANES] = C_A + (hd // A_HPS) * 4 * A_HW + seg * A_HW + (hd % A_HPS) * LANES + np.arange(LANES)
            o += LANES
    for seg, width in ((SEG_Q, B_DK), (SEG_K, B_DK), (SEG_V, LANES), (SEG_Z, LANES)):
        for hd in range(B_HEADS):
            col[o:o + width] = C_B + hd * 4 * LANES + seg * LANES + np.arange(width)
            o += width
    col[o:o + GATE_RANK] = C_LR + np.arange(GATE_RANK)
    o += GATE_RANK
    for seg in (0, 1):
        for hd in range(M_HEADS):
            col[o:o + LANES] = C_M + hd * 2 * LANES + seg * LANES + np.arange(LANES)
            o += LANES
    assert o == IN_WIDTH
    return col


def _runs(idx):
    out, start = [], 0
    for k in range(1, len(idx) + 1):
        if k == len(idx) or idx[k] != idx[k - 1] + 1:
            out.append((int(idx[start]), k - start))
            start = k
    return out


def _chip_columns(g, j, n):
    runs = _runs(_padded_col_of()[j * n:(j + 1) * n])
    return jnp.concatenate([g[:, a:a + ln] for a, ln in runs], axis=1)


def _padded_from_shards(shards):
    n = shards[0].shape[1]
    src = np.full(HP, -1, np.int64)
    src[_padded_col_of()] = np.arange(IN_WIDTH)
    parts, k = [], 0
    while k < HP:
        e = k + 1
        if src[k] < 0:
            while e < HP and src[e] < 0:
                e += 1
            parts.append(jnp.zeros((shards[0].shape[0], e - k), shards[0].dtype))
        else:
            while e < HP and src[e] == src[e - 1] + 1 and src[e] // n == src[k] // n:
                e += 1
            parts.append(shards[src[k] // n][:, src[k] % n:src[k] % n + e - k])
        k = e
    return jnp.concatenate(parts, axis=1)


ADAMW_BLOCK_BYTES = 1 << 20


def _adamw(w, g, m, v, name):
    L, R, C = w.shape
    tl, tr = 1, R
    if R * C * 4 <= ADAMW_BLOCK_BYTES:
        tl = max(d for d in range(1, L + 1) if L % d == 0 and d * R * C * 4 <= ADAMW_BLOCK_BYTES)
    else:
        for cand in (256, 128, 64, 32, 16, 8):
            if R % cand == 0 and R > cand:
                tr = cand
                break

    def body(w_ref, g_ref, m_ref, v_ref, d_ref, nm_ref, nv_ref):
        g_ = g_ref[...]
        nm = ADAM_B1 * m_ref[...] + (1.0 - ADAM_B1) * g_
        nv = ADAM_B2 * v_ref[...] + (1.0 - ADAM_B2) * (g_ * g_)
        m_hat = nm / (1.0 - ADAM_B1 ** ADAM_STEP)
        v_hat = nv / (1.0 - ADAM_B2 ** ADAM_STEP)
        d_ref[...] = -ADAM_LR * (m_hat / (jnp.sqrt(v_hat) + ADAM_EPS) + ADAM_WD * w_ref[...])
        nm_ref[...] = nm
        nv_ref[...] = nv

    spec = pl.BlockSpec((tl, tr, C), lambda l, i: (l, i, 0))
    sd = jax.ShapeDtypeStruct((L, R, C), F32)
    return pl.pallas_call(
        body, name=name, grid=(L // tl, R // tr), in_specs=[spec] * 4, out_specs=[spec] * 3, out_shape=[sd] * 3,
        compiler_params=_params(("parallel", "parallel")),
    )(w, g, m, v)


def _adamw_nd(w, g, m, v, name):
    shape = w.shape
    f = (lambda a: a) if w.ndim == 3 else (lambda a: a.reshape(1, shape[0], shape[1]))
    return tuple(o.reshape(shape) for o in _adamw(f(w), f(g), f(m), f(v), name))


ANY = pl.BlockSpec(memory_space=pl.ANY)


def _place():
    x, y, c = lax.axis_index("x"), lax.axis_index("y"), lax.axis_index("c")
    chips = [(1 - x, y), (x, 1 - y), (1 - x, 1 - y)]
    return x, y, c, chips


class _WeightGather:
    def __init__(self, hop, layer, rows):
        self.hop, self.layer, self.rows = hop, layer, rows
        self.n_sem = 3 * len(rows)

    def _copies(self, shard_refs, buf_refs, send, recv, received):
        x, y, c, chips = _place()
        out = []
        for t, R in enumerate(self.rows):
            half = R // 2
            assert half % 16 == 0
            mine = pl.ds(pl.multiple_of(c * half, 16), half)
            other = pl.ds(pl.multiple_of((1 - c) * half, 16), half)
            mine_of_shard = pl.ds(pl.multiple_of(self.layer * R + c * half, 16), half)
            for k, chip in enumerate(chips):
                theirs = buf_refs[t].at[2 * chip[0] + chip[1]]
                if self.hop == "chips":
                    src, dst, to = shard_refs[t].at[mine_of_shard], buf_refs[t].at[2 * x + y, mine], (*chip, c)
                    landed = theirs.at[mine]
                else:
                    src, dst, to = theirs.at[mine], theirs.at[mine], (x, y, 1 - c)
                    landed = theirs.at[other]
                out.append(pltpu.make_async_remote_copy(
                    src_ref=src, dst_ref=landed if received else dst, send_sem=send.at[3 * t + k],
                    recv_sem=recv.at[3 * t + k], device_id=to, device_id_type=MESH))
        return out

    def start(self, shard_refs, buf_refs, send, recv):
        for cp in self._copies(shard_refs, buf_refs, send, recv, False):
            cp.start()

    def wait(self, shard_refs, buf_refs, send, recv):
        for cp in self._copies(shard_refs, buf_refs, send, recv, True):
            cp.wait_recv()
        for cp in self._copies(shard_refs, buf_refs, send, recv, False):
            cp.wait_send()

    def sems(self):
        return [pltpu.SemaphoreType.DMA((self.n_sem,)), pltpu.SemaphoreType.DMA((self.n_sem,))]

    def call(self, srcs, bufs, name):
        ns, nb = len(srcs), len(bufs)

        def body(*refs):
            src_refs, buf_refs, (send, recv) = refs[:ns], refs[ns + nb:ns + 2 * nb], refs[ns + 2 * nb:]
            self.start(src_refs, buf_refs, send, recv)
            self.wait(src_refs, buf_refs, send, recv)

        return pl.pallas_call(
            body, name=name, in_specs=[ANY] * (ns + nb), out_specs=[ANY] * nb,
            out_shape=[jax.ShapeDtypeStruct(b.shape, b.dtype) for b in bufs],
            input_output_aliases={ns + t: t for t in range(nb)},
            scratch_shapes=self.sems(),
        )(*srcs, *bufs)


class _GradHop(_WeightGather):
    def __init__(self, hop, layer, slices):
        self.hop, self.layer, self.slices = hop, layer, slices
        self.n_sem = {"pair": N_CHIPS, "chips": N_CHIPS - 1, "gather": 1}[hop] * len(slices)

    def _copies(self, src_refs, buf_refs, send, recv, received):
        x, y, c, chips = _place()
        me, out = 2 * x + y, []

        def remote(src, dst, to):
            k = len(out)
            out.append(pltpu.make_async_remote_copy(src_ref=src, dst_ref=dst, send_sem=send.at[k], recv_sem=recv.at[k],
                                                    device_id=to, device_id_type=MESH))

        for t, (half, where) in enumerate(self.slices):
            if self.hop == "pair":
                for j, (a, first) in enumerate(where):
                    rows = pl.ds(pl.multiple_of(first + (1 - c) * half, 8), half)
                    remote(src_refs[a].at[rows], buf_refs[t].at[j], (x, y, 1 - c))
            elif self.hop == "chips":
                for chip in chips:
                    slot = 2 * chip[0] + chip[1]
                    remote(src_refs[t].at[slot], buf_refs[t].at[slot if received else me], (*chip, c))
            else:
                mine = buf_refs[t].at[self.layer, c]
                remote(mine, buf_refs[t].at[self.layer, 1 - c] if received else mine, (x, y, 1 - c))
        return out


def _add_halves(parts, got, c_idx, name):
    n, L, half, C = got.shape
    tr = 64

    def body(*refs):
        ins, (got_ref, o_ref) = refs[1:1 + len(parts)], refs[1 + len(parts):]
        for k in range(len(parts)):
            o_ref[k // L, k % L] = (ins[k][...] + got_ref[k // L, k % L]).astype(BF16)

    def rows_of(first):
        assert first % tr == 0 and half % tr == 0
        return lambda i, c: (first // tr + c[0] * (half // tr) + i, 0)

    whole = pl.BlockSpec((n, L, tr, C), lambda i, c: (0, 0, i, 0))
    return pl.pallas_call(
        body, name=name,
        grid_spec=pltpu.PrefetchScalarGridSpec(
            num_scalar_prefetch=1, grid=(half // tr,),
            in_specs=[pl.BlockSpec((tr, C), rows_of(first)) for _, first in parts] + [whole],
            out_specs=whole),
        out_shape=jax.ShapeDtypeStruct((n, L, half, C), BF16),
        compiler_params=_params(("parallel",)),
    )(c_idx, *[a for a, _ in parts], got)


def _add_slots(r, c_idx, dest, layer, name):
    n, half, C = r.shape
    tr = 256

    def body(c_ref, r_ref, dest_ref, o_ref):
        del dest_ref
        acc = r_ref[0].astype(F32)
        for j in range(1, n):
            acc = acc + r_ref[j].astype(F32)
        o_ref[0, 0] = acc

    return pl.pallas_call(
        body, name=name,
        grid_spec=pltpu.PrefetchScalarGridSpec(
            num_scalar_prefetch=1, grid=(half // tr,),
            in_specs=[pl.BlockSpec((n, tr, C), lambda i, c: (0, i, 0)), ANY],
            out_specs=pl.BlockSpec((1, 1, tr, C), lambda i, c: (layer, c[0], i, 0))),
        out_shape=jax.ShapeDtypeStruct(dest.shape, F32),
        input_output_aliases={2: 0},
        compiler_params=_params(("parallel",)),
    )(c_idx, r, dest)


class _LayerReduce:
    def __init__(self, layer, grads, slices, c_idx, chip, dests):
        self.layer, self.grads, self.slices, self.c_idx, self.chip, self.dests = layer, grads, slices, c_idx, chip, dests
        self.widths = [grads[where[0][0]].shape[1] for _, where in slices]
        self.stage = 0

    def _hop(self, kind):
        return _GradHop(kind, self.layer, self.slices)

    def rider(self):
        if self.stage == 0:
            got = [lax.empty((N_CHIPS, half, w), F32) for (half, _), w in zip(self.slices, self.widths)]
            return self._hop("pair"), self.grads, got
        if self.stage == 1:
            q = [lax.empty(p.shape, BF16) for p in self.pair_sums]
            return self._hop("chips"), self.pair_sums, q
        return self._hop("gather"), [], self.dests

    def landed(self, bufs):
        tag = f"{self.layer}"
        if self.stage == 0:
            self.pair_sums = []
            for t, ((half, where), got) in enumerate(zip(self.slices, bufs)):
                parts = [(self.grads[a], first) for a, first in where]
                p = _add_halves(parts, got[:, None], self.c_idx, f"rs_add2_{t}_{tag}")
                self.pair_sums.append(p.reshape(N_CHIPS, half, p.shape[-1]))
        elif self.stage == 1:
            for t, (q, p) in enumerate(zip(bufs, self.pair_sums)):
                q = lax.dynamic_update_slice_in_dim(q, lax.dynamic_slice_in_dim(p, self.chip, 1, axis=0), self.chip, axis=0)
                self.dests[t] = _add_slots(q, self.c_idx, self.dests[t], self.layer, f"rs_add4_{t}_{tag}")
        else:
            self.dests = list(bufs)
        self.stage += 1

    def finish(self):
        while self.stage < 3:
            hop, srcs, bufs = self.rider()
            self.landed(hop.call(srcs, bufs, f"rs_{hop.hop}_{self.layer}"))
        return self.dests


def _all_reduce_small(buf, name):
    R = buf.shape[0]

    def flipped(k, x, y, c):
        return ((1 - x) if k & 4 else x, (1 - y) if k & 2 else y, (1 - c) if k & 1 else c)

    def body(b_ref, o_ref, land, send, recv):
        x, y, c, _ = _place()
        me = 4 * x + 2 * y + c
        land[me] = b_ref[...]
        cps = []
        for k in range(1, N_DEV):
            peer = flipped(k, x, y, c)
            cps.append(pltpu.make_async_remote_copy(src_ref=b_ref, dst_ref=land.at[me], send_sem=send.at[k - 1],
                                                    recv_sem=recv.at[k - 1], device_id=peer, device_id_type=MESH))
        for cp in cps:
            cp.start()
        for k in range(1, N_DEV):
            peer = flipped(k, x, y, c)
            slot = 4 * peer[0] + 2 * peer[1] + peer[2]
            pltpu.make_async_remote_copy(src_ref=b_ref, dst_ref=land.at[slot], send_sem=send.at[k - 1],
                                         recv_sem=recv.at[k - 1], device_id=peer, device_id_type=MESH).wait_recv()
        for cp in cps:
            cp.wait_send()
        acc = land[0]
        for j in range(1, N_DEV):
            acc = acc + land[j]
        o_ref[...] = acc

    vm = pl.BlockSpec(memory_space=pltpu.VMEM)
    return pl.pallas_call(
        body, name=name, in_specs=[vm], out_specs=vm,
        out_shape=jax.ShapeDtypeStruct((R, LANES), F32),
        scratch_shapes=[pltpu.VMEM((N_DEV, R, LANES), F32), pltpu.SemaphoreType.DMA((N_DEV - 1,)),
                        pltpu.SemaphoreType.DMA((N_DEV - 1,))],
    )(buf)


def kernel(x, mem, w_in, a_rel_bias, b_gate_w, b_gate_b, b_norm_g, w_mem_kv, w_out, ln_g, ln_b, loss_target, m_w_in, m_a_rel_bias, m_b_gate_w, m_b_gate_b, m_b_norm_g, m_w_mem_kv, m_w_out, m_ln_g, m_ln_b, v_w_in, v_a_rel_bias, v_b_gate_w, v_b_gate_b, v_b_norm_g, v_w_mem_kv, v_w_out, v_ln_g, v_ln_b):
    L = w_in.shape[0]
    S = x.shape[1]
    cx, cy, cc = lax.axis_index("x"), lax.axis_index("y"), lax.axis_index("c")
    chip = 2 * cx + cy
    c_idx = jnp.reshape(cc, (1,)).astype(jnp.int32)

    n_in, r_kv, r_out = w_in.shape[2], w_mem_kv.shape[1], w_out.shape[1]
    shards = [w.astype(BF16).reshape(-1, w.shape[2]) for w in (w_in, w_mem_kv, w_out)]
    rows = [D_MODEL, r_kv, r_out]

    def landing(l):
        return [lax.dynamic_update_slice_in_dim(lax.empty((N_CHIPS, r, s.shape[1]), BF16),
                                                s[l * r:(l + 1) * r][None], chip, axis=0)
                for s, r in zip(shards, rows)]

    def assembled(bufs):
        return (_padded_from_shards([bufs[0][j] for j in range(N_CHIPS)]),
                bufs[1].reshape(D_MODEL, bufs[1].shape[2]), bufs[2].reshape(D_MODEL, D_MODEL))

    bufs0 = _WeightGather("chips", 0, rows).call(shards, landing(0), "gather_chips_0")
    weights = [assembled(_WeightGather("pair", 0, rows).call(shards, bufs0, "gather_pair_0"))]

    gw_cols = b_gate_w.shape[2]
    gw_slot = jnp.zeros((N_CHIPS, L, GATE_RANK, gw_cols), F32)
    gw_slot = lax.dynamic_update_slice(gw_slot, (0.5 * b_gate_w)[None], (chip, 0, 0, 0))
    gw_flat = gw_slot.reshape(-1)
    n_gw = gw_flat.shape[0]
    pad = (-n_gw) % (8 * LANES)
    gw_full = _all_reduce_small(jnp.pad(gw_flat, (0, pad)).reshape(-1, LANES), "gather_gate_w").reshape(-1)[:n_gw]
    gw_full = gw_full.reshape(N_CHIPS, L, GATE_RANK, gw_cols).transpose(1, 2, 0, 3).reshape(L, GATE_RANK, B_KEY_WIDTH)

    xs = x.reshape(S, D_MODEL)
    mem_b = mem.reshape(mem.shape[1], D_MODEL).astype(BF16)
    target = loss_target.reshape(S, D_MODEL)

    small_w = []
    for l in range(L):
        gw_l, gb_l = _pad_gate(gw_full[l], b_gate_b[l])
        small_w.append((_bias_by_offset(a_rel_bias[l]), gw_l, gb_l,
                        b_norm_g[l].reshape(1, LANES), ln_g[l].reshape(1, D_MODEL), ln_b[l].reshape(1, D_MODEL)))

    y, yb = xs, xs.astype(BF16)
    yt = _transpose(yb, "x_t")
    saved = []
    for l in range(L):
        if l + 1 < L:
            rider = (_WeightGather("chips", l + 1, rows), shards, landing(l + 1))
            y, yb, yt, sv, bufs = _layer_fwd(y, yb, yt, mem_b, *weights[l], *small_w[l], rider=rider)
            weights.append(assembled(_WeightGather("pair", l + 1, rows).call(shards, bufs, f"gather_pair_{l + 1}")))
        else:
            y, yb, yt, sv = _layer_fwd(y, yb, yt, mem_b, *weights[l], *small_w[l])
        saved.append(sv)
    layer_w = [(*weights[l], *small_w[l]) for l in range(L)]
    loss_part, dy = _loss_head(y, target)

    halves = [D_MODEL // 2, r_kv // 2, r_out // 2]
    dests = [lax.empty((L, 2, hf, w.shape[2]), F32) for hf, w in zip(halves, (w_in, w_mem_kv, w_out))]
    grads, reduce = [None] * L, None
    for l in reversed(range(L)):
        w_in_l, w_kv_l, w_out_l, u_l, gw_l, gb_l, gn_l, lg_l, lb_l = layer_w[l]
        dy, grads[l] = _layer_bwd(dy, saved[l], mem_b, w_in_l, w_out_l, u_l, gw_l, gb_l, gn_l, lg_l, reduce=reduce)
        if reduce is not None:
            dests = reduce.finish()
        arrays = [_chip_columns(grads[l][0], j, n_in) for j in range(N_CHIPS)] + [grads[l][5], grads[l][6]]
        slices = [(halves[0], [(j, 0) for j in range(N_CHIPS)]),
                  (halves[1], [(N_CHIPS, j * r_kv) for j in range(N_CHIPS)]),
                  (halves[2], [(N_CHIPS + 1, j * r_out) for j in range(N_CHIPS)])]
        reduce = _LayerReduce(l, arrays, slices, c_idx, chip, dests)
    r_w_in, r_w_kv, r_w_out = [d.reshape(L, 2 * d.shape[2], d.shape[3]) for d in reduce.finish()]
    grad_x = dy.reshape(x.shape)

    g_rel = jnp.stack([_bias_grad_from_offset(g[1]) for g in grads])
    g_gw = jnp.stack([_unpad_heads(g[2][:GATE_RANK]) for g in grads])
    g_gb = jnp.stack([_unpad_heads(g[3])[0] for g in grads])
    g_gn = jnp.stack([g[4][0] for g in grads])
    g_lg = jnp.stack([g[7][0] for g in grads])
    g_lb = jnp.stack([g[8][0] for g in grads])

    small = [g_rel, g_gw, g_gb, g_gn, g_lg, g_lb, loss_part]
    flat = jnp.concatenate([s.reshape(-1) for s in small])
    n_small = flat.shape[0]
    pad = (-n_small) % (8 * LANES)
    red = _all_reduce_small(jnp.pad(flat, (0, pad)).reshape(-1, LANES), "all_reduce_small").reshape(-1)
    outs, off = [], 0
    for s in small:
        outs.append(red[off:off + s.size].reshape(s.shape))
        off += s.size
    g_rel, g_gw, g_gb, g_gn, g_lg, g_lb, loss = outs
    loss = loss.reshape(())
    g_gw = lax.dynamic_slice_in_dim(g_gw.reshape(L, GATE_RANK, N_CHIPS, gw_cols), chip, 1, axis=2).reshape(L, GATE_RANK, gw_cols)

    g_list = [r_w_in, g_rel, g_gw, g_gb, g_gn, r_w_kv, r_w_out, g_lg, g_lb]
    w_list = [w_in, a_rel_bias, b_gate_w, b_gate_b, b_norm_g, w_mem_kv, w_out, ln_g, ln_b]
    m_list = [m_w_in, m_a_rel_bias, m_b_gate_w, m_b_gate_b, m_b_norm_g, m_w_mem_kv, m_w_out, m_ln_g, m_ln_b]
    v_list = [v_w_in, v_a_rel_bias, v_b_gate_w, v_b_gate_b, v_b_norm_g, v_w_mem_kv, v_w_out, v_ln_g, v_ln_b]
    names = ["w_in", "rel", "gate_w", "gate_b", "norm_g", "w_kv", "w_out", "ln_g", "ln_b"]
    to_cols = lambda a: jnp.transpose(a, (2, 0, 1))
    upd = [tuple(jnp.transpose(o, (1, 2, 0)) for o in
                 _adamw(to_cols(w_in), to_cols(r_w_in), to_cols(m_w_in), to_cols(v_w_in), "adamw_w_in"))]
    upd += [_adamw_nd(w, g, m, v, "adamw_" + n)
            for w, g, m, v, n in list(zip(w_list, g_list, m_list, v_list, names))[1:]]
    deltas = [u_[0] for u_ in upd]
    new_m = [u_[1] for u_ in upd]
    new_v = [u_[2] for u_ in upd]
    return (loss, grad_x, *g_list, *deltas, *new_m, *new_v)
```

```python
import functools

import numpy as np
import jax
import jax.numpy as jnp
from jax import lax
from jax.experimental import pallas as pl
from jax.experimental.pallas import tpu as pltpu

F32 = jnp.float32
BF16 = jnp.bfloat16
MESH = pl.DeviceIdType.MESH

D_MODEL = 2048
DEPTH = 4
CHUNK = 64
LEFT_CHUNKS = 8
MAX_REL = 128
N_REL = 2 * MAX_REL + 1
A_HEADS = 8
HEAD_DIM = 128
B_HEADS = 4
B_DK = 64
M_HEADS = 4
GATE_RANK = 16
GATE_TAU = 16.0
A_WIDTH = A_HEADS * HEAD_DIM
B_WIDTH = B_HEADS * HEAD_DIM
B_KEY_WIDTH = B_HEADS * B_DK
M_WIDTH = M_HEADS * HEAD_DIM
IN_WIDTH = 4 * A_WIDTH + 2 * B_KEY_WIDTH + 2 * B_WIDTH + GATE_RANK + 2 * M_WIDTH
ALPHA = (2.0 * DEPTH) ** 0.25
LN_EPS = 1e-5
RMS_EPS = 1e-6
NEG_INF = -1e30
ADAM_LR = 0.001
ADAM_B1 = 0.9
ADAM_B2 = 0.999
ADAM_EPS = 1e-08
ADAM_WD = 0.01
ADAM_STEP = 10

LANES = 128
VMEM_LIMIT = 56 * 1024 * 1024

C_A, C_B, C_M, C_LR = 0, 4096, 6144, 7168
HP = 7680
SEG_Q, SEG_K, SEG_V, SEG_Z = 0, 1, 2, 3
TQ = 512
CPB = TQ // CHUNK
N_CHIPS = 4
N_DEV = 8


def _params(sem, vmem=VMEM_LIMIT):
    return pltpu.CompilerParams(dimension_semantics=sem, vmem_limit_bytes=vmem)


def _dot(a, b):
    return jnp.dot(a, b, preferred_element_type=F32)


def _dot_nt(a, b):
    return lax.dot_general(a, b, (((1,), (1,)), ((), ())), preferred_element_type=F32)


def _dot_tn(a, b):
    return lax.dot_general(a, b, (((0,), (0,)), ((), ())), preferred_element_type=F32)


def _sigmoid(x):
    return 1.0 / (1.0 + jnp.exp(-x))


def _split3(x):
    hi = x.astype(BF16)
    r = x - hi.astype(F32)
    mid = r.astype(BF16)
    lo = (r - mid.astype(F32)).astype(BF16)
    return hi, mid, lo


def _dot3(m_bf, x):
    hi, mid, lo = _split3(x)
    return _dot(m_bf, hi) + _dot(m_bf, mid) + _dot(m_bf, lo)


def _matmul(a, b, *, mode, out_dtype, tm, tn, tk, name, add=None, add_scale=1.0, rider=None):
    if mode == "nn":
        (M, K), (K2, N) = a.shape, b.shape
        a_spec = pl.BlockSpec((tm, tk), lambda i, j, k: (i, k))
        b_spec = pl.BlockSpec((tk, tn), lambda i, j, k: (k, j))
        dot = _dot
    elif mode == "nt":
        (M, K), (N, K2) = a.shape, b.shape
        a_spec = pl.BlockSpec((tm, tk), lambda i, j, k: (i, k))
        b_spec = pl.BlockSpec((tn, tk), lambda i, j, k: (j, k))
        dot = _dot_nt
    else:
        (K, M), (K2, N) = a.shape, b.shape
        a_spec = pl.BlockSpec((tk, tm), lambda i, j, k: (k, i))
        b_spec = pl.BlockSpec((tk, tn), lambda i, j, k: (k, j))
        dot = _dot_tn
    assert K == K2 and M % tm == 0 and N % tn == 0 and K % tk == 0, (a.shape, b.shape, mode)
    nk = K // tk
    has_add = add is not None
    assert nk == 1 or out_dtype == F32
    grid = (M // tm, N // tn, nk)
    hop, srcs, bufs = rider if rider is not None else (None, [], [])
    n_in = 2 + has_add

    def body(*refs):
        a_ref, b_ref = refs[:2]
        add_ref = refs[2] if has_add else None
        src_refs = refs[n_in:n_in + len(srcs)]
        o_ref = refs[n_in + len(srcs) + len(bufs)]
        buf_refs = refs[n_in + len(srcs) + len(bufs) + 1:n_in + len(srcs) + 2 * len(bufs) + 1]
        sems = refs[n_in + len(srcs) + 2 * len(bufs) + 1:]
        i, j, k = pl.program_id(0), pl.program_id(1), pl.program_id(2)
        if hop is not None:
            @pl.when(jnp.logical_and(jnp.logical_and(i == 0, j == 0), k == 0))
            def _():
                hop.start(src_refs, buf_refs, *sems)

        part = dot(a_ref[...].astype(BF16), b_ref[...].astype(BF16))

        @pl.when(k == 0)
        def _():
            first = part + add_scale * add_ref[...] if has_add else part
            o_ref[...] = first.astype(out_dtype)

        if nk > 1:
            @pl.when(k > 0)
            def _():
                o_ref[...] += part

        if hop is not None:
            @pl.when(jnp.logical_and(jnp.logical_and(i == grid[0] - 1, j == grid[1] - 1), k == nk - 1))
            def _():
                hop.wait(src_refs, buf_refs, *sems)

    in_specs = [a_spec, b_spec]
    args = [a, b]
    if has_add:
        in_specs.append(pl.BlockSpec((tm, tn), lambda i, j, k: (i, j)))
        args.append(add)
    out = pl.pallas_call(
        body,
        name=name,
        grid=grid,
        in_specs=in_specs + [ANY] * (len(srcs) + len(bufs)),
        out_specs=[pl.BlockSpec((tm, tn), lambda i, j, k: (i, j))] + [ANY] * len(bufs),
        out_shape=[jax.ShapeDtypeStruct((M, N), out_dtype)] + [jax.ShapeDtypeStruct(x.shape, x.dtype) for x in bufs],
        input_output_aliases={n_in + len(srcs) + t: 1 + t for t in range(len(bufs))},
        scratch_shapes=hop.sems() if hop is not None else [],
        compiler_params=_params(("parallel", "parallel", "arbitrary") if hop is None
                                else ("arbitrary", "arbitrary", "arbitrary")),
    )(*args, *srcs, *bufs)
    return out[0] if hop is None else (out[0], list(out[1:]))


def _transpose(a, name):
    R, C = a.shape
    t = 512

    def body(a_ref, o_ref):
        o_ref[...] = a_ref[...].T

    return pl.pallas_call(
        body, name=name, grid=(R // t, C // t),
        in_specs=[pl.BlockSpec((t, t), lambda i, j: (i, j))],
        out_specs=pl.BlockSpec((t, t), lambda i, j: (j, i)),
        out_shape=jax.ShapeDtypeStruct((C, R), a.dtype),
        compiler_params=_params(("parallel", "parallel")),
    )(a)


def _chunk_of(rows):
    return lax.shift_right_logical(rows, CHUNK.bit_length() - 1)


A_HPS = 2
A_HW = A_HPS * LANES


def _band_bias(u_row, first):
    bias = pltpu.roll(jnp.broadcast_to(u_row, (TQ, 2 * TQ)), 0, 1, stride=1, stride_axis=0)
    qc = _chunk_of(lax.broadcasted_iota(jnp.int32, (TQ, 2 * TQ), 0))
    col = lax.broadcasted_iota(jnp.int32, (TQ, 2 * TQ), 1)
    kc = _chunk_of(jnp.bitwise_and(col, TQ - 1))
    ok = jnp.logical_or(jnp.logical_and(col < TQ, kc >= qc), jnp.logical_and(col >= TQ, kc <= qc))
    return jnp.where(ok, bias, NEG_INF) + jnp.where(col < TQ, first * NEG_INF, 0.0)


def _band_probs(q, kp, kc, bias):
    scale = HEAD_DIM ** -0.5
    sp = _dot_nt(q, kp) * scale + bias[:, :TQ]
    sc = _dot_nt(q, kc) * scale + bias[:, TQ:]
    m = jnp.maximum(jnp.max(sp, axis=1, keepdims=True), jnp.max(sc, axis=1, keepdims=True))
    pp = jnp.exp(sp - m)
    pc = jnp.exp(sc - m)
    inv = 1.0 / (jnp.sum(pp, axis=1, keepdims=True) + jnp.sum(pc, axis=1, keepdims=True))
    return pp, pc, inv


def _band_specs(nq):
    def col(seg, h):
        return C_A // A_HW + 4 * h + seg

    q_spec = pl.BlockSpec((TQ, A_HW), lambda h, i: (jnp.minimum(i, nq - 1), col(SEG_Q, h)))
    kp_spec = pl.BlockSpec((TQ, A_HW), lambda h, i: (jnp.clip(i - 1, 0, nq - 1), col(SEG_K, h)))
    kc_spec = pl.BlockSpec((TQ, A_HW), lambda h, i: (jnp.minimum(i, nq - 1), col(SEG_K, h)))
    vp_spec = pl.BlockSpec((TQ, A_HW), lambda h, i: (jnp.clip(i - 1, 0, nq - 1), col(SEG_V, h)))
    vc_spec = pl.BlockSpec((TQ, A_HW), lambda h, i: (jnp.minimum(i, nq - 1), col(SEG_V, h)))
    z_spec = pl.BlockSpec((TQ, A_HW), lambda h, i: (jnp.minimum(i, nq - 1), col(SEG_Z, h)))
    u_spec = pl.BlockSpec((A_HPS, 1, 2 * TQ), lambda h, i: (h, 0, 0))
    return q_spec, kp_spec, kc_spec, vp_spec, vc_spec, z_spec, u_spec


def _band_fwd(h, u):
    S = h.shape[0]
    nq = S // TQ

    def body(q_ref, kp_ref, kc_ref, vp_ref, vc_ref, z_ref, u_ref, y_ref, yt_ref, bias_scr):
        i = pl.program_id(1)

        @pl.when(i <= 1)
        def _():
            for hh in range(A_HPS):
                bias_scr[hh] = _band_bias(u_ref[hh], (i == 0).astype(F32))

        for hh in range(A_HPS):
            cs = slice(hh * LANES, (hh + 1) * LANES)
            pp, pc, inv = _band_probs(q_ref[:, cs], kp_ref[:, cs], kc_ref[:, cs], bias_scr[hh])
            o = (_dot(pp.astype(BF16), vp_ref[:, cs]) + _dot(pc.astype(BF16), vc_ref[:, cs])) * inv
            z = z_ref[:, cs].astype(F32)
            y = o * (z * _sigmoid(z))
            y_ref[:, cs] = y.astype(BF16)
            yt_ref[cs, :] = y.T.astype(BF16)

    specs = _band_specs(nq)
    return pl.pallas_call(
        body,
        name="band_fwd",
        grid=(A_HEADS // A_HPS, nq),
        in_specs=[specs[0], specs[1], specs[2], specs[3], specs[4], specs[5], specs[6]],
        out_specs=[pl.BlockSpec((TQ, A_HW), lambda h, i: (i, h)), pl.BlockSpec((A_HW, TQ), lambda h, i: (h, i))],
        out_shape=[jax.ShapeDtypeStruct((S, D_MODEL), BF16), jax.ShapeDtypeStruct((D_MODEL, S), BF16)],
        scratch_shapes=[pltpu.VMEM((A_HPS, TQ, 2 * TQ), F32)],
        compiler_params=_params(("parallel", "arbitrary")),
    )(h, h, h, h, h, h, u)


def _band_bwd(h, u, dycat):
    S = h.shape[0]
    nq = S // TQ
    scale = HEAD_DIM ** -0.5
    qs, ks, vs, zs = (slice(s * A_HW, (s + 1) * A_HW) for s in (SEG_Q, SEG_K, SEG_V, SEG_Z))

    def body(q_ref, kp_ref, kc_ref, vp_ref, vc_ref, z_ref, u_ref, dy_ref,
             dh_ref, du_ref, bias_scr, db_scr, ckt_scr, cvt_scr, cq_scr, cz_scr):
        i = pl.program_id(1)

        @pl.when(i <= 1)
        def _():
            for hh in range(A_HPS):
                bias_scr[hh] = _band_bias(u_ref[hh], (i == 0).astype(F32))

        @pl.when(i == 0)
        def _():
            db_scr[...] = jnp.zeros_like(db_scr)
            ckt_scr[...] = jnp.zeros_like(ckt_scr)
            cvt_scr[...] = jnp.zeros_like(cvt_scr)
            cq_scr[...] = jnp.zeros_like(cq_scr)
            cz_scr[...] = jnp.zeros_like(cz_scr)

        @pl.when(i < nq)
        def _():
            dh_ref[:, qs] = cq_scr[...]
            dh_ref[:, zs] = cz_scr[...]
            for hh in range(A_HPS):
                cs = slice(hh * LANES, (hh + 1) * LANES)
                q, kp, kc, vp, vc = q_ref[:, cs], kp_ref[:, cs], kc_ref[:, cs], vp_ref[:, cs], vc_ref[:, cs]
                pp, pc, inv = _band_probs(q, kp, kc, bias_scr[hh])
                pp, pc = pp * inv, pc * inv
                ppb, pcb = pp.astype(BF16), pc.astype(BF16)
                o = _dot(ppb, vp) + _dot(pcb, vc)
                z = z_ref[:, cs].astype(F32)
                sg = _sigmoid(z)
                dy = dy_ref[:, cs].astype(F32)
                do = dy * (z * sg)
                cz_scr[:, cs] = (dy * o * (sg * (1.0 + z * (1.0 - sg)))).astype(BF16)
                dob = do.astype(BF16)
                delta = jnp.sum(do * o, axis=1, keepdims=True)
                dsp = pp * (_dot_nt(dob, vp) - delta)
                dsc = pc * (_dot_nt(dob, vc) - delta)
                db_scr[hh, :, :TQ] += dsp
                db_scr[hh, :, TQ:] += dsc
                dspb, dscb = dsp.astype(BF16), dsc.astype(BF16)
                cq_scr[:, cs] = (scale * (_dot(dspb, kp) + _dot(dscb, kc))).astype(BF16)
                qt, dot_ = q.T, dob.T
                dh_ref[:, SEG_K * A_HW + hh * LANES:SEG_K * A_HW + (hh + 1) * LANES] = (
                    ckt_scr[cs, :] + scale * _dot(qt, dspb)).T.astype(BF16)
                dh_ref[:, SEG_V * A_HW + hh * LANES:SEG_V * A_HW + (hh + 1) * LANES] = (
                    cvt_scr[cs, :] + _dot(dot_, ppb)).T.astype(BF16)
                ckt_scr[cs, :] = scale * _dot(qt, dscb)
                cvt_scr[cs, :] = _dot(dot_, pcb)

        @pl.when(i == nq)
        def _():
            dh_ref[:, qs] = cq_scr[...]
            dh_ref[:, zs] = cz_scr[...]
            dh_ref[:, ks] = ckt_scr[...].T.astype(BF16)
            dh_ref[:, vs] = cvt_scr[...].T.astype(BF16)
            r0 = lax.broadcasted_iota(jnp.int32, (TQ, TQ), 0)
            r1 = lax.broadcasted_iota(jnp.int32, (TQ, TQ), 1)
            flip = (r0 + r1 == TQ - 1).astype(BF16)
            for hh in range(A_HPS):
                fl = _dot3(flip, db_scr[hh])
                rolled = pltpu.roll(fl, 0, 1, stride=1, stride_axis=0)
                du_ref[hh] = jnp.sum(rolled, axis=0, keepdims=True)

    specs = _band_specs(nq)
    dy_spec = pl.BlockSpec((TQ, A_HW), lambda h, i: (jnp.minimum(i, nq - 1), h))
    return pl.pallas_call(
        body,
        name="band_bwd",
        grid=(A_HEADS // A_HPS, nq + 1),
        in_specs=[specs[0], specs[1], specs[2], specs[3], specs[4], specs[5], specs[6], dy_spec],
        out_specs=[pl.BlockSpec((TQ, 4 * A_HW), lambda h, i: (jnp.maximum(i - 1, 0), C_A // (4 * A_HW) + h)),
                   pl.BlockSpec((A_HPS, 1, 2 * TQ), lambda h, i: (h, 0, 0))],
        out_shape=[jax.ShapeDtypeStruct((S, HP), BF16), jax.ShapeDtypeStruct((A_HEADS, 1, 2 * TQ), F32)],
        scratch_shapes=[pltpu.VMEM((A_HPS, TQ, 2 * TQ), F32), pltpu.VMEM((A_HPS, TQ, 2 * TQ), F32),
                        pltpu.VMEM((A_HW, TQ), F32), pltpu.VMEM((A_HW, TQ), F32),
                        pltpu.VMEM((TQ, A_HW), BF16), pltpu.VMEM((TQ, A_HW), BF16)],
        compiler_params=_params(("parallel", "arbitrary")),
    )(h, h, h, h, h, h, u, dycat)


def _bias_by_offset(table):
    far = jnp.broadcast_to(table[:, N_REL - 1:], (A_HEADS, TQ - MAX_REL))
    ramp = jnp.flip(table, axis=1)
    rest = jnp.broadcast_to(table[:, :1], (A_HEADS, 2 * TQ - CHUNK - (TQ + MAX_REL + 1)))
    wrap = jnp.broadcast_to(table[:, N_REL - 1:], (A_HEADS, CHUNK))
    return jnp.concatenate([far, ramp, rest, wrap], axis=1)[:, None, :]


def _bias_grad_from_offset(du):
    g = jnp.roll(du[:, 0, :], -(TQ - 1), axis=1)
    far = jnp.sum(g[:, :TQ - MAX_REL], axis=1) + jnp.sum(g[:, 2 * TQ - CHUNK:], axis=1)
    ramp = jnp.flip(g[:, TQ - MAX_REL:TQ + MAX_REL + 1], axis=1)
    return ramp.at[:, N_REL - 1].add(far)


def _mem_probs(q, mk):
    s = _dot_nt(q, mk) * (HEAD_DIM ** -0.5)
    p = jnp.exp(s - jnp.max(s, axis=1, keepdims=True))
    return p * (1.0 / jnp.sum(p, axis=1, keepdims=True))


def _mem_fwd(h, mkv, ycat, ycat_t):
    S = h.shape[0]
    nm = mkv.shape[0]
    c0 = (A_WIDTH + B_WIDTH) // LANES

    def body(q_ref, z_ref, mk_ref, mv_ref, yin_ref, ytin_ref, y_ref, yt_ref):
        del yin_ref, ytin_ref
        p = _mem_probs(q_ref[...], mk_ref[...])
        o = _dot(p.astype(BF16), mv_ref[...])
        z = z_ref[...].astype(F32)
        y = o * (z * _sigmoid(z))
        y_ref[...] = y.astype(BF16)
        yt_ref[...] = y.T.astype(BF16)

    return pl.pallas_call(
        body,
        name="mem_fwd",
        grid=(M_HEADS, S // TQ),
        in_specs=[pl.BlockSpec((TQ, LANES), lambda h, i: (i, C_M // LANES + 2 * h)),
                  pl.BlockSpec((TQ, LANES), lambda h, i: (i, C_M // LANES + 2 * h + 1)),
                  pl.BlockSpec((nm, LANES), lambda h, i: (0, h)),
                  pl.BlockSpec((nm, LANES), lambda h, i: (0, M_HEADS + h)), ANY, ANY],
        out_specs=[pl.BlockSpec((TQ, LANES), lambda h, i: (i, c0 + h)), pl.BlockSpec((LANES, TQ), lambda h, i: (c0 + h, i))],
        out_shape=[jax.ShapeDtypeStruct(ycat.shape, BF16), jax.ShapeDtypeStruct(ycat_t.shape, BF16)],
        input_output_aliases={4: 0, 5: 1},
        compiler_params=_params(("parallel", "arbitrary")),
    )(h, h, mkv, mkv, ycat, ycat_t)


def _mem_bwd(h, mkv, dycat, dh):
    S = h.shape[0]
    nm = mkv.shape[0]
    scale = HEAD_DIM ** -0.5

    def body(q_ref, z_ref, mk_ref, mv_ref, dy_ref, dhin_ref, dh_ref, dmk_ref, dmv_ref):
        del dhin_ref
        i = pl.program_id(1)
        q, mk, mv = q_ref[...], mk_ref[...], mv_ref[...]
        p = _mem_probs(q, mk)
        pb = p.astype(BF16)
        o = _dot(pb, mv)
        z = z_ref[...].astype(F32)
        sg = _sigmoid(z)
        dy = dy_ref[...].astype(F32)
        do = dy * (z * sg)
        dh_ref[:, LANES:] = (dy * o * (sg * (1.0 + z * (1.0 - sg)))).astype(BF16)
        dob = do.astype(BF16)
        ds = p * (_dot_nt(dob, mv) - jnp.sum(do * o, axis=1, keepdims=True))
        dsb = ds.astype(BF16)
        dh_ref[:, :LANES] = (scale * _dot(dsb, mk)).astype(BF16)
        dmk = scale * _dot_tn(dsb, q)
        dmv = _dot_tn(pb, dob)

        @pl.when(i == 0)
        def _():
            dmk_ref[...] = dmk
            dmv_ref[...] = dmv

        @pl.when(i > 0)
        def _():
            dmk_ref[...] += dmk
            dmv_ref[...] += dmv

    dh, dmk, dmv = pl.pallas_call(
        body,
        name="mem_bwd",
        grid=(M_HEADS, S // TQ),
        in_specs=[pl.BlockSpec((TQ, LANES), lambda h, i: (i, C_M // LANES + 2 * h)),
                  pl.BlockSpec((TQ, LANES), lambda h, i: (i, C_M // LANES + 2 * h + 1)),
                  pl.BlockSpec((nm, LANES), lambda h, i: (0, h)),
                  pl.BlockSpec((nm, LANES), lambda h, i: (0, M_HEADS + h)),
                  pl.BlockSpec((TQ, LANES), lambda h, i: (i, (A_WIDTH + B_WIDTH) // LANES + h)), ANY],
        out_specs=[pl.BlockSpec((TQ, 2 * LANES), lambda h, i: (i, C_M // (2 * LANES) + h)),
                   pl.BlockSpec((nm, LANES), lambda h, i: (0, h)),
                   pl.BlockSpec((nm, LANES), lambda h, i: (0, h))],
        out_shape=[jax.ShapeDtypeStruct(dh.shape, BF16),
                   jax.ShapeDtypeStruct((nm, M_WIDTH), F32), jax.ShapeDtypeStruct((nm, M_WIDTH), F32)],
        input_output_aliases={5: 0},
        compiler_params=_params(("parallel", "arbitrary")),
    )(h, h, mkv, mkv, dycat, dh)
    return dh, jnp.concatenate([dmk, dmv], axis=1)


def _chunk_masks():
    r = lax.broadcasted_iota(jnp.int32, (TQ, TQ), 0)
    c = lax.broadcasted_iota(jnp.int32, (TQ, TQ), 1)
    same = _chunk_of(r) == _chunk_of(c)
    return jnp.logical_and(same, c <= r), jnp.logical_and(same, c > r)


def _gla_gates(lr, gw, gb):
    logit = _dot(lr, gw) + gb
    sg = _sigmoid(logit)
    g = (jnp.minimum(logit, 0.0) - jnp.log(1.0 + jnp.exp(-jnp.abs(logit)))) * (1.0 / GATE_TAU)
    lo, _ = _chunk_masks()
    return sg, _dot3(lo.astype(BF16), g)


def _gla_factors(q, k, b):
    eb = jnp.exp(b)
    enb = jnp.exp(-b)
    return eb, enb, q * eb, q * enb, k * eb, k * enb


def _gla_intra(qp, qn, kp, kn):
    lo, up = _chunk_masks()
    return (jnp.where(lo, _dot_nt(qp.astype(BF16), kn.astype(BF16)), 0.0)
            + jnp.where(up, _dot_nt(qn.astype(BF16), kp.astype(BF16)), 0.0))


B_HPS = 2
B_HW = B_HPS * LANES


def _gla_specs(nb, rev):
    blk = (lambda i: nb - 1 - i) if rev else (lambda i: i)
    qkvz_spec = pl.BlockSpec((TQ, 4 * B_HW), lambda i, p: (blk(i), C_B // (4 * B_HW) + p))
    lr_spec = pl.BlockSpec((TQ, LANES), lambda i, p: (blk(i), C_LR // LANES))
    gw_spec = pl.BlockSpec((LANES, B_HW), lambda i, p: (0, p))
    gb_spec = pl.BlockSpec((1, B_HW), lambda i, p: (0, p))
    gn_spec = pl.BlockSpec((1, LANES), lambda i, p: (0, 0))
    return qkvz_spec, lr_spec, gw_spec, gb_spec, gn_spec, blk


def _head_cols(hh, seg):
    return slice((4 * hh + seg) * LANES, (4 * hh + seg + 1) * LANES)


def _gla_fwd(h, gw, gb, gn, ycat, ycat_t):
    S = h.shape[0]
    nb = S // TQ
    c0 = A_WIDTH // LANES

    def body(qkvz_ref, lr_ref, gw_ref, gb_ref, gn_ref, yin_ref, ytin_ref,
             y_ref, yt_ref, o_ref, st_ref, st_scr):
        del yin_ref, ytin_ref
        i, p = pl.program_id(0), pl.program_id(1)
        for hh in range(B_HPS):
            hd = B_HPS * p + hh
            lane = slice(hh * LANES, (hh + 1) * LANES)

            @pl.when(i == 0)
            def _():
                st_scr[hd] = jnp.zeros((LANES, LANES), F32)

            q = qkvz_ref[:, _head_cols(hh, SEG_Q)].astype(F32) * (B_DK ** -0.5)
            k = qkvz_ref[:, _head_cols(hh, SEG_K)].astype(F32)
            v = qkvz_ref[:, _head_cols(hh, SEG_V)]
            _, b = _gla_gates(lr_ref[...], gw_ref[:, lane], gb_ref[:, lane])
            _, _, qp, qn, kp, kn = _gla_factors(q, k, b)
            o_intra = _dot(_gla_intra(qp, qn, kp, kn).astype(BF16), v)
            qpb, knb = qp.astype(BF16), kn.astype(BF16)
            st = st_scr[hd]
            outs = []
            for c in range(CPB):
                rows = slice(c * CHUNK, (c + 1) * CHUNK)
                st_ref[hh, c] = st
                outs.append(_dot_nt(qpb[rows], st.astype(BF16)))
                e_last = jnp.exp(b[(c + 1) * CHUNK - 1:(c + 1) * CHUNK, :])
                st = (st + _dot_tn(v[rows], knb[rows])) * e_last
            st_scr[hd] = st
            o = o_intra + jnp.concatenate(outs, axis=0)
            o_ref[:, lane] = o
            r = lax.rsqrt(jnp.mean(o * o, axis=1, keepdims=True) + RMS_EPS)
            z = qkvz_ref[:, _head_cols(hh, SEG_Z)].astype(F32)
            y = o * r * gn_ref[...] * (z * _sigmoid(z))
            y_ref[:, lane] = y.astype(BF16)
            yt_ref[lane, :] = y.T.astype(BF16)

    qkvz_s, lr_s, gw_s, gb_s, gn_s, _ = _gla_specs(nb, False)
    return pl.pallas_call(
        body,
        name="gla_fwd",
        grid=(nb, B_HEADS // B_HPS),
        in_specs=[qkvz_s, lr_s, gw_s, gb_s, gn_s, ANY, ANY],
        out_specs=[pl.BlockSpec((TQ, B_HW), lambda i, p: (i, c0 // B_HPS + p)),
                   pl.BlockSpec((B_HW, TQ), lambda i, p: (c0 // B_HPS + p, i)),
                   pl.BlockSpec((TQ, B_HW), lambda i, p: (i, p)),
                   pl.BlockSpec((B_HPS, CPB, LANES, LANES), lambda i, p: (p, i, 0, 0))],
        out_shape=[jax.ShapeDtypeStruct(ycat.shape, BF16), jax.ShapeDtypeStruct(ycat_t.shape, BF16),
                   jax.ShapeDtypeStruct((S, B_WIDTH), F32),
                   jax.ShapeDtypeStruct((B_HEADS, S // CHUNK, LANES, LANES), F32)],
        input_output_aliases={5: 0, 6: 1},
        scratch_shapes=[pltpu.VMEM((B_HEADS, LANES, LANES), F32)],
        compiler_params=_params(("arbitrary", "arbitrary")),
    )(h, h, gw, gb, gn, ycat, ycat_t)


def _gla_bwd(h, gw, gb, gn, o_pre, states, dycat, dh):
    S = h.shape[0]
    nb = S // TQ

    def body(qkvz_ref, lr_ref, gw_ref, gb_ref, gn_ref, o_ref, st_ref, dy_ref, dhin_ref,
             dh_ref, dlr_ref, dgw_ref, dgb_ref, dgn_ref,
             dst_scr, dgw_scr, dgb_scr, dgn_scr):
        del dhin_ref
        i, p = pl.program_id(0), pl.program_id(1)

        @pl.when(jnp.logical_and(i == 0, p == 0))
        def _():
            dgn_scr[...] = jnp.zeros_like(dgn_scr)

        dlr_heads = [one_head(hh, i, p, qkvz_ref, lr_ref, gw_ref, gb_ref, gn_ref, o_ref, st_ref, dy_ref,
                              dh_ref, dgw_ref, dgb_ref, dst_scr, dgw_scr, dgb_scr, dgn_scr) for hh in range(B_HPS)]
        dlr = dlr_heads[0]
        for more in dlr_heads[1:]:
            dlr = dlr + more

        @pl.when(p == 0)
        def _():
            dlr_ref[...] = dlr

        @pl.when(p > 0)
        def _():
            dlr_ref[...] += dlr

        @pl.when(i == nb - 1)
        def _():
            dgn_ref[...] = dgn_scr[...]

    def one_head(hh, i, p, qkvz_ref, lr_ref, gw_ref, gb_ref, gn_ref, o_ref, st_ref, dy_ref,
                 dh_ref, dgw_ref, dgb_ref, dst_scr, dgw_scr, dgb_scr, dgn_scr):
        hd = B_HPS * p + hh
        lane = slice(hh * LANES, (hh + 1) * LANES)

        @pl.when(i == 0)
        def _():
            dst_scr[hd] = jnp.zeros((LANES, LANES), F32)
            dgw_scr[hd] = jnp.zeros((LANES, LANES), F32)
            dgb_scr[hd] = jnp.zeros((1, LANES), F32)

        q = qkvz_ref[:, _head_cols(hh, SEG_Q)].astype(F32) * (B_DK ** -0.5)
        k = qkvz_ref[:, _head_cols(hh, SEG_K)].astype(F32)
        v = qkvz_ref[:, _head_cols(hh, SEG_V)]
        lr, gwv = lr_ref[...], gw_ref[:, lane]
        sg, b = _gla_gates(lr, gwv, gb_ref[:, lane])
        eb, enb, qp, qn, kp, kn = _gla_factors(q, k, b)
        a = _gla_intra(qp, qn, kp, kn)
        qpb, qnb, kpb, knb = qp.astype(BF16), qn.astype(BF16), kp.astype(BF16), kn.astype(BF16)

        o = o_ref[:, lane]
        gn = gn_ref[...]
        r = lax.rsqrt(jnp.mean(o * o, axis=1, keepdims=True) + RMS_EPS)
        z = qkvz_ref[:, _head_cols(hh, SEG_Z)].astype(F32)
        sz = _sigmoid(z)
        dy = dy_ref[:, lane].astype(F32)
        d_on = dy * (z * sz)
        dh_ref[:, _head_cols(hh, SEG_Z)] = (dy * (o * r * gn) * (sz * (1.0 + z * (1.0 - sz)))).astype(BF16)
        dgn_scr[...] += jnp.sum(d_on * o * r, axis=0, keepdims=True)
        t = d_on * gn
        do = r * t - o * (r * r * r) * jnp.mean(t * o, axis=1, keepdims=True)
        dob = do.astype(BF16)

        lo, up = _chunk_masks()
        da = _dot_nt(dob, v)
        dalo = jnp.where(lo, da, 0.0).astype(BF16)
        daup = jnp.where(up, da, 0.0).astype(BF16)
        dqp = _dot(dalo, knb)
        dkn = _dot_tn(dalo, qpb)
        dqn = _dot(daup, kpb)
        dkp = _dot_tn(daup, qnb)
        dv = _dot_tn(a.astype(BF16), dob)

        dst = dst_scr[hd]
        dqp_c, dkn_c, dv_c, dbl_c = [None] * CPB, [None] * CPB, [None] * CPB, [None] * CPB
        for c in reversed(range(CPB)):
            rows = slice(c * CHUNK, (c + 1) * CHUNK)
            st = st_ref[hh, c]
            e_last = jnp.exp(b[(c + 1) * CHUNK - 1:(c + 1) * CHUNK, :])
            if c == CPB - 1:
                st_next = (st + _dot_tn(v[rows], knb[rows])) * e_last
            else:
                st_next = st_ref[hh, c + 1]
            dbl_c[c] = jnp.sum(dst * st_next, axis=0, keepdims=True)
            dtt = (dst * e_last).astype(BF16)
            dv_c[c] = _dot_nt(knb[rows], dtt)
            dkn_c[c] = _dot(v[rows], dtt)
            dqp_c[c] = _dot(dob[rows], st.astype(BF16))
            dst = _dot_tn(dob[rows], qpb[rows]) + dst * e_last
        dst_scr[hd] = dst
        dqp = dqp + jnp.concatenate(dqp_c, axis=0)
        dkn = dkn + jnp.concatenate(dkn_c, axis=0)
        dv = dv + jnp.concatenate(dv_c, axis=0)
        dh_ref[:, _head_cols(hh, SEG_V)] = dv.astype(BF16)
        dh_ref[:, _head_cols(hh, SEG_Q)] = ((dqp * eb + dqn * enb) * (B_DK ** -0.5)).astype(BF16)
        dh_ref[:, _head_cols(hh, SEG_K)] = (dkp * eb + dkn * enb).astype(BF16)

        last = jnp.bitwise_and(lax.broadcasted_iota(jnp.int32, (TQ, 1), 0), CHUNK - 1) == CHUNK - 1
        dbl = jnp.concatenate([jnp.broadcast_to(x, (CHUNK, LANES)) for x in dbl_c], axis=0)
        db = dqp * qp - dqn * qn + dkp * kp - dkn * kn + jnp.where(last, dbl, 0.0)
        r0 = lax.broadcasted_iota(jnp.int32, (TQ, TQ), 0)
        r1 = lax.broadcasted_iota(jnp.int32, (TQ, TQ), 1)
        upper = jnp.logical_and(_chunk_of(r0) == _chunk_of(r1), r1 >= r0).astype(BF16)
        dlogit = _dot3(upper, db) * (1.0 / GATE_TAU) * (1.0 - sg)
        dlb = dlogit.astype(BF16)
        dgw_scr[hd] += _dot_tn(lr, dlb)
        dgb_scr[hd] += jnp.sum(dlogit, axis=0, keepdims=True)

        @pl.when(i == nb - 1)
        def _():
            dgw_ref[:, lane] = dgw_scr[hd]
            dgb_ref[:, lane] = dgb_scr[hd]

        return _dot_nt(dlb, gwv)

    qkvz_s, lr_s, gw_s, gb_s, gn_s, blk = _gla_specs(nb, True)
    row = pl.BlockSpec((TQ, B_HW), lambda i, p: (blk(i), p))
    dy_spec = pl.BlockSpec((TQ, B_HW), lambda i, p: (blk(i), A_WIDTH // B_HW + p))
    st_spec = pl.BlockSpec((B_HPS, CPB, LANES, LANES), lambda i, p: (p, blk(i), 0, 0))
    return pl.pallas_call(
        body,
        name="gla_bwd",
        grid=(nb, B_HEADS // B_HPS),
        in_specs=[qkvz_s, lr_s, gw_s, gb_s, gn_s, row, st_spec, dy_spec, ANY],
        out_specs=[pl.BlockSpec((TQ, 4 * B_HW), lambda i, p: (blk(i), C_B // (4 * B_HW) + p)),
                   pl.BlockSpec((TQ, LANES), lambda i, p: (blk(i), 0)),
                   pl.BlockSpec((LANES, B_HW), lambda i, p: (0, jnp.where(i == nb - 1, p, 0))),
                   pl.BlockSpec((1, B_HW), lambda i, p: (0, jnp.where(i == nb - 1, p, 0))),
                   pl.BlockSpec((1, LANES), lambda i, p: (0, 0))],
        out_shape=[jax.ShapeDtypeStruct(dh.shape, BF16),
                   jax.ShapeDtypeStruct((S, LANES), F32),
                   jax.ShapeDtypeStruct((LANES, B_HEADS * LANES), F32),
                   jax.ShapeDtypeStruct((1, B_HEADS * LANES), F32),
                   jax.ShapeDtypeStruct((1, LANES), F32)],
        input_output_aliases={8: 0},
        scratch_shapes=[pltpu.VMEM((B_HEADS, LANES, LANES), F32), pltpu.VMEM((B_HEADS, LANES, LANES), F32),
                        pltpu.VMEM((B_HEADS, 1, LANES), F32), pltpu.VMEM((1, LANES), F32)],
        compiler_params=_params(("arbitrary", "arbitrary")),
    )(h, h, gw, gb, gn, o_pre, states, dycat, dh)


def _lr_fill(dlr, dh):
    S = dlr.shape[0]
    w = HP - C_LR

    def body(dlr_ref, dhin_ref, dh_ref):
        del dhin_ref
        dh_ref[:, :LANES] = dlr_ref[...].astype(BF16)
        dh_ref[:, LANES:] = jnp.zeros((TQ, w - LANES), BF16)

    return pl.pallas_call(
        body, name="lr_fill", grid=(S // TQ,),
        in_specs=[pl.BlockSpec((TQ, LANES), lambda i: (i, 0)), ANY],
        out_specs=pl.BlockSpec((TQ, w), lambda i: (i, C_LR // w)),
        out_shape=jax.ShapeDtypeStruct(dh.shape, BF16),
        input_output_aliases={1: 0},
        compiler_params=_params(("parallel",)),
    )(dlr, dh)


LN_ROWS = 256


def _outproj_ln(ycat, w_out, x, g, b):
    S = x.shape[0]

    def body(yc_ref, w_ref, x_ref, g_ref, b_ref, y_ref, yb_ref, yt_ref, xh_ref, rs_ref):
        u = ALPHA * x_ref[...] + _dot(yc_ref[...], w_ref[...])
        mu = jnp.mean(u, axis=1, keepdims=True)
        d = u - mu
        rstd = lax.rsqrt(jnp.mean(d * d, axis=1, keepdims=True) + LN_EPS)
        xh = d * rstd
        y = xh * g_ref[...] + b_ref[...]
        y_ref[...] = y
        yb_ref[...] = y.astype(BF16)
        yt_ref[...] = y.T.astype(BF16)
        xh_ref[...] = xh
        rs_ref[...] = rstd

    row = lambda w: pl.BlockSpec((LN_ROWS, w), lambda i: (i, 0))
    vec = pl.BlockSpec((1, D_MODEL), lambda i: (0, 0))
    return pl.pallas_call(
        body,
        name="outproj_ln",
        grid=(S // LN_ROWS,),
        in_specs=[row(D_MODEL), pl.BlockSpec((D_MODEL, D_MODEL), lambda i: (0, 0)), row(D_MODEL), vec, vec],
        out_specs=[row(D_MODEL), row(D_MODEL), pl.BlockSpec((D_MODEL, LN_ROWS), lambda i: (0, i)), row(D_MODEL), row(1)],
        out_shape=[jax.ShapeDtypeStruct((S, D_MODEL), F32), jax.ShapeDtypeStruct((S, D_MODEL), BF16),
                   jax.ShapeDtypeStruct((D_MODEL, S), BF16),
                   jax.ShapeDtypeStruct((S, D_MODEL), F32), jax.ShapeDtypeStruct((S, 1), F32)],
        compiler_params=_params(("parallel",)),
    )(ycat, w_out, x, g, b)


def _ln_bwd(dy, xhat, rstd, g):
    S = dy.shape[0]

    def body(dy_ref, xh_ref, rs_ref, g_ref, du_ref, dub_ref, dg_ref, db_ref):
        i = pl.program_id(0)
        dy_, xh = dy_ref[...], xh_ref[...]
        dyg = dy_ * g_ref[...]
        m1 = jnp.mean(dyg, axis=1, keepdims=True)
        m2 = jnp.mean(dyg * xh, axis=1, keepdims=True)
        du = rs_ref[...] * (dyg - m1 - xh * m2)
        du_ref[...] = du
        dub_ref[...] = du.astype(BF16)
        dg = jnp.sum(dy_ * xh, axis=0, keepdims=True)
        db = jnp.sum(dy_, axis=0, keepdims=True)

        @pl.when(i == 0)
        def _():
            dg_ref[...] = dg
            db_ref[...] = db

        @pl.when(i > 0)
        def _():
            dg_ref[...] += dg
            db_ref[...] += db

    row = lambda w: pl.BlockSpec((TQ, w), lambda i: (i, 0))
    vec = pl.BlockSpec((1, D_MODEL), lambda i: (0, 0))
    return pl.pallas_call(
        body,
        name="ln_bwd",
        grid=(S // TQ,),
        in_specs=[row(D_MODEL), row(D_MODEL), row(1), vec],
        out_specs=[row(D_MODEL), row(D_MODEL), vec, vec],
        out_shape=[jax.ShapeDtypeStruct((S, D_MODEL), F32), jax.ShapeDtypeStruct((S, D_MODEL), BF16),
                   jax.ShapeDtypeStruct((1, D_MODEL), F32), jax.ShapeDtypeStruct((1, D_MODEL), F32)],
        compiler_params=_params(("arbitrary",)),
    )(dy, xhat, rstd, g)


def _loss_head(y, target):
    S = y.shape[0]

    def body(y_ref, t_ref, l_ref, dy_ref):
        i = pl.program_id(0)
        err = y_ref[...] - t_ref[...]
        dy_ref[...] = err * (1.0 / D_MODEL)
        part = (0.5 / D_MODEL) * jnp.sum(jnp.sum(err * err, axis=1, keepdims=True), axis=0, keepdims=True)

        @pl.when(i == 0)
        def _():
            l_ref[...] = part

        @pl.when(i > 0)
        def _():
            l_ref[...] += part

    row = pl.BlockSpec((TQ, D_MODEL), lambda i: (i, 0))
    return pl.pallas_call(
        body,
        name="loss_head",
        grid=(S // TQ,),
        in_specs=[row, row],
        out_specs=[pl.BlockSpec((1, 1), lambda i: (0, 0)), row],
        out_shape=[jax.ShapeDtypeStruct((1, 1), F32), jax.ShapeDtypeStruct((S, D_MODEL), F32)],
        compiler_params=_params(("arbitrary",)),
    )(y, target)


def _pad_gate(gate_w, gate_b):
    gw = gate_w.reshape(GATE_RANK, B_HEADS, B_DK)
    gw = jnp.pad(gw, ((0, LANES - GATE_RANK), (0, 0), (0, LANES - B_DK))).reshape(LANES, B_HEADS * LANES)
    gb = jnp.pad(gate_b.reshape(B_HEADS, B_DK), ((0, 0), (0, LANES - B_DK))).reshape(1, B_HEADS * LANES)
    return gw.astype(BF16), gb.astype(F32)


def _layer_fwd(x, xb, xt, mem_b, w_in, w_kv, w_out, u, gw, gb, gn, ln_g, ln_b, rider=None):
    h = _matmul(xb, w_in, mode="nn", out_dtype=BF16, tm=1024, tn=768, tk=D_MODEL, name="in_proj", rider=rider)
    if rider is not None:
        h, rode = h
    mkv = _matmul(mem_b, w_kv, mode="nn", out_dtype=BF16, tm=mem_b.shape[0], tn=1024, tk=D_MODEL, name="mem_kv")
    ycat, ycat_t = _band_fwd(h, u)
    ycat, ycat_t, o_pre, states = _gla_fwd(h, gw, gb, gn, ycat, ycat_t)
    ycat, ycat_t = _mem_fwd(h, mkv, ycat, ycat_t)
    y, ybf, yt, xhat, rstd = _outproj_ln(ycat, w_out, x, ln_g, ln_b)
    saved = (xt, h, mkv, ycat_t, o_pre, states, xhat, rstd)
    return (y, ybf, yt, saved) if rider is None else (y, ybf, yt, saved, rode)


def _layer_bwd(dy, saved, mem_b, w_in, w_out, u, gw, gb, gn, ln_g, reduce=None, own_reduce=None):
    xt, h, mkv, ycat_t, o_pre, states, xhat, rstd = saved

    def riding(**kw):
        if reduce is None:
            return _matmul(**kw)
        out, bufs = _matmul(rider=reduce.rider(), **kw)
        reduce.landed(bufs)
        return out

    du, dub, d_ln_g, d_ln_b = _ln_bwd(dy, xhat, rstd, ln_g)
    dycat = riding(a=dub, b=w_out, mode="nt", out_dtype=BF16, tm=1024, tn=1024, tk=D_MODEL, name="dycat")
    d_w_out = _matmul(ycat_t, dub, mode="nn", out_dtype=F32, tm=1024, tn=1024, tk=2048, name="d_w_out")
    dh, d_u = _band_bwd(h, u, dycat)
    dh, dlr, dgw, dgb, dgn = _gla_bwd(h, gw, gb, gn, o_pre, states, dycat, dh)
    dh, dmkv = _mem_bwd(h, mkv, dycat, dh)
    dh = _lr_fill(dlr, dh)
    d_w_kv = _matmul(mem_b, dmkv, mode="tn", out_dtype=F32, tm=1024, tn=1024, tk=mem_b.shape[0], name="d_w_kv")
    dx_args = dict(a=dh, b=w_in, mode="nt", out_dtype=F32, tm=1024, tn=1024, tk=2560, name="dx", add=du, add_scale=ALPHA)
    dw_args = dict(a=xt, b=dh, mode="nn", out_dtype=F32, tm=1024, tn=768, tk=2048, name="d_w_in")
    if own_reduce is None:
        dx = riding(**dx_args)
        d_w_in = riding(**dw_args)
        return dx, (d_w_in, d_u, dgw, dgb, dgn, d_w_kv, d_w_out, d_ln_g, d_ln_b)
    d_w_in = riding(**dw_args)
    grads = (d_w_in, d_u, dgw, dgb, dgn, d_w_kv, d_w_out, d_ln_g, d_ln_b)
    own = own_reduce(grads, reduce.finish() if reduce is not None else None)
    own.step()
    dx, bufs = _matmul(rider=own.rider(), **dx_args)
    own.landed(bufs)
    return dx, grads, own


def _unpad_heads(w):
    r = w.shape[0]
    return w.reshape(r, B_HEADS, LANES)[:, :, :B_DK].reshape(r, B_KEY_WIDTH)


def _padded_col_of():
    col, o = np.zeros(IN_WIDTH, np.int64), 0
    for seg in (SEG_Q, SEG_K, SEG_V, SEG_Z):
        for hd in range(A_HEADS):
            col[o:o + LANES] = C_A + (hd // A_HPS) * 4 * A_HW + seg * A_HW + (hd % A_HPS) * LANES + np.arange(LANES)
            o += LANES
    for seg, width in ((SEG_Q, B_DK), (SEG_K, B_DK), (SEG_V, LANES), (SEG_Z, LANES)):
        for hd in range(B_HEADS):
            col[o:o + width] = C_B + hd * 4 * LANES + seg * LANES + np.arange(width)
            o += width
    col[o:o + GATE_RANK] = C_LR + np.arange(GATE_RANK)
    o += GATE_RANK
    for seg in (0, 1):
        for hd in range(M_HEADS):
            col[o:o + LANES] = C_M + hd * 2 * LANES + seg * LANES + np.arange(LANES)
            o += LANES
    assert o == IN_WIDTH
    return col


def _runs(idx):
    out, start = [], 0
    for k in range(1, len(idx) + 1):
        if k == len(idx) or idx[k] != idx[k - 1] + 1:
            out.append((int(idx[start]), k - start))
            start = k
    return out


def _chip_columns(g, j, n):
    runs = _runs(_padded_col_of()[j * n:(j + 1) * n])
    return jnp.concatenate([g[:, a:a + ln] for a, ln in runs], axis=1)


def _padded_from_shards(shards):
    n = shards[0].shape[1]
    src = np.full(HP, -1, np.int64)
    src[_padded_col_of()] = np.arange(IN_WIDTH)
    parts, k = [], 0
    while k < HP:
        e = k + 1
        if src[k] < 0:
            while e < HP and src[e] < 0:
                e += 1
            parts.append(jnp.zeros((shards[0].shape[0], e - k), shards[0].dtype))
        else:
            while e < HP and src[e] == src[e - 1] + 1 and src[e] // n == src[k] // n:
                e += 1
            parts.append(shards[src[k] // n][:, src[k] % n:src[k] % n + e - k])
        k = e
    return jnp.concatenate(parts, axis=1)


ADAMW_BLOCK_BYTES = 1 << 20


def _adamw(w, g, m, v, name):
    L, R, C = w.shape
    tl, tr = 1, R
    if R * C * 4 <= ADAMW_BLOCK_BYTES:
        tl = max(d for d in range(1, L + 1) if L % d == 0 and d * R * C * 4 <= ADAMW_BLOCK_BYTES)
    else:
        for cand in (256, 128, 64, 32, 16, 8):
            if R % cand == 0 and R > cand:
                tr = cand
                break

    def body(w_ref, g_ref, m_ref, v_ref, d_ref, nm_ref, nv_ref):
        g_ = g_ref[...]
        nm = ADAM_B1 * m_ref[...] + (1.0 - ADAM_B1) * g_
        nv = ADAM_B2 * v_ref[...] + (1.0 - ADAM_B2) * (g_ * g_)
        m_hat = nm / (1.0 - ADAM_B1 ** ADAM_STEP)
        v_hat = nv / (1.0 - ADAM_B2 ** ADAM_STEP)
        d_ref[...] = -ADAM_LR * (m_hat / (jnp.sqrt(v_hat) + ADAM_EPS) + ADAM_WD * w_ref[...])
        nm_ref[...] = nm
        nv_ref[...] = nv

    spec = pl.BlockSpec((tl, tr, C), lambda l, i: (l, i, 0))
    sd = jax.ShapeDtypeStruct((L, R, C), F32)
    return pl.pallas_call(
        body, name=name, grid=(L // tl, R // tr), in_specs=[spec] * 4, out_specs=[spec] * 3, out_shape=[sd] * 3,
        compiler_params=_params(("parallel", "parallel")),
    )(w, g, m, v)


def _adamw_nd(w, g, m, v, name):
    shape = w.shape
    f = (lambda a: a) if w.ndim == 3 else (lambda a: a.reshape(1, shape[0], shape[1]))
    return tuple(o.reshape(shape) for o in _adamw(f(w), f(g), f(m), f(v), name))


ANY = pl.BlockSpec(memory_space=pl.ANY)


def _place():
    x, y, c = lax.axis_index("x"), lax.axis_index("y"), lax.axis_index("c")
    chips = [(1 - x, y), (x, 1 - y), (1 - x, 1 - y)]
    return x, y, c, chips


class _WeightGather:
    def __init__(self, hop, layer, rows):
        self.hop, self.layer, self.rows = hop, layer, rows
        self.n_sem = 3 * len(rows)

    def _copies(self, shard_refs, buf_refs, send, recv, received):
        x, y, c, chips = _place()
        out = []
        for t, R in enumerate(self.rows):
            half = R // 2
            assert half % 16 == 0
            mine = pl.ds(pl.multiple_of(c * half, 16), half)
            other = pl.ds(pl.multiple_of((1 - c) * half, 16), half)
            mine_of_shard = pl.ds(pl.multiple_of(self.layer * R + c * half, 16), half)
            for k, chip in enumerate(chips):
                theirs = buf_refs[t].at[2 * chip[0] + chip[1]]
                if self.hop == "chips":
                    src, dst, to = shard_refs[t].at[mine_of_shard], buf_refs[t].at[2 * x + y, mine], (*chip, c)
                    landed = theirs.at[mine]
                else:
                    src, dst, to = theirs.at[mine], theirs.at[mine], (x, y, 1 - c)
                    landed = theirs.at[other]
                out.append(pltpu.make_async_remote_copy(
                    src_ref=src, dst_ref=landed if received else dst, send_sem=send.at[3 * t + k],
                    recv_sem=recv.at[3 * t + k], device_id=to, device_id_type=MESH))
        return out

    def start(self, shard_refs, buf_refs, send, recv):
        for cp in self._copies(shard_refs, buf_refs, send, recv, False):
            cp.start()

    def wait(self, shard_refs, buf_refs, send, recv):
        for cp in self._copies(shard_refs, buf_refs, send, recv, True):
            cp.wait_recv()
        for cp in self._copies(shard_refs, buf_refs, send, recv, False):
            cp.wait_send()

    def sems(self):
        return [pltpu.SemaphoreType.DMA((self.n_sem,)), pltpu.SemaphoreType.DMA((self.n_sem,))]

    def call(self, srcs, bufs, name):
        ns, nb = len(srcs), len(bufs)

        def body(*refs):
            src_refs, buf_refs, (send, recv) = refs[:ns], refs[ns + nb:ns + 2 * nb], refs[ns + 2 * nb:]
            self.start(src_refs, buf_refs, send, recv)
            self.wait(src_refs, buf_refs, send, recv)

        return pl.pallas_call(
            body, name=name, in_specs=[ANY] * (ns + nb), out_specs=[ANY] * nb,
            out_shape=[jax.ShapeDtypeStruct(b.shape, b.dtype) for b in bufs],
            input_output_aliases={ns + t: t for t in range(nb)},
            scratch_shapes=self.sems(),
        )(*srcs, *bufs)


class _GradHop(_WeightGather):
    def __init__(self, hop, layer, slices):
        self.hop, self.layer, self.slices = hop, layer, slices
        self.n_sem = {"pair": N_CHIPS, "chips": N_CHIPS - 1, "gather": 1}[hop] * len(slices)

    def _copies(self, src_refs, buf_refs, send, recv, received):
        x, y, c, chips = _place()
        me, out = 2 * x + y, []

        def remote(src, dst, to):
            k = len(out)
            out.append(pltpu.make_async_remote_copy(src_ref=src, dst_ref=dst, send_sem=send.at[k], recv_sem=recv.at[k],
                                                    device_id=to, device_id_type=MESH))

        for t, (half, where) in enumerate(self.slices):
            if self.hop == "pair":
                for j, (a, first) in enumerate(where):
                    rows = pl.ds(pl.multiple_of(first + (1 - c) * half, 8), half)
                    remote(src_refs[a].at[rows], buf_refs[t].at[j], (x, y, 1 - c))
            elif self.hop == "chips":
                for chip in chips:
                    slot = 2 * chip[0] + chip[1]
                    remote(src_refs[t].at[slot], buf_refs[t].at[slot if received else me], (*chip, c))
            else:
                mine = buf_refs[t].at[self.layer, c]
                remote(mine, buf_refs[t].at[self.layer, 1 - c] if received else mine, (x, y, 1 - c))
        return out


def _add_halves(parts, got, c_idx, name):
    n, L, half, C = got.shape
    tr = 64

    def body(*refs):
        ins, (got_ref, o_ref) = refs[1:1 + len(parts)], refs[1 + len(parts):]
        for k in range(len(parts)):
            o_ref[k // L, k % L] = (ins[k][...] + got_ref[k // L, k % L]).astype(BF16)

    def rows_of(first):
        assert first % tr == 0 and half % tr == 0
        return lambda i, c: (first // tr + c[0] * (half // tr) + i, 0)

    whole = pl.BlockSpec((n, L, tr, C), lambda i, c: (0, 0, i, 0))
    return pl.pallas_call(
        body, name=name,
        grid_spec=pltpu.PrefetchScalarGridSpec(
            num_scalar_prefetch=1, grid=(half // tr,),
            in_specs=[pl.BlockSpec((tr, C), rows_of(first)) for _, first in parts] + [whole],
            out_specs=whole),
        out_shape=jax.ShapeDtypeStruct((n, L, half, C), BF16),
        compiler_params=_params(("parallel",)),
    )(c_idx, *[a for a, _ in parts], got)


def _add_slots(r, c_idx, dest, layer, name):
    n, half, C = r.shape
    tr = 256

    def body(c_ref, r_ref, dest_ref, o_ref):
        del dest_ref
        acc = r_ref[0].astype(F32)
        for j in range(1, n):
            acc = acc + r_ref[j].astype(F32)
        o_ref[0, 0] = acc

    return pl.pallas_call(
        body, name=name,
        grid_spec=pltpu.PrefetchScalarGridSpec(
            num_scalar_prefetch=1, grid=(half // tr,),
            in_specs=[pl.BlockSpec((n, tr, C), lambda i, c: (0, i, 0)), ANY],
            out_specs=pl.BlockSpec((1, 1, tr, C), lambda i, c: (layer, c[0], i, 0))),
        out_shape=jax.ShapeDtypeStruct(dest.shape, F32),
        input_output_aliases={2: 0},
        compiler_params=_params(("parallel",)),
    )(c_idx, r, dest)


class _LayerReduce:
    def __init__(self, layer, grads, slices, c_idx, chip, dests):
        self.layer, self.grads, self.slices, self.c_idx, self.chip, self.dests = layer, grads, slices, c_idx, chip, dests
        self.widths = [grads[where[0][0]].shape[1] for _, where in slices]
        self.stage = 0

    def _hop(self, kind):
        return _GradHop(kind, self.layer, self.slices)

    def rider(self):
        if self.stage == 0:
            got = [lax.empty((N_CHIPS, half, w), F32) for (half, _), w in zip(self.slices, self.widths)]
            return self._hop("pair"), self.grads, got
        if self.stage == 1:
            q = [lax.empty(p.shape, BF16) for p in self.pair_sums]
            return self._hop("chips"), self.pair_sums, q
        return self._hop("gather"), [], self.dests

    def landed(self, bufs):
        tag = f"{self.layer}"
        if self.stage == 0:
            self.pair_sums = []
            for t, ((half, where), got) in enumerate(zip(self.slices, bufs)):
                parts = [(self.grads[a], first) for a, first in where]
                p = _add_halves(parts, got[:, None], self.c_idx, f"rs_add2_{t}_{tag}")
                self.pair_sums.append(p.reshape(N_CHIPS, half, p.shape[-1]))
        elif self.stage == 1:
            for t, (q, p) in enumerate(zip(bufs, self.pair_sums)):
                q = lax.dynamic_update_slice_in_dim(q, lax.dynamic_slice_in_dim(p, self.chip, 1, axis=0), self.chip, axis=0)
                self.dests[t] = _add_slots(q, self.c_idx, self.dests[t], self.layer, f"rs_add4_{t}_{tag}")
        else:
            self.dests = list(bufs)
        self.stage += 1

    def step(self):
        hop, srcs, bufs = self.rider()
        self.landed(hop.call(srcs, bufs, f"rs_{hop.hop}_{self.layer}"))

    def finish(self):
        while self.stage < 3:
            self.step()
        return self.dests


def _all_reduce_small(buf, name):
    R = buf.shape[0]

    def flipped(k, x, y, c):
        return ((1 - x) if k & 4 else x, (1 - y) if k & 2 else y, (1 - c) if k & 1 else c)

    def body(b_ref, o_ref, land, send, recv):
        x, y, c, _ = _place()
        me = 4 * x + 2 * y + c
        land[me] = b_ref[...]
        cps = []
        for k in range(1, N_DEV):
            peer = flipped(k, x, y, c)
            cps.append(pltpu.make_async_remote_copy(src_ref=b_ref, dst_ref=land.at[me], send_sem=send.at[k - 1],
                                                    recv_sem=recv.at[k - 1], device_id=peer, device_id_type=MESH))
        for cp in cps:
            cp.start()
        for k in range(1, N_DEV):
            peer = flipped(k, x, y, c)
            slot = 4 * peer[0] + 2 * peer[1] + peer[2]
            pltpu.make_async_remote_copy(src_ref=b_ref, dst_ref=land.at[slot], send_sem=send.at[k - 1],
                                         recv_sem=recv.at[k - 1], device_id=peer, device_id_type=MESH).wait_recv()
        for cp in cps:
            cp.wait_send()
        acc = land[0]
        for j in range(1, N_DEV):
            acc = acc + land[j]
        o_ref[...] = acc

    vm = pl.BlockSpec(memory_space=pltpu.VMEM)
    return pl.pallas_call(
        body, name=name, in_specs=[vm], out_specs=vm,
        out_shape=jax.ShapeDtypeStruct((R, LANES), F32),
        scratch_shapes=[pltpu.VMEM((N_DEV, R, LANES), F32), pltpu.SemaphoreType.DMA((N_DEV - 1,)),
                        pltpu.SemaphoreType.DMA((N_DEV - 1,))],
    )(buf)


def kernel(x, mem, w_in, a_rel_bias, b_gate_w, b_gate_b, b_norm_g, w_mem_kv, w_out, ln_g, ln_b, loss_target, m_w_in, m_a_rel_bias, m_b_gate_w, m_b_gate_b, m_b_norm_g, m_w_mem_kv, m_w_out, m_ln_g, m_ln_b, v_w_in, v_a_rel_bias, v_b_gate_w, v_b_gate_b, v_b_norm_g, v_w_mem_kv, v_w_out, v_ln_g, v_ln_b):
    L = w_in.shape[0]
    S = x.shape[1]
    cx, cy, cc = lax.axis_index("x"), lax.axis_index("y"), lax.axis_index("c")
    chip = 2 * cx + cy
    c_idx = jnp.reshape(cc, (1,)).astype(jnp.int32)

    n_in, r_kv, r_out = w_in.shape[2], w_mem_kv.shape[1], w_out.shape[1]
    shards = [w.astype(BF16).reshape(-1, w.shape[2]) for w in (w_in, w_mem_kv, w_out)]
    rows = [D_MODEL, r_kv, r_out]

    def landing(l):
        return [lax.dynamic_update_slice_in_dim(lax.empty((N_CHIPS, r, s.shape[1]), BF16),
                                                s[l * r:(l + 1) * r][None], chip, axis=0)
                for s, r in zip(shards, rows)]

    def assembled(bufs):
        return (_padded_from_shards([bufs[0][j] for j in range(N_CHIPS)]),
                bufs[1].reshape(D_MODEL, bufs[1].shape[2]), bufs[2].reshape(D_MODEL, D_MODEL))

    bufs0 = _WeightGather("chips", 0, rows).call(shards, landing(0), "gather_chips_0")
    weights = [assembled(_WeightGather("pair", 0, rows).call(shards, bufs0, "gather_pair_0"))]

    gw_cols = b_gate_w.shape[2]
    gw_slot = jnp.zeros((N_CHIPS, L, GATE_RANK, gw_cols), F32)
    gw_slot = lax.dynamic_update_slice(gw_slot, (0.5 * b_gate_w)[None], (chip, 0, 0, 0))
    gw_flat = gw_slot.reshape(-1)
    n_gw = gw_flat.shape[0]
    pad = (-n_gw) % (8 * LANES)
    gw_full = _all_reduce_small(jnp.pad(gw_flat, (0, pad)).reshape(-1, LANES), "gather_gate_w").reshape(-1)[:n_gw]
    gw_full = gw_full.reshape(N_CHIPS, L, GATE_RANK, gw_cols).transpose(1, 2, 0, 3).reshape(L, GATE_RANK, B_KEY_WIDTH)

    xs = x.reshape(S, D_MODEL)
    mem_b = mem.reshape(mem.shape[1], D_MODEL).astype(BF16)
    target = loss_target.reshape(S, D_MODEL)

    small_w = []
    for l in range(L):
        gw_l, gb_l = _pad_gate(gw_full[l], b_gate_b[l])
        small_w.append((_bias_by_offset(a_rel_bias[l]), gw_l, gb_l,
                        b_norm_g[l].reshape(1, LANES), ln_g[l].reshape(1, D_MODEL), ln_b[l].reshape(1, D_MODEL)))

    y, yb = xs, xs.astype(BF16)
    yt = _transpose(yb, "x_t")
    saved = []
    for l in range(L):
        if l + 1 < L:
            rider = (_WeightGather("chips", l + 1, rows), shards, landing(l + 1))
            y, yb, yt, sv, bufs = _layer_fwd(y, yb, yt, mem_b, *weights[l], *small_w[l], rider=rider)
            weights.append(assembled(_WeightGather("pair", l + 1, rows).call(shards, bufs, f"gather_pair_{l + 1}")))
        else:
            y, yb, yt, sv = _layer_fwd(y, yb, yt, mem_b, *weights[l], *small_w[l])
        saved.append(sv)
    layer_w = [(*weights[l], *small_w[l]) for l in range(L)]
    loss_part, dy = _loss_head(y, target)

    halves = [D_MODEL // 2, r_kv // 2, r_out // 2]
    dests = [lax.empty((L, 2, hf, w.shape[2]), F32) for hf, w in zip(halves, (w_in, w_mem_kv, w_out))]
    slices = [(halves[0], [(j, 0) for j in range(N_CHIPS)]),
              (halves[1], [(N_CHIPS, j * r_kv) for j in range(N_CHIPS)]),
              (halves[2], [(N_CHIPS + 1, j * r_out) for j in range(N_CHIPS)])]

    def reduction(l, g, into):
        arrays = [_chip_columns(g[0], j, n_in) for j in range(N_CHIPS)] + [g[5], g[6]]
        return _LayerReduce(l, arrays, slices, c_idx, chip, into)

    grads, reduce = [None] * L, None
    for l in reversed(range(L)):
        w_in_l, w_kv_l, w_out_l, u_l, gw_l, gb_l, gn_l, lg_l, lb_l = layer_w[l]
        args = (dy, saved[l], mem_b, w_in_l, w_out_l, u_l, gw_l, gb_l, gn_l, lg_l)
        if l > 0:
            dy, grads[l] = _layer_bwd(*args, reduce=reduce)
            if reduce is not None:
                dests = reduce.finish()
            reduce = reduction(l, grads[l], dests)
        else:
            own = lambda g, above: reduction(0, g, dests if above is None else above)
            dy, grads[l], reduce = _layer_bwd(*args, reduce=reduce, own_reduce=own)
    r_w_in, r_w_kv, r_w_out = [d.reshape(L, 2 * d.shape[2], d.shape[3]) for d in reduce.finish()]
    grad_x = dy.reshape(x.shape)

    g_rel = jnp.stack([_bias_grad_from_offset(g[1]) for g in grads])
    g_gw = jnp.stack([_unpad_heads(g[2][:GATE_RANK]) for g in grads])
    g_gb = jnp.stack([_unpad_heads(g[3])[0] for g in grads])
    g_gn = jnp.stack([g[4][0] for g in grads])
    g_lg = jnp.stack([g[7][0] for g in grads])
    g_lb = jnp.stack([g[8][0] for g in grads])

    small = [g_rel, g_gw, g_gb, g_gn, g_lg, g_lb, loss_part]
    flat = jnp.concatenate([s.reshape(-1) for s in small])
    n_small = flat.shape[0]
    pad = (-n_small) % (8 * LANES)
    red = _all_reduce_small(jnp.pad(flat, (0, pad)).reshape(-1, LANES), "all_reduce_small").reshape(-1)
    outs, off = [], 0
    for s in small:
        outs.append(red[off:off + s.size].reshape(s.shape))
        off += s.size
    g_rel, g_gw, g_gb, g_gn, g_lg, g_lb, loss = outs
    loss = loss.reshape(())
    g_gw = lax.dynamic_slice_in_dim(g_gw.reshape(L, GATE_RANK, N_CHIPS, gw_cols), chip, 1, axis=2).reshape(L, GATE_RANK, gw_cols)

    g_list = [r_w_in, g_rel, g_gw, g_gb, g_gn, r_w_kv, r_w_out, g_lg, g_lb]
    w_list = [w_in, a_rel_bias, b_gate_w, b_gate_b, b_norm_g, w_mem_kv, w_out, ln_g, ln_b]
    m_list = [m_w_in, m_a_rel_bias, m_b_gate_w, m_b_gate_b, m_b_norm_g, m_w_mem_kv, m_w_out, m_ln_g, m_ln_b]
    v_list = [v_w_in, v_a_rel_bias, v_b_gate_w, v_b_gate_b, v_b_norm_g, v_w_mem_kv, v_w_out, v_ln_g, v_ln_b]
    names = ["w_in", "rel", "gate_w", "gate_b", "norm_g", "w_kv", "w_out", "ln_g", "ln_b"]
    to_cols = lambda a: jnp.transpose(a, (2, 0, 1))
    upd = [tuple(jnp.transpose(o, (1, 2, 0)) for o in
                 _adamw(to_cols(w_in), to_cols(r_w_in), to_cols(m_w_in), to_cols(v_w_in), "adamw_w_in"))]
    upd += [_adamw_nd(w, g, m, v, "adamw_" + n)
            for w, g, m, v, n in list(zip(w_list, g_list, m_list, v_list, names))[1:]]
    deltas = [u_[0] for u_ in upd]
    new_m = [u_[1] for u_ in upd]
    new_v = [u_[2] for u_ in upd]
    return (loss, grad_x, *g_list, *deltas, *new_m, *new_v)
```

```python
import functools

import numpy as np
import jax
import jax.numpy as jnp
from jax import lax
from jax.experimental import pallas as pl
from jax.experimental.pallas import tpu as pltpu

F32 = jnp.float32
BF16 = jnp.bfloat16
MESH = pl.DeviceIdType.MESH

D_MODEL = 2048
DEPTH = 4
CHUNK = 64
LEFT_CHUNKS = 8
MAX_REL = 128
N_REL = 2 * MAX_REL + 1
A_HEADS = 8
HEAD_DIM = 128
B_HEADS = 4
B_DK = 64
M_HEADS = 4
GATE_RANK = 16
GATE_TAU = 16.0
A_WIDTH = A_HEADS * HEAD_DIM
B_WIDTH = B_HEADS * HEAD_DIM
B_KEY_WIDTH = B_HEADS * B_DK
M_WIDTH = M_HEADS * HEAD_DIM
IN_WIDTH = 4 * A_WIDTH + 2 * B_KEY_WIDTH + 2 * B_WIDTH + GATE_RANK + 2 * M_WIDTH
ALPHA = (2.0 * DEPTH) ** 0.25
LN_EPS = 1e-5
RMS_EPS = 1e-6
NEG_INF = -1e30
ADAM_LR = 0.001
ADAM_B1 = 0.9
ADAM_B2 = 0.999
ADAM_EPS = 1e-08
ADAM_WD = 0.01
ADAM_STEP = 10

LANES = 128
VMEM_LIMIT = 56 * 1024 * 1024

C_A, C_B, C_M, C_LR = 0, 4096, 6144, 7168
HP = 7680
SEG_Q, SEG_K, SEG_V, SEG_Z = 0, 1, 2, 3
TQ = 512
CPB = TQ // CHUNK
N_CHIPS = 4
N_DEV = 8


def _params(sem, vmem=VMEM_LIMIT):
    return pltpu.CompilerParams(dimension_semantics=sem, vmem_limit_bytes=vmem)


def _dot(a, b):
    return jnp.dot(a, b, preferred_element_type=F32)


def _dot_nt(a, b):
    return lax.dot_general(a, b, (((1,), (1,)), ((), ())), preferred_element_type=F32)


def _dot_tn(a, b):
    return lax.dot_general(a, b, (((0,), (0,)), ((), ())), preferred_element_type=F32)


def _sigmoid(x):
    return 1.0 / (1.0 + jnp.exp(-x))


def _split3(x):
    hi = x.astype(BF16)
    r = x - hi.astype(F32)
    mid = r.astype(BF16)
    lo = (r - mid.astype(F32)).astype(BF16)
    return hi, mid, lo


def _dot3(m_bf, x):
    hi, mid, lo = _split3(x)
    return _dot(m_bf, hi) + _dot(m_bf, mid) + _dot(m_bf, lo)


def _matmul(a, b, *, mode, out_dtype, tm, tn, tk, name, add=None, add_scale=1.0, rider=None):
    if mode == "nn":
        (M, K), (K2, N) = a.shape, b.shape
        a_spec = pl.BlockSpec((tm, tk), lambda i, j, k: (i, k))
        b_spec = pl.BlockSpec((tk, tn), lambda i, j, k: (k, j))
        dot = _dot
    elif mode == "nt":
        (M, K), (N, K2) = a.shape, b.shape
        a_spec = pl.BlockSpec((tm, tk), lambda i, j, k: (i, k))
        b_spec = pl.BlockSpec((tn, tk), lambda i, j, k: (j, k))
        dot = _dot_nt
    else:
        (K, M), (K2, N) = a.shape, b.shape
        a_spec = pl.BlockSpec((tk, tm), lambda i, j, k: (k, i))
        b_spec = pl.BlockSpec((tk, tn), lambda i, j, k: (k, j))
        dot = _dot_tn
    assert K == K2 and M % tm == 0 and N % tn == 0 and K % tk == 0, (a.shape, b.shape, mode)
    nk = K // tk
    has_add = add is not None
    assert nk == 1 or out_dtype == F32
    grid = (M // tm, N // tn, nk)
    hop, srcs, bufs = rider if rider is not None else (None, [], [])
    n_in = 2 + has_add

    def body(*refs):
        a_ref, b_ref = refs[:2]
        add_ref = refs[2] if has_add else None
        src_refs = refs[n_in:n_in + len(srcs)]
        o_ref = refs[n_in + len(srcs) + len(bufs)]
        buf_refs = refs[n_in + len(srcs) + len(bufs) + 1:n_in + len(srcs) + 2 * len(bufs) + 1]
        sems = refs[n_in + len(srcs) + 2 * len(bufs) + 1:]
        i, j, k = pl.program_id(0), pl.program_id(1), pl.program_id(2)
        if hop is not None:
            @pl.when(jnp.logical_and(jnp.logical_and(i == 0, j == 0), k == 0))
            def _():
                hop.start(src_refs, buf_refs, *sems)

        part = dot(a_ref[...].astype(BF16), b_ref[...].astype(BF16))

        @pl.when(k == 0)
        def _():
            first = part + add_scale * add_ref[...] if has_add else part
            o_ref[...] = first.astype(out_dtype)

        if nk > 1:
            @pl.when(k > 0)
            def _():
                o_ref[...] += part

        if hop is not None:
            @pl.when(jnp.logical_and(jnp.logical_and(i == grid[0] - 1, j == grid[1] - 1), k == nk - 1))
            def _():
                hop.wait(src_refs, buf_refs, *sems)

    in_specs = [a_spec, b_spec]
    args = [a, b]
    if has_add:
        in_specs.append(pl.BlockSpec((tm, tn), lambda i, j, k: (i, j)))
        args.append(add)
    out = pl.pallas_call(
        body,
        name=name,
        grid=grid,
        in_specs=in_specs + [ANY] * (len(srcs) + len(bufs)),
        out_specs=[pl.BlockSpec((tm, tn), lambda i, j, k: (i, j))] + [ANY] * len(bufs),
        out_shape=[jax.ShapeDtypeStruct((M, N), out_dtype)] + [jax.ShapeDtypeStruct(x.shape, x.dtype) for x in bufs],
        input_output_aliases={n_in + len(srcs) + t: 1 + t for t in range(len(bufs))},
        scratch_shapes=hop.sems() if hop is not None else [],
        compiler_params=_params(("parallel", "parallel", "arbitrary") if hop is None
                                else ("arbitrary", "arbitrary", "arbitrary")),
    )(*args, *srcs, *bufs)
    return out[0] if hop is None else (out[0], list(out[1:]))


def _transpose(a, name):
    R, C = a.shape
    t = 512

    def body(a_ref, o_ref):
        o_ref[...] = a_ref[...].T

    return pl.pallas_call(
        body, name=name, grid=(R // t, C // t),
        in_specs=[pl.BlockSpec((t, t), lambda i, j: (i, j))],
        out_specs=pl.BlockSpec((t, t), lambda i, j: (j, i)),
        out_shape=jax.ShapeDtypeStruct((C, R), a.dtype),
        compiler_params=_params(("parallel", "parallel")),
    )(a)


def _chunk_of(rows):
    return lax.shift_right_logical(rows, CHUNK.bit_length() - 1)


A_HPS = 2
A_HW = A_HPS * LANES


def _band_bias(u_row, first):
    bias = pltpu.roll(jnp.broadcast_to(u_row, (TQ, 2 * TQ)), 0, 1, stride=1, stride_axis=0)
    qc = _chunk_of(lax.broadcasted_iota(jnp.int32, (TQ, 2 * TQ), 0))
    col = lax.broadcasted_iota(jnp.int32, (TQ, 2 * TQ), 1)
    kc = _chunk_of(jnp.bitwise_and(col, TQ - 1))
    ok = jnp.logical_or(jnp.logical_and(col < TQ, kc >= qc), jnp.logical_and(col >= TQ, kc <= qc))
    return jnp.where(ok, bias, NEG_INF) + jnp.where(col < TQ, first * NEG_INF, 0.0)


HQ = TQ // 2
HALVES = ((slice(0, HQ), slice(0, 3 * HQ)),
          (slice(HQ, TQ), slice(HQ, 4 * HQ)))


def _band_probs(q, kcat, bias):
    scale = HEAD_DIM ** -0.5
    out = []
    for rows, cols in HALVES:
        s = _dot_nt(q[rows], kcat[cols]) * scale + bias[rows, cols]
        p = jnp.exp(s - jnp.max(s, axis=1, keepdims=True))
        out.append((p, 1.0 / jnp.sum(p, axis=1, keepdims=True)))
    return out


def _band_specs(nq):
    def col(seg, h):
        return C_A // A_HW + 4 * h + seg

    q_spec = pl.BlockSpec((TQ, A_HW), lambda h, i: (jnp.minimum(i, nq - 1), col(SEG_Q, h)))
    kp_spec = pl.BlockSpec((TQ, A_HW), lambda h, i: (jnp.clip(i - 1, 0, nq - 1), col(SEG_K, h)))
    kc_spec = pl.BlockSpec((TQ, A_HW), lambda h, i: (jnp.minimum(i, nq - 1), col(SEG_K, h)))
    vp_spec = pl.BlockSpec((TQ, A_HW), lambda h, i: (jnp.clip(i - 1, 0, nq - 1), col(SEG_V, h)))
    vc_spec = pl.BlockSpec((TQ, A_HW), lambda h, i: (jnp.minimum(i, nq - 1), col(SEG_V, h)))
    z_spec = pl.BlockSpec((TQ, A_HW), lambda h, i: (jnp.minimum(i, nq - 1), col(SEG_Z, h)))
    u_spec = pl.BlockSpec((A_HPS, 1, 2 * TQ), lambda h, i: (h, 0, 0))
    return q_spec, kp_spec, kc_spec, vp_spec, vc_spec, z_spec, u_spec


def _band_fwd(h, u):
    S = h.shape[0]
    nq = S // TQ

    def body(q_ref, kp_ref, kc_ref, vp_ref, vc_ref, z_ref, u_ref, y_ref, yt_ref, bias_scr):
        i = pl.program_id(1)

        @pl.when(i <= 1)
        def _():
            for hh in range(A_HPS):
                bias_scr[hh] = _band_bias(u_ref[hh], (i == 0).astype(F32))

        for hh in range(A_HPS):
            cs = slice(hh * LANES, (hh + 1) * LANES)
            kcat = jnp.concatenate([kp_ref[:, cs], kc_ref[:, cs]], axis=0)
            vcat = jnp.concatenate([vp_ref[:, cs], vc_ref[:, cs]], axis=0)
            probs = _band_probs(q_ref[:, cs], kcat, bias_scr[hh])
            o = jnp.concatenate([_dot(p.astype(BF16), vcat[cols]) * inv
                                 for (p, inv), (_, cols) in zip(probs, HALVES)], axis=0)
            z = z_ref[:, cs].astype(F32)
            y = o * (z * _sigmoid(z))
            y_ref[:, cs] = y.astype(BF16)
            yt_ref[cs, :] = y.T.astype(BF16)

    specs = _band_specs(nq)
    return pl.pallas_call(
        body,
        name="band_fwd",
        grid=(A_HEADS // A_HPS, nq),
        in_specs=[specs[0], specs[1], specs[2], specs[3], specs[4], specs[5], specs[6]],
        out_specs=[pl.BlockSpec((TQ, A_HW), lambda h, i: (i, h)), pl.BlockSpec((A_HW, TQ), lambda h, i: (h, i))],
        out_shape=[jax.ShapeDtypeStruct((S, D_MODEL), BF16), jax.ShapeDtypeStruct((D_MODEL, S), BF16)],
        scratch_shapes=[pltpu.VMEM((A_HPS, TQ, 2 * TQ), F32)],
        compiler_params=_params(("parallel", "arbitrary")),
    )(h, h, h, h, h, h, u)


def _band_bwd(h, u, dycat):
    S = h.shape[0]
    nq = S // TQ
    scale = HEAD_DIM ** -0.5
    qs, ks, vs, zs = (slice(s * A_HW, (s + 1) * A_HW) for s in (SEG_Q, SEG_K, SEG_V, SEG_Z))

    def body(q_ref, kp_ref, kc_ref, vp_ref, vc_ref, z_ref, u_ref, dy_ref,
             dh_ref, du_ref, bias_scr, db_scr, ckt_scr, cvt_scr, cq_scr, cz_scr):
        i = pl.program_id(1)

        @pl.when(i <= 1)
        def _():
            for hh in range(A_HPS):
                bias_scr[hh] = _band_bias(u_ref[hh], (i == 0).astype(F32))

        @pl.when(i == 0)
        def _():
            db_scr[...] = jnp.zeros_like(db_scr)
            ckt_scr[...] = jnp.zeros_like(ckt_scr)
            cvt_scr[...] = jnp.zeros_like(cvt_scr)
            cq_scr[...] = jnp.zeros_like(cq_scr)
            cz_scr[...] = jnp.zeros_like(cz_scr)

        @pl.when(i < nq)
        def _():
            dh_ref[:, qs] = cq_scr[...]
            dh_ref[:, zs] = cz_scr[...]
            for hh in range(A_HPS):
                cs = slice(hh * LANES, (hh + 1) * LANES)
                q = q_ref[:, cs]
                kcat = jnp.concatenate([kp_ref[:, cs], kc_ref[:, cs]], axis=0)
                vcat = jnp.concatenate([vp_ref[:, cs], vc_ref[:, cs]], axis=0)
                probs = [p * inv for p, inv in _band_probs(q, kcat, bias_scr[hh])]
                o = jnp.concatenate([_dot(p.astype(BF16), vcat[cols]) for p, (_, cols) in zip(probs, HALVES)], axis=0)
                z = z_ref[:, cs].astype(F32)
                sg = _sigmoid(z)
                dy = dy_ref[:, cs].astype(F32)
                do = dy * (z * sg)
                cz_scr[:, cs] = (dy * o * (sg * (1.0 + z * (1.0 - sg)))).astype(BF16)
                dob = do.astype(BF16)
                delta = jnp.sum(do * o, axis=1, keepdims=True)
                qt, dot_ = q.T, dob.T
                dq, dkt, dvt = [], [], []
                for p, (rows, cols) in zip(probs, HALVES):
                    ds = p * (_dot_nt(dob[rows], vcat[cols]) - delta[rows])
                    db_scr[hh, rows, cols] += ds
                    dsb = ds.astype(BF16)
                    dq.append(scale * _dot(dsb, kcat[cols]))
                    dkt.append(scale * _dot(qt[:, rows], dsb))
                    dvt.append(_dot(dot_[:, rows], p.astype(BF16)))
                cq_scr[:, cs] = jnp.concatenate(dq, axis=0).astype(BF16)

                def over_keys(parts):
                    lo, hi = parts
                    prev = jnp.concatenate([lo[:, :HQ], lo[:, HQ:2 * HQ] + hi[:, :HQ]], axis=1)
                    cur = jnp.concatenate([lo[:, 2 * HQ:] + hi[:, HQ:2 * HQ], hi[:, 2 * HQ:]], axis=1)
                    return prev, cur

                (dk_prev, dk_cur), (dv_prev, dv_cur) = over_keys(dkt), over_keys(dvt)
                dh_ref[:, SEG_K * A_HW + hh * LANES:SEG_K * A_HW + (hh + 1) * LANES] = (
                    ckt_scr[cs, :] + dk_prev).T.astype(BF16)
                dh_ref[:, SEG_V * A_HW + hh * LANES:SEG_V * A_HW + (hh + 1) * LANES] = (
                    cvt_scr[cs, :] + dv_prev).T.astype(BF16)
                ckt_scr[cs, :] = dk_cur
                cvt_scr[cs, :] = dv_cur

        @pl.when(i == nq)
        def _():
            dh_ref[:, qs] = cq_scr[...]
            dh_ref[:, zs] = cz_scr[...]
            dh_ref[:, ks] = ckt_scr[...].T.astype(BF16)
            dh_ref[:, vs] = cvt_scr[...].T.astype(BF16)
            r0 = lax.broadcasted_iota(jnp.int32, (TQ, TQ), 0)
            r1 = lax.broadcasted_iota(jnp.int32, (TQ, TQ), 1)
            flip = (r0 + r1 == TQ - 1).astype(BF16)
            for hh in range(A_HPS):
                fl = _dot3(flip, db_scr[hh])
                rolled = pltpu.roll(fl, 0, 1, stride=1, stride_axis=0)
                du_ref[hh] = jnp.sum(rolled, axis=0, keepdims=True)

    specs = _band_specs(nq)
    dy_spec = pl.BlockSpec((TQ, A_HW), lambda h, i: (jnp.minimum(i, nq - 1), h))
    return pl.pallas_call(
        body,
        name="band_bwd",
        grid=(A_HEADS // A_HPS, nq + 1),
        in_specs=[specs[0], specs[1], specs[2], specs[3], specs[4], specs[5], specs[6], dy_spec],
        out_specs=[pl.BlockSpec((TQ, 4 * A_HW), lambda h, i: (jnp.maximum(i - 1, 0), C_A // (4 * A_HW) + h)),
                   pl.BlockSpec((A_HPS, 1, 2 * TQ), lambda h, i: (h, 0, 0))],
        out_shape=[jax.ShapeDtypeStruct((S, HP), BF16), jax.ShapeDtypeStruct((A_HEADS, 1, 2 * TQ), F32)],
        scratch_shapes=[pltpu.VMEM((A_HPS, TQ, 2 * TQ), F32), pltpu.VMEM((A_HPS, TQ, 2 * TQ), F32),
                        pltpu.VMEM((A_HW, TQ), F32), pltpu.VMEM((A_HW, TQ), F32),
                        pltpu.VMEM((TQ, A_HW), BF16), pltpu.VMEM((TQ, A_HW), BF16)],
        compiler_params=_params(("parallel", "arbitrary")),
    )(h, h, h, h, h, h, u, dycat)


def _bias_by_offset(table):
    far = jnp.broadcast_to(table[:, N_REL - 1:], (A_HEADS, TQ - MAX_REL))
    ramp = jnp.flip(table, axis=1)
    rest = jnp.broadcast_to(table[:, :1], (A_HEADS, 2 * TQ - CHUNK - (TQ + MAX_REL + 1)))
    wrap = jnp.broadcast_to(table[:, N_REL - 1:], (A_HEADS, CHUNK))
    return jnp.concatenate([far, ramp, rest, wrap], axis=1)[:, None, :]


def _bias_grad_from_offset(du):
    g = jnp.roll(du[:, 0, :], -(TQ - 1), axis=1)
    far = jnp.sum(g[:, :TQ - MAX_REL], axis=1) + jnp.sum(g[:, 2 * TQ - CHUNK:], axis=1)
    ramp = jnp.flip(g[:, TQ - MAX_REL:TQ + MAX_REL + 1], axis=1)
    return ramp.at[:, N_REL - 1].add(far)


def _mem_probs(q, mk):
    s = _dot_nt(q, mk) * (HEAD_DIM ** -0.5)
    p = jnp.exp(s - jnp.max(s, axis=1, keepdims=True))
    return p * (1.0 / jnp.sum(p, axis=1, keepdims=True))


def _mem_fwd(h, mkv, ycat, ycat_t):
    S = h.shape[0]
    nm = mkv.shape[0]
    c0 = (A_WIDTH + B_WIDTH) // LANES

    def body(q_ref, z_ref, mk_ref, mv_ref, yin_ref, ytin_ref, y_ref, yt_ref):
        del yin_ref, ytin_ref
        p = _mem_probs(q_ref[...], mk_ref[...])
        o = _dot(p.astype(BF16), mv_ref[...])
        z = z_ref[...].astype(F32)
        y = o * (z * _sigmoid(z))
        y_ref[...] = y.astype(BF16)
        yt_ref[...] = y.T.astype(BF16)

    return pl.pallas_call(
        body,
        name="mem_fwd",
        grid=(M_HEADS, S // TQ),
        in_specs=[pl.BlockSpec((TQ, LANES), lambda h, i: (i, C_M // LANES + 2 * h)),
                  pl.BlockSpec((TQ, LANES), lambda h, i: (i, C_M // LANES + 2 * h + 1)),
                  pl.BlockSpec((nm, LANES), lambda h, i: (0, h)),
                  pl.BlockSpec((nm, LANES), lambda h, i: (0, M_HEADS + h)), ANY, ANY],
        out_specs=[pl.BlockSpec((TQ, LANES), lambda h, i: (i, c0 + h)), pl.BlockSpec((LANES, TQ), lambda h, i: (c0 + h, i))],
        out_shape=[jax.ShapeDtypeStruct(ycat.shape, BF16), jax.ShapeDtypeStruct(ycat_t.shape, BF16)],
        input_output_aliases={4: 0, 5: 1},
        compiler_params=_params(("parallel", "arbitrary")),
    )(h, h, mkv, mkv, ycat, ycat_t)


def _mem_bwd(h, mkv, dycat, dh):
    S = h.shape[0]
    nm = mkv.shape[0]
    scale = HEAD_DIM ** -0.5

    def body(q_ref, z_ref, mk_ref, mv_ref, dy_ref, dhin_ref, dh_ref, dmk_ref, dmv_ref):
        del dhin_ref
        i = pl.program_id(1)
        q, mk, mv = q_ref[...], mk_ref[...], mv_ref[...]
        p = _mem_probs(q, mk)
        pb = p.astype(BF16)
        o = _dot(pb, mv)
        z = z_ref[...].astype(F32)
        sg = _sigmoid(z)
        dy = dy_ref[...].astype(F32)
        do = dy * (z * sg)
        dh_ref[:, LANES:] = (dy * o * (sg * (1.0 + z * (1.0 - sg)))).astype(BF16)
        dob = do.astype(BF16)
        ds = p * (_dot_nt(dob, mv) - jnp.sum(do * o, axis=1, keepdims=True))
        dsb = ds.astype(BF16)
        dh_ref[:, :LANES] = (scale * _dot(dsb, mk)).astype(BF16)
        dmk = scale * _dot_tn(dsb, q)
        dmv = _dot_tn(pb, dob)

        @pl.when(i == 0)
        def _():
            dmk_ref[...] = dmk
            dmv_ref[...] = dmv

        @pl.when(i > 0)
        def _():
            dmk_ref[...] += dmk
            dmv_ref[...] += dmv

    dh, dmk, dmv = pl.pallas_call(
        body,
        name="mem_bwd",
        grid=(M_HEADS, S // TQ),
        in_specs=[pl.BlockSpec((TQ, LANES), lambda h, i: (i, C_M // LANES + 2 * h)),
                  pl.BlockSpec((TQ, LANES), lambda h, i: (i, C_M // LANES + 2 * h + 1)),
                  pl.BlockSpec((nm, LANES), lambda h, i: (0, h)),
                  pl.BlockSpec((nm, LANES), lambda h, i: (0, M_HEADS + h)),
                  pl.BlockSpec((TQ, LANES), lambda h, i: (i, (A_WIDTH + B_WIDTH) // LANES + h)), ANY],
        out_specs=[pl.BlockSpec((TQ, 2 * LANES), lambda h, i: (i, C_M // (2 * LANES) + h)),
                   pl.BlockSpec((nm, LANES), lambda h, i: (0, h)),
                   pl.BlockSpec((nm, LANES), lambda h, i: (0, h))],
        out_shape=[jax.ShapeDtypeStruct(dh.shape, BF16),
                   jax.ShapeDtypeStruct((nm, M_WIDTH), F32), jax.ShapeDtypeStruct((nm, M_WIDTH), F32)],
        input_output_aliases={5: 0},
        compiler_params=_params(("parallel", "arbitrary")),
    )(h, h, mkv, mkv, dycat, dh)
    return dh, jnp.concatenate([dmk, dmv], axis=1)


def _chunk_masks():
    r = lax.broadcasted_iota(jnp.int32, (TQ, TQ), 0)
    c = lax.broadcasted_iota(jnp.int32, (TQ, TQ), 1)
    same = _chunk_of(r) == _chunk_of(c)
    return jnp.logical_and(same, c <= r), jnp.logical_and(same, c > r)


def _gla_gates(lr, gw, gb):
    logit = _dot(lr, gw) + gb
    sg = _sigmoid(logit)
    g = (jnp.minimum(logit, 0.0) - jnp.log(1.0 + jnp.exp(-jnp.abs(logit)))) * (1.0 / GATE_TAU)
    lo, _ = _chunk_masks()
    return sg, _dot3(lo.astype(BF16), g)


def _gla_factors(q, k, b):
    eb = jnp.exp(b)
    enb = jnp.exp(-b)
    return eb, enb, q * eb, q * enb, k * eb, k * enb


def _gla_intra(qp, qn, kp, kn):
    lo, up = _chunk_masks()
    return (jnp.where(lo, _dot_nt(qp.astype(BF16), kn.astype(BF16)), 0.0)
            + jnp.where(up, _dot_nt(qn.astype(BF16), kp.astype(BF16)), 0.0))


B_HPS = 2
B_HW = B_HPS * LANES


def _gla_specs(nb, rev):
    blk = (lambda i: nb - 1 - i) if rev else (lambda i: i)
    qkvz_spec = pl.BlockSpec((TQ, 4 * B_HW), lambda i, p: (blk(i), C_B // (4 * B_HW) + p))
    lr_spec = pl.BlockSpec((TQ, LANES), lambda i, p: (blk(i), C_LR // LANES))
    gw_spec = pl.BlockSpec((LANES, B_HW), lambda i, p: (0, p))
    gb_spec = pl.BlockSpec((1, B_HW), lambda i, p: (0, p))
    gn_spec = pl.BlockSpec((1, LANES), lambda i, p: (0, 0))
    return qkvz_spec, lr_spec, gw_spec, gb_spec, gn_spec, blk


def _head_cols(hh, seg):
    return slice((4 * hh + seg) * LANES, (4 * hh + seg + 1) * LANES)


def _gla_fwd(h, gw, gb, gn, ycat, ycat_t):
    S = h.shape[0]
    nb = S // TQ
    c0 = A_WIDTH // LANES

    def body(qkvz_ref, lr_ref, gw_ref, gb_ref, gn_ref, yin_ref, ytin_ref,
             y_ref, yt_ref, o_ref, st_ref, st_scr):
        del yin_ref, ytin_ref
        i, p = pl.program_id(0), pl.program_id(1)
        for hh in range(B_HPS):
            hd = B_HPS * p + hh
            lane = slice(hh * LANES, (hh + 1) * LANES)

            @pl.when(i == 0)
            def _():
                st_scr[hd] = jnp.zeros((LANES, LANES), F32)

            q = qkvz_ref[:, _head_cols(hh, SEG_Q)].astype(F32) * (B_DK ** -0.5)
            k = qkvz_ref[:, _head_cols(hh, SEG_K)].astype(F32)
            v = qkvz_ref[:, _head_cols(hh, SEG_V)]
            _, b = _gla_gates(lr_ref[...], gw_ref[:, lane], gb_ref[:, lane])
            _, _, qp, qn, kp, kn = _gla_factors(q, k, b)
            o_intra = _dot(_gla_intra(qp, qn, kp, kn).astype(BF16), v)
            qpb, knb = qp.astype(BF16), kn.astype(BF16)
            st = st_scr[hd]
            outs = []
            for c in range(CPB):
                rows = slice(c * CHUNK, (c + 1) * CHUNK)
                st_ref[hh, c] = st
                outs.append(_dot_nt(qpb[rows], st.astype(BF16)))
                e_last = jnp.exp(b[(c + 1) * CHUNK - 1:(c + 1) * CHUNK, :])
                st = (st + _dot_tn(v[rows], knb[rows])) * e_last
            st_scr[hd] = st
            o = o_intra + jnp.concatenate(outs, axis=0)
            o_ref[:, lane] = o
            r = lax.rsqrt(jnp.mean(o * o, axis=1, keepdims=True) + RMS_EPS)
            z = qkvz_ref[:, _head_cols(hh, SEG_Z)].astype(F32)
            y = o * r * gn_ref[...] * (z * _sigmoid(z))
            y_ref[:, lane] = y.astype(BF16)
            yt_ref[lane, :] = y.T.astype(BF16)

    qkvz_s, lr_s, gw_s, gb_s, gn_s, _ = _gla_specs(nb, False)
    return pl.pallas_call(
        body,
        name="gla_fwd",
        grid=(nb, B_HEADS // B_HPS),
        in_specs=[qkvz_s, lr_s, gw_s, gb_s, gn_s, ANY, ANY],
        out_specs=[pl.BlockSpec((TQ, B_HW), lambda i, p: (i, c0 // B_HPS + p)),
                   pl.BlockSpec((B_HW, TQ), lambda i, p: (c0 // B_HPS + p, i)),
                   pl.BlockSpec((TQ, B_HW), lambda i, p: (i, p)),
                   pl.BlockSpec((B_HPS, CPB, LANES, LANES), lambda i, p: (p, i, 0, 0))],
        out_shape=[jax.ShapeDtypeStruct(ycat.shape, BF16), jax.ShapeDtypeStruct(ycat_t.shape, BF16),
                   jax.ShapeDtypeStruct((S, B_WIDTH), F32),
                   jax.ShapeDtypeStruct((B_HEADS, S // CHUNK, LANES, LANES), F32)],
        input_output_aliases={5: 0, 6: 1},
        scratch_shapes=[pltpu.VMEM((B_HEADS, LANES, LANES), F32)],
        compiler_params=_params(("arbitrary", "arbitrary")),
    )(h, h, gw, gb, gn, ycat, ycat_t)


def _gla_bwd(h, gw, gb, gn, o_pre, states, dycat, dh):
    S = h.shape[0]
    nb = S // TQ

    def body(qkvz_ref, lr_ref, gw_ref, gb_ref, gn_ref, o_ref, st_ref, dy_ref, dhin_ref,
             dh_ref, dlr_ref, dgw_ref, dgb_ref, dgn_ref,
             dst_scr, dgw_scr, dgb_scr, dgn_scr):
        del dhin_ref
        i, p = pl.program_id(0), pl.program_id(1)

        @pl.when(jnp.logical_and(i == 0, p == 0))
        def _():
            dgn_scr[...] = jnp.zeros_like(dgn_scr)

        dlr_heads = [one_head(hh, i, p, qkvz_ref, lr_ref, gw_ref, gb_ref, gn_ref, o_ref, st_ref, dy_ref,
                              dh_ref, dgw_ref, dgb_ref, dst_scr, dgw_scr, dgb_scr, dgn_scr) for hh in range(B_HPS)]
        dlr = dlr_heads[0]
        for more in dlr_heads[1:]:
            dlr = dlr + more

        @pl.when(p == 0)
        def _():
            dlr_ref[...] = dlr

        @pl.when(p > 0)
        def _():
            dlr_ref[...] += dlr

        @pl.when(i == nb - 1)
        def _():
            dgn_ref[...] = dgn_scr[...]

    def one_head(hh, i, p, qkvz_ref, lr_ref, gw_ref, gb_ref, gn_ref, o_ref, st_ref, dy_ref,
                 dh_ref, dgw_ref, dgb_ref, dst_scr, dgw_scr, dgb_scr, dgn_scr):
        hd = B_HPS * p + hh
        lane = slice(hh * LANES, (hh + 1) * LANES)

        @pl.when(i == 0)
        def _():
            dst_scr[hd] = jnp.zeros((LANES, LANES), F32)
            dgw_scr[hd] = jnp.zeros((LANES, LANES), F32)
            dgb_scr[hd] = jnp.zeros((1, LANES), F32)

        q = qkvz_ref[:, _head_cols(hh, SEG_Q)].astype(F32) * (B_DK ** -0.5)
        k = qkvz_ref[:, _head_cols(hh, SEG_K)].astype(F32)
        v = qkvz_ref[:, _head_cols(hh, SEG_V)]
        lr, gwv = lr_ref[...], gw_ref[:, lane]
        sg, b = _gla_gates(lr, gwv, gb_ref[:, lane])
        eb, enb, qp, qn, kp, kn = _gla_factors(q, k, b)
        a = _gla_intra(qp, qn, kp, kn)
        qpb, qnb, kpb, knb = qp.astype(BF16), qn.astype(BF16), kp.astype(BF16), kn.astype(BF16)

        o = o_ref[:, lane]
        gn = gn_ref[...]
        r = lax.rsqrt(jnp.mean(o * o, axis=1, keepdims=True) + RMS_EPS)
        z = qkvz_ref[:, _head_cols(hh, SEG_Z)].astype(F32)
        sz = _sigmoid(z)
        dy = dy_ref[:, lane].astype(F32)
        d_on = dy * (z * sz)
        dh_ref[:, _head_cols(hh, SEG_Z)] = (dy * (o * r * gn) * (sz * (1.0 + z * (1.0 - sz)))).astype(BF16)
        dgn_scr[...] += jnp.sum(d_on * o * r, axis=0, keepdims=True)
        t = d_on * gn
        do = r * t - o * (r * r * r) * jnp.mean(t * o, axis=1, keepdims=True)
        dob = do.astype(BF16)

        lo, up = _chunk_masks()
        da = _dot_nt(dob, v)
        dalo = jnp.where(lo, da, 0.0).astype(BF16)
        daup = jnp.where(up, da, 0.0).astype(BF16)
        dqp = _dot(dalo, knb)
        dkn = _dot_tn(dalo, qpb)
        dqn = _dot(daup, kpb)
        dkp = _dot_tn(daup, qnb)
        dv = _dot_tn(a.astype(BF16), dob)

        dst = dst_scr[hd]
        dqp_c, dkn_c, dv_c, dbl_c = [None] * CPB, [None] * CPB, [None] * CPB, [None] * CPB
        for c in reversed(range(CPB)):
            rows = slice(c * CHUNK, (c + 1) * CHUNK)
            st = st_ref[hh, c]
            e_last = jnp.exp(b[(c + 1) * CHUNK - 1:(c + 1) * CHUNK, :])
            if c == CPB - 1:
                st_next = (st + _dot_tn(v[rows], knb[rows])) * e_last
            else:
                st_next = st_ref[hh, c + 1]
            dbl_c[c] = jnp.sum(dst * st_next, axis=0, keepdims=True)
            dtt = (dst * e_last).astype(BF16)
            dv_c[c] = _dot_nt(knb[rows], dtt)
            dkn_c[c] = _dot(v[rows], dtt)
            dqp_c[c] = _dot(dob[rows], st.astype(BF16))
            dst = _dot_tn(dob[rows], qpb[rows]) + dst * e_last
        dst_scr[hd] = dst
        dqp = dqp + jnp.concatenate(dqp_c, axis=0)
        dkn = dkn + jnp.concatenate(dkn_c, axis=0)
        dv = dv + jnp.concatenate(dv_c, axis=0)
        dh_ref[:, _head_cols(hh, SEG_V)] = dv.astype(BF16)
        dh_ref[:, _head_cols(hh, SEG_Q)] = ((dqp * eb + dqn * enb) * (B_DK ** -0.5)).astype(BF16)
        dh_ref[:, _head_cols(hh, SEG_K)] = (dkp * eb + dkn * enb).astype(BF16)

        last = jnp.bitwise_and(lax.broadcasted_iota(jnp.int32, (TQ, 1), 0), CHUNK - 1) == CHUNK - 1
        dbl = jnp.concatenate([jnp.broadcast_to(x, (CHUNK, LANES)) for x in dbl_c], axis=0)
        db = dqp * qp - dqn * qn + dkp * kp - dkn * kn + jnp.where(last, dbl, 0.0)
        r0 = lax.broadcasted_iota(jnp.int32, (TQ, TQ), 0)
        r1 = lax.broadcasted_iota(jnp.int32, (TQ, TQ), 1)
        upper = jnp.logical_and(_chunk_of(r0) == _chunk_of(r1), r1 >= r0).astype(BF16)
        dlogit = _dot3(upper, db) * (1.0 / GATE_TAU) * (1.0 - sg)
        dlb = dlogit.astype(BF16)
        dgw_scr[hd] += _dot_tn(lr, dlb)
        dgb_scr[hd] += jnp.sum(dlogit, axis=0, keepdims=True)

        @pl.when(i == nb - 1)
        def _():
            dgw_ref[:, lane] = dgw_scr[hd]
            dgb_ref[:, lane] = dgb_scr[hd]

        return _dot_nt(dlb, gwv)

    qkvz_s, lr_s, gw_s, gb_s, gn_s, blk = _gla_specs(nb, True)
    row = pl.BlockSpec((TQ, B_HW), lambda i, p: (blk(i), p))
    dy_spec = pl.BlockSpec((TQ, B_HW), lambda i, p: (blk(i), A_WIDTH // B_HW + p))
    st_spec = pl.BlockSpec((B_HPS, CPB, LANES, LANES), lambda i, p: (p, blk(i), 0, 0))
    return pl.pallas_call(
        body,
        name="gla_bwd",
        grid=(nb, B_HEADS // B_HPS),
        in_specs=[qkvz_s, lr_s, gw_s, gb_s, gn_s, row, st_spec, dy_spec, ANY],
        out_specs=[pl.BlockSpec((TQ, 4 * B_HW), lambda i, p: (blk(i), C_B // (4 * B_HW) + p)),
                   pl.BlockSpec((TQ, LANES), lambda i, p: (blk(i), 0)),
                   pl.BlockSpec((LANES, B_HW), lambda i, p: (0, jnp.where(i == nb - 1, p, 0))),
                   pl.BlockSpec((1, B_HW), lambda i, p: (0, jnp.where(i == nb - 1, p, 0))),
                   pl.BlockSpec((1, LANES), lambda i, p: (0, 0))],
        out_shape=[jax.ShapeDtypeStruct(dh.shape, BF16),
                   jax.ShapeDtypeStruct((S, LANES), F32),
                   jax.ShapeDtypeStruct((LANES, B_HEADS * LANES), F32),
                   jax.ShapeDtypeStruct((1, B_HEADS * LANES), F32),
                   jax.ShapeDtypeStruct((1, LANES), F32)],
        input_output_aliases={8: 0},
        scratch_shapes=[pltpu.VMEM((B_HEADS, LANES, LANES), F32), pltpu.VMEM((B_HEADS, LANES, LANES), F32),
                        pltpu.VMEM((B_HEADS, 1, LANES), F32), pltpu.VMEM((1, LANES), F32)],
        compiler_params=_params(("arbitrary", "arbitrary")),
    )(h, h, gw, gb, gn, o_pre, states, dycat, dh)


def _lr_fill(dlr, dh):
    S = dlr.shape[0]
    w = HP - C_LR

    def body(dlr_ref, dhin_ref, dh_ref):
        del dhin_ref
        dh_ref[:, :LANES] = dlr_ref[...].astype(BF16)
        dh_ref[:, LANES:] = jnp.zeros((TQ, w - LANES), BF16)

    return pl.pallas_call(
        body, name="lr_fill", grid=(S // TQ,),
        in_specs=[pl.BlockSpec((TQ, LANES), lambda i: (i, 0)), ANY],
        out_specs=pl.BlockSpec((TQ, w), lambda i: (i, C_LR // w)),
        out_shape=jax.ShapeDtypeStruct(dh.shape, BF16),
        input_output_aliases={1: 0},
        compiler_params=_params(("parallel",)),
    )(dlr, dh)


LN_ROWS = 256


def _outproj_ln(ycat, w_out, x, g, b):
    S = x.shape[0]

    def body(yc_ref, w_ref, x_ref, g_ref, b_ref, y_ref, yb_ref, yt_ref, xh_ref, rs_ref):
        u = ALPHA * x_ref[...] + _dot(yc_ref[...], w_ref[...])
        mu = jnp.mean(u, axis=1, keepdims=True)
        d = u - mu
        rstd = lax.rsqrt(jnp.mean(d * d, axis=1, keepdims=True) + LN_EPS)
        xh = d * rstd
        y = xh * g_ref[...] + b_ref[...]
        y_ref[...] = y
        yb_ref[...] = y.astype(BF16)
        yt_ref[...] = y.T.astype(BF16)
        xh_ref[...] = xh
        rs_ref[...] = rstd

    row = lambda w: pl.BlockSpec((LN_ROWS, w), lambda i: (i, 0))
    vec = pl.BlockSpec((1, D_MODEL), lambda i: (0, 0))
    return pl.pallas_call(
        body,
        name="outproj_ln",
        grid=(S // LN_ROWS,),
        in_specs=[row(D_MODEL), pl.BlockSpec((D_MODEL, D_MODEL), lambda i: (0, 0)), row(D_MODEL), vec, vec],
        out_specs=[row(D_MODEL), row(D_MODEL), pl.BlockSpec((D_MODEL, LN_ROWS), lambda i: (0, i)), row(D_MODEL), row(1)],
        out_shape=[jax.ShapeDtypeStruct((S, D_MODEL), F32), jax.ShapeDtypeStruct((S, D_MODEL), BF16),
                   jax.ShapeDtypeStruct((D_MODEL, S), BF16),
                   jax.ShapeDtypeStruct((S, D_MODEL), F32), jax.ShapeDtypeStruct((S, 1), F32)],
        compiler_params=_params(("parallel",)),
    )(ycat, w_out, x, g, b)


def _ln_bwd(dy, xhat, rstd, g):
    S = dy.shape[0]

    def body(dy_ref, xh_ref, rs_ref, g_ref, du_ref, dub_ref, dg_ref, db_ref):
        i = pl.program_id(0)
        dy_, xh = dy_ref[...], xh_ref[...]
        dyg = dy_ * g_ref[...]
        m1 = jnp.mean(dyg, axis=1, keepdims=True)
        m2 = jnp.mean(dyg * xh, axis=1, keepdims=True)
        du = rs_ref[...] * (dyg - m1 - xh * m2)
        du_ref[...] = du
        dub_ref[...] = du.astype(BF16)
        dg = jnp.sum(dy_ * xh, axis=0, keepdims=True)
        db = jnp.sum(dy_, axis=0, keepdims=True)

        @pl.when(i == 0)
        def _():
            dg_ref[...] = dg
            db_ref[...] = db

        @pl.when(i > 0)
        def _():
            dg_ref[...] += dg
            db_ref[...] += db

    row = lambda w: pl.BlockSpec((TQ, w), lambda i: (i, 0))
    vec = pl.BlockSpec((1, D_MODEL), lambda i: (0, 0))
    return pl.pallas_call(
        body,
        name="ln_bwd",
        grid=(S // TQ,),
        in_specs=[row(D_MODEL), row(D_MODEL), row(1), vec],
        out_specs=[row(D_MODEL), row(D_MODEL), vec, vec],
        out_shape=[jax.ShapeDtypeStruct((S, D_MODEL), F32), jax.ShapeDtypeStruct((S, D_MODEL), BF16),
                   jax.ShapeDtypeStruct((1, D_MODEL), F32), jax.ShapeDtypeStruct((1, D_MODEL), F32)],
        compiler_params=_params(("arbitrary",)),
    )(dy, xhat, rstd, g)


def _loss_head(y, target):
    S = y.shape[0]

    def body(y_ref, t_ref, l_ref, dy_ref):
        i = pl.program_id(0)
        err = y_ref[...] - t_ref[...]
        dy_ref[...] = err * (1.0 / D_MODEL)
        part = (0.5 / D_MODEL) * jnp.sum(jnp.sum(err * err, axis=1, keepdims=True), axis=0, keepdims=True)

        @pl.when(i == 0)
        def _():
            l_ref[...] = part

        @pl.when(i > 0)
        def _():
            l_ref[...] += part

    row = pl.BlockSpec((TQ, D_MODEL), lambda i: (i, 0))
    return pl.pallas_call(
        body,
        name="loss_head",
        grid=(S // TQ,),
        in_specs=[row, row],
        out_specs=[pl.BlockSpec((1, 1), lambda i: (0, 0)), row],
        out_shape=[jax.ShapeDtypeStruct((1, 1), F32), jax.ShapeDtypeStruct((S, D_MODEL), F32)],
        compiler_params=_params(("arbitrary",)),
    )(y, target)


def _pad_gate(gate_w, gate_b):
    gw = gate_w.reshape(GATE_RANK, B_HEADS, B_DK)
    gw = jnp.pad(gw, ((0, LANES - GATE_RANK), (0, 0), (0, LANES - B_DK))).reshape(LANES, B_HEADS * LANES)
    gb = jnp.pad(gate_b.reshape(B_HEADS, B_DK), ((0, 0), (0, LANES - B_DK))).reshape(1, B_HEADS * LANES)
    return gw.astype(BF16), gb.astype(F32)


def _layer_fwd(x, xb, xt, mem_b, w_in, w_kv, w_out, u, gw, gb, gn, ln_g, ln_b, rider=None):
    h = _matmul(xb, w_in, mode="nn", out_dtype=BF16, tm=1024, tn=768, tk=D_MODEL, name="in_proj", rider=rider)
    if rider is not None:
        h, rode = h
    mkv = _matmul(mem_b, w_kv, mode="nn", out_dtype=BF16, tm=mem_b.shape[0], tn=1024, tk=D_MODEL, name="mem_kv")
    ycat, ycat_t = _band_fwd(h, u)
    ycat, ycat_t, o_pre, states = _gla_fwd(h, gw, gb, gn, ycat, ycat_t)
    ycat, ycat_t = _mem_fwd(h, mkv, ycat, ycat_t)
    y, ybf, yt, xhat, rstd = _outproj_ln(ycat, w_out, x, ln_g, ln_b)
    saved = (xt, h, mkv, ycat_t, o_pre, states, xhat, rstd)
    return (y, ybf, yt, saved) if rider is None else (y, ybf, yt, saved, rode)


def _layer_bwd(dy, saved, mem_b, w_in, w_out, u, gw, gb, gn, ln_g, reduce=None, own_reduce=None):
    xt, h, mkv, ycat_t, o_pre, states, xhat, rstd = saved

    def riding(**kw):
        if reduce is None:
            return _matmul(**kw)
        out, bufs = _matmul(rider=reduce.rider(), **kw)
        reduce.landed(bufs)
        return out

    du, dub, d_ln_g, d_ln_b = _ln_bwd(dy, xhat, rstd, ln_g)
    dycat = riding(a=dub, b=w_out, mode="nt", out_dtype=BF16, tm=1024, tn=1024, tk=D_MODEL, name="dycat")
    d_w_out = _matmul(ycat_t, dub, mode="nn", out_dtype=F32, tm=1024, tn=1024, tk=min(4096, dub.shape[0]), name="d_w_out")
    dh, d_u = _band_bwd(h, u, dycat)
    dh, dlr, dgw, dgb, dgn = _gla_bwd(h, gw, gb, gn, o_pre, states, dycat, dh)
    dh, dmkv = _mem_bwd(h, mkv, dycat, dh)
    dh = _lr_fill(dlr, dh)
    d_w_kv = _matmul(mem_b, dmkv, mode="tn", out_dtype=F32, tm=1024, tn=1024, tk=mem_b.shape[0], name="d_w_kv")
    dx_args = dict(a=dh, b=w_in, mode="nt", out_dtype=F32, tm=1024, tn=1024, tk=3840, name="dx", add=du, add_scale=ALPHA)
    dw_args = dict(a=xt, b=dh, mode="nn", out_dtype=F32, tm=1024, tn=768, tk=min(4096, dh.shape[0]), name="d_w_in")
    if own_reduce is None:
        dx = riding(**dx_args)
        d_w_in = riding(**dw_args)
        return dx, (d_w_in, d_u, dgw, dgb, dgn, d_w_kv, d_w_out, d_ln_g, d_ln_b)
    d_w_in = riding(**dw_args)
    grads = (d_w_in, d_u, dgw, dgb, dgn, d_w_kv, d_w_out, d_ln_g, d_ln_b)
    own = own_reduce(grads, reduce.finish() if reduce is not None else None)
    own.step()
    dx, bufs = _matmul(rider=own.rider(), **dx_args)
    own.landed(bufs)
    return dx, grads, own


def _unpad_heads(w):
    r = w.shape[0]
    return w.reshape(r, B_HEADS, LANES)[:, :, :B_DK].reshape(r, B_KEY_WIDTH)


def _padded_col_of():
    col, o = np.zeros(IN_WIDTH, np.int64), 0
    for seg in (SEG_Q, SEG_K, SEG_V, SEG_Z):
        for hd in range(A_HEADS):
            col[o:o + LANES] = C_A + (hd // A_HPS) * 4 * A_HW + seg * A_HW + (hd % A_HPS) * LANES + np.arange(LANES)
            o += LANES
    for seg, width in ((SEG_Q, B_DK), (SEG_K, B_DK), (SEG_V, LANES), (SEG_Z, LANES)):
        for hd in range(B_HEADS):
            col[o:o + width] = C_B + hd * 4 * LANES + seg * LANES + np.arange(width)
            o += width
    col[o:o + GATE_RANK] = C_LR + np.arange(GATE_RANK)
    o += GATE_RANK
    for seg in (0, 1):
        for hd in range(M_HEADS):
            col[o:o + LANES] = C_M + hd * 2 * LANES + seg * LANES + np.arange(LANES)
            o += LANES
    assert o == IN_WIDTH
    return col


def _runs(idx):
    out, start = [], 0
    for k in range(1, len(idx) + 1):
        if k == len(idx) or idx[k] != idx[k - 1] + 1:
            out.append((int(idx[start]), k - start))
            start = k
    return out


def _chip_columns(g, j, n):
    runs = _runs(_padded_col_of()[j * n:(j + 1) * n])
    return jnp.concatenate([g[:, a:a + ln] for a, ln in runs], axis=1)


def _padded_from_shards(shards):
    n = shards[0].shape[1]
    src = np.full(HP, -1, np.int64)
    src[_padded_col_of()] = np.arange(IN_WIDTH)
    parts, k = [], 0
    while k < HP:
        e = k + 1
        if src[k] < 0:
            while e < HP and src[e] < 0:
                e += 1
            parts.append(jnp.zeros((shards[0].shape[0], e - k), shards[0].dtype))
        else:
            while e < HP and src[e] == src[e - 1] + 1 and src[e] // n == src[k] // n:
                e += 1
            parts.append(shards[src[k] // n][:, src[k] % n:src[k] % n + e - k])
        k = e
    return jnp.concatenate(parts, axis=1)


ADAMW_BLOCK_BYTES = 1 << 20


def _adamw(w, g, m, v, name):
    L, R, C = w.shape
    tl, tr = 1, R
    if R * C * 4 <= ADAMW_BLOCK_BYTES:
        tl = max(d for d in range(1, L + 1) if L % d == 0 and d * R * C * 4 <= ADAMW_BLOCK_BYTES)
    else:
        for cand in (256, 128, 64, 32, 16, 8):
            if R % cand == 0 and R > cand:
                tr = cand
                break

    def body(w_ref, g_ref, m_ref, v_ref, d_ref, nm_ref, nv_ref):
        g_ = g_ref[...]
        nm = ADAM_B1 * m_ref[...] + (1.0 - ADAM_B1) * g_
        nv = ADAM_B2 * v_ref[...] + (1.0 - ADAM_B2) * (g_ * g_)
        m_hat = nm / (1.0 - ADAM_B1 ** ADAM_STEP)
        v_hat = nv / (1.0 - ADAM_B2 ** ADAM_STEP)
        d_ref[...] = -ADAM_LR * (m_hat / (jnp.sqrt(v_hat) + ADAM_EPS) + ADAM_WD * w_ref[...])
        nm_ref[...] = nm
        nv_ref[...] = nv

    spec = pl.BlockSpec((tl, tr, C), lambda l, i: (l, i, 0))
    sd = jax.ShapeDtypeStruct((L, R, C), F32)
    return pl.pallas_call(
        body, name=name, grid=(L // tl, R // tr), in_specs=[spec] * 4, out_specs=[spec] * 3, out_shape=[sd] * 3,
        compiler_params=_params(("parallel", "parallel")),
    )(w, g, m, v)


def _adamw_nd(w, g, m, v, name):
    shape = w.shape
    f = (lambda a: a) if w.ndim == 3 else (lambda a: a.reshape(1, shape[0], shape[1]))
    return tuple(o.reshape(shape) for o in _adamw(f(w), f(g), f(m), f(v), name))


ANY = pl.BlockSpec(memory_space=pl.ANY)


def _place():
    x, y, c = lax.axis_index("x"), lax.axis_index("y"), lax.axis_index("c")
    chips = [(1 - x, y), (x, 1 - y), (1 - x, 1 - y)]
    return x, y, c, chips


class _WeightGather:
    def __init__(self, hop, layer, rows):
        self.hop, self.layer, self.rows = hop, layer, rows
        self.n_sem = 3 * len(rows)

    def _copies(self, shard_refs, buf_refs, send, recv, received):
        x, y, c, chips = _place()
        out = []
        for t, R in enumerate(self.rows):
            half = R // 2
            assert half % 16 == 0
            mine = pl.ds(pl.multiple_of(c * half, 16), half)
            other = pl.ds(pl.multiple_of((1 - c) * half, 16), half)
            mine_of_shard = pl.ds(pl.multiple_of(self.layer * R + c * half, 16), half)
            for k, chip in enumerate(chips):
                theirs = buf_refs[t].at[2 * chip[0] + chip[1]]
                if self.hop == "chips":
                    src, dst, to = shard_refs[t].at[mine_of_shard], buf_refs[t].at[2 * x + y, mine], (*chip, c)
                    landed = theirs.at[mine]
                else:
                    src, dst, to = theirs.at[mine], theirs.at[mine], (x, y, 1 - c)
                    landed = theirs.at[other]
                out.append(pltpu.make_async_remote_copy(
                    src_ref=src, dst_ref=landed if received else dst, send_sem=send.at[3 * t + k],
                    recv_sem=recv.at[3 * t + k], device_id=to, device_id_type=MESH))
        return out

    def start(self, shard_refs, buf_refs, send, recv):
        for cp in self._copies(shard_refs, buf_refs, send, recv, False):
            cp.start()

    def wait(self, shard_refs, buf_refs, send, recv):
        for cp in self._copies(shard_refs, buf_refs, send, recv, True):
            cp.wait_recv()
        for cp in self._copies(shard_refs, buf_refs, send, recv, False):
            cp.wait_send()

    def sems(self):
        return [pltpu.SemaphoreType.DMA((self.n_sem,)), pltpu.SemaphoreType.DMA((self.n_sem,))]

    def call(self, srcs, bufs, name):
        ns, nb = len(srcs), len(bufs)

        def body(*refs):
            src_refs, buf_refs, (send, recv) = refs[:ns], refs[ns + nb:ns + 2 * nb], refs[ns + 2 * nb:]
            self.start(src_refs, buf_refs, send, recv)
            self.wait(src_refs, buf_refs, send, recv)

        return pl.pallas_call(
            body, name=name, in_specs=[ANY] * (ns + nb), out_specs=[ANY] * nb,
            out_shape=[jax.ShapeDtypeStruct(b.shape, b.dtype) for b in bufs],
            input_output_aliases={ns + t: t for t in range(nb)},
            scratch_shapes=self.sems(),
        )(*srcs, *bufs)


class _GradHop(_WeightGather):
    def __init__(self, hop, layer, slices):
        self.hop, self.layer, self.slices = hop, layer, slices
        self.n_sem = {"pair": N_CHIPS, "chips": N_CHIPS - 1, "gather": 1}[hop] * len(slices)

    def _copies(self, src_refs, buf_refs, send, recv, received):
        x, y, c, chips = _place()
        me, out = 2 * x + y, []

        def remote(src, dst, to):
            k = len(out)
            out.append(pltpu.make_async_remote_copy(src_ref=src, dst_ref=dst, send_sem=send.at[k], recv_sem=recv.at[k],
                                                    device_id=to, device_id_type=MESH))

        for t, (half, where) in enumerate(self.slices):
            if self.hop == "pair":
                for j, (a, first) in enumerate(where):
                    rows = pl.ds(pl.multiple_of(first + (1 - c) * half, 8), half)
                    remote(src_refs[a].at[rows], buf_refs[t].at[j], (x, y, 1 - c))
            elif self.hop == "chips":
                for chip in chips:
                    slot = 2 * chip[0] + chip[1]
                    remote(src_refs[t].at[slot], buf_refs[t].at[slot if received else me], (*chip, c))
            else:
                mine = buf_refs[t].at[self.layer, c]
                remote(mine, buf_refs[t].at[self.layer, 1 - c] if received else mine, (x, y, 1 - c))
        return out


def _add_halves(parts, got, c_idx, name):
    n, L, half, C = got.shape
    tr = 64

    def body(*refs):
        ins, (got_ref, o_ref) = refs[1:1 + len(parts)], refs[1 + len(parts):]
        for k in range(len(parts)):
            o_ref[k // L, k % L] = (ins[k][...] + got_ref[k // L, k % L]).astype(BF16)

    def rows_of(first):
        assert first % tr == 0 and half % tr == 0
        return lambda i, c: (first // tr + c[0] * (half // tr) + i, 0)

    whole = pl.BlockSpec((n, L, tr, C), lambda i, c: (0, 0, i, 0))
    return pl.pallas_call(
        body, name=name,
        grid_spec=pltpu.PrefetchScalarGridSpec(
            num_scalar_prefetch=1, grid=(half // tr,),
            in_specs=[pl.BlockSpec((tr, C), rows_of(first)) for _, first in parts] + [whole],
            out_specs=whole),
        out_shape=jax.ShapeDtypeStruct((n, L, half, C), BF16),
        compiler_params=_params(("parallel",)),
    )(c_idx, *[a for a, _ in parts], got)


def _add_slots(r, c_idx, dest, layer, name):
    n, half, C = r.shape
    tr = 256

    def body(c_ref, r_ref, dest_ref, o_ref):
        del dest_ref
        acc = r_ref[0].astype(F32)
        for j in range(1, n):
            acc = acc + r_ref[j].astype(F32)
        o_ref[0, 0] = acc

    return pl.pallas_call(
        body, name=name,
        grid_spec=pltpu.PrefetchScalarGridSpec(
            num_scalar_prefetch=1, grid=(half // tr,),
            in_specs=[pl.BlockSpec((n, tr, C), lambda i, c: (0, i, 0)), ANY],
            out_specs=pl.BlockSpec((1, 1, tr, C), lambda i, c: (layer, c[0], i, 0))),
        out_shape=jax.ShapeDtypeStruct(dest.shape, F32),
        input_output_aliases={2: 0},
        compiler_params=_params(("parallel",)),
    )(c_idx, r, dest)


class _LayerReduce:
    def __init__(self, layer, grads, slices, c_idx, chip, dests):
        self.layer, self.grads, self.slices, self.c_idx, self.chip, self.dests = layer, grads, slices, c_idx, chip, dests
        self.widths = [grads[where[0][0]].shape[1] for _, where in slices]
        self.stage = 0

    def _hop(self, kind):
        return _GradHop(kind, self.layer, self.slices)

    def rider(self):
        if self.stage == 0:
            got = [lax.empty((N_CHIPS, half, w), F32) for (half, _), w in zip(self.slices, self.widths)]
            return self._hop("pair"), self.grads, got
        if self.stage == 1:
            q = [lax.empty(p.shape, BF16) for p in self.pair_sums]
            return self._hop("chips"), self.pair_sums, q
        return self._hop("gather"), [], self.dests

    def landed(self, bufs):
        tag = f"{self.layer}"
        if self.stage == 0:
            self.pair_sums = []
            for t, ((half, where), got) in enumerate(zip(self.slices, bufs)):
                parts = [(self.grads[a], first) for a, first in where]
                p = _add_halves(parts, got[:, None], self.c_idx, f"rs_add2_{t}_{tag}")
                self.pair_sums.append(p.reshape(N_CHIPS, half, p.shape[-1]))
        elif self.stage == 1:
            for t, (q, p) in enumerate(zip(bufs, self.pair_sums)):
                q = lax.dynamic_update_slice_in_dim(q, lax.dynamic_slice_in_dim(p, self.chip, 1, axis=0), self.chip, axis=0)
                self.dests[t] = _add_slots(q, self.c_idx, self.dests[t], self.layer, f"rs_add4_{t}_{tag}")
        else:
            self.dests = list(bufs)
        self.stage += 1

    def step(self):
        hop, srcs, bufs = self.rider()
        self.landed(hop.call(srcs, bufs, f"rs_{hop.hop}_{self.layer}"))

    def finish(self):
        while self.stage < 3:
            self.step()
        return self.dests


def _all_reduce_small(buf, name):
    R = buf.shape[0]

    def flipped(k, x, y, c):
        return ((1 - x) if k & 4 else x, (1 - y) if k & 2 else y, (1 - c) if k & 1 else c)

    def body(b_ref, o_ref, land, send, recv):
        x, y, c, _ = _place()
        me = 4 * x + 2 * y + c
        land[me] = b_ref[...]
        cps = []
        for k in range(1, N_DEV):
            peer = flipped(k, x, y, c)
            cps.append(pltpu.make_async_remote_copy(src_ref=b_ref, dst_ref=land.at[me], send_sem=send.at[k - 1],
                                                    recv_sem=recv.at[k - 1], device_id=peer, device_id_type=MESH))
        for cp in cps:
            cp.start()
        for k in range(1, N_DEV):
            peer = flipped(k, x, y, c)
            slot = 4 * peer[0] + 2 * peer[1] + peer[2]
            pltpu.make_async_remote_copy(src_ref=b_ref, dst_ref=land.at[slot], send_sem=send.at[k - 1],
                                         recv_sem=recv.at[k - 1], device_id=peer, device_id_type=MESH).wait_recv()
        for cp in cps:
            cp.wait_send()
        acc = land[0]
        for j in range(1, N_DEV):
            acc = acc + land[j]
        o_ref[...] = acc

    vm = pl.BlockSpec(memory_space=pltpu.VMEM)
    return pl.pallas_call(
        body, name=name, in_specs=[vm], out_specs=vm,
        out_shape=jax.ShapeDtypeStruct((R, LANES), F32),
        scratch_shapes=[pltpu.VMEM((N_DEV, R, LANES), F32), pltpu.SemaphoreType.DMA((N_DEV - 1,)),
                        pltpu.SemaphoreType.DMA((N_DEV - 1,))],
    )(buf)


def kernel(x, mem, w_in, a_rel_bias, b_gate_w, b_gate_b, b_norm_g, w_mem_kv, w_out, ln_g, ln_b, loss_target, m_w_in, m_a_rel_bias, m_b_gate_w, m_b_gate_b, m_b_norm_g, m_w_mem_kv, m_w_out, m_ln_g, m_ln_b, v_w_in, v_a_rel_bias, v_b_gate_w, v_b_gate_b, v_b_norm_g, v_w_mem_kv, v_w_out, v_ln_g, v_ln_b):
    L = w_in.shape[0]
    S = x.shape[1]
    cx, cy, cc = lax.axis_index("x"), lax.axis_index("y"), lax.axis_index("c")
    chip = 2 * cx + cy
    c_idx = jnp.reshape(cc, (1,)).astype(jnp.int32)

    n_in, r_kv, r_out = w_in.shape[2], w_mem_kv.shape[1], w_out.shape[1]
    shards = [w.astype(BF16).reshape(-1, w.shape[2]) for w in (w_in, w_mem_kv, w_out)]
    rows = [D_MODEL, r_kv, r_out]

    def landing(l):
        return [lax.dynamic_update_slice_in_dim(lax.empty((N_CHIPS, r, s.shape[1]), BF16),
                                                s[l * r:(l + 1) * r][None], chip, axis=0)
                for s, r in zip(shards, rows)]

    def assembled(bufs):
        return (_padded_from_shards([bufs[0][j] for j in range(N_CHIPS)]),
                bufs[1].reshape(D_MODEL, bufs[1].shape[2]), bufs[2].reshape(D_MODEL, D_MODEL))

    bufs0 = _WeightGather("chips", 0, rows).call(shards, landing(0), "gather_chips_0")
    weights = [assembled(_WeightGather("pair", 0, rows).call(shards, bufs0, "gather_pair_0"))]

    gw_cols = b_gate_w.shape[2]
    gw_slot = jnp.zeros((N_CHIPS, L, GATE_RANK, gw_cols), F32)
    gw_slot = lax.dynamic_update_slice(gw_slot, (0.5 * b_gate_w)[None], (chip, 0, 0, 0))
    gw_flat = gw_slot.reshape(-1)
    n_gw = gw_flat.shape[0]
    pad = (-n_gw) % (8 * LANES)
    gw_full = _all_reduce_small(jnp.pad(gw_flat, (0, pad)).reshape(-1, LANES), "gather_gate_w").reshape(-1)[:n_gw]
    gw_full = gw_full.reshape(N_CHIPS, L, GATE_RANK, gw_cols).transpose(1, 2, 0, 3).reshape(L, GATE_RANK, B_KEY_WIDTH)

    xs = x.reshape(S, D_MODEL)
    mem_b = mem.reshape(mem.shape[1], D_MODEL).astype(BF16)
    target = loss_target.reshape(S, D_MODEL)

    small_w = []
    for l in range(L):
        gw_l, gb_l = _pad_gate(gw_full[l], b_gate_b[l])
        small_w.append((_bias_by_offset(a_rel_bias[l]), gw_l, gb_l,
                        b_norm_g[l].reshape(1, LANES), ln_g[l].reshape(1, D_MODEL), ln_b[l].reshape(1, D_MODEL)))

    y, yb = xs, xs.astype(BF16)
    yt = _transpose(yb, "x_t")
    saved = []
    for l in range(L):
        if l + 1 < L:
            rider = (_WeightGather("chips", l + 1, rows), shards, landing(l + 1))
            y, yb, yt, sv, bufs = _layer_fwd(y, yb, yt, mem_b, *weights[l], *small_w[l], rider=rider)
            weights.append(assembled(_WeightGather("pair", l + 1, rows).call(shards, bufs, f"gather_pair_{l + 1}")))
        else:
            y, yb, yt, sv = _layer_fwd(y, yb, yt, mem_b, *weights[l], *small_w[l])
        saved.append(sv)
    layer_w = [(*weights[l], *small_w[l]) for l in range(L)]
    loss_part, dy = _loss_head(y, target)

    halves = [D_MODEL // 2, r_kv // 2, r_out // 2]
    dests = [lax.empty((L, 2, hf, w.shape[2]), F32) for hf, w in zip(halves, (w_in, w_mem_kv, w_out))]
    slices = [(halves[0], [(j, 0) for j in range(N_CHIPS)]),
              (halves[1], [(N_CHIPS, j * r_kv) for j in range(N_CHIPS)]),
              (halves[2], [(N_CHIPS + 1, j * r_out) for j in range(N_CHIPS)])]

    def reduction(l, g, into):
        arrays = [_chip_columns(g[0], j, n_in) for j in range(N_CHIPS)] + [g[5], g[6]]
        return _LayerReduce(l, arrays, slices, c_idx, chip, into)

    grads, reduce = [None] * L, None
    for l in reversed(range(L)):
        w_in_l, w_kv_l, w_out_l, u_l, gw_l, gb_l, gn_l, lg_l, lb_l = layer_w[l]
        args = (dy, saved[l], mem_b, w_in_l, w_out_l, u_l, gw_l, gb_l, gn_l, lg_l)
        if l > 0:
            dy, grads[l] = _layer_bwd(*args, reduce=reduce)
            if reduce is not None:
                dests = reduce.finish()
            reduce = reduction(l, grads[l], dests)
        else:
            own = lambda g, above: reduction(0, g, dests if above is None else above)
            dy, grads[l], reduce = _layer_bwd(*args, reduce=reduce, own_reduce=own)
    r_w_in, r_w_kv, r_w_out = [d.reshape(L, 2 * d.shape[2], d.shape[3]) for d in reduce.finish()]
    grad_x = dy.reshape(x.shape)

    g_rel = jnp.stack([_bias_grad_from_offset(g[1]) for g in grads])
    g_gw = jnp.stack([_unpad_heads(g[2][:GATE_RANK]) for g in grads])
    g_gb = jnp.stack([_unpad_heads(g[3])[0] for g in grads])
    g_gn = jnp.stack([g[4][0] for g in grads])
    g_lg = jnp.stack([g[7][0] for g in grads])
    g_lb = jnp.stack([g[8][0] for g in grads])

    small = [g_rel, g_gw, g_gb, g_gn, g_lg, g_lb, loss_part]
    flat = jnp.concatenate([s.reshape(-1) for s in small])
    n_small = flat.shape[0]
    pad = (-n_small) % (8 * LANES)
    red = _all_reduce_small(jnp.pad(flat, (0, pad)).reshape(-1, LANES), "all_reduce_small").reshape(-1)
    outs, off = [], 0
    for s in small:
        outs.append(red[off:off + s.size].reshape(s.shape))
        off += s.size
    g_rel, g_gw, g_gb, g_gn, g_lg, g_lb, loss = outs
    loss = loss.reshape(())
    g_gw = lax.dynamic_slice_in_dim(g_gw.reshape(L, GATE_RANK, N_CHIPS, gw_cols), chip, 1, axis=2).reshape(L, GATE_RANK, gw_cols)

    g_list = [r_w_in, g_rel, g_gw, g_gb, g_gn, r_w_kv, r_w_out, g_lg, g_lb]
    w_list = [w_in, a_rel_bias, b_gate_w, b_gate_b, b_norm_g, w_mem_kv, w_out, ln_g, ln_b]
    m_list = [m_w_in, m_a_rel_bias, m_b_gate_w, m_b_gate_b, m_b_norm_g, m_w_mem_kv, m_w_out, m_ln_g, m_ln_b]
    v_list = [v_w_in, v_a_rel_bias, v_b_gate_w, v_b_gate_b, v_b_norm_g, v_w_mem_kv, v_w_out, v_ln_g, v_ln_b]
    names = ["w_in", "rel", "gate_w", "gate_b", "norm_g", "w_kv", "w_out", "ln_g", "ln_b"]
    to_cols = lambda a: jnp.transpose(a, (2, 0, 1))
    upd = [tuple(jnp.transpose(o, (1, 2, 0)) for o in
                 _adamw(to_cols(w_in), to_cols(r_w_in), to_cols(m_w_in), to_cols(v_w_in), "adamw_w_in"))]
    upd += [_adamw_nd(w, g, m, v, "adamw_" + n)
            for w, g, m, v, n in list(zip(w_list, g_list, m_list, v_list, names))[1:]]
    deltas = [u_[0] for u_ in upd]
    new_m = [u_[1] for u_ in upd]
    new_v = [u_[2] for u_ in upd]
    return (loss, grad_x, *g_list, *deltas, *new_m, *new_v)
```

```python
import functools

import numpy as np
import jax
import jax.numpy as jnp
from jax import lax
from jax.experimental import pallas as pl
from jax.experimental.pallas import tpu as pltpu

F32 = jnp.float32
BF16 = jnp.bfloat16
MESH = pl.DeviceIdType.MESH

D_MODEL = 2048
DEPTH = 4
CHUNK = 64
LEFT_CHUNKS = 8
MAX_REL = 128
N_REL = 2 * MAX_REL + 1
A_HEADS = 8
HEAD_DIM = 128
B_HEADS = 4
B_DK = 64
M_HEADS = 4
GATE_RANK = 16
GATE_TAU = 16.0
A_WIDTH = A_HEADS * HEAD_DIM
B_WIDTH = B_HEADS * HEAD_DIM
B_KEY_WIDTH = B_HEADS * B_DK
M_WIDTH = M_HEADS * HEAD_DIM
IN_WIDTH = 4 * A_WIDTH + 2 * B_KEY_WIDTH + 2 * B_WIDTH + GATE_RANK + 2 * M_WIDTH
ALPHA = (2.0 * DEPTH) ** 0.25
LN_EPS = 1e-5
RMS_EPS = 1e-6
NEG_INF = -1e30
ADAM_LR = 0.001
ADAM_B1 = 0.9
ADAM_B2 = 0.999
ADAM_EPS = 1e-08
ADAM_WD = 0.01
ADAM_STEP = 10

LANES = 128
VMEM_LIMIT = 56 * 1024 * 1024

C_A, C_B, C_M, C_LR = 0, 4096, 6144, 7168
HP = 7680
SEG_Q, SEG_K, SEG_V, SEG_Z = 0, 1, 2, 3
TQ = 512
CPB = TQ // CHUNK
N_CHIPS = 4
N_DEV = 8


def _params(sem, vmem=VMEM_LIMIT):
    return pltpu.CompilerParams(dimension_semantics=sem, vmem_limit_bytes=vmem)


def _dot(a, b):
    return jnp.dot(a, b, preferred_element_type=F32)


def _dot_nt(a, b):
    return lax.dot_general(a, b, (((1,), (1,)), ((), ())), preferred_element_type=F32)


def _dot_tn(a, b):
    return lax.dot_general(a, b, (((0,), (0,)), ((), ())), preferred_element_type=F32)


def _sigmoid(x):
    return 1.0 / (1.0 + jnp.exp(-x))


def _split3(x):
    hi = x.astype(BF16)
    r = x - hi.astype(F32)
    mid = r.astype(BF16)
    lo = (r - mid.astype(F32)).astype(BF16)
    return hi, mid, lo


def _dot3(m_bf, x):
    hi, mid, lo = _split3(x)
    return _dot(m_bf, hi) + _dot(m_bf, mid) + _dot(m_bf, lo)


def _matmul(a, b, *, mode, out_dtype, tm, tn, tk, name, add=None, add_scale=1.0, rider=None):
    if mode == "nn":
        (M, K), (K2, N) = a.shape, b.shape
        a_spec = pl.BlockSpec((tm, tk), lambda i, j, k: (i, k))
        b_spec = pl.BlockSpec((tk, tn), lambda i, j, k: (k, j))
        dot = _dot
    elif mode == "nt":
        (M, K), (N, K2) = a.shape, b.shape
        a_spec = pl.BlockSpec((tm, tk), lambda i, j, k: (i, k))
        b_spec = pl.BlockSpec((tn, tk), lambda i, j, k: (j, k))
        dot = _dot_nt
    else:
        (K, M), (K2, N) = a.shape, b.shape
        a_spec = pl.BlockSpec((tk, tm), lambda i, j, k: (k, i))
        b_spec = pl.BlockSpec((tk, tn), lambda i, j, k: (k, j))
        dot = _dot_tn
    assert K == K2 and M % tm == 0 and N % tn == 0 and K % tk == 0, (a.shape, b.shape, mode)
    nk = K // tk
    has_add = add is not None
    assert nk == 1 or out_dtype == F32
    grid = (M // tm, N // tn, nk)
    hop, srcs, bufs = rider if rider is not None else (None, [], [])
    n_in = 2 + has_add

    def body(*refs):
        a_ref, b_ref = refs[:2]
        add_ref = refs[2] if has_add else None
        src_refs = refs[n_in:n_in + len(srcs)]
        o_ref = refs[n_in + len(srcs) + len(bufs)]
        buf_refs = refs[n_in + len(srcs) + len(bufs) + 1:n_in + len(srcs) + 2 * len(bufs) + 1]
        sems = refs[n_in + len(srcs) + 2 * len(bufs) + 1:]
        i, j, k = pl.program_id(0), pl.program_id(1), pl.program_id(2)
        if hop is not None:
            @pl.when(jnp.logical_and(jnp.logical_and(i == 0, j == 0), k == 0))
            def _():
                hop.start(src_refs, buf_refs, *sems)

        part = dot(a_ref[...].astype(BF16), b_ref[...].astype(BF16))

        @pl.when(k == 0)
        def _():
            first = part + add_scale * add_ref[...] if has_add else part
            o_ref[...] = first.astype(out_dtype)

        if nk > 1:
            @pl.when(k > 0)
            def _():
                o_ref[...] += part

        if hop is not None:
            @pl.when(jnp.logical_and(jnp.logical_and(i == grid[0] - 1, j == grid[1] - 1), k == nk - 1))
            def _():
                hop.wait(src_refs, buf_refs, *sems)

    in_specs = [a_spec, b_spec]
    args = [a, b]
    if has_add:
        in_specs.append(pl.BlockSpec((tm, tn), lambda i, j, k: (i, j)))
        args.append(add)
    out = pl.pallas_call(
        body,
        name=name,
        grid=grid,
        in_specs=in_specs + [ANY] * (len(srcs) + len(bufs)),
        out_specs=[pl.BlockSpec((tm, tn), lambda i, j, k: (i, j))] + [ANY] * len(bufs),
        out_shape=[jax.ShapeDtypeStruct((M, N), out_dtype)] + [jax.ShapeDtypeStruct(x.shape, x.dtype) for x in bufs],
        input_output_aliases={n_in + len(srcs) + t: 1 + t for t in range(len(bufs))},
        scratch_shapes=hop.sems() if hop is not None else [],
        compiler_params=_params(("parallel", "parallel", "arbitrary") if hop is None
                                else ("arbitrary", "arbitrary", "arbitrary")),
    )(*args, *srcs, *bufs)
    return out[0] if hop is None else (out[0], list(out[1:]))


def _transpose(a, name):
    R, C = a.shape
    t = 512

    def body(a_ref, o_ref):
        o_ref[...] = a_ref[...].T

    return pl.pallas_call(
        body, name=name, grid=(R // t, C // t),
        in_specs=[pl.BlockSpec((t, t), lambda i, j: (i, j))],
        out_specs=pl.BlockSpec((t, t), lambda i, j: (j, i)),
        out_shape=jax.ShapeDtypeStruct((C, R), a.dtype),
        compiler_params=_params(("parallel", "parallel")),
    )(a)


def _chunk_of(rows):
    return lax.shift_right_logical(rows, CHUNK.bit_length() - 1)


A_HPS = 2
A_HW = A_HPS * LANES


def _band_bias(u_row, first):
    bias = pltpu.roll(jnp.broadcast_to(u_row, (TQ, 2 * TQ)), 0, 1, stride=1, stride_axis=0)
    qc = _chunk_of(lax.broadcasted_iota(jnp.int32, (TQ, 2 * TQ), 0))
    col = lax.broadcasted_iota(jnp.int32, (TQ, 2 * TQ), 1)
    kc = _chunk_of(jnp.bitwise_and(col, TQ - 1))
    ok = jnp.logical_or(jnp.logical_and(col < TQ, kc >= qc), jnp.logical_and(col >= TQ, kc <= qc))
    return jnp.where(ok, bias, NEG_INF) + jnp.where(col < TQ, first * NEG_INF, 0.0)


HQ = TQ // 2
HALVES = ((slice(0, HQ), slice(0, 3 * HQ)),
          (slice(HQ, TQ), slice(HQ, 4 * HQ)))


def _band_probs(q, kcat, bias):
    scale = HEAD_DIM ** -0.5
    out = []
    for rows, cols in HALVES:
        s = _dot_nt(q[rows], kcat[cols]) * scale + bias[rows, cols]
        p = jnp.exp(s - jnp.max(s, axis=1, keepdims=True))
        out.append((p, 1.0 / jnp.sum(p, axis=1, keepdims=True)))
    return out


def _band_specs(nq):
    def col(seg, h):
        return C_A // A_HW + 4 * h + seg

    q_spec = pl.BlockSpec((TQ, A_HW), lambda h, i: (jnp.minimum(i, nq - 1), col(SEG_Q, h)))
    kp_spec = pl.BlockSpec((TQ, A_HW), lambda h, i: (jnp.clip(i - 1, 0, nq - 1), col(SEG_K, h)))
    kc_spec = pl.BlockSpec((TQ, A_HW), lambda h, i: (jnp.minimum(i, nq - 1), col(SEG_K, h)))
    vp_spec = pl.BlockSpec((TQ, A_HW), lambda h, i: (jnp.clip(i - 1, 0, nq - 1), col(SEG_V, h)))
    vc_spec = pl.BlockSpec((TQ, A_HW), lambda h, i: (jnp.minimum(i, nq - 1), col(SEG_V, h)))
    z_spec = pl.BlockSpec((TQ, A_HW), lambda h, i: (jnp.minimum(i, nq - 1), col(SEG_Z, h)))
    u_spec = pl.BlockSpec((A_HPS, 1, 2 * TQ), lambda h, i: (h, 0, 0))
    return q_spec, kp_spec, kc_spec, vp_spec, vc_spec, z_spec, u_spec


def _band_fwd(h, u):
    S = h.shape[0]
    nq = S // TQ

    def body(q_ref, kp_ref, kc_ref, vp_ref, vc_ref, z_ref, u_ref, y_ref, yt_ref, bias_scr):
        i = pl.program_id(1)

        @pl.when(i <= 1)
        def _():
            for hh in range(A_HPS):
                bias_scr[hh] = _band_bias(u_ref[hh], (i == 0).astype(F32))

        for hh in range(A_HPS):
            cs = slice(hh * LANES, (hh + 1) * LANES)
            kcat = jnp.concatenate([kp_ref[:, cs], kc_ref[:, cs]], axis=0)
            vcat = jnp.concatenate([vp_ref[:, cs], vc_ref[:, cs]], axis=0)
            probs = _band_probs(q_ref[:, cs], kcat, bias_scr[hh])
            o = jnp.concatenate([_dot(p.astype(BF16), vcat[cols]) * inv
                                 for (p, inv), (_, cols) in zip(probs, HALVES)], axis=0)
            z = z_ref[:, cs].astype(F32)
            y = o * (z * _sigmoid(z))
            y_ref[:, cs] = y.astype(BF16)
            yt_ref[cs, :] = y.T.astype(BF16)

    specs = _band_specs(nq)
    return pl.pallas_call(
        body,
        name="band_fwd",
        grid=(A_HEADS // A_HPS, nq),
        in_specs=[specs[0], specs[1], specs[2], specs[3], specs[4], specs[5], specs[6]],
        out_specs=[pl.BlockSpec((TQ, A_HW), lambda h, i: (i, h)), pl.BlockSpec((A_HW, TQ), lambda h, i: (h, i))],
        out_shape=[jax.ShapeDtypeStruct((S, D_MODEL), BF16), jax.ShapeDtypeStruct((D_MODEL, S), BF16)],
        scratch_shapes=[pltpu.VMEM((A_HPS, TQ, 2 * TQ), F32)],
        compiler_params=_params(("parallel", "arbitrary")),
    )(h, h, h, h, h, h, u)


def _band_bwd(h, u, dycat):
    S = h.shape[0]
    nq = S // TQ
    scale = HEAD_DIM ** -0.5
    qs, ks, vs, zs = (slice(s * A_HW, (s + 1) * A_HW) for s in (SEG_Q, SEG_K, SEG_V, SEG_Z))

    def body(q_ref, kp_ref, kc_ref, vp_ref, vc_ref, z_ref, u_ref, dy_ref,
             dh_ref, du_ref, bias_scr, db_scr, ckt_scr, cvt_scr, cq_scr, cz_scr):
        i = pl.program_id(1)

        @pl.when(i <= 1)
        def _():
            for hh in range(A_HPS):
                bias_scr[hh] = _band_bias(u_ref[hh], (i == 0).astype(F32))

        @pl.when(i == 0)
        def _():
            db_scr[...] = jnp.zeros_like(db_scr)
            ckt_scr[...] = jnp.zeros_like(ckt_scr)
            cvt_scr[...] = jnp.zeros_like(cvt_scr)
            cq_scr[...] = jnp.zeros_like(cq_scr)
            cz_scr[...] = jnp.zeros_like(cz_scr)

        @pl.when(i < nq)
        def _():
            dh_ref[:, qs] = cq_scr[...]
            dh_ref[:, zs] = cz_scr[...]
            for hh in range(A_HPS):
                cs = slice(hh * LANES, (hh + 1) * LANES)
                q = q_ref[:, cs]
                kcat = jnp.concatenate([kp_ref[:, cs], kc_ref[:, cs]], axis=0)
                vcat = jnp.concatenate([vp_ref[:, cs], vc_ref[:, cs]], axis=0)
                probs = [p * inv for p, inv in _band_probs(q, kcat, bias_scr[hh])]
                o = jnp.concatenate([_dot(p.astype(BF16), vcat[cols]) for p, (_, cols) in zip(probs, HALVES)], axis=0)
                z = z_ref[:, cs].astype(F32)
                sg = _sigmoid(z)
                dy = dy_ref[:, cs].astype(F32)
                do = dy * (z * sg)
                cz_scr[:, cs] = (dy * o * (sg * (1.0 + z * (1.0 - sg)))).astype(BF16)
                dob = do.astype(BF16)
                delta = jnp.sum(do * o, axis=1, keepdims=True)
                qt, dot_ = q.T, dob.T
                dq, dkt, dvt = [], [], []
                for p, (rows, cols) in zip(probs, HALVES):
                    ds = p * (_dot_nt(dob[rows], vcat[cols]) - delta[rows])
                    db_scr[hh, rows, cols] += ds
                    dsb = ds.astype(BF16)
                    dq.append(scale * _dot(dsb, kcat[cols]))
                    dkt.append(scale * _dot(qt[:, rows], dsb))
                    dvt.append(_dot(dot_[:, rows], p.astype(BF16)))
                cq_scr[:, cs] = jnp.concatenate(dq, axis=0).astype(BF16)

                def over_keys(parts):
                    lo, hi = parts
                    prev = jnp.concatenate([lo[:, :HQ], lo[:, HQ:2 * HQ] + hi[:, :HQ]], axis=1)
                    cur = jnp.concatenate([lo[:, 2 * HQ:] + hi[:, HQ:2 * HQ], hi[:, 2 * HQ:]], axis=1)
                    return prev, cur

                (dk_prev, dk_cur), (dv_prev, dv_cur) = over_keys(dkt), over_keys(dvt)
                dh_ref[:, SEG_K * A_HW + hh * LANES:SEG_K * A_HW + (hh + 1) * LANES] = (
                    ckt_scr[cs, :] + dk_prev).T.astype(BF16)
                dh_ref[:, SEG_V * A_HW + hh * LANES:SEG_V * A_HW + (hh + 1) * LANES] = (
                    cvt_scr[cs, :] + dv_prev).T.astype(BF16)
                ckt_scr[cs, :] = dk_cur
                cvt_scr[cs, :] = dv_cur

        @pl.when(i == nq)
        def _():
            dh_ref[:, qs] = cq_scr[...]
            dh_ref[:, zs] = cz_scr[...]
            dh_ref[:, ks] = ckt_scr[...].T.astype(BF16)
            dh_ref[:, vs] = cvt_scr[...].T.astype(BF16)
            r0 = lax.broadcasted_iota(jnp.int32, (TQ, TQ), 0)
            r1 = lax.broadcasted_iota(jnp.int32, (TQ, TQ), 1)
            flip = (r0 + r1 == TQ - 1).astype(BF16)
            for hh in range(A_HPS):
                fl = _dot3(flip, db_scr[hh])
                rolled = pltpu.roll(fl, 0, 1, stride=1, stride_axis=0)
                du_ref[hh] = jnp.sum(rolled, axis=0, keepdims=True)

    specs = _band_specs(nq)
    dy_spec = pl.BlockSpec((TQ, A_HW), lambda h, i: (jnp.minimum(i, nq - 1), h))
    return pl.pallas_call(
        body,
        name="band_bwd",
        grid=(A_HEADS // A_HPS, nq + 1),
        in_specs=[specs[0], specs[1], specs[2], specs[3], specs[4], specs[5], specs[6], dy_spec],
        out_specs=[pl.BlockSpec((TQ, 4 * A_HW), lambda h, i: (jnp.maximum(i - 1, 0), C_A // (4 * A_HW) + h)),
                   pl.BlockSpec((A_HPS, 1, 2 * TQ), lambda h, i: (h, 0, 0))],
        out_shape=[jax.ShapeDtypeStruct((S, HP), BF16), jax.ShapeDtypeStruct((A_HEADS, 1, 2 * TQ), F32)],
        scratch_shapes=[pltpu.VMEM((A_HPS, TQ, 2 * TQ), F32), pltpu.VMEM((A_HPS, TQ, 2 * TQ), F32),
                        pltpu.VMEM((A_HW, TQ), F32), pltpu.VMEM((A_HW, TQ), F32),
                        pltpu.VMEM((TQ, A_HW), BF16), pltpu.VMEM((TQ, A_HW), BF16)],
        compiler_params=_params(("parallel", "arbitrary")),
    )(h, h, h, h, h, h, u, dycat)


def _bias_by_offset(table):
    far = jnp.broadcast_to(table[:, N_REL - 1:], (A_HEADS, TQ - MAX_REL))
    ramp = jnp.flip(table, axis=1)
    rest = jnp.broadcast_to(table[:, :1], (A_HEADS, 2 * TQ - CHUNK - (TQ + MAX_REL + 1)))
    wrap = jnp.broadcast_to(table[:, N_REL - 1:], (A_HEADS, CHUNK))
    return jnp.concatenate([far, ramp, rest, wrap], axis=1)[:, None, :]


def _bias_grad_from_offset(du):
    g = jnp.roll(du[:, 0, :], -(TQ - 1), axis=1)
    far = jnp.sum(g[:, :TQ - MAX_REL], axis=1) + jnp.sum(g[:, 2 * TQ - CHUNK:], axis=1)
    ramp = jnp.flip(g[:, TQ - MAX_REL:TQ + MAX_REL + 1], axis=1)
    return ramp.at[:, N_REL - 1].add(far)


def _mem_probs(q, mk):
    s = _dot_nt(q, mk) * (HEAD_DIM ** -0.5)
    p = jnp.exp(s - jnp.max(s, axis=1, keepdims=True))
    return p * (1.0 / jnp.sum(p, axis=1, keepdims=True))


def _mem_fwd(h, mkv, ycat, ycat_t):
    S = h.shape[0]
    nm = mkv.shape[0]
    c0 = (A_WIDTH + B_WIDTH) // LANES

    def body(q_ref, z_ref, mk_ref, mv_ref, yin_ref, ytin_ref, y_ref, yt_ref):
        del yin_ref, ytin_ref
        p = _mem_probs(q_ref[...], mk_ref[...])
        o = _dot(p.astype(BF16), mv_ref[...])
        z = z_ref[...].astype(F32)
        y = o * (z * _sigmoid(z))
        y_ref[...] = y.astype(BF16)
        yt_ref[...] = y.T.astype(BF16)

    return pl.pallas_call(
        body,
        name="mem_fwd",
        grid=(M_HEADS, S // TQ),
        in_specs=[pl.BlockSpec((TQ, LANES), lambda h, i: (i, C_M // LANES + 2 * h)),
                  pl.BlockSpec((TQ, LANES), lambda h, i: (i, C_M // LANES + 2 * h + 1)),
                  pl.BlockSpec((nm, LANES), lambda h, i: (0, h)),
                  pl.BlockSpec((nm, LANES), lambda h, i: (0, M_HEADS + h)), ANY, ANY],
        out_specs=[pl.BlockSpec((TQ, LANES), lambda h, i: (i, c0 + h)), pl.BlockSpec((LANES, TQ), lambda h, i: (c0 + h, i))],
        out_shape=[jax.ShapeDtypeStruct(ycat.shape, BF16), jax.ShapeDtypeStruct(ycat_t.shape, BF16)],
        input_output_aliases={4: 0, 5: 1},
        compiler_params=_params(("parallel", "arbitrary")),
    )(h, h, mkv, mkv, ycat, ycat_t)


def _mem_bwd(h, mkv, dycat, dh):
    S = h.shape[0]
    nm = mkv.shape[0]
    scale = HEAD_DIM ** -0.5

    def body(q_ref, z_ref, mk_ref, mv_ref, dy_ref, dhin_ref, dh_ref, dmk_ref, dmv_ref):
        del dhin_ref
        i = pl.program_id(1)
        q, mk, mv = q_ref[...], mk_ref[...], mv_ref[...]
        p = _mem_probs(q, mk)
        pb = p.astype(BF16)
        o = _dot(pb, mv)
        z = z_ref[...].astype(F32)
        sg = _sigmoid(z)
        dy = dy_ref[...].astype(F32)
        do = dy * (z * sg)
        dh_ref[:, LANES:] = (dy * o * (sg * (1.0 + z * (1.0 - sg)))).astype(BF16)
        dob = do.astype(BF16)
        ds = p * (_dot_nt(dob, mv) - jnp.sum(do * o, axis=1, keepdims=True))
        dsb = ds.astype(BF16)
        dh_ref[:, :LANES] = (scale * _dot(dsb, mk)).astype(BF16)
        dmk = scale * _dot_tn(dsb, q)
        dmv = _dot_tn(pb, dob)

        @pl.when(i == 0)
        def _():
            dmk_ref[...] = dmk
            dmv_ref[...] = dmv

        @pl.when(i > 0)
        def _():
            dmk_ref[...] += dmk
            dmv_ref[...] += dmv

    dh, dmk, dmv = pl.pallas_call(
        body,
        name="mem_bwd",
        grid=(M_HEADS, S // TQ),
        in_specs=[pl.BlockSpec((TQ, LANES), lambda h, i: (i, C_M // LANES + 2 * h)),
                  pl.BlockSpec((TQ, LANES), lambda h, i: (i, C_M // LANES + 2 * h + 1)),
                  pl.BlockSpec((nm, LANES), lambda h, i: (0, h)),
                  pl.BlockSpec((nm, LANES), lambda h, i: (0, M_HEADS + h)),
                  pl.BlockSpec((TQ, LANES), lambda h, i: (i, (A_WIDTH + B_WIDTH) // LANES + h)), ANY],
        out_specs=[pl.BlockSpec((TQ, 2 * LANES), lambda h, i: (i, C_M // (2 * LANES) + h)),
                   pl.BlockSpec((nm, LANES), lambda h, i: (0, h)),
                   pl.BlockSpec((nm, LANES), lambda h, i: (0, h))],
        out_shape=[jax.ShapeDtypeStruct(dh.shape, BF16),
                   jax.ShapeDtypeStruct((nm, M_WIDTH), F32), jax.ShapeDtypeStruct((nm, M_WIDTH), F32)],
        input_output_aliases={5: 0},
        compiler_params=_params(("parallel", "arbitrary")),
    )(h, h, mkv, mkv, dycat, dh)
    return dh, jnp.concatenate([dmk, dmv], axis=1)


SUB = 2 * CHUNK
SUBS = [slice(s * SUB, (s + 1) * SUB) for s in range(TQ // SUB)]


def _chunk_masks():
    r = lax.broadcasted_iota(jnp.int32, (SUB, SUB), 0)
    c = lax.broadcasted_iota(jnp.int32, (SUB, SUB), 1)
    same = _chunk_of(r) == _chunk_of(c)
    return jnp.logical_and(same, c <= r), jnp.logical_and(same, c > r), jnp.logical_and(same, c >= r)


def _by_sub(fn):
    return jnp.concatenate([fn(rows) for rows in SUBS], axis=0)


def _gla_gates(lr, gw, gb):
    logit = _dot(lr, gw) + gb
    sg = _sigmoid(logit)
    g = (jnp.minimum(logit, 0.0) - jnp.log(1.0 + jnp.exp(-jnp.abs(logit)))) * (1.0 / GATE_TAU)
    lo = _chunk_masks()[0].astype(BF16)
    return sg, _by_sub(lambda rows: _dot3(lo, g[rows]))


def _gla_factors(q, k, b):
    eb = jnp.exp(b)
    enb = jnp.exp(-b)
    return eb, enb, q * eb, q * enb, k * eb, k * enb


def _gla_intra(qp, qn, kp, kn):
    lo, up, _ = _chunk_masks()
    qp, qn, kp, kn = qp.astype(BF16), qn.astype(BF16), kp.astype(BF16), kn.astype(BF16)
    return [jnp.where(lo, _dot_nt(qp[rows], kn[rows]), 0.0) + jnp.where(up, _dot_nt(qn[rows], kp[rows]), 0.0)
            for rows in SUBS]


B_HPS = 2
B_HW = B_HPS * LANES


def _gla_specs(nb, rev):
    blk = (lambda i: nb - 1 - i) if rev else (lambda i: i)
    qkvz_spec = pl.BlockSpec((TQ, 4 * B_HW), lambda i, p: (blk(i), C_B // (4 * B_HW) + p))
    lr_spec = pl.BlockSpec((TQ, LANES), lambda i, p: (blk(i), C_LR // LANES))
    gw_spec = pl.BlockSpec((LANES, B_HW), lambda i, p: (0, p))
    gb_spec = pl.BlockSpec((1, B_HW), lambda i, p: (0, p))
    gn_spec = pl.BlockSpec((1, LANES), lambda i, p: (0, 0))
    return qkvz_spec, lr_spec, gw_spec, gb_spec, gn_spec, blk


def _head_cols(hh, seg):
    return slice((4 * hh + seg) * LANES, (4 * hh + seg + 1) * LANES)


def _gla_fwd(h, gw, gb, gn, ycat, ycat_t):
    S = h.shape[0]
    nb = S // TQ
    c0 = A_WIDTH // LANES

    def body(qkvz_ref, lr_ref, gw_ref, gb_ref, gn_ref, yin_ref, ytin_ref,
             y_ref, yt_ref, o_ref, st_ref, st_scr):
        del yin_ref, ytin_ref
        i, p = pl.program_id(0), pl.program_id(1)
        for hh in range(B_HPS):
            hd = B_HPS * p + hh
            lane = slice(hh * LANES, (hh + 1) * LANES)

            @pl.when(i == 0)
            def _():
                st_scr[hd] = jnp.zeros((LANES, LANES), F32)

            q = qkvz_ref[:, _head_cols(hh, SEG_Q)].astype(F32) * (B_DK ** -0.5)
            k = qkvz_ref[:, _head_cols(hh, SEG_K)].astype(F32)
            v = qkvz_ref[:, _head_cols(hh, SEG_V)]
            _, b = _gla_gates(lr_ref[...], gw_ref[:, lane], gb_ref[:, lane])
            _, _, qp, qn, kp, kn = _gla_factors(q, k, b)
            o_intra = jnp.concatenate([_dot(a.astype(BF16), v[rows])
                                       for a, rows in zip(_gla_intra(qp, qn, kp, kn), SUBS)], axis=0)
            qpb, knb = qp.astype(BF16), kn.astype(BF16)
            st = st_scr[hd]
            outs = []
            for c in range(CPB):
                rows = slice(c * CHUNK, (c + 1) * CHUNK)
                st_ref[hh, c] = st
                outs.append(_dot_nt(qpb[rows], st.astype(BF16)))
                e_last = jnp.exp(b[(c + 1) * CHUNK - 1:(c + 1) * CHUNK, :])
                st = (st + _dot_tn(v[rows], knb[rows])) * e_last
            st_scr[hd] = st
            o = o_intra + jnp.concatenate(outs, axis=0)
            o_ref[:, lane] = o
            r = lax.rsqrt(jnp.mean(o * o, axis=1, keepdims=True) + RMS_EPS)
            z = qkvz_ref[:, _head_cols(hh, SEG_Z)].astype(F32)
            y = o * r * gn_ref[...] * (z * _sigmoid(z))
            y_ref[:, lane] = y.astype(BF16)
            yt_ref[lane, :] = y.T.astype(BF16)

    qkvz_s, lr_s, gw_s, gb_s, gn_s, _ = _gla_specs(nb, False)
    return pl.pallas_call(
        body,
        name="gla_fwd",
        grid=(nb, B_HEADS // B_HPS),
        in_specs=[qkvz_s, lr_s, gw_s, gb_s, gn_s, ANY, ANY],
        out_specs=[pl.BlockSpec((TQ, B_HW), lambda i, p: (i, c0 // B_HPS + p)),
                   pl.BlockSpec((B_HW, TQ), lambda i, p: (c0 // B_HPS + p, i)),
                   pl.BlockSpec((TQ, B_HW), lambda i, p: (i, p)),
                   pl.BlockSpec((B_HPS, CPB, LANES, LANES), lambda i, p: (p, i, 0, 0))],
        out_shape=[jax.ShapeDtypeStruct(ycat.shape, BF16), jax.ShapeDtypeStruct(ycat_t.shape, BF16),
                   jax.ShapeDtypeStruct((S, B_WIDTH), F32),
                   jax.ShapeDtypeStruct((B_HEADS, S // CHUNK, LANES, LANES), F32)],
        input_output_aliases={5: 0, 6: 1},
        scratch_shapes=[pltpu.VMEM((B_HEADS, LANES, LANES), F32)],
        compiler_params=_params(("arbitrary", "arbitrary")),
    )(h, h, gw, gb, gn, ycat, ycat_t)


def _gla_bwd(h, gw, gb, gn, o_pre, states, dycat, dh):
    S = h.shape[0]
    nb = S // TQ

    def body(qkvz_ref, lr_ref, gw_ref, gb_ref, gn_ref, o_ref, st_ref, dy_ref, dhin_ref,
             dh_ref, dlr_ref, dgw_ref, dgb_ref, dgn_ref,
             dst_scr, dgw_scr, dgb_scr, dgn_scr):
        del dhin_ref
        i, p = pl.program_id(0), pl.program_id(1)

        @pl.when(jnp.logical_and(i == 0, p == 0))
        def _():
            dgn_scr[...] = jnp.zeros_like(dgn_scr)

        dlr_heads = [one_head(hh, i, p, qkvz_ref, lr_ref, gw_ref, gb_ref, gn_ref, o_ref, st_ref, dy_ref,
                              dh_ref, dgw_ref, dgb_ref, dst_scr, dgw_scr, dgb_scr, dgn_scr) for hh in range(B_HPS)]
        dlr = dlr_heads[0]
        for more in dlr_heads[1:]:
            dlr = dlr + more

        @pl.when(p == 0)
        def _():
            dlr_ref[...] = dlr

        @pl.when(p > 0)
        def _():
            dlr_ref[...] += dlr

        @pl.when(i == nb - 1)
        def _():
            dgn_ref[...] = dgn_scr[...]

    def one_head(hh, i, p, qkvz_ref, lr_ref, gw_ref, gb_ref, gn_ref, o_ref, st_ref, dy_ref,
                 dh_ref, dgw_ref, dgb_ref, dst_scr, dgw_scr, dgb_scr, dgn_scr):
        hd = B_HPS * p + hh
        lane = slice(hh * LANES, (hh + 1) * LANES)

        @pl.when(i == 0)
        def _():
            dst_scr[hd] = jnp.zeros((LANES, LANES), F32)
            dgw_scr[hd] = jnp.zeros((LANES, LANES), F32)
            dgb_scr[hd] = jnp.zeros((1, LANES), F32)

        q = qkvz_ref[:, _head_cols(hh, SEG_Q)].astype(F32) * (B_DK ** -0.5)
        k = qkvz_ref[:, _head_cols(hh, SEG_K)].astype(F32)
        v = qkvz_ref[:, _head_cols(hh, SEG_V)]
        lr, gwv = lr_ref[...], gw_ref[:, lane]
        sg, b = _gla_gates(lr, gwv, gb_ref[:, lane])
        eb, enb, qp, qn, kp, kn = _gla_factors(q, k, b)
        a = _gla_intra(qp, qn, kp, kn)
        qpb, qnb, kpb, knb = qp.astype(BF16), qn.astype(BF16), kp.astype(BF16), kn.astype(BF16)

        o = o_ref[:, lane]
        gn = gn_ref[...]
        r = lax.rsqrt(jnp.mean(o * o, axis=1, keepdims=True) + RMS_EPS)
        z = qkvz_ref[:, _head_cols(hh, SEG_Z)].astype(F32)
        sz = _sigmoid(z)
        dy = dy_ref[:, lane].astype(F32)
        d_on = dy * (z * sz)
        dh_ref[:, _head_cols(hh, SEG_Z)] = (dy * (o * r * gn) * (sz * (1.0 + z * (1.0 - sz)))).astype(BF16)
        dgn_scr[...] += jnp.sum(d_on * o * r, axis=0, keepdims=True)
        t = d_on * gn
        do = r * t - o * (r * r * r) * jnp.mean(t * o, axis=1, keepdims=True)
        dob = do.astype(BF16)

        lo, up, upper = _chunk_masks()
        dqp, dkn, dqn, dkp, dv = [], [], [], [], []
        for a_s, rows in zip(a, SUBS):
            da = _dot_nt(dob[rows], v[rows])
            dalo = jnp.where(lo, da, 0.0).astype(BF16)
            daup = jnp.where(up, da, 0.0).astype(BF16)
            dqp.append(_dot(dalo, knb[rows]))
            dkn.append(_dot_tn(dalo, qpb[rows]))
            dqn.append(_dot(daup, kpb[rows]))
            dkp.append(_dot_tn(daup, qnb[rows]))
            dv.append(_dot_tn(a_s.astype(BF16), dob[rows]))
        dqp, dkn, dqn, dkp, dv = (jnp.concatenate(x, axis=0) for x in (dqp, dkn, dqn, dkp, dv))

        dst = dst_scr[hd]
        dqp_c, dkn_c, dv_c, dbl_c = [None] * CPB, [None] * CPB, [None] * CPB, [None] * CPB
        for c in reversed(range(CPB)):
            rows = slice(c * CHUNK, (c + 1) * CHUNK)
            st = st_ref[hh, c]
            e_last = jnp.exp(b[(c + 1) * CHUNK - 1:(c + 1) * CHUNK, :])
            if c == CPB - 1:
                st_next = (st + _dot_tn(v[rows], knb[rows])) * e_last
            else:
                st_next = st_ref[hh, c + 1]
            dbl_c[c] = jnp.sum(dst * st_next, axis=0, keepdims=True)
            dtt = (dst * e_last).astype(BF16)
            dv_c[c] = _dot_nt(knb[rows], dtt)
            dkn_c[c] = _dot(v[rows], dtt)
            dqp_c[c] = _dot(dob[rows], st.astype(BF16))
            dst = _dot_tn(dob[rows], qpb[rows]) + dst * e_last
        dst_scr[hd] = dst
        dqp = dqp + jnp.concatenate(dqp_c, axis=0)
        dkn = dkn + jnp.concatenate(dkn_c, axis=0)
        dv = dv + jnp.concatenate(dv_c, axis=0)
        dh_ref[:, _head_cols(hh, SEG_V)] = dv.astype(BF16)
        dh_ref[:, _head_cols(hh, SEG_Q)] = ((dqp * eb + dqn * enb) * (B_DK ** -0.5)).astype(BF16)
        dh_ref[:, _head_cols(hh, SEG_K)] = (dkp * eb + dkn * enb).astype(BF16)

        last = jnp.bitwise_and(lax.broadcasted_iota(jnp.int32, (TQ, 1), 0), CHUNK - 1) == CHUNK - 1
        dbl = jnp.concatenate([jnp.broadcast_to(x, (CHUNK, LANES)) for x in dbl_c], axis=0)
        db = dqp * qp - dqn * qn + dkp * kp - dkn * kn + jnp.where(last, dbl, 0.0)
        upper_b = upper.astype(BF16)
        dlogit = _by_sub(lambda rows: _dot3(upper_b, db[rows])) * (1.0 / GATE_TAU) * (1.0 - sg)
        dlb = dlogit.astype(BF16)
        dgw_scr[hd] += _dot_tn(lr, dlb)
        dgb_scr[hd] += jnp.sum(dlogit, axis=0, keepdims=True)

        @pl.when(i == nb - 1)
        def _():
            dgw_ref[:, lane] = dgw_scr[hd]
            dgb_ref[:, lane] = dgb_scr[hd]

        return _dot_nt(dlb, gwv)

    qkvz_s, lr_s, gw_s, gb_s, gn_s, blk = _gla_specs(nb, True)
    row = pl.BlockSpec((TQ, B_HW), lambda i, p: (blk(i), p))
    dy_spec = pl.BlockSpec((TQ, B_HW), lambda i, p: (blk(i), A_WIDTH // B_HW + p))
    st_spec = pl.BlockSpec((B_HPS, CPB, LANES, LANES), lambda i, p: (p, blk(i), 0, 0))
    return pl.pallas_call(
        body,
        name="gla_bwd",
        grid=(nb, B_HEADS // B_HPS),
        in_specs=[qkvz_s, lr_s, gw_s, gb_s, gn_s, row, st_spec, dy_spec, ANY],
        out_specs=[pl.BlockSpec((TQ, 4 * B_HW), lambda i, p: (blk(i), C_B // (4 * B_HW) + p)),
                   pl.BlockSpec((TQ, LANES), lambda i, p: (blk(i), 0)),
                   pl.BlockSpec((LANES, B_HW), lambda i, p: (0, jnp.where(i == nb - 1, p, 0))),
                   pl.BlockSpec((1, B_HW), lambda i, p: (0, jnp.where(i == nb - 1, p, 0))),
                   pl.BlockSpec((1, LANES), lambda i, p: (0, 0))],
        out_shape=[jax.ShapeDtypeStruct(dh.shape, BF16),
                   jax.ShapeDtypeStruct((S, LANES), F32),
                   jax.ShapeDtypeStruct((LANES, B_HEADS * LANES), F32),
                   jax.ShapeDtypeStruct((1, B_HEADS * LANES), F32),
                   jax.ShapeDtypeStruct((1, LANES), F32)],
        input_output_aliases={8: 0},
        scratch_shapes=[pltpu.VMEM((B_HEADS, LANES, LANES), F32), pltpu.VMEM((B_HEADS, LANES, LANES), F32),
                        pltpu.VMEM((B_HEADS, 1, LANES), F32), pltpu.VMEM((1, LANES), F32)],
        compiler_params=_params(("arbitrary", "arbitrary")),
    )(h, h, gw, gb, gn, o_pre, states, dycat, dh)


def _lr_fill(dlr, dh):
    S = dlr.shape[0]
    w = HP - C_LR

    def body(dlr_ref, dhin_ref, dh_ref):
        del dhin_ref
        dh_ref[:, :LANES] = dlr_ref[...].astype(BF16)
        dh_ref[:, LANES:] = jnp.zeros((TQ, w - LANES), BF16)

    return pl.pallas_call(
        body, name="lr_fill", grid=(S // TQ,),
        in_specs=[pl.BlockSpec((TQ, LANES), lambda i: (i, 0)), ANY],
        out_specs=pl.BlockSpec((TQ, w), lambda i: (i, C_LR // w)),
        out_shape=jax.ShapeDtypeStruct(dh.shape, BF16),
        input_output_aliases={1: 0},
        compiler_params=_params(("parallel",)),
    )(dlr, dh)


LN_ROWS = 256


def _outproj_ln(ycat, w_out, x, g, b):
    S = x.shape[0]

    def body(yc_ref, w_ref, x_ref, g_ref, b_ref, y_ref, yb_ref, yt_ref, xh_ref, rs_ref):
        u = ALPHA * x_ref[...] + _dot(yc_ref[...], w_ref[...])
        mu = jnp.mean(u, axis=1, keepdims=True)
        d = u - mu
        rstd = lax.rsqrt(jnp.mean(d * d, axis=1, keepdims=True) + LN_EPS)
        xh = d * rstd
        y = xh * g_ref[...] + b_ref[...]
        y_ref[...] = y
        yb_ref[...] = y.astype(BF16)
        yt_ref[...] = y.T.astype(BF16)
        xh_ref[...] = xh
        rs_ref[...] = rstd

    row = lambda w: pl.BlockSpec((LN_ROWS, w), lambda i: (i, 0))
    vec = pl.BlockSpec((1, D_MODEL), lambda i: (0, 0))
    return pl.pallas_call(
        body,
        name="outproj_ln",
        grid=(S // LN_ROWS,),
        in_specs=[row(D_MODEL), pl.BlockSpec((D_MODEL, D_MODEL), lambda i: (0, 0)), row(D_MODEL), vec, vec],
        out_specs=[row(D_MODEL), row(D_MODEL), pl.BlockSpec((D_MODEL, LN_ROWS), lambda i: (0, i)), row(D_MODEL), row(1)],
        out_shape=[jax.ShapeDtypeStruct((S, D_MODEL), F32), jax.ShapeDtypeStruct((S, D_MODEL), BF16),
                   jax.ShapeDtypeStruct((D_MODEL, S), BF16),
                   jax.ShapeDtypeStruct((S, D_MODEL), F32), jax.ShapeDtypeStruct((S, 1), F32)],
        compiler_params=_params(("parallel",)),
    )(ycat, w_out, x, g, b)


def _ln_bwd(dy, xhat, rstd, g):
    S = dy.shape[0]

    def body(dy_ref, xh_ref, rs_ref, g_ref, du_ref, dub_ref, dg_ref, db_ref):
        i = pl.program_id(0)
        dy_, xh = dy_ref[...], xh_ref[...]
        dyg = dy_ * g_ref[...]
        m1 = jnp.mean(dyg, axis=1, keepdims=True)
        m2 = jnp.mean(dyg * xh, axis=1, keepdims=True)
        du = rs_ref[...] * (dyg - m1 - xh * m2)
        du_ref[...] = du
        dub_ref[...] = du.astype(BF16)
        dg = jnp.sum(dy_ * xh, axis=0, keepdims=True)
        db = jnp.sum(dy_, axis=0, keepdims=True)

        @pl.when(i == 0)
        def _():
            dg_ref[...] = dg
            db_ref[...] = db

        @pl.when(i > 0)
        def _():
            dg_ref[...] += dg
            db_ref[...] += db

    row = lambda w: pl.BlockSpec((TQ, w), lambda i: (i, 0))
    vec = pl.BlockSpec((1, D_MODEL), lambda i: (0, 0))
    return pl.pallas_call(
        body,
        name="ln_bwd",
        grid=(S // TQ,),
        in_specs=[row(D_MODEL), row(D_MODEL), row(1), vec],
        out_specs=[row(D_MODEL), row(D_MODEL), vec, vec],
        out_shape=[jax.ShapeDtypeStruct((S, D_MODEL), F32), jax.ShapeDtypeStruct((S, D_MODEL), BF16),
                   jax.ShapeDtypeStruct((1, D_MODEL), F32), jax.ShapeDtypeStruct((1, D_MODEL), F32)],
        compiler_params=_params(("arbitrary",)),
    )(dy, xhat, rstd, g)


def _loss_head(y, target):
    S = y.shape[0]

    def body(y_ref, t_ref, l_ref, dy_ref):
        i = pl.program_id(0)
        err = y_ref[...] - t_ref[...]
        dy_ref[...] = err * (1.0 / D_MODEL)
        part = (0.5 / D_MODEL) * jnp.sum(jnp.sum(err * err, axis=1, keepdims=True), axis=0, keepdims=True)

        @pl.when(i == 0)
        def _():
            l_ref[...] = part

        @pl.when(i > 0)
        def _():
            l_ref[...] += part

    row = pl.BlockSpec((TQ, D_MODEL), lambda i: (i, 0))
    return pl.pallas_call(
        body,
        name="loss_head",
        grid=(S // TQ,),
        in_specs=[row, row],
        out_specs=[pl.BlockSpec((1, 1), lambda i: (0, 0)), row],
        out_shape=[jax.ShapeDtypeStruct((1, 1), F32), jax.ShapeDtypeStruct((S, D_MODEL), F32)],
        compiler_params=_params(("arbitrary",)),
    )(y, target)


def _pad_gate(gate_w, gate_b):
    gw = gate_w.reshape(GATE_RANK, B_HEADS, B_DK)
    gw = jnp.pad(gw, ((0, LANES - GATE_RANK), (0, 0), (0, LANES - B_DK))).reshape(LANES, B_HEADS * LANES)
    gb = jnp.pad(gate_b.reshape(B_HEADS, B_DK), ((0, 0), (0, LANES - B_DK))).reshape(1, B_HEADS * LANES)
    return gw.astype(BF16), gb.astype(F32)


def _layer_fwd(x, xb, xt, mem_b, w_in, w_kv, w_out, u, gw, gb, gn, ln_g, ln_b, rider=None):
    h = _matmul(xb, w_in, mode="nn", out_dtype=BF16, tm=1024, tn=768, tk=D_MODEL, name="in_proj", rider=rider)
    if rider is not None:
        h, rode = h
    mkv = _matmul(mem_b, w_kv, mode="nn", out_dtype=BF16, tm=mem_b.shape[0], tn=1024, tk=D_MODEL, name="mem_kv")
    ycat, ycat_t = _band_fwd(h, u)
    ycat, ycat_t, o_pre, states = _gla_fwd(h, gw, gb, gn, ycat, ycat_t)
    ycat, ycat_t = _mem_fwd(h, mkv, ycat, ycat_t)
    y, ybf, yt, xhat, rstd = _outproj_ln(ycat, w_out, x, ln_g, ln_b)
    saved = (xt, h, mkv, ycat_t, o_pre, states, xhat, rstd)
    return (y, ybf, yt, saved) if rider is None else (y, ybf, yt, saved, rode)


def _layer_bwd(dy, saved, mem_b, w_in, w_out, u, gw, gb, gn, ln_g, reduce=None, own_reduce=None):
    xt, h, mkv, ycat_t, o_pre, states, xhat, rstd = saved

    def riding(**kw):
        if reduce is None:
            return _matmul(**kw)
        out, bufs = _matmul(rider=reduce.rider(), **kw)
        reduce.landed(bufs)
        return out

    du, dub, d_ln_g, d_ln_b = _ln_bwd(dy, xhat, rstd, ln_g)
    dycat = riding(a=dub, b=w_out, mode="nt", out_dtype=BF16, tm=1024, tn=1024, tk=D_MODEL, name="dycat")
    d_w_out = _matmul(ycat_t, dub, mode="nn", out_dtype=F32, tm=1024, tn=1024, tk=min(4096, dub.shape[0]), name="d_w_out")
    dh, d_u = _band_bwd(h, u, dycat)
    dh, dlr, dgw, dgb, dgn = _gla_bwd(h, gw, gb, gn, o_pre, states, dycat, dh)
    dh, dmkv = _mem_bwd(h, mkv, dycat, dh)
    dh = _lr_fill(dlr, dh)
    d_w_kv = _matmul(mem_b, dmkv, mode="tn", out_dtype=F32, tm=1024, tn=1024, tk=mem_b.shape[0], name="d_w_kv")
    dx_args = dict(a=dh, b=w_in, mode="nt", out_dtype=F32, tm=1024, tn=1024, tk=3840, name="dx", add=du, add_scale=ALPHA)
    dw_args = dict(a=xt, b=dh, mode="nn", out_dtype=F32, tm=1024, tn=768, tk=min(4096, dh.shape[0]), name="d_w_in")
    if own_reduce is None:
        dx = riding(**dx_args)
        d_w_in = riding(**dw_args)
        return dx, (d_w_in, d_u, dgw, dgb, dgn, d_w_kv, d_w_out, d_ln_g, d_ln_b)
    d_w_in = riding(**dw_args)
    grads = (d_w_in, d_u, dgw, dgb, dgn, d_w_kv, d_w_out, d_ln_g, d_ln_b)
    own = own_reduce(grads, reduce.finish() if reduce is not None else None)
    own.step()
    dx, bufs = _matmul(rider=own.rider(), **dx_args)
    own.landed(bufs)
    return dx, grads, own


def _unpad_heads(w):
    r = w.shape[0]
    return w.reshape(r, B_HEADS, LANES)[:, :, :B_DK].reshape(r, B_KEY_WIDTH)


def _padded_col_of():
    col, o = np.zeros(IN_WIDTH, np.int64), 0
    for seg in (SEG_Q, SEG_K, SEG_V, SEG_Z):
        for hd in range(A_HEADS):
            col[o:o + LANES] = C_A + (hd // A_HPS) * 4 * A_HW + seg * A_HW + (hd % A_HPS) * LANES + np.arange(LANES)
            o += LANES
    for seg, width in ((SEG_Q, B_DK), (SEG_K, B_DK), (SEG_V, LANES), (SEG_Z, LANES)):
        for hd in range(B_HEADS):
            col[o:o + width] = C_B + hd * 4 * LANES + seg * LANES + np.arange(width)
            o += width
    col[o:o + GATE_RANK] = C_LR + np.arange(GATE_RANK)
    o += GATE_RANK
    for seg in (0, 1):
        for hd in range(M_HEADS):
            col[o:o + LANES] = C_M + hd * 2 * LANES + seg * LANES + np.arange(LANES)
            o += LANES
    assert o == IN_WIDTH
    return col


def _runs(idx):
    out, start = [], 0
    for k in range(1, len(idx) + 1):
        if k == len(idx) or idx[k] != idx[k - 1] + 1:
            out.append((int(idx[start]), k - start))
            start = k
    return out


def _chip_columns(g, j, n):
    runs = _runs(_padded_col_of()[j * n:(j + 1) * n])
    return jnp.concatenate([g[:, a:a + ln] for a, ln in runs], axis=1)


def _padded_from_shards(shards):
    n = shards[0].shape[1]
    src = np.full(HP, -1, np.int64)
    src[_padded_col_of()] = np.arange(IN_WIDTH)
    parts, k = [], 0
    while k < HP:
        e = k + 1
        if src[k] < 0:
            while e < HP and src[e] < 0:
                e += 1
            parts.append(jnp.zeros((shards[0].shape[0], e - k), shards[0].dtype))
        else:
            while e < HP and src[e] == src[e - 1] + 1 and src[e] // n == src[k] // n:
                e += 1
            parts.append(shards[src[k] // n][:, src[k] % n:src[k] % n + e - k])
        k = e
    return jnp.concatenate(parts, axis=1)


ADAMW_BLOCK_BYTES = 1 << 20


def _adamw(w, g, m, v, name):
    L, R, C = w.shape
    tl, tr = 1, R
    if R * C * 4 <= ADAMW_BLOCK_BYTES:
        tl = max(d for d in range(1, L + 1) if L % d == 0 and d * R * C * 4 <= ADAMW_BLOCK_BYTES)
    else:
        for cand in (256, 128, 64, 32, 16, 8):
            if R % cand == 0 and R > cand:
                tr = cand
                break

    def body(w_ref, g_ref, m_ref, v_ref, d_ref, nm_ref, nv_ref):
        g_ = g_ref[...]
        nm = ADAM_B1 * m_ref[...] + (1.0 - ADAM_B1) * g_
        nv = ADAM_B2 * v_ref[...] + (1.0 - ADAM_B2) * (g_ * g_)
        m_hat = nm / (1.0 - ADAM_B1 ** ADAM_STEP)
        v_hat = nv / (1.0 - ADAM_B2 ** ADAM_STEP)
        d_ref[...] = -ADAM_LR * (m_hat / (jnp.sqrt(v_hat) + ADAM_EPS) + ADAM_WD * w_ref[...])
        nm_ref[...] = nm
        nv_ref[...] = nv

    spec = pl.BlockSpec((tl, tr, C), lambda l, i: (l, i, 0))
    sd = jax.ShapeDtypeStruct((L, R, C), F32)
    return pl.pallas_call(
        body, name=name, grid=(L // tl, R // tr), in_specs=[spec] * 4, out_specs=[spec] * 3, out_shape=[sd] * 3,
        compiler_params=_params(("parallel", "parallel")),
    )(w, g, m, v)


def _adamw_nd(w, g, m, v, name):
    shape = w.shape
    f = (lambda a: a) if w.ndim == 3 else (lambda a: a.reshape(1, shape[0], shape[1]))
    return tuple(o.reshape(shape) for o in _adamw(f(w), f(g), f(m), f(v), name))


ANY = pl.BlockSpec(memory_space=pl.ANY)


def _place():
    x, y, c = lax.axis_index("x"), lax.axis_index("y"), lax.axis_index("c")
    chips = [(1 - x, y), (x, 1 - y), (1 - x, 1 - y)]
    return x, y, c, chips


class _WeightGather:
    def __init__(self, hop, layer, rows):
        self.hop, self.layer, self.rows = hop, layer, rows
        self.n_sem = 3 * len(rows)

    def _copies(self, shard_refs, buf_refs, send, recv, received):
        x, y, c, chips = _place()
        out = []
        for t, R in enumerate(self.rows):
            half = R // 2
            assert half % 16 == 0
            mine = pl.ds(pl.multiple_of(c * half, 16), half)
            other = pl.ds(pl.multiple_of((1 - c) * half, 16), half)
            mine_of_shard = pl.ds(pl.multiple_of(self.layer * R + c * half, 16), half)
            for k, chip in enumerate(chips):
                theirs = buf_refs[t].at[2 * chip[0] + chip[1]]
                if self.hop == "chips":
                    src, dst, to = shard_refs[t].at[mine_of_shard], buf_refs[t].at[2 * x + y, mine], (*chip, c)
                    landed = theirs.at[mine]
                else:
                    src, dst, to = theirs.at[mine], theirs.at[mine], (x, y, 1 - c)
                    landed = theirs.at[other]
                out.append(pltpu.make_async_remote_copy(
                    src_ref=src, dst_ref=landed if received else dst, send_sem=send.at[3 * t + k],
                    recv_sem=recv.at[3 * t + k], device_id=to, device_id_type=MESH))
        return out

    def start(self, shard_refs, buf_refs, send, recv):
        for cp in self._copies(shard_refs, buf_refs, send, recv, False):
            cp.start()

    def wait(self, shard_refs, buf_refs, send, recv):
        for cp in self._copies(shard_refs, buf_refs, send, recv, True):
            cp.wait_recv()
        for cp in self._copies(shard_refs, buf_refs, send, recv, False):
            cp.wait_send()

    def sems(self):
        return [pltpu.SemaphoreType.DMA((self.n_sem,)), pltpu.SemaphoreType.DMA((self.n_sem,))]

    def call(self, srcs, bufs, name):
        ns, nb = len(srcs), len(bufs)

        def body(*refs):
            src_refs, buf_refs, (send, recv) = refs[:ns], refs[ns + nb:ns + 2 * nb], refs[ns + 2 * nb:]
            self.start(src_refs, buf_refs, send, recv)
            self.wait(src_refs, buf_refs, send, recv)

        return pl.pallas_call(
            body, name=name, in_specs=[ANY] * (ns + nb), out_specs=[ANY] * nb,
            out_shape=[jax.ShapeDtypeStruct(b.shape, b.dtype) for b in bufs],
            input_output_aliases={ns + t: t for t in range(nb)},
            scratch_shapes=self.sems(),
        )(*srcs, *bufs)


class _GradHop(_WeightGather):
    def __init__(self, hop, layer, slices):
        self.hop, self.layer, self.slices = hop, layer, slices
        self.n_sem = {"pair": N_CHIPS, "chips": N_CHIPS - 1, "gather": 1}[hop] * len(slices)

    def _copies(self, src_refs, buf_refs, send, recv, received):
        x, y, c, chips = _place()
        me, out = 2 * x + y, []

        def remote(src, dst, to):
            k = len(out)
            out.append(pltpu.make_async_remote_copy(src_ref=src, dst_ref=dst, send_sem=send.at[k], recv_sem=recv.at[k],
                                                    device_id=to, device_id_type=MESH))

        for t, (half, where) in enumerate(self.slices):
            if self.hop == "pair":
                for j, (a, first) in enumerate(where):
                    rows = pl.ds(pl.multiple_of(first + (1 - c) * half, 8), half)
                    remote(src_refs[a].at[rows], buf_refs[t].at[j], (x, y, 1 - c))
            elif self.hop == "chips":
                for chip in chips:
                    slot = 2 * chip[0] + chip[1]
                    remote(src_refs[t].at[slot], buf_refs[t].at[slot if received else me], (*chip, c))
            else:
                mine = buf_refs[t].at[self.layer, c]
                remote(mine, buf_refs[t].at[self.layer, 1 - c] if received else mine, (x, y, 1 - c))
        return out


def _add_halves(parts, got, c_idx, name):
    n, L, half, C = got.shape
    tr = 64

    def body(*refs):
        ins, (got_ref, o_ref) = refs[1:1 + len(parts)], refs[1 + len(parts):]
        for k in range(len(parts)):
            o_ref[k // L, k % L] = (ins[k][...] + got_ref[k // L, k % L]).astype(BF16)

    def rows_of(first):
        assert first % tr == 0 and half % tr == 0
        return lambda i, c: (first // tr + c[0] * (half // tr) + i, 0)

    whole = pl.BlockSpec((n, L, tr, C), lambda i, c: (0, 0, i, 0))
    return pl.pallas_call(
        body, name=name,
        grid_spec=pltpu.PrefetchScalarGridSpec(
            num_scalar_prefetch=1, grid=(half // tr,),
            in_specs=[pl.BlockSpec((tr, C), rows_of(first)) for _, first in parts] + [whole],
            out_specs=whole),
        out_shape=jax.ShapeDtypeStruct((n, L, half, C), BF16),
        compiler_params=_params(("parallel",)),
    )(c_idx, *[a for a, _ in parts], got)


def _add_slots(r, c_idx, dest, layer, name):
    n, half, C = r.shape
    tr = 256

    def body(c_ref, r_ref, dest_ref, o_ref):
        del dest_ref
        acc = r_ref[0].astype(F32)
        for j in range(1, n):
            acc = acc + r_ref[j].astype(F32)
        o_ref[0, 0] = acc

    return pl.pallas_call(
        body, name=name,
        grid_spec=pltpu.PrefetchScalarGridSpec(
            num_scalar_prefetch=1, grid=(half // tr,),
            in_specs=[pl.BlockSpec((n, tr, C), lambda i, c: (0, i, 0)), ANY],
            out_specs=pl.BlockSpec((1, 1, tr, C), lambda i, c: (layer, c[0], i, 0))),
        out_shape=jax.ShapeDtypeStruct(dest.shape, F32),
        input_output_aliases={2: 0},
        compiler_params=_params(("parallel",)),
    )(c_idx, r, dest)


class _LayerReduce:
    def __init__(self, layer, grads, slices, c_idx, chip, dests):
        self.layer, self.grads, self.slices, self.c_idx, self.chip, self.dests = layer, grads, slices, c_idx, chip, dests
        self.widths = [grads[where[0][0]].shape[1] for _, where in slices]
        self.stage = 0

    def _hop(self, kind):
        return _GradHop(kind, self.layer, self.slices)

    def rider(self):
        if self.stage == 0:
            got = [lax.empty((N_CHIPS, half, w), F32) for (half, _), w in zip(self.slices, self.widths)]
            return self._hop("pair"), self.grads, got
        if self.stage == 1:
            q = [lax.empty(p.shape, BF16) for p in self.pair_sums]
            return self._hop("chips"), self.pair_sums, q
        return self._hop("gather"), [], self.dests

    def landed(self, bufs):
        tag = f"{self.layer}"
        if self.stage == 0:
            self.pair_sums = []
            for t, ((half, where), got) in enumerate(zip(self.slices, bufs)):
                parts = [(self.grads[a], first) for a, first in where]
                p = _add_halves(parts, got[:, None], self.c_idx, f"rs_add2_{t}_{tag}")
                self.pair_sums.append(p.reshape(N_CHIPS, half, p.shape[-1]))
        elif self.stage == 1:
            for t, (q, p) in enumerate(zip(bufs, self.pair_sums)):
                q = lax.dynamic_update_slice_in_dim(q, lax.dynamic_slice_in_dim(p, self.chip, 1, axis=0), self.chip, axis=0)
                self.dests[t] = _add_slots(q, self.c_idx, self.dests[t], self.layer, f"rs_add4_{t}_{tag}")
        else:
            self.dests = list(bufs)
        self.stage += 1

    def step(self):
        hop, srcs, bufs = self.rider()
        self.landed(hop.call(srcs, bufs, f"rs_{hop.hop}_{self.layer}"))

    def finish(self):
        while self.stage < 3:
            self.step()
        return self.dests


def _all_reduce_small(buf, name):
    R = buf.shape[0]

    def flipped(k, x, y, c):
        return ((1 - x) if k & 4 else x, (1 - y) if k & 2 else y, (1 - c) if k & 1 else c)

    def body(b_ref, o_ref, land, send, recv):
        x, y, c, _ = _place()
        me = 4 * x + 2 * y + c
        land[me] = b_ref[...]
        cps = []
        for k in range(1, N_DEV):
            peer = flipped(k, x, y, c)
            cps.append(pltpu.make_async_remote_copy(src_ref=b_ref, dst_ref=land.at[me], send_sem=send.at[k - 1],
                                                    recv_sem=recv.at[k - 1], device_id=peer, device_id_type=MESH))
        for cp in cps:
            cp.start()
        for k in range(1, N_DEV):
            peer = flipped(k, x, y, c)
            slot = 4 * peer[0] + 2 * peer[1] + peer[2]
            pltpu.make_async_remote_copy(src_ref=b_ref, dst_ref=land.at[slot], send_sem=send.at[k - 1],
                                         recv_sem=recv.at[k - 1], device_id=peer, device_id_type=MESH).wait_recv()
        for cp in cps:
            cp.wait_send()
        acc = land[0]
        for j in range(1, N_DEV):
            acc = acc + land[j]
        o_ref[...] = acc

    vm = pl.BlockSpec(memory_space=pltpu.VMEM)
    return pl.pallas_call(
        body, name=name, in_specs=[vm], out_specs=vm,
        out_shape=jax.ShapeDtypeStruct((R, LANES), F32),
        scratch_shapes=[pltpu.VMEM((N_DEV, R, LANES), F32), pltpu.SemaphoreType.DMA((N_DEV - 1,)),
                        pltpu.SemaphoreType.DMA((N_DEV - 1,))],
    )(buf)


def kernel(x, mem, w_in, a_rel_bias, b_gate_w, b_gate_b, b_norm_g, w_mem_kv, w_out, ln_g, ln_b, loss_target, m_w_in, m_a_rel_bias, m_b_gate_w, m_b_gate_b, m_b_norm_g, m_w_mem_kv, m_w_out, m_ln_g, m_ln_b, v_w_in, v_a_rel_bias, v_b_gate_w, v_b_gate_b, v_b_norm_g, v_w_mem_kv, v_w_out, v_ln_g, v_ln_b):
    L = w_in.shape[0]
    S = x.shape[1]
    cx, cy, cc = lax.axis_index("x"), lax.axis_index("y"), lax.axis_index("c")
    chip = 2 * cx + cy
    c_idx = jnp.reshape(cc, (1,)).astype(jnp.int32)

    n_in, r_kv, r_out = w_in.shape[2], w_mem_kv.shape[1], w_out.shape[1]
    shards = [w.astype(BF16).reshape(-1, w.shape[2]) for w in (w_in, w_mem_kv, w_out)]
    rows = [D_MODEL, r_kv, r_out]

    def landing(l):
        return [lax.dynamic_update_slice_in_dim(lax.empty((N_CHIPS, r, s.shape[1]), BF16),
                                                s[l * r:(l + 1) * r][None], chip, axis=0)
                for s, r in zip(shards, rows)]

    def assembled(bufs):
        return (_padded_from_shards([bufs[0][j] for j in range(N_CHIPS)]),
                bufs[1].reshape(D_MODEL, bufs[1].shape[2]), bufs[2].reshape(D_MODEL, D_MODEL))

    bufs0 = _WeightGather("chips", 0, rows).call(shards, landing(0), "gather_chips_0")
    weights = [assembled(_WeightGather("pair", 0, rows).call(shards, bufs0, "gather_pair_0"))]

    gw_cols = b_gate_w.shape[2]
    gw_slot = jnp.zeros((N_CHIPS, L, GATE_RANK, gw_cols), F32)
    gw_slot = lax.dynamic_update_slice(gw_slot, (0.5 * b_gate_w)[None], (chip, 0, 0, 0))
    gw_flat = gw_slot.reshape(-1)
    n_gw = gw_flat.shape[0]
    pad = (-n_gw) % (8 * LANES)
    gw_full = _all_reduce_small(jnp.pad(gw_flat, (0, pad)).reshape(-1, LANES), "gather_gate_w").reshape(-1)[:n_gw]
    gw_full = gw_full.reshape(N_CHIPS, L, GATE_RANK, gw_cols).transpose(1, 2, 0, 3).reshape(L, GATE_RANK, B_KEY_WIDTH)

    xs = x.reshape(S, D_MODEL)
    mem_b = mem.reshape(mem.shape[1], D_MODEL).astype(BF16)
    target = loss_target.reshape(S, D_MODEL)

    small_w = []
    for l in range(L):
        gw_l, gb_l = _pad_gate(gw_full[l], b_gate_b[l])
        small_w.append((_bias_by_offset(a_rel_bias[l]), gw_l, gb_l,
                        b_norm_g[l].reshape(1, LANES), ln_g[l].reshape(1, D_MODEL), ln_b[l].reshape(1, D_MODEL)))

    y, yb = xs, xs.astype(BF16)
    yt = _transpose(yb, "x_t")
    saved = []
    for l in range(L):
        if l + 1 < L:
            rider = (_WeightGather("chips", l + 1, rows), shards, landing(l + 1))
            y, yb, yt, sv, bufs = _layer_fwd(y, yb, yt, mem_b, *weights[l], *small_w[l], rider=rider)
            weights.append(assembled(_WeightGather("pair", l + 1, rows).call(shards, bufs, f"gather_pair_{l + 1}")))
        else:
            y, yb, yt, sv = _layer_fwd(y, yb, yt, mem_b, *weights[l], *small_w[l])
        saved.append(sv)
    layer_w = [(*weights[l], *small_w[l]) for l in range(L)]
    loss_part, dy = _loss_head(y, target)

    halves = [D_MODEL // 2, r_kv // 2, r_out // 2]
    dests = [lax.empty((L, 2, hf, w.shape[2]), F32) for hf, w in zip(halves, (w_in, w_mem_kv, w_out))]
    slices = [(halves[0], [(j, 0) for j in range(N_CHIPS)]),
              (halves[1], [(N_CHIPS, j * r_kv) for j in range(N_CHIPS)]),
              (halves[2], [(N_CHIPS + 1, j * r_out) for j in range(N_CHIPS)])]

    def reduction(l, g, into):
        arrays = [_chip_columns(g[0], j, n_in) for j in range(N_CHIPS)] + [g[5], g[6]]
        return _LayerReduce(l, arrays, slices, c_idx, chip, into)

    grads, reduce = [None] * L, None
    for l in reversed(range(L)):
        w_in_l, w_kv_l, w_out_l, u_l, gw_l, gb_l, gn_l, lg_l, lb_l = layer_w[l]
        args = (dy, saved[l], mem_b, w_in_l, w_out_l, u_l, gw_l, gb_l, gn_l, lg_l)
        if l > 0:
            dy, grads[l] = _layer_bwd(*args, reduce=reduce)
            if reduce is not None:
                dests = reduce.finish()
            reduce = reduction(l, grads[l], dests)
        else:
            own = lambda g, above: reduction(0, g, dests if above is None else above)
            dy, grads[l], reduce = _layer_bwd(*args, reduce=reduce, own_reduce=own)
    r_w_in, r_w_kv, r_w_out = [d.reshape(L, 2 * d.shape[2], d.shape[3]) for d in reduce.finish()]
    grad_x = dy.reshape(x.shape)

    g_rel = jnp.stack([_bias_grad_from_offset(g[1]) for g in grads])
    g_gw = jnp.stack([_unpad_heads(g[2][:GATE_RANK]) for g in grads])
    g_gb = jnp.stack([_unpad_heads(g[3])[0] for g in grads])
    g_gn = jnp.stack([g[4][0] for g in grads])
    g_lg = jnp.stack([g[7][0] for g in grads])
    g_lb = jnp.stack([g[8][0] for g in grads])

    small = [g_rel, g_gw, g_gb, g_gn, g_lg, g_lb, loss_part]
    flat = jnp.concatenate([s.reshape(-1) for s in small])
    n_small = flat.shape[0]
    pad = (-n_small) % (8 * LANES)
    red = _all_reduce_small(jnp.pad(flat, (0, pad)).reshape(-1, LANES), "all_reduce_small").reshape(-1)
    outs, off = [], 0
    for s in small:
        outs.append(red[off:off + s.size].reshape(s.shape))
        off += s.size
    g_rel, g_gw, g_gb, g_gn, g_lg, g_lb, loss = outs
    loss = loss.reshape(())
    g_gw = lax.dynamic_slice_in_dim(g_gw.reshape(L, GATE_RANK, N_CHIPS, gw_cols), chip, 1, axis=2).reshape(L, GATE_RANK, gw_cols)

    g_list = [r_w_in, g_rel, g_gw, g_gb, g_gn, r_w_kv, r_w_out, g_lg, g_lb]
    w_list = [w_in, a_rel_bias, b_gate_w, b_gate_b, b_norm_g, w_mem_kv, w_out, ln_g, ln_b]
    m_list = [m_w_in, m_a_rel_bias, m_b_gate_w, m_b_gate_b, m_b_norm_g, m_w_mem_kv, m_w_out, m_ln_g, m_ln_b]
    v_list = [v_w_in, v_a_rel_bias, v_b_gate_w, v_b_gate_b, v_b_norm_g, v_w_mem_kv, v_w_out, v_ln_g, v_ln_b]
    names = ["w_in", "rel", "gate_w", "gate_b", "norm_g", "w_kv", "w_out", "ln_g", "ln_b"]
    to_cols = lambda a: jnp.transpose(a, (2, 0, 1))
    upd = [tuple(jnp.transpose(o, (1, 2, 0)) for o in
                 _adamw(to_cols(w_in), to_cols(r_w_in), to_cols(m_w_in), to_cols(v_w_in), "adamw_w_in"))]
    upd += [_adamw_nd(w, g, m, v, "adamw_" + n)
            for w, g, m, v, n in list(zip(w_list, g_list, m_list, v_list, names))[1:]]
    deltas = [u_[0] for u_ in upd]
    new_m = [u_[1] for u_ in upd]
    new_v = [u_[2] for u_ in upd]
    return (loss, grad_x, *g_list, *deltas, *new_m, *new_v)
```

```python
import functools

import numpy as np
import jax
import jax.numpy as jnp
from jax import lax
from jax.experimental import pallas as pl
from jax.experimental.pallas import tpu as pltpu

F32 = jnp.float32
BF16 = jnp.bfloat16
MESH = pl.DeviceIdType.MESH

D_MODEL = 2048
DEPTH = 4
CHUNK = 64
LEFT_CHUNKS = 8
MAX_REL = 128
N_REL = 2 * MAX_REL + 1
A_HEADS = 8
HEAD_DIM = 128
B_HEADS = 4
B_DK = 64
M_HEADS = 4
GATE_RANK = 16
GATE_TAU = 16.0
A_WIDTH = A_HEADS * HEAD_DIM
B_WIDTH = B_HEADS * HEAD_DIM
B_KEY_WIDTH = B_HEADS * B_DK
M_WIDTH = M_HEADS * HEAD_DIM
IN_WIDTH = 4 * A_WIDTH + 2 * B_KEY_WIDTH + 2 * B_WIDTH + GATE_RANK + 2 * M_WIDTH
ALPHA = (2.0 * DEPTH) ** 0.25
LN_EPS = 1e-5
RMS_EPS = 1e-6
NEG_INF = -1e30
ADAM_LR = 0.001
ADAM_B1 = 0.9
ADAM_B2 = 0.999
ADAM_EPS = 1e-08
ADAM_WD = 0.01
ADAM_STEP = 10

LANES = 128
VMEM_LIMIT = 56 * 1024 * 1024

C_A, C_B, C_M = 0, 4096, 6144
HP = 7168
LR_HEAD, LR_LANE = B_HEADS - 1, B_DK
C_LR = C_B + LR_HEAD * 4 * LANES
SEG_Q, SEG_K, SEG_V, SEG_Z = 0, 1, 2, 3
TQ = 512
CPB = TQ // CHUNK
N_CHIPS = 4
N_DEV = 8


def _params(sem, vmem=VMEM_LIMIT):
    return pltpu.CompilerParams(dimension_semantics=sem, vmem_limit_bytes=vmem)


def _dot(a, b):
    return jnp.dot(a, b, preferred_element_type=F32)


def _dot_nt(a, b):
    return lax.dot_general(a, b, (((1,), (1,)), ((), ())), preferred_element_type=F32)


def _dot_tn(a, b):
    return lax.dot_general(a, b, (((0,), (0,)), ((), ())), preferred_element_type=F32)


def _sigmoid(x):
    return 1.0 / (1.0 + jnp.exp(-x))


def _split3(x):
    hi = x.astype(BF16)
    r = x - hi.astype(F32)
    mid = r.astype(BF16)
    lo = (r - mid.astype(F32)).astype(BF16)
    return hi, mid, lo


def _dot3(m_bf, x):
    hi, mid, lo = _split3(x)
    return _dot(m_bf, hi) + _dot(m_bf, mid) + _dot(m_bf, lo)


def _matmul(a, b, *, mode, out_dtype, tm, tn, tk, name, add=None, add_scale=1.0, rider=None):
    if mode == "nn":
        (M, K), (K2, N) = a.shape, b.shape
        a_spec = pl.BlockSpec((tm, tk), lambda i, j, k: (i, k))
        b_spec = pl.BlockSpec((tk, tn), lambda i, j, k: (k, j))
        dot = _dot
    elif mode == "nt":
        (M, K), (N, K2) = a.shape, b.shape
        a_spec = pl.BlockSpec((tm, tk), lambda i, j, k: (i, k))
        b_spec = pl.BlockSpec((tn, tk), lambda i, j, k: (j, k))
        dot = _dot_nt
    else:
        (K, M), (K2, N) = a.shape, b.shape
        a_spec = pl.BlockSpec((tk, tm), lambda i, j, k: (k, i))
        b_spec = pl.BlockSpec((tk, tn), lambda i, j, k: (k, j))
        dot = _dot_tn
    assert K == K2 and M % tm == 0 and N % tn == 0 and K % tk == 0, (a.shape, b.shape, mode)
    nk = K // tk
    has_add = add is not None
    assert nk == 1 or out_dtype == F32
    grid = (M // tm, N // tn, nk)
    hop, srcs, bufs = rider if rider is not None else (None, [], [])
    n_in = 2 + has_add

    def body(*refs):
        a_ref, b_ref = refs[:2]
        add_ref = refs[2] if has_add else None
        src_refs = refs[n_in:n_in + len(srcs)]
        o_ref = refs[n_in + len(srcs) + len(bufs)]
        buf_refs = refs[n_in + len(srcs) + len(bufs) + 1:n_in + len(srcs) + 2 * len(bufs) + 1]
        sems = refs[n_in + len(srcs) + 2 * len(bufs) + 1:]
        i, j, k = pl.program_id(0), pl.program_id(1), pl.program_id(2)
        if hop is not None:
            @pl.when(jnp.logical_and(jnp.logical_and(i == 0, j == 0), k == 0))
            def _():
                hop.start(src_refs, buf_refs, *sems)

        part = dot(a_ref[...].astype(BF16), b_ref[...].astype(BF16))

        @pl.when(k == 0)
        def _():
            first = part + add_scale * add_ref[...] if has_add else part
            o_ref[...] = first.astype(out_dtype)

        if nk > 1:
            @pl.when(k > 0)
            def _():
                o_ref[...] += part

        if hop is not None:
            @pl.when(jnp.logical_and(jnp.logical_and(i == grid[0] - 1, j == grid[1] - 1), k == nk - 1))
            def _():
                hop.wait(src_refs, buf_refs, *sems)

    in_specs = [a_spec, b_spec]
    args = [a, b]
    if has_add:
        in_specs.append(pl.BlockSpec((tm, tn), lambda i, j, k: (i, j)))
        args.append(add)
    out = pl.pallas_call(
        body,
        name=name,
        grid=grid,
        in_specs=in_specs + [ANY] * (len(srcs) + len(bufs)),
        out_specs=[pl.BlockSpec((tm, tn), lambda i, j, k: (i, j))] + [ANY] * len(bufs),
        out_shape=[jax.ShapeDtypeStruct((M, N), out_dtype)] + [jax.ShapeDtypeStruct(x.shape, x.dtype) for x in bufs],
        input_output_aliases={n_in + len(srcs) + t: 1 + t for t in range(len(bufs))},
        scratch_shapes=hop.sems() if hop is not None else [],
        compiler_params=_params(("parallel", "parallel", "arbitrary") if hop is None
                                else ("arbitrary", "arbitrary", "arbitrary")),
    )(*args, *srcs, *bufs)
    return out[0] if hop is None else (out[0], list(out[1:]))


def _transpose(a, name):
    R, C = a.shape
    t = 512

    def body(a_ref, o_ref):
        o_ref[...] = a_ref[...].T

    return pl.pallas_call(
        body, name=name, grid=(R // t, C // t),
        in_specs=[pl.BlockSpec((t, t), lambda i, j: (i, j))],
        out_specs=pl.BlockSpec((t, t), lambda i, j: (j, i)),
        out_shape=jax.ShapeDtypeStruct((C, R), a.dtype),
        compiler_params=_params(("parallel", "parallel")),
    )(a)


def _chunk_of(rows):
    return lax.shift_right_logical(rows, CHUNK.bit_length() - 1)


A_HPS = 2
A_HW = A_HPS * LANES


def _band_bias(u_row, first):
    bias = pltpu.roll(jnp.broadcast_to(u_row, (TQ, 2 * TQ)), 0, 1, stride=1, stride_axis=0)
    qc = _chunk_of(lax.broadcasted_iota(jnp.int32, (TQ, 2 * TQ), 0))
    col = lax.broadcasted_iota(jnp.int32, (TQ, 2 * TQ), 1)
    kc = _chunk_of(jnp.bitwise_and(col, TQ - 1))
    ok = jnp.logical_or(jnp.logical_and(col < TQ, kc >= qc), jnp.logical_and(col >= TQ, kc <= qc))
    return jnp.where(ok, bias, NEG_INF) + jnp.where(col < TQ, first * NEG_INF, 0.0)


HQ = TQ // 2
HALVES = ((slice(0, HQ), slice(0, 3 * HQ)),
          (slice(HQ, TQ), slice(HQ, 4 * HQ)))


def _band_probs(q, kcat, bias):
    scale = HEAD_DIM ** -0.5
    out = []
    for rows, cols in HALVES:
        s = _dot_nt(q[rows], kcat[cols]) * scale + bias[rows, cols]
        p = jnp.exp(s - jnp.max(s, axis=1, keepdims=True))
        out.append((p, 1.0 / jnp.sum(p, axis=1, keepdims=True)))
    return out


def _band_specs(nq):
    def col(seg, h):
        return C_A // A_HW + 4 * h + seg

    q_spec = pl.BlockSpec((TQ, A_HW), lambda h, i: (jnp.minimum(i, nq - 1), col(SEG_Q, h)))
    kp_spec = pl.BlockSpec((TQ, A_HW), lambda h, i: (jnp.clip(i - 1, 0, nq - 1), col(SEG_K, h)))
    kc_spec = pl.BlockSpec((TQ, A_HW), lambda h, i: (jnp.minimum(i, nq - 1), col(SEG_K, h)))
    vp_spec = pl.BlockSpec((TQ, A_HW), lambda h, i: (jnp.clip(i - 1, 0, nq - 1), col(SEG_V, h)))
    vc_spec = pl.BlockSpec((TQ, A_HW), lambda h, i: (jnp.minimum(i, nq - 1), col(SEG_V, h)))
    z_spec = pl.BlockSpec((TQ, A_HW), lambda h, i: (jnp.minimum(i, nq - 1), col(SEG_Z, h)))
    u_spec = pl.BlockSpec((A_HPS, 1, 2 * TQ), lambda h, i: (h, 0, 0))
    return q_spec, kp_spec, kc_spec, vp_spec, vc_spec, z_spec, u_spec


def _band_fwd(h, u):
    S = h.shape[0]
    nq = S // TQ

    def body(q_ref, kp_ref, kc_ref, vp_ref, vc_ref, z_ref, u_ref, y_ref, yt_ref, bias_scr):
        i = pl.program_id(1)

        @pl.when(i <= 1)
        def _():
            for hh in range(A_HPS):
                bias_scr[hh] = _band_bias(u_ref[hh], (i == 0).astype(F32))

        for hh in range(A_HPS):
            cs = slice(hh * LANES, (hh + 1) * LANES)
            kcat = jnp.concatenate([kp_ref[:, cs], kc_ref[:, cs]], axis=0)
            vcat = jnp.concatenate([vp_ref[:, cs], vc_ref[:, cs]], axis=0)
            probs = _band_probs(q_ref[:, cs], kcat, bias_scr[hh])
            o = jnp.concatenate([_dot(p.astype(BF16), vcat[cols]) * inv
                                 for (p, inv), (_, cols) in zip(probs, HALVES)], axis=0)
            z = z_ref[:, cs].astype(F32)
            y = o * (z * _sigmoid(z))
            y_ref[:, cs] = y.astype(BF16)
            yt_ref[cs, :] = y.T.astype(BF16)

    specs = _band_specs(nq)
    return pl.pallas_call(
        body,
        name="band_fwd",
        grid=(A_HEADS // A_HPS, nq),
        in_specs=[specs[0], specs[1], specs[2], specs[3], specs[4], specs[5], specs[6]],
        out_specs=[pl.BlockSpec((TQ, A_HW), lambda h, i: (i, h)), pl.BlockSpec((A_HW, TQ), lambda h, i: (h, i))],
        out_shape=[jax.ShapeDtypeStruct((S, D_MODEL), BF16), jax.ShapeDtypeStruct((D_MODEL, S), BF16)],
        scratch_shapes=[pltpu.VMEM((A_HPS, TQ, 2 * TQ), F32)],
        compiler_params=_params(("parallel", "arbitrary")),
    )(h, h, h, h, h, h, u)


def _band_bwd(h, u, dycat):
    S = h.shape[0]
    nq = S // TQ
    scale = HEAD_DIM ** -0.5
    qs, ks, vs, zs = (slice(s * A_HW, (s + 1) * A_HW) for s in (SEG_Q, SEG_K, SEG_V, SEG_Z))

    def body(q_ref, kp_ref, kc_ref, vp_ref, vc_ref, z_ref, u_ref, dy_ref,
             dh_ref, du_ref, bias_scr, db_scr, ckt_scr, cvt_scr, cq_scr, cz_scr):
        i = pl.program_id(1)

        @pl.when(i <= 1)
        def _():
            for hh in range(A_HPS):
                bias_scr[hh] = _band_bias(u_ref[hh], (i == 0).astype(F32))

        @pl.when(i == 0)
        def _():
            db_scr[...] = jnp.zeros_like(db_scr)
            ckt_scr[...] = jnp.zeros_like(ckt_scr)
            cvt_scr[...] = jnp.zeros_like(cvt_scr)
            cq_scr[...] = jnp.zeros_like(cq_scr)
            cz_scr[...] = jnp.zeros_like(cz_scr)

        @pl.when(i < nq)
        def _():
            dh_ref[:, qs] = cq_scr[...]
            dh_ref[:, zs] = cz_scr[...]
            for hh in range(A_HPS):
                cs = slice(hh * LANES, (hh + 1) * LANES)
                q = q_ref[:, cs]
                kcat = jnp.concatenate([kp_ref[:, cs], kc_ref[:, cs]], axis=0)
                vcat = jnp.concatenate([vp_ref[:, cs], vc_ref[:, cs]], axis=0)
                probs = [p * inv for p, inv in _band_probs(q, kcat, bias_scr[hh])]
                o = jnp.concatenate([_dot(p.astype(BF16), vcat[cols]) for p, (_, cols) in zip(probs, HALVES)], axis=0)
                z = z_ref[:, cs].astype(F32)
                sg = _sigmoid(z)
                dy = dy_ref[:, cs].astype(F32)
                do = dy * (z * sg)
                cz_scr[:, cs] = (dy * o * (sg * (1.0 + z * (1.0 - sg)))).astype(BF16)
                dob = do.astype(BF16)
                delta = jnp.sum(do * o, axis=1, keepdims=True)
                qt, dot_ = q.T, dob.T
                dq, dkt, dvt = [], [], []
                for p, (rows, cols) in zip(probs, HALVES):
                    ds = p * (_dot_nt(dob[rows], vcat[cols]) - delta[rows])
                    db_scr[hh, rows, cols] += ds
                    dsb = ds.astype(BF16)
                    dq.append(scale * _dot(dsb, kcat[cols]))
                    dkt.append(scale * _dot(qt[:, rows], dsb))
                    dvt.append(_dot(dot_[:, rows], p.astype(BF16)))
                cq_scr[:, cs] = jnp.concatenate(dq, axis=0).astype(BF16)

                def over_keys(parts):
                    lo, hi = parts
                    prev = jnp.concatenate([lo[:, :HQ], lo[:, HQ:2 * HQ] + hi[:, :HQ]], axis=1)
                    cur = jnp.concatenate([lo[:, 2 * HQ:] + hi[:, HQ:2 * HQ], hi[:, 2 * HQ:]], axis=1)
                    return prev, cur

                (dk_prev, dk_cur), (dv_prev, dv_cur) = over_keys(dkt), over_keys(dvt)
                dh_ref[:, SEG_K * A_HW + hh * LANES:SEG_K * A_HW + (hh + 1) * LANES] = (
                    ckt_scr[cs, :] + dk_prev).T.astype(BF16)
                dh_ref[:, SEG_V * A_HW + hh * LANES:SEG_V * A_HW + (hh + 1) * LANES] = (
                    cvt_scr[cs, :] + dv_prev).T.astype(BF16)
                ckt_scr[cs, :] = dk_cur
                cvt_scr[cs, :] = dv_cur

        @pl.when(i == nq)
        def _():
            dh_ref[:, qs] = cq_scr[...]
            dh_ref[:, zs] = cz_scr[...]
            dh_ref[:, ks] = ckt_scr[...].T.astype(BF16)
            dh_ref[:, vs] = cvt_scr[...].T.astype(BF16)
            r0 = lax.broadcasted_iota(jnp.int32, (TQ, TQ), 0)
            r1 = lax.broadcasted_iota(jnp.int32, (TQ, TQ), 1)
            flip = (r0 + r1 == TQ - 1).astype(BF16)
            for hh in range(A_HPS):
                fl = _dot3(flip, db_scr[hh])
                rolled = pltpu.roll(fl, 0, 1, stride=1, stride_axis=0)
                du_ref[hh] = jnp.sum(rolled, axis=0, keepdims=True)

    specs = _band_specs(nq)
    dy_spec = pl.BlockSpec((TQ, A_HW), lambda h, i: (jnp.minimum(i, nq - 1), h))
    return pl.pallas_call(
        body,
        name="band_bwd",
        grid=(A_HEADS // A_HPS, nq + 1),
        in_specs=[specs[0], specs[1], specs[2], specs[3], specs[4], specs[5], specs[6], dy_spec],
        out_specs=[pl.BlockSpec((TQ, 4 * A_HW), lambda h, i: (jnp.maximum(i - 1, 0), C_A // (4 * A_HW) + h)),
                   pl.BlockSpec((A_HPS, 1, 2 * TQ), lambda h, i: (h, 0, 0))],
        out_shape=[jax.ShapeDtypeStruct((S, HP), BF16), jax.ShapeDtypeStruct((A_HEADS, 1, 2 * TQ), F32)],
        scratch_shapes=[pltpu.VMEM((A_HPS, TQ, 2 * TQ), F32), pltpu.VMEM((A_HPS, TQ, 2 * TQ), F32),
                        pltpu.VMEM((A_HW, TQ), F32), pltpu.VMEM((A_HW, TQ), F32),
                        pltpu.VMEM((TQ, A_HW), BF16), pltpu.VMEM((TQ, A_HW), BF16)],
        compiler_params=_params(("parallel", "arbitrary")),
    )(h, h, h, h, h, h, u, dycat)


def _bias_by_offset(table):
    far = jnp.broadcast_to(table[:, N_REL - 1:], (A_HEADS, TQ - MAX_REL))
    ramp = jnp.flip(table, axis=1)
    rest = jnp.broadcast_to(table[:, :1], (A_HEADS, 2 * TQ - CHUNK - (TQ + MAX_REL + 1)))
    wrap = jnp.broadcast_to(table[:, N_REL - 1:], (A_HEADS, CHUNK))
    return jnp.concatenate([far, ramp, rest, wrap], axis=1)[:, None, :]


def _bias_grad_from_offset(du):
    g = jnp.roll(du[:, 0, :], -(TQ - 1), axis=1)
    far = jnp.sum(g[:, :TQ - MAX_REL], axis=1) + jnp.sum(g[:, 2 * TQ - CHUNK:], axis=1)
    ramp = jnp.flip(g[:, TQ - MAX_REL:TQ + MAX_REL + 1], axis=1)
    return ramp.at[:, N_REL - 1].add(far)


def _mem_probs(q, mk):
    s = _dot_nt(q, mk) * (HEAD_DIM ** -0.5)
    p = jnp.exp(s - jnp.max(s, axis=1, keepdims=True))
    return p * (1.0 / jnp.sum(p, axis=1, keepdims=True))


def _mem_fwd(h, mkv, ycat, ycat_t):
    S = h.shape[0]
    nm = mkv.shape[0]
    c0 = (A_WIDTH + B_WIDTH) // LANES

    def body(q_ref, z_ref, mk_ref, mv_ref, yin_ref, ytin_ref, y_ref, yt_ref):
        del yin_ref, ytin_ref
        p = _mem_probs(q_ref[...], mk_ref[...])
        o = _dot(p.astype(BF16), mv_ref[...])
        z = z_ref[...].astype(F32)
        y = o * (z * _sigmoid(z))
        y_ref[...] = y.astype(BF16)
        yt_ref[...] = y.T.astype(BF16)

    return pl.pallas_call(
        body,
        name="mem_fwd",
        grid=(M_HEADS, S // TQ),
        in_specs=[pl.BlockSpec((TQ, LANES), lambda h, i: (i, C_M // LANES + 2 * h)),
                  pl.BlockSpec((TQ, LANES), lambda h, i: (i, C_M // LANES + 2 * h + 1)),
                  pl.BlockSpec((nm, LANES), lambda h, i: (0, h)),
                  pl.BlockSpec((nm, LANES), lambda h, i: (0, M_HEADS + h)), ANY, ANY],
        out_specs=[pl.BlockSpec((TQ, LANES), lambda h, i: (i, c0 + h)), pl.BlockSpec((LANES, TQ), lambda h, i: (c0 + h, i))],
        out_shape=[jax.ShapeDtypeStruct(ycat.shape, BF16), jax.ShapeDtypeStruct(ycat_t.shape, BF16)],
        input_output_aliases={4: 0, 5: 1},
        compiler_params=_params(("parallel", "arbitrary")),
    )(h, h, mkv, mkv, ycat, ycat_t)


def _mem_bwd(h, mkv, dycat, dh):
    S = h.shape[0]
    nm = mkv.shape[0]
    scale = HEAD_DIM ** -0.5

    def body(q_ref, z_ref, mk_ref, mv_ref, dy_ref, dhin_ref, dh_ref, dmk_ref, dmv_ref):
        del dhin_ref
        i = pl.program_id(1)
        q, mk, mv = q_ref[...], mk_ref[...], mv_ref[...]
        p = _mem_probs(q, mk)
        pb = p.astype(BF16)
        o = _dot(pb, mv)
        z = z_ref[...].astype(F32)
        sg = _sigmoid(z)
        dy = dy_ref[...].astype(F32)
        do = dy * (z * sg)
        dh_ref[:, LANES:] = (dy * o * (sg * (1.0 + z * (1.0 - sg)))).astype(BF16)
        dob = do.astype(BF16)
        ds = p * (_dot_nt(dob, mv) - jnp.sum(do * o, axis=1, keepdims=True))
        dsb = ds.astype(BF16)
        dh_ref[:, :LANES] = (scale * _dot(dsb, mk)).astype(BF16)
        dmk = scale * _dot_tn(dsb, q)
        dmv = _dot_tn(pb, dob)

        @pl.when(i == 0)
        def _():
            dmk_ref[...] = dmk
            dmv_ref[...] = dmv

        @pl.when(i > 0)
        def _():
            dmk_ref[...] += dmk
            dmv_ref[...] += dmv

    dh, dmk, dmv = pl.pallas_call(
        body,
        name="mem_bwd",
        grid=(M_HEADS, S // TQ),
        in_specs=[pl.BlockSpec((TQ, LANES), lambda h, i: (i, C_M // LANES + 2 * h)),
                  pl.BlockSpec((TQ, LANES), lambda h, i: (i, C_M // LANES + 2 * h + 1)),
                  pl.BlockSpec((nm, LANES), lambda h, i: (0, h)),
                  pl.BlockSpec((nm, LANES), lambda h, i: (0, M_HEADS + h)),
                  pl.BlockSpec((TQ, LANES), lambda h, i: (i, (A_WIDTH + B_WIDTH) // LANES + h)), ANY],
        out_specs=[pl.BlockSpec((TQ, 2 * LANES), lambda h, i: (i, C_M // (2 * LANES) + h)),
                   pl.BlockSpec((nm, LANES), lambda h, i: (0, h)),
                   pl.BlockSpec((nm, LANES), lambda h, i: (0, h))],
        out_shape=[jax.ShapeDtypeStruct(dh.shape, BF16),
                   jax.ShapeDtypeStruct((nm, M_WIDTH), F32), jax.ShapeDtypeStruct((nm, M_WIDTH), F32)],
        input_output_aliases={5: 0},
        compiler_params=_params(("parallel", "arbitrary")),
    )(h, h, mkv, mkv, dycat, dh)
    return dh, jnp.concatenate([dmk, dmv], axis=1)


SUB = 2 * CHUNK
SUBS = [slice(s * SUB, (s + 1) * SUB) for s in range(TQ // SUB)]


def _chunk_masks():
    r = lax.broadcasted_iota(jnp.int32, (SUB, SUB), 0)
    c = lax.broadcasted_iota(jnp.int32, (SUB, SUB), 1)
    same = _chunk_of(r) == _chunk_of(c)
    return jnp.logical_and(same, c <= r), jnp.logical_and(same, c > r), jnp.logical_and(same, c >= r)


def _by_sub(fn):
    return jnp.concatenate([fn(rows) for rows in SUBS], axis=0)


def _gla_gates(lr, gw, gb):
    logit = _dot(lr, gw) + gb
    sg = _sigmoid(logit)
    g = (jnp.minimum(logit, 0.0) - jnp.log(1.0 + jnp.exp(-jnp.abs(logit)))) * (1.0 / GATE_TAU)
    lo = _chunk_masks()[0].astype(BF16)
    return sg, _by_sub(lambda rows: _dot3(lo, g[rows]))


def _gla_factors(q, k, b):
    eb = jnp.exp(b)
    enb = jnp.exp(-b)
    return eb, enb, q * eb, q * enb, k * eb, k * enb


def _gla_intra(qp, qn, kp, kn):
    lo, up, _ = _chunk_masks()
    qp, qn, kp, kn = qp.astype(BF16), qn.astype(BF16), kp.astype(BF16), kn.astype(BF16)
    return [jnp.where(lo, _dot_nt(qp[rows], kn[rows]), 0.0) + jnp.where(up, _dot_nt(qn[rows], kp[rows]), 0.0)
            for rows in SUBS]


B_HPS = 2
B_HW = B_HPS * LANES


def _gla_specs(nb, rev):
    blk = (lambda i: nb - 1 - i) if rev else (lambda i: i)
    qkvz_spec = pl.BlockSpec((TQ, 4 * B_HW), lambda i, p: (blk(i), C_B // (4 * B_HW) + p))
    lr_spec = pl.BlockSpec((TQ, LANES), lambda i, p: (blk(i), C_LR // LANES))
    gw_spec = pl.BlockSpec((LANES, B_HW), lambda i, p: (0, p))
    gb_spec = pl.BlockSpec((1, B_HW), lambda i, p: (0, p))
    gn_spec = pl.BlockSpec((1, LANES), lambda i, p: (0, 0))
    return qkvz_spec, lr_spec, gw_spec, gb_spec, gn_spec, blk


def _head_cols(hh, seg):
    return slice((4 * hh + seg) * LANES, (4 * hh + seg + 1) * LANES)


def _gla_fwd(h, gw, gb, gn, ycat, ycat_t):
    S = h.shape[0]
    nb = S // TQ
    c0 = A_WIDTH // LANES

    def body(qkvz_ref, lr_ref, gw_ref, gb_ref, gn_ref, yin_ref, ytin_ref,
             y_ref, yt_ref, o_ref, st_ref, st_scr):
        del yin_ref, ytin_ref
        i, p = pl.program_id(0), pl.program_id(1)
        for hh in range(B_HPS):
            hd = B_HPS * p + hh
            lane = slice(hh * LANES, (hh + 1) * LANES)

            @pl.when(i == 0)
            def _():
                st_scr[hd] = jnp.zeros((LANES, LANES), F32)

            q = qkvz_ref[:, _head_cols(hh, SEG_Q)].astype(F32) * (B_DK ** -0.5)
            k = qkvz_ref[:, _head_cols(hh, SEG_K)].astype(F32)
            v = qkvz_ref[:, _head_cols(hh, SEG_V)]
            _, b = _gla_gates(lr_ref[...], gw_ref[:, lane], gb_ref[:, lane])
            _, _, qp, qn, kp, kn = _gla_factors(q, k, b)
            o_intra = jnp.concatenate([_dot(a.astype(BF16), v[rows])
                                       for a, rows in zip(_gla_intra(qp, qn, kp, kn), SUBS)], axis=0)
            qpb, knb = qp.astype(BF16), kn.astype(BF16)
            st = st_scr[hd]
            outs = []
            for c in range(CPB):
                rows = slice(c * CHUNK, (c + 1) * CHUNK)
                st_ref[hh, c] = st
                outs.append(_dot_nt(qpb[rows], st.astype(BF16)))
                e_last = jnp.exp(b[(c + 1) * CHUNK - 1:(c + 1) * CHUNK, :])
                st = (st + _dot_tn(v[rows], knb[rows])) * e_last
            st_scr[hd] = st
            o = o_intra + jnp.concatenate(outs, axis=0)
            o_ref[:, lane] = o
            r = lax.rsqrt(jnp.mean(o * o, axis=1, keepdims=True) + RMS_EPS)
            z = qkvz_ref[:, _head_cols(hh, SEG_Z)].astype(F32)
            y = o * r * gn_ref[...] * (z * _sigmoid(z))
            y_ref[:, lane] = y.astype(BF16)
            yt_ref[lane, :] = y.T.astype(BF16)

    qkvz_s, lr_s, gw_s, gb_s, gn_s, _ = _gla_specs(nb, False)
    return pl.pallas_call(
        body,
        name="gla_fwd",
        grid=(nb, B_HEADS // B_HPS),
        in_specs=[qkvz_s, lr_s, gw_s, gb_s, gn_s, ANY, ANY],
        out_specs=[pl.BlockSpec((TQ, B_HW), lambda i, p: (i, c0 // B_HPS + p)),
                   pl.BlockSpec((B_HW, TQ), lambda i, p: (c0 // B_HPS + p, i)),
                   pl.BlockSpec((TQ, B_HW), lambda i, p: (i, p)),
                   pl.BlockSpec((B_HPS, CPB, LANES, LANES), lambda i, p: (p, i, 0, 0))],
        out_shape=[jax.ShapeDtypeStruct(ycat.shape, BF16), jax.ShapeDtypeStruct(ycat_t.shape, BF16),
                   jax.ShapeDtypeStruct((S, B_WIDTH), F32),
                   jax.ShapeDtypeStruct((B_HEADS, S // CHUNK, LANES, LANES), F32)],
        input_output_aliases={5: 0, 6: 1},
        scratch_shapes=[pltpu.VMEM((B_HEADS, LANES, LANES), F32)],
        compiler_params=_params(("arbitrary", "arbitrary")),
    )(h, h, gw, gb, gn, ycat, ycat_t)


def _gla_bwd(h, gw, gb, gn, o_pre, states, dycat, dh):
    S = h.shape[0]
    nb = S // TQ
    n_steps = B_HEADS // B_HPS

    def body(qkvz_ref, lr_ref, gw_ref, gb_ref, gn_ref, o_ref, st_ref, dy_ref, dhin_ref,
             dh_ref, dgw_ref, dgb_ref, dgn_ref,
             dst_scr, dgw_scr, dgb_scr, dgn_scr, dlr_scr):
        del dhin_ref
        i, p = pl.program_id(0), pl.program_id(1)

        @pl.when(jnp.logical_and(i == 0, p == 0))
        def _():
            dgn_scr[...] = jnp.zeros_like(dgn_scr)

        dlr_heads = [one_head(hh, i, p, qkvz_ref, lr_ref, gw_ref, gb_ref, gn_ref, o_ref, st_ref, dy_ref,
                              dh_ref, dgw_ref, dgb_ref, dst_scr, dgw_scr, dgb_scr, dgn_scr) for hh in range(B_HPS)]
        dlr = dlr_heads[0]
        for more in dlr_heads[1:]:
            dlr = dlr + more

        @pl.when(p == 0)
        def _():
            dlr_scr[...] = dlr

        @pl.when(p > 0)
        def _():
            dlr_scr[...] += dlr

        @pl.when(p == n_steps - 1)
        def _():
            cols = _head_cols(B_HPS - 1, SEG_Q)
            dh_ref[:, cols] = (dh_ref[:, cols].astype(F32) + dlr_scr[...]).astype(BF16)

        @pl.when(i == nb - 1)
        def _():
            dgn_ref[...] = dgn_scr[...]

    def one_head(hh, i, p, qkvz_ref, lr_ref, gw_ref, gb_ref, gn_ref, o_ref, st_ref, dy_ref,
                 dh_ref, dgw_ref, dgb_ref, dst_scr, dgw_scr, dgb_scr, dgn_scr):
        hd = B_HPS * p + hh
        lane = slice(hh * LANES, (hh + 1) * LANES)

        @pl.when(i == 0)
        def _():
            dst_scr[hd] = jnp.zeros((LANES, LANES), F32)
            dgw_scr[hd] = jnp.zeros((LANES, LANES), F32)
            dgb_scr[hd] = jnp.zeros((1, LANES), F32)

        q = qkvz_ref[:, _head_cols(hh, SEG_Q)].astype(F32) * (B_DK ** -0.5)
        k = qkvz_ref[:, _head_cols(hh, SEG_K)].astype(F32)
        v = qkvz_ref[:, _head_cols(hh, SEG_V)]
        lr, gwv = lr_ref[...], gw_ref[:, lane]
        sg, b = _gla_gates(lr, gwv, gb_ref[:, lane])
        eb, enb, qp, qn, kp, kn = _gla_factors(q, k, b)
        a = _gla_intra(qp, qn, kp, kn)
        qpb, qnb, kpb, knb = qp.astype(BF16), qn.astype(BF16), kp.astype(BF16), kn.astype(BF16)

        o = o_ref[:, lane]
        gn = gn_ref[...]
        r = lax.rsqrt(jnp.mean(o * o, axis=1, keepdims=True) + RMS_EPS)
        z = qkvz_ref[:, _head_cols(hh, SEG_Z)].astype(F32)
        sz = _sigmoid(z)
        dy = dy_ref[:, lane].astype(F32)
        d_on = dy * (z * sz)
        dh_ref[:, _head_cols(hh, SEG_Z)] = (dy * (o * r * gn) * (sz * (1.0 + z * (1.0 - sz)))).astype(BF16)
        dgn_scr[...] += jnp.sum(d_on * o * r, axis=0, keepdims=True)
        t = d_on * gn
        do = r * t - o * (r * r * r) * jnp.mean(t * o, axis=1, keepdims=True)
        dob = do.astype(BF16)

        lo, up, upper = _chunk_masks()
        dqp, dkn, dqn, dkp, dv = [], [], [], [], []
        for a_s, rows in zip(a, SUBS):
            da = _dot_nt(dob[rows], v[rows])
            dalo = jnp.where(lo, da, 0.0).astype(BF16)
            daup = jnp.where(up, da, 0.0).astype(BF16)
            dqp.append(_dot(dalo, knb[rows]))
            dkn.append(_dot_tn(dalo, qpb[rows]))
            dqn.append(_dot(daup, kpb[rows]))
            dkp.append(_dot_tn(daup, qnb[rows]))
            dv.append(_dot_tn(a_s.astype(BF16), dob[rows]))
        dqp, dkn, dqn, dkp, dv = (jnp.concatenate(x, axis=0) for x in (dqp, dkn, dqn, dkp, dv))

        dst = dst_scr[hd]
        dqp_c, dkn_c, dv_c, dbl_c = [None] * CPB, [None] * CPB, [None] * CPB, [None] * CPB
        for c in reversed(range(CPB)):
            rows = slice(c * CHUNK, (c + 1) * CHUNK)
            st = st_ref[hh, c]
            e_last = jnp.exp(b[(c + 1) * CHUNK - 1:(c + 1) * CHUNK, :])
            if c == CPB - 1:
                st_next = (st + _dot_tn(v[rows], knb[rows])) * e_last
            else:
                st_next = st_ref[hh, c + 1]
            dbl_c[c] = jnp.sum(dst * st_next, axis=0, keepdims=True)
            dtt = (dst * e_last).astype(BF16)
            dv_c[c] = _dot_nt(knb[rows], dtt)
            dkn_c[c] = _dot(v[rows], dtt)
            dqp_c[c] = _dot(dob[rows], st.astype(BF16))
            dst = _dot_tn(dob[rows], qpb[rows]) + dst * e_last
        dst_scr[hd] = dst
        dqp = dqp + jnp.concatenate(dqp_c, axis=0)
        dkn = dkn + jnp.concatenate(dkn_c, axis=0)
        dv = dv + jnp.concatenate(dv_c, axis=0)
        dh_ref[:, _head_cols(hh, SEG_V)] = dv.astype(BF16)
        dh_ref[:, _head_cols(hh, SEG_Q)] = ((dqp * eb + dqn * enb) * (B_DK ** -0.5)).astype(BF16)
        dh_ref[:, _head_cols(hh, SEG_K)] = (dkp * eb + dkn * enb).astype(BF16)

        last = jnp.bitwise_and(lax.broadcasted_iota(jnp.int32, (TQ, 1), 0), CHUNK - 1) == CHUNK - 1
        dbl = jnp.concatenate([jnp.broadcast_to(x, (CHUNK, LANES)) for x in dbl_c], axis=0)
        db = dqp * qp - dqn * qn + dkp * kp - dkn * kn + jnp.where(last, dbl, 0.0)
        upper_b = upper.astype(BF16)
        dlogit = _by_sub(lambda rows: _dot3(upper_b, db[rows])) * (1.0 / GATE_TAU) * (1.0 - sg)
        dlb = dlogit.astype(BF16)
        dgw_scr[hd] += _dot_tn(lr, dlb)
        dgb_scr[hd] += jnp.sum(dlogit, axis=0, keepdims=True)

        @pl.when(i == nb - 1)
        def _():
            dgw_ref[:, lane] = dgw_scr[hd]
            dgb_ref[:, lane] = dgb_scr[hd]

        return _dot_nt(dlb, gwv)

    qkvz_s, lr_s, gw_s, gb_s, gn_s, blk = _gla_specs(nb, True)
    row = pl.BlockSpec((TQ, B_HW), lambda i, p: (blk(i), p))
    dy_spec = pl.BlockSpec((TQ, B_HW), lambda i, p: (blk(i), A_WIDTH // B_HW + p))
    st_spec = pl.BlockSpec((B_HPS, CPB, LANES, LANES), lambda i, p: (p, blk(i), 0, 0))
    return pl.pallas_call(
        body,
        name="gla_bwd",
        grid=(nb, B_HEADS // B_HPS),
        in_specs=[qkvz_s, lr_s, gw_s, gb_s, gn_s, row, st_spec, dy_spec, ANY],
        out_specs=[pl.BlockSpec((TQ, 4 * B_HW), lambda i, p: (blk(i), C_B // (4 * B_HW) + p)),
                   pl.BlockSpec((LANES, B_HW), lambda i, p: (0, jnp.where(i == nb - 1, p, 0))),
                   pl.BlockSpec((1, B_HW), lambda i, p: (0, jnp.where(i == nb - 1, p, 0))),
                   pl.BlockSpec((1, LANES), lambda i, p: (0, 0))],
        out_shape=[jax.ShapeDtypeStruct(dh.shape, BF16),
                   jax.ShapeDtypeStruct((LANES, B_HEADS * LANES), F32),
                   jax.ShapeDtypeStruct((1, B_HEADS * LANES), F32),
                   jax.ShapeDtypeStruct((1, LANES), F32)],
        input_output_aliases={8: 0},
        scratch_shapes=[pltpu.VMEM((B_HEADS, LANES, LANES), F32), pltpu.VMEM((B_HEADS, LANES, LANES), F32),
                        pltpu.VMEM((B_HEADS, 1, LANES), F32), pltpu.VMEM((1, LANES), F32),
                        pltpu.VMEM((TQ, LANES), F32)],
        compiler_params=_params(("arbitrary", "arbitrary")),
    )(h, h, gw, gb, gn, o_pre, states, dycat, dh)


LN_ROWS = 256


def _outproj_ln(ycat, w_out, x, g, b):
    S = x.shape[0]

    def body(yc_ref, w_ref, x_ref, g_ref, b_ref, y_ref, yb_ref, yt_ref, xh_ref, rs_ref):
        u = ALPHA * x_ref[...] + _dot(yc_ref[...], w_ref[...])
        mu = jnp.mean(u, axis=1, keepdims=True)
        d = u - mu
        rstd = lax.rsqrt(jnp.mean(d * d, axis=1, keepdims=True) + LN_EPS)
        xh = d * rstd
        y = xh * g_ref[...] + b_ref[...]
        y_ref[...] = y
        yb_ref[...] = y.astype(BF16)
        yt_ref[...] = y.T.astype(BF16)
        xh_ref[...] = xh
        rs_ref[...] = rstd

    row = lambda w: pl.BlockSpec((LN_ROWS, w), lambda i: (i, 0))
    vec = pl.BlockSpec((1, D_MODEL), lambda i: (0, 0))
    return pl.pallas_call(
        body,
        name="outproj_ln",
        grid=(S // LN_ROWS,),
        in_specs=[row(D_MODEL), pl.BlockSpec((D_MODEL, D_MODEL), lambda i: (0, 0)), row(D_MODEL), vec, vec],
        out_specs=[row(D_MODEL), row(D_MODEL), pl.BlockSpec((D_MODEL, LN_ROWS), lambda i: (0, i)), row(D_MODEL), row(1)],
        out_shape=[jax.ShapeDtypeStruct((S, D_MODEL), F32), jax.ShapeDtypeStruct((S, D_MODEL), BF16),
                   jax.ShapeDtypeStruct((D_MODEL, S), BF16),
                   jax.ShapeDtypeStruct((S, D_MODEL), F32), jax.ShapeDtypeStruct((S, 1), F32)],
        compiler_params=_params(("parallel",)),
    )(ycat, w_out, x, g, b)


def _ln_bwd(dy, xhat, rstd, g):
    S = dy.shape[0]

    def body(dy_ref, xh_ref, rs_ref, g_ref, du_ref, dub_ref, dg_ref, db_ref):
        i = pl.program_id(0)
        dy_, xh = dy_ref[...], xh_ref[...]
        dyg = dy_ * g_ref[...]
        m1 = jnp.mean(dyg, axis=1, keepdims=True)
        m2 = jnp.mean(dyg * xh, axis=1, keepdims=True)
        du = rs_ref[...] * (dyg - m1 - xh * m2)
        du_ref[...] = du
        dub_ref[...] = du.astype(BF16)
        dg = jnp.sum(dy_ * xh, axis=0, keepdims=True)
        db = jnp.sum(dy_, axis=0, keepdims=True)

        @pl.when(i == 0)
        def _():
            dg_ref[...] = dg
            db_ref[...] = db

        @pl.when(i > 0)
        def _():
            dg_ref[...] += dg
            db_ref[...] += db

    row = lambda w: pl.BlockSpec((TQ, w), lambda i: (i, 0))
    vec = pl.BlockSpec((1, D_MODEL), lambda i: (0, 0))
    return pl.pallas_call(
        body,
        name="ln_bwd",
        grid=(S // TQ,),
        in_specs=[row(D_MODEL), row(D_MODEL), row(1), vec],
        out_specs=[row(D_MODEL), row(D_MODEL), vec, vec],
        out_shape=[jax.ShapeDtypeStruct((S, D_MODEL), F32), jax.ShapeDtypeStruct((S, D_MODEL), BF16),
                   jax.ShapeDtypeStruct((1, D_MODEL), F32), jax.ShapeDtypeStruct((1, D_MODEL), F32)],
        compiler_params=_params(("arbitrary",)),
    )(dy, xhat, rstd, g)


def _loss_head(y, target):
    S = y.shape[0]

    def body(y_ref, t_ref, l_ref, dy_ref):
        i = pl.program_id(0)
        err = y_ref[...] - t_ref[...]
        dy_ref[...] = err * (1.0 / D_MODEL)
        part = (0.5 / D_MODEL) * jnp.sum(jnp.sum(err * err, axis=1, keepdims=True), axis=0, keepdims=True)

        @pl.when(i == 0)
        def _():
            l_ref[...] = part

        @pl.when(i > 0)
        def _():
            l_ref[...] += part

    row = pl.BlockSpec((TQ, D_MODEL), lambda i: (i, 0))
    return pl.pallas_call(
        body,
        name="loss_head",
        grid=(S // TQ,),
        in_specs=[row, row],
        out_specs=[pl.BlockSpec((1, 1), lambda i: (0, 0)), row],
        out_shape=[jax.ShapeDtypeStruct((1, 1), F32), jax.ShapeDtypeStruct((S, D_MODEL), F32)],
        compiler_params=_params(("arbitrary",)),
    )(y, target)


def _pad_gate(gate_w, gate_b):
    gw = gate_w.reshape(GATE_RANK, B_HEADS, B_DK)
    gw = jnp.pad(gw, ((LR_LANE, LANES - LR_LANE - GATE_RANK), (0, 0), (0, LANES - B_DK))).reshape(LANES, B_HEADS * LANES)
    gb = jnp.pad(gate_b.reshape(B_HEADS, B_DK), ((0, 0), (0, LANES - B_DK))).reshape(1, B_HEADS * LANES)
    return gw.astype(BF16), gb.astype(F32)


def _layer_fwd(x, xb, xt, mem_b, w_in, w_kv, w_out, u, gw, gb, gn, ln_g, ln_b, rider=None):
    h = _matmul(xb, w_in, mode="nn", out_dtype=BF16, tm=1024, tn=1792, tk=D_MODEL, name="in_proj", rider=rider)
    if rider is not None:
        h, rode = h
    mkv = _matmul(mem_b, w_kv, mode="nn", out_dtype=BF16, tm=mem_b.shape[0], tn=1024, tk=D_MODEL, name="mem_kv")
    ycat, ycat_t = _band_fwd(h, u)
    ycat, ycat_t, o_pre, states = _gla_fwd(h, gw, gb, gn, ycat, ycat_t)
    ycat, ycat_t = _mem_fwd(h, mkv, ycat, ycat_t)
    y, ybf, yt, xhat, rstd = _outproj_ln(ycat, w_out, x, ln_g, ln_b)
    saved = (xt, h, mkv, ycat_t, o_pre, states, xhat, rstd)
    return (y, ybf, yt, saved) if rider is None else (y, ybf, yt, saved, rode)


def _layer_bwd(dy, saved, mem_b, w_in, w_out, u, gw, gb, gn, ln_g, reduce=None, own_reduce=None):
    xt, h, mkv, ycat_t, o_pre, states, xhat, rstd = saved

    def riding(**kw):
        if reduce is None:
            return _matmul(**kw)
        out, bufs = _matmul(rider=reduce.rider(), **kw)
        reduce.landed(bufs)
        return out

    du, dub, d_ln_g, d_ln_b = _ln_bwd(dy, xhat, rstd, ln_g)
    dycat = riding(a=dub, b=w_out, mode="nt", out_dtype=BF16, tm=1024, tn=1024, tk=D_MODEL, name="dycat")
    d_w_out = _matmul(ycat_t, dub, mode="nn", out_dtype=F32, tm=1024, tn=1024, tk=min(4096, dub.shape[0]), name="d_w_out")
    dh, d_u = _band_bwd(h, u, dycat)
    dh, dgw, dgb, dgn = _gla_bwd(h, gw, gb, gn, o_pre, states, dycat, dh)
    dh, dmkv = _mem_bwd(h, mkv, dycat, dh)
    d_w_kv = _matmul(mem_b, dmkv, mode="tn", out_dtype=F32, tm=1024, tn=1024, tk=mem_b.shape[0], name="d_w_kv")
    dx_args = dict(a=dh, b=w_in, mode="nt", out_dtype=F32, tm=1024, tn=1024, tk=3584, name="dx", add=du, add_scale=ALPHA)
    dw_args = dict(a=xt, b=dh, mode="nn", out_dtype=F32, tm=1024, tn=1024, tk=min(4096, dh.shape[0]), name="d_w_in")
    if own_reduce is None:
        dx = riding(**dx_args)
        d_w_in = riding(**dw_args)
        return dx, (d_w_in, d_u, dgw, dgb, dgn, d_w_kv, d_w_out, d_ln_g, d_ln_b)
    d_w_in = riding(**dw_args)
    grads = (d_w_in, d_u, dgw, dgb, dgn, d_w_kv, d_w_out, d_ln_g, d_ln_b)
    own = own_reduce(grads, reduce.finish() if reduce is not None else None)
    own.step()
    dx, bufs = _matmul(rider=own.rider(), **dx_args)
    own.landed(bufs)
    return dx, grads, own


def _unpad_heads(w):
    r = w.shape[0]
    return w.reshape(r, B_HEADS, LANES)[:, :, :B_DK].reshape(r, B_KEY_WIDTH)


def _padded_col_of():
    col, o = np.zeros(IN_WIDTH, np.int64), 0
    for seg in (SEG_Q, SEG_K, SEG_V, SEG_Z):
        for hd in range(A_HEADS):
            col[o:o + LANES] = C_A + (hd // A_HPS) * 4 * A_HW + seg * A_HW + (hd % A_HPS) * LANES + np.arange(LANES)
            o += LANES
    for seg, width in ((SEG_Q, B_DK), (SEG_K, B_DK), (SEG_V, LANES), (SEG_Z, LANES)):
        for hd in range(B_HEADS):
            col[o:o + width] = C_B + hd * 4 * LANES + seg * LANES + np.arange(width)
            o += width
    col[o:o + GATE_RANK] = C_LR + LR_LANE + np.arange(GATE_RANK)
    o += GATE_RANK
    for seg in (0, 1):
        for hd in range(M_HEADS):
            col[o:o + LANES] = C_M + hd * 2 * LANES + seg * LANES + np.arange(LANES)
            o += LANES
    assert o == IN_WIDTH
    return col


def _runs(idx):
    out, start = [], 0
    for k in range(1, len(idx) + 1):
        if k == len(idx) or idx[k] != idx[k - 1] + 1:
            out.append((int(idx[start]), k - start))
            start = k
    return out


def _chip_columns(g, j, n):
    runs = _runs(_padded_col_of()[j * n:(j + 1) * n])
    return jnp.concatenate([g[:, a:a + ln] for a, ln in runs], axis=1)


def _padded_from_shards(shards):
    n = shards[0].shape[1]
    src = np.full(HP, -1, np.int64)
    src[_padded_col_of()] = np.arange(IN_WIDTH)
    parts, k = [], 0
    while k < HP:
        e = k + 1
        if src[k] < 0:
            while e < HP and src[e] < 0:
                e += 1
            parts.append(jnp.zeros((shards[0].shape[0], e - k), shards[0].dtype))
        else:
            while e < HP and src[e] == src[e - 1] + 1 and src[e] // n == src[k] // n:
                e += 1
            parts.append(shards[src[k] // n][:, src[k] % n:src[k] % n + e - k])
        k = e
    return jnp.concatenate(parts, axis=1)


ADAMW_BLOCK_BYTES = 1 << 20


def _adamw(w, g, m, v, name):
    L, R, C = w.shape
    tl, tr = 1, R
    if R * C * 4 <= ADAMW_BLOCK_BYTES:
        tl = max(d for d in range(1, L + 1) if L % d == 0 and d * R * C * 4 <= ADAMW_BLOCK_BYTES)
    else:
        for cand in (256, 128, 64, 32, 16, 8):
            if R % cand == 0 and R > cand:
                tr = cand
                break

    def body(w_ref, g_ref, m_ref, v_ref, d_ref, nm_ref, nv_ref):
        g_ = g_ref[...]
        nm = ADAM_B1 * m_ref[...] + (1.0 - ADAM_B1) * g_
        nv = ADAM_B2 * v_ref[...] + (1.0 - ADAM_B2) * (g_ * g_)
        m_hat = nm / (1.0 - ADAM_B1 ** ADAM_STEP)
        v_hat = nv / (1.0 - ADAM_B2 ** ADAM_STEP)
        d_ref[...] = -ADAM_LR * (m_hat / (jnp.sqrt(v_hat) + ADAM_EPS) + ADAM_WD * w_ref[...])
        nm_ref[...] = nm
        nv_ref[...] = nv

    spec = pl.BlockSpec((tl, tr, C), lambda l, i: (l, i, 0))
    sd = jax.ShapeDtypeStruct((L, R, C), F32)
    return pl.pallas_call(
        body, name=name, grid=(L // tl, R // tr), in_specs=[spec] * 4, out_specs=[spec] * 3, out_shape=[sd] * 3,
        compiler_params=_params(("parallel", "parallel")),
    )(w, g, m, v)


def _adamw_nd(w, g, m, v, name):
    shape = w.shape
    f = (lambda a: a) if w.ndim == 3 else (lambda a: a.reshape(1, shape[0], shape[1]))
    return tuple(o.reshape(shape) for o in _adamw(f(w), f(g), f(m), f(v), name))


ANY = pl.BlockSpec(memory_space=pl.ANY)


def _place():
    x, y, c = lax.axis_index("x"), lax.axis_index("y"), lax.axis_index("c")
    chips = [(1 - x, y), (x, 1 - y), (1 - x, 1 - y)]
    return x, y, c, chips


class _WeightGather:
    def __init__(self, hop, layer, rows):
        self.hop, self.layer, self.rows = hop, layer, rows
        self.n_sem = 3 * len(rows)

    def _copies(self, shard_refs, buf_refs, send, recv, received):
        x, y, c, chips = _place()
        out = []
        for t, R in enumerate(self.rows):
            half = R // 2
            assert half % 16 == 0
            mine = pl.ds(pl.multiple_of(c * half, 16), half)
            other = pl.ds(pl.multiple_of((1 - c) * half, 16), half)
            mine_of_shard = pl.ds(pl.multiple_of(self.layer * R + c * half, 16), half)
            for k, chip in enumerate(chips):
                theirs = buf_refs[t].at[2 * chip[0] + chip[1]]
                if self.hop == "chips":
                    src, dst, to = shard_refs[t].at[mine_of_shard], buf_refs[t].at[2 * x + y, mine], (*chip, c)
                    landed = theirs.at[mine]
                else:
                    src, dst, to = theirs.at[mine], theirs.at[mine], (x, y, 1 - c)
                    landed = theirs.at[other]
                out.append(pltpu.make_async_remote_copy(
                    src_ref=src, dst_ref=landed if received else dst, send_sem=send.at[3 * t + k],
                    recv_sem=recv.at[3 * t + k], device_id=to, device_id_type=MESH))
        return out

    def start(self, shard_refs, buf_refs, send, recv):
        for cp in self._copies(shard_refs, buf_refs, send, recv, False):
            cp.start()

    def wait(self, shard_refs, buf_refs, send, recv):
        for cp in self._copies(shard_refs, buf_refs, send, recv, True):
            cp.wait_recv()
        for cp in self._copies(shard_refs, buf_refs, send, recv, False):
            cp.wait_send()

    def sems(self):
        return [pltpu.SemaphoreType.DMA((self.n_sem,)), pltpu.SemaphoreType.DMA((self.n_sem,))]

    def call(self, srcs, bufs, name):
        ns, nb = len(srcs), len(bufs)

        def body(*refs):
            src_refs, buf_refs, (send, recv) = refs[:ns], refs[ns + nb:ns + 2 * nb], refs[ns + 2 * nb:]
            self.start(src_refs, buf_refs, send, recv)
            self.wait(src_refs, buf_refs, send, recv)

        return pl.pallas_call(
            body, name=name, in_specs=[ANY] * (ns + nb), out_specs=[ANY] * nb,
            out_shape=[jax.ShapeDtypeStruct(b.shape, b.dtype) for b in bufs],
            input_output_aliases={ns + t: t for t in range(nb)},
            scratch_shapes=self.sems(),
        )(*srcs, *bufs)


class _GradHop(_WeightGather):
    def __init__(self, hop, layer, slices):
        self.hop, self.layer, self.slices = hop, layer, slices
        self.n_sem = {"pair": N_CHIPS, "chips": N_CHIPS - 1, "gather": 1}[hop] * len(slices)

    def _copies(self, src_refs, buf_refs, send, recv, received):
        x, y, c, chips = _place()
        me, out = 2 * x + y, []

        def remote(src, dst, to):
            k = len(out)
            out.append(pltpu.make_async_remote_copy(src_ref=src, dst_ref=dst, send_sem=send.at[k], recv_sem=recv.at[k],
                                                    device_id=to, device_id_type=MESH))

        for t, (half, where) in enumerate(self.slices):
            if self.hop == "pair":
                for j, (a, first) in enumerate(where):
                    rows = pl.ds(pl.multiple_of(first + (1 - c) * half, 8), half)
                    remote(src_refs[a].at[rows], buf_refs[t].at[j], (x, y, 1 - c))
            elif self.hop == "chips":
                for chip in chips:
                    slot = 2 * chip[0] + chip[1]
                    remote(src_refs[t].at[slot], buf_refs[t].at[slot if received else me], (*chip, c))
            else:
                mine = buf_refs[t].at[self.layer, c]
                remote(mine, buf_refs[t].at[self.layer, 1 - c] if received else mine, (x, y, 1 - c))
        return out


def _add_halves(parts, got, c_idx, name):
    n, L, half, C = got.shape
    tr = 64

    def body(*refs):
        ins, (got_ref, o_ref) = refs[1:1 + len(parts)], refs[1 + len(parts):]
        for k in range(len(parts)):
            o_ref[k // L, k % L] = (ins[k][...] + got_ref[k // L, k % L]).astype(BF16)

    def rows_of(first):
        assert first % tr == 0 and half % tr == 0
        return lambda i, c: (first // tr + c[0] * (half // tr) + i, 0)

    whole = pl.BlockSpec((n, L, tr, C), lambda i, c: (0, 0, i, 0))
    return pl.pallas_call(
        body, name=name,
        grid_spec=pltpu.PrefetchScalarGridSpec(
            num_scalar_prefetch=1, grid=(half // tr,),
            in_specs=[pl.BlockSpec((tr, C), rows_of(first)) for _, first in parts] + [whole],
            out_specs=whole),
        out_shape=jax.ShapeDtypeStruct((n, L, half, C), BF16),
        compiler_params=_params(("parallel",)),
    )(c_idx, *[a for a, _ in parts], got)


def _add_slots(r, c_idx, dest, layer, name):
    n, half, C = r.shape
    tr = 256

    def body(c_ref, r_ref, dest_ref, o_ref):
        del dest_ref
        acc = r_ref[0].astype(F32)
        for j in range(1, n):
            acc = acc + r_ref[j].astype(F32)
        o_ref[0, 0] = acc

    return pl.pallas_call(
        body, name=name,
        grid_spec=pltpu.PrefetchScalarGridSpec(
            num_scalar_prefetch=1, grid=(half // tr,),
            in_specs=[pl.BlockSpec((n, tr, C), lambda i, c: (0, i, 0)), ANY],
            out_specs=pl.BlockSpec((1, 1, tr, C), lambda i, c: (layer, c[0], i, 0))),
        out_shape=jax.ShapeDtypeStruct(dest.shape, F32),
        input_output_aliases={2: 0},
        compiler_params=_params(("parallel",)),
    )(c_idx, r, dest)


class _LayerReduce:
    def __init__(self, layer, grads, slices, c_idx, chip, dests):
        self.layer, self.grads, self.slices, self.c_idx, self.chip, self.dests = layer, grads, slices, c_idx, chip, dests
        self.widths = [grads[where[0][0]].shape[1] for _, where in slices]
        self.stage = 0

    def _hop(self, kind):
        return _GradHop(kind, self.layer, self.slices)

    def rider(self):
        if self.stage == 0:
            got = [lax.empty((N_CHIPS, half, w), F32) for (half, _), w in zip(self.slices, self.widths)]
            return self._hop("pair"), self.grads, got
        if self.stage == 1:
            q = [lax.empty(p.shape, BF16) for p in self.pair_sums]
            return self._hop("chips"), self.pair_sums, q
        return self._hop("gather"), [], self.dests

    def landed(self, bufs):
        tag = f"{self.layer}"
        if self.stage == 0:
            self.pair_sums = []
            for t, ((half, where), got) in enumerate(zip(self.slices, bufs)):
                parts = [(self.grads[a], first) for a, first in where]
                p = _add_halves(parts, got[:, None], self.c_idx, f"rs_add2_{t}_{tag}")
                self.pair_sums.append(p.reshape(N_CHIPS, half, p.shape[-1]))
        elif self.stage == 1:
            for t, (q, p) in enumerate(zip(bufs, self.pair_sums)):
                q = lax.dynamic_update_slice_in_dim(q, lax.dynamic_slice_in_dim(p, self.chip, 1, axis=0), self.chip, axis=0)
                self.dests[t] = _add_slots(q, self.c_idx, self.dests[t], self.layer, f"rs_add4_{t}_{tag}")
        else:
            self.dests = list(bufs)
        self.stage += 1

    def step(self):
        hop, srcs, bufs = self.rider()
        self.landed(hop.call(srcs, bufs, f"rs_{hop.hop}_{self.layer}"))

    def finish(self):
        while self.stage < 3:
            self.step()
        return self.dests


def _all_reduce_small(buf, name):
    R = buf.shape[0]

    def flipped(k, x, y, c):
        return ((1 - x) if k & 4 else x, (1 - y) if k & 2 else y, (1 - c) if k & 1 else c)

    def body(b_ref, o_ref, land, send, recv):
        x, y, c, _ = _place()
        me = 4 * x + 2 * y + c
        land[me] = b_ref[...]
        cps = []
        for k in range(1, N_DEV):
            peer = flipped(k, x, y, c)
            cps.append(pltpu.make_async_remote_copy(src_ref=b_ref, dst_ref=land.at[me], send_sem=send.at[k - 1],
                                                    recv_sem=recv.at[k - 1], device_id=peer, device_id_type=MESH))
        for cp in cps:
            cp.start()
        for k in range(1, N_DEV):
            peer = flipped(k, x, y, c)
            slot = 4 * peer[0] + 2 * peer[1] + peer[2]
            pltpu.make_async_remote_copy(src_ref=b_ref, dst_ref=land.at[slot], send_sem=send.at[k - 1],
                                         recv_sem=recv.at[k - 1], device_id=peer, device_id_type=MESH).wait_recv()
        for cp in cps:
            cp.wait_send()
        acc = land[0]
        for j in range(1, N_DEV):
            acc = acc + land[j]
        o_ref[...] = acc

    vm = pl.BlockSpec(memory_space=pltpu.VMEM)
    return pl.pallas_call(
        body, name=name, in_specs=[vm], out_specs=vm,
        out_shape=jax.ShapeDtypeStruct((R, LANES), F32),
        scratch_shapes=[pltpu.VMEM((N_DEV, R, LANES), F32), pltpu.SemaphoreType.DMA((N_DEV - 1,)),
                        pltpu.SemaphoreType.DMA((N_DEV - 1,))],
    )(buf)


def kernel(x, mem, w_in, a_rel_bias, b_gate_w, b_gate_b, b_norm_g, w_mem_kv, w_out, ln_g, ln_b, loss_target, m_w_in, m_a_rel_bias, m_b_gate_w, m_b_gate_b, m_b_norm_g, m_w_mem_kv, m_w_out, m_ln_g, m_ln_b, v_w_in, v_a_rel_bias, v_b_gate_w, v_b_gate_b, v_b_norm_g, v_w_mem_kv, v_w_out, v_ln_g, v_ln_b):
    L = w_in.shape[0]
    S = x.shape[1]
    cx, cy, cc = lax.axis_index("x"), lax.axis_index("y"), lax.axis_index("c")
    chip = 2 * cx + cy
    c_idx = jnp.reshape(cc, (1,)).astype(jnp.int32)

    n_in, r_kv, r_out = w_in.shape[2], w_mem_kv.shape[1], w_out.shape[1]
    shards = [w.astype(BF16).reshape(-1, w.shape[2]) for w in (w_in, w_mem_kv, w_out)]
    rows = [D_MODEL, r_kv, r_out]

    def landing(l):
        return [lax.dynamic_update_slice_in_dim(lax.empty((N_CHIPS, r, s.shape[1]), BF16),
                                                s[l * r:(l + 1) * r][None], chip, axis=0)
                for s, r in zip(shards, rows)]

    def assembled(bufs):
        return (_padded_from_shards([bufs[0][j] for j in range(N_CHIPS)]),
                bufs[1].reshape(D_MODEL, bufs[1].shape[2]), bufs[2].reshape(D_MODEL, D_MODEL))

    bufs0 = _WeightGather("chips", 0, rows).call(shards, landing(0), "gather_chips_0")
    weights = [assembled(_WeightGather("pair", 0, rows).call(shards, bufs0, "gather_pair_0"))]

    gw_cols = b_gate_w.shape[2]
    gw_slot = jnp.zeros((N_CHIPS, L, GATE_RANK, gw_cols), F32)
    gw_slot = lax.dynamic_update_slice(gw_slot, (0.5 * b_gate_w)[None], (chip, 0, 0, 0))
    gw_flat = gw_slot.reshape(-1)
    n_gw = gw_flat.shape[0]
    pad = (-n_gw) % (8 * LANES)
    gw_full = _all_reduce_small(jnp.pad(gw_flat, (0, pad)).reshape(-1, LANES), "gather_gate_w").reshape(-1)[:n_gw]
    gw_full = gw_full.reshape(N_CHIPS, L, GATE_RANK, gw_cols).transpose(1, 2, 0, 3).reshape(L, GATE_RANK, B_KEY_WIDTH)

    xs = x.reshape(S, D_MODEL)
    mem_b = mem.reshape(mem.shape[1], D_MODEL).astype(BF16)
    target = loss_target.reshape(S, D_MODEL)

    small_w = []
    for l in range(L):
        gw_l, gb_l = _pad_gate(gw_full[l], b_gate_b[l])
        small_w.append((_bias_by_offset(a_rel_bias[l]), gw_l, gb_l,
                        b_norm_g[l].reshape(1, LANES), ln_g[l].reshape(1, D_MODEL), ln_b[l].reshape(1, D_MODEL)))

    y, yb = xs, xs.astype(BF16)
    yt = _transpose(yb, "x_t")
    saved = []
    for l in range(L):
        if l + 1 < L:
            rider = (_WeightGather("chips", l + 1, rows), shards, landing(l + 1))
            y, yb, yt, sv, bufs = _layer_fwd(y, yb, yt, mem_b, *weights[l], *small_w[l], rider=rider)
            weights.append(assembled(_WeightGather("pair", l + 1, rows).call(shards, bufs, f"gather_pair_{l + 1}")))
        else:
            y, yb, yt, sv = _layer_fwd(y, yb, yt, mem_b, *weights[l], *small_w[l])
        saved.append(sv)
    layer_w = [(*weights[l], *small_w[l]) for l in range(L)]
    loss_part, dy = _loss_head(y, target)

    halves = [D_MODEL // 2, r_kv // 2, r_out // 2]
    dests = [lax.empty((L, 2, hf, w.shape[2]), F32) for hf, w in zip(halves, (w_in, w_mem_kv, w_out))]
    slices = [(halves[0], [(j, 0) for j in range(N_CHIPS)]),
              (halves[1], [(N_CHIPS, j * r_kv) for j in range(N_CHIPS)]),
              (halves[2], [(N_CHIPS + 1, j * r_out) for j in range(N_CHIPS)])]

    def reduction(l, g, into):
        arrays = [_chip_columns(g[0], j, n_in) for j in range(N_CHIPS)] + [g[5], g[6]]
        return _LayerReduce(l, arrays, slices, c_idx, chip, into)

    grads, reduce = [None] * L, None
    for l in reversed(range(L)):
        w_in_l, w_kv_l, w_out_l, u_l, gw_l, gb_l, gn_l, lg_l, lb_l = layer_w[l]
        args = (dy, saved[l], mem_b, w_in_l, w_out_l, u_l, gw_l, gb_l, gn_l, lg_l)
        if l > 0:
            dy, grads[l] = _layer_bwd(*args, reduce=reduce)
            if reduce is not None:
                dests = reduce.finish()
            reduce = reduction(l, grads[l], dests)
        else:
            own = lambda g, above: reduction(0, g, dests if above is None else above)
            dy, grads[l], reduce = _layer_bwd(*args, reduce=reduce, own_reduce=own)
    r_w_in, r_w_kv, r_w_out = [d.reshape(L, 2 * d.shape[2], d.shape[3]) for d in reduce.finish()]
    grad_x = dy.reshape(x.shape)

    g_rel = jnp.stack([_bias_grad_from_offset(g[1]) for g in grads])
    g_gw = jnp.stack([_unpad_heads(g[2][LR_LANE:LR_LANE + GATE_RANK]) for g in grads])
    g_gb = jnp.stack([_unpad_heads(g[3])[0] for g in grads])
    g_gn = jnp.stack([g[4][0] for g in grads])
    g_lg = jnp.stack([g[7][0] for g in grads])
    g_lb = jnp.stack([g[8][0] for g in grads])

    small = [g_rel, g_gw, g_gb, g_gn, g_lg, g_lb, loss_part]
    flat = jnp.concatenate([s.reshape(-1) for s in small])
    n_small = flat.shape[0]
    pad = (-n_small) % (8 * LANES)
    red = _all_reduce_small(jnp.pad(flat, (0, pad)).reshape(-1, LANES), "all_reduce_small").reshape(-1)
    outs, off = [], 0
    for s in small:
        outs.append(red[off:off + s.size].reshape(s.shape))
        off += s.size
    g_rel, g_gw, g_gb, g_gn, g_lg, g_lb, loss = outs
    loss = loss.reshape(())
    g_gw = lax.dynamic_slice_in_dim(g_gw.reshape(L, GATE_RANK, N_CHIPS, gw_cols), chip, 1, axis=2).reshape(L, GATE_RANK, gw_cols)

    g_list = [r_w_in, g_rel, g_gw, g_gb, g_gn, r_w_kv, r_w_out, g_lg, g_lb]
    w_list = [w_in, a_rel_bias, b_gate_w, b_gate_b, b_norm_g, w_mem_kv, w_out, ln_g, ln_b]
    m_list = [m_w_in, m_a_rel_bias, m_b_gate_w, m_b_gate_b, m_b_norm_g, m_w_mem_kv, m_w_out, m_ln_g, m_ln_b]
    v_list = [v_w_in, v_a_rel_bias, v_b_gate_w, v_b_gate_b, v_b_norm_g, v_w_mem_kv, v_w_out, v_ln_g, v_ln_b]
    names = ["w_in", "rel", "gate_w", "gate_b", "norm_g", "w_kv", "w_out", "ln_g", "ln_b"]
    to_cols = lambda a: jnp.transpose(a, (2, 0, 1))
    upd = [tuple(jnp.transpose(o, (1, 2, 0)) for o in
                 _adamw(to_cols(w_in), to_cols(r_w_in), to_cols(m_w_in), to_cols(v_w_in), "adamw_w_in"))]
    upd += [_adamw_nd(w, g, m, v, "adamw_" + n)
            for w, g, m, v, n in list(zip(w_list, g_list, m_list, v_list, names))[1:]]
    deltas = [u_[0] for u_ in upd]
    new_m = [u_[1] for u_ in upd]
    new_v = [u_[2] for u_ in upd]
    return (loss, grad_x, *g_list, *deltas, *new_m, *new_v)
```

```python
import functools

import numpy as np
import jax
import jax.numpy as jnp
from jax import lax
from jax.experimental import pallas as pl
from jax.experimental.pallas import tpu as pltpu

F32 = jnp.float32
BF16 = jnp.bfloat16
MESH = pl.DeviceIdType.MESH

D_MODEL = 2048
DEPTH = 4
CHUNK = 64
LEFT_CHUNKS = 8
MAX_REL = 128
N_REL = 2 * MAX_REL + 1
A_HEADS = 8
HEAD_DIM = 128
B_HEADS = 4
B_DK = 64
M_HEADS = 4
GATE_RANK = 16
GATE_TAU = 16.0
A_WIDTH = A_HEADS * HEAD_DIM
B_WIDTH = B_HEADS * HEAD_DIM
B_KEY_WIDTH = B_HEADS * B_DK
M_WIDTH = M_HEADS * HEAD_DIM
IN_WIDTH = 4 * A_WIDTH + 2 * B_KEY_WIDTH + 2 * B_WIDTH + GATE_RANK + 2 * M_WIDTH
ALPHA = (2.0 * DEPTH) ** 0.25
LN_EPS = 1e-5
RMS_EPS = 1e-6
NEG_INF = -1e30
ADAM_LR = 0.001
ADAM_B1 = 0.9
ADAM_B2 = 0.999
ADAM_EPS = 1e-08
ADAM_WD = 0.01
ADAM_STEP = 10

LANES = 128
VMEM_LIMIT = 56 * 1024 * 1024

C_A, C_B, C_M = 0, 4096, 6144
HP = 7168
LR_HEAD, LR_LANE = B_HEADS - 1, B_DK
C_LR = C_B + LR_HEAD * 4 * LANES
SEG_Q, SEG_K, SEG_V, SEG_Z = 0, 1, 2, 3
TQ = 512
CPB = TQ // CHUNK
N_CHIPS = 4
N_DEV = 8


def _params(sem, vmem=VMEM_LIMIT):
    return pltpu.CompilerParams(dimension_semantics=sem, vmem_limit_bytes=vmem)


def _dot(a, b):
    return jnp.dot(a, b, preferred_element_type=F32)


def _dot_nt(a, b):
    return lax.dot_general(a, b, (((1,), (1,)), ((), ())), preferred_element_type=F32)


def _dot_tn(a, b):
    return lax.dot_general(a, b, (((0,), (0,)), ((), ())), preferred_element_type=F32)


def _sigmoid(x):
    return 1.0 / (1.0 + jnp.exp(-x))


def _split3(x):
    hi = x.astype(BF16)
    r = x - hi.astype(F32)
    mid = r.astype(BF16)
    lo = (r - mid.astype(F32)).astype(BF16)
    return hi, mid, lo


def _dot3(m_bf, x):
    hi, mid, lo = _split3(x)
    return _dot(m_bf, hi) + _dot(m_bf, mid) + _dot(m_bf, lo)


def _matmul(a, b, *, mode, out_dtype, tm, tn, tk, name, add=None, add_scale=1.0, rider=None):
    if mode == "nn":
        (M, K), (K2, N) = a.shape, b.shape
        a_spec = pl.BlockSpec((tm, tk), lambda i, j, k: (i, k))
        b_spec = pl.BlockSpec((tk, tn), lambda i, j, k: (k, j))
        dot = _dot
    elif mode == "nt":
        (M, K), (N, K2) = a.shape, b.shape
        a_spec = pl.BlockSpec((tm, tk), lambda i, j, k: (i, k))
        b_spec = pl.BlockSpec((tn, tk), lambda i, j, k: (j, k))
        dot = _dot_nt
    else:
        (K, M), (K2, N) = a.shape, b.shape
        a_spec = pl.BlockSpec((tk, tm), lambda i, j, k: (k, i))
        b_spec = pl.BlockSpec((tk, tn), lambda i, j, k: (k, j))
        dot = _dot_tn
    assert K == K2 and M % tm == 0 and N % tn == 0 and K % tk == 0, (a.shape, b.shape, mode)
    nk = K // tk
    has_add = add is not None
    assert nk == 1 or out_dtype == F32
    grid = (M // tm, N // tn, nk)
    hop, srcs, bufs = rider if rider is not None else (None, [], [])
    n_in = 2 + has_add

    def body(*refs):
        a_ref, b_ref = refs[:2]
        add_ref = refs[2] if has_add else None
        src_refs = refs[n_in:n_in + len(srcs)]
        o_ref = refs[n_in + len(srcs) + len(bufs)]
        buf_refs = refs[n_in + len(srcs) + len(bufs) + 1:n_in + len(srcs) + 2 * len(bufs) + 1]
        sems = refs[n_in + len(srcs) + 2 * len(bufs) + 1:]
        i, j, k = pl.program_id(0), pl.program_id(1), pl.program_id(2)
        if hop is not None:
            @pl.when(jnp.logical_and(jnp.logical_and(i == 0, j == 0), k == 0))
            def _():
                hop.start(src_refs, buf_refs, *sems)

        part = dot(a_ref[...].astype(BF16), b_ref[...].astype(BF16))

        @pl.when(k == 0)
        def _():
            first = part + add_scale * add_ref[...] if has_add else part
            o_ref[...] = first.astype(out_dtype)

        if nk > 1:
            @pl.when(k > 0)
            def _():
                o_ref[...] += part

        if hop is not None:
            @pl.when(jnp.logical_and(jnp.logical_and(i == grid[0] - 1, j == grid[1] - 1), k == nk - 1))
            def _():
                hop.wait(src_refs, buf_refs, *sems)

    in_specs = [a_spec, b_spec]
    args = [a, b]
    if has_add:
        in_specs.append(pl.BlockSpec((tm, tn), lambda i, j, k: (i, j)))
        args.append(add)
    out = pl.pallas_call(
        body,
        name=name,
        grid=grid,
        in_specs=in_specs + [ANY] * (len(srcs) + len(bufs)),
        out_specs=[pl.BlockSpec((tm, tn), lambda i, j, k: (i, j))] + [ANY] * len(bufs),
        out_shape=[jax.ShapeDtypeStruct((M, N), out_dtype)] + [jax.ShapeDtypeStruct(x.shape, x.dtype) for x in bufs],
        input_output_aliases={n_in + len(srcs) + t: 1 + t for t in range(len(bufs))},
        scratch_shapes=hop.sems() if hop is not None else [],
        compiler_params=_params(("parallel", "parallel", "arbitrary") if hop is None
                                else ("arbitrary", "arbitrary", "arbitrary")),
    )(*args, *srcs, *bufs)
    return out[0] if hop is None else (out[0], list(out[1:]))


def _transpose(a, name):
    R, C = a.shape
    t = 512

    def body(a_ref, o_ref):
        o_ref[...] = a_ref[...].T

    return pl.pallas_call(
        body, name=name, grid=(R // t, C // t),
        in_specs=[pl.BlockSpec((t, t), lambda i, j: (i, j))],
        out_specs=pl.BlockSpec((t, t), lambda i, j: (j, i)),
        out_shape=jax.ShapeDtypeStruct((C, R), a.dtype),
        compiler_params=_params(("parallel", "parallel")),
    )(a)


def _chunk_of(rows):
    return lax.shift_right_logical(rows, CHUNK.bit_length() - 1)


A_HPS = 2
A_HW = A_HPS * LANES


def _band_bias(u_row, first):
    bias = pltpu.roll(jnp.broadcast_to(u_row, (TQ, 2 * TQ)), 0, 1, stride=1, stride_axis=0)
    qc = _chunk_of(lax.broadcasted_iota(jnp.int32, (TQ, 2 * TQ), 0))
    col = lax.broadcasted_iota(jnp.int32, (TQ, 2 * TQ), 1)
    kc = _chunk_of(jnp.bitwise_and(col, TQ - 1))
    ok = jnp.logical_or(jnp.logical_and(col < TQ, kc >= qc), jnp.logical_and(col >= TQ, kc <= qc))
    return jnp.where(ok, bias, NEG_INF) + jnp.where(col < TQ, first * NEG_INF, 0.0)


HQ = TQ // 2
HALVES = ((slice(0, HQ), slice(0, 3 * HQ)),
          (slice(HQ, TQ), slice(HQ, 4 * HQ)))


def _band_probs(q, kcat, bias):
    scale = HEAD_DIM ** -0.5
    out = []
    for rows, cols in HALVES:
        s = _dot_nt(q[rows], kcat[cols]) * scale + bias[rows, cols]
        p = jnp.exp(s - jnp.max(s, axis=1, keepdims=True))
        out.append((p, 1.0 / jnp.sum(p, axis=1, keepdims=True)))
    return out


def _band_specs(nq):
    def col(seg, h):
        return C_A // A_HW + 4 * h + seg

    q_spec = pl.BlockSpec((TQ, A_HW), lambda h, i: (jnp.minimum(i, nq - 1), col(SEG_Q, h)))
    kp_spec = pl.BlockSpec((TQ, A_HW), lambda h, i: (jnp.clip(i - 1, 0, nq - 1), col(SEG_K, h)))
    kc_spec = pl.BlockSpec((TQ, A_HW), lambda h, i: (jnp.minimum(i, nq - 1), col(SEG_K, h)))
    vp_spec = pl.BlockSpec((TQ, A_HW), lambda h, i: (jnp.clip(i - 1, 0, nq - 1), col(SEG_V, h)))
    vc_spec = pl.BlockSpec((TQ, A_HW), lambda h, i: (jnp.minimum(i, nq - 1), col(SEG_V, h)))
    z_spec = pl.BlockSpec((TQ, A_HW), lambda h, i: (jnp.minimum(i, nq - 1), col(SEG_Z, h)))
    u_spec = pl.BlockSpec((A_HPS, 1, 2 * TQ), lambda h, i: (h, 0, 0))
    return q_spec, kp_spec, kc_spec, vp_spec, vc_spec, z_spec, u_spec


def _band_fwd(h, u):
    S = h.shape[0]
    nq = S // TQ

    def body(q_ref, kp_ref, kc_ref, vp_ref, vc_ref, z_ref, u_ref, y_ref, yt_ref, bias_scr):
        i = pl.program_id(1)

        @pl.when(i <= 1)
        def _():
            for hh in range(A_HPS):
                bias_scr[hh] = _band_bias(u_ref[hh], (i == 0).astype(F32))

        for hh in range(A_HPS):
            cs = slice(hh * LANES, (hh + 1) * LANES)
            kcat = jnp.concatenate([kp_ref[:, cs], kc_ref[:, cs]], axis=0)
            vcat = jnp.concatenate([vp_ref[:, cs], vc_ref[:, cs]], axis=0)
            probs = _band_probs(q_ref[:, cs], kcat, bias_scr[hh])
            o = jnp.concatenate([_dot(p.astype(BF16), vcat[cols]) * inv
                                 for (p, inv), (_, cols) in zip(probs, HALVES)], axis=0)
            z = z_ref[:, cs].astype(F32)
            y = o * (z * _sigmoid(z))
            y_ref[:, cs] = y.astype(BF16)
            yt_ref[cs, :] = y.T.astype(BF16)

    specs = _band_specs(nq)
    return pl.pallas_call(
        body,
        name="band_fwd",
        grid=(A_HEADS // A_HPS, nq),
        in_specs=[specs[0], specs[1], specs[2], specs[3], specs[4], specs[5], specs[6]],
        out_specs=[pl.BlockSpec((TQ, A_HW), lambda h, i: (i, h)), pl.BlockSpec((A_HW, TQ), lambda h, i: (h, i))],
        out_shape=[jax.ShapeDtypeStruct((S, D_MODEL), BF16), jax.ShapeDtypeStruct((D_MODEL, S), BF16)],
        scratch_shapes=[pltpu.VMEM((A_HPS, TQ, 2 * TQ), F32)],
        compiler_params=_params(("parallel", "arbitrary")),
    )(h, h, h, h, h, h, u)


def _band_bwd(h, u, dycat):
    S = h.shape[0]
    nq = S // TQ
    scale = HEAD_DIM ** -0.5
    qs, ks, vs, zs = (slice(s * A_HW, (s + 1) * A_HW) for s in (SEG_Q, SEG_K, SEG_V, SEG_Z))

    def body(q_ref, kp_ref, kc_ref, vp_ref, vc_ref, z_ref, u_ref, dy_ref,
             dh_ref, du_ref, bias_scr, db_scr, ckt_scr, cvt_scr, cq_scr, cz_scr):
        i = pl.program_id(1)

        @pl.when(i <= 1)
        def _():
            for hh in range(A_HPS):
                bias_scr[hh] = _band_bias(u_ref[hh], (i == 0).astype(F32))

        @pl.when(i == 0)
        def _():
            db_scr[...] = jnp.zeros_like(db_scr)
            ckt_scr[...] = jnp.zeros_like(ckt_scr)
            cvt_scr[...] = jnp.zeros_like(cvt_scr)
            cq_scr[...] = jnp.zeros_like(cq_scr)
            cz_scr[...] = jnp.zeros_like(cz_scr)

        @pl.when(i < nq)
        def _():
            dh_ref[:, qs] = cq_scr[...]
            dh_ref[:, zs] = cz_scr[...]
            for hh in range(A_HPS):
                cs = slice(hh * LANES, (hh + 1) * LANES)
                q = q_ref[:, cs]
                kcat = jnp.concatenate([kp_ref[:, cs], kc_ref[:, cs]], axis=0)
                vcat = jnp.concatenate([vp_ref[:, cs], vc_ref[:, cs]], axis=0)
                probs = [p * inv for p, inv in _band_probs(q, kcat, bias_scr[hh])]
                o = jnp.concatenate([_dot(p.astype(BF16), vcat[cols]) for p, (_, cols) in zip(probs, HALVES)], axis=0)
                z = z_ref[:, cs].astype(F32)
                sg = _sigmoid(z)
                dy = dy_ref[:, cs].astype(F32)
                do = dy * (z * sg)
                cz_scr[:, cs] = (dy * o * (sg * (1.0 + z * (1.0 - sg)))).astype(BF16)
                dob = do.astype(BF16)
                delta = jnp.sum(do * o, axis=1, keepdims=True)
                qt, dot_ = q.T, dob.T
                dq, dkt, dvt = [], [], []
                for p, (rows, cols) in zip(probs, HALVES):
                    ds = p * (_dot_nt(dob[rows], vcat[cols]) - delta[rows])
                    db_scr[hh, rows, cols] += ds
                    dsb = ds.astype(BF16)
                    dq.append(scale * _dot(dsb, kcat[cols]))
                    dkt.append(scale * _dot(qt[:, rows], dsb))
                    dvt.append(_dot(dot_[:, rows], p.astype(BF16)))
                cq_scr[:, cs] = jnp.concatenate(dq, axis=0).astype(BF16)

                def over_keys(parts):
                    lo, hi = parts
                    prev = jnp.concatenate([lo[:, :HQ], lo[:, HQ:2 * HQ] + hi[:, :HQ]], axis=1)
                    cur = jnp.concatenate([lo[:, 2 * HQ:] + hi[:, HQ:2 * HQ], hi[:, 2 * HQ:]], axis=1)
                    return prev, cur

                (dk_prev, dk_cur), (dv_prev, dv_cur) = over_keys(dkt), over_keys(dvt)
                dh_ref[:, SEG_K * A_HW + hh * LANES:SEG_K * A_HW + (hh + 1) * LANES] = (
                    ckt_scr[cs, :] + dk_prev).T.astype(BF16)
                dh_ref[:, SEG_V * A_HW + hh * LANES:SEG_V * A_HW + (hh + 1) * LANES] = (
                    cvt_scr[cs, :] + dv_prev).T.astype(BF16)
                ckt_scr[cs, :] = dk_cur
                cvt_scr[cs, :] = dv_cur

        @pl.when(i == nq)
        def _():
            dh_ref[:, qs] = cq_scr[...]
            dh_ref[:, zs] = cz_scr[...]
            dh_ref[:, ks] = ckt_scr[...].T.astype(BF16)
            dh_ref[:, vs] = cvt_scr[...].T.astype(BF16)
            r0 = lax.broadcasted_iota(jnp.int32, (TQ, TQ), 0)
            r1 = lax.broadcasted_iota(jnp.int32, (TQ, TQ), 1)
            flip = (r0 + r1 == TQ - 1).astype(BF16)
            for hh in range(A_HPS):
                fl = _dot3(flip, db_scr[hh])
                rolled = pltpu.roll(fl, 0, 1, stride=1, stride_axis=0)
                du_ref[hh] = jnp.sum(rolled, axis=0, keepdims=True)

    specs = _band_specs(nq)
    dy_spec = pl.BlockSpec((TQ, A_HW), lambda h, i: (jnp.minimum(i, nq - 1), h))
    return pl.pallas_call(
        body,
        name="band_bwd",
        grid=(A_HEADS // A_HPS, nq + 1),
        in_specs=[specs[0], specs[1], specs[2], specs[3], specs[4], specs[5], specs[6], dy_spec],
        out_specs=[pl.BlockSpec((TQ, 4 * A_HW), lambda h, i: (jnp.maximum(i - 1, 0), C_A // (4 * A_HW) + h)),
                   pl.BlockSpec((A_HPS, 1, 2 * TQ), lambda h, i: (h, 0, 0))],
        out_shape=[jax.ShapeDtypeStruct((S, HP), BF16), jax.ShapeDtypeStruct((A_HEADS, 1, 2 * TQ), F32)],
        scratch_shapes=[pltpu.VMEM((A_HPS, TQ, 2 * TQ), F32), pltpu.VMEM((A_HPS, TQ, 2 * TQ), F32),
                        pltpu.VMEM((A_HW, TQ), F32), pltpu.VMEM((A_HW, TQ), F32),
                        pltpu.VMEM((TQ, A_HW), BF16), pltpu.VMEM((TQ, A_HW), BF16)],
        compiler_params=_params(("parallel", "arbitrary")),
    )(h, h, h, h, h, h, u, dycat)


def _bias_by_offset(table):
    far = jnp.broadcast_to(table[:, N_REL - 1:], (A_HEADS, TQ - MAX_REL))
    ramp = jnp.flip(table, axis=1)
    rest = jnp.broadcast_to(table[:, :1], (A_HEADS, 2 * TQ - CHUNK - (TQ + MAX_REL + 1)))
    wrap = jnp.broadcast_to(table[:, N_REL - 1:], (A_HEADS, CHUNK))
    return jnp.concatenate([far, ramp, rest, wrap], axis=1)[:, None, :]


def _bias_grad_from_offset(du):
    g = jnp.roll(du[:, 0, :], -(TQ - 1), axis=1)
    far = jnp.sum(g[:, :TQ - MAX_REL], axis=1) + jnp.sum(g[:, 2 * TQ - CHUNK:], axis=1)
    ramp = jnp.flip(g[:, TQ - MAX_REL:TQ + MAX_REL + 1], axis=1)
    return ramp.at[:, N_REL - 1].add(far)


def _mem_probs(q, mk):
    s = _dot_nt(q, mk) * (HEAD_DIM ** -0.5)
    p = jnp.exp(s - jnp.max(s, axis=1, keepdims=True))
    return p * (1.0 / jnp.sum(p, axis=1, keepdims=True))


def _mem_cols(hd, seg):
    return slice((2 * hd + seg) * LANES, (2 * hd + seg + 1) * LANES)


def _mem_fwd(h, mkv, ycat, ycat_t):
    S = h.shape[0]
    nm = mkv.shape[0]
    c0 = (A_WIDTH + B_WIDTH) // LANES

    def body(qz_ref, mkv_ref, yin_ref, ytin_ref, y_ref, yt_ref):
        del yin_ref, ytin_ref
        for hd in range(M_HEADS):
            lane = slice(hd * LANES, (hd + 1) * LANES)
            p = _mem_probs(qz_ref[:, _mem_cols(hd, 0)], mkv_ref[:, lane])
            o = _dot(p.astype(BF16), mkv_ref[:, M_WIDTH + hd * LANES:M_WIDTH + (hd + 1) * LANES])
            z = qz_ref[:, _mem_cols(hd, 1)].astype(F32)
            y = o * (z * _sigmoid(z))
            y_ref[:, lane] = y.astype(BF16)
            yt_ref[lane, :] = y.T.astype(BF16)

    return pl.pallas_call(
        body,
        name="mem_fwd",
        grid=(S // TQ,),
        in_specs=[pl.BlockSpec((TQ, 2 * M_WIDTH), lambda i: (i, C_M // (2 * M_WIDTH))),
                  pl.BlockSpec((nm, 2 * M_WIDTH), lambda i: (0, 0)), ANY, ANY],
        out_specs=[pl.BlockSpec((TQ, M_WIDTH), lambda i: (i, c0 // M_HEADS)),
                   pl.BlockSpec((M_WIDTH, TQ), lambda i: (c0 // M_HEADS, i))],
        out_shape=[jax.ShapeDtypeStruct(ycat.shape, BF16), jax.ShapeDtypeStruct(ycat_t.shape, BF16)],
        input_output_aliases={2: 0, 3: 1},
        compiler_params=_params(("parallel",)),
    )(h, mkv, ycat, ycat_t)


def _mem_bwd(h, mkv, dycat, dh):
    S = h.shape[0]
    nm = mkv.shape[0]
    scale = HEAD_DIM ** -0.5

    def body(qz_ref, mkv_ref, dy_ref, dhin_ref, dh_ref, dmkv_ref):
        del dhin_ref
        i = pl.program_id(0)
        for hd in range(M_HEADS):
            lane = slice(hd * LANES, (hd + 1) * LANES)
            lane_v = slice(M_WIDTH + hd * LANES, M_WIDTH + (hd + 1) * LANES)
            q, mk, mv = qz_ref[:, _mem_cols(hd, 0)], mkv_ref[:, lane], mkv_ref[:, lane_v]
            p = _mem_probs(q, mk)
            pb = p.astype(BF16)
            o = _dot(pb, mv)
            z = qz_ref[:, _mem_cols(hd, 1)].astype(F32)
            sg = _sigmoid(z)
            dy = dy_ref[:, lane].astype(F32)
            do = dy * (z * sg)
            dh_ref[:, _mem_cols(hd, 1)] = (dy * o * (sg * (1.0 + z * (1.0 - sg)))).astype(BF16)
            dob = do.astype(BF16)
            ds = p * (_dot_nt(dob, mv) - jnp.sum(do * o, axis=1, keepdims=True))
            dsb = ds.astype(BF16)
            dh_ref[:, _mem_cols(hd, 0)] = (scale * _dot(dsb, mk)).astype(BF16)
            dmk = scale * _dot_tn(dsb, q)
            dmv = _dot_tn(pb, dob)

            @pl.when(i == 0)
            def _():
                dmkv_ref[:, lane] = dmk
                dmkv_ref[:, lane_v] = dmv

            @pl.when(i > 0)
            def _():
                dmkv_ref[:, lane] += dmk
                dmkv_ref[:, lane_v] += dmv

    return pl.pallas_call(
        body,
        name="mem_bwd",
        grid=(S // TQ,),
        in_specs=[pl.BlockSpec((TQ, 2 * M_WIDTH), lambda i: (i, C_M // (2 * M_WIDTH))),
                  pl.BlockSpec((nm, 2 * M_WIDTH), lambda i: (0, 0)),
                  pl.BlockSpec((TQ, M_WIDTH), lambda i: (i, (A_WIDTH + B_WIDTH) // M_WIDTH)), ANY],
        out_specs=[pl.BlockSpec((TQ, 2 * M_WIDTH), lambda i: (i, C_M // (2 * M_WIDTH))),
                   pl.BlockSpec((nm, 2 * M_WIDTH), lambda i: (0, 0))],
        out_shape=[jax.ShapeDtypeStruct(dh.shape, BF16), jax.ShapeDtypeStruct((nm, 2 * M_WIDTH), F32)],
        input_output_aliases={3: 0},
        compiler_params=_params(("arbitrary",)),
    )(h, mkv, dycat, dh)


SUB = 2 * CHUNK
SUBS = [slice(s * SUB, (s + 1) * SUB) for s in range(TQ // SUB)]


def _chunk_masks():
    r = lax.broadcasted_iota(jnp.int32, (SUB, SUB), 0)
    c = lax.broadcasted_iota(jnp.int32, (SUB, SUB), 1)
    same = _chunk_of(r) == _chunk_of(c)
    return jnp.logical_and(same, c <= r), jnp.logical_and(same, c > r), jnp.logical_and(same, c >= r)


def _by_sub(fn):
    return jnp.concatenate([fn(rows) for rows in SUBS], axis=0)


def _gla_gates(lr, gw, gb):
    logit = _dot(lr, gw) + gb
    sg = _sigmoid(logit)
    g = (jnp.minimum(logit, 0.0) - jnp.log(1.0 + jnp.exp(-jnp.abs(logit)))) * (1.0 / GATE_TAU)
    lo = _chunk_masks()[0].astype(BF16)
    return sg, _by_sub(lambda rows: _dot3(lo, g[rows]))


def _gla_factors(q, k, b):
    eb = jnp.exp(b)
    enb = jnp.exp(-b)
    return eb, enb, q * eb, q * enb, k * eb, k * enb


def _gla_intra(qp, qn, kp, kn):
    lo, up, _ = _chunk_masks()
    qp, qn, kp, kn = qp.astype(BF16), qn.astype(BF16), kp.astype(BF16), kn.astype(BF16)
    return [jnp.where(lo, _dot_nt(qp[rows], kn[rows]), 0.0) + jnp.where(up, _dot_nt(qn[rows], kp[rows]), 0.0)
            for rows in SUBS]


B_HPS = 2
B_HW = B_HPS * LANES


def _gla_specs(nb, rev):
    blk = (lambda i: nb - 1 - i) if rev else (lambda i: i)
    qkvz_spec = pl.BlockSpec((TQ, 4 * B_HW), lambda i, p: (blk(i), C_B // (4 * B_HW) + p))
    lr_spec = pl.BlockSpec((TQ, LANES), lambda i, p: (blk(i), C_LR // LANES))
    gw_spec = pl.BlockSpec((LANES, B_HW), lambda i, p: (0, p))
    gb_spec = pl.BlockSpec((1, B_HW), lambda i, p: (0, p))
    gn_spec = pl.BlockSpec((1, LANES), lambda i, p: (0, 0))
    return qkvz_spec, lr_spec, gw_spec, gb_spec, gn_spec, blk


def _head_cols(hh, seg):
    return slice((4 * hh + seg) * LANES, (4 * hh + seg + 1) * LANES)


def _gla_fwd(h, gw, gb, gn, ycat, ycat_t):
    S = h.shape[0]
    nb = S // TQ
    c0 = A_WIDTH // LANES

    def body(qkvz_ref, lr_ref, gw_ref, gb_ref, gn_ref, yin_ref, ytin_ref,
             y_ref, yt_ref, o_ref, st_ref, st_scr):
        del yin_ref, ytin_ref
        i, p = pl.program_id(0), pl.program_id(1)
        for hh in range(B_HPS):
            hd = B_HPS * p + hh
            lane = slice(hh * LANES, (hh + 1) * LANES)

            @pl.when(i == 0)
            def _():
                st_scr[hd] = jnp.zeros((LANES, LANES), F32)

            q = qkvz_ref[:, _head_cols(hh, SEG_Q)].astype(F32) * (B_DK ** -0.5)
            k = qkvz_ref[:, _head_cols(hh, SEG_K)].astype(F32)
            v = qkvz_ref[:, _head_cols(hh, SEG_V)]
            _, b = _gla_gates(lr_ref[...], gw_ref[:, lane], gb_ref[:, lane])
            _, _, qp, qn, kp, kn = _gla_factors(q, k, b)
            o_intra = jnp.concatenate([_dot(a.astype(BF16), v[rows])
                                       for a, rows in zip(_gla_intra(qp, qn, kp, kn), SUBS)], axis=0)
            qpb, knb = qp.astype(BF16), kn.astype(BF16)
            st = st_scr[hd]
            outs = []
            for c in range(CPB):
                rows = slice(c * CHUNK, (c + 1) * CHUNK)
                st_ref[hh, c] = st
                outs.append(_dot_nt(qpb[rows], st.astype(BF16)))
                e_last = jnp.exp(b[(c + 1) * CHUNK - 1:(c + 1) * CHUNK, :])
                st = (st + _dot_tn(v[rows], knb[rows])) * e_last
            st_scr[hd] = st
            o = o_intra + jnp.concatenate(outs, axis=0)
            o_ref[:, lane] = o
            r = lax.rsqrt(jnp.mean(o * o, axis=1, keepdims=True) + RMS_EPS)
            z = qkvz_ref[:, _head_cols(hh, SEG_Z)].astype(F32)
            y = o * r * gn_ref[...] * (z * _sigmoid(z))
            y_ref[:, lane] = y.astype(BF16)
            yt_ref[lane, :] = y.T.astype(BF16)

    qkvz_s, lr_s, gw_s, gb_s, gn_s, _ = _gla_specs(nb, False)
    return pl.pallas_call(
        body,
        name="gla_fwd",
        grid=(nb, B_HEADS // B_HPS),
        in_specs=[qkvz_s, lr_s, gw_s, gb_s, gn_s, ANY, ANY],
        out_specs=[pl.BlockSpec((TQ, B_HW), lambda i, p: (i, c0 // B_HPS + p)),
                   pl.BlockSpec((B_HW, TQ), lambda i, p: (c0 // B_HPS + p, i)),
                   pl.BlockSpec((TQ, B_HW), lambda i, p: (i, p)),
                   pl.BlockSpec((B_HPS, CPB, LANES, LANES), lambda i, p: (p, i, 0, 0))],
        out_shape=[jax.ShapeDtypeStruct(ycat.shape, BF16), jax.ShapeDtypeStruct(ycat_t.shape, BF16),
                   jax.ShapeDtypeStruct((S, B_WIDTH), F32),
                   jax.ShapeDtypeStruct((B_HEADS, S // CHUNK, LANES, LANES), F32)],
        input_output_aliases={5: 0, 6: 1},
        scratch_shapes=[pltpu.VMEM((B_HEADS, LANES, LANES), F32)],
        compiler_params=_params(("arbitrary", "arbitrary")),
    )(h, h, gw, gb, gn, ycat, ycat_t)


def _gla_bwd(h, gw, gb, gn, o_pre, states, dycat, dh):
    S = h.shape[0]
    nb = S // TQ
    n_steps = B_HEADS // B_HPS

    def body(qkvz_ref, lr_ref, gw_ref, gb_ref, gn_ref, o_ref, st_ref, dy_ref, dhin_ref,
             dh_ref, dgw_ref, dgb_ref, dgn_ref,
             dst_scr, dgw_scr, dgb_scr, dgn_scr, dlr_scr):
        del dhin_ref
        i, p = pl.program_id(0), pl.program_id(1)

        @pl.when(jnp.logical_and(i == 0, p == 0))
        def _():
            dgn_scr[...] = jnp.zeros_like(dgn_scr)

        dlr_heads = [one_head(hh, i, p, qkvz_ref, lr_ref, gw_ref, gb_ref, gn_ref, o_ref, st_ref, dy_ref,
                              dh_ref, dgw_ref, dgb_ref, dst_scr, dgw_scr, dgb_scr, dgn_scr) for hh in range(B_HPS)]
        dlr = dlr_heads[0]
        for more in dlr_heads[1:]:
            dlr = dlr + more

        @pl.when(p == 0)
        def _():
            dlr_scr[...] = dlr

        @pl.when(p > 0)
        def _():
            dlr_scr[...] += dlr

        @pl.when(p == n_steps - 1)
        def _():
            cols = _head_cols(B_HPS - 1, SEG_Q)
            dh_ref[:, cols] = (dh_ref[:, cols].astype(F32) + dlr_scr[...]).astype(BF16)

        @pl.when(i == nb - 1)
        def _():
            dgn_ref[...] = dgn_scr[...]

    def one_head(hh, i, p, qkvz_ref, lr_ref, gw_ref, gb_ref, gn_ref, o_ref, st_ref, dy_ref,
                 dh_ref, dgw_ref, dgb_ref, dst_scr, dgw_scr, dgb_scr, dgn_scr):
        hd = B_HPS * p + hh
        lane = slice(hh * LANES, (hh + 1) * LANES)

        @pl.when(i == 0)
        def _():
            dst_scr[hd] = jnp.zeros((LANES, LANES), F32)
            dgw_scr[hd] = jnp.zeros((LANES, LANES), F32)
            dgb_scr[hd] = jnp.zeros((1, LANES), F32)

        q = qkvz_ref[:, _head_cols(hh, SEG_Q)].astype(F32) * (B_DK ** -0.5)
        k = qkvz_ref[:, _head_cols(hh, SEG_K)].astype(F32)
        v = qkvz_ref[:, _head_cols(hh, SEG_V)]
        lr, gwv = lr_ref[...], gw_ref[:, lane]
        sg, b = _gla_gates(lr, gwv, gb_ref[:, lane])
        eb, enb, qp, qn, kp, kn = _gla_factors(q, k, b)
        a = _gla_intra(qp, qn, kp, kn)
        qpb, qnb, kpb, knb = qp.astype(BF16), qn.astype(BF16), kp.astype(BF16), kn.astype(BF16)

        o = o_ref[:, lane]
        gn = gn_ref[...]
        r = lax.rsqrt(jnp.mean(o * o, axis=1, keepdims=True) + RMS_EPS)
        z = qkvz_ref[:, _head_cols(hh, SEG_Z)].astype(F32)
        sz = _sigmoid(z)
        dy = dy_ref[:, lane].astype(F32)
        d_on = dy * (z * sz)
        dh_ref[:, _head_cols(hh, SEG_Z)] = (dy * (o * r * gn) * (sz * (1.0 + z * (1.0 - sz)))).astype(BF16)
        dgn_scr[...] += jnp.sum(d_on * o * r, axis=0, keepdims=True)
        t = d_on * gn
        do = r * t - o * (r * r * r) * jnp.mean(t * o, axis=1, keepdims=True)
        dob = do.astype(BF16)

        lo, up, upper = _chunk_masks()
        dqp, dkn, dqn, dkp, dv = [], [], [], [], []
        for a_s, rows in zip(a, SUBS):
            da = _dot_nt(dob[rows], v[rows])
            dalo = jnp.where(lo, da, 0.0).astype(BF16)
            daup = jnp.where(up, da, 0.0).astype(BF16)
            dqp.append(_dot(dalo, knb[rows]))
            dkn.append(_dot_tn(dalo, qpb[rows]))
            dqn.append(_dot(daup, kpb[rows]))
            dkp.append(_dot_tn(daup, qnb[rows]))
            dv.append(_dot_tn(a_s.astype(BF16), dob[rows]))
        dqp, dkn, dqn, dkp, dv = (jnp.concatenate(x, axis=0) for x in (dqp, dkn, dqn, dkp, dv))

        dst = dst_scr[hd]
        dqp_c, dkn_c, dv_c, dbl_c = [None] * CPB, [None] * CPB, [None] * CPB, [None] * CPB
        for c in reversed(range(CPB)):
            rows = slice(c * CHUNK, (c + 1) * CHUNK)
            st = st_ref[hh, c]
            e_last = jnp.exp(b[(c + 1) * CHUNK - 1:(c + 1) * CHUNK, :])
            if c == CPB - 1:
                st_next = (st + _dot_tn(v[rows], knb[rows])) * e_last
            else:
                st_next = st_ref[hh, c + 1]
            dbl_c[c] = jnp.sum(dst * st_next, axis=0, keepdims=True)
            dtt = (dst * e_last).astype(BF16)
            dv_c[c] = _dot_nt(knb[rows], dtt)
            dkn_c[c] = _dot(v[rows], dtt)
            dqp_c[c] = _dot(dob[rows], st.astype(BF16))
            dst = _dot_tn(dob[rows], qpb[rows]) + dst * e_last
        dst_scr[hd] = dst
        dqp = dqp + jnp.concatenate(dqp_c, axis=0)
        dkn = dkn + jnp.concatenate(dkn_c, axis=0)
        dv = dv + jnp.concatenate(dv_c, axis=0)
        dh_ref[:, _head_cols(hh, SEG_V)] = dv.astype(BF16)
        dh_ref[:, _head_cols(hh, SEG_Q)] = ((dqp * eb + dqn * enb) * (B_DK ** -0.5)).astype(BF16)
        dh_ref[:, _head_cols(hh, SEG_K)] = (dkp * eb + dkn * enb).astype(BF16)

        last = jnp.bitwise_and(lax.broadcasted_iota(jnp.int32, (TQ, 1), 0), CHUNK - 1) == CHUNK - 1
        dbl = jnp.concatenate([jnp.broadcast_to(x, (CHUNK, LANES)) for x in dbl_c], axis=0)
        db = dqp * qp - dqn * qn + dkp * kp - dkn * kn + jnp.where(last, dbl, 0.0)
        upper_b = upper.astype(BF16)
        dlogit = _by_sub(lambda rows: _dot3(upper_b, db[rows])) * (1.0 / GATE_TAU) * (1.0 - sg)
        dlb = dlogit.astype(BF16)
        dgw_scr[hd] += _dot_tn(lr, dlb)
        dgb_scr[hd] += jnp.sum(dlogit, axis=0, keepdims=True)

        @pl.when(i == nb - 1)
        def _():
            dgw_ref[:, lane] = dgw_scr[hd]
            dgb_ref[:, lane] = dgb_scr[hd]

        return _dot_nt(dlb, gwv)

    qkvz_s, lr_s, gw_s, gb_s, gn_s, blk = _gla_specs(nb, True)
    row = pl.BlockSpec((TQ, B_HW), lambda i, p: (blk(i), p))
    dy_spec = pl.BlockSpec((TQ, B_HW), lambda i, p: (blk(i), A_WIDTH // B_HW + p))
    st_spec = pl.BlockSpec((B_HPS, CPB, LANES, LANES), lambda i, p: (p, blk(i), 0, 0))
    return pl.pallas_call(
        body,
        name="gla_bwd",
        grid=(nb, B_HEADS // B_HPS),
        in_specs=[qkvz_s, lr_s, gw_s, gb_s, gn_s, row, st_spec, dy_spec, ANY],
        out_specs=[pl.BlockSpec((TQ, 4 * B_HW), lambda i, p: (blk(i), C_B // (4 * B_HW) + p)),
                   pl.BlockSpec((LANES, B_HW), lambda i, p: (0, jnp.where(i == nb - 1, p, 0))),
                   pl.BlockSpec((1, B_HW), lambda i, p: (0, jnp.where(i == nb - 1, p, 0))),
                   pl.BlockSpec((1, LANES), lambda i, p: (0, 0))],
        out_shape=[jax.ShapeDtypeStruct(dh.shape, BF16),
                   jax.ShapeDtypeStruct((LANES, B_HEADS * LANES), F32),
                   jax.ShapeDtypeStruct((1, B_HEADS * LANES), F32),
                   jax.ShapeDtypeStruct((1, LANES), F32)],
        input_output_aliases={8: 0},
        scratch_shapes=[pltpu.VMEM((B_HEADS, LANES, LANES), F32), pltpu.VMEM((B_HEADS, LANES, LANES), F32),
                        pltpu.VMEM((B_HEADS, 1, LANES), F32), pltpu.VMEM((1, LANES), F32),
                        pltpu.VMEM((TQ, LANES), F32)],
        compiler_params=_params(("arbitrary", "arbitrary")),
    )(h, h, gw, gb, gn, o_pre, states, dycat, dh)


LN_ROWS = 256


def _outproj_ln(ycat, w_out, x, g, b):
    S = x.shape[0]

    def body(yc_ref, w_ref, x_ref, g_ref, b_ref, y_ref, yb_ref, yt_ref, xh_ref, rs_ref):
        u = ALPHA * x_ref[...] + _dot(yc_ref[...], w_ref[...])
        mu = jnp.mean(u, axis=1, keepdims=True)
        d = u - mu
        rstd = lax.rsqrt(jnp.mean(d * d, axis=1, keepdims=True) + LN_EPS)
        xh = d * rstd
        y = xh * g_ref[...] + b_ref[...]
        y_ref[...] = y
        yb_ref[...] = y.astype(BF16)
        yt_ref[...] = y.T.astype(BF16)
        xh_ref[...] = xh
        rs_ref[...] = rstd

    row = lambda w: pl.BlockSpec((LN_ROWS, w), lambda i: (i, 0))
    vec = pl.BlockSpec((1, D_MODEL), lambda i: (0, 0))
    return pl.pallas_call(
        body,
        name="outproj_ln",
        grid=(S // LN_ROWS,),
        in_specs=[row(D_MODEL), pl.BlockSpec((D_MODEL, D_MODEL), lambda i: (0, 0)), row(D_MODEL), vec, vec],
        out_specs=[row(D_MODEL), row(D_MODEL), pl.BlockSpec((D_MODEL, LN_ROWS), lambda i: (0, i)), row(D_MODEL), row(1)],
        out_shape=[jax.ShapeDtypeStruct((S, D_MODEL), F32), jax.ShapeDtypeStruct((S, D_MODEL), BF16),
                   jax.ShapeDtypeStruct((D_MODEL, S), BF16),
                   jax.ShapeDtypeStruct((S, D_MODEL), F32), jax.ShapeDtypeStruct((S, 1), F32)],
        compiler_params=_params(("parallel",)),
    )(ycat, w_out, x, g, b)


def _ln_bwd_dycat(dy, xhat, rstd, g, w_out, rider=None):
    S = dy.shape[0]
    n = S // LN_ROWS
    hop, srcs, bufs = rider if rider is not None else (None, [], [])

    def body(*refs):
        dy_ref, xh_ref, rs_ref, g_ref, w_ref = refs[:5]
        src_refs = refs[5:5 + len(srcs)]
        outs = refs[5 + len(srcs) + len(bufs):]
        du_ref, dub_ref, dyc_ref, dg_ref, db_ref = outs[:5]
        buf_refs, sems = outs[5:5 + len(bufs)], outs[5 + len(bufs):]
        i = pl.program_id(0)
        if hop is not None:
            @pl.when(i == 0)
            def _():
                hop.start(src_refs, buf_refs, *sems)

        dy_, xh = dy_ref[...], xh_ref[...]
        dyg = dy_ * g_ref[...]
        m1 = jnp.mean(dyg, axis=1, keepdims=True)
        m2 = jnp.mean(dyg * xh, axis=1, keepdims=True)
        du = rs_ref[...] * (dyg - m1 - xh * m2)
        dub = du.astype(BF16)
        du_ref[...] = du
        dub_ref[...] = dub
        dyc_ref[...] = _dot_nt(dub, w_ref[...]).astype(BF16)
        dg = jnp.sum(dy_ * xh, axis=0, keepdims=True)
        db = jnp.sum(dy_, axis=0, keepdims=True)

        @pl.when(i == 0)
        def _():
            dg_ref[...] = dg
            db_ref[...] = db

        @pl.when(i > 0)
        def _():
            dg_ref[...] += dg
            db_ref[...] += db

        if hop is not None:
            @pl.when(i == n - 1)
            def _():
                hop.wait(src_refs, buf_refs, *sems)

    row = lambda w: pl.BlockSpec((LN_ROWS, w), lambda i: (i, 0))
    vec = pl.BlockSpec((1, D_MODEL), lambda i: (0, 0))
    n_in = 5 + len(srcs)
    out = pl.pallas_call(
        body,
        name="ln_bwd_dycat",
        grid=(n,),
        in_specs=[row(D_MODEL), row(D_MODEL), row(1), vec, pl.BlockSpec((D_MODEL, D_MODEL), lambda i: (0, 0))]
        + [ANY] * (len(srcs) + len(bufs)),
        out_specs=[row(D_MODEL), row(D_MODEL), row(D_MODEL), vec, vec] + [ANY] * len(bufs),
        out_shape=[jax.ShapeDtypeStruct((S, D_MODEL), F32), jax.ShapeDtypeStruct((S, D_MODEL), BF16),
                   jax.ShapeDtypeStruct((S, D_MODEL), BF16),
                   jax.ShapeDtypeStruct((1, D_MODEL), F32), jax.ShapeDtypeStruct((1, D_MODEL), F32)]
        + [jax.ShapeDtypeStruct(b.shape, b.dtype) for b in bufs],
        input_output_aliases={n_in + t: 5 + t for t in range(len(bufs))},
        scratch_shapes=hop.sems() if hop is not None else [],
        compiler_params=_params(("arbitrary",)),
    )(dy, xhat, rstd, g, w_out, *srcs, *bufs)
    return tuple(out[:5]) if hop is None else (tuple(out[:5]), list(out[5:]))


def _loss_head(y, target):
    S = y.shape[0]

    def body(y_ref, t_ref, l_ref, dy_ref):
        i = pl.program_id(0)
        err = y_ref[...] - t_ref[...]
        dy_ref[...] = err * (1.0 / D_MODEL)
        part = (0.5 / D_MODEL) * jnp.sum(jnp.sum(err * err, axis=1, keepdims=True), axis=0, keepdims=True)

        @pl.when(i == 0)
        def _():
            l_ref[...] = part

        @pl.when(i > 0)
        def _():
            l_ref[...] += part

    row = pl.BlockSpec((TQ, D_MODEL), lambda i: (i, 0))
    return pl.pallas_call(
        body,
        name="loss_head",
        grid=(S // TQ,),
        in_specs=[row, row],
        out_specs=[pl.BlockSpec((1, 1), lambda i: (0, 0)), row],
        out_shape=[jax.ShapeDtypeStruct((1, 1), F32), jax.ShapeDtypeStruct((S, D_MODEL), F32)],
        compiler_params=_params(("arbitrary",)),
    )(y, target)


def _pad_gate(gate_w, gate_b):
    gw = gate_w.reshape(GATE_RANK, B_HEADS, B_DK)
    gw = jnp.pad(gw, ((LR_LANE, LANES - LR_LANE - GATE_RANK), (0, 0), (0, LANES - B_DK))).reshape(LANES, B_HEADS * LANES)
    gb = jnp.pad(gate_b.reshape(B_HEADS, B_DK), ((0, 0), (0, LANES - B_DK))).reshape(1, B_HEADS * LANES)
    return gw.astype(BF16), gb.astype(F32)


def _layer_fwd(x, xb, xt, mem_b, w_in, w_kv, w_out, u, gw, gb, gn, ln_g, ln_b, rider=None):
    h = _matmul(xb, w_in, mode="nn", out_dtype=BF16, tm=1024, tn=1792, tk=D_MODEL, name="in_proj", rider=rider)
    if rider is not None:
        h, rode = h
    mkv = _matmul(mem_b, w_kv, mode="nn", out_dtype=BF16, tm=mem_b.shape[0], tn=1024, tk=D_MODEL, name="mem_kv")
    ycat, ycat_t = _band_fwd(h, u)
    ycat, ycat_t, o_pre, states = _gla_fwd(h, gw, gb, gn, ycat, ycat_t)
    ycat, ycat_t = _mem_fwd(h, mkv, ycat, ycat_t)
    y, ybf, yt, xhat, rstd = _outproj_ln(ycat, w_out, x, ln_g, ln_b)
    saved = (xt, h, mkv, ycat_t, o_pre, states, xhat, rstd)
    return (y, ybf, yt, saved) if rider is None else (y, ybf, yt, saved, rode)


def _layer_bwd(dy, saved, mem_b, w_in, w_out, u, gw, gb, gn, ln_g, reduce=None, own_reduce=None):
    xt, h, mkv, ycat_t, o_pre, states, xhat, rstd = saved

    def riding(**kw):
        if reduce is None:
            return _matmul(**kw)
        out, bufs = _matmul(rider=reduce.rider(), **kw)
        reduce.landed(bufs)
        return out

    if reduce is None:
        du, dub, dycat, d_ln_g, d_ln_b = _ln_bwd_dycat(dy, xhat, rstd, ln_g, w_out)
    else:
        (du, dub, dycat, d_ln_g, d_ln_b), bufs = _ln_bwd_dycat(dy, xhat, rstd, ln_g, w_out, rider=reduce.rider())
        reduce.landed(bufs)
    d_w_out = _matmul(ycat_t, dub, mode="nn", out_dtype=F32, tm=1024, tn=1024, tk=min(4096, dub.shape[0]), name="d_w_out")
    dh, d_u = _band_bwd(h, u, dycat)
    dh, dgw, dgb, dgn = _gla_bwd(h, gw, gb, gn, o_pre, states, dycat, dh)
    dh, dmkv = _mem_bwd(h, mkv, dycat, dh)
    d_w_kv = _matmul(mem_b, dmkv, mode="tn", out_dtype=F32, tm=1024, tn=1024, tk=mem_b.shape[0], name="d_w_kv")
    dx_args = dict(a=dh, b=w_in, mode="nt", out_dtype=F32, tm=1024, tn=1024, tk=3584, name="dx", add=du, add_scale=ALPHA)
    dw_args = dict(a=xt, b=dh, mode="nn", out_dtype=F32, tm=1024, tn=1024, tk=min(4096, dh.shape[0]), name="d_w_in")
    if own_reduce is None:
        dx = riding(**dx_args)
        d_w_in = riding(**dw_args)
        return dx, (d_w_in, d_u, dgw, dgb, dgn, d_w_kv, d_w_out, d_ln_g, d_ln_b)
    d_w_in = riding(**dw_args)
    grads = (d_w_in, d_u, dgw, dgb, dgn, d_w_kv, d_w_out, d_ln_g, d_ln_b)
    own = own_reduce(grads, reduce.finish() if reduce is not None else None)
    own.step()
    dx, bufs = _matmul(rider=own.rider(), **dx_args)
    own.landed(bufs)
    return dx, grads, own


def _unpad_heads(w):
    r = w.shape[0]
    return w.reshape(r, B_HEADS, LANES)[:, :, :B_DK].reshape(r, B_KEY_WIDTH)


def _padded_col_of():
    col, o = np.zeros(IN_WIDTH, np.int64), 0
    for seg in (SEG_Q, SEG_K, SEG_V, SEG_Z):
        for hd in range(A_HEADS):
            col[o:o + LANES] = C_A + (hd // A_HPS) * 4 * A_HW + seg * A_HW + (hd % A_HPS) * LANES + np.arange(LANES)
            o += LANES
    for seg, width in ((SEG_Q, B_DK), (SEG_K, B_DK), (SEG_V, LANES), (SEG_Z, LANES)):
        for hd in range(B_HEADS):
            col[o:o + width] = C_B + hd * 4 * LANES + seg * LANES + np.arange(width)
            o += width
    col[o:o + GATE_RANK] = C_LR + LR_LANE + np.arange(GATE_RANK)
    o += GATE_RANK
    for seg in (0, 1):
        for hd in range(M_HEADS):
            col[o:o + LANES] = C_M + hd * 2 * LANES + seg * LANES + np.arange(LANES)
            o += LANES
    assert o == IN_WIDTH
    return col


def _runs(idx):
    out, start = [], 0
    for k in range(1, len(idx) + 1):
        if k == len(idx) or idx[k] != idx[k - 1] + 1:
            out.append((int(idx[start]), k - start))
            start = k
    return out


def _chip_columns(g, j, n):
    runs = _runs(_padded_col_of()[j * n:(j + 1) * n])
    return jnp.concatenate([g[:, a:a + ln] for a, ln in runs], axis=1)


def _padded_from_shards(shards):
    n = shards[0].shape[1]
    src = np.full(HP, -1, np.int64)
    src[_padded_col_of()] = np.arange(IN_WIDTH)
    parts, k = [], 0
    while k < HP:
        e = k + 1
        if src[k] < 0:
            while e < HP and src[e] < 0:
                e += 1
            parts.append(jnp.zeros((shards[0].shape[0], e - k), shards[0].dtype))
        else:
            while e < HP and src[e] == src[e - 1] + 1 and src[e] // n == src[k] // n:
                e += 1
            parts.append(shards[src[k] // n][:, src[k] % n:src[k] % n + e - k])
        k = e
    return jnp.concatenate(parts, axis=1)


ADAMW_BLOCK_BYTES = 1 << 20


def _adamw(w, g, m, v, name):
    L, R, C = w.shape
    tl, tr = 1, R
    if R * C * 4 <= ADAMW_BLOCK_BYTES:
        tl = max(d for d in range(1, L + 1) if L % d == 0 and d * R * C * 4 <= ADAMW_BLOCK_BYTES)
    else:
        for cand in (256, 128, 64, 32, 16, 8):
            if R % cand == 0 and R > cand:
                tr = cand
                break

    def body(w_ref, g_ref, m_ref, v_ref, d_ref, nm_ref, nv_ref):
        g_ = g_ref[...]
        nm = ADAM_B1 * m_ref[...] + (1.0 - ADAM_B1) * g_
        nv = ADAM_B2 * v_ref[...] + (1.0 - ADAM_B2) * (g_ * g_)
        m_hat = nm / (1.0 - ADAM_B1 ** ADAM_STEP)
        v_hat = nv / (1.0 - ADAM_B2 ** ADAM_STEP)
        d_ref[...] = -ADAM_LR * (m_hat / (jnp.sqrt(v_hat) + ADAM_EPS) + ADAM_WD * w_ref[...])
        nm_ref[...] = nm
        nv_ref[...] = nv

    spec = pl.BlockSpec((tl, tr, C), lambda l, i: (l, i, 0))
    sd = jax.ShapeDtypeStruct((L, R, C), F32)
    return pl.pallas_call(
        body, name=name, grid=(L // tl, R // tr), in_specs=[spec] * 4, out_specs=[spec] * 3, out_shape=[sd] * 3,
        compiler_params=_params(("parallel", "parallel")),
    )(w, g, m, v)


def _adamw_nd(w, g, m, v, name):
    shape = w.shape
    f = (lambda a: a) if w.ndim == 3 else (lambda a: a.reshape(1, shape[0], shape[1]))
    return tuple(o.reshape(shape) for o in _adamw(f(w), f(g), f(m), f(v), name))


ANY = pl.BlockSpec(memory_space=pl.ANY)


def _place():
    x, y, c = lax.axis_index("x"), lax.axis_index("y"), lax.axis_index("c")
    chips = [(1 - x, y), (x, 1 - y), (1 - x, 1 - y)]
    return x, y, c, chips


class _WeightGather:
    def __init__(self, hop, layer, rows):
        self.hop, self.layer, self.rows = hop, layer, rows
        self.n_sem = 3 * len(rows)

    def _copies(self, shard_refs, buf_refs, send, recv, received):
        x, y, c, chips = _place()
        out = []
        for t, R in enumerate(self.rows):
            half = R // 2
            assert half % 16 == 0
            mine = pl.ds(pl.multiple_of(c * half, 16), half)
            other = pl.ds(pl.multiple_of((1 - c) * half, 16), half)
            mine_of_shard = pl.ds(pl.multiple_of(self.layer * R + c * half, 16), half)
            for k, chip in enumerate(chips):
                theirs = buf_refs[t].at[2 * chip[0] + chip[1]]
                if self.hop == "chips":
                    src, dst, to = shard_refs[t].at[mine_of_shard], buf_refs[t].at[2 * x + y, mine], (*chip, c)
                    landed = theirs.at[mine]
                else:
                    src, dst, to = theirs.at[mine], theirs.at[mine], (x, y, 1 - c)
                    landed = theirs.at[other]
                out.append(pltpu.make_async_remote_copy(
                    src_ref=src, dst_ref=landed if received else dst, send_sem=send.at[3 * t + k],
                    recv_sem=recv.at[3 * t + k], device_id=to, device_id_type=MESH))
        return out

    def start(self, shard_refs, buf_refs, send, recv):
        for cp in self._copies(shard_refs, buf_refs, send, recv, False):
            cp.start()

    def wait(self, shard_refs, buf_refs, send, recv):
        for cp in self._copies(shard_refs, buf_refs, send, recv, True):
            cp.wait_recv()
        for cp in self._copies(shard_refs, buf_refs, send, recv, False):
            cp.wait_send()

    def sems(self):
        return [pltpu.SemaphoreType.DMA((self.n_sem,)), pltpu.SemaphoreType.DMA((self.n_sem,))]

    def call(self, srcs, bufs, name):
        ns, nb = len(srcs), len(bufs)

        def body(*refs):
            src_refs, buf_refs, (send, recv) = refs[:ns], refs[ns + nb:ns + 2 * nb], refs[ns + 2 * nb:]
            self.start(src_refs, buf_refs, send, recv)
            self.wait(src_refs, buf_refs, send, recv)

        return pl.pallas_call(
            body, name=name, in_specs=[ANY] * (ns + nb), out_specs=[ANY] * nb,
            out_shape=[jax.ShapeDtypeStruct(b.shape, b.dtype) for b in bufs],
            input_output_aliases={ns + t: t for t in range(nb)},
            scratch_shapes=self.sems(),
        )(*srcs, *bufs)


class _GradHop(_WeightGather):
    def __init__(self, hop, layer, slices):
        self.hop, self.layer, self.slices = hop, layer, slices
        self.n_sem = {"pair": N_CHIPS, "chips": N_CHIPS - 1, "gather": 1}[hop] * len(slices)

    def _copies(self, src_refs, buf_refs, send, recv, received):
        x, y, c, chips = _place()
        me, out = 2 * x + y, []

        def remote(src, dst, to):
            k = len(out)
            out.append(pltpu.make_async_remote_copy(src_ref=src, dst_ref=dst, send_sem=send.at[k], recv_sem=recv.at[k],
                                                    device_id=to, device_id_type=MESH))

        for t, (half, where) in enumerate(self.slices):
            if self.hop == "pair":
                for j, (a, first) in enumerate(where):
                    rows = pl.ds(pl.multiple_of(first + (1 - c) * half, 8), half)
                    remote(src_refs[a].at[rows], buf_refs[t].at[j], (x, y, 1 - c))
            elif self.hop == "chips":
                for chip in chips:
                    slot = 2 * chip[0] + chip[1]
                    remote(src_refs[t].at[slot], buf_refs[t].at[slot if received else me], (*chip, c))
            else:
                mine = buf_refs[t].at[self.layer, c]
                remote(mine, buf_refs[t].at[self.layer, 1 - c] if received else mine, (x, y, 1 - c))
        return out


def _add_halves(parts, got, c_idx, name):
    n, L, half, C = got.shape
    tr = 64

    def body(*refs):
        ins, (got_ref, o_ref) = refs[1:1 + len(parts)], refs[1 + len(parts):]
        for k in range(len(parts)):
            o_ref[k // L, k % L] = (ins[k][...] + got_ref[k // L, k % L]).astype(BF16)

    def rows_of(first):
        assert first % tr == 0 and half % tr == 0
        return lambda i, c: (first // tr + c[0] * (half // tr) + i, 0)

    whole = pl.BlockSpec((n, L, tr, C), lambda i, c: (0, 0, i, 0))
    return pl.pallas_call(
        body, name=name,
        grid_spec=pltpu.PrefetchScalarGridSpec(
            num_scalar_prefetch=1, grid=(half // tr,),
            in_specs=[pl.BlockSpec((tr, C), rows_of(first)) for _, first in parts] + [whole],
            out_specs=whole),
        out_shape=jax.ShapeDtypeStruct((n, L, half, C), BF16),
        compiler_params=_params(("parallel",)),
    )(c_idx, *[a for a, _ in parts], got)


def _add_slots(r, c_idx, dest, layer, name):
    n, half, C = r.shape
    tr = 256

    def body(c_ref, r_ref, dest_ref, o_ref):
        del dest_ref
        acc = r_ref[0].astype(F32)
        for j in range(1, n):
            acc = acc + r_ref[j].astype(F32)
        o_ref[0, 0] = acc

    return pl.pallas_call(
        body, name=name,
        grid_spec=pltpu.PrefetchScalarGridSpec(
            num_scalar_prefetch=1, grid=(half // tr,),
            in_specs=[pl.BlockSpec((n, tr, C), lambda i, c: (0, i, 0)), ANY],
            out_specs=pl.BlockSpec((1, 1, tr, C), lambda i, c: (layer, c[0], i, 0))),
        out_shape=jax.ShapeDtypeStruct(dest.shape, F32),
        input_output_aliases={2: 0},
        compiler_params=_params(("parallel",)),
    )(c_idx, r, dest)


class _LayerReduce:
    def __init__(self, layer, grads, slices, c_idx, chip, dests):
        self.layer, self.grads, self.slices, self.c_idx, self.chip, self.dests = layer, grads, slices, c_idx, chip, dests
        self.widths = [grads[where[0][0]].shape[1] for _, where in slices]
        self.stage = 0

    def _hop(self, kind):
        return _GradHop(kind, self.layer, self.slices)

    def rider(self):
        if self.stage == 0:
            got = [lax.empty((N_CHIPS, half, w), F32) for (half, _), w in zip(self.slices, self.widths)]
            return self._hop("pair"), self.grads, got
        if self.stage == 1:
            q = [lax.empty(p.shape, BF16) for p in self.pair_sums]
            return self._hop("chips"), self.pair_sums, q
        return self._hop("gather"), [], self.dests

    def landed(self, bufs):
        tag = f"{self.layer}"
        if self.stage == 0:
            self.pair_sums = []
            for t, ((half, where), got) in enumerate(zip(self.slices, bufs)):
                parts = [(self.grads[a], first) for a, first in where]
                p = _add_halves(parts, got[:, None], self.c_idx, f"rs_add2_{t}_{tag}")
                self.pair_sums.append(p.reshape(N_CHIPS, half, p.shape[-1]))
        elif self.stage == 1:
            for t, (q, p) in enumerate(zip(bufs, self.pair_sums)):
                q = lax.dynamic_update_slice_in_dim(q, lax.dynamic_slice_in_dim(p, self.chip, 1, axis=0), self.chip, axis=0)
                self.dests[t] = _add_slots(q, self.c_idx, self.dests[t], self.layer, f"rs_add4_{t}_{tag}")
        else:
            self.dests = list(bufs)
        self.stage += 1

    def step(self):
        hop, srcs, bufs = self.rider()
        self.landed(hop.call(srcs, bufs, f"rs_{hop.hop}_{self.layer}"))

    def finish(self):
        while self.stage < 3:
            self.step()
        return self.dests


def _all_reduce_small(buf, name):
    R = buf.shape[0]

    def flipped(k, x, y, c):
        return ((1 - x) if k & 4 else x, (1 - y) if k & 2 else y, (1 - c) if k & 1 else c)

    def body(b_ref, o_ref, land, send, recv):
        x, y, c, _ = _place()
        me = 4 * x + 2 * y + c
        land[me] = b_ref[...]
        cps = []
        for k in range(1, N_DEV):
            peer = flipped(k, x, y, c)
            cps.append(pltpu.make_async_remote_copy(src_ref=b_ref, dst_ref=land.at[me], send_sem=send.at[k - 1],
                                                    recv_sem=recv.at[k - 1], device_id=peer, device_id_type=MESH))
        for cp in cps:
            cp.start()
        for k in range(1, N_DEV):
            peer = flipped(k, x, y, c)
            slot = 4 * peer[0] + 2 * peer[1] + peer[2]
            pltpu.make_async_remote_copy(src_ref=b_ref, dst_ref=land.at[slot], send_sem=send.at[k - 1],
                                         recv_sem=recv.at[k - 1], device_id=peer, device_id_type=MESH).wait_recv()
        for cp in cps:
            cp.wait_send()
        acc = land[0]
        for j in range(1, N_DEV):
            acc = acc + land[j]
        o_ref[...] = acc

    vm = pl.BlockSpec(memory_space=pltpu.VMEM)
    return pl.pallas_call(
        body, name=name, in_specs=[vm], out_specs=vm,
        out_shape=jax.ShapeDtypeStruct((R, LANES), F32),
        scratch_shapes=[pltpu.VMEM((N_DEV, R, LANES), F32), pltpu.SemaphoreType.DMA((N_DEV - 1,)),
                        pltpu.SemaphoreType.DMA((N_DEV - 1,))],
    )(buf)


def kernel(x, mem, w_in, a_rel_bias, b_gate_w, b_gate_b, b_norm_g, w_mem_kv, w_out, ln_g, ln_b, loss_target, m_w_in, m_a_rel_bias, m_b_gate_w, m_b_gate_b, m_b_norm_g, m_w_mem_kv, m_w_out, m_ln_g, m_ln_b, v_w_in, v_a_rel_bias, v_b_gate_w, v_b_gate_b, v_b_norm_g, v_w_mem_kv, v_w_out, v_ln_g, v_ln_b):
    L = w_in.shape[0]
    S = x.shape[1]
    cx, cy, cc = lax.axis_index("x"), lax.axis_index("y"), lax.axis_index("c")
    chip = 2 * cx + cy
    c_idx = jnp.reshape(cc, (1,)).astype(jnp.int32)

    n_in, r_kv, r_out = w_in.shape[2], w_mem_kv.shape[1], w_out.shape[1]
    shards = [w.astype(BF16).reshape(-1, w.shape[2]) for w in (w_in, w_mem_kv, w_out)]
    rows = [D_MODEL, r_kv, r_out]

    def landing(l):
        return [lax.dynamic_update_slice_in_dim(lax.empty((N_CHIPS, r, s.shape[1]), BF16),
                                                s[l * r:(l + 1) * r][None], chip, axis=0)
                for s, r in zip(shards, rows)]

    def assembled(bufs):
        return (_padded_from_shards([bufs[0][j] for j in range(N_CHIPS)]),
                bufs[1].reshape(D_MODEL, bufs[1].shape[2]), bufs[2].reshape(D_MODEL, D_MODEL))

    bufs0 = _WeightGather("chips", 0, rows).call(shards, landing(0), "gather_chips_0")
    weights = [assembled(_WeightGather("pair", 0, rows).call(shards, bufs0, "gather_pair_0"))]

    gw_cols = b_gate_w.shape[2]
    gw_slot = jnp.zeros((N_CHIPS, L, GATE_RANK, gw_cols), F32)
    gw_slot = lax.dynamic_update_slice(gw_slot, (0.5 * b_gate_w)[None], (chip, 0, 0, 0))
    gw_flat = gw_slot.reshape(-1)
    n_gw = gw_flat.shape[0]
    pad = (-n_gw) % (8 * LANES)
    gw_full = _all_reduce_small(jnp.pad(gw_flat, (0, pad)).reshape(-1, LANES), "gather_gate_w").reshape(-1)[:n_gw]
    gw_full = gw_full.reshape(N_CHIPS, L, GATE_RANK, gw_cols).transpose(1, 2, 0, 3).reshape(L, GATE_RANK, B_KEY_WIDTH)

    xs = x.reshape(S, D_MODEL)
    mem_b = mem.reshape(mem.shape[1], D_MODEL).astype(BF16)
    target = loss_target.reshape(S, D_MODEL)

    small_w = []
    for l in range(L):
        gw_l, gb_l = _pad_gate(gw_full[l], b_gate_b[l])
        small_w.append((_bias_by_offset(a_rel_bias[l]), gw_l, gb_l,
                        b_norm_g[l].reshape(1, LANES), ln_g[l].reshape(1, D_MODEL), ln_b[l].reshape(1, D_MODEL)))

    y, yb = xs, xs.astype(BF16)
    yt = _transpose(yb, "x_t")
    saved = []
    for l in range(L):
        if l + 1 < L:
            rider = (_WeightGather("chips", l + 1, rows), shards, landing(l + 1))
            y, yb, yt, sv, bufs = _layer_fwd(y, yb, yt, mem_b, *weights[l], *small_w[l], rider=rider)
            weights.append(assembled(_WeightGather("pair", l + 1, rows).call(shards, bufs, f"gather_pair_{l + 1}")))
        else:
            y, yb, yt, sv = _layer_fwd(y, yb, yt, mem_b, *weights[l], *small_w[l])
        saved.append(sv)
    layer_w = [(*weights[l], *small_w[l]) for l in range(L)]
    loss_part, dy = _loss_head(y, target)

    halves = [D_MODEL // 2, r_kv // 2, r_out // 2]
    dests = [lax.empty((L, 2, hf, w.shape[2]), F32) for hf, w in zip(halves, (w_in, w_mem_kv, w_out))]
    slices = [(halves[0], [(j, 0) for j in range(N_CHIPS)]),
              (halves[1], [(N_CHIPS, j * r_kv) for j in range(N_CHIPS)]),
              (halves[2], [(N_CHIPS + 1, j * r_out) for j in range(N_CHIPS)])]

    def reduction(l, g, into):
        arrays = [_chip_columns(g[0], j, n_in) for j in range(N_CHIPS)] + [g[5], g[6]]
        return _LayerReduce(l, arrays, slices, c_idx, chip, into)

    grads, reduce = [None] * L, None
    for l in reversed(range(L)):
        w_in_l, w_kv_l, w_out_l, u_l, gw_l, gb_l, gn_l, lg_l, lb_l = layer_w[l]
        args = (dy, saved[l], mem_b, w_in_l, w_out_l, u_l, gw_l, gb_l, gn_l, lg_l)
        if l > 0:
            dy, grads[l] = _layer_bwd(*args, reduce=reduce)
            if reduce is not None:
                dests = reduce.finish()
            reduce = reduction(l, grads[l], dests)
        else:
            own = lambda g, above: reduction(0, g, dests if above is None else above)
            dy, grads[l], reduce = _layer_bwd(*args, reduce=reduce, own_reduce=own)
    r_w_in, r_w_kv, r_w_out = [d.reshape(L, 2 * d.shape[2], d.shape[3]) for d in reduce.finish()]
    grad_x = dy.reshape(x.shape)

    g_rel = jnp.stack([_bias_grad_from_offset(g[1]) for g in grads])
    g_gw = jnp.stack([_unpad_heads(g[2][LR_LANE:LR_LANE + GATE_RANK]) for g in grads])
    g_gb = jnp.stack([_unpad_heads(g[3])[0] for g in grads])
    g_gn = jnp.stack([g[4][0] for g in grads])
    g_lg = jnp.stack([g[7][0] for g in grads])
    g_lb = jnp.stack([g[8][0] for g in grads])

    small = [g_rel, g_gw, g_gb, g_gn, g_lg, g_lb, loss_part]
    flat = jnp.concatenate([s.reshape(-1) for s in small])
    n_small = flat.shape[0]
    pad = (-n_small) % (8 * LANES)
    red = _all_reduce_small(jnp.pad(flat, (0, pad)).reshape(-1, LANES), "all_reduce_small").reshape(-1)
    outs, off = [], 0
    for s in small:
        outs.append(red[off:off + s.size].reshape(s.shape))
        off += s.size
    g_rel, g_gw, g_gb, g_gn, g_lg, g_lb, loss = outs
    loss = loss.reshape(())
    g_gw = lax.dynamic_slice_in_dim(g_gw.reshape(L, GATE_RANK, N_CHIPS, gw_cols), chip, 1, axis=2).reshape(L, GATE_RANK, gw_cols)

    g_list = [r_w_in, g_rel, g_gw, g_gb, g_gn, r_w_kv, r_w_out, g_lg, g_lb]
    w_list = [w_in, a_rel_bias, b_gate_w, b_gate_b, b_norm_g, w_mem_kv, w_out, ln_g, ln_b]
    m_list = [m_w_in, m_a_rel_bias, m_b_gate_w, m_b_gate_b, m_b_norm_g, m_w_mem_kv, m_w_out, m_ln_g, m_ln_b]
    v_list = [v_w_in, v_a_rel_bias, v_b_gate_w, v_b_gate_b, v_b_norm_g, v_w_mem_kv, v_w_out, v_ln_g, v_ln_b]
    names = ["w_in", "rel", "gate_w", "gate_b", "norm_g", "w_kv", "w_out", "ln_g", "ln_b"]
    to_cols = lambda a: jnp.transpose(a, (2, 0, 1))
    upd = [tuple(jnp.transpose(o, (1, 2, 0)) for o in
                 _adamw(to_cols(w_in), to_cols(r_w_in), to_cols(m_w_in), to_cols(v_w_in), "adamw_w_in"))]
    upd += [_adamw_nd(w, g, m, v, "adamw_" + n)
            for w, g, m, v, n in list(zip(w_list, g_list, m_list, v_list, names))[1:]]
    deltas = [u_[0] for u_ in upd]
    new_m = [u_[1] for u_ in upd]
    new_v = [u_[2] for u_ in upd]
    return (loss, grad_x, *g_list, *deltas, *new_m, *new_v)
```

```python
import functools

import numpy as np
import jax
import jax.numpy as jnp
from jax import lax
from jax.experimental import pallas as pl
from jax.experimental.pallas import tpu as pltpu

F32 = jnp.float32
BF16 = jnp.bfloat16
MESH = pl.DeviceIdType.MESH

D_MODEL = 2048
DEPTH = 4
CHUNK = 64
LEFT_CHUNKS = 8
MAX_REL = 128
N_REL = 2 * MAX_REL + 1
A_HEADS = 8
HEAD_DIM = 128
B_HEADS = 4
B_DK = 64
M_HEADS = 4
GATE_RANK = 16
GATE_TAU = 16.0
A_WIDTH = A_HEADS * HEAD_DIM
B_WIDTH = B_HEADS * HEAD_DIM
B_KEY_WIDTH = B_HEADS * B_DK
M_WIDTH = M_HEADS * HEAD_DIM
IN_WIDTH = 4 * A_WIDTH + 2 * B_KEY_WIDTH + 2 * B_WIDTH + GATE_RANK + 2 * M_WIDTH
ALPHA = (2.0 * DEPTH) ** 0.25
LN_EPS = 1e-5
RMS_EPS = 1e-6
NEG_INF = -1e30
ADAM_LR = 0.001
ADAM_B1 = 0.9
ADAM_B2 = 0.999
ADAM_EPS = 1e-08
ADAM_WD = 0.01
ADAM_STEP = 10

LANES = 128
VMEM_LIMIT = 56 * 1024 * 1024

C_A, C_B, C_M = 0, 4096, 6144
HP = 7168
LR_HEAD, LR_LANE = B_HEADS - 1, B_DK
C_LR = C_B + LR_HEAD * 4 * LANES
SEG_Q, SEG_K, SEG_V, SEG_Z = 0, 1, 2, 3
TQ = 512
CPB = TQ // CHUNK
N_CHIPS = 4
N_DEV = 8


def _params(sem, vmem=VMEM_LIMIT):
    return pltpu.CompilerParams(dimension_semantics=sem, vmem_limit_bytes=vmem)


def _dot(a, b):
    return jnp.dot(a, b, preferred_element_type=F32)


def _dot_nt(a, b):
    return lax.dot_general(a, b, (((1,), (1,)), ((), ())), preferred_element_type=F32)


def _dot_tn(a, b):
    return lax.dot_general(a, b, (((0,), (0,)), ((), ())), preferred_element_type=F32)


def _sigmoid(x):
    return 1.0 / (1.0 + jnp.exp(-x))


def _split3(x):
    hi = x.astype(BF16)
    r = x - hi.astype(F32)
    mid = r.astype(BF16)
    lo = (r - mid.astype(F32)).astype(BF16)
    return hi, mid, lo


def _dot3(m_bf, x):
    hi, mid, lo = _split3(x)
    return _dot(m_bf, hi) + _dot(m_bf, mid) + _dot(m_bf, lo)


def _matmul(a, b, *, mode, out_dtype, tm, tn, tk, name, add=None, add_scale=1.0, rider=None):
    if mode == "nn":
        (M, K), (K2, N) = a.shape, b.shape
        a_spec = pl.BlockSpec((tm, tk), lambda i, j, k: (i, k))
        b_spec = pl.BlockSpec((tk, tn), lambda i, j, k: (k, j))
        dot = _dot
    elif mode == "nt":
        (M, K), (N, K2) = a.shape, b.shape
        a_spec = pl.BlockSpec((tm, tk), lambda i, j, k: (i, k))
        b_spec = pl.BlockSpec((tn, tk), lambda i, j, k: (j, k))
        dot = _dot_nt
    else:
        (K, M), (K2, N) = a.shape, b.shape
        a_spec = pl.BlockSpec((tk, tm), lambda i, j, k: (k, i))
        b_spec = pl.BlockSpec((tk, tn), lambda i, j, k: (k, j))
        dot = _dot_tn
    assert K == K2 and M % tm == 0 and N % tn == 0 and K % tk == 0, (a.shape, b.shape, mode)
    nk = K // tk
    has_add = add is not None
    assert nk == 1 or out_dtype == F32
    grid = (M // tm, N // tn, nk)
    hop, srcs, bufs = rider if rider is not None else (None, [], [])
    n_in = 2 + has_add

    def body(*refs):
        a_ref, b_ref = refs[:2]
        add_ref = refs[2] if has_add else None
        src_refs = refs[n_in:n_in + len(srcs)]
        o_ref = refs[n_in + len(srcs) + len(bufs)]
        buf_refs = refs[n_in + len(srcs) + len(bufs) + 1:n_in + len(srcs) + 2 * len(bufs) + 1]
        sems = refs[n_in + len(srcs) + 2 * len(bufs) + 1:]
        i, j, k = pl.program_id(0), pl.program_id(1), pl.program_id(2)
        if hop is not None:
            @pl.when(jnp.logical_and(jnp.logical_and(i == 0, j == 0), k == 0))
            def _():
                hop.start(src_refs, buf_refs, *sems)

        part = dot(a_ref[...].astype(BF16), b_ref[...].astype(BF16))

        @pl.when(k == 0)
        def _():
            first = part + add_scale * add_ref[...] if has_add else part
            o_ref[...] = first.astype(out_dtype)

        if nk > 1:
            @pl.when(k > 0)
            def _():
                o_ref[...] += part

        if hop is not None:
            @pl.when(jnp.logical_and(jnp.logical_and(i == grid[0] - 1, j == grid[1] - 1), k == nk - 1))
            def _():
                hop.wait(src_refs, buf_refs, *sems)

    in_specs = [a_spec, b_spec]
    args = [a, b]
    if has_add:
        in_specs.append(pl.BlockSpec((tm, tn), lambda i, j, k: (i, j)))
        args.append(add)
    out = pl.pallas_call(
        body,
        name=name,
        grid=grid,
        in_specs=in_specs + [ANY] * (len(srcs) + len(bufs)),
        out_specs=[pl.BlockSpec((tm, tn), lambda i, j, k: (i, j))] + [ANY] * len(bufs),
        out_shape=[jax.ShapeDtypeStruct((M, N), out_dtype)] + [jax.ShapeDtypeStruct(x.shape, x.dtype) for x in bufs],
        input_output_aliases={n_in + len(srcs) + t: 1 + t for t in range(len(bufs))},
        scratch_shapes=hop.sems() if hop is not None else [],
        compiler_params=_params(("parallel", "parallel", "arbitrary") if hop is None
                                else ("arbitrary", "arbitrary", "arbitrary")),
    )(*args, *srcs, *bufs)
    return out[0] if hop is None else (out[0], list(out[1:]))


def _transpose(a, name):
    R, C = a.shape
    t = 512

    def body(a_ref, o_ref):
        o_ref[...] = a_ref[...].T

    return pl.pallas_call(
        body, name=name, grid=(R // t, C // t),
        in_specs=[pl.BlockSpec((t, t), lambda i, j: (i, j))],
        out_specs=pl.BlockSpec((t, t), lambda i, j: (j, i)),
        out_shape=jax.ShapeDtypeStruct((C, R), a.dtype),
        compiler_params=_params(("parallel", "parallel")),
    )(a)


def _chunk_of(rows):
    return lax.shift_right_logical(rows, CHUNK.bit_length() - 1)


A_HPS = 2
A_HW = A_HPS * LANES


def _band_bias(u_row, first):
    bias = pltpu.roll(jnp.broadcast_to(u_row, (TQ, 2 * TQ)), 0, 1, stride=1, stride_axis=0)
    qc = _chunk_of(lax.broadcasted_iota(jnp.int32, (TQ, 2 * TQ), 0))
    col = lax.broadcasted_iota(jnp.int32, (TQ, 2 * TQ), 1)
    kc = _chunk_of(jnp.bitwise_and(col, TQ - 1))
    ok = jnp.logical_or(jnp.logical_and(col < TQ, kc >= qc), jnp.logical_and(col >= TQ, kc <= qc))
    return jnp.where(ok, bias, NEG_INF) + jnp.where(col < TQ, first * NEG_INF, 0.0)


HQ = TQ // 2
HALVES = ((slice(0, HQ), slice(0, 3 * HQ)),
          (slice(HQ, TQ), slice(HQ, 4 * HQ)))


def _band_probs(q, kcat, bias):
    scale = HEAD_DIM ** -0.5
    out = []
    for rows, cols in HALVES:
        s = _dot_nt(q[rows], kcat[cols]) * scale + bias[rows, cols]
        p = jnp.exp(s - jnp.max(s, axis=1, keepdims=True))
        out.append((p, 1.0 / jnp.sum(p, axis=1, keepdims=True)))
    return out


def _band_specs(nq):
    def col(seg, h):
        return C_A // A_HW + 4 * h + seg

    q_spec = pl.BlockSpec((TQ, A_HW), lambda h, i: (jnp.minimum(i, nq - 1), col(SEG_Q, h)))
    kp_spec = pl.BlockSpec((TQ, A_HW), lambda h, i: (jnp.clip(i - 1, 0, nq - 1), col(SEG_K, h)))
    kc_spec = pl.BlockSpec((TQ, A_HW), lambda h, i: (jnp.minimum(i, nq - 1), col(SEG_K, h)))
    vp_spec = pl.BlockSpec((TQ, A_HW), lambda h, i: (jnp.clip(i - 1, 0, nq - 1), col(SEG_V, h)))
    vc_spec = pl.BlockSpec((TQ, A_HW), lambda h, i: (jnp.minimum(i, nq - 1), col(SEG_V, h)))
    z_spec = pl.BlockSpec((TQ, A_HW), lambda h, i: (jnp.minimum(i, nq - 1), col(SEG_Z, h)))
    u_spec = pl.BlockSpec((A_HPS, 1, 2 * TQ), lambda h, i: (h, 0, 0))
    return q_spec, kp_spec, kc_spec, vp_spec, vc_spec, z_spec, u_spec


def _band_fwd(h, u):
    S = h.shape[0]
    nq = S // TQ

    def body(q_ref, kp_ref, kc_ref, vp_ref, vc_ref, z_ref, u_ref, y_ref, yt_ref, bias_scr):
        i = pl.program_id(1)

        @pl.when(i <= 1)
        def _():
            for hh in range(A_HPS):
                bias_scr[hh] = _band_bias(u_ref[hh], (i == 0).astype(F32))

        for hh in range(A_HPS):
            cs = slice(hh * LANES, (hh + 1) * LANES)
            kcat = jnp.concatenate([kp_ref[:, cs], kc_ref[:, cs]], axis=0)
            vcat = jnp.concatenate([vp_ref[:, cs], vc_ref[:, cs]], axis=0)
            probs = _band_probs(q_ref[:, cs], kcat, bias_scr[hh])
            o = jnp.concatenate([_dot(p.astype(BF16), vcat[cols]) * inv
                                 for (p, inv), (_, cols) in zip(probs, HALVES)], axis=0)
            z = z_ref[:, cs].astype(F32)
            y = o * (z * _sigmoid(z))
            y_ref[:, cs] = y.astype(BF16)
            yt_ref[cs, :] = y.T.astype(BF16)

    specs = _band_specs(nq)
    return pl.pallas_call(
        body,
        name="band_fwd",
        grid=(A_HEADS // A_HPS, nq),
        in_specs=[specs[0], specs[1], specs[2], specs[3], specs[4], specs[5], specs[6]],
        out_specs=[pl.BlockSpec((TQ, A_HW), lambda h, i: (i, h)), pl.BlockSpec((A_HW, TQ), lambda h, i: (h, i))],
        out_shape=[jax.ShapeDtypeStruct((S, D_MODEL), BF16), jax.ShapeDtypeStruct((D_MODEL, S), BF16)],
        scratch_shapes=[pltpu.VMEM((A_HPS, TQ, 2 * TQ), F32)],
        compiler_params=_params(("parallel", "arbitrary")),
    )(h, h, h, h, h, h, u)


def _band_bwd(h, u, dycat):
    S = h.shape[0]
    nq = S // TQ
    scale = HEAD_DIM ** -0.5
    qs, ks, vs, zs = (slice(s * A_HW, (s + 1) * A_HW) for s in (SEG_Q, SEG_K, SEG_V, SEG_Z))

    def body(q_ref, kp_ref, kc_ref, vp_ref, vc_ref, z_ref, u_ref, dy_ref,
             dh_ref, du_ref, bias_scr, db_scr, ckt_scr, cvt_scr, cq_scr, cz_scr):
        i = pl.program_id(1)

        @pl.when(i <= 1)
        def _():
            for hh in range(A_HPS):
                bias_scr[hh] = _band_bias(u_ref[hh], (i == 0).astype(F32))

        @pl.when(i == 0)
        def _():
            db_scr[...] = jnp.zeros_like(db_scr)
            ckt_scr[...] = jnp.zeros_like(ckt_scr)
            cvt_scr[...] = jnp.zeros_like(cvt_scr)
            cq_scr[...] = jnp.zeros_like(cq_scr)
            cz_scr[...] = jnp.zeros_like(cz_scr)

        @pl.when(i < nq)
        def _():
            dh_ref[:, qs] = cq_scr[...]
            dh_ref[:, zs] = cz_scr[...]
            for hh in range(A_HPS):
                cs = slice(hh * LANES, (hh + 1) * LANES)
                q = q_ref[:, cs]
                kcat = jnp.concatenate([kp_ref[:, cs], kc_ref[:, cs]], axis=0)
                vcat = jnp.concatenate([vp_ref[:, cs], vc_ref[:, cs]], axis=0)
                probs = [p * inv for p, inv in _band_probs(q, kcat, bias_scr[hh])]
                o = jnp.concatenate([_dot(p.astype(BF16), vcat[cols]) for p, (_, cols) in zip(probs, HALVES)], axis=0)
                z = z_ref[:, cs].astype(F32)
                sg = _sigmoid(z)
                dy = dy_ref[:, cs].astype(F32)
                do = dy * (z * sg)
                cz_scr[:, cs] = (dy * o * (sg * (1.0 + z * (1.0 - sg)))).astype(BF16)
                dob = do.astype(BF16)
                delta = jnp.sum(do * o, axis=1, keepdims=True)
                qt, dot_ = q.T, dob.T
                dq, dkt, dvt = [], [], []
                for p, (rows, cols) in zip(probs, HALVES):
                    ds = p * (_dot_nt(dob[rows], vcat[cols]) - delta[rows])
                    db_scr[hh, rows, cols] += ds
                    dsb = ds.astype(BF16)
                    dq.append(scale * _dot(dsb, kcat[cols]))
                    dkt.append(scale * _dot(qt[:, rows], dsb))
                    dvt.append(_dot(dot_[:, rows], p.astype(BF16)))
                cq_scr[:, cs] = jnp.concatenate(dq, axis=0).astype(BF16)

                def over_keys(parts):
                    lo, hi = parts
                    prev = jnp.concatenate([lo[:, :HQ], lo[:, HQ:2 * HQ] + hi[:, :HQ]], axis=1)
                    cur = jnp.concatenate([lo[:, 2 * HQ:] + hi[:, HQ:2 * HQ], hi[:, 2 * HQ:]], axis=1)
                    return prev, cur

                (dk_prev, dk_cur), (dv_prev, dv_cur) = over_keys(dkt), over_keys(dvt)
                dh_ref[:, SEG_K * A_HW + hh * LANES:SEG_K * A_HW + (hh + 1) * LANES] = (
                    ckt_scr[cs, :] + dk_prev).T.astype(BF16)
                dh_ref[:, SEG_V * A_HW + hh * LANES:SEG_V * A_HW + (hh + 1) * LANES] = (
                    cvt_scr[cs, :] + dv_prev).T.astype(BF16)
                ckt_scr[cs, :] = dk_cur
                cvt_scr[cs, :] = dv_cur

        @pl.when(i == nq)
        def _():
            dh_ref[:, qs] = cq_scr[...]
            dh_ref[:, zs] = cz_scr[...]
            dh_ref[:, ks] = ckt_scr[...].T.astype(BF16)
            dh_ref[:, vs] = cvt_scr[...].T.astype(BF16)
            r0 = lax.broadcasted_iota(jnp.int32, (TQ, TQ), 0)
            r1 = lax.broadcasted_iota(jnp.int32, (TQ, TQ), 1)
            flip = (r0 + r1 == TQ - 1).astype(BF16)
            for hh in range(A_HPS):
                fl = _dot3(flip, db_scr[hh])
                rolled = pltpu.roll(fl, 0, 1, stride=1, stride_axis=0)
                du_ref[hh] = jnp.sum(rolled, axis=0, keepdims=True)

    specs = _band_specs(nq)
    dy_spec = pl.BlockSpec((TQ, A_HW), lambda h, i: (jnp.minimum(i, nq - 1), h))
    return pl.pallas_call(
        body,
        name="band_bwd",
        grid=(A_HEADS // A_HPS, nq + 1),
        in_specs=[specs[0], specs[1], specs[2], specs[3], specs[4], specs[5], specs[6], dy_spec],
        out_specs=[pl.BlockSpec((TQ, 4 * A_HW), lambda h, i: (jnp.maximum(i - 1, 0), C_A // (4 * A_HW) + h)),
                   pl.BlockSpec((A_HPS, 1, 2 * TQ), lambda h, i: (h, 0, 0))],
        out_shape=[jax.ShapeDtypeStruct((S, HP), BF16), jax.ShapeDtypeStruct((A_HEADS, 1, 2 * TQ), F32)],
        scratch_shapes=[pltpu.VMEM((A_HPS, TQ, 2 * TQ), F32), pltpu.VMEM((A_HPS, TQ, 2 * TQ), F32),
                        pltpu.VMEM((A_HW, TQ), F32), pltpu.VMEM((A_HW, TQ), F32),
                        pltpu.VMEM((TQ, A_HW), BF16), pltpu.VMEM((TQ, A_HW), BF16)],
        compiler_params=_params(("parallel", "arbitrary")),
    )(h, h, h, h, h, h, u, dycat)


def _bias_by_offset(table):
    far = jnp.broadcast_to(table[:, N_REL - 1:], (A_HEADS, TQ - MAX_REL))
    ramp = jnp.flip(table, axis=1)
    rest = jnp.broadcast_to(table[:, :1], (A_HEADS, 2 * TQ - CHUNK - (TQ + MAX_REL + 1)))
    wrap = jnp.broadcast_to(table[:, N_REL - 1:], (A_HEADS, CHUNK))
    return jnp.concatenate([far, ramp, rest, wrap], axis=1)[:, None, :]


def _bias_grad_from_offset(du):
    g = jnp.roll(du[:, 0, :], -(TQ - 1), axis=1)
    far = jnp.sum(g[:, :TQ - MAX_REL], axis=1) + jnp.sum(g[:, 2 * TQ - CHUNK:], axis=1)
    ramp = jnp.flip(g[:, TQ - MAX_REL:TQ + MAX_REL + 1], axis=1)
    return ramp.at[:, N_REL - 1].add(far)


def _mem_probs(q, mk):
    s = _dot_nt(q, mk) * (HEAD_DIM ** -0.5)
    p = jnp.exp(s - jnp.max(s, axis=1, keepdims=True))
    return p * (1.0 / jnp.sum(p, axis=1, keepdims=True))


def _mem_cols(hd, seg):
    return slice((2 * hd + seg) * LANES, (2 * hd + seg + 1) * LANES)


def _mem_fwd(h, mkv, ycat, ycat_t):
    S = h.shape[0]
    nm = mkv.shape[0]
    c0 = (A_WIDTH + B_WIDTH) // LANES

    def body(qz_ref, mkv_ref, yin_ref, ytin_ref, y_ref, yt_ref):
        del yin_ref, ytin_ref
        for hd in range(M_HEADS):
            lane = slice(hd * LANES, (hd + 1) * LANES)
            p = _mem_probs(qz_ref[:, _mem_cols(hd, 0)], mkv_ref[:, lane])
            o = _dot(p.astype(BF16), mkv_ref[:, M_WIDTH + hd * LANES:M_WIDTH + (hd + 1) * LANES])
            z = qz_ref[:, _mem_cols(hd, 1)].astype(F32)
            y = o * (z * _sigmoid(z))
            y_ref[:, lane] = y.astype(BF16)
            yt_ref[lane, :] = y.T.astype(BF16)

    return pl.pallas_call(
        body,
        name="mem_fwd",
        grid=(S // TQ,),
        in_specs=[pl.BlockSpec((TQ, 2 * M_WIDTH), lambda i: (i, C_M // (2 * M_WIDTH))),
                  pl.BlockSpec((nm, 2 * M_WIDTH), lambda i: (0, 0)), ANY, ANY],
        out_specs=[pl.BlockSpec((TQ, M_WIDTH), lambda i: (i, c0 // M_HEADS)),
                   pl.BlockSpec((M_WIDTH, TQ), lambda i: (c0 // M_HEADS, i))],
        out_shape=[jax.ShapeDtypeStruct(ycat.shape, BF16), jax.ShapeDtypeStruct(ycat_t.shape, BF16)],
        input_output_aliases={2: 0, 3: 1},
        compiler_params=_params(("parallel",)),
    )(h, mkv, ycat, ycat_t)


def _mem_bwd(h, mkv, dycat, dh):
    S = h.shape[0]
    nm = mkv.shape[0]
    scale = HEAD_DIM ** -0.5

    def body(qz_ref, mkv_ref, dy_ref, dhin_ref, dh_ref, dmkv_ref):
        del dhin_ref
        i = pl.program_id(0)
        for hd in range(M_HEADS):
            lane = slice(hd * LANES, (hd + 1) * LANES)
            lane_v = slice(M_WIDTH + hd * LANES, M_WIDTH + (hd + 1) * LANES)
            q, mk, mv = qz_ref[:, _mem_cols(hd, 0)], mkv_ref[:, lane], mkv_ref[:, lane_v]
            p = _mem_probs(q, mk)
            pb = p.astype(BF16)
            o = _dot(pb, mv)
            z = qz_ref[:, _mem_cols(hd, 1)].astype(F32)
            sg = _sigmoid(z)
            dy = dy_ref[:, lane].astype(F32)
            do = dy * (z * sg)
            dh_ref[:, _mem_cols(hd, 1)] = (dy * o * (sg * (1.0 + z * (1.0 - sg)))).astype(BF16)
            dob = do.astype(BF16)
            ds = p * (_dot_nt(dob, mv) - jnp.sum(do * o, axis=1, keepdims=True))
            dsb = ds.astype(BF16)
            dh_ref[:, _mem_cols(hd, 0)] = (scale * _dot(dsb, mk)).astype(BF16)
            dmk = scale * _dot_tn(dsb, q)
            dmv = _dot_tn(pb, dob)

            @pl.when(i == 0)
            def _():
                dmkv_ref[:, lane] = dmk
                dmkv_ref[:, lane_v] = dmv

            @pl.when(i > 0)
            def _():
                dmkv_ref[:, lane] += dmk
                dmkv_ref[:, lane_v] += dmv

    return pl.pallas_call(
        body,
        name="mem_bwd",
        grid=(S // TQ,),
        in_specs=[pl.BlockSpec((TQ, 2 * M_WIDTH), lambda i: (i, C_M // (2 * M_WIDTH))),
                  pl.BlockSpec((nm, 2 * M_WIDTH), lambda i: (0, 0)),
                  pl.BlockSpec((TQ, M_WIDTH), lambda i: (i, (A_WIDTH + B_WIDTH) // M_WIDTH)), ANY],
        out_specs=[pl.BlockSpec((TQ, 2 * M_WIDTH), lambda i: (i, C_M // (2 * M_WIDTH))),
                   pl.BlockSpec((nm, 2 * M_WIDTH), lambda i: (0, 0))],
        out_shape=[jax.ShapeDtypeStruct(dh.shape, BF16), jax.ShapeDtypeStruct((nm, 2 * M_WIDTH), F32)],
        input_output_aliases={3: 0},
        compiler_params=_params(("arbitrary",)),
    )(h, mkv, dycat, dh)


SUB = 2 * CHUNK
SUBS = [slice(s * SUB, (s + 1) * SUB) for s in range(TQ // SUB)]


def _chunk_masks():
    r = lax.broadcasted_iota(jnp.int32, (SUB, SUB), 0)
    c = lax.broadcasted_iota(jnp.int32, (SUB, SUB), 1)
    same = _chunk_of(r) == _chunk_of(c)
    return jnp.logical_and(same, c <= r), jnp.logical_and(same, c > r), jnp.logical_and(same, c >= r)


def _by_sub(fn):
    return jnp.concatenate([fn(rows) for rows in SUBS], axis=0)


def _gla_gates(lr, gw, gb):
    logit = _dot(lr, gw) + gb
    sg = _sigmoid(logit)
    g = (jnp.minimum(logit, 0.0) - jnp.log(1.0 + jnp.exp(-jnp.abs(logit)))) * (1.0 / GATE_TAU)
    lo = _chunk_masks()[0].astype(BF16)
    return sg, _by_sub(lambda rows: _dot3(lo, g[rows]))


def _gla_factors(q, k, b):
    eb = jnp.exp(b)
    enb = jnp.exp(-b)
    return eb, enb, q * eb, q * enb, k * eb, k * enb


def _gla_intra(qp, qn, kp, kn):
    lo, up, _ = _chunk_masks()
    qp, qn, kp, kn = qp.astype(BF16), qn.astype(BF16), kp.astype(BF16), kn.astype(BF16)
    return [jnp.where(lo, _dot_nt(qp[rows], kn[rows]), 0.0) + jnp.where(up, _dot_nt(qn[rows], kp[rows]), 0.0)
            for rows in SUBS]


B_HPS = 2
B_HW = B_HPS * LANES


def _gla_specs(nb, rev):
    blk = (lambda i: nb - 1 - i) if rev else (lambda i: i)
    qkvz_spec = pl.BlockSpec((TQ, 4 * B_HW), lambda i, p: (blk(i), C_B // (4 * B_HW) + p))
    lr_spec = pl.BlockSpec((TQ, LANES), lambda i, p: (blk(i), C_LR // LANES))
    gw_spec = pl.BlockSpec((LANES, B_HW), lambda i, p: (0, p))
    gb_spec = pl.BlockSpec((1, B_HW), lambda i, p: (0, p))
    gn_spec = pl.BlockSpec((1, LANES), lambda i, p: (0, 0))
    return qkvz_spec, lr_spec, gw_spec, gb_spec, gn_spec, blk


def _head_cols(hh, seg):
    return slice((4 * hh + seg) * LANES, (4 * hh + seg + 1) * LANES)


def _gla_fwd(h, gw, gb, gn, ycat, ycat_t):
    S = h.shape[0]
    nb = S // TQ
    c0 = A_WIDTH // LANES

    def body(qkvz_ref, lr_ref, gw_ref, gb_ref, gn_ref, yin_ref, ytin_ref,
             y_ref, yt_ref, o_ref, st_ref, st_scr):
        del yin_ref, ytin_ref
        i, p = pl.program_id(0), pl.program_id(1)
        for hh in range(B_HPS):
            hd = B_HPS * p + hh
            lane = slice(hh * LANES, (hh + 1) * LANES)

            @pl.when(i == 0)
            def _():
                st_scr[hd] = jnp.zeros((LANES, LANES), F32)

            q = qkvz_ref[:, _head_cols(hh, SEG_Q)].astype(F32) * (B_DK ** -0.5)
            k = qkvz_ref[:, _head_cols(hh, SEG_K)].astype(F32)
            v = qkvz_ref[:, _head_cols(hh, SEG_V)]
            _, b = _gla_gates(lr_ref[...], gw_ref[:, lane], gb_ref[:, lane])
            _, _, qp, qn, kp, kn = _gla_factors(q, k, b)
            o_intra = jnp.concatenate([_dot(a.astype(BF16), v[rows])
                                       for a, rows in zip(_gla_intra(qp, qn, kp, kn), SUBS)], axis=0)
            qpb, knb = qp.astype(BF16), kn.astype(BF16)
            st = st_scr[hd]
            outs = []
            for c in range(CPB):
                rows = slice(c * CHUNK, (c + 1) * CHUNK)
                st_ref[hh, c] = st
                outs.append(_dot_nt(qpb[rows], st.astype(BF16)))
                e_last = jnp.exp(b[(c + 1) * CHUNK - 1:(c + 1) * CHUNK, :])
                st = (st + _dot_tn(v[rows], knb[rows])) * e_last
            st_scr[hd] = st
            o = o_intra + jnp.concatenate(outs, axis=0)
            o_ref[:, lane] = o
            r = lax.rsqrt(jnp.mean(o * o, axis=1, keepdims=True) + RMS_EPS)
            z = qkvz_ref[:, _head_cols(hh, SEG_Z)].astype(F32)
            y = o * r * gn_ref[...] * (z * _sigmoid(z))
            y_ref[:, lane] = y.astype(BF16)
            yt_ref[lane, :] = y.T.astype(BF16)

    qkvz_s, lr_s, gw_s, gb_s, gn_s, _ = _gla_specs(nb, False)
    return pl.pallas_call(
        body,
        name="gla_fwd",
        grid=(nb, B_HEADS // B_HPS),
        in_specs=[qkvz_s, lr_s, gw_s, gb_s, gn_s, ANY, ANY],
        out_specs=[pl.BlockSpec((TQ, B_HW), lambda i, p: (i, c0 // B_HPS + p)),
                   pl.BlockSpec((B_HW, TQ), lambda i, p: (c0 // B_HPS + p, i)),
                   pl.BlockSpec((TQ, B_HW), lambda i, p: (i, p)),
                   pl.BlockSpec((B_HPS, CPB, LANES, LANES), lambda i, p: (p, i, 0, 0))],
        out_shape=[jax.ShapeDtypeStruct(ycat.shape, BF16), jax.ShapeDtypeStruct(ycat_t.shape, BF16),
                   jax.ShapeDtypeStruct((S, B_WIDTH), F32),
                   jax.ShapeDtypeStruct((B_HEADS, S // CHUNK, LANES, LANES), F32)],
        input_output_aliases={5: 0, 6: 1},
        scratch_shapes=[pltpu.VMEM((B_HEADS, LANES, LANES), F32)],
        compiler_params=_params(("arbitrary", "arbitrary")),
    )(h, h, gw, gb, gn, ycat, ycat_t)


def _gla_bwd(h, gw, gb, gn, o_pre, states, dycat, dh):
    S = h.shape[0]
    nb = S // TQ
    n_steps = B_HEADS // B_HPS

    def body(qkvz_ref, lr_ref, gw_ref, gb_ref, gn_ref, o_ref, st_ref, dy_ref, dhin_ref,
             dh_ref, dgw_ref, dgb_ref, dgn_ref,
             dst_scr, dgw_scr, dgb_scr, dgn_scr, dlr_scr):
        del dhin_ref
        i, p = pl.program_id(0), pl.program_id(1)

        @pl.when(jnp.logical_and(i == 0, p == 0))
        def _():
            dgn_scr[...] = jnp.zeros_like(dgn_scr)

        dlr_heads = [one_head(hh, i, p, qkvz_ref, lr_ref, gw_ref, gb_ref, gn_ref, o_ref, st_ref, dy_ref,
                              dh_ref, dgw_ref, dgb_ref, dst_scr, dgw_scr, dgb_scr, dgn_scr) for hh in range(B_HPS)]
        dlr = dlr_heads[0]
        for more in dlr_heads[1:]:
            dlr = dlr + more

        @pl.when(p == 0)
        def _():
            dlr_scr[...] = dlr

        @pl.when(p > 0)
        def _():
            dlr_scr[...] += dlr

        @pl.when(p == n_steps - 1)
        def _():
            cols = _head_cols(B_HPS - 1, SEG_Q)
            dh_ref[:, cols] = (dh_ref[:, cols].astype(F32) + dlr_scr[...]).astype(BF16)

        @pl.when(i == nb - 1)
        def _():
            dgn_ref[...] = dgn_scr[...]

    def one_head(hh, i, p, qkvz_ref, lr_ref, gw_ref, gb_ref, gn_ref, o_ref, st_ref, dy_ref,
                 dh_ref, dgw_ref, dgb_ref, dst_scr, dgw_scr, dgb_scr, dgn_scr):
        hd = B_HPS * p + hh
        lane = slice(hh * LANES, (hh + 1) * LANES)

        @pl.when(i == 0)
        def _():
            dst_scr[hd] = jnp.zeros((LANES, LANES), F32)
            dgw_scr[hd] = jnp.zeros((LANES, LANES), F32)
            dgb_scr[hd] = jnp.zeros((1, LANES), F32)

        q = qkvz_ref[:, _head_cols(hh, SEG_Q)].astype(F32) * (B_DK ** -0.5)
        k = qkvz_ref[:, _head_cols(hh, SEG_K)].astype(F32)
        v = qkvz_ref[:, _head_cols(hh, SEG_V)]
        lr, gwv = lr_ref[...], gw_ref[:, lane]
        sg, b = _gla_gates(lr, gwv, gb_ref[:, lane])
        eb, enb, qp, qn, kp, kn = _gla_factors(q, k, b)
        a = _gla_intra(qp, qn, kp, kn)
        qpb, qnb, kpb, knb = qp.astype(BF16), qn.astype(BF16), kp.astype(BF16), kn.astype(BF16)

        o = o_ref[:, lane]
        gn = gn_ref[...]
        r = lax.rsqrt(jnp.mean(o * o, axis=1, keepdims=True) + RMS_EPS)
        z = qkvz_ref[:, _head_cols(hh, SEG_Z)].astype(F32)
        sz = _sigmoid(z)
        dy = dy_ref[:, lane].astype(F32)
        d_on = dy * (z * sz)
        dh_ref[:, _head_cols(hh, SEG_Z)] = (dy * (o * r * gn) * (sz * (1.0 + z * (1.0 - sz)))).astype(BF16)
        dgn_scr[...] += jnp.sum(d_on * o * r, axis=0, keepdims=True)
        t = d_on * gn
        do = r * t - o * (r * r * r) * jnp.mean(t * o, axis=1, keepdims=True)
        dob = do.astype(BF16)

        lo, up, upper = _chunk_masks()
        dqp, dkn, dqn, dkp, dv = [], [], [], [], []
        for a_s, rows in zip(a, SUBS):
            da = _dot_nt(dob[rows], v[rows])
            dalo = jnp.where(lo, da, 0.0).astype(BF16)
            daup = jnp.where(up, da, 0.0).astype(BF16)
            dqp.append(_dot(dalo, knb[rows]))
            dkn.append(_dot_tn(dalo, qpb[rows]))
            dqn.append(_dot(daup, kpb[rows]))
            dkp.append(_dot_tn(daup, qnb[rows]))
            dv.append(_dot_tn(a_s.astype(BF16), dob[rows]))
        dqp, dkn, dqn, dkp, dv = (jnp.concatenate(x, axis=0) for x in (dqp, dkn, dqn, dkp, dv))

        dst = dst_scr[hd]
        dqp_c, dkn_c, dv_c, dbl_c = [None] * CPB, [None] * CPB, [None] * CPB, [None] * CPB
        for c in reversed(range(CPB)):
            rows = slice(c * CHUNK, (c + 1) * CHUNK)
            st = st_ref[hh, c]
            e_last = jnp.exp(b[(c + 1) * CHUNK - 1:(c + 1) * CHUNK, :])
            if c == CPB - 1:
                st_next = (st + _dot_tn(v[rows], knb[rows])) * e_last
            else:
                st_next = st_ref[hh, c + 1]
            dbl_c[c] = jnp.sum(dst * st_next, axis=0, keepdims=True)
            dtt = (dst * e_last).astype(BF16)
            dv_c[c] = _dot_nt(knb[rows], dtt)
            dkn_c[c] = _dot(v[rows], dtt)
            dqp_c[c] = _dot(dob[rows], st.astype(BF16))
            dst = _dot_tn(dob[rows], qpb[rows]) + dst * e_last
        dst_scr[hd] = dst
        dqp = dqp + jnp.concatenate(dqp_c, axis=0)
        dkn = dkn + jnp.concatenate(dkn_c, axis=0)
        dv = dv + jnp.concatenate(dv_c, axis=0)
        dh_ref[:, _head_cols(hh, SEG_V)] = dv.astype(BF16)
        dh_ref[:, _head_cols(hh, SEG_Q)] = ((dqp * eb + dqn * enb) * (B_DK ** -0.5)).astype(BF16)
        dh_ref[:, _head_cols(hh, SEG_K)] = (dkp * eb + dkn * enb).astype(BF16)

        last = jnp.bitwise_and(lax.broadcasted_iota(jnp.int32, (TQ, 1), 0), CHUNK - 1) == CHUNK - 1
        dbl = jnp.concatenate([jnp.broadcast_to(x, (CHUNK, LANES)) for x in dbl_c], axis=0)
        db = dqp * qp - dqn * qn + dkp * kp - dkn * kn + jnp.where(last, dbl, 0.0)
        upper_b = upper.astype(BF16)
        dlogit = _by_sub(lambda rows: _dot3(upper_b, db[rows])) * (1.0 / GATE_TAU) * (1.0 - sg)
        dlb = dlogit.astype(BF16)
        dgw_scr[hd] += _dot_tn(lr, dlb)
        dgb_scr[hd] += jnp.sum(dlogit, axis=0, keepdims=True)

        @pl.when(i == nb - 1)
        def _():
            dgw_ref[:, lane] = dgw_scr[hd]
            dgb_ref[:, lane] = dgb_scr[hd]

        return _dot_nt(dlb, gwv)

    qkvz_s, lr_s, gw_s, gb_s, gn_s, blk = _gla_specs(nb, True)
    row = pl.BlockSpec((TQ, B_HW), lambda i, p: (blk(i), p))
    dy_spec = pl.BlockSpec((TQ, B_HW), lambda i, p: (blk(i), A_WIDTH // B_HW + p))
    st_spec = pl.BlockSpec((B_HPS, CPB, LANES, LANES), lambda i, p: (p, blk(i), 0, 0))
    return pl.pallas_call(
        body,
        name="gla_bwd",
        grid=(nb, B_HEADS // B_HPS),
        in_specs=[qkvz_s, lr_s, gw_s, gb_s, gn_s, row, st_spec, dy_spec, ANY],
        out_specs=[pl.BlockSpec((TQ, 4 * B_HW), lambda i, p: (blk(i), C_B // (4 * B_HW) + p)),
                   pl.BlockSpec((LANES, B_HW), lambda i, p: (0, jnp.where(i == nb - 1, p, 0))),
                   pl.BlockSpec((1, B_HW), lambda i, p: (0, jnp.where(i == nb - 1, p, 0))),
                   pl.BlockSpec((1, LANES), lambda i, p: (0, 0))],
        out_shape=[jax.ShapeDtypeStruct(dh.shape, BF16),
                   jax.ShapeDtypeStruct((LANES, B_HEADS * LANES), F32),
                   jax.ShapeDtypeStruct((1, B_HEADS * LANES), F32),
                   jax.ShapeDtypeStruct((1, LANES), F32)],
        input_output_aliases={8: 0},
        scratch_shapes=[pltpu.VMEM((B_HEADS, LANES, LANES), F32), pltpu.VMEM((B_HEADS, LANES, LANES), F32),
                        pltpu.VMEM((B_HEADS, 1, LANES), F32), pltpu.VMEM((1, LANES), F32),
                        pltpu.VMEM((TQ, LANES), F32)],
        compiler_params=_params(("arbitrary", "arbitrary")),
    )(h, h, gw, gb, gn, o_pre, states, dycat, dh)


LN_ROWS = 512


def _resident(shape):
    return pl.BlockSpec(shape, lambda i: (0,) * len(shape), pipeline_mode=pl.Buffered(1))


def _outproj_ln(ycat, w_out, x, g, b):
    S = x.shape[0]

    def body(yc_ref, w_ref, x_ref, g_ref, b_ref, y_ref, yb_ref, yt_ref, xh_ref, rs_ref):
        u = ALPHA * x_ref[...] + _dot(yc_ref[...], w_ref[...])
        mu = jnp.mean(u, axis=1, keepdims=True)
        d = u - mu
        rstd = lax.rsqrt(jnp.mean(d * d, axis=1, keepdims=True) + LN_EPS)
        xh = d * rstd
        y = xh * g_ref[...] + b_ref[...]
        y_ref[...] = y
        yb_ref[...] = y.astype(BF16)
        yt_ref[...] = y.T.astype(BF16)
        xh_ref[...] = xh
        rs_ref[...] = rstd

    row = lambda w: pl.BlockSpec((LN_ROWS, w), lambda i: (i, 0))
    vec = pl.BlockSpec((1, D_MODEL), lambda i: (0, 0))
    return pl.pallas_call(
        body,
        name="outproj_ln",
        grid=(S // LN_ROWS,),
        in_specs=[row(D_MODEL), _resident((D_MODEL, D_MODEL)), row(D_MODEL), vec, vec],
        out_specs=[row(D_MODEL), row(D_MODEL), pl.BlockSpec((D_MODEL, LN_ROWS), lambda i: (0, i)), row(D_MODEL), row(1)],
        out_shape=[jax.ShapeDtypeStruct((S, D_MODEL), F32), jax.ShapeDtypeStruct((S, D_MODEL), BF16),
                   jax.ShapeDtypeStruct((D_MODEL, S), BF16),
                   jax.ShapeDtypeStruct((S, D_MODEL), F32), jax.ShapeDtypeStruct((S, 1), F32)],
        compiler_params=_params(("parallel",)),
    )(ycat, w_out, x, g, b)


def _ln_bwd_dycat(dy, xhat, rstd, g, w_out, rider=None):
    S = dy.shape[0]
    n = S // LN_ROWS
    hop, srcs, bufs = rider if rider is not None else (None, [], [])

    def body(*refs):
        dy_ref, xh_ref, rs_ref, g_ref, w_ref = refs[:5]
        src_refs = refs[5:5 + len(srcs)]
        outs = refs[5 + len(srcs) + len(bufs):]
        du_ref, dub_ref, dyc_ref, dg_ref, db_ref = outs[:5]
        buf_refs, sems = outs[5:5 + len(bufs)], outs[5 + len(bufs):]
        i = pl.program_id(0)
        if hop is not None:
            @pl.when(i == 0)
            def _():
                hop.start(src_refs, buf_refs, *sems)

        dy_, xh = dy_ref[...], xh_ref[...]
        dyg = dy_ * g_ref[...]
        m1 = jnp.mean(dyg, axis=1, keepdims=True)
        m2 = jnp.mean(dyg * xh, axis=1, keepdims=True)
        du = rs_ref[...] * (dyg - m1 - xh * m2)
        dub = du.astype(BF16)
        du_ref[...] = du
        dub_ref[...] = dub
        dyc_ref[...] = _dot_nt(dub, w_ref[...]).astype(BF16)
        dg = jnp.sum(dy_ * xh, axis=0, keepdims=True)
        db = jnp.sum(dy_, axis=0, keepdims=True)

        @pl.when(i == 0)
        def _():
            dg_ref[...] = dg
            db_ref[...] = db

        @pl.when(i > 0)
        def _():
            dg_ref[...] += dg
            db_ref[...] += db

        if hop is not None:
            @pl.when(i == n - 1)
            def _():
                hop.wait(src_refs, buf_refs, *sems)

    row = lambda w: pl.BlockSpec((LN_ROWS, w), lambda i: (i, 0))
    vec = pl.BlockSpec((1, D_MODEL), lambda i: (0, 0))
    n_in = 5 + len(srcs)
    out = pl.pallas_call(
        body,
        name="ln_bwd_dycat",
        grid=(n,),
        in_specs=[row(D_MODEL), row(D_MODEL), row(1), vec, pl.BlockSpec((D_MODEL, D_MODEL), lambda i: (0, 0))]
        + [ANY] * (len(srcs) + len(bufs)),
        out_specs=[row(D_MODEL), row(D_MODEL), row(D_MODEL), vec, vec] + [ANY] * len(bufs),
        out_shape=[jax.ShapeDtypeStruct((S, D_MODEL), F32), jax.ShapeDtypeStruct((S, D_MODEL), BF16),
                   jax.ShapeDtypeStruct((S, D_MODEL), BF16),
                   jax.ShapeDtypeStruct((1, D_MODEL), F32), jax.ShapeDtypeStruct((1, D_MODEL), F32)]
        + [jax.ShapeDtypeStruct(b.shape, b.dtype) for b in bufs],
        input_output_aliases={n_in + t: 5 + t for t in range(len(bufs))},
        scratch_shapes=hop.sems() if hop is not None else [],
        compiler_params=_params(("arbitrary",)),
    )(dy, xhat, rstd, g, w_out, *srcs, *bufs)
    return tuple(out[:5]) if hop is None else (tuple(out[:5]), list(out[5:]))


def _loss_head(y, target):
    S = y.shape[0]

    def body(y_ref, t_ref, l_ref, dy_ref):
        i = pl.program_id(0)
        err = y_ref[...] - t_ref[...]
        dy_ref[...] = err * (1.0 / D_MODEL)
        part = (0.5 / D_MODEL) * jnp.sum(jnp.sum(err * err, axis=1, keepdims=True), axis=0, keepdims=True)

        @pl.when(i == 0)
        def _():
            l_ref[...] = part

        @pl.when(i > 0)
        def _():
            l_ref[...] += part

    row = pl.BlockSpec((TQ, D_MODEL), lambda i: (i, 0))
    return pl.pallas_call(
        body,
        name="loss_head",
        grid=(S // TQ,),
        in_specs=[row, row],
        out_specs=[pl.BlockSpec((1, 1), lambda i: (0, 0)), row],
        out_shape=[jax.ShapeDtypeStruct((1, 1), F32), jax.ShapeDtypeStruct((S, D_MODEL), F32)],
        compiler_params=_params(("arbitrary",)),
    )(y, target)


def _pad_gate(gate_w, gate_b):
    gw = gate_w.reshape(GATE_RANK, B_HEADS, B_DK)
    gw = jnp.pad(gw, ((LR_LANE, LANES - LR_LANE - GATE_RANK), (0, 0), (0, LANES - B_DK))).reshape(LANES, B_HEADS * LANES)
    gb = jnp.pad(gate_b.reshape(B_HEADS, B_DK), ((0, 0), (0, LANES - B_DK))).reshape(1, B_HEADS * LANES)
    return gw.astype(BF16), gb.astype(F32)


def _layer_fwd(x, xb, xt, mem_b, w_in, w_kv, w_out, u, gw, gb, gn, ln_g, ln_b, rider=None):
    h = _matmul(xb, w_in, mode="nn", out_dtype=BF16, tm=1024, tn=1792, tk=D_MODEL, name="in_proj", rider=rider)
    if rider is not None:
        h, rode = h
    mkv = _matmul(mem_b, w_kv, mode="nn", out_dtype=BF16, tm=mem_b.shape[0], tn=1024, tk=D_MODEL, name="mem_kv")
    ycat, ycat_t = _band_fwd(h, u)
    ycat, ycat_t, o_pre, states = _gla_fwd(h, gw, gb, gn, ycat, ycat_t)
    ycat, ycat_t = _mem_fwd(h, mkv, ycat, ycat_t)
    y, ybf, yt, xhat, rstd = _outproj_ln(ycat, w_out, x, ln_g, ln_b)
    saved = (xt, h, mkv, ycat_t, o_pre, states, xhat, rstd)
    return (y, ybf, yt, saved) if rider is None else (y, ybf, yt, saved, rode)


def _layer_bwd(dy, saved, mem_b, w_in, w_out, u, gw, gb, gn, ln_g, reduce=None, own_reduce=None):
    xt, h, mkv, ycat_t, o_pre, states, xhat, rstd = saved

    def riding(**kw):
        if reduce is None:
            return _matmul(**kw)
        out, bufs = _matmul(rider=reduce.rider(), **kw)
        reduce.landed(bufs)
        return out

    if reduce is None:
        du, dub, dycat, d_ln_g, d_ln_b = _ln_bwd_dycat(dy, xhat, rstd, ln_g, w_out)
    else:
        (du, dub, dycat, d_ln_g, d_ln_b), bufs = _ln_bwd_dycat(dy, xhat, rstd, ln_g, w_out, rider=reduce.rider())
        reduce.landed(bufs)
    d_w_out = _matmul(ycat_t, dub, mode="nn", out_dtype=F32, tm=1024, tn=1024, tk=min(4096, dub.shape[0]), name="d_w_out")
    dh, d_u = _band_bwd(h, u, dycat)
    dh, dgw, dgb, dgn = _gla_bwd(h, gw, gb, gn, o_pre, states, dycat, dh)
    dh, dmkv = _mem_bwd(h, mkv, dycat, dh)
    d_w_kv = _matmul(mem_b, dmkv, mode="tn", out_dtype=F32, tm=1024, tn=1024, tk=mem_b.shape[0], name="d_w_kv")
    dx_args = dict(a=dh, b=w_in, mode="nt", out_dtype=F32, tm=1024, tn=1024, tk=3584, name="dx", add=du, add_scale=ALPHA)
    dw_args = dict(a=xt, b=dh, mode="nn", out_dtype=F32, tm=1024, tn=1024, tk=min(4096, dh.shape[0]), name="d_w_in")
    if own_reduce is None:
        dx = riding(**dx_args)
        d_w_in = riding(**dw_args)
        return dx, (d_w_in, d_u, dgw, dgb, dgn, d_w_kv, d_w_out, d_ln_g, d_ln_b)
    d_w_in = riding(**dw_args)
    grads = (d_w_in, d_u, dgw, dgb, dgn, d_w_kv, d_w_out, d_ln_g, d_ln_b)
    own = own_reduce(grads, reduce.finish() if reduce is not None else None)
    own.step()
    dx, bufs = _matmul(rider=own.rider(), **dx_args)
    own.landed(bufs)
    return dx, grads, own


def _unpad_heads(w):
    r = w.shape[0]
    return w.reshape(r, B_HEADS, LANES)[:, :, :B_DK].reshape(r, B_KEY_WIDTH)


def _padded_col_of():
    col, o = np.zeros(IN_WIDTH, np.int64), 0
    for seg in (SEG_Q, SEG_K, SEG_V, SEG_Z):
        for hd in range(A_HEADS):
            col[o:o + LANES] = C_A + (hd // A_HPS) * 4 * A_HW + seg * A_HW + (hd % A_HPS) * LANES + np.arange(LANES)
            o += LANES
    for seg, width in ((SEG_Q, B_DK), (SEG_K, B_DK), (SEG_V, LANES), (SEG_Z, LANES)):
        for hd in range(B_HEADS):
            col[o:o + width] = C_B + hd * 4 * LANES + seg * LANES + np.arange(width)
            o += width
    col[o:o + GATE_RANK] = C_LR + LR_LANE + np.arange(GATE_RANK)
    o += GATE_RANK
    for seg in (0, 1):
        for hd in range(M_HEADS):
            col[o:o + LANES] = C_M + hd * 2 * LANES + seg * LANES + np.arange(LANES)
            o += LANES
    assert o == IN_WIDTH
    return col


def _runs(idx):
    out, start = [], 0
    for k in range(1, len(idx) + 1):
        if k == len(idx) or idx[k] != idx[k - 1] + 1:
            out.append((int(idx[start]), k - start))
            start = k
    return out


def _chip_columns(g, j, n):
    runs = _runs(_padded_col_of()[j * n:(j + 1) * n])
    return jnp.concatenate([g[:, a:a + ln] for a, ln in runs], axis=1)


def _padded_from_shards(shards):
    n = shards[0].shape[1]
    src = np.full(HP, -1, np.int64)
    src[_padded_col_of()] = np.arange(IN_WIDTH)
    parts, k = [], 0
    while k < HP:
        e = k + 1
        if src[k] < 0:
            while e < HP and src[e] < 0:
                e += 1
            parts.append(jnp.zeros((shards[0].shape[0], e - k), shards[0].dtype))
        else:
            while e < HP and src[e] == src[e - 1] + 1 and src[e] // n == src[k] // n:
                e += 1
            parts.append(shards[src[k] // n][:, src[k] % n:src[k] % n + e - k])
        k = e
    return jnp.concatenate(parts, axis=1)


ADAMW_BLOCK_BYTES = 1 << 20


def _adamw(w, g, m, v, name):
    L, R, C = w.shape
    tl, tr = 1, R
    if R * C * 4 <= ADAMW_BLOCK_BYTES:
        tl = max(d for d in range(1, L + 1) if L % d == 0 and d * R * C * 4 <= ADAMW_BLOCK_BYTES)
    else:
        for cand in (256, 128, 64, 32, 16, 8):
            if R % cand == 0 and R > cand:
                tr = cand
                break

    def body(w_ref, g_ref, m_ref, v_ref, d_ref, nm_ref, nv_ref):
        g_ = g_ref[...]
        nm = ADAM_B1 * m_ref[...] + (1.0 - ADAM_B1) * g_
        nv = ADAM_B2 * v_ref[...] + (1.0 - ADAM_B2) * (g_ * g_)
        m_hat = nm / (1.0 - ADAM_B1 ** ADAM_STEP)
        v_hat = nv / (1.0 - ADAM_B2 ** ADAM_STEP)
        d_ref[...] = -ADAM_LR * (m_hat / (jnp.sqrt(v_hat) + ADAM_EPS) + ADAM_WD * w_ref[...])
        nm_ref[...] = nm
        nv_ref[...] = nv

    spec = pl.BlockSpec((tl, tr, C), lambda l, i: (l, i, 0))
    sd = jax.ShapeDtypeStruct((L, R, C), F32)
    return pl.pallas_call(
        body, name=name, grid=(L // tl, R // tr), in_specs=[spec] * 4, out_specs=[spec] * 3, out_shape=[sd] * 3,
        compiler_params=_params(("parallel", "parallel")),
    )(w, g, m, v)


def _adamw_nd(w, g, m, v, name):
    shape = w.shape
    f = (lambda a: a) if w.ndim == 3 else (lambda a: a.reshape(1, shape[0], shape[1]))
    return tuple(o.reshape(shape) for o in _adamw(f(w), f(g), f(m), f(v), name))


ANY = pl.BlockSpec(memory_space=pl.ANY)


def _place():
    x, y, c = lax.axis_index("x"), lax.axis_index("y"), lax.axis_index("c")
    chips = [(1 - x, y), (x, 1 - y), (1 - x, 1 - y)]
    return x, y, c, chips


class _WeightGather:
    def __init__(self, hop, layer, rows):
        self.hop, self.layer, self.rows = hop, layer, rows
        self.n_sem = 3 * len(rows)

    def _copies(self, shard_refs, buf_refs, send, recv, received):
        x, y, c, chips = _place()
        out = []
        for t, R in enumerate(self.rows):
            half = R // 2
            assert half % 16 == 0
            mine = pl.ds(pl.multiple_of(c * half, 16), half)
            other = pl.ds(pl.multiple_of((1 - c) * half, 16), half)
            mine_of_shard = pl.ds(pl.multiple_of(self.layer * R + c * half, 16), half)
            for k, chip in enumerate(chips):
                theirs = buf_refs[t].at[2 * chip[0] + chip[1]]
                if self.hop == "chips":
                    src, dst, to = shard_refs[t].at[mine_of_shard], buf_refs[t].at[2 * x + y, mine], (*chip, c)
                    landed = theirs.at[mine]
                else:
                    src, dst, to = theirs.at[mine], theirs.at[mine], (x, y, 1 - c)
                    landed = theirs.at[other]
                out.append(pltpu.make_async_remote_copy(
                    src_ref=src, dst_ref=landed if received else dst, send_sem=send.at[3 * t + k],
                    recv_sem=recv.at[3 * t + k], device_id=to, device_id_type=MESH))
        return out

    def start(self, shard_refs, buf_refs, send, recv):
        for cp in self._copies(shard_refs, buf_refs, send, recv, False):
            cp.start()

    def wait(self, shard_refs, buf_refs, send, recv):
        for cp in self._copies(shard_refs, buf_refs, send, recv, True):
            cp.wait_recv()
        for cp in self._copies(shard_refs, buf_refs, send, recv, False):
            cp.wait_send()

    def sems(self):
        return [pltpu.SemaphoreType.DMA((self.n_sem,)), pltpu.SemaphoreType.DMA((self.n_sem,))]

    def call(self, srcs, bufs, name):
        ns, nb = len(srcs), len(bufs)

        def body(*refs):
            src_refs, buf_refs, (send, recv) = refs[:ns], refs[ns + nb:ns + 2 * nb], refs[ns + 2 * nb:]
            self.start(src_refs, buf_refs, send, recv)
            self.wait(src_refs, buf_refs, send, recv)

        return pl.pallas_call(
            body, name=name, in_specs=[ANY] * (ns + nb), out_specs=[ANY] * nb,
            out_shape=[jax.ShapeDtypeStruct(b.shape, b.dtype) for b in bufs],
            input_output_aliases={ns + t: t for t in range(nb)},
            scratch_shapes=self.sems(),
        )(*srcs, *bufs)


class _GradHop(_WeightGather):
    def __init__(self, hop, layer, slices):
        self.hop, self.layer, self.slices = hop, layer, slices
        self.n_sem = {"pair": N_CHIPS, "chips": N_CHIPS - 1, "gather": 1}[hop] * len(slices)

    def _copies(self, src_refs, buf_refs, send, recv, received):
        x, y, c, chips = _place()
        me, out = 2 * x + y, []

        def remote(src, dst, to):
            k = len(out)
            out.append(pltpu.make_async_remote_copy(src_ref=src, dst_ref=dst, send_sem=send.at[k], recv_sem=recv.at[k],
                                                    device_id=to, device_id_type=MESH))

        for t, (half, where) in enumerate(self.slices):
            if self.hop == "pair":
                for j, (a, first) in enumerate(where):
                    rows = pl.ds(pl.multiple_of(first + (1 - c) * half, 8), half)
                    remote(src_refs[a].at[rows], buf_refs[t].at[j], (x, y, 1 - c))
            elif self.hop == "chips":
                for chip in chips:
                    slot = 2 * chip[0] + chip[1]
                    remote(src_refs[t].at[slot], buf_refs[t].at[slot if received else me], (*chip, c))
            else:
                mine = buf_refs[t].at[self.layer, c]
                remote(mine, buf_refs[t].at[self.layer, 1 - c] if received else mine, (x, y, 1 - c))
        return out


def _add_halves(parts, got, c_idx, name):
    n, L, half, C = got.shape
    tr = 64

    def body(*refs):
        ins, (got_ref, o_ref) = refs[1:1 + len(parts)], refs[1 + len(parts):]
        for k in range(len(parts)):
            o_ref[k // L, k % L] = (ins[k][...] + got_ref[k // L, k % L]).astype(BF16)

    def rows_of(first):
        assert first % tr == 0 and half % tr == 0
        return lambda i, c: (first // tr + c[0] * (half // tr) + i, 0)

    whole = pl.BlockSpec((n, L, tr, C), lambda i, c: (0, 0, i, 0))
    return pl.pallas_call(
        body, name=name,
        grid_spec=pltpu.PrefetchScalarGridSpec(
            num_scalar_prefetch=1, grid=(half // tr,),
            in_specs=[pl.BlockSpec((tr, C), rows_of(first)) for _, first in parts] + [whole],
            out_specs=whole),
        out_shape=jax.ShapeDtypeStruct((n, L, half, C), BF16),
        compiler_params=_params(("parallel",)),
    )(c_idx, *[a for a, _ in parts], got)


def _add_slots(r, c_idx, dest, layer, name):
    n, half, C = r.shape
    tr = 256

    def body(c_ref, r_ref, dest_ref, o_ref):
        del dest_ref
        acc = r_ref[0].astype(F32)
        for j in range(1, n):
            acc = acc + r_ref[j].astype(F32)
        o_ref[0, 0] = acc

    return pl.pallas_call(
        body, name=name,
        grid_spec=pltpu.PrefetchScalarGridSpec(
            num_scalar_prefetch=1, grid=(half // tr,),
            in_specs=[pl.BlockSpec((n, tr, C), lambda i, c: (0, i, 0)), ANY],
            out_specs=pl.BlockSpec((1, 1, tr, C), lambda i, c: (layer, c[0], i, 0))),
        out_shape=jax.ShapeDtypeStruct(dest.shape, F32),
        input_output_aliases={2: 0},
        compiler_params=_params(("parallel",)),
    )(c_idx, r, dest)


class _LayerReduce:
    def __init__(self, layer, grads, slices, c_idx, chip, dests):
        self.layer, self.grads, self.slices, self.c_idx, self.chip, self.dests = layer, grads, slices, c_idx, chip, dests
        self.widths = [grads[where[0][0]].shape[1] for _, where in slices]
        self.stage = 0

    def _hop(self, kind):
        return _GradHop(kind, self.layer, self.slices)

    def rider(self):
        if self.stage == 0:
            got = [lax.empty((N_CHIPS, half, w), F32) for (half, _), w in zip(self.slices, self.widths)]
            return self._hop("pair"), self.grads, got
        if self.stage == 1:
            q = [lax.empty(p.shape, BF16) for p in self.pair_sums]
            return self._hop("chips"), self.pair_sums, q
        return self._hop("gather"), [], self.dests

    def landed(self, bufs):
        tag = f"{self.layer}"
        if self.stage == 0:
            self.pair_sums = []
            for t, ((half, where), got) in enumerate(zip(self.slices, bufs)):
                parts = [(self.grads[a], first) for a, first in where]
                p = _add_halves(parts, got[:, None], self.c_idx, f"rs_add2_{t}_{tag}")
                self.pair_sums.append(p.reshape(N_CHIPS, half, p.shape[-1]))
        elif self.stage == 1:
            for t, (q, p) in enumerate(zip(bufs, self.pair_sums)):
                q = lax.dynamic_update_slice_in_dim(q, lax.dynamic_slice_in_dim(p, self.chip, 1, axis=0), self.chip, axis=0)
                self.dests[t] = _add_slots(q, self.c_idx, self.dests[t], self.layer, f"rs_add4_{t}_{tag}")
        else:
            self.dests = list(bufs)
        self.stage += 1

    def step(self):
        hop, srcs, bufs = self.rider()
        self.landed(hop.call(srcs, bufs, f"rs_{hop.hop}_{self.layer}"))

    def finish(self):
        while self.stage < 3:
            self.step()
        return self.dests


def _all_reduce_small(buf, name):
    R = buf.shape[0]

    def flipped(k, x, y, c):
        return ((1 - x) if k & 4 else x, (1 - y) if k & 2 else y, (1 - c) if k & 1 else c)

    def body(b_ref, o_ref, land, send, recv):
        x, y, c, _ = _place()
        me = 4 * x + 2 * y + c
        land[me] = b_ref[...]
        cps = []
        for k in range(1, N_DEV):
            peer = flipped(k, x, y, c)
            cps.append(pltpu.make_async_remote_copy(src_ref=b_ref, dst_ref=land.at[me], send_sem=send.at[k - 1],
                                                    recv_sem=recv.at[k - 1], device_id=peer, device_id_type=MESH))
        for cp in cps:
            cp.start()
        for k in range(1, N_DEV):
            peer = flipped(k, x, y, c)
            slot = 4 * peer[0] + 2 * peer[1] + peer[2]
            pltpu.make_async_remote_copy(src_ref=b_ref, dst_ref=land.at[slot], send_sem=send.at[k - 1],
                                         recv_sem=recv.at[k - 1], device_id=peer, device_id_type=MESH).wait_recv()
        for cp in cps:
            cp.wait_send()
        acc = land[0]
        for j in range(1, N_DEV):
            acc = acc + land[j]
        o_ref[...] = acc

    vm = pl.BlockSpec(memory_space=pltpu.VMEM)
    return pl.pallas_call(
        body, name=name, in_specs=[vm], out_specs=vm,
        out_shape=jax.ShapeDtypeStruct((R, LANES), F32),
        scratch_shapes=[pltpu.VMEM((N_DEV, R, LANES), F32), pltpu.SemaphoreType.DMA((N_DEV - 1,)),
                        pltpu.SemaphoreType.DMA((N_DEV - 1,))],
    )(buf)


def kernel(x, mem, w_in, a_rel_bias, b_gate_w, b_gate_b, b_norm_g, w_mem_kv, w_out, ln_g, ln_b, loss_target, m_w_in, m_a_rel_bias, m_b_gate_w, m_b_gate_b, m_b_norm_g, m_w_mem_kv, m_w_out, m_ln_g, m_ln_b, v_w_in, v_a_rel_bias, v_b_gate_w, v_b_gate_b, v_b_norm_g, v_w_mem_kv, v_w_out, v_ln_g, v_ln_b):
    L = w_in.shape[0]
    S = x.shape[1]
    cx, cy, cc = lax.axis_index("x"), lax.axis_index("y"), lax.axis_index("c")
    chip = 2 * cx + cy
    c_idx = jnp.reshape(cc, (1,)).astype(jnp.int32)

    n_in, r_kv, r_out = w_in.shape[2], w_mem_kv.shape[1], w_out.shape[1]
    shards = [w.astype(BF16).reshape(-1, w.shape[2]) for w in (w_in, w_mem_kv, w_out)]
    rows = [D_MODEL, r_kv, r_out]

    def landing(l):
        return [lax.dynamic_update_slice_in_dim(lax.empty((N_CHIPS, r, s.shape[1]), BF16),
                                                s[l * r:(l + 1) * r][None], chip, axis=0)
                for s, r in zip(shards, rows)]

    def assembled(bufs):
        return (_padded_from_shards([bufs[0][j] for j in range(N_CHIPS)]),
                bufs[1].reshape(D_MODEL, bufs[1].shape[2]), bufs[2].reshape(D_MODEL, D_MODEL))

    bufs0 = _WeightGather("chips", 0, rows).call(shards, landing(0), "gather_chips_0")
    weights = [assembled(_WeightGather("pair", 0, rows).call(shards, bufs0, "gather_pair_0"))]

    gw_cols = b_gate_w.shape[2]
    gw_slot = jnp.zeros((N_CHIPS, L, GATE_RANK, gw_cols), F32)
    gw_slot = lax.dynamic_update_slice(gw_slot, (0.5 * b_gate_w)[None], (chip, 0, 0, 0))
    gw_flat = gw_slot.reshape(-1)
    n_gw = gw_flat.shape[0]
    pad = (-n_gw) % (8 * LANES)
    gw_full = _all_reduce_small(jnp.pad(gw_flat, (0, pad)).reshape(-1, LANES), "gather_gate_w").reshape(-1)[:n_gw]
    gw_full = gw_full.reshape(N_CHIPS, L, GATE_RANK, gw_cols).transpose(1, 2, 0, 3).reshape(L, GATE_RANK, B_KEY_WIDTH)

    xs = x.reshape(S, D_MODEL)
    mem_b = mem.reshape(mem.shape[1], D_MODEL).astype(BF16)
    target = loss_target.reshape(S, D_MODEL)

    small_w = []
    for l in range(L):
        gw_l, gb_l = _pad_gate(gw_full[l], b_gate_b[l])
        small_w.append((_bias_by_offset(a_rel_bias[l]), gw_l, gb_l,
                        b_norm_g[l].reshape(1, LANES), ln_g[l].reshape(1, D_MODEL), ln_b[l].reshape(1, D_MODEL)))

    y, yb = xs, xs.astype(BF16)
    yt = _transpose(yb, "x_t")
    saved = []
    for l in range(L):
        if l + 1 < L:
            rider = (_WeightGather("chips", l + 1, rows), shards, landing(l + 1))
            y, yb, yt, sv, bufs = _layer_fwd(y, yb, yt, mem_b, *weights[l], *small_w[l], rider=rider)
            weights.append(assembled(_WeightGather("pair", l + 1, rows).call(shards, bufs, f"gather_pair_{l + 1}")))
        else:
            y, yb, yt, sv = _layer_fwd(y, yb, yt, mem_b, *weights[l], *small_w[l])
        saved.append(sv)
    layer_w = [(*weights[l], *small_w[l]) for l in range(L)]
    loss_part, dy = _loss_head(y, target)

    halves = [D_MODEL // 2, r_kv // 2, r_out // 2]
    dests = [lax.empty((L, 2, hf, w.shape[2]), F32) for hf, w in zip(halves, (w_in, w_mem_kv, w_out))]
    slices = [(halves[0], [(j, 0) for j in range(N_CHIPS)]),
              (halves[1], [(N_CHIPS, j * r_kv) for j in range(N_CHIPS)]),
              (halves[2], [(N_CHIPS + 1, j * r_out) for j in range(N_CHIPS)])]

    def reduction(l, g, into):
        arrays = [_chip_columns(g[0], j, n_in) for j in range(N_CHIPS)] + [g[5], g[6]]
        return _LayerReduce(l, arrays, slices, c_idx, chip, into)

    grads, reduce = [None] * L, None
    for l in reversed(range(L)):
        w_in_l, w_kv_l, w_out_l, u_l, gw_l, gb_l, gn_l, lg_l, lb_l = layer_w[l]
        args = (dy, saved[l], mem_b, w_in_l, w_out_l, u_l, gw_l, gb_l, gn_l, lg_l)
        if l > 0:
            dy, grads[l] = _layer_bwd(*args, reduce=reduce)
            if reduce is not None:
                dests = reduce.finish()
            reduce = reduction(l, grads[l], dests)
        else:
            own = lambda g, above: reduction(0, g, dests if above is None else above)
            dy, grads[l], reduce = _layer_bwd(*args, reduce=reduce, own_reduce=own)
    r_w_in, r_w_kv, r_w_out = [d.reshape(L, 2 * d.shape[2], d.shape[3]) for d in reduce.finish()]
    grad_x = dy.reshape(x.shape)

    g_rel = jnp.stack([_bias_grad_from_offset(g[1]) for g in grads])
    g_gw = jnp.stack([_unpad_heads(g[2][LR_LANE:LR_LANE + GATE_RANK]) for g in grads])
    g_gb = jnp.stack([_unpad_heads(g[3])[0] for g in grads])
    g_gn = jnp.stack([g[4][0] for g in grads])
    g_lg = jnp.stack([g[7][0] for g in grads])
    g_lb = jnp.stack([g[8][0] for g in grads])

    small = [g_rel, g_gw, g_gb, g_gn, g_lg, g_lb, loss_part]
    flat = jnp.concatenate([s.reshape(-1) for s in small])
    n_small = flat.shape[0]
    pad = (-n_small) % (8 * LANES)
    red = _all_reduce_small(jnp.pad(flat, (0, pad)).reshape(-1, LANES), "all_reduce_small").reshape(-1)
    outs, off = [], 0
    for s in small:
        outs.append(red[off:off + s.size].reshape(s.shape))
        off += s.size
    g_rel, g_gw, g_gb, g_gn, g_lg, g_lb, loss = outs
    loss = loss.reshape(())
    g_gw = lax.dynamic_slice_in_dim(g_gw.reshape(L, GATE_RANK, N_CHIPS, gw_cols), chip, 1, axis=2).reshape(L, GATE_RANK, gw_cols)

    g_list = [r_w_in, g_rel, g_gw, g_gb, g_gn, r_w_kv, r_w_out, g_lg, g_lb]
    w_list = [w_in, a_rel_bias, b_gate_w, b_gate_b, b_norm_g, w_mem_kv, w_out, ln_g, ln_b]
    m_list = [m_w_in, m_a_rel_bias, m_b_gate_w, m_b_gate_b, m_b_norm_g, m_w_mem_kv, m_w_out, m_ln_g, m_ln_b]
    v_list = [v_w_in, v_a_rel_bias, v_b_gate_w, v_b_gate_b, v_b_norm_g, v_w_mem_kv, v_w_out, v_ln_g, v_ln_b]
    names = ["w_in", "rel", "gate_w", "gate_b", "norm_g", "w_kv", "w_out", "ln_g", "ln_b"]
    to_cols = lambda a: jnp.transpose(a, (2, 0, 1))
    upd = [tuple(jnp.transpose(o, (1, 2, 0)) for o in
                 _adamw(to_cols(w_in), to_cols(r_w_in), to_cols(m_w_in), to_cols(v_w_in), "adamw_w_in"))]
    upd += [_adamw_nd(w, g, m, v, "adamw_" + n)
            for w, g, m, v, n in list(zip(w_list, g_list, m_list, v_list, names))[1:]]
    deltas = [u_[0] for u_ in upd]
    new_m = [u_[1] for u_ in upd]
    new_v = [u_[2] for u_ in upd]
    return (loss, grad_x, *g_list, *deltas, *new_m, *new_v)
```

```python
import functools

import numpy as np
import jax
import jax.numpy as jnp
from jax import lax
from jax.experimental import pallas as pl
from jax.experimental.pallas import tpu as pltpu

F32 = jnp.float32
BF16 = jnp.bfloat16
MESH = pl.DeviceIdType.MESH

D_MODEL = 2048
DEPTH = 4
CHUNK = 64
LEFT_CHUNKS = 8
MAX_REL = 128
N_REL = 2 * MAX_REL + 1
A_HEADS = 8
HEAD_DIM = 128
B_HEADS = 4
B_DK = 64
M_HEADS = 4
GATE_RANK = 16
GATE_TAU = 16.0
A_WIDTH = A_HEADS * HEAD_DIM
B_WIDTH = B_HEADS * HEAD_DIM
B_KEY_WIDTH = B_HEADS * B_DK
M_WIDTH = M_HEADS * HEAD_DIM
IN_WIDTH = 4 * A_WIDTH + 2 * B_KEY_WIDTH + 2 * B_WIDTH + GATE_RANK + 2 * M_WIDTH
ALPHA = (2.0 * DEPTH) ** 0.25
LN_EPS = 1e-5
RMS_EPS = 1e-6
NEG_INF = -1e30
ADAM_LR = 0.001
ADAM_B1 = 0.9
ADAM_B2 = 0.999
ADAM_EPS = 1e-08
ADAM_WD = 0.01
ADAM_STEP = 10

LANES = 128
VMEM_LIMIT = 56 * 1024 * 1024

C_A, C_B, C_M = 0, 4096, 6144
HP = 7168
LR_HEAD, LR_LANE = B_HEADS - 1, B_DK
C_LR = C_B + LR_HEAD * 4 * LANES
SEG_Q, SEG_K, SEG_V, SEG_Z = 0, 1, 2, 3
TQ = 512
CPB = TQ // CHUNK
N_CHIPS = 4
N_DEV = 8


def _params(sem, vmem=VMEM_LIMIT):
    return pltpu.CompilerParams(dimension_semantics=sem, vmem_limit_bytes=vmem)


def _dot(a, b):
    return jnp.dot(a, b, preferred_element_type=F32)


def _dot_nt(a, b):
    return lax.dot_general(a, b, (((1,), (1,)), ((), ())), preferred_element_type=F32)


def _dot_tn(a, b):
    return lax.dot_general(a, b, (((0,), (0,)), ((), ())), preferred_element_type=F32)


def _sigmoid(x):
    return 1.0 / (1.0 + jnp.exp(-x))


def _split3(x):
    hi = x.astype(BF16)
    r = x - hi.astype(F32)
    mid = r.astype(BF16)
    lo = (r - mid.astype(F32)).astype(BF16)
    return hi, mid, lo


def _dot3(m_bf, x):
    hi, mid, lo = _split3(x)
    return _dot(m_bf, hi) + _dot(m_bf, mid) + _dot(m_bf, lo)


def _matmul(a, b, *, mode, out_dtype, tm, tn, tk, name, add=None, add_scale=1.0, rider=None):
    if mode == "nn":
        (M, K), (K2, N) = a.shape, b.shape
        a_spec = pl.BlockSpec((tm, tk), lambda i, j, k: (i, k))
        b_spec = pl.BlockSpec((tk, tn), lambda i, j, k: (k, j))
        dot = _dot
    elif mode == "nt":
        (M, K), (N, K2) = a.shape, b.shape
        a_spec = pl.BlockSpec((tm, tk), lambda i, j, k: (i, k))
        b_spec = pl.BlockSpec((tn, tk), lambda i, j, k: (j, k))
        dot = _dot_nt
    else:
        (K, M), (K2, N) = a.shape, b.shape
        a_spec = pl.BlockSpec((tk, tm), lambda i, j, k: (k, i))
        b_spec = pl.BlockSpec((tk, tn), lambda i, j, k: (k, j))
        dot = _dot_tn
    assert K == K2 and M % tm == 0 and N % tn == 0 and K % tk == 0, (a.shape, b.shape, mode)
    nk = K // tk
    has_add = add is not None
    assert nk == 1 or out_dtype == F32
    grid = (M // tm, N // tn, nk)
    hop, srcs, bufs = rider if rider is not None else (None, [], [])
    n_in = 2 + has_add

    def body(*refs):
        a_ref, b_ref = refs[:2]
        add_ref = refs[2] if has_add else None
        src_refs = refs[n_in:n_in + len(srcs)]
        o_ref = refs[n_in + len(srcs) + len(bufs)]
        buf_refs = refs[n_in + len(srcs) + len(bufs) + 1:n_in + len(srcs) + 2 * len(bufs) + 1]
        sems = refs[n_in + len(srcs) + 2 * len(bufs) + 1:]
        i, j, k = pl.program_id(0), pl.program_id(1), pl.program_id(2)
        if hop is not None:
            @pl.when(jnp.logical_and(jnp.logical_and(i == 0, j == 0), k == 0))
            def _():
                hop.start(src_refs, buf_refs, *sems)

        part = dot(a_ref[...].astype(BF16), b_ref[...].astype(BF16))

        @pl.when(k == 0)
        def _():
            first = part + add_scale * add_ref[...] if has_add else part
            o_ref[...] = first.astype(out_dtype)

        if nk > 1:
            @pl.when(k > 0)
            def _():
                o_ref[...] += part

        if hop is not None:
            @pl.when(jnp.logical_and(jnp.logical_and(i == grid[0] - 1, j == grid[1] - 1), k == nk - 1))
            def _():
                hop.wait(src_refs, buf_refs, *sems)

    in_specs = [a_spec, b_spec]
    args = [a, b]
    if has_add:
        in_specs.append(pl.BlockSpec((tm, tn), lambda i, j, k: (i, j)))
        args.append(add)
    out = pl.pallas_call(
        body,
        name=name,
        grid=grid,
        in_specs=in_specs + [ANY] * (len(srcs) + len(bufs)),
        out_specs=[pl.BlockSpec((tm, tn), lambda i, j, k: (i, j))] + [ANY] * len(bufs),
        out_shape=[jax.ShapeDtypeStruct((M, N), out_dtype)] + [jax.ShapeDtypeStruct(x.shape, x.dtype) for x in bufs],
        input_output_aliases={n_in + len(srcs) + t: 1 + t for t in range(len(bufs))},
        scratch_shapes=hop.sems() if hop is not None else [],
        compiler_params=_params(("parallel", "parallel", "arbitrary") if hop is None
                                else ("arbitrary", "arbitrary", "arbitrary")),
    )(*args, *srcs, *bufs)
    return out[0] if hop is None else (out[0], list(out[1:]))


def _cast_transpose(a, name):
    R, C = a.shape
    t = 512

    def body(a_ref, b_ref, t_ref):
        b = a_ref[...].astype(BF16)
        b_ref[...] = b
        t_ref[...] = b.T

    return pl.pallas_call(
        body, name=name, grid=(R // t, C // t),
        in_specs=[pl.BlockSpec((t, t), lambda i, j: (i, j))],
        out_specs=[pl.BlockSpec((t, t), lambda i, j: (i, j)), pl.BlockSpec((t, t), lambda i, j: (j, i))],
        out_shape=[jax.ShapeDtypeStruct((R, C), BF16), jax.ShapeDtypeStruct((C, R), BF16)],
        compiler_params=_params(("parallel", "parallel")),
    )(a)


def _chunk_of(rows):
    return lax.shift_right_logical(rows, CHUNK.bit_length() - 1)


A_HPS = 2
A_HW = A_HPS * LANES


def _band_bias(u_row, first):
    bias = pltpu.roll(jnp.broadcast_to(u_row, (TQ, 2 * TQ)), 0, 1, stride=1, stride_axis=0)
    qc = _chunk_of(lax.broadcasted_iota(jnp.int32, (TQ, 2 * TQ), 0))
    col = lax.broadcasted_iota(jnp.int32, (TQ, 2 * TQ), 1)
    kc = _chunk_of(jnp.bitwise_and(col, TQ - 1))
    ok = jnp.logical_or(jnp.logical_and(col < TQ, kc >= qc), jnp.logical_and(col >= TQ, kc <= qc))
    return jnp.where(ok, bias, NEG_INF) + jnp.where(col < TQ, first * NEG_INF, 0.0)


HQ = TQ // 2
HALVES = ((slice(0, HQ), slice(0, 3 * HQ)),
          (slice(HQ, TQ), slice(HQ, 4 * HQ)))


def _band_probs(q, kcat, bias):
    scale = HEAD_DIM ** -0.5
    out = []
    for rows, cols in HALVES:
        s = _dot_nt(q[rows], kcat[cols]) * scale + bias[rows, cols]
        p = jnp.exp(s - jnp.max(s, axis=1, keepdims=True))
        out.append((p, 1.0 / jnp.sum(p, axis=1, keepdims=True)))
    return out


def _band_specs(nq):
    def col(seg, h):
        return C_A // A_HW + 4 * h + seg

    q_spec = pl.BlockSpec((TQ, A_HW), lambda h, i: (jnp.minimum(i, nq - 1), col(SEG_Q, h)))
    kp_spec = pl.BlockSpec((TQ, A_HW), lambda h, i: (jnp.clip(i - 1, 0, nq - 1), col(SEG_K, h)))
    kc_spec = pl.BlockSpec((TQ, A_HW), lambda h, i: (jnp.minimum(i, nq - 1), col(SEG_K, h)))
    vp_spec = pl.BlockSpec((TQ, A_HW), lambda h, i: (jnp.clip(i - 1, 0, nq - 1), col(SEG_V, h)))
    vc_spec = pl.BlockSpec((TQ, A_HW), lambda h, i: (jnp.minimum(i, nq - 1), col(SEG_V, h)))
    z_spec = pl.BlockSpec((TQ, A_HW), lambda h, i: (jnp.minimum(i, nq - 1), col(SEG_Z, h)))
    u_spec = pl.BlockSpec((A_HPS, 1, 2 * TQ), lambda h, i: (h, 0, 0))
    return q_spec, kp_spec, kc_spec, vp_spec, vc_spec, z_spec, u_spec


def _band_fwd(h, u):
    S = h.shape[0]
    nq = S // TQ

    def body(q_ref, kp_ref, kc_ref, vp_ref, vc_ref, z_ref, u_ref, y_ref, yt_ref, bias_scr):
        i = pl.program_id(1)

        @pl.when(i <= 1)
        def _():
            for hh in range(A_HPS):
                bias_scr[hh] = _band_bias(u_ref[hh], (i == 0).astype(F32))

        for hh in range(A_HPS):
            cs = slice(hh * LANES, (hh + 1) * LANES)
            kcat = jnp.concatenate([kp_ref[:, cs], kc_ref[:, cs]], axis=0)
            vcat = jnp.concatenate([vp_ref[:, cs], vc_ref[:, cs]], axis=0)
            probs = _band_probs(q_ref[:, cs], kcat, bias_scr[hh])
            o = jnp.concatenate([_dot(p.astype(BF16), vcat[cols]) * inv
                                 for (p, inv), (_, cols) in zip(probs, HALVES)], axis=0)
            z = z_ref[:, cs].astype(F32)
            y = o * (z * _sigmoid(z))
            y_ref[:, cs] = y.astype(BF16)
            yt_ref[cs, :] = y.T.astype(BF16)

    specs = _band_specs(nq)
    return pl.pallas_call(
        body,
        name="band_fwd",
        grid=(A_HEADS // A_HPS, nq),
        in_specs=[specs[0], specs[1], specs[2], specs[3], specs[4], specs[5], specs[6]],
        out_specs=[pl.BlockSpec((TQ, A_HW), lambda h, i: (i, h)), pl.BlockSpec((A_HW, TQ), lambda h, i: (h, i))],
        out_shape=[jax.ShapeDtypeStruct((S, D_MODEL), BF16), jax.ShapeDtypeStruct((D_MODEL, S), BF16)],
        scratch_shapes=[pltpu.VMEM((A_HPS, TQ, 2 * TQ), F32)],
        compiler_params=_params(("parallel", "arbitrary")),
    )(h, h, h, h, h, h, u)


def _band_bwd(h, u, dycat):
    S = h.shape[0]
    nq = S // TQ
    scale = HEAD_DIM ** -0.5
    qs, ks, vs, zs = (slice(s * A_HW, (s + 1) * A_HW) for s in (SEG_Q, SEG_K, SEG_V, SEG_Z))

    def body(q_ref, kp_ref, kc_ref, vp_ref, vc_ref, z_ref, u_ref, dy_ref,
             dh_ref, du_ref, bias_scr, db_scr, ckt_scr, cvt_scr, cq_scr, cz_scr):
        i = pl.program_id(1)

        @pl.when(i <= 1)
        def _():
            for hh in range(A_HPS):
                bias_scr[hh] = _band_bias(u_ref[hh], (i == 0).astype(F32))

        @pl.when(i == 0)
        def _():
            db_scr[...] = jnp.zeros_like(db_scr)
            ckt_scr[...] = jnp.zeros_like(ckt_scr)
            cvt_scr[...] = jnp.zeros_like(cvt_scr)
            cq_scr[...] = jnp.zeros_like(cq_scr)
            cz_scr[...] = jnp.zeros_like(cz_scr)

        @pl.when(i < nq)
        def _():
            dh_ref[:, qs] = cq_scr[...]
            dh_ref[:, zs] = cz_scr[...]
            for hh in range(A_HPS):
                cs = slice(hh * LANES, (hh + 1) * LANES)
                q = q_ref[:, cs]
                kcat = jnp.concatenate([kp_ref[:, cs], kc_ref[:, cs]], axis=0)
                vcat = jnp.concatenate([vp_ref[:, cs], vc_ref[:, cs]], axis=0)
                probs = [p * inv for p, inv in _band_probs(q, kcat, bias_scr[hh])]
                o = jnp.concatenate([_dot(p.astype(BF16), vcat[cols]) for p, (_, cols) in zip(probs, HALVES)], axis=0)
                z = z_ref[:, cs].astype(F32)
                sg = _sigmoid(z)
                dy = dy_ref[:, cs].astype(F32)
                do = dy * (z * sg)
                cz_scr[:, cs] = (dy * o * (sg * (1.0 + z * (1.0 - sg)))).astype(BF16)
                dob = do.astype(BF16)
                delta = jnp.sum(do * o, axis=1, keepdims=True)
                qt, dot_ = q.T, dob.T
                dq, dkt, dvt = [], [], []
                for p, (rows, cols) in zip(probs, HALVES):
                    ds = p * (_dot_nt(dob[rows], vcat[cols]) - delta[rows])
                    db_scr[hh, rows, cols] += ds
                    dsb = ds.astype(BF16)
                    dq.append(scale * _dot(dsb, kcat[cols]))
                    dkt.append(scale * _dot(qt[:, rows], dsb))
                    dvt.append(_dot(dot_[:, rows], p.astype(BF16)))
                cq_scr[:, cs] = jnp.concatenate(dq, axis=0).astype(BF16)

                def over_keys(parts):
                    lo, hi = parts
                    prev = jnp.concatenate([lo[:, :HQ], lo[:, HQ:2 * HQ] + hi[:, :HQ]], axis=1)
                    cur = jnp.concatenate([lo[:, 2 * HQ:] + hi[:, HQ:2 * HQ], hi[:, 2 * HQ:]], axis=1)
                    return prev, cur

                (dk_prev, dk_cur), (dv_prev, dv_cur) = over_keys(dkt), over_keys(dvt)
                dh_ref[:, SEG_K * A_HW + hh * LANES:SEG_K * A_HW + (hh + 1) * LANES] = (
                    ckt_scr[cs, :] + dk_prev).T.astype(BF16)
                dh_ref[:, SEG_V * A_HW + hh * LANES:SEG_V * A_HW + (hh + 1) * LANES] = (
                    cvt_scr[cs, :] + dv_prev).T.astype(BF16)
                ckt_scr[cs, :] = dk_cur
                cvt_scr[cs, :] = dv_cur

        @pl.when(i == nq)
        def _():
            dh_ref[:, qs] = cq_scr[...]
            dh_ref[:, zs] = cz_scr[...]
            dh_ref[:, ks] = ckt_scr[...].T.astype(BF16)
            dh_ref[:, vs] = cvt_scr[...].T.astype(BF16)
            r0 = lax.broadcasted_iota(jnp.int32, (TQ, TQ), 0)
            r1 = lax.broadcasted_iota(jnp.int32, (TQ, TQ), 1)
            flip = (r0 + r1 == TQ - 1).astype(BF16)
            for hh in range(A_HPS):
                fl = _dot3(flip, db_scr[hh])
                rolled = pltpu.roll(fl, 0, 1, stride=1, stride_axis=0)
                du_ref[hh] = jnp.sum(rolled, axis=0, keepdims=True)

    specs = _band_specs(nq)
    dy_spec = pl.BlockSpec((TQ, A_HW), lambda h, i: (jnp.minimum(i, nq - 1), h))
    return pl.pallas_call(
        body,
        name="band_bwd",
        grid=(A_HEADS // A_HPS, nq + 1),
        in_specs=[specs[0], specs[1], specs[2], specs[3], specs[4], specs[5], specs[6], dy_spec],
        out_specs=[pl.BlockSpec((TQ, 4 * A_HW), lambda h, i: (jnp.maximum(i - 1, 0), C_A // (4 * A_HW) + h)),
                   pl.BlockSpec((A_HPS, 1, 2 * TQ), lambda h, i: (h, 0, 0))],
        out_shape=[jax.ShapeDtypeStruct((S, HP), BF16), jax.ShapeDtypeStruct((A_HEADS, 1, 2 * TQ), F32)],
        scratch_shapes=[pltpu.VMEM((A_HPS, TQ, 2 * TQ), F32), pltpu.VMEM((A_HPS, TQ, 2 * TQ), F32),
                        pltpu.VMEM((A_HW, TQ), F32), pltpu.VMEM((A_HW, TQ), F32),
                        pltpu.VMEM((TQ, A_HW), BF16), pltpu.VMEM((TQ, A_HW), BF16)],
        compiler_params=_params(("parallel", "arbitrary")),
    )(h, h, h, h, h, h, u, dycat)


def _bias_by_offset(table):
    far = jnp.broadcast_to(table[:, N_REL - 1:], (A_HEADS, TQ - MAX_REL))
    ramp = jnp.flip(table, axis=1)
    rest = jnp.broadcast_to(table[:, :1], (A_HEADS, 2 * TQ - CHUNK - (TQ + MAX_REL + 1)))
    wrap = jnp.broadcast_to(table[:, N_REL - 1:], (A_HEADS, CHUNK))
    return jnp.concatenate([far, ramp, rest, wrap], axis=1)[:, None, :]


def _bias_grad_from_offset(du):
    g = jnp.roll(du[:, 0, :], -(TQ - 1), axis=1)
    far = jnp.sum(g[:, :TQ - MAX_REL], axis=1) + jnp.sum(g[:, 2 * TQ - CHUNK:], axis=1)
    ramp = jnp.flip(g[:, TQ - MAX_REL:TQ + MAX_REL + 1], axis=1)
    return ramp.at[:, N_REL - 1].add(far)


def _mem_probs(q, mk):
    s = _dot_nt(q, mk) * (HEAD_DIM ** -0.5)
    p = jnp.exp(s - jnp.max(s, axis=1, keepdims=True))
    return p * (1.0 / jnp.sum(p, axis=1, keepdims=True))


def _mem_cols(hd, seg):
    return slice((2 * hd + seg) * LANES, (2 * hd + seg + 1) * LANES)


def _mem_fwd(h, mkv, ycat, ycat_t):
    S = h.shape[0]
    nm = mkv.shape[0]
    c0 = (A_WIDTH + B_WIDTH) // LANES

    def body(qz_ref, mkv_ref, yin_ref, ytin_ref, y_ref, yt_ref):
        del yin_ref, ytin_ref
        for hd in range(M_HEADS):
            lane = slice(hd * LANES, (hd + 1) * LANES)
            p = _mem_probs(qz_ref[:, _mem_cols(hd, 0)], mkv_ref[:, lane])
            o = _dot(p.astype(BF16), mkv_ref[:, M_WIDTH + hd * LANES:M_WIDTH + (hd + 1) * LANES])
            z = qz_ref[:, _mem_cols(hd, 1)].astype(F32)
            y = o * (z * _sigmoid(z))
            y_ref[:, lane] = y.astype(BF16)
            yt_ref[lane, :] = y.T.astype(BF16)

    return pl.pallas_call(
        body,
        name="mem_fwd",
        grid=(S // TQ,),
        in_specs=[pl.BlockSpec((TQ, 2 * M_WIDTH), lambda i: (i, C_M // (2 * M_WIDTH))),
                  pl.BlockSpec((nm, 2 * M_WIDTH), lambda i: (0, 0)), ANY, ANY],
        out_specs=[pl.BlockSpec((TQ, M_WIDTH), lambda i: (i, c0 // M_HEADS)),
                   pl.BlockSpec((M_WIDTH, TQ), lambda i: (c0 // M_HEADS, i))],
        out_shape=[jax.ShapeDtypeStruct(ycat.shape, BF16), jax.ShapeDtypeStruct(ycat_t.shape, BF16)],
        input_output_aliases={2: 0, 3: 1},
        compiler_params=_params(("parallel",)),
    )(h, mkv, ycat, ycat_t)


def _mem_bwd(h, mkv, dycat, dh):
    S = h.shape[0]
    nm = mkv.shape[0]
    scale = HEAD_DIM ** -0.5

    def body(qz_ref, mkv_ref, dy_ref, dhin_ref, dh_ref, dmkv_ref):
        del dhin_ref
        i = pl.program_id(0)
        for hd in range(M_HEADS):
            lane = slice(hd * LANES, (hd + 1) * LANES)
            lane_v = slice(M_WIDTH + hd * LANES, M_WIDTH + (hd + 1) * LANES)
            q, mk, mv = qz_ref[:, _mem_cols(hd, 0)], mkv_ref[:, lane], mkv_ref[:, lane_v]
            p = _mem_probs(q, mk)
            pb = p.astype(BF16)
            o = _dot(pb, mv)
            z = qz_ref[:, _mem_cols(hd, 1)].astype(F32)
            sg = _sigmoid(z)
            dy = dy_ref[:, lane].astype(F32)
            do = dy * (z * sg)
            dh_ref[:, _mem_cols(hd, 1)] = (dy * o * (sg * (1.0 + z * (1.0 - sg)))).astype(BF16)
            dob = do.astype(BF16)
            ds = p * (_dot_nt(dob, mv) - jnp.sum(do * o, axis=1, keepdims=True))
            dsb = ds.astype(BF16)
            dh_ref[:, _mem_cols(hd, 0)] = (scale * _dot(dsb, mk)).astype(BF16)
            dmk = scale * _dot_tn(dsb, q)
            dmv = _dot_tn(pb, dob)

            @pl.when(i == 0)
            def _():
                dmkv_ref[:, lane] = dmk
                dmkv_ref[:, lane_v] = dmv

            @pl.when(i > 0)
            def _():
                dmkv_ref[:, lane] += dmk
                dmkv_ref[:, lane_v] += dmv

    return pl.pallas_call(
        body,
        name="mem_bwd",
        grid=(S // TQ,),
        in_specs=[pl.BlockSpec((TQ, 2 * M_WIDTH), lambda i: (i, C_M // (2 * M_WIDTH))),
                  pl.BlockSpec((nm, 2 * M_WIDTH), lambda i: (0, 0)),
                  pl.BlockSpec((TQ, M_WIDTH), lambda i: (i, (A_WIDTH + B_WIDTH) // M_WIDTH)), ANY],
        out_specs=[pl.BlockSpec((TQ, 2 * M_WIDTH), lambda i: (i, C_M // (2 * M_WIDTH))),
                   pl.BlockSpec((nm, 2 * M_WIDTH), lambda i: (0, 0))],
        out_shape=[jax.ShapeDtypeStruct(dh.shape, BF16), jax.ShapeDtypeStruct((nm, 2 * M_WIDTH), F32)],
        input_output_aliases={3: 0},
        compiler_params=_params(("arbitrary",)),
    )(h, mkv, dycat, dh)


SUB = 2 * CHUNK
SUBS = [slice(s * SUB, (s + 1) * SUB) for s in range(TQ // SUB)]


def _chunk_masks():
    r = lax.broadcasted_iota(jnp.int32, (SUB, SUB), 0)
    c = lax.broadcasted_iota(jnp.int32, (SUB, SUB), 1)
    same = _chunk_of(r) == _chunk_of(c)
    return jnp.logical_and(same, c <= r), jnp.logical_and(same, c > r), jnp.logical_and(same, c >= r)


def _by_sub(fn):
    return jnp.concatenate([fn(rows) for rows in SUBS], axis=0)


def _gla_gates(lr, gw, gb):
    logit = _dot(lr, gw) + gb
    sg = _sigmoid(logit)
    g = (jnp.minimum(logit, 0.0) - jnp.log(1.0 + jnp.exp(-jnp.abs(logit)))) * (1.0 / GATE_TAU)
    lo = _chunk_masks()[0].astype(BF16)
    return sg, _by_sub(lambda rows: _dot3(lo, g[rows]))


def _gla_factors(q, k, b):
    eb = jnp.exp(b)
    enb = jnp.exp(-b)
    return eb, enb, q * eb, q * enb, k * eb, k * enb


def _gla_intra(qp, qn, kp, kn):
    lo, up, _ = _chunk_masks()
    qp, qn, kp, kn = qp.astype(BF16), qn.astype(BF16), kp.astype(BF16), kn.astype(BF16)
    return [jnp.where(lo, _dot_nt(qp[rows], kn[rows]), 0.0) + jnp.where(up, _dot_nt(qn[rows], kp[rows]), 0.0)
            for rows in SUBS]


B_HPS = 2
B_HW = B_HPS * LANES


def _gla_specs(nb, rev):
    blk = (lambda i: nb - 1 - i) if rev else (lambda i: i)
    qkvz_spec = pl.BlockSpec((TQ, 4 * B_HW), lambda i, p: (blk(i), C_B // (4 * B_HW) + p))
    lr_spec = pl.BlockSpec((TQ, LANES), lambda i, p: (blk(i), C_LR // LANES))
    gw_spec = pl.BlockSpec((LANES, B_HW), lambda i, p: (0, p))
    gb_spec = pl.BlockSpec((1, B_HW), lambda i, p: (0, p))
    gn_spec = pl.BlockSpec((1, LANES), lambda i, p: (0, 0))
    return qkvz_spec, lr_spec, gw_spec, gb_spec, gn_spec, blk


def _head_cols(hh, seg):
    return slice((4 * hh + seg) * LANES, (4 * hh + seg + 1) * LANES)


def _gla_fwd(h, gw, gb, gn, ycat, ycat_t):
    S = h.shape[0]
    nb = S // TQ
    c0 = A_WIDTH // LANES

    def body(qkvz_ref, lr_ref, gw_ref, gb_ref, gn_ref, yin_ref, ytin_ref,
             y_ref, yt_ref, o_ref, st_ref, st_scr):
        del yin_ref, ytin_ref
        i, p = pl.program_id(0), pl.program_id(1)
        for hh in range(B_HPS):
            hd = B_HPS * p + hh
            lane = slice(hh * LANES, (hh + 1) * LANES)

            @pl.when(i == 0)
            def _():
                st_scr[hd] = jnp.zeros((LANES, LANES), F32)

            q = qkvz_ref[:, _head_cols(hh, SEG_Q)].astype(F32) * (B_DK ** -0.5)
            k = qkvz_ref[:, _head_cols(hh, SEG_K)].astype(F32)
            v = qkvz_ref[:, _head_cols(hh, SEG_V)]
            _, b = _gla_gates(lr_ref[...], gw_ref[:, lane], gb_ref[:, lane])
            _, _, qp, qn, kp, kn = _gla_factors(q, k, b)
            o_intra = jnp.concatenate([_dot(a.astype(BF16), v[rows])
                                       for a, rows in zip(_gla_intra(qp, qn, kp, kn), SUBS)], axis=0)
            qpb, knb = qp.astype(BF16), kn.astype(BF16)
            st = st_scr[hd]
            outs = []
            for c in range(CPB):
                rows = slice(c * CHUNK, (c + 1) * CHUNK)
                st_ref[hh, c] = st
                outs.append(_dot_nt(qpb[rows], st.astype(BF16)))
                e_last = jnp.exp(b[(c + 1) * CHUNK - 1:(c + 1) * CHUNK, :])
                st = (st + _dot_tn(v[rows], knb[rows])) * e_last
            st_scr[hd] = st
            o = o_intra + jnp.concatenate(outs, axis=0)
            o_ref[:, lane] = o
            r = lax.rsqrt(jnp.mean(o * o, axis=1, keepdims=True) + RMS_EPS)
            z = qkvz_ref[:, _head_cols(hh, SEG_Z)].astype(F32)
            y = o * r * gn_ref[...] * (z * _sigmoid(z))
            y_ref[:, lane] = y.astype(BF16)
            yt_ref[lane, :] = y.T.astype(BF16)

    qkvz_s, lr_s, gw_s, gb_s, gn_s, _ = _gla_specs(nb, False)
    return pl.pallas_call(
        body,
        name="gla_fwd",
        grid=(nb, B_HEADS // B_HPS),
        in_specs=[qkvz_s, lr_s, gw_s, gb_s, gn_s, ANY, ANY],
        out_specs=[pl.BlockSpec((TQ, B_HW), lambda i, p: (i, c0 // B_HPS + p)),
                   pl.BlockSpec((B_HW, TQ), lambda i, p: (c0 // B_HPS + p, i)),
                   pl.BlockSpec((TQ, B_HW), lambda i, p: (i, p)),
                   pl.BlockSpec((B_HPS, CPB, LANES, LANES), lambda i, p: (p, i, 0, 0))],
        out_shape=[jax.ShapeDtypeStruct(ycat.shape, BF16), jax.ShapeDtypeStruct(ycat_t.shape, BF16),
                   jax.ShapeDtypeStruct((S, B_WIDTH), F32),
                   jax.ShapeDtypeStruct((B_HEADS, S // CHUNK, LANES, LANES), F32)],
        input_output_aliases={5: 0, 6: 1},
        scratch_shapes=[pltpu.VMEM((B_HEADS, LANES, LANES), F32)],
        compiler_params=_params(("arbitrary", "arbitrary")),
    )(h, h, gw, gb, gn, ycat, ycat_t)


def _gla_bwd(h, gw, gb, gn, o_pre, states, dycat, dh):
    S = h.shape[0]
    nb = S // TQ
    n_steps = B_HEADS // B_HPS

    def body(qkvz_ref, lr_ref, gw_ref, gb_ref, gn_ref, o_ref, st_ref, dy_ref, dhin_ref,
             dh_ref, dgw_ref, dgb_ref, dgn_ref,
             dst_scr, dgw_scr, dgb_scr, dgn_scr, dlr_scr):
        del dhin_ref
        i, p = pl.program_id(0), pl.program_id(1)

        @pl.when(jnp.logical_and(i == 0, p == 0))
        def _():
            dgn_scr[...] = jnp.zeros_like(dgn_scr)

        dlr_heads = [one_head(hh, i, p, qkvz_ref, lr_ref, gw_ref, gb_ref, gn_ref, o_ref, st_ref, dy_ref,
                              dh_ref, dgw_ref, dgb_ref, dst_scr, dgw_scr, dgb_scr, dgn_scr) for hh in range(B_HPS)]
        dlr = dlr_heads[0]
        for more in dlr_heads[1:]:
            dlr = dlr + more

        @pl.when(p == 0)
        def _():
            dlr_scr[...] = dlr

        @pl.when(p > 0)
        def _():
            dlr_scr[...] += dlr

        @pl.when(p == n_steps - 1)
        def _():
            cols = _head_cols(B_HPS - 1, SEG_Q)
            dh_ref[:, cols] = (dh_ref[:, cols].astype(F32) + dlr_scr[...]).astype(BF16)

        @pl.when(i == nb - 1)
        def _():
            dgn_ref[...] = dgn_scr[...]

    def one_head(hh, i, p, qkvz_ref, lr_ref, gw_ref, gb_ref, gn_ref, o_ref, st_ref, dy_ref,
                 dh_ref, dgw_ref, dgb_ref, dst_scr, dgw_scr, dgb_scr, dgn_scr):
        hd = B_HPS * p + hh
        lane = slice(hh * LANES, (hh + 1) * LANES)

        @pl.when(i == 0)
        def _():
            dst_scr[hd] = jnp.zeros((LANES, LANES), F32)
            dgw_scr[hd] = jnp.zeros((LANES, LANES), F32)
            dgb_scr[hd] = jnp.zeros((1, LANES), F32)

        q = qkvz_ref[:, _head_cols(hh, SEG_Q)].astype(F32) * (B_DK ** -0.5)
        k = qkvz_ref[:, _head_cols(hh, SEG_K)].astype(F32)
        v = qkvz_ref[:, _head_cols(hh, SEG_V)]
        lr, gwv = lr_ref[...], gw_ref[:, lane]
        sg, b = _gla_gates(lr, gwv, gb_ref[:, lane])
        eb, enb, qp, qn, kp, kn = _gla_factors(q, k, b)
        a = _gla_intra(qp, qn, kp, kn)
        qpb, qnb, kpb, knb = qp.astype(BF16), qn.astype(BF16), kp.astype(BF16), kn.astype(BF16)

        o = o_ref[:, lane]
        gn = gn_ref[...]
        r = lax.rsqrt(jnp.mean(o * o, axis=1, keepdims=True) + RMS_EPS)
        z = qkvz_ref[:, _head_cols(hh, SEG_Z)].astype(F32)
        sz = _sigmoid(z)
        dy = dy_ref[:, lane].astype(F32)
        d_on = dy * (z * sz)
        dh_ref[:, _head_cols(hh, SEG_Z)] = (dy * (o * r * gn) * (sz * (1.0 + z * (1.0 - sz)))).astype(BF16)
        dgn_scr[...] += jnp.sum(d_on * o * r, axis=0, keepdims=True)
        t = d_on * gn
        do = r * t - o * (r * r * r) * jnp.mean(t * o, axis=1, keepdims=True)
        dob = do.astype(BF16)

        lo, up, upper = _chunk_masks()
        dqp, dkn, dqn, dkp, dv = [], [], [], [], []
        for a_s, rows in zip(a, SUBS):
            da = _dot_nt(dob[rows], v[rows])
            dalo = jnp.where(lo, da, 0.0).astype(BF16)
            daup = jnp.where(up, da, 0.0).astype(BF16)
            dqp.append(_dot(dalo, knb[rows]))
            dkn.append(_dot_tn(dalo, qpb[rows]))
            dqn.append(_dot(daup, kpb[rows]))
            dkp.append(_dot_tn(daup, qnb[rows]))
            dv.append(_dot_tn(a_s.astype(BF16), dob[rows]))
        dqp, dkn, dqn, dkp, dv = (jnp.concatenate(x, axis=0) for x in (dqp, dkn, dqn, dkp, dv))

        dst = dst_scr[hd]
        dqp_c, dkn_c, dv_c, dbl_c = [None] * CPB, [None] * CPB, [None] * CPB, [None] * CPB
        for c in reversed(range(CPB)):
            rows = slice(c * CHUNK, (c + 1) * CHUNK)
            st = st_ref[hh, c]
            e_last = jnp.exp(b[(c + 1) * CHUNK - 1:(c + 1) * CHUNK, :])
            if c == CPB - 1:
                st_next = (st + _dot_tn(v[rows], knb[rows])) * e_last
            else:
                st_next = st_ref[hh, c + 1]
            dbl_c[c] = jnp.sum(dst * st_next, axis=0, keepdims=True)
            dtt = (dst * e_last).astype(BF16)
            dv_c[c] = _dot_nt(knb[rows], dtt)
            dkn_c[c] = _dot(v[rows], dtt)
            dqp_c[c] = _dot(dob[rows], st.astype(BF16))
            dst = _dot_tn(dob[rows], qpb[rows]) + dst * e_last
        dst_scr[hd] = dst
        dqp = dqp + jnp.concatenate(dqp_c, axis=0)
        dkn = dkn + jnp.concatenate(dkn_c, axis=0)
        dv = dv + jnp.concatenate(dv_c, axis=0)
        dh_ref[:, _head_cols(hh, SEG_V)] = dv.astype(BF16)
        dh_ref[:, _head_cols(hh, SEG_Q)] = ((dqp * eb + dqn * enb) * (B_DK ** -0.5)).astype(BF16)
        dh_ref[:, _head_cols(hh, SEG_K)] = (dkp * eb + dkn * enb).astype(BF16)

        last = jnp.bitwise_and(lax.broadcasted_iota(jnp.int32, (TQ, 1), 0), CHUNK - 1) == CHUNK - 1
        dbl = jnp.concatenate([jnp.broadcast_to(x, (CHUNK, LANES)) for x in dbl_c], axis=0)
        db = dqp * qp - dqn * qn + dkp * kp - dkn * kn + jnp.where(last, dbl, 0.0)
        upper_b = upper.astype(BF16)
        dlogit = _by_sub(lambda rows: _dot3(upper_b, db[rows])) * (1.0 / GATE_TAU) * (1.0 - sg)
        dlb = dlogit.astype(BF16)
        dgw_scr[hd] += _dot_tn(lr, dlb)
        dgb_scr[hd] += jnp.sum(dlogit, axis=0, keepdims=True)

        @pl.when(i == nb - 1)
        def _():
            dgw_ref[:, lane] = dgw_scr[hd]
            dgb_ref[:, lane] = dgb_scr[hd]

        return _dot_nt(dlb, gwv)

    qkvz_s, lr_s, gw_s, gb_s, gn_s, blk = _gla_specs(nb, True)
    row = pl.BlockSpec((TQ, B_HW), lambda i, p: (blk(i), p))
    dy_spec = pl.BlockSpec((TQ, B_HW), lambda i, p: (blk(i), A_WIDTH // B_HW + p))
    st_spec = pl.BlockSpec((B_HPS, CPB, LANES, LANES), lambda i, p: (p, blk(i), 0, 0))
    return pl.pallas_call(
        body,
        name="gla_bwd",
        grid=(nb, B_HEADS // B_HPS),
        in_specs=[qkvz_s, lr_s, gw_s, gb_s, gn_s, row, st_spec, dy_spec, ANY],
        out_specs=[pl.BlockSpec((TQ, 4 * B_HW), lambda i, p: (blk(i), C_B // (4 * B_HW) + p)),
                   pl.BlockSpec((LANES, B_HW), lambda i, p: (0, jnp.where(i == nb - 1, p, 0))),
                   pl.BlockSpec((1, B_HW), lambda i, p: (0, jnp.where(i == nb - 1, p, 0))),
                   pl.BlockSpec((1, LANES), lambda i, p: (0, 0))],
        out_shape=[jax.ShapeDtypeStruct(dh.shape, BF16),
                   jax.ShapeDtypeStruct((LANES, B_HEADS * LANES), F32),
                   jax.ShapeDtypeStruct((1, B_HEADS * LANES), F32),
                   jax.ShapeDtypeStruct((1, LANES), F32)],
        input_output_aliases={8: 0},
        scratch_shapes=[pltpu.VMEM((B_HEADS, LANES, LANES), F32), pltpu.VMEM((B_HEADS, LANES, LANES), F32),
                        pltpu.VMEM((B_HEADS, 1, LANES), F32), pltpu.VMEM((1, LANES), F32),
                        pltpu.VMEM((TQ, LANES), F32)],
        compiler_params=_params(("arbitrary", "arbitrary")),
    )(h, h, gw, gb, gn, o_pre, states, dycat, dh)


LN_ROWS = 512


def _resident(shape):
    return pl.BlockSpec(shape, lambda i: (0,) * len(shape), pipeline_mode=pl.Buffered(1))


def _outproj_ln(ycat, w_out, x, g, b, rider=None):
    S = x.shape[0]
    n = S // LN_ROWS
    hop, srcs, bufs = rider if rider is not None else (None, [], [])

    def body(*refs):
        yc_ref, w_ref, x_ref, g_ref, b_ref = refs[:5]
        src_refs = refs[5:5 + len(srcs)]
        outs = refs[5 + len(srcs) + len(bufs):]
        y_ref, yb_ref, yt_ref, xh_ref, rs_ref = outs[:5]
        buf_refs, sems = outs[5:5 + len(bufs)], outs[5 + len(bufs):]
        i = pl.program_id(0)
        if hop is not None:
            @pl.when(i == 0)
            def _():
                hop.start(src_refs, buf_refs, *sems)

        u = ALPHA * x_ref[...] + _dot(yc_ref[...], w_ref[...])
        mu = jnp.mean(u, axis=1, keepdims=True)
        d = u - mu
        rstd = lax.rsqrt(jnp.mean(d * d, axis=1, keepdims=True) + LN_EPS)
        xh = d * rstd
        y = xh * g_ref[...] + b_ref[...]
        y_ref[...] = y
        yb_ref[...] = y.astype(BF16)
        yt_ref[...] = y.T.astype(BF16)
        xh_ref[...] = xh
        rs_ref[...] = rstd
        if hop is not None:
            @pl.when(i == n - 1)
            def _():
                hop.wait(src_refs, buf_refs, *sems)

    row = lambda w: pl.BlockSpec((LN_ROWS, w), lambda i: (i, 0))
    vec = pl.BlockSpec((1, D_MODEL), lambda i: (0, 0))
    n_in = 5 + len(srcs)
    out = pl.pallas_call(
        body,
        name="outproj_ln",
        grid=(n,),
        in_specs=[row(D_MODEL), _resident((D_MODEL, D_MODEL)), row(D_MODEL), vec, vec] + [ANY] * (len(srcs) + len(bufs)),
        out_specs=[row(D_MODEL), row(D_MODEL), pl.BlockSpec((D_MODEL, LN_ROWS), lambda i: (0, i)), row(D_MODEL), row(1)]
        + [ANY] * len(bufs),
        out_shape=[jax.ShapeDtypeStruct((S, D_MODEL), F32), jax.ShapeDtypeStruct((S, D_MODEL), BF16),
                   jax.ShapeDtypeStruct((D_MODEL, S), BF16),
                   jax.ShapeDtypeStruct((S, D_MODEL), F32), jax.ShapeDtypeStruct((S, 1), F32)]
        + [jax.ShapeDtypeStruct(x_.shape, x_.dtype) for x_ in bufs],
        input_output_aliases={n_in + t: 5 + t for t in range(len(bufs))},
        scratch_shapes=hop.sems() if hop is not None else [],
        compiler_params=_params(("parallel",) if hop is None else ("arbitrary",)),
    )(ycat, w_out, x, g, b, *srcs, *bufs)
    return tuple(out[:5]) if hop is None else (tuple(out[:5]), list(out[5:]))


def _ln_bwd_dycat(dy, xhat, rstd, g, w_out, rider=None):
    S = dy.shape[0]
    n = S // LN_ROWS
    hop, srcs, bufs = rider if rider is not None else (None, [], [])

    def body(*refs):
        dy_ref, xh_ref, rs_ref, g_ref, w_ref = refs[:5]
        src_refs = refs[5:5 + len(srcs)]
        outs = refs[5 + len(srcs) + len(bufs):]
        du_ref, dub_ref, dyc_ref, dg_ref, db_ref = outs[:5]
        buf_refs, sems = outs[5:5 + len(bufs)], outs[5 + len(bufs):]
        i = pl.program_id(0)
        if hop is not None:
            @pl.when(i == 0)
            def _():
                hop.start(src_refs, buf_refs, *sems)

        dy_, xh = dy_ref[...], xh_ref[...]
        dyg = dy_ * g_ref[...]
        m1 = jnp.mean(dyg, axis=1, keepdims=True)
        m2 = jnp.mean(dyg * xh, axis=1, keepdims=True)
        du = rs_ref[...] * (dyg - m1 - xh * m2)
        dub = du.astype(BF16)
        du_ref[...] = du
        dub_ref[...] = dub
        dyc_ref[...] = _dot_nt(dub, w_ref[...]).astype(BF16)
        dg = jnp.sum(dy_ * xh, axis=0, keepdims=True)
        db = jnp.sum(dy_, axis=0, keepdims=True)

        @pl.when(i == 0)
        def _():
            dg_ref[...] = dg
            db_ref[...] = db

        @pl.when(i > 0)
        def _():
            dg_ref[...] += dg
            db_ref[...] += db

        if hop is not None:
            @pl.when(i == n - 1)
            def _():
                hop.wait(src_refs, buf_refs, *sems)

    row = lambda w: pl.BlockSpec((LN_ROWS, w), lambda i: (i, 0))
    vec = pl.BlockSpec((1, D_MODEL), lambda i: (0, 0))
    n_in = 5 + len(srcs)
    out = pl.pallas_call(
        body,
        name="ln_bwd_dycat",
        grid=(n,),
        in_specs=[row(D_MODEL), row(D_MODEL), row(1), vec, pl.BlockSpec((D_MODEL, D_MODEL), lambda i: (0, 0))]
        + [ANY] * (len(srcs) + len(bufs)),
        out_specs=[row(D_MODEL), row(D_MODEL), row(D_MODEL), vec, vec] + [ANY] * len(bufs),
        out_shape=[jax.ShapeDtypeStruct((S, D_MODEL), F32), jax.ShapeDtypeStruct((S, D_MODEL), BF16),
                   jax.ShapeDtypeStruct((S, D_MODEL), BF16),
                   jax.ShapeDtypeStruct((1, D_MODEL), F32), jax.ShapeDtypeStruct((1, D_MODEL), F32)]
        + [jax.ShapeDtypeStruct(b.shape, b.dtype) for b in bufs],
        input_output_aliases={n_in + t: 5 + t for t in range(len(bufs))},
        scratch_shapes=hop.sems() if hop is not None else [],
        compiler_params=_params(("arbitrary",)),
    )(dy, xhat, rstd, g, w_out, *srcs, *bufs)
    return tuple(out[:5]) if hop is None else (tuple(out[:5]), list(out[5:]))


def _loss_head(y, target):
    S = y.shape[0]

    def body(y_ref, t_ref, l_ref, dy_ref):
        i = pl.program_id(0)
        err = y_ref[...] - t_ref[...]
        dy_ref[...] = err * (1.0 / D_MODEL)
        part = (0.5 / D_MODEL) * jnp.sum(jnp.sum(err * err, axis=1, keepdims=True), axis=0, keepdims=True)

        @pl.when(i == 0)
        def _():
            l_ref[...] = part

        @pl.when(i > 0)
        def _():
            l_ref[...] += part

    row = pl.BlockSpec((TQ, D_MODEL), lambda i: (i, 0))
    return pl.pallas_call(
        body,
        name="loss_head",
        grid=(S // TQ,),
        in_specs=[row, row],
        out_specs=[pl.BlockSpec((1, 1), lambda i: (0, 0)), row],
        out_shape=[jax.ShapeDtypeStruct((1, 1), F32), jax.ShapeDtypeStruct((S, D_MODEL), F32)],
        compiler_params=_params(("arbitrary",)),
    )(y, target)


def _pad_gate(gate_w, gate_b):
    gw = gate_w.reshape(GATE_RANK, B_HEADS, B_DK)
    gw = jnp.pad(gw, ((LR_LANE, LANES - LR_LANE - GATE_RANK), (0, 0), (0, LANES - B_DK))).reshape(LANES, B_HEADS * LANES)
    gb = jnp.pad(gate_b.reshape(B_HEADS, B_DK), ((0, 0), (0, LANES - B_DK))).reshape(1, B_HEADS * LANES)
    return gw.astype(BF16), gb.astype(F32)


def _layer_fwd(x, xb, xt, mem_b, w_in, w_kv, w_out, u, gw, gb, gn, ln_g, ln_b, rider=None, next_hop=None):
    h = _matmul(xb, w_in, mode="nn", out_dtype=BF16, tm=1024, tn=1792, tk=D_MODEL, name="in_proj", rider=rider)
    if rider is not None:
        h, rode = h
    mkv = _matmul(mem_b, w_kv, mode="nn", out_dtype=BF16, tm=mem_b.shape[0], tn=1024, tk=D_MODEL, name="mem_kv")
    ycat, ycat_t = _band_fwd(h, u)
    ycat, ycat_t, o_pre, states = _gla_fwd(h, gw, gb, gn, ycat, ycat_t)
    ycat, ycat_t = _mem_fwd(h, mkv, ycat, ycat_t)
    if next_hop is None:
        y, ybf, yt, xhat, rstd = _outproj_ln(ycat, w_out, x, ln_g, ln_b)
    else:
        (y, ybf, yt, xhat, rstd), rode = _outproj_ln(ycat, w_out, x, ln_g, ln_b, rider=(next_hop, [], rode))
    saved = (xt, h, mkv, ycat_t, o_pre, states, xhat, rstd)
    return (y, ybf, yt, saved) if rider is None else (y, ybf, yt, saved, rode)


def _layer_bwd(dy, saved, mem_b, w_in, w_out, u, gw, gb, gn, ln_g, reduce=None, own_reduce=None):
    xt, h, mkv, ycat_t, o_pre, states, xhat, rstd = saved

    def riding(**kw):
        if reduce is None:
            return _matmul(**kw)
        out, bufs = _matmul(rider=reduce.rider(), **kw)
        reduce.landed(bufs)
        return out

    if reduce is None:
        du, dub, dycat, d_ln_g, d_ln_b = _ln_bwd_dycat(dy, xhat, rstd, ln_g, w_out)
    else:
        (du, dub, dycat, d_ln_g, d_ln_b), bufs = _ln_bwd_dycat(dy, xhat, rstd, ln_g, w_out, rider=reduce.rider())
        reduce.landed(bufs)
    d_w_out = _matmul(ycat_t, dub, mode="nn", out_dtype=F32, tm=1024, tn=1024, tk=min(4096, dub.shape[0]), name="d_w_out")
    dh, d_u = _band_bwd(h, u, dycat)
    dh, dgw, dgb, dgn = _gla_bwd(h, gw, gb, gn, o_pre, states, dycat, dh)
    dh, dmkv = _mem_bwd(h, mkv, dycat, dh)
    d_w_kv = _matmul(mem_b, dmkv, mode="tn", out_dtype=F32, tm=1024, tn=1024, tk=mem_b.shape[0], name="d_w_kv")
    dx_args = dict(a=dh, b=w_in, mode="nt", out_dtype=F32, tm=1024, tn=1024, tk=3584, name="dx", add=du, add_scale=ALPHA)
    dw_args = dict(a=xt, b=dh, mode="nn", out_dtype=F32, tm=1024, tn=1024, tk=min(4096, dh.shape[0]), name="d_w_in")
    if own_reduce is None:
        dx = riding(**dx_args)
        d_w_in = riding(**dw_args)
        return dx, (d_w_in, d_u, dgw, dgb, dgn, d_w_kv, d_w_out, d_ln_g, d_ln_b)
    d_w_in = riding(**dw_args)
    grads = (d_w_in, d_u, dgw, dgb, dgn, d_w_kv, d_w_out, d_ln_g, d_ln_b)
    own = own_reduce(grads, reduce.finish() if reduce is not None else None)
    own.step()
    dx, bufs = _matmul(rider=own.rider(), **dx_args)
    own.landed(bufs)
    return dx, grads, own


def _unpad_heads(w):
    r = w.shape[0]
    return w.reshape(r, B_HEADS, LANES)[:, :, :B_DK].reshape(r, B_KEY_WIDTH)


def _padded_col_of():
    col, o = np.zeros(IN_WIDTH, np.int64), 0
    for seg in (SEG_Q, SEG_K, SEG_V, SEG_Z):
        for hd in range(A_HEADS):
            col[o:o + LANES] = C_A + (hd // A_HPS) * 4 * A_HW + seg * A_HW + (hd % A_HPS) * LANES + np.arange(LANES)
            o += LANES
    for seg, width in ((SEG_Q, B_DK), (SEG_K, B_DK), (SEG_V, LANES), (SEG_Z, LANES)):
        for hd in range(B_HEADS):
            col[o:o + width] = C_B + hd * 4 * LANES + seg * LANES + np.arange(width)
            o += width
    col[o:o + GATE_RANK] = C_LR + LR_LANE + np.arange(GATE_RANK)
    o += GATE_RANK
    for seg in (0, 1):
        for hd in range(M_HEADS):
            col[o:o + LANES] = C_M + hd * 2 * LANES + seg * LANES + np.arange(LANES)
            o += LANES
    assert o == IN_WIDTH
    return col


def _runs(idx):
    out, start = [], 0
    for k in range(1, len(idx) + 1):
        if k == len(idx) or idx[k] != idx[k - 1] + 1:
            out.append((int(idx[start]), k - start))
            start = k
    return out


def _chip_columns(g, j, n):
    runs = _runs(_padded_col_of()[j * n:(j + 1) * n])
    return jnp.concatenate([g[:, a:a + ln] for a, ln in runs], axis=1)


def _padded_from_shards(shards):
    n = shards[0].shape[1]
    src = np.full(HP, -1, np.int64)
    src[_padded_col_of()] = np.arange(IN_WIDTH)
    parts, k = [], 0
    while k < HP:
        e = k + 1
        if src[k] < 0:
            while e < HP and src[e] < 0:
                e += 1
            parts.append(jnp.zeros((shards[0].shape[0], e - k), shards[0].dtype))
        else:
            while e < HP and src[e] == src[e - 1] + 1 and src[e] // n == src[k] // n:
                e += 1
            parts.append(shards[src[k] // n][:, src[k] % n:src[k] % n + e - k])
        k = e
    return jnp.concatenate(parts, axis=1)


ADAMW_BLOCK_BYTES = 1 << 20


def _adamw(w, g, m, v, name):
    L, R, C = w.shape
    tl, tr = 1, R
    if R * C * 4 <= ADAMW_BLOCK_BYTES:
        tl = max(d for d in range(1, L + 1) if L % d == 0 and d * R * C * 4 <= ADAMW_BLOCK_BYTES)
    else:
        for cand in (256, 128, 64, 32, 16, 8):
            if R % cand == 0 and R > cand:
                tr = cand
                break

    def body(w_ref, g_ref, m_ref, v_ref, d_ref, nm_ref, nv_ref):
        g_ = g_ref[...]
        nm = ADAM_B1 * m_ref[...] + (1.0 - ADAM_B1) * g_
        nv = ADAM_B2 * v_ref[...] + (1.0 - ADAM_B2) * (g_ * g_)
        m_hat = nm / (1.0 - ADAM_B1 ** ADAM_STEP)
        v_hat = nv / (1.0 - ADAM_B2 ** ADAM_STEP)
        d_ref[...] = -ADAM_LR * (m_hat / (jnp.sqrt(v_hat) + ADAM_EPS) + ADAM_WD * w_ref[...])
        nm_ref[...] = nm
        nv_ref[...] = nv

    spec = pl.BlockSpec((tl, tr, C), lambda l, i: (l, i, 0))
    sd = jax.ShapeDtypeStruct((L, R, C), F32)
    return pl.pallas_call(
        body, name=name, grid=(L // tl, R // tr), in_specs=[spec] * 4, out_specs=[spec] * 3, out_shape=[sd] * 3,
        compiler_params=_params(("parallel", "parallel")),
    )(w, g, m, v)


def _adamw_nd(w, g, m, v, name):
    shape = w.shape
    f = (lambda a: a) if w.ndim == 3 else (lambda a: a.reshape(1, shape[0], shape[1]))
    return tuple(o.reshape(shape) for o in _adamw(f(w), f(g), f(m), f(v), name))


ANY = pl.BlockSpec(memory_space=pl.ANY)


def _place():
    x, y, c = lax.axis_index("x"), lax.axis_index("y"), lax.axis_index("c")
    chips = [(1 - x, y), (x, 1 - y), (1 - x, 1 - y)]
    return x, y, c, chips


class _WeightGather:
    def __init__(self, hop, layer, rows):
        self.hop, self.layer, self.rows = hop, layer, rows
        self.n_sem = 3 * len(rows)

    def _copies(self, shard_refs, buf_refs, send, recv, received):
        x, y, c, chips = _place()
        out = []
        for t, R in enumerate(self.rows):
            half = R // 2
            assert half % 16 == 0
            mine = pl.ds(pl.multiple_of(c * half, 16), half)
            other = pl.ds(pl.multiple_of((1 - c) * half, 16), half)
            mine_of_shard = pl.ds(pl.multiple_of(self.layer * R + c * half, 16), half)
            for k, chip in enumerate(chips):
                theirs = buf_refs[t].at[2 * chip[0] + chip[1]]
                if self.hop == "chips":
                    src, dst, to = shard_refs[t].at[mine_of_shard], buf_refs[t].at[2 * x + y, mine], (*chip, c)
                    landed = theirs.at[mine]
                else:
                    src, dst, to = theirs.at[mine], theirs.at[mine], (x, y, 1 - c)
                    landed = theirs.at[other]
                out.append(pltpu.make_async_remote_copy(
                    src_ref=src, dst_ref=landed if received else dst, send_sem=send.at[3 * t + k],
                    recv_sem=recv.at[3 * t + k], device_id=to, device_id_type=MESH))
        return out

    def start(self, shard_refs, buf_refs, send, recv):
        for cp in self._copies(shard_refs, buf_refs, send, recv, False):
            cp.start()

    def wait(self, shard_refs, buf_refs, send, recv):
        for cp in self._copies(shard_refs, buf_refs, send, recv, True):
            cp.wait_recv()
        for cp in self._copies(shard_refs, buf_refs, send, recv, False):
            cp.wait_send()

    def sems(self):
        return [pltpu.SemaphoreType.DMA((self.n_sem,)), pltpu.SemaphoreType.DMA((self.n_sem,))]

    def call(self, srcs, bufs, name):
        ns, nb = len(srcs), len(bufs)

        def body(*refs):
            src_refs, buf_refs, (send, recv) = refs[:ns], refs[ns + nb:ns + 2 * nb], refs[ns + 2 * nb:]
            self.start(src_refs, buf_refs, send, recv)
            self.wait(src_refs, buf_refs, send, recv)

        return pl.pallas_call(
            body, name=name, in_specs=[ANY] * (ns + nb), out_specs=[ANY] * nb,
            out_shape=[jax.ShapeDtypeStruct(b.shape, b.dtype) for b in bufs],
            input_output_aliases={ns + t: t for t in range(nb)},
            scratch_shapes=self.sems(),
        )(*srcs, *bufs)


class _GradHop(_WeightGather):
    def __init__(self, hop, layer, slices):
        self.hop, self.layer, self.slices = hop, layer, slices
        self.n_sem = {"pair": N_CHIPS, "chips": N_CHIPS - 1, "gather": 1}[hop] * len(slices)

    def _copies(self, src_refs, buf_refs, send, recv, received):
        x, y, c, chips = _place()
        me, out = 2 * x + y, []

        def remote(src, dst, to):
            k = len(out)
            out.append(pltpu.make_async_remote_copy(src_ref=src, dst_ref=dst, send_sem=send.at[k], recv_sem=recv.at[k],
                                                    device_id=to, device_id_type=MESH))

        for t, (half, where) in enumerate(self.slices):
            if self.hop == "pair":
                for j, (a, first) in enumerate(where):
                    rows = pl.ds(pl.multiple_of(first + (1 - c) * half, 8), half)
                    remote(src_refs[a].at[rows], buf_refs[t].at[j], (x, y, 1 - c))
            elif self.hop == "chips":
                for chip in chips:
                    slot = 2 * chip[0] + chip[1]
                    remote(src_refs[t].at[slot], buf_refs[t].at[slot if received else me], (*chip, c))
            else:
                mine = buf_refs[t].at[self.layer, c]
                remote(mine, buf_refs[t].at[self.layer, 1 - c] if received else mine, (x, y, 1 - c))
        return out


def _add_halves(parts, got, c_idx, name):
    n, L, half, C = got.shape
    tr = 64

    def body(*refs):
        ins, (got_ref, o_ref) = refs[1:1 + len(parts)], refs[1 + len(parts):]
        for k in range(len(parts)):
            o_ref[k // L, k % L] = (ins[k][...] + got_ref[k // L, k % L]).astype(BF16)

    def rows_of(first):
        assert first % tr == 0 and half % tr == 0
        return lambda i, c: (first // tr + c[0] * (half // tr) + i, 0)

    whole = pl.BlockSpec((n, L, tr, C), lambda i, c: (0, 0, i, 0))
    return pl.pallas_call(
        body, name=name,
        grid_spec=pltpu.PrefetchScalarGridSpec(
            num_scalar_prefetch=1, grid=(half // tr,),
            in_specs=[pl.BlockSpec((tr, C), rows_of(first)) for _, first in parts] + [whole],
            out_specs=whole),
        out_shape=jax.ShapeDtypeStruct((n, L, half, C), BF16),
        compiler_params=_params(("parallel",)),
    )(c_idx, *[a for a, _ in parts], got)


def _add_slots(r, c_idx, dest, layer, name):
    n, half, C = r.shape
    tr = 256

    def body(c_ref, r_ref, dest_ref, o_ref):
        del dest_ref
        acc = r_ref[0].astype(F32)
        for j in range(1, n):
            acc = acc + r_ref[j].astype(F32)
        o_ref[0, 0] = acc

    return pl.pallas_call(
        body, name=name,
        grid_spec=pltpu.PrefetchScalarGridSpec(
            num_scalar_prefetch=1, grid=(half // tr,),
            in_specs=[pl.BlockSpec((n, tr, C), lambda i, c: (0, i, 0)), ANY],
            out_specs=pl.BlockSpec((1, 1, tr, C), lambda i, c: (layer, c[0], i, 0))),
        out_shape=jax.ShapeDtypeStruct(dest.shape, F32),
        input_output_aliases={2: 0},
        compiler_params=_params(("parallel",)),
    )(c_idx, r, dest)


class _LayerReduce:
    def __init__(self, layer, grads, slices, c_idx, chip, dests):
        self.layer, self.grads, self.slices, self.c_idx, self.chip, self.dests = layer, grads, slices, c_idx, chip, dests
        self.widths = [grads[where[0][0]].shape[1] for _, where in slices]
        self.stage = 0

    def _hop(self, kind):
        return _GradHop(kind, self.layer, self.slices)

    def rider(self):
        if self.stage == 0:
            got = [lax.empty((N_CHIPS, half, w), F32) for (half, _), w in zip(self.slices, self.widths)]
            return self._hop("pair"), self.grads, got
        if self.stage == 1:
            q = [lax.empty(p.shape, BF16) for p in self.pair_sums]
            return self._hop("chips"), self.pair_sums, q
        return self._hop("gather"), [], self.dests

    def landed(self, bufs):
        tag = f"{self.layer}"
        if self.stage == 0:
            self.pair_sums = []
            for t, ((half, where), got) in enumerate(zip(self.slices, bufs)):
                parts = [(self.grads[a], first) for a, first in where]
                p = _add_halves(parts, got[:, None], self.c_idx, f"rs_add2_{t}_{tag}")
                self.pair_sums.append(p.reshape(N_CHIPS, half, p.shape[-1]))
        elif self.stage == 1:
            for t, (q, p) in enumerate(zip(bufs, self.pair_sums)):
                q = lax.dynamic_update_slice_in_dim(q, lax.dynamic_slice_in_dim(p, self.chip, 1, axis=0), self.chip, axis=0)
                self.dests[t] = _add_slots(q, self.c_idx, self.dests[t], self.layer, f"rs_add4_{t}_{tag}")
        else:
            self.dests = list(bufs)
        self.stage += 1

    def step(self):
        hop, srcs, bufs = self.rider()
        self.landed(hop.call(srcs, bufs, f"rs_{hop.hop}_{self.layer}"))

    def finish(self):
        while self.stage < 3:
            self.step()
        return self.dests


def _all_reduce_small(buf, name):
    R = buf.shape[0]

    def flipped(k, x, y, c):
        return ((1 - x) if k & 4 else x, (1 - y) if k & 2 else y, (1 - c) if k & 1 else c)

    def body(b_ref, o_ref, land, send, recv):
        x, y, c, _ = _place()
        me = 4 * x + 2 * y + c
        land[me] = b_ref[...]
        cps = []
        for k in range(1, N_DEV):
            peer = flipped(k, x, y, c)
            cps.append(pltpu.make_async_remote_copy(src_ref=b_ref, dst_ref=land.at[me], send_sem=send.at[k - 1],
                                                    recv_sem=recv.at[k - 1], device_id=peer, device_id_type=MESH))
        for cp in cps:
            cp.start()
        for k in range(1, N_DEV):
            peer = flipped(k, x, y, c)
            slot = 4 * peer[0] + 2 * peer[1] + peer[2]
            pltpu.make_async_remote_copy(src_ref=b_ref, dst_ref=land.at[slot], send_sem=send.at[k - 1],
                                         recv_sem=recv.at[k - 1], device_id=peer, device_id_type=MESH).wait_recv()
        for cp in cps:
            cp.wait_send()
        acc = land[0]
        for j in range(1, N_DEV):
            acc = acc + land[j]
        o_ref[...] = acc

    vm = pl.BlockSpec(memory_space=pltpu.VMEM)
    return pl.pallas_call(
        body, name=name, in_specs=[vm], out_specs=vm,
        out_shape=jax.ShapeDtypeStruct((R, LANES), F32),
        scratch_shapes=[pltpu.VMEM((N_DEV, R, LANES), F32), pltpu.SemaphoreType.DMA((N_DEV - 1,)),
                        pltpu.SemaphoreType.DMA((N_DEV - 1,))],
    )(buf)


def kernel(x, mem, w_in, a_rel_bias, b_gate_w, b_gate_b, b_norm_g, w_mem_kv, w_out, ln_g, ln_b, loss_target, m_w_in, m_a_rel_bias, m_b_gate_w, m_b_gate_b, m_b_norm_g, m_w_mem_kv, m_w_out, m_ln_g, m_ln_b, v_w_in, v_a_rel_bias, v_b_gate_w, v_b_gate_b, v_b_norm_g, v_w_mem_kv, v_w_out, v_ln_g, v_ln_b):
    L = w_in.shape[0]
    S = x.shape[1]
    cx, cy, cc = lax.axis_index("x"), lax.axis_index("y"), lax.axis_index("c")
    chip = 2 * cx + cy
    c_idx = jnp.reshape(cc, (1,)).astype(jnp.int32)

    n_in, r_kv, r_out = w_in.shape[2], w_mem_kv.shape[1], w_out.shape[1]
    shards = [w.astype(BF16).reshape(-1, w.shape[2]) for w in (w_in, w_mem_kv, w_out)]
    rows = [D_MODEL, r_kv, r_out]

    def landing(l):
        return [lax.dynamic_update_slice_in_dim(lax.empty((N_CHIPS, r, s.shape[1]), BF16),
                                                s[l * r:(l + 1) * r][None], chip, axis=0)
                for s, r in zip(shards, rows)]

    def assembled(bufs):
        return (_padded_from_shards([bufs[0][j] for j in range(N_CHIPS)]),
                bufs[1].reshape(D_MODEL, bufs[1].shape[2]), bufs[2].reshape(D_MODEL, D_MODEL))

    bufs0 = _WeightGather("chips", 0, rows).call(shards, landing(0), "gather_chips_0")
    weights = [assembled(_WeightGather("pair", 0, rows).call(shards, bufs0, "gather_pair_0"))]

    gw_cols = b_gate_w.shape[2]
    gw_slot = jnp.zeros((N_CHIPS, L, GATE_RANK, gw_cols), F32)
    gw_slot = lax.dynamic_update_slice(gw_slot, (0.5 * b_gate_w)[None], (chip, 0, 0, 0))
    gw_flat = gw_slot.reshape(-1)
    n_gw = gw_flat.shape[0]
    pad = (-n_gw) % (8 * LANES)
    gw_full = _all_reduce_small(jnp.pad(gw_flat, (0, pad)).reshape(-1, LANES), "gather_gate_w").reshape(-1)[:n_gw]
    gw_full = gw_full.reshape(N_CHIPS, L, GATE_RANK, gw_cols).transpose(1, 2, 0, 3).reshape(L, GATE_RANK, B_KEY_WIDTH)

    xs = x.reshape(S, D_MODEL)
    mem_b = mem.reshape(mem.shape[1], D_MODEL).astype(BF16)
    target = loss_target.reshape(S, D_MODEL)

    small_w = []
    for l in range(L):
        gw_l, gb_l = _pad_gate(gw_full[l], b_gate_b[l])
        small_w.append((_bias_by_offset(a_rel_bias[l]), gw_l, gb_l,
                        b_norm_g[l].reshape(1, LANES), ln_g[l].reshape(1, D_MODEL), ln_b[l].reshape(1, D_MODEL)))

    y = xs
    yb, yt = _cast_transpose(xs, "x_t")
    saved = []
    for l in range(L):
        if l + 1 < L:
            rider = (_WeightGather("chips", l + 1, rows), shards, landing(l + 1))
            y, yb, yt, sv, bufs = _layer_fwd(y, yb, yt, mem_b, *weights[l], *small_w[l], rider=rider,
                                             next_hop=_WeightGather("pair", l + 1, rows))
            weights.append(assembled(bufs))
        else:
            y, yb, yt, sv = _layer_fwd(y, yb, yt, mem_b, *weights[l], *small_w[l])
        saved.append(sv)
    layer_w = [(*weights[l], *small_w[l]) for l in range(L)]
    loss_part, dy = _loss_head(y, target)

    halves = [D_MODEL // 2, r_kv // 2, r_out // 2]
    dests = [lax.empty((L, 2, hf, w.shape[2]), F32) for hf, w in zip(halves, (w_in, w_mem_kv, w_out))]
    slices = [(halves[0], [(j, 0) for j in range(N_CHIPS)]),
              (halves[1], [(N_CHIPS, j * r_kv) for j in range(N_CHIPS)]),
              (halves[2], [(N_CHIPS + 1, j * r_out) for j in range(N_CHIPS)])]

    def reduction(l, g, into):
        arrays = [_chip_columns(g[0], j, n_in) for j in range(N_CHIPS)] + [g[5], g[6]]
        return _LayerReduce(l, arrays, slices, c_idx, chip, into)

    grads, reduce = [None] * L, None
    for l in reversed(range(L)):
        w_in_l, w_kv_l, w_out_l, u_l, gw_l, gb_l, gn_l, lg_l, lb_l = layer_w[l]
        args = (dy, saved[l], mem_b, w_in_l, w_out_l, u_l, gw_l, gb_l, gn_l, lg_l)
        if l > 0:
            dy, grads[l] = _layer_bwd(*args, reduce=reduce)
            if reduce is not None:
                dests = reduce.finish()
            reduce = reduction(l, grads[l], dests)
        else:
            own = lambda g, above: reduction(0, g, dests if above is None else above)
            dy, grads[l], reduce = _layer_bwd(*args, reduce=reduce, own_reduce=own)
    r_w_in, r_w_kv, r_w_out = [d.reshape(L, 2 * d.shape[2], d.shape[3]) for d in reduce.finish()]
    grad_x = dy.reshape(x.shape)

    g_rel = jnp.stack([_bias_grad_from_offset(g[1]) for g in grads])
    g_gw = jnp.stack([_unpad_heads(g[2][LR_LANE:LR_LANE + GATE_RANK]) for g in grads])
    g_gb = jnp.stack([_unpad_heads(g[3])[0] for g in grads])
    g_gn = jnp.stack([g[4][0] for g in grads])
    g_lg = jnp.stack([g[7][0] for g in grads])
    g_lb = jnp.stack([g[8][0] for g in grads])

    small = [g_rel, g_gw, g_gb, g_gn, g_lg, g_lb, loss_part]
    flat = jnp.concatenate([s.reshape(-1) for s in small])
    n_small = flat.shape[0]
    pad = (-n_small) % (8 * LANES)
    red = _all_reduce_small(jnp.pad(flat, (0, pad)).reshape(-1, LANES), "all_reduce_small").reshape(-1)
    outs, off = [], 0
    for s in small:
        outs.append(red[off:off + s.size].reshape(s.shape))
        off += s.size
    g_rel, g_gw, g_gb, g_gn, g_lg, g_lb, loss = outs
    loss = loss.reshape(())
    g_gw = lax.dynamic_slice_in_dim(g_gw.reshape(L, GATE_RANK, N_CHIPS, gw_cols), chip, 1, axis=2).reshape(L, GATE_RANK, gw_cols)

    g_list = [r_w_in, g_rel, g_gw, g_gb, g_gn, r_w_kv, r_w_out, g_lg, g_lb]
    w_list = [w_in, a_rel_bias, b_gate_w, b_gate_b, b_norm_g, w_mem_kv, w_out, ln_g, ln_b]
    m_list = [m_w_in, m_a_rel_bias, m_b_gate_w, m_b_gate_b, m_b_norm_g, m_w_mem_kv, m_w_out, m_ln_g, m_ln_b]
    v_list = [v_w_in, v_a_rel_bias, v_b_gate_w, v_b_gate_b, v_b_norm_g, v_w_mem_kv, v_w_out, v_ln_g, v_ln_b]
    names = ["w_in", "rel", "gate_w", "gate_b", "norm_g", "w_kv", "w_out", "ln_g", "ln_b"]
    to_cols = lambda a: jnp.transpose(a, (2, 0, 1))
    upd = [tuple(jnp.transpose(o, (1, 2, 0)) for o in
                 _adamw(to_cols(w_in), to_cols(r_w_in), to_cols(m_w_in), to_cols(v_w_in), "adamw_w_in"))]
    upd += [_adamw_nd(w, g, m, v, "adamw_" + n)
            for w, g, m, v, n in list(zip(w_list, g_list, m_list, v_list, names))[1:]]
    deltas = [u_[0] for u_ in upd]
    new_m = [u_[1] for u_ in upd]
    new_v = [u_[2] for u_ in upd]
    return (loss, grad_x, *g_list, *deltas, *new_m, *new_v)
```

```python
import functools

import numpy as np
import jax
import jax.numpy as jnp
from jax import lax
from jax.experimental import pallas as pl
from jax.experimental.pallas import tpu as pltpu

F32 = jnp.float32
BF16 = jnp.bfloat16
MESH = pl.DeviceIdType.MESH

D_MODEL = 2048
DEPTH = 4
CHUNK = 64
LEFT_CHUNKS = 8
MAX_REL = 128
N_REL = 2 * MAX_REL + 1
A_HEADS = 8
HEAD_DIM = 128
B_HEADS = 4
B_DK = 64
M_HEADS = 4
GATE_RANK = 16
GATE_TAU = 16.0
A_WIDTH = A_HEADS * HEAD_DIM
B_WIDTH = B_HEADS * HEAD_DIM
B_KEY_WIDTH = B_HEADS * B_DK
M_WIDTH = M_HEADS * HEAD_DIM
IN_WIDTH = 4 * A_WIDTH + 2 * B_KEY_WIDTH + 2 * B_WIDTH + GATE_RANK + 2 * M_WIDTH
ALPHA = (2.0 * DEPTH) ** 0.25
LN_EPS = 1e-5
RMS_EPS = 1e-6
NEG_INF = -1e30
ADAM_LR = 0.001
ADAM_B1 = 0.9
ADAM_B2 = 0.999
ADAM_EPS = 1e-08
ADAM_WD = 0.01
ADAM_STEP = 10

LANES = 128
VMEM_LIMIT = 56 * 1024 * 1024

C_A, C_B, C_M = 0, 4096, 6144
HP = 7168
LR_HEAD, LR_LANE = B_HEADS - 1, B_DK
C_LR = C_B + LR_HEAD * 4 * LANES
SEG_Q, SEG_K, SEG_V, SEG_Z = 0, 1, 2, 3
TQ = 512
CPB = TQ // CHUNK
N_CHIPS = 4
N_DEV = 8


def _params(sem, vmem=VMEM_LIMIT):
    return pltpu.CompilerParams(dimension_semantics=sem, vmem_limit_bytes=vmem)


def _dot(a, b):
    return jnp.dot(a, b, preferred_element_type=F32)


def _dot_nt(a, b):
    return lax.dot_general(a, b, (((1,), (1,)), ((), ())), preferred_element_type=F32)


def _dot_tn(a, b):
    return lax.dot_general(a, b, (((0,), (0,)), ((), ())), preferred_element_type=F32)


def _sigmoid(x):
    return 1.0 / (1.0 + jnp.exp(-x))


def _split3(x):
    hi = x.astype(BF16)
    r = x - hi.astype(F32)
    mid = r.astype(BF16)
    lo = (r - mid.astype(F32)).astype(BF16)
    return hi, mid, lo


def _dot3(m_bf, x):
    hi, mid, lo = _split3(x)
    return _dot(m_bf, hi) + _dot(m_bf, mid) + _dot(m_bf, lo)


def _matmul(a, b, *, mode, out_dtype, tm, tn, tk, name, add=None, add_scale=1.0, rider=None):
    if mode == "nn":
        (M, K), (K2, N) = a.shape, b.shape
        a_spec = pl.BlockSpec((tm, tk), lambda i, j, k: (i, k))
        b_spec = pl.BlockSpec((tk, tn), lambda i, j, k: (k, j))
        dot = _dot
    elif mode == "nt":
        (M, K), (N, K2) = a.shape, b.shape
        a_spec = pl.BlockSpec((tm, tk), lambda i, j, k: (i, k))
        b_spec = pl.BlockSpec((tn, tk), lambda i, j, k: (j, k))
        dot = _dot_nt
    else:
        (K, M), (K2, N) = a.shape, b.shape
        a_spec = pl.BlockSpec((tk, tm), lambda i, j, k: (k, i))
        b_spec = pl.BlockSpec((tk, tn), lambda i, j, k: (k, j))
        dot = _dot_tn
    assert K == K2 and M % tm == 0 and N % tn == 0 and K % tk == 0, (a.shape, b.shape, mode)
    nk = K // tk
    has_add = add is not None
    assert nk == 1 or out_dtype == F32
    grid = (M // tm, N // tn, nk)
    hop, srcs, bufs = rider if rider is not None else (None, [], [])
    n_in = 2 + has_add

    def body(*refs):
        a_ref, b_ref = refs[:2]
        add_ref = refs[2] if has_add else None
        src_refs = refs[n_in:n_in + len(srcs)]
        o_ref = refs[n_in + len(srcs) + len(bufs)]
        buf_refs = refs[n_in + len(srcs) + len(bufs) + 1:n_in + len(srcs) + 2 * len(bufs) + 1]
        sems = refs[n_in + len(srcs) + 2 * len(bufs) + 1:]
        i, j, k = pl.program_id(0), pl.program_id(1), pl.program_id(2)
        if hop is not None:
            @pl.when(jnp.logical_and(jnp.logical_and(i == 0, j == 0), k == 0))
            def _():
                hop.start(src_refs, buf_refs, *sems)

        part = dot(a_ref[...].astype(BF16), b_ref[...].astype(BF16))

        @pl.when(k == 0)
        def _():
            first = part + add_scale * add_ref[...] if has_add else part
            o_ref[...] = first.astype(out_dtype)

        if nk > 1:
            @pl.when(k > 0)
            def _():
                o_ref[...] += part

        if hop is not None:
            @pl.when(jnp.logical_and(jnp.logical_and(i == grid[0] - 1, j == grid[1] - 1), k == nk - 1))
            def _():
                hop.wait(src_refs, buf_refs, *sems)

    in_specs = [a_spec, b_spec]
    args = [a, b]
    if has_add:
        in_specs.append(pl.BlockSpec((tm, tn), lambda i, j, k: (i, j)))
        args.append(add)
    out = pl.pallas_call(
        body,
        name=name,
        grid=grid,
        in_specs=in_specs + [ANY] * (len(srcs) + len(bufs)),
        out_specs=[pl.BlockSpec((tm, tn), lambda i, j, k: (i, j))] + [ANY] * len(bufs),
        out_shape=[jax.ShapeDtypeStruct((M, N), out_dtype)] + [jax.ShapeDtypeStruct(x.shape, x.dtype) for x in bufs],
        input_output_aliases={n_in + len(srcs) + t: 1 + t for t in range(len(bufs))},
        scratch_shapes=hop.sems() if hop is not None else [],
        compiler_params=_params(("parallel", "parallel", "arbitrary") if hop is None
                                else ("arbitrary", "arbitrary", "arbitrary")),
    )(*args, *srcs, *bufs)
    return out[0] if hop is None else (out[0], list(out[1:]))


def _cast_transpose(a, name):
    R, C = a.shape
    t = 512

    def body(a_ref, b_ref, t_ref):
        b = a_ref[...].astype(BF16)
        b_ref[...] = b
        t_ref[...] = b.T

    return pl.pallas_call(
        body, name=name, grid=(R // t, C // t),
        in_specs=[pl.BlockSpec((t, t), lambda i, j: (i, j))],
        out_specs=[pl.BlockSpec((t, t), lambda i, j: (i, j)), pl.BlockSpec((t, t), lambda i, j: (j, i))],
        out_shape=[jax.ShapeDtypeStruct((R, C), BF16), jax.ShapeDtypeStruct((C, R), BF16)],
        compiler_params=_params(("parallel", "parallel")),
    )(a)


def _chunk_of(rows):
    return lax.shift_right_logical(rows, CHUNK.bit_length() - 1)


A_HPS = 2
A_HW = A_HPS * LANES


def _band_bias(u_row, first):
    bias = pltpu.roll(jnp.broadcast_to(u_row, (TQ, 2 * TQ)), 0, 1, stride=1, stride_axis=0)
    qc = _chunk_of(lax.broadcasted_iota(jnp.int32, (TQ, 2 * TQ), 0))
    col = lax.broadcasted_iota(jnp.int32, (TQ, 2 * TQ), 1)
    kc = _chunk_of(jnp.bitwise_and(col, TQ - 1))
    ok = jnp.logical_or(jnp.logical_and(col < TQ, kc >= qc), jnp.logical_and(col >= TQ, kc <= qc))
    return jnp.where(ok, bias, NEG_INF) + jnp.where(col < TQ, first * NEG_INF, 0.0)


HQ = TQ // 2
HALVES = ((slice(0, HQ), slice(0, 3 * HQ)),
          (slice(HQ, TQ), slice(HQ, 4 * HQ)))


def _band_probs(q, kcat, bias):
    scale = HEAD_DIM ** -0.5
    out = []
    for rows, cols in HALVES:
        s = _dot_nt(q[rows], kcat[cols]) * scale + bias[rows, cols]
        p = jnp.exp(s - jnp.max(s, axis=1, keepdims=True))
        out.append((p, 1.0 / jnp.sum(p, axis=1, keepdims=True)))
    return out


def _band_specs(nq):
    def col(seg, h):
        return C_A // A_HW + 4 * h + seg

    q_spec = pl.BlockSpec((TQ, A_HW), lambda h, i: (jnp.minimum(i, nq - 1), col(SEG_Q, h)))
    kp_spec = pl.BlockSpec((TQ, A_HW), lambda h, i: (jnp.clip(i - 1, 0, nq - 1), col(SEG_K, h)))
    kc_spec = pl.BlockSpec((TQ, A_HW), lambda h, i: (jnp.minimum(i, nq - 1), col(SEG_K, h)))
    vp_spec = pl.BlockSpec((TQ, A_HW), lambda h, i: (jnp.clip(i - 1, 0, nq - 1), col(SEG_V, h)))
    vc_spec = pl.BlockSpec((TQ, A_HW), lambda h, i: (jnp.minimum(i, nq - 1), col(SEG_V, h)))
    z_spec = pl.BlockSpec((TQ, A_HW), lambda h, i: (jnp.minimum(i, nq - 1), col(SEG_Z, h)))
    u_spec = pl.BlockSpec((A_HPS, 1, 2 * TQ), lambda h, i: (h, 0, 0))
    return q_spec, kp_spec, kc_spec, vp_spec, vc_spec, z_spec, u_spec


def _band_fwd(h, u):
    S = h.shape[0]
    nq = S // TQ

    def body(q_ref, kp_ref, kc_ref, vp_ref, vc_ref, z_ref, u_ref, y_ref, yt_ref, bias_scr):
        i = pl.program_id(1)

        @pl.when(i <= 1)
        def _():
            for hh in range(A_HPS):
                bias_scr[hh] = _band_bias(u_ref[hh], (i == 0).astype(F32))

        for hh in range(A_HPS):
            cs = slice(hh * LANES, (hh + 1) * LANES)
            kcat = jnp.concatenate([kp_ref[:, cs], kc_ref[:, cs]], axis=0)
            vcat = jnp.concatenate([vp_ref[:, cs], vc_ref[:, cs]], axis=0)
            probs = _band_probs(q_ref[:, cs], kcat, bias_scr[hh])
            o = jnp.concatenate([_dot(p.astype(BF16), vcat[cols]) * inv
                                 for (p, inv), (_, cols) in zip(probs, HALVES)], axis=0)
            z = z_ref[:, cs].astype(F32)
            y = o * (z * _sigmoid(z))
            y_ref[:, cs] = y.astype(BF16)
            yt_ref[cs, :] = y.T.astype(BF16)

    specs = _band_specs(nq)
    return pl.pallas_call(
        body,
        name="band_fwd",
        grid=(A_HEADS // A_HPS, nq),
        in_specs=[specs[0], specs[1], specs[2], specs[3], specs[4], specs[5], specs[6]],
        out_specs=[pl.BlockSpec((TQ, A_HW), lambda h, i: (i, h)), pl.BlockSpec((A_HW, TQ), lambda h, i: (h, i))],
        out_shape=[jax.ShapeDtypeStruct((S, D_MODEL), BF16), jax.ShapeDtypeStruct((D_MODEL, S), BF16)],
        scratch_shapes=[pltpu.VMEM((A_HPS, TQ, 2 * TQ), F32)],
        compiler_params=_params(("parallel", "arbitrary")),
    )(h, h, h, h, h, h, u)


def _band_bwd(h, u, dycat):
    S = h.shape[0]
    nq = S // TQ
    scale = HEAD_DIM ** -0.5
    qs, ks, vs, zs = (slice(s * A_HW, (s + 1) * A_HW) for s in (SEG_Q, SEG_K, SEG_V, SEG_Z))

    def body(q_ref, kp_ref, kc_ref, vp_ref, vc_ref, z_ref, u_ref, dy_ref,
             dh_ref, du_ref, bias_scr, db_scr, ckt_scr, cvt_scr, cq_scr, cz_scr):
        i = pl.program_id(1)

        @pl.when(i <= 1)
        def _():
            for hh in range(A_HPS):
                bias_scr[hh] = _band_bias(u_ref[hh], (i == 0).astype(F32))

        @pl.when(i == 0)
        def _():
            db_scr[...] = jnp.zeros_like(db_scr)
            ckt_scr[...] = jnp.zeros_like(ckt_scr)
            cvt_scr[...] = jnp.zeros_like(cvt_scr)
            cq_scr[...] = jnp.zeros_like(cq_scr)
            cz_scr[...] = jnp.zeros_like(cz_scr)

        @pl.when(i < nq)
        def _():
            dh_ref[:, qs] = cq_scr[...]
            dh_ref[:, zs] = cz_scr[...]
            for hh in range(A_HPS):
                cs = slice(hh * LANES, (hh + 1) * LANES)
                q = q_ref[:, cs]
                kcat = jnp.concatenate([kp_ref[:, cs], kc_ref[:, cs]], axis=0)
                vcat = jnp.concatenate([vp_ref[:, cs], vc_ref[:, cs]], axis=0)
                probs = [p * inv for p, inv in _band_probs(q, kcat, bias_scr[hh])]
                o = jnp.concatenate([_dot(p.astype(BF16), vcat[cols]) for p, (_, cols) in zip(probs, HALVES)], axis=0)
                z = z_ref[:, cs].astype(F32)
                sg = _sigmoid(z)
                dy = dy_ref[:, cs].astype(F32)
                do = dy * (z * sg)
                cz_scr[:, cs] = (dy * o * (sg * (1.0 + z * (1.0 - sg)))).astype(BF16)
                dob = do.astype(BF16)
                delta = jnp.sum(do * o, axis=1, keepdims=True)
                qt, dot_ = q.T, dob.T
                dq, dkt, dvt = [], [], []
                for p, (rows, cols) in zip(probs, HALVES):
                    ds = p * (_dot_nt(dob[rows], vcat[cols]) - delta[rows])
                    db_scr[hh, rows, cols] += ds
                    dsb = ds.astype(BF16)
                    dq.append(scale * _dot(dsb, kcat[cols]))
                    dkt.append(scale * _dot(qt[:, rows], dsb))
                    dvt.append(_dot(dot_[:, rows], p.astype(BF16)))
                cq_scr[:, cs] = jnp.concatenate(dq, axis=0).astype(BF16)

                def over_keys(parts):
                    lo, hi = parts
                    prev = jnp.concatenate([lo[:, :HQ], lo[:, HQ:2 * HQ] + hi[:, :HQ]], axis=1)
                    cur = jnp.concatenate([lo[:, 2 * HQ:] + hi[:, HQ:2 * HQ], hi[:, 2 * HQ:]], axis=1)
                    return prev, cur

                (dk_prev, dk_cur), (dv_prev, dv_cur) = over_keys(dkt), over_keys(dvt)
                dh_ref[:, SEG_K * A_HW + hh * LANES:SEG_K * A_HW + (hh + 1) * LANES] = (
                    ckt_scr[cs, :] + dk_prev).T.astype(BF16)
                dh_ref[:, SEG_V * A_HW + hh * LANES:SEG_V * A_HW + (hh + 1) * LANES] = (
                    cvt_scr[cs, :] + dv_prev).T.astype(BF16)
                ckt_scr[cs, :] = dk_cur
                cvt_scr[cs, :] = dv_cur

        @pl.when(i == nq)
        def _():
            dh_ref[:, qs] = cq_scr[...]
            dh_ref[:, zs] = cz_scr[...]
            dh_ref[:, ks] = ckt_scr[...].T.astype(BF16)
            dh_ref[:, vs] = cvt_scr[...].T.astype(BF16)
            r0 = lax.broadcasted_iota(jnp.int32, (TQ, TQ), 0)
            r1 = lax.broadcasted_iota(jnp.int32, (TQ, TQ), 1)
            flip = (r0 + r1 == TQ - 1).astype(BF16)
            for hh in range(A_HPS):
                fl = _dot3(flip, db_scr[hh])
                rolled = pltpu.roll(fl, 0, 1, stride=1, stride_axis=0)
                du_ref[hh] = jnp.sum(rolled, axis=0, keepdims=True)

    specs = _band_specs(nq)
    dy_spec = pl.BlockSpec((TQ, A_HW), lambda h, i: (jnp.minimum(i, nq - 1), h))
    return pl.pallas_call(
        body,
        name="band_bwd",
        grid=(A_HEADS // A_HPS, nq + 1),
        in_specs=[specs[0], specs[1], specs[2], specs[3], specs[4], specs[5], specs[6], dy_spec],
        out_specs=[pl.BlockSpec((TQ, 4 * A_HW), lambda h, i: (jnp.maximum(i - 1, 0), C_A // (4 * A_HW) + h)),
                   pl.BlockSpec((A_HPS, 1, 2 * TQ), lambda h, i: (h, 0, 0))],
        out_shape=[jax.ShapeDtypeStruct((S, HP), BF16), jax.ShapeDtypeStruct((A_HEADS, 1, 2 * TQ), F32)],
        scratch_shapes=[pltpu.VMEM((A_HPS, TQ, 2 * TQ), F32), pltpu.VMEM((A_HPS, TQ, 2 * TQ), F32),
                        pltpu.VMEM((A_HW, TQ), F32), pltpu.VMEM((A_HW, TQ), F32),
                        pltpu.VMEM((TQ, A_HW), BF16), pltpu.VMEM((TQ, A_HW), BF16)],
        compiler_params=_params(("parallel", "arbitrary")),
    )(h, h, h, h, h, h, u, dycat)


def _bias_by_offset(table):
    far = jnp.broadcast_to(table[:, N_REL - 1:], (A_HEADS, TQ - MAX_REL))
    ramp = jnp.flip(table, axis=1)
    rest = jnp.broadcast_to(table[:, :1], (A_HEADS, 2 * TQ - CHUNK - (TQ + MAX_REL + 1)))
    wrap = jnp.broadcast_to(table[:, N_REL - 1:], (A_HEADS, CHUNK))
    return jnp.concatenate([far, ramp, rest, wrap], axis=1)[:, None, :]


def _bias_grad_from_offset(du):
    g = jnp.roll(du[:, 0, :], -(TQ - 1), axis=1)
    far = jnp.sum(g[:, :TQ - MAX_REL], axis=1) + jnp.sum(g[:, 2 * TQ - CHUNK:], axis=1)
    ramp = jnp.flip(g[:, TQ - MAX_REL:TQ + MAX_REL + 1], axis=1)
    return ramp.at[:, N_REL - 1].add(far)


def _mem_probs(q, mk):
    s = _dot_nt(q, mk) * (HEAD_DIM ** -0.5)
    p = jnp.exp(s - jnp.max(s, axis=1, keepdims=True))
    return p * (1.0 / jnp.sum(p, axis=1, keepdims=True))


def _mem_cols(hd, seg):
    return slice((2 * hd + seg) * LANES, (2 * hd + seg + 1) * LANES)


def _mem_fwd(h, mkv, ycat, ycat_t):
    S = h.shape[0]
    nm = mkv.shape[0]
    c0 = (A_WIDTH + B_WIDTH) // LANES

    def body(qz_ref, mkv_ref, yin_ref, ytin_ref, y_ref, yt_ref):
        del yin_ref, ytin_ref
        for hd in range(M_HEADS):
            lane = slice(hd * LANES, (hd + 1) * LANES)
            p = _mem_probs(qz_ref[:, _mem_cols(hd, 0)], mkv_ref[:, lane])
            o = _dot(p.astype(BF16), mkv_ref[:, M_WIDTH + hd * LANES:M_WIDTH + (hd + 1) * LANES])
            z = qz_ref[:, _mem_cols(hd, 1)].astype(F32)
            y = o * (z * _sigmoid(z))
            y_ref[:, lane] = y.astype(BF16)
            yt_ref[lane, :] = y.T.astype(BF16)

    return pl.pallas_call(
        body,
        name="mem_fwd",
        grid=(S // TQ,),
        in_specs=[pl.BlockSpec((TQ, 2 * M_WIDTH), lambda i: (i, C_M // (2 * M_WIDTH))),
                  pl.BlockSpec((nm, 2 * M_WIDTH), lambda i: (0, 0)), ANY, ANY],
        out_specs=[pl.BlockSpec((TQ, M_WIDTH), lambda i: (i, c0 // M_HEADS)),
                   pl.BlockSpec((M_WIDTH, TQ), lambda i: (c0 // M_HEADS, i))],
        out_shape=[jax.ShapeDtypeStruct(ycat.shape, BF16), jax.ShapeDtypeStruct(ycat_t.shape, BF16)],
        input_output_aliases={2: 0, 3: 1},
        compiler_params=_params(("parallel",)),
    )(h, mkv, ycat, ycat_t)


def _mem_bwd(h, mkv, dycat, dh):
    S = h.shape[0]
    nm = mkv.shape[0]
    scale = HEAD_DIM ** -0.5

    def body(qz_ref, mkv_ref, dy_ref, dhin_ref, dh_ref, dmkv_ref):
        del dhin_ref
        i = pl.program_id(0)
        for hd in range(M_HEADS):
            lane = slice(hd * LANES, (hd + 1) * LANES)
            lane_v = slice(M_WIDTH + hd * LANES, M_WIDTH + (hd + 1) * LANES)
            q, mk, mv = qz_ref[:, _mem_cols(hd, 0)], mkv_ref[:, lane], mkv_ref[:, lane_v]
            p = _mem_probs(q, mk)
            pb = p.astype(BF16)
            o = _dot(pb, mv)
            z = qz_ref[:, _mem_cols(hd, 1)].astype(F32)
            sg = _sigmoid(z)
            dy = dy_ref[:, lane].astype(F32)
            do = dy * (z * sg)
            dh_ref[:, _mem_cols(hd, 1)] = (dy * o * (sg * (1.0 + z * (1.0 - sg)))).astype(BF16)
            dob = do.astype(BF16)
            ds = p * (_dot_nt(dob, mv) - jnp.sum(do * o, axis=1, keepdims=True))
            dsb = ds.astype(BF16)
            dh_ref[:, _mem_cols(hd, 0)] = (scale * _dot(dsb, mk)).astype(BF16)
            dmk = (scale * _dot(q.T, dsb)).T
            dmv = _dot(dob.T, pb).T

            @pl.when(i == 0)
            def _():
                dmkv_ref[:, lane] = dmk
                dmkv_ref[:, lane_v] = dmv

            @pl.when(i > 0)
            def _():
                dmkv_ref[:, lane] += dmk
                dmkv_ref[:, lane_v] += dmv

    return pl.pallas_call(
        body,
        name="mem_bwd",
        grid=(S // TQ,),
        in_specs=[pl.BlockSpec((TQ, 2 * M_WIDTH), lambda i: (i, C_M // (2 * M_WIDTH))),
                  pl.BlockSpec((nm, 2 * M_WIDTH), lambda i: (0, 0)),
                  pl.BlockSpec((TQ, M_WIDTH), lambda i: (i, (A_WIDTH + B_WIDTH) // M_WIDTH)), ANY],
        out_specs=[pl.BlockSpec((TQ, 2 * M_WIDTH), lambda i: (i, C_M // (2 * M_WIDTH))),
                   pl.BlockSpec((nm, 2 * M_WIDTH), lambda i: (0, 0))],
        out_shape=[jax.ShapeDtypeStruct(dh.shape, BF16), jax.ShapeDtypeStruct((nm, 2 * M_WIDTH), F32)],
        input_output_aliases={3: 0},
        compiler_params=_params(("arbitrary",)),
    )(h, mkv, dycat, dh)


SUB = 2 * CHUNK
SUBS = [slice(s * SUB, (s + 1) * SUB) for s in range(TQ // SUB)]


def _chunk_masks():
    r = lax.broadcasted_iota(jnp.int32, (SUB, SUB), 0)
    c = lax.broadcasted_iota(jnp.int32, (SUB, SUB), 1)
    same = _chunk_of(r) == _chunk_of(c)
    return jnp.logical_and(same, c <= r), jnp.logical_and(same, c > r), jnp.logical_and(same, c >= r)


def _by_sub(fn):
    return jnp.concatenate([fn(rows) for rows in SUBS], axis=0)


def _gla_gates(lr, gw, gb):
    logit = _dot(lr, gw) + gb
    sg = _sigmoid(logit)
    g = (jnp.minimum(logit, 0.0) - jnp.log(1.0 + jnp.exp(-jnp.abs(logit)))) * (1.0 / GATE_TAU)
    lo = _chunk_masks()[0].astype(BF16)
    return sg, _by_sub(lambda rows: _dot3(lo, g[rows]))


def _gla_factors(q, k, b):
    eb = jnp.exp(b)
    enb = jnp.exp(-b)
    return eb, enb, q * eb, q * enb, k * eb, k * enb


def _gla_intra(qp, qn, kp, kn):
    lo, up, _ = _chunk_masks()
    qp, qn, kp, kn = qp.astype(BF16), qn.astype(BF16), kp.astype(BF16), kn.astype(BF16)
    return [jnp.where(lo, _dot_nt(qp[rows], kn[rows]), 0.0) + jnp.where(up, _dot_nt(qn[rows], kp[rows]), 0.0)
            for rows in SUBS]


B_HPS = 2
B_HW = B_HPS * LANES


def _gla_specs(nb, rev):
    blk = (lambda i: nb - 1 - i) if rev else (lambda i: i)
    qkvz_spec = pl.BlockSpec((TQ, 4 * B_HW), lambda i, p: (blk(i), C_B // (4 * B_HW) + p))
    lr_spec = pl.BlockSpec((TQ, LANES), lambda i, p: (blk(i), C_LR // LANES))
    gw_spec = pl.BlockSpec((LANES, B_HW), lambda i, p: (0, p))
    gb_spec = pl.BlockSpec((1, B_HW), lambda i, p: (0, p))
    gn_spec = pl.BlockSpec((1, LANES), lambda i, p: (0, 0))
    return qkvz_spec, lr_spec, gw_spec, gb_spec, gn_spec, blk


def _head_cols(hh, seg):
    return slice((4 * hh + seg) * LANES, (4 * hh + seg + 1) * LANES)


def _gla_fwd(h, gw, gb, gn, ycat, ycat_t):
    S = h.shape[0]
    nb = S // TQ
    c0 = A_WIDTH // LANES

    def body(qkvz_ref, lr_ref, gw_ref, gb_ref, gn_ref, yin_ref, ytin_ref,
             y_ref, yt_ref, o_ref, st_ref, st_scr):
        del yin_ref, ytin_ref
        i, p = pl.program_id(0), pl.program_id(1)
        for hh in range(B_HPS):
            hd = B_HPS * p + hh
            lane = slice(hh * LANES, (hh + 1) * LANES)

            @pl.when(i == 0)
            def _():
                st_scr[hd] = jnp.zeros((LANES, LANES), F32)

            q = qkvz_ref[:, _head_cols(hh, SEG_Q)].astype(F32) * (B_DK ** -0.5)
            k = qkvz_ref[:, _head_cols(hh, SEG_K)].astype(F32)
            v = qkvz_ref[:, _head_cols(hh, SEG_V)]
            _, b = _gla_gates(lr_ref[...], gw_ref[:, lane], gb_ref[:, lane])
            _, _, qp, qn, kp, kn = _gla_factors(q, k, b)
            o_intra = jnp.concatenate([_dot(a.astype(BF16), v[rows])
                                       for a, rows in zip(_gla_intra(qp, qn, kp, kn), SUBS)], axis=0)
            qpb, knb = qp.astype(BF16), kn.astype(BF16)
            st = st_scr[hd]
            outs = []
            for c in range(CPB):
                rows = slice(c * CHUNK, (c + 1) * CHUNK)
                st_ref[hh, c] = st
                outs.append(_dot_nt(qpb[rows], st.astype(BF16)))
                e_last = jnp.exp(b[(c + 1) * CHUNK - 1:(c + 1) * CHUNK, :])
                st = (st + _dot_tn(v[rows], knb[rows])) * e_last
            st_scr[hd] = st
            o = o_intra + jnp.concatenate(outs, axis=0)
            o_ref[:, lane] = o
            r = lax.rsqrt(jnp.mean(o * o, axis=1, keepdims=True) + RMS_EPS)
            z = qkvz_ref[:, _head_cols(hh, SEG_Z)].astype(F32)
            y = o * r * gn_ref[...] * (z * _sigmoid(z))
            y_ref[:, lane] = y.astype(BF16)
            yt_ref[lane, :] = y.T.astype(BF16)

    qkvz_s, lr_s, gw_s, gb_s, gn_s, _ = _gla_specs(nb, False)
    return pl.pallas_call(
        body,
        name="gla_fwd",
        grid=(nb, B_HEADS // B_HPS),
        in_specs=[qkvz_s, lr_s, gw_s, gb_s, gn_s, ANY, ANY],
        out_specs=[pl.BlockSpec((TQ, B_HW), lambda i, p: (i, c0 // B_HPS + p)),
                   pl.BlockSpec((B_HW, TQ), lambda i, p: (c0 // B_HPS + p, i)),
                   pl.BlockSpec((TQ, B_HW), lambda i, p: (i, p)),
                   pl.BlockSpec((B_HPS, CPB, LANES, LANES), lambda i, p: (p, i, 0, 0))],
        out_shape=[jax.ShapeDtypeStruct(ycat.shape, BF16), jax.ShapeDtypeStruct(ycat_t.shape, BF16),
                   jax.ShapeDtypeStruct((S, B_WIDTH), F32),
                   jax.ShapeDtypeStruct((B_HEADS, S // CHUNK, LANES, LANES), F32)],
        input_output_aliases={5: 0, 6: 1},
        scratch_shapes=[pltpu.VMEM((B_HEADS, LANES, LANES), F32)],
        compiler_params=_params(("arbitrary", "arbitrary")),
    )(h, h, gw, gb, gn, ycat, ycat_t)


def _gla_bwd(h, gw, gb, gn, o_pre, states, dycat, dh):
    S = h.shape[0]
    nb = S // TQ
    n_steps = B_HEADS // B_HPS

    def body(qkvz_ref, lr_ref, gw_ref, gb_ref, gn_ref, o_ref, st_ref, dy_ref, dhin_ref,
             dh_ref, dgw_ref, dgb_ref, dgn_ref,
             dst_scr, dgw_scr, dgb_scr, dgn_scr, dlr_scr):
        del dhin_ref
        i, p = pl.program_id(0), pl.program_id(1)

        @pl.when(jnp.logical_and(i == 0, p == 0))
        def _():
            dgn_scr[...] = jnp.zeros_like(dgn_scr)

        dlr_heads = [one_head(hh, i, p, qkvz_ref, lr_ref, gw_ref, gb_ref, gn_ref, o_ref, st_ref, dy_ref,
                              dh_ref, dgw_ref, dgb_ref, dst_scr, dgw_scr, dgb_scr, dgn_scr) for hh in range(B_HPS)]
        dlr = dlr_heads[0]
        for more in dlr_heads[1:]:
            dlr = dlr + more

        @pl.when(p == 0)
        def _():
            dlr_scr[...] = dlr

        @pl.when(p > 0)
        def _():
            dlr_scr[...] += dlr

        @pl.when(p == n_steps - 1)
        def _():
            cols = _head_cols(B_HPS - 1, SEG_Q)
            dh_ref[:, cols] = (dh_ref[:, cols].astype(F32) + dlr_scr[...]).astype(BF16)

        @pl.when(i == nb - 1)
        def _():
            dgn_ref[...] = dgn_scr[...]

    def one_head(hh, i, p, qkvz_ref, lr_ref, gw_ref, gb_ref, gn_ref, o_ref, st_ref, dy_ref,
                 dh_ref, dgw_ref, dgb_ref, dst_scr, dgw_scr, dgb_scr, dgn_scr):
        hd = B_HPS * p + hh
        lane = slice(hh * LANES, (hh + 1) * LANES)

        @pl.when(i == 0)
        def _():
            dst_scr[hd] = jnp.zeros((LANES, LANES), F32)
            dgw_scr[hd] = jnp.zeros((LANES, LANES), F32)
            dgb_scr[hd] = jnp.zeros((1, LANES), F32)

        q = qkvz_ref[:, _head_cols(hh, SEG_Q)].astype(F32) * (B_DK ** -0.5)
        k = qkvz_ref[:, _head_cols(hh, SEG_K)].astype(F32)
        v = qkvz_ref[:, _head_cols(hh, SEG_V)]
        lr, gwv = lr_ref[...], gw_ref[:, lane]
        sg, b = _gla_gates(lr, gwv, gb_ref[:, lane])
        eb, enb, qp, qn, kp, kn = _gla_factors(q, k, b)
        a = _gla_intra(qp, qn, kp, kn)
        qpb, qnb, kpb, knb = qp.astype(BF16), qn.astype(BF16), kp.astype(BF16), kn.astype(BF16)

        o = o_ref[:, lane]
        gn = gn_ref[...]
        r = lax.rsqrt(jnp.mean(o * o, axis=1, keepdims=True) + RMS_EPS)
        z = qkvz_ref[:, _head_cols(hh, SEG_Z)].astype(F32)
        sz = _sigmoid(z)
        dy = dy_ref[:, lane].astype(F32)
        d_on = dy * (z * sz)
        dh_ref[:, _head_cols(hh, SEG_Z)] = (dy * (o * r * gn) * (sz * (1.0 + z * (1.0 - sz)))).astype(BF16)
        dgn_scr[...] += jnp.sum(d_on * o * r, axis=0, keepdims=True)
        t = d_on * gn
        do = r * t - o * (r * r * r) * jnp.mean(t * o, axis=1, keepdims=True)
        dob = do.astype(BF16)

        lo, up, upper = _chunk_masks()
        dqp, dkn, dqn, dkp, dv = [], [], [], [], []
        for a_s, rows in zip(a, SUBS):
            da = _dot_nt(dob[rows], v[rows])
            dalo = jnp.where(lo, da, 0.0).astype(BF16)
            daup = jnp.where(up, da, 0.0).astype(BF16)
            dqp.append(_dot(dalo, knb[rows]))
            dkn.append(_dot_tn(dalo, qpb[rows]))
            dqn.append(_dot(daup, kpb[rows]))
            dkp.append(_dot_tn(daup, qnb[rows]))
            dv.append(_dot_tn(a_s.astype(BF16), dob[rows]))
        dqp, dkn, dqn, dkp, dv = (jnp.concatenate(x, axis=0) for x in (dqp, dkn, dqn, dkp, dv))

        dst = dst_scr[hd]
        dqp_c, dkn_c, dv_c, dbl_c = [None] * CPB, [None] * CPB, [None] * CPB, [None] * CPB
        for c in reversed(range(CPB)):
            rows = slice(c * CHUNK, (c + 1) * CHUNK)
            st = st_ref[hh, c]
            e_last = jnp.exp(b[(c + 1) * CHUNK - 1:(c + 1) * CHUNK, :])
            if c == CPB - 1:
                st_next = (st + _dot_tn(v[rows], knb[rows])) * e_last
            else:
                st_next = st_ref[hh, c + 1]
            dbl_c[c] = jnp.sum(dst * st_next, axis=0, keepdims=True)
            dtt = (dst * e_last).astype(BF16)
            dv_c[c] = _dot_nt(knb[rows], dtt)
            dkn_c[c] = _dot(v[rows], dtt)
            dqp_c[c] = _dot(dob[rows], st.astype(BF16))
            dst = _dot_tn(dob[rows], qpb[rows]) + dst * e_last
        dst_scr[hd] = dst
        dqp = dqp + jnp.concatenate(dqp_c, axis=0)
        dkn = dkn + jnp.concatenate(dkn_c, axis=0)
        dv = dv + jnp.concatenate(dv_c, axis=0)
        dh_ref[:, _head_cols(hh, SEG_V)] = dv.astype(BF16)
        dh_ref[:, _head_cols(hh, SEG_Q)] = ((dqp * eb + dqn * enb) * (B_DK ** -0.5)).astype(BF16)
        dh_ref[:, _head_cols(hh, SEG_K)] = (dkp * eb + dkn * enb).astype(BF16)

        last = jnp.bitwise_and(lax.broadcasted_iota(jnp.int32, (TQ, 1), 0), CHUNK - 1) == CHUNK - 1
        dbl = jnp.concatenate([jnp.broadcast_to(x, (CHUNK, LANES)) for x in dbl_c], axis=0)
        db = dqp * qp - dqn * qn + dkp * kp - dkn * kn + jnp.where(last, dbl, 0.0)
        upper_b = upper.astype(BF16)
        dlogit = _by_sub(lambda rows: _dot3(upper_b, db[rows])) * (1.0 / GATE_TAU) * (1.0 - sg)
        dlb = dlogit.astype(BF16)
        dgw_scr[hd] += _dot_tn(lr, dlb)
        dgb_scr[hd] += jnp.sum(dlogit, axis=0, keepdims=True)

        @pl.when(i == nb - 1)
        def _():
            dgw_ref[:, lane] = dgw_scr[hd]
            dgb_ref[:, lane] = dgb_scr[hd]

        return _dot_nt(dlb, gwv)

    qkvz_s, lr_s, gw_s, gb_s, gn_s, blk = _gla_specs(nb, True)
    row = pl.BlockSpec((TQ, B_HW), lambda i, p: (blk(i), p))
    dy_spec = pl.BlockSpec((TQ, B_HW), lambda i, p: (blk(i), A_WIDTH // B_HW + p))
    st_spec = pl.BlockSpec((B_HPS, CPB, LANES, LANES), lambda i, p: (p, blk(i), 0, 0))
    return pl.pallas_call(
        body,
        name="gla_bwd",
        grid=(nb, B_HEADS // B_HPS),
        in_specs=[qkvz_s, lr_s, gw_s, gb_s, gn_s, row, st_spec, dy_spec, ANY],
        out_specs=[pl.BlockSpec((TQ, 4 * B_HW), lambda i, p: (blk(i), C_B // (4 * B_HW) + p)),
                   pl.BlockSpec((LANES, B_HW), lambda i, p: (0, jnp.where(i == nb - 1, p, 0))),
                   pl.BlockSpec((1, B_HW), lambda i, p: (0, jnp.where(i == nb - 1, p, 0))),
                   pl.BlockSpec((1, LANES), lambda i, p: (0, 0))],
        out_shape=[jax.ShapeDtypeStruct(dh.shape, BF16),
                   jax.ShapeDtypeStruct((LANES, B_HEADS * LANES), F32),
                   jax.ShapeDtypeStruct((1, B_HEADS * LANES), F32),
                   jax.ShapeDtypeStruct((1, LANES), F32)],
        input_output_aliases={8: 0},
        scratch_shapes=[pltpu.VMEM((B_HEADS, LANES, LANES), F32), pltpu.VMEM((B_HEADS, LANES, LANES), F32),
                        pltpu.VMEM((B_HEADS, 1, LANES), F32), pltpu.VMEM((1, LANES), F32),
                        pltpu.VMEM((TQ, LANES), F32)],
        compiler_params=_params(("arbitrary", "arbitrary")),
    )(h, h, gw, gb, gn, o_pre, states, dycat, dh)


LN_ROWS = 512


def _resident(shape):
    return pl.BlockSpec(shape, lambda i: (0,) * len(shape), pipeline_mode=pl.Buffered(1))


def _outproj_ln(ycat, w_out, x, g, b, rider=None):
    S = x.shape[0]
    n = S // LN_ROWS
    hop, srcs, bufs = rider if rider is not None else (None, [], [])

    def body(*refs):
        yc_ref, w_ref, x_ref, g_ref, b_ref = refs[:5]
        src_refs = refs[5:5 + len(srcs)]
        outs = refs[5 + len(srcs) + len(bufs):]
        y_ref, yb_ref, yt_ref, xh_ref, rs_ref = outs[:5]
        buf_refs, sems = outs[5:5 + len(bufs)], outs[5 + len(bufs):]
        i = pl.program_id(0)
        if hop is not None:
            @pl.when(i == 0)
            def _():
                hop.start(src_refs, buf_refs, *sems)

        u = ALPHA * x_ref[...] + _dot(yc_ref[...], w_ref[...])
        mu = jnp.mean(u, axis=1, keepdims=True)
        d = u - mu
        rstd = lax.rsqrt(jnp.mean(d * d, axis=1, keepdims=True) + LN_EPS)
        xh = d * rstd
        y = xh * g_ref[...] + b_ref[...]
        y_ref[...] = y
        yb_ref[...] = y.astype(BF16)
        yt_ref[...] = y.T.astype(BF16)
        xh_ref[...] = xh
        rs_ref[...] = rstd
        if hop is not None:
            @pl.when(i == n - 1)
            def _():
                hop.wait(src_refs, buf_refs, *sems)

    row = lambda w: pl.BlockSpec((LN_ROWS, w), lambda i: (i, 0))
    vec = pl.BlockSpec((1, D_MODEL), lambda i: (0, 0))
    n_in = 5 + len(srcs)
    out = pl.pallas_call(
        body,
        name="outproj_ln",
        grid=(n,),
        in_specs=[row(D_MODEL), _resident((D_MODEL, D_MODEL)), row(D_MODEL), vec, vec] + [ANY] * (len(srcs) + len(bufs)),
        out_specs=[row(D_MODEL), row(D_MODEL), pl.BlockSpec((D_MODEL, LN_ROWS), lambda i: (0, i)), row(D_MODEL), row(1)]
        + [ANY] * len(bufs),
        out_shape=[jax.ShapeDtypeStruct((S, D_MODEL), F32), jax.ShapeDtypeStruct((S, D_MODEL), BF16),
                   jax.ShapeDtypeStruct((D_MODEL, S), BF16),
                   jax.ShapeDtypeStruct((S, D_MODEL), F32), jax.ShapeDtypeStruct((S, 1), F32)]
        + [jax.ShapeDtypeStruct(x_.shape, x_.dtype) for x_ in bufs],
        input_output_aliases={n_in + t: 5 + t for t in range(len(bufs))},
        scratch_shapes=hop.sems() if hop is not None else [],
        compiler_params=_params(("parallel",) if hop is None else ("arbitrary",)),
    )(ycat, w_out, x, g, b, *srcs, *bufs)
    return tuple(out[:5]) if hop is None else (tuple(out[:5]), list(out[5:]))


def _ln_bwd_dycat(dy, xhat, rstd, g, w_out, rider=None):
    S = dy.shape[0]
    n = S // LN_ROWS
    hop, srcs, bufs = rider if rider is not None else (None, [], [])

    def body(*refs):
        dy_ref, xh_ref, rs_ref, g_ref, w_ref = refs[:5]
        src_refs = refs[5:5 + len(srcs)]
        outs = refs[5 + len(srcs) + len(bufs):]
        du_ref, dub_ref, dyc_ref, dg_ref, db_ref = outs[:5]
        buf_refs, sems = outs[5:5 + len(bufs)], outs[5 + len(bufs):]
        i = pl.program_id(0)
        if hop is not None:
            @pl.when(i == 0)
            def _():
                hop.start(src_refs, buf_refs, *sems)

        dy_, xh = dy_ref[...], xh_ref[...]
        dyg = dy_ * g_ref[...]
        m1 = jnp.mean(dyg, axis=1, keepdims=True)
        m2 = jnp.mean(dyg * xh, axis=1, keepdims=True)
        du = rs_ref[...] * (dyg - m1 - xh * m2)
        dub = du.astype(BF16)
        du_ref[...] = du
        dub_ref[...] = dub
        dyc_ref[...] = _dot_nt(dub, w_ref[...]).astype(BF16)
        dg = jnp.sum(dy_ * xh, axis=0, keepdims=True)
        db = jnp.sum(dy_, axis=0, keepdims=True)

        @pl.when(i == 0)
        def _():
            dg_ref[...] = dg
            db_ref[...] = db

        @pl.when(i > 0)
        def _():
            dg_ref[...] += dg
            db_ref[...] += db

        if hop is not None:
            @pl.when(i == n - 1)
            def _():
                hop.wait(src_refs, buf_refs, *sems)

    row = lambda w: pl.BlockSpec((LN_ROWS, w), lambda i: (i, 0))
    vec = pl.BlockSpec((1, D_MODEL), lambda i: (0, 0))
    n_in = 5 + len(srcs)
    out = pl.pallas_call(
        body,
        name="ln_bwd_dycat",
        grid=(n,),
        in_specs=[row(D_MODEL), row(D_MODEL), row(1), vec, pl.BlockSpec((D_MODEL, D_MODEL), lambda i: (0, 0))]
        + [ANY] * (len(srcs) + len(bufs)),
        out_specs=[row(D_MODEL), row(D_MODEL), row(D_MODEL), vec, vec] + [ANY] * len(bufs),
        out_shape=[jax.ShapeDtypeStruct((S, D_MODEL), F32), jax.ShapeDtypeStruct((S, D_MODEL), BF16),
                   jax.ShapeDtypeStruct((S, D_MODEL), BF16),
                   jax.ShapeDtypeStruct((1, D_MODEL), F32), jax.ShapeDtypeStruct((1, D_MODEL), F32)]
        + [jax.ShapeDtypeStruct(b.shape, b.dtype) for b in bufs],
        input_output_aliases={n_in + t: 5 + t for t in range(len(bufs))},
        scratch_shapes=hop.sems() if hop is not None else [],
        compiler_params=_params(("arbitrary",)),
    )(dy, xhat, rstd, g, w_out, *srcs, *bufs)
    return tuple(out[:5]) if hop is None else (tuple(out[:5]), list(out[5:]))


def _loss_head(y, target):
    S = y.shape[0]

    def body(y_ref, t_ref, l_ref, dy_ref):
        i = pl.program_id(0)
        err = y_ref[...] - t_ref[...]
        dy_ref[...] = err * (1.0 / D_MODEL)
        part = (0.5 / D_MODEL) * jnp.sum(jnp.sum(err * err, axis=1, keepdims=True), axis=0, keepdims=True)

        @pl.when(i == 0)
        def _():
            l_ref[...] = part

        @pl.when(i > 0)
        def _():
            l_ref[...] += part

    row = pl.BlockSpec((TQ, D_MODEL), lambda i: (i, 0))
    return pl.pallas_call(
        body,
        name="loss_head",
        grid=(S // TQ,),
        in_specs=[row, row],
        out_specs=[pl.BlockSpec((1, 1), lambda i: (0, 0)), row],
        out_shape=[jax.ShapeDtypeStruct((1, 1), F32), jax.ShapeDtypeStruct((S, D_MODEL), F32)],
        compiler_params=_params(("arbitrary",)),
    )(y, target)


def _pad_gate(gate_w, gate_b):
    gw = gate_w.reshape(GATE_RANK, B_HEADS, B_DK)
    gw = jnp.pad(gw, ((LR_LANE, LANES - LR_LANE - GATE_RANK), (0, 0), (0, LANES - B_DK))).reshape(LANES, B_HEADS * LANES)
    gb = jnp.pad(gate_b.reshape(B_HEADS, B_DK), ((0, 0), (0, LANES - B_DK))).reshape(1, B_HEADS * LANES)
    return gw.astype(BF16), gb.astype(F32)


def _layer_fwd(x, xb, xt, mem_b, w_in, w_kv, w_out, u, gw, gb, gn, ln_g, ln_b, rider=None, next_hop=None):
    h = _matmul(xb, w_in, mode="nn", out_dtype=BF16, tm=1024, tn=1792, tk=D_MODEL, name="in_proj", rider=rider)
    if rider is not None:
        h, rode = h
    mkv = _matmul(mem_b, w_kv, mode="nn", out_dtype=BF16, tm=mem_b.shape[0], tn=1024, tk=D_MODEL, name="mem_kv")
    ycat, ycat_t = _band_fwd(h, u)
    ycat, ycat_t, o_pre, states = _gla_fwd(h, gw, gb, gn, ycat, ycat_t)
    ycat, ycat_t = _mem_fwd(h, mkv, ycat, ycat_t)
    if next_hop is None:
        y, ybf, yt, xhat, rstd = _outproj_ln(ycat, w_out, x, ln_g, ln_b)
    else:
        (y, ybf, yt, xhat, rstd), rode = _outproj_ln(ycat, w_out, x, ln_g, ln_b, rider=(next_hop, [], rode))
    saved = (xt, h, mkv, ycat_t, o_pre, states, xhat, rstd)
    return (y, ybf, yt, saved) if rider is None else (y, ybf, yt, saved, rode)


def _layer_bwd(dy, saved, mem_b, w_in, w_out, u, gw, gb, gn, ln_g, reduce=None, own_reduce=None):
    xt, h, mkv, ycat_t, o_pre, states, xhat, rstd = saved

    def riding(**kw):
        if reduce is None:
            return _matmul(**kw)
        out, bufs = _matmul(rider=reduce.rider(), **kw)
        reduce.landed(bufs)
        return out

    if reduce is None:
        du, dub, dycat, d_ln_g, d_ln_b = _ln_bwd_dycat(dy, xhat, rstd, ln_g, w_out)
    else:
        (du, dub, dycat, d_ln_g, d_ln_b), bufs = _ln_bwd_dycat(dy, xhat, rstd, ln_g, w_out, rider=reduce.rider())
        reduce.landed(bufs)
    d_w_out = _matmul(ycat_t, dub, mode="nn", out_dtype=F32, tm=1024, tn=1024, tk=min(4096, dub.shape[0]), name="d_w_out")
    dh, d_u = _band_bwd(h, u, dycat)
    dh, dgw, dgb, dgn = _gla_bwd(h, gw, gb, gn, o_pre, states, dycat, dh)
    dh, dmkv = _mem_bwd(h, mkv, dycat, dh)
    d_w_kv = _matmul(mem_b, dmkv, mode="tn", out_dtype=F32, tm=1024, tn=1024, tk=mem_b.shape[0], name="d_w_kv")
    dx_args = dict(a=dh, b=w_in, mode="nt", out_dtype=F32, tm=1024, tn=1024, tk=3584, name="dx", add=du, add_scale=ALPHA)
    dw_args = dict(a=xt, b=dh, mode="nn", out_dtype=F32, tm=1024, tn=1024, tk=min(4096, dh.shape[0]), name="d_w_in")
    if own_reduce is None:
        dx = riding(**dx_args)
        d_w_in = riding(**dw_args)
        return dx, (d_w_in, d_u, dgw, dgb, dgn, d_w_kv, d_w_out, d_ln_g, d_ln_b)
    d_w_in = riding(**dw_args)
    grads = (d_w_in, d_u, dgw, dgb, dgn, d_w_kv, d_w_out, d_ln_g, d_ln_b)
    own = own_reduce(grads, reduce.finish() if reduce is not None else None)
    own.step()
    dx, bufs = _matmul(rider=own.rider(), **dx_args)
    own.landed(bufs)
    return dx, grads, own


def _unpad_heads(w):
    r = w.shape[0]
    return w.reshape(r, B_HEADS, LANES)[:, :, :B_DK].reshape(r, B_KEY_WIDTH)


def _padded_col_of():
    col, o = np.zeros(IN_WIDTH, np.int64), 0
    for seg in (SEG_Q, SEG_K, SEG_V, SEG_Z):
        for hd in range(A_HEADS):
            col[o:o + LANES] = C_A + (hd // A_HPS) * 4 * A_HW + seg * A_HW + (hd % A_HPS) * LANES + np.arange(LANES)
            o += LANES
    for seg, width in ((SEG_Q, B_DK), (SEG_K, B_DK), (SEG_V, LANES), (SEG_Z, LANES)):
        for hd in range(B_HEADS):
            col[o:o + width] = C_B + hd * 4 * LANES + seg * LANES + np.arange(width)
            o += width
    col[o:o + GATE_RANK] = C_LR + LR_LANE + np.arange(GATE_RANK)
    o += GATE_RANK
    for seg in (0, 1):
        for hd in range(M_HEADS):
            col[o:o + LANES] = C_M + hd * 2 * LANES + seg * LANES + np.arange(LANES)
            o += LANES
    assert o == IN_WIDTH
    return col


def _runs(idx):
    out, start = [], 0
    for k in range(1, len(idx) + 1):
        if k == len(idx) or idx[k] != idx[k - 1] + 1:
            out.append((int(idx[start]), k - start))
            start = k
    return out


def _chip_columns(g, j, n):
    runs = _runs(_padded_col_of()[j * n:(j + 1) * n])
    return jnp.concatenate([g[:, a:a + ln] for a, ln in runs], axis=1)


def _padded_from_shards(shards):
    n = shards[0].shape[1]
    src = np.full(HP, -1, np.int64)
    src[_padded_col_of()] = np.arange(IN_WIDTH)
    parts, k = [], 0
    while k < HP:
        e = k + 1
        if src[k] < 0:
            while e < HP and src[e] < 0:
                e += 1
            parts.append(jnp.zeros((shards[0].shape[0], e - k), shards[0].dtype))
        else:
            while e < HP and src[e] == src[e - 1] + 1 and src[e] // n == src[k] // n:
                e += 1
            parts.append(shards[src[k] // n][:, src[k] % n:src[k] % n + e - k])
        k = e
    return jnp.concatenate(parts, axis=1)


ADAMW_BLOCK_BYTES = 1 << 20


def _adamw(w, g, m, v, name):
    L, R, C = w.shape
    tl, tr = 1, R
    if R * C * 4 <= ADAMW_BLOCK_BYTES:
        tl = max(d for d in range(1, L + 1) if L % d == 0 and d * R * C * 4 <= ADAMW_BLOCK_BYTES)
    else:
        for cand in (256, 128, 64, 32, 16, 8):
            if R % cand == 0 and R > cand:
                tr = cand
                break

    def body(w_ref, g_ref, m_ref, v_ref, d_ref, nm_ref, nv_ref):
        g_ = g_ref[...]
        nm = ADAM_B1 * m_ref[...] + (1.0 - ADAM_B1) * g_
        nv = ADAM_B2 * v_ref[...] + (1.0 - ADAM_B2) * (g_ * g_)
        m_hat = nm / (1.0 - ADAM_B1 ** ADAM_STEP)
        v_hat = nv / (1.0 - ADAM_B2 ** ADAM_STEP)
        d_ref[...] = -ADAM_LR * (m_hat / (jnp.sqrt(v_hat) + ADAM_EPS) + ADAM_WD * w_ref[...])
        nm_ref[...] = nm
        nv_ref[...] = nv

    spec = pl.BlockSpec((tl, tr, C), lambda l, i: (l, i, 0))
    sd = jax.ShapeDtypeStruct((L, R, C), F32)
    return pl.pallas_call(
        body, name=name, grid=(L // tl, R // tr), in_specs=[spec] * 4, out_specs=[spec] * 3, out_shape=[sd] * 3,
        compiler_params=_params(("parallel", "parallel")),
    )(w, g, m, v)


def _adamw_nd(w, g, m, v, name):
    shape = w.shape
    f = (lambda a: a) if w.ndim == 3 else (lambda a: a.reshape(1, shape[0], shape[1]))
    return tuple(o.reshape(shape) for o in _adamw(f(w), f(g), f(m), f(v), name))


ANY = pl.BlockSpec(memory_space=pl.ANY)


def _place():
    x, y, c = lax.axis_index("x"), lax.axis_index("y"), lax.axis_index("c")
    chips = [(1 - x, y), (x, 1 - y), (1 - x, 1 - y)]
    return x, y, c, chips


class _WeightGather:
    def __init__(self, hop, layer, rows):
        self.hop, self.layer, self.rows = hop, layer, rows
        self.n_sem = 3 * len(rows)

    def _copies(self, shard_refs, buf_refs, send, recv, received):
        x, y, c, chips = _place()
        out = []
        for t, R in enumerate(self.rows):
            half = R // 2
            assert half % 16 == 0
            mine = pl.ds(pl.multiple_of(c * half, 16), half)
            other = pl.ds(pl.multiple_of((1 - c) * half, 16), half)
            mine_of_shard = pl.ds(pl.multiple_of(self.layer * R + c * half, 16), half)
            for k, chip in enumerate(chips):
                theirs = buf_refs[t].at[2 * chip[0] + chip[1]]
                if self.hop == "chips":
                    src, dst, to = shard_refs[t].at[mine_of_shard], buf_refs[t].at[2 * x + y, mine], (*chip, c)
                    landed = theirs.at[mine]
                else:
                    src, dst, to = theirs.at[mine], theirs.at[mine], (x, y, 1 - c)
                    landed = theirs.at[other]
                out.append(pltpu.make_async_remote_copy(
                    src_ref=src, dst_ref=landed if received else dst, send_sem=send.at[3 * t + k],
                    recv_sem=recv.at[3 * t + k], device_id=to, device_id_type=MESH))
        return out

    def start(self, shard_refs, buf_refs, send, recv):
        for cp in self._copies(shard_refs, buf_refs, send, recv, False):
            cp.start()

    def wait(self, shard_refs, buf_refs, send, recv):
        for cp in self._copies(shard_refs, buf_refs, send, recv, True):
            cp.wait_recv()
        for cp in self._copies(shard_refs, buf_refs, send, recv, False):
            cp.wait_send()

    def sems(self):
        return [pltpu.SemaphoreType.DMA((self.n_sem,)), pltpu.SemaphoreType.DMA((self.n_sem,))]

    def call(self, srcs, bufs, name):
        ns, nb = len(srcs), len(bufs)

        def body(*refs):
            src_refs, buf_refs, (send, recv) = refs[:ns], refs[ns + nb:ns + 2 * nb], refs[ns + 2 * nb:]
            self.start(src_refs, buf_refs, send, recv)
            self.wait(src_refs, buf_refs, send, recv)

        return pl.pallas_call(
            body, name=name, in_specs=[ANY] * (ns + nb), out_specs=[ANY] * nb,
            out_shape=[jax.ShapeDtypeStruct(b.shape, b.dtype) for b in bufs],
            input_output_aliases={ns + t: t for t in range(nb)},
            scratch_shapes=self.sems(),
        )(*srcs, *bufs)


class _GradHop(_WeightGather):
    def __init__(self, hop, layer, slices):
        self.hop, self.layer, self.slices = hop, layer, slices
        self.n_sem = {"pair": N_CHIPS, "chips": N_CHIPS - 1, "gather": 1}[hop] * len(slices)

    def _copies(self, src_refs, buf_refs, send, recv, received):
        x, y, c, chips = _place()
        me, out = 2 * x + y, []

        def remote(src, dst, to):
            k = len(out)
            out.append(pltpu.make_async_remote_copy(src_ref=src, dst_ref=dst, send_sem=send.at[k], recv_sem=recv.at[k],
                                                    device_id=to, device_id_type=MESH))

        for t, (half, where) in enumerate(self.slices):
            if self.hop == "pair":
                for j, (a, first) in enumerate(where):
                    rows = pl.ds(pl.multiple_of(first + (1 - c) * half, 8), half)
                    remote(src_refs[a].at[rows], buf_refs[t].at[j], (x, y, 1 - c))
            elif self.hop == "chips":
                for chip in chips:
                    slot = 2 * chip[0] + chip[1]
                    remote(src_refs[t].at[slot], buf_refs[t].at[slot if received else me], (*chip, c))
            else:
                mine = buf_refs[t].at[self.layer, c]
                remote(mine, buf_refs[t].at[self.layer, 1 - c] if received else mine, (x, y, 1 - c))
        return out


def _add_halves(parts, got, c_idx, name):
    n, L, half, C = got.shape
    tr = 64

    def body(*refs):
        ins, (got_ref, o_ref) = refs[1:1 + len(parts)], refs[1 + len(parts):]
        for k in range(len(parts)):
            o_ref[k // L, k % L] = (ins[k][...] + got_ref[k // L, k % L]).astype(BF16)

    def rows_of(first):
        assert first % tr == 0 and half % tr == 0
        return lambda i, c: (first // tr + c[0] * (half // tr) + i, 0)

    whole = pl.BlockSpec((n, L, tr, C), lambda i, c: (0, 0, i, 0))
    return pl.pallas_call(
        body, name=name,
        grid_spec=pltpu.PrefetchScalarGridSpec(
            num_scalar_prefetch=1, grid=(half // tr,),
            in_specs=[pl.BlockSpec((tr, C), rows_of(first)) for _, first in parts] + [whole],
            out_specs=whole),
        out_shape=jax.ShapeDtypeStruct((n, L, half, C), BF16),
        compiler_params=_params(("parallel",)),
    )(c_idx, *[a for a, _ in parts], got)


def _add_slots(r, c_idx, dest, layer, name):
    n, half, C = r.shape
    tr = 256

    def body(c_ref, r_ref, dest_ref, o_ref):
        del dest_ref
        acc = r_ref[0].astype(F32)
        for j in range(1, n):
            acc = acc + r_ref[j].astype(F32)
        o_ref[0, 0] = acc

    return pl.pallas_call(
        body, name=name,
        grid_spec=pltpu.PrefetchScalarGridSpec(
            num_scalar_prefetch=1, grid=(half // tr,),
            in_specs=[pl.BlockSpec((n, tr, C), lambda i, c: (0, i, 0)), ANY],
            out_specs=pl.BlockSpec((1, 1, tr, C), lambda i, c: (layer, c[0], i, 0))),
        out_shape=jax.ShapeDtypeStruct(dest.shape, F32),
        input_output_aliases={2: 0},
        compiler_params=_params(("parallel",)),
    )(c_idx, r, dest)


class _LayerReduce:
    def __init__(self, layer, grads, slices, c_idx, chip, dests):
        self.layer, self.grads, self.slices, self.c_idx, self.chip, self.dests = layer, grads, slices, c_idx, chip, dests
        self.widths = [grads[where[0][0]].shape[1] for _, where in slices]
        self.stage = 0

    def _hop(self, kind):
        return _GradHop(kind, self.layer, self.slices)

    def rider(self):
        if self.stage == 0:
            got = [lax.empty((N_CHIPS, half, w), F32) for (half, _), w in zip(self.slices, self.widths)]
            return self._hop("pair"), self.grads, got
        if self.stage == 1:
            q = [lax.empty(p.shape, BF16) for p in self.pair_sums]
            return self._hop("chips"), self.pair_sums, q
        return self._hop("gather"), [], self.dests

    def landed(self, bufs):
        tag = f"{self.layer}"
        if self.stage == 0:
            self.pair_sums = []
            for t, ((half, where), got) in enumerate(zip(self.slices, bufs)):
                parts = [(self.grads[a], first) for a, first in where]
                p = _add_halves(parts, got[:, None], self.c_idx, f"rs_add2_{t}_{tag}")
                self.pair_sums.append(p.reshape(N_CHIPS, half, p.shape[-1]))
        elif self.stage == 1:
            for t, (q, p) in enumerate(zip(bufs, self.pair_sums)):
                q = lax.dynamic_update_slice_in_dim(q, lax.dynamic_slice_in_dim(p, self.chip, 1, axis=0), self.chip, axis=0)
                self.dests[t] = _add_slots(q, self.c_idx, self.dests[t], self.layer, f"rs_add4_{t}_{tag}")
        else:
            self.dests = list(bufs)
        self.stage += 1

    def step(self):
        hop, srcs, bufs = self.rider()
        self.landed(hop.call(srcs, bufs, f"rs_{hop.hop}_{self.layer}"))

    def finish(self):
        while self.stage < 3:
            self.step()
        return self.dests


def _all_reduce_small(buf, name):
    R = buf.shape[0]

    def flipped(k, x, y, c):
        return ((1 - x) if k & 4 else x, (1 - y) if k & 2 else y, (1 - c) if k & 1 else c)

    def body(b_ref, o_ref, land, send, recv):
        x, y, c, _ = _place()
        me = 4 * x + 2 * y + c
        land[me] = b_ref[...]
        cps = []
        for k in range(1, N_DEV):
            peer = flipped(k, x, y, c)
            cps.append(pltpu.make_async_remote_copy(src_ref=b_ref, dst_ref=land.at[me], send_sem=send.at[k - 1],
                                                    recv_sem=recv.at[k - 1], device_id=peer, device_id_type=MESH))
        for cp in cps:
            cp.start()
        for k in range(1, N_DEV):
            peer = flipped(k, x, y, c)
            slot = 4 * peer[0] + 2 * peer[1] + peer[2]
            pltpu.make_async_remote_copy(src_ref=b_ref, dst_ref=land.at[slot], send_sem=send.at[k - 1],
                                         recv_sem=recv.at[k - 1], device_id=peer, device_id_type=MESH).wait_recv()
        for cp in cps:
            cp.wait_send()
        acc = land[0]
        for j in range(1, N_DEV):
            acc = acc + land[j]
        o_ref[...] = acc

    vm = pl.BlockSpec(memory_space=pltpu.VMEM)
    return pl.pallas_call(
        body, name=name, in_specs=[vm], out_specs=vm,
        out_shape=jax.ShapeDtypeStruct((R, LANES), F32),
        scratch_shapes=[pltpu.VMEM((N_DEV, R, LANES), F32), pltpu.SemaphoreType.DMA((N_DEV - 1,)),
                        pltpu.SemaphoreType.DMA((N_DEV - 1,))],
    )(buf)


def kernel(x, mem, w_in, a_rel_bias, b_gate_w, b_gate_b, b_norm_g, w_mem_kv, w_out, ln_g, ln_b, loss_target, m_w_in, m_a_rel_bias, m_b_gate_w, m_b_gate_b, m_b_norm_g, m_w_mem_kv, m_w_out, m_ln_g, m_ln_b, v_w_in, v_a_rel_bias, v_b_gate_w, v_b_gate_b, v_b_norm_g, v_w_mem_kv, v_w_out, v_ln_g, v_ln_b):
    L = w_in.shape[0]
    S = x.shape[1]
    cx, cy, cc = lax.axis_index("x"), lax.axis_index("y"), lax.axis_index("c")
    chip = 2 * cx + cy
    c_idx = jnp.reshape(cc, (1,)).astype(jnp.int32)

    n_in, r_kv, r_out = w_in.shape[2], w_mem_kv.shape[1], w_out.shape[1]
    shards = [w.astype(BF16).reshape(-1, w.shape[2]) for w in (w_in, w_mem_kv, w_out)]
    rows = [D_MODEL, r_kv, r_out]

    def landing(l):
        return [lax.dynamic_update_slice_in_dim(lax.empty((N_CHIPS, r, s.shape[1]), BF16),
                                                s[l * r:(l + 1) * r][None], chip, axis=0)
                for s, r in zip(shards, rows)]

    def assembled(bufs):
        return (_padded_from_shards([bufs[0][j] for j in range(N_CHIPS)]),
                bufs[1].reshape(D_MODEL, bufs[1].shape[2]), bufs[2].reshape(D_MODEL, D_MODEL))

    bufs0 = _WeightGather("chips", 0, rows).call(shards, landing(0), "gather_chips_0")
    weights = [assembled(_WeightGather("pair", 0, rows).call(shards, bufs0, "gather_pair_0"))]

    gw_cols = b_gate_w.shape[2]
    gw_slot = jnp.zeros((N_CHIPS, L, GATE_RANK, gw_cols), F32)
    gw_slot = lax.dynamic_update_slice(gw_slot, (0.5 * b_gate_w)[None], (chip, 0, 0, 0))
    gw_flat = gw_slot.reshape(-1)
    n_gw = gw_flat.shape[0]
    pad = (-n_gw) % (8 * LANES)
    gw_full = _all_reduce_small(jnp.pad(gw_flat, (0, pad)).reshape(-1, LANES), "gather_gate_w").reshape(-1)[:n_gw]
    gw_full = gw_full.reshape(N_CHIPS, L, GATE_RANK, gw_cols).transpose(1, 2, 0, 3).reshape(L, GATE_RANK, B_KEY_WIDTH)

    xs = x.reshape(S, D_MODEL)
    mem_b = mem.reshape(mem.shape[1], D_MODEL).astype(BF16)
    target = loss_target.reshape(S, D_MODEL)

    small_w = []
    for l in range(L):
        gw_l, gb_l = _pad_gate(gw_full[l], b_gate_b[l])
        small_w.append((_bias_by_offset(a_rel_bias[l]), gw_l, gb_l,
                        b_norm_g[l].reshape(1, LANES), ln_g[l].reshape(1, D_MODEL), ln_b[l].reshape(1, D_MODEL)))

    y = xs
    yb, yt = _cast_transpose(xs, "x_t")
    saved = []
    for l in range(L):
        if l + 1 < L:
            rider = (_WeightGather("chips", l + 1, rows), shards, landing(l + 1))
            y, yb, yt, sv, bufs = _layer_fwd(y, yb, yt, mem_b, *weights[l], *small_w[l], rider=rider,
                                             next_hop=_WeightGather("pair", l + 1, rows))
            weights.append(assembled(bufs))
        else:
            y, yb, yt, sv = _layer_fwd(y, yb, yt, mem_b, *weights[l], *small_w[l])
        saved.append(sv)
    layer_w = [(*weights[l], *small_w[l]) for l in range(L)]
    loss_part, dy = _loss_head(y, target)

    halves = [D_MODEL // 2, r_kv // 2, r_out // 2]
    dests = [lax.empty((L, 2, hf, w.shape[2]), F32) for hf, w in zip(halves, (w_in, w_mem_kv, w_out))]
    slices = [(halves[0], [(j, 0) for j in range(N_CHIPS)]),
              (halves[1], [(N_CHIPS, j * r_kv) for j in range(N_CHIPS)]),
              (halves[2], [(N_CHIPS + 1, j * r_out) for j in range(N_CHIPS)])]

    def reduction(l, g, into):
        arrays = [_chip_columns(g[0], j, n_in) for j in range(N_CHIPS)] + [g[5], g[6]]
        return _LayerReduce(l, arrays, slices, c_idx, chip, into)

    grads, reduce = [None] * L, None
    for l in reversed(range(L)):
        w_in_l, w_kv_l, w_out_l, u_l, gw_l, gb_l, gn_l, lg_l, lb_l = layer_w[l]
        args = (dy, saved[l], mem_b, w_in_l, w_out_l, u_l, gw_l, gb_l, gn_l, lg_l)
        if l > 0:
            dy, grads[l] = _layer_bwd(*args, reduce=reduce)
            if reduce is not None:
                dests = reduce.finish()
            reduce = reduction(l, grads[l], dests)
        else:
            own = lambda g, above: reduction(0, g, dests if above is None else above)
            dy, grads[l], reduce = _layer_bwd(*args, reduce=reduce, own_reduce=own)
    r_w_in, r_w_kv, r_w_out = [d.reshape(L, 2 * d.shape[2], d.shape[3]) for d in reduce.finish()]
    grad_x = dy.reshape(x.shape)

    g_rel = jnp.stack([_bias_grad_from_offset(g[1]) for g in grads])
    g_gw = jnp.stack([_unpad_heads(g[2][LR_LANE:LR_LANE + GATE_RANK]) for g in grads])
    g_gb = jnp.stack([_unpad_heads(g[3])[0] for g in grads])
    g_gn = jnp.stack([g[4][0] for g in grads])
    g_lg = jnp.stack([g[7][0] for g in grads])
    g_lb = jnp.stack([g[8][0] for g in grads])

    small = [g_rel, g_gw, g_gb, g_gn, g_lg, g_lb, loss_part]
    flat = jnp.concatenate([s.reshape(-1) for s in small])
    n_small = flat.shape[0]
    pad = (-n_small) % (8 * LANES)
    red = _all_reduce_small(jnp.pad(flat, (0, pad)).reshape(-1, LANES), "all_reduce_small").reshape(-1)
    outs, off = [], 0
    for s in small:
        outs.append(red[off:off + s.size].reshape(s.shape))
        off += s.size
    g_rel, g_gw, g_gb, g_gn, g_lg, g_lb, loss = outs
    loss = loss.reshape(())
    g_gw = lax.dynamic_slice_in_dim(g_gw.reshape(L, GATE_RANK, N_CHIPS, gw_cols), chip, 1, axis=2).reshape(L, GATE_RANK, gw_cols)

    g_list = [r_w_in, g_rel, g_gw, g_gb, g_gn, r_w_kv, r_w_out, g_lg, g_lb]
    w_list = [w_in, a_rel_bias, b_gate_w, b_gate_b, b_norm_g, w_mem_kv, w_out, ln_g, ln_b]
    m_list = [m_w_in, m_a_rel_bias, m_b_gate_w, m_b_gate_b, m_b_norm_g, m_w_mem_kv, m_w_out, m_ln_g, m_ln_b]
    v_list = [v_w_in, v_a_rel_bias, v_b_gate_w, v_b_gate_b, v_b_norm_g, v_w_mem_kv, v_w_out, v_ln_g, v_ln_b]
    names = ["w_in", "rel", "gate_w", "gate_b", "norm_g", "w_kv", "w_out", "ln_g", "ln_b"]
    to_cols = lambda a: jnp.transpose(a, (2, 0, 1))
    upd = [tuple(jnp.transpose(o, (1, 2, 0)) for o in
                 _adamw(to_cols(w_in), to_cols(r_w_in), to_cols(m_w_in), to_cols(v_w_in), "adamw_w_in"))]
    upd += [_adamw_nd(w, g, m, v, "adamw_" + n)
            for w, g, m, v, n in list(zip(w_list, g_list, m_list, v_list, names))[1:]]
    deltas = [u_[0] for u_ in upd]
    new_m = [u_[1] for u_ in upd]
    new_v = [u_[2] for u_ in upd]
    return (loss, grad_x, *g_list, *deltas, *new_m, *new_v)
```

```python
import functools

import numpy as np
import jax
import jax.numpy as jnp
from jax import lax
from jax.experimental import pallas as pl
from jax.experimental.pallas import tpu as pltpu

F32 = jnp.float32
BF16 = jnp.bfloat16
MESH = pl.DeviceIdType.MESH

D_MODEL = 2048
DEPTH = 4
CHUNK = 64
LEFT_CHUNKS = 8
MAX_REL = 128
N_REL = 2 * MAX_REL + 1
A_HEADS = 8
HEAD_DIM = 128
B_HEADS = 4
B_DK = 64
M_HEADS = 4
GATE_RANK = 16
GATE_TAU = 16.0
A_WIDTH = A_HEADS * HEAD_DIM
B_WIDTH = B_HEADS * HEAD_DIM
B_KEY_WIDTH = B_HEADS * B_DK
M_WIDTH = M_HEADS * HEAD_DIM
IN_WIDTH = 4 * A_WIDTH + 2 * B_KEY_WIDTH + 2 * B_WIDTH + GATE_RANK + 2 * M_WIDTH
ALPHA = (2.0 * DEPTH) ** 0.25
LN_EPS = 1e-5
RMS_EPS = 1e-6
NEG_INF = -1e30
ADAM_LR = 0.001
ADAM_B1 = 0.9
ADAM_B2 = 0.999
ADAM_EPS = 1e-08
ADAM_WD = 0.01
ADAM_STEP = 10

LANES = 128
VMEM_LIMIT = 56 * 1024 * 1024

C_A, C_B, C_M = 0, 4096, 6144
HP = 7168
LR_HEAD, LR_LANE = B_HEADS - 1, B_DK
C_LR = C_B + LR_HEAD * 4 * LANES
SEG_Q, SEG_K, SEG_V, SEG_Z = 0, 1, 2, 3
TQ = 512
CPB = TQ // CHUNK
N_CHIPS = 4
N_DEV = 8


def _params(sem, vmem=VMEM_LIMIT):
    return pltpu.CompilerParams(dimension_semantics=sem, vmem_limit_bytes=vmem)


def _dot(a, b):
    return jnp.dot(a, b, preferred_element_type=F32)


def _dot_nt(a, b):
    return lax.dot_general(a, b, (((1,), (1,)), ((), ())), preferred_element_type=F32)


def _dot_tn(a, b):
    return lax.dot_general(a, b, (((0,), (0,)), ((), ())), preferred_element_type=F32)


def _sigmoid(x):
    return 1.0 / (1.0 + jnp.exp(-x))


def _split3(x):
    hi = x.astype(BF16)
    r = x - hi.astype(F32)
    mid = r.astype(BF16)
    lo = (r - mid.astype(F32)).astype(BF16)
    return hi, mid, lo


def _dot3(m_bf, x):
    hi, mid, lo = _split3(x)
    return _dot(m_bf, hi) + _dot(m_bf, mid) + _dot(m_bf, lo)


def _matmul(a, b, *, mode, out_dtype, tm, tn, tk, name, add=None, add_scale=1.0, rider=None):
    if mode == "nn":
        (M, K), (K2, N) = a.shape, b.shape
        a_spec = pl.BlockSpec((tm, tk), lambda i, j, k: (i, k))
        b_spec = pl.BlockSpec((tk, tn), lambda i, j, k: (k, j))
        dot = _dot
    elif mode == "nt":
        (M, K), (N, K2) = a.shape, b.shape
        a_spec = pl.BlockSpec((tm, tk), lambda i, j, k: (i, k))
        b_spec = pl.BlockSpec((tn, tk), lambda i, j, k: (j, k))
        dot = _dot_nt
    else:
        (K, M), (K2, N) = a.shape, b.shape
        a_spec = pl.BlockSpec((tk, tm), lambda i, j, k: (k, i))
        b_spec = pl.BlockSpec((tk, tn), lambda i, j, k: (k, j))
        dot = _dot_tn
    assert K == K2 and M % tm == 0 and N % tn == 0 and K % tk == 0, (a.shape, b.shape, mode)
    nk = K // tk
    has_add = add is not None
    assert nk == 1 or out_dtype == F32
    grid = (M // tm, N // tn, nk)
    hop, srcs, bufs = rider if rider is not None else (None, [], [])
    n_in = 2 + has_add

    def body(*refs):
        a_ref, b_ref = refs[:2]
        add_ref = refs[2] if has_add else None
        src_refs = refs[n_in:n_in + len(srcs)]
        o_ref = refs[n_in + len(srcs) + len(bufs)]
        buf_refs = refs[n_in + len(srcs) + len(bufs) + 1:n_in + len(srcs) + 2 * len(bufs) + 1]
        sems = refs[n_in + len(srcs) + 2 * len(bufs) + 1:]
        i, j, k = pl.program_id(0), pl.program_id(1), pl.program_id(2)
        if hop is not None:
            @pl.when(jnp.logical_and(jnp.logical_and(i == 0, j == 0), k == 0))
            def _():
                hop.start(src_refs, buf_refs, *sems)

        part = dot(a_ref[...].astype(BF16), b_ref[...].astype(BF16))

        @pl.when(k == 0)
        def _():
            first = part + add_scale * add_ref[...] if has_add else part
            o_ref[...] = first.astype(out_dtype)

        if nk > 1:
            @pl.when(k > 0)
            def _():
                o_ref[...] += part

        if hop is not None:
            @pl.when(jnp.logical_and(jnp.logical_and(i == grid[0] - 1, j == grid[1] - 1), k == nk - 1))
            def _():
                hop.wait(src_refs, buf_refs, *sems)

    in_specs = [a_spec, b_spec]
    args = [a, b]
    if has_add:
        in_specs.append(pl.BlockSpec((tm, tn), lambda i, j, k: (i, j)))
        args.append(add)
    out = pl.pallas_call(
        body,
        name=name,
        grid=grid,
        in_specs=in_specs + [ANY] * (len(srcs) + len(bufs)),
        out_specs=[pl.BlockSpec((tm, tn), lambda i, j, k: (i, j))] + [ANY] * len(bufs),
        out_shape=[jax.ShapeDtypeStruct((M, N), out_dtype)] + [jax.ShapeDtypeStruct(x.shape, x.dtype) for x in bufs],
        input_output_aliases={n_in + len(srcs) + t: 1 + t for t in range(len(bufs))},
        scratch_shapes=hop.sems() if hop is not None else [],
        compiler_params=_params(("parallel", "parallel", "arbitrary") if hop is None
                                else ("arbitrary", "arbitrary", "arbitrary")),
    )(*args, *srcs, *bufs)
    return out[0] if hop is None else (out[0], list(out[1:]))


def _cast_transpose(a, name):
    R, C = a.shape
    t = 512

    def body(a_ref, b_ref, t_ref):
        b = a_ref[...].astype(BF16)
        b_ref[...] = b
        t_ref[...] = b.T

    return pl.pallas_call(
        body, name=name, grid=(R // t, C // t),
        in_specs=[pl.BlockSpec((t, t), lambda i, j: (i, j))],
        out_specs=[pl.BlockSpec((t, t), lambda i, j: (i, j)), pl.BlockSpec((t, t), lambda i, j: (j, i))],
        out_shape=[jax.ShapeDtypeStruct((R, C), BF16), jax.ShapeDtypeStruct((C, R), BF16)],
        compiler_params=_params(("parallel", "parallel")),
    )(a)


def _chunk_of(rows):
    return lax.shift_right_logical(rows, CHUNK.bit_length() - 1)


A_HPS = 2
A_HW = A_HPS * LANES


def _band_bias(u_row, first):
    bias = pltpu.roll(jnp.broadcast_to(u_row, (TQ, 2 * TQ)), 0, 1, stride=1, stride_axis=0)
    qc = _chunk_of(lax.broadcasted_iota(jnp.int32, (TQ, 2 * TQ), 0))
    col = lax.broadcasted_iota(jnp.int32, (TQ, 2 * TQ), 1)
    kc = _chunk_of(jnp.bitwise_and(col, TQ - 1))
    ok = jnp.logical_or(jnp.logical_and(col < TQ, kc >= qc), jnp.logical_and(col >= TQ, kc <= qc))
    return jnp.where(ok, bias, NEG_INF) + jnp.where(col < TQ, first * NEG_INF, 0.0)


HQ = TQ // 2
HALVES = ((slice(0, HQ), slice(0, 3 * HQ)),
          (slice(HQ, TQ), slice(HQ, 4 * HQ)))


def _band_probs(q, kcat, bias):
    scale = HEAD_DIM ** -0.5
    out = []
    for rows, cols in HALVES:
        s = _dot_nt(q[rows], kcat[cols]) * scale + bias[rows, cols]
        p = jnp.exp(s - jnp.max(s, axis=1, keepdims=True))
        out.append((p, 1.0 / jnp.sum(p, axis=1, keepdims=True)))
    return out


def _band_specs(nq):
    def col(seg, h):
        return C_A // A_HW + 4 * h + seg

    q_spec = pl.BlockSpec((TQ, A_HW), lambda h, i: (jnp.minimum(i, nq - 1), col(SEG_Q, h)))
    kp_spec = pl.BlockSpec((TQ, A_HW), lambda h, i: (jnp.clip(i - 1, 0, nq - 1), col(SEG_K, h)))
    kc_spec = pl.BlockSpec((TQ, A_HW), lambda h, i: (jnp.minimum(i, nq - 1), col(SEG_K, h)))
    vp_spec = pl.BlockSpec((TQ, A_HW), lambda h, i: (jnp.clip(i - 1, 0, nq - 1), col(SEG_V, h)))
    vc_spec = pl.BlockSpec((TQ, A_HW), lambda h, i: (jnp.minimum(i, nq - 1), col(SEG_V, h)))
    z_spec = pl.BlockSpec((TQ, A_HW), lambda h, i: (jnp.minimum(i, nq - 1), col(SEG_Z, h)))
    u_spec = pl.BlockSpec((A_HPS, 1, 2 * TQ), lambda h, i: (h, 0, 0))
    return q_spec, kp_spec, kc_spec, vp_spec, vc_spec, z_spec, u_spec


def _band_fwd(h, u):
    S = h.shape[0]
    nq = S // TQ

    def body(q_ref, kp_ref, kc_ref, vp_ref, vc_ref, z_ref, u_ref, y_ref, yt_ref, bias_scr):
        i = pl.program_id(1)

        @pl.when(i <= 1)
        def _():
            for hh in range(A_HPS):
                bias_scr[hh] = _band_bias(u_ref[hh], (i == 0).astype(F32))

        for hh in range(A_HPS):
            cs = slice(hh * LANES, (hh + 1) * LANES)
            kcat = jnp.concatenate([kp_ref[:, cs], kc_ref[:, cs]], axis=0)
            vcat = jnp.concatenate([vp_ref[:, cs], vc_ref[:, cs]], axis=0)
            probs = _band_probs(q_ref[:, cs], kcat, bias_scr[hh])
            o = jnp.concatenate([_dot(p.astype(BF16), vcat[cols]) * inv
                                 for (p, inv), (_, cols) in zip(probs, HALVES)], axis=0)
            z = z_ref[:, cs].astype(F32)
            y = o * (z * _sigmoid(z))
            y_ref[:, cs] = y.astype(BF16)
            yt_ref[cs, :] = y.T.astype(BF16)

    specs = _band_specs(nq)
    return pl.pallas_call(
        body,
        name="band_fwd",
        grid=(A_HEADS // A_HPS, nq),
        in_specs=[specs[0], specs[1], specs[2], specs[3], specs[4], specs[5], specs[6]],
        out_specs=[pl.BlockSpec((TQ, A_HW), lambda h, i: (i, h)), pl.BlockSpec((A_HW, TQ), lambda h, i: (h, i))],
        out_shape=[jax.ShapeDtypeStruct((S, D_MODEL), BF16), jax.ShapeDtypeStruct((D_MODEL, S), BF16)],
        scratch_shapes=[pltpu.VMEM((A_HPS, TQ, 2 * TQ), F32)],
        compiler_params=_params(("parallel", "arbitrary")),
    )(h, h, h, h, h, h, u)


def _band_bwd(h, u, dycat):
    S = h.shape[0]
    nq = S // TQ
    scale = HEAD_DIM ** -0.5
    qs, ks, vs, zs = (slice(s * A_HW, (s + 1) * A_HW) for s in (SEG_Q, SEG_K, SEG_V, SEG_Z))

    def body(q_ref, kp_ref, kc_ref, vp_ref, vc_ref, z_ref, u_ref, dy_ref,
             dh_ref, du_ref, bias_scr, db_scr, ckt_scr, cvt_scr, cq_scr, cz_scr):
        i = pl.program_id(1)

        @pl.when(i <= 1)
        def _():
            for hh in range(A_HPS):
                bias_scr[hh] = _band_bias(u_ref[hh], (i == 0).astype(F32))

        @pl.when(i == 0)
        def _():
            db_scr[...] = jnp.zeros_like(db_scr)
            ckt_scr[...] = jnp.zeros_like(ckt_scr)
            cvt_scr[...] = jnp.zeros_like(cvt_scr)
            cq_scr[...] = jnp.zeros_like(cq_scr)
            cz_scr[...] = jnp.zeros_like(cz_scr)

        @pl.when(i < nq)
        def _():
            dh_ref[:, qs] = cq_scr[...]
            dh_ref[:, zs] = cz_scr[...]
            for hh in range(A_HPS):
                cs = slice(hh * LANES, (hh + 1) * LANES)
                q = q_ref[:, cs]
                kcat = jnp.concatenate([kp_ref[:, cs], kc_ref[:, cs]], axis=0)
                vcat = jnp.concatenate([vp_ref[:, cs], vc_ref[:, cs]], axis=0)
                probs = [p * inv for p, inv in _band_probs(q, kcat, bias_scr[hh])]
                o = jnp.concatenate([_dot(p.astype(BF16), vcat[cols]) for p, (_, cols) in zip(probs, HALVES)], axis=0)
                z = z_ref[:, cs].astype(F32)
                sg = _sigmoid(z)
                dy = dy_ref[:, cs].astype(F32)
                do = dy * (z * sg)
                cz_scr[:, cs] = (dy * o * (sg * (1.0 + z * (1.0 - sg)))).astype(BF16)
                dob = do.astype(BF16)
                delta = jnp.sum(do * o, axis=1, keepdims=True)
                qt, dot_ = q.T, dob.T
                dq, dkt, dvt = [], [], []
                for p, (rows, cols) in zip(probs, HALVES):
                    ds = p * (_dot_nt(dob[rows], vcat[cols]) - delta[rows])
                    db_scr[hh, rows, cols] += ds
                    dsb = ds.astype(BF16)
                    dq.append(scale * _dot(dsb, kcat[cols]))
                    dkt.append(scale * _dot(qt[:, rows], dsb))
                    dvt.append(_dot(dot_[:, rows], p.astype(BF16)))
                cq_scr[:, cs] = jnp.concatenate(dq, axis=0).astype(BF16)

                def over_keys(parts):
                    lo, hi = parts
                    prev = jnp.concatenate([lo[:, :HQ], lo[:, HQ:2 * HQ] + hi[:, :HQ]], axis=1)
                    cur = jnp.concatenate([lo[:, 2 * HQ:] + hi[:, HQ:2 * HQ], hi[:, 2 * HQ:]], axis=1)
                    return prev, cur

                (dk_prev, dk_cur), (dv_prev, dv_cur) = over_keys(dkt), over_keys(dvt)
                dh_ref[:, SEG_K * A_HW + hh * LANES:SEG_K * A_HW + (hh + 1) * LANES] = (
                    ckt_scr[cs, :] + dk_prev).T.astype(BF16)
                dh_ref[:, SEG_V * A_HW + hh * LANES:SEG_V * A_HW + (hh + 1) * LANES] = (
                    cvt_scr[cs, :] + dv_prev).T.astype(BF16)
                ckt_scr[cs, :] = dk_cur
                cvt_scr[cs, :] = dv_cur

        @pl.when(i == nq)
        def _():
            dh_ref[:, qs] = cq_scr[...]
            dh_ref[:, zs] = cz_scr[...]
            dh_ref[:, ks] = ckt_scr[...].T.astype(BF16)
            dh_ref[:, vs] = cvt_scr[...].T.astype(BF16)
            r0 = lax.broadcasted_iota(jnp.int32, (TQ, TQ), 0)
            r1 = lax.broadcasted_iota(jnp.int32, (TQ, TQ), 1)
            flip = (r0 + r1 == TQ - 1).astype(BF16)
            for hh in range(A_HPS):
                fl = _dot3(flip, db_scr[hh])
                rolled = pltpu.roll(fl, 0, 1, stride=1, stride_axis=0)
                du_ref[hh] = jnp.sum(rolled, axis=0, keepdims=True)

    specs = _band_specs(nq)
    dy_spec = pl.BlockSpec((TQ, A_HW), lambda h, i: (jnp.minimum(i, nq - 1), h))
    return pl.pallas_call(
        body,
        name="band_bwd",
        grid=(A_HEADS // A_HPS, nq + 1),
        in_specs=[specs[0], specs[1], specs[2], specs[3], specs[4], specs[5], specs[6], dy_spec],
        out_specs=[pl.BlockSpec((TQ, 4 * A_HW), lambda h, i: (jnp.maximum(i - 1, 0), C_A // (4 * A_HW) + h)),
                   pl.BlockSpec((A_HPS, 1, 2 * TQ), lambda h, i: (h, 0, 0))],
        out_shape=[jax.ShapeDtypeStruct((S, HP), BF16), jax.ShapeDtypeStruct((A_HEADS, 1, 2 * TQ), F32)],
        scratch_shapes=[pltpu.VMEM((A_HPS, TQ, 2 * TQ), F32), pltpu.VMEM((A_HPS, TQ, 2 * TQ), F32),
                        pltpu.VMEM((A_HW, TQ), F32), pltpu.VMEM((A_HW, TQ), F32),
                        pltpu.VMEM((TQ, A_HW), BF16), pltpu.VMEM((TQ, A_HW), BF16)],
        compiler_params=_params(("parallel", "arbitrary")),
    )(h, h, h, h, h, h, u, dycat)


def _bias_by_offset(table):
    far = jnp.broadcast_to(table[:, N_REL - 1:], (A_HEADS, TQ - MAX_REL))
    ramp = jnp.flip(table, axis=1)
    rest = jnp.broadcast_to(table[:, :1], (A_HEADS, 2 * TQ - CHUNK - (TQ + MAX_REL + 1)))
    wrap = jnp.broadcast_to(table[:, N_REL - 1:], (A_HEADS, CHUNK))
    return jnp.concatenate([far, ramp, rest, wrap], axis=1)[:, None, :]


def _bias_grad_from_offset(du):
    g = jnp.roll(du[:, 0, :], -(TQ - 1), axis=1)
    far = jnp.sum(g[:, :TQ - MAX_REL], axis=1) + jnp.sum(g[:, 2 * TQ - CHUNK:], axis=1)
    ramp = jnp.flip(g[:, TQ - MAX_REL:TQ + MAX_REL + 1], axis=1)
    return ramp.at[:, N_REL - 1].add(far)


def _mem_probs(q, mk):
    s = _dot_nt(q, mk) * (HEAD_DIM ** -0.5)
    p = jnp.exp(s - jnp.max(s, axis=1, keepdims=True))
    return p * (1.0 / jnp.sum(p, axis=1, keepdims=True))


def _mem_cols(hd, seg):
    return slice((2 * hd + seg) * LANES, (2 * hd + seg + 1) * LANES)


def _mem_fwd(h, mkv, ycat, ycat_t):
    S = h.shape[0]
    nm = mkv.shape[0]
    c0 = (A_WIDTH + B_WIDTH) // LANES

    def body(qz_ref, mkv_ref, yin_ref, ytin_ref, y_ref, yt_ref):
        del yin_ref, ytin_ref
        for hd in range(M_HEADS):
            lane = slice(hd * LANES, (hd + 1) * LANES)
            p = _mem_probs(qz_ref[:, _mem_cols(hd, 0)], mkv_ref[:, lane])
            o = _dot(p.astype(BF16), mkv_ref[:, M_WIDTH + hd * LANES:M_WIDTH + (hd + 1) * LANES])
            z = qz_ref[:, _mem_cols(hd, 1)].astype(F32)
            y = o * (z * _sigmoid(z))
            y_ref[:, lane] = y.astype(BF16)
            yt_ref[lane, :] = y.T.astype(BF16)

    return pl.pallas_call(
        body,
        name="mem_fwd",
        grid=(S // TQ,),
        in_specs=[pl.BlockSpec((TQ, 2 * M_WIDTH), lambda i: (i, C_M // (2 * M_WIDTH))),
                  pl.BlockSpec((nm, 2 * M_WIDTH), lambda i: (0, 0)), ANY, ANY],
        out_specs=[pl.BlockSpec((TQ, M_WIDTH), lambda i: (i, c0 // M_HEADS)),
                   pl.BlockSpec((M_WIDTH, TQ), lambda i: (c0 // M_HEADS, i))],
        out_shape=[jax.ShapeDtypeStruct(ycat.shape, BF16), jax.ShapeDtypeStruct(ycat_t.shape, BF16)],
        input_output_aliases={2: 0, 3: 1},
        compiler_params=_params(("parallel",)),
    )(h, mkv, ycat, ycat_t)


def _mem_bwd(h, mkv, dycat, dh):
    S = h.shape[0]
    nm = mkv.shape[0]
    scale = HEAD_DIM ** -0.5

    def body(qz_ref, mkv_ref, dy_ref, dhin_ref, dh_ref, dmkv_ref):
        del dhin_ref
        i = pl.program_id(0)
        for hd in range(M_HEADS):
            lane = slice(hd * LANES, (hd + 1) * LANES)
            lane_v = slice(M_WIDTH + hd * LANES, M_WIDTH + (hd + 1) * LANES)
            q, mk, mv = qz_ref[:, _mem_cols(hd, 0)], mkv_ref[:, lane], mkv_ref[:, lane_v]
            p = _mem_probs(q, mk)
            pb = p.astype(BF16)
            o = _dot(pb, mv)
            z = qz_ref[:, _mem_cols(hd, 1)].astype(F32)
            sg = _sigmoid(z)
            dy = dy_ref[:, lane].astype(F32)
            do = dy * (z * sg)
            dh_ref[:, _mem_cols(hd, 1)] = (dy * o * (sg * (1.0 + z * (1.0 - sg)))).astype(BF16)
            dob = do.astype(BF16)
            ds = p * (_dot_nt(dob, mv) - jnp.sum(do * o, axis=1, keepdims=True))
            dsb = ds.astype(BF16)
            dh_ref[:, _mem_cols(hd, 0)] = (scale * _dot(dsb, mk)).astype(BF16)
            dmk = (scale * _dot(q.T, dsb)).T
            dmv = _dot(dob.T, pb).T

            @pl.when(i == 0)
            def _():
                dmkv_ref[:, lane] = dmk
                dmkv_ref[:, lane_v] = dmv

            @pl.when(i > 0)
            def _():
                dmkv_ref[:, lane] += dmk
                dmkv_ref[:, lane_v] += dmv

    return pl.pallas_call(
        body,
        name="mem_bwd",
        grid=(S // TQ,),
        in_specs=[pl.BlockSpec((TQ, 2 * M_WIDTH), lambda i: (i, C_M // (2 * M_WIDTH))),
                  pl.BlockSpec((nm, 2 * M_WIDTH), lambda i: (0, 0)),
                  pl.BlockSpec((TQ, M_WIDTH), lambda i: (i, (A_WIDTH + B_WIDTH) // M_WIDTH)), ANY],
        out_specs=[pl.BlockSpec((TQ, 2 * M_WIDTH), lambda i: (i, C_M // (2 * M_WIDTH))),
                   pl.BlockSpec((nm, 2 * M_WIDTH), lambda i: (0, 0))],
        out_shape=[jax.ShapeDtypeStruct(dh.shape, BF16), jax.ShapeDtypeStruct((nm, 2 * M_WIDTH), F32)],
        input_output_aliases={3: 0},
        compiler_params=_params(("arbitrary",)),
    )(h, mkv, dycat, dh)


SUB = 2 * CHUNK
SUBS = [slice(s * SUB, (s + 1) * SUB) for s in range(TQ // SUB)]


def _chunk_masks():
    r = lax.broadcasted_iota(jnp.int32, (SUB, SUB), 0)
    c = lax.broadcasted_iota(jnp.int32, (SUB, SUB), 1)
    same = _chunk_of(r) == _chunk_of(c)
    return jnp.logical_and(same, c <= r), jnp.logical_and(same, c > r), jnp.logical_and(same, c >= r)


def _by_sub(fn):
    return jnp.concatenate([fn(rows) for rows in SUBS], axis=0)


def _gla_gates(lr, gw, gb):
    logit = _dot(lr, gw) + gb
    sg = _sigmoid(logit)
    g = (jnp.minimum(logit, 0.0) - jnp.log(1.0 + jnp.exp(-jnp.abs(logit)))) * (1.0 / GATE_TAU)
    lo = _chunk_masks()[0].astype(BF16)
    return sg, _by_sub(lambda rows: _dot3(lo, g[rows]))


def _gla_factors(q, k, b):
    eb = jnp.exp(b)
    enb = jnp.exp(-b)
    return eb, enb, q * eb, q * enb, k * eb, k * enb


def _gla_intra(qp, qn, kp, kn):
    lo, up, _ = _chunk_masks()
    qp, qn, kp, kn = qp.astype(BF16), qn.astype(BF16), kp.astype(BF16), kn.astype(BF16)
    return [jnp.where(lo, _dot_nt(qp[rows], kn[rows]), 0.0) + jnp.where(up, _dot_nt(qn[rows], kp[rows]), 0.0)
            for rows in SUBS]


B_HPS = 4
B_HW = B_HPS * LANES


def _gla_specs(nb, rev):
    blk = (lambda i: nb - 1 - i) if rev else (lambda i: i)
    qkvz_spec = pl.BlockSpec((TQ, 4 * B_HW), lambda i, p: (blk(i), C_B // (4 * B_HW) + p))
    lr_spec = pl.BlockSpec((TQ, LANES), lambda i, p: (blk(i), C_LR // LANES))
    gw_spec = pl.BlockSpec((LANES, B_HW), lambda i, p: (0, p))
    gb_spec = pl.BlockSpec((1, B_HW), lambda i, p: (0, p))
    gn_spec = pl.BlockSpec((1, LANES), lambda i, p: (0, 0))
    return qkvz_spec, lr_spec, gw_spec, gb_spec, gn_spec, blk


def _head_cols(hh, seg):
    return slice((4 * hh + seg) * LANES, (4 * hh + seg + 1) * LANES)


def _gla_fwd(h, gw, gb, gn, ycat, ycat_t):
    S = h.shape[0]
    nb = S // TQ
    c0 = A_WIDTH // LANES

    def body(qkvz_ref, lr_ref, gw_ref, gb_ref, gn_ref, yin_ref, ytin_ref,
             y_ref, yt_ref, o_ref, st_ref, st_scr):
        del yin_ref, ytin_ref
        i, p = pl.program_id(0), pl.program_id(1)
        for hh in range(B_HPS):
            hd = B_HPS * p + hh
            lane = slice(hh * LANES, (hh + 1) * LANES)

            @pl.when(i == 0)
            def _():
                st_scr[hd] = jnp.zeros((LANES, LANES), F32)

            q = qkvz_ref[:, _head_cols(hh, SEG_Q)].astype(F32) * (B_DK ** -0.5)
            k = qkvz_ref[:, _head_cols(hh, SEG_K)].astype(F32)
            v = qkvz_ref[:, _head_cols(hh, SEG_V)]
            _, b = _gla_gates(lr_ref[...], gw_ref[:, lane], gb_ref[:, lane])
            _, _, qp, qn, kp, kn = _gla_factors(q, k, b)
            o_intra = jnp.concatenate([_dot(a.astype(BF16), v[rows])
                                       for a, rows in zip(_gla_intra(qp, qn, kp, kn), SUBS)], axis=0)
            qpb, knb = qp.astype(BF16), kn.astype(BF16)
            st = st_scr[hd]
            outs = []
            for c in range(CPB):
                rows = slice(c * CHUNK, (c + 1) * CHUNK)
                st_ref[hh, c] = st
                outs.append(_dot_nt(qpb[rows], st.astype(BF16)))
                e_last = jnp.exp(b[(c + 1) * CHUNK - 1:(c + 1) * CHUNK, :])
                st = (st + _dot_tn(v[rows], knb[rows])) * e_last
            st_scr[hd] = st
            o = o_intra + jnp.concatenate(outs, axis=0)
            o_ref[:, lane] = o
            r = lax.rsqrt(jnp.mean(o * o, axis=1, keepdims=True) + RMS_EPS)
            z = qkvz_ref[:, _head_cols(hh, SEG_Z)].astype(F32)
            y = o * r * gn_ref[...] * (z * _sigmoid(z))
            y_ref[:, lane] = y.astype(BF16)
            yt_ref[lane, :] = y.T.astype(BF16)

    qkvz_s, lr_s, gw_s, gb_s, gn_s, _ = _gla_specs(nb, False)
    return pl.pallas_call(
        body,
        name="gla_fwd",
        grid=(nb, B_HEADS // B_HPS),
        in_specs=[qkvz_s, lr_s, gw_s, gb_s, gn_s, ANY, ANY],
        out_specs=[pl.BlockSpec((TQ, B_HW), lambda i, p: (i, c0 // B_HPS + p)),
                   pl.BlockSpec((B_HW, TQ), lambda i, p: (c0 // B_HPS + p, i)),
                   pl.BlockSpec((TQ, B_HW), lambda i, p: (i, p)),
                   pl.BlockSpec((B_HPS, CPB, LANES, LANES), lambda i, p: (p, i, 0, 0))],
        out_shape=[jax.ShapeDtypeStruct(ycat.shape, BF16), jax.ShapeDtypeStruct(ycat_t.shape, BF16),
                   jax.ShapeDtypeStruct((S, B_WIDTH), F32),
                   jax.ShapeDtypeStruct((B_HEADS, S // CHUNK, LANES, LANES), F32)],
        input_output_aliases={5: 0, 6: 1},
        scratch_shapes=[pltpu.VMEM((B_HEADS, LANES, LANES), F32)],
        compiler_params=_params(("arbitrary", "arbitrary")),
    )(h, h, gw, gb, gn, ycat, ycat_t)


def _gla_bwd(h, gw, gb, gn, o_pre, states, dycat, dh):
    S = h.shape[0]
    nb = S // TQ
    n_steps = B_HEADS // B_HPS

    def body(qkvz_ref, lr_ref, gw_ref, gb_ref, gn_ref, o_ref, st_ref, dy_ref, dhin_ref,
             dh_ref, dgw_ref, dgb_ref, dgn_ref,
             dst_scr, dgw_scr, dgb_scr, dgn_scr, dlr_scr):
        del dhin_ref
        i, p = pl.program_id(0), pl.program_id(1)

        @pl.when(jnp.logical_and(i == 0, p == 0))
        def _():
            dgn_scr[...] = jnp.zeros_like(dgn_scr)

        dlr_heads = [one_head(hh, i, p, qkvz_ref, lr_ref, gw_ref, gb_ref, gn_ref, o_ref, st_ref, dy_ref,
                              dh_ref, dgw_ref, dgb_ref, dst_scr, dgw_scr, dgb_scr, dgn_scr) for hh in range(B_HPS)]
        dlr = dlr_heads[0]
        for more in dlr_heads[1:]:
            dlr = dlr + more

        @pl.when(p == 0)
        def _():
            dlr_scr[...] = dlr

        @pl.when(p > 0)
        def _():
            dlr_scr[...] += dlr

        @pl.when(p == n_steps - 1)
        def _():
            cols = _head_cols(B_HPS - 1, SEG_Q)
            dh_ref[:, cols] = (dh_ref[:, cols].astype(F32) + dlr_scr[...]).astype(BF16)

        @pl.when(i == nb - 1)
        def _():
            dgn_ref[...] = dgn_scr[...]

    def one_head(hh, i, p, qkvz_ref, lr_ref, gw_ref, gb_ref, gn_ref, o_ref, st_ref, dy_ref,
                 dh_ref, dgw_ref, dgb_ref, dst_scr, dgw_scr, dgb_scr, dgn_scr):
        hd = B_HPS * p + hh
        lane = slice(hh * LANES, (hh + 1) * LANES)

        @pl.when(i == 0)
        def _():
            dst_scr[hd] = jnp.zeros((LANES, LANES), F32)
            dgw_scr[hd] = jnp.zeros((LANES, LANES), F32)
            dgb_scr[hd] = jnp.zeros((1, LANES), F32)

        q = qkvz_ref[:, _head_cols(hh, SEG_Q)].astype(F32) * (B_DK ** -0.5)
        k = qkvz_ref[:, _head_cols(hh, SEG_K)].astype(F32)
        v = qkvz_ref[:, _head_cols(hh, SEG_V)]
        lr, gwv = lr_ref[...], gw_ref[:, lane]
        sg, b = _gla_gates(lr, gwv, gb_ref[:, lane])
        eb, enb, qp, qn, kp, kn = _gla_factors(q, k, b)
        a = _gla_intra(qp, qn, kp, kn)
        qpb, qnb, kpb, knb = qp.astype(BF16), qn.astype(BF16), kp.astype(BF16), kn.astype(BF16)

        o = o_ref[:, lane]
        gn = gn_ref[...]
        r = lax.rsqrt(jnp.mean(o * o, axis=1, keepdims=True) + RMS_EPS)
        z = qkvz_ref[:, _head_cols(hh, SEG_Z)].astype(F32)
        sz = _sigmoid(z)
        dy = dy_ref[:, lane].astype(F32)
        d_on = dy * (z * sz)
        dh_ref[:, _head_cols(hh, SEG_Z)] = (dy * (o * r * gn) * (sz * (1.0 + z * (1.0 - sz)))).astype(BF16)
        dgn_scr[...] += jnp.sum(d_on * o * r, axis=0, keepdims=True)
        t = d_on * gn
        do = r * t - o * (r * r * r) * jnp.mean(t * o, axis=1, keepdims=True)
        dob = do.astype(BF16)

        lo, up, upper = _chunk_masks()
        dqp, dkn, dqn, dkp, dv = [], [], [], [], []
        for a_s, rows in zip(a, SUBS):
            da = _dot_nt(dob[rows], v[rows])
            dalo = jnp.where(lo, da, 0.0).astype(BF16)
            daup = jnp.where(up, da, 0.0).astype(BF16)
            dqp.append(_dot(dalo, knb[rows]))
            dkn.append(_dot_tn(dalo, qpb[rows]))
            dqn.append(_dot(daup, kpb[rows]))
            dkp.append(_dot_tn(daup, qnb[rows]))
            dv.append(_dot_tn(a_s.astype(BF16), dob[rows]))
        dqp, dkn, dqn, dkp, dv = (jnp.concatenate(x, axis=0) for x in (dqp, dkn, dqn, dkp, dv))

        dst = dst_scr[hd]
        dqp_c, dkn_c, dv_c, dbl_c = [None] * CPB, [None] * CPB, [None] * CPB, [None] * CPB
        for c in reversed(range(CPB)):
            rows = slice(c * CHUNK, (c + 1) * CHUNK)
            st = st_ref[hh, c]
            e_last = jnp.exp(b[(c + 1) * CHUNK - 1:(c + 1) * CHUNK, :])
            if c == CPB - 1:
                st_next = (st + _dot_tn(v[rows], knb[rows])) * e_last
            else:
                st_next = st_ref[hh, c + 1]
            dbl_c[c] = jnp.sum(dst * st_next, axis=0, keepdims=True)
            dtt = (dst * e_last).astype(BF16)
            dv_c[c] = _dot_nt(knb[rows], dtt)
            dkn_c[c] = _dot(v[rows], dtt)
            dqp_c[c] = _dot(dob[rows], st.astype(BF16))
            dst = _dot_tn(dob[rows], qpb[rows]) + dst * e_last
        dst_scr[hd] = dst
        dqp = dqp + jnp.concatenate(dqp_c, axis=0)
        dkn = dkn + jnp.concatenate(dkn_c, axis=0)
        dv = dv + jnp.concatenate(dv_c, axis=0)
        dh_ref[:, _head_cols(hh, SEG_V)] = dv.astype(BF16)
        dh_ref[:, _head_cols(hh, SEG_Q)] = ((dqp * eb + dqn * enb) * (B_DK ** -0.5)).astype(BF16)
        dh_ref[:, _head_cols(hh, SEG_K)] = (dkp * eb + dkn * enb).astype(BF16)

        last = jnp.bitwise_and(lax.broadcasted_iota(jnp.int32, (TQ, 1), 0), CHUNK - 1) == CHUNK - 1
        dbl = jnp.concatenate([jnp.broadcast_to(x, (CHUNK, LANES)) for x in dbl_c], axis=0)
        db = dqp * qp - dqn * qn + dkp * kp - dkn * kn + jnp.where(last, dbl, 0.0)
        upper_b = upper.astype(BF16)
        dlogit = _by_sub(lambda rows: _dot3(upper_b, db[rows])) * (1.0 / GATE_TAU) * (1.0 - sg)
        dlb = dlogit.astype(BF16)
        dgw_scr[hd] += _dot_tn(lr, dlb)
        dgb_scr[hd] += jnp.sum(dlogit, axis=0, keepdims=True)

        @pl.when(i == nb - 1)
        def _():
            dgw_ref[:, lane] = dgw_scr[hd]
            dgb_ref[:, lane] = dgb_scr[hd]

        return _dot_nt(dlb, gwv)

    qkvz_s, lr_s, gw_s, gb_s, gn_s, blk = _gla_specs(nb, True)
    row = pl.BlockSpec((TQ, B_HW), lambda i, p: (blk(i), p))
    dy_spec = pl.BlockSpec((TQ, B_HW), lambda i, p: (blk(i), A_WIDTH // B_HW + p))
    st_spec = pl.BlockSpec((B_HPS, CPB, LANES, LANES), lambda i, p: (p, blk(i), 0, 0))
    return pl.pallas_call(
        body,
        name="gla_bwd",
        grid=(nb, B_HEADS // B_HPS),
        in_specs=[qkvz_s, lr_s, gw_s, gb_s, gn_s, row, st_spec, dy_spec, ANY],
        out_specs=[pl.BlockSpec((TQ, 4 * B_HW), lambda i, p: (blk(i), C_B // (4 * B_HW) + p)),
                   pl.BlockSpec((LANES, B_HW), lambda i, p: (0, jnp.where(i == nb - 1, p, 0))),
                   pl.BlockSpec((1, B_HW), lambda i, p: (0, jnp.where(i == nb - 1, p, 0))),
                   pl.BlockSpec((1, LANES), lambda i, p: (0, 0))],
        out_shape=[jax.ShapeDtypeStruct(dh.shape, BF16),
                   jax.ShapeDtypeStruct((LANES, B_HEADS * LANES), F32),
                   jax.ShapeDtypeStruct((1, B_HEADS * LANES), F32),
                   jax.ShapeDtypeStruct((1, LANES), F32)],
        input_output_aliases={8: 0},
        scratch_shapes=[pltpu.VMEM((B_HEADS, LANES, LANES), F32), pltpu.VMEM((B_HEADS, LANES, LANES), F32),
                        pltpu.VMEM((B_HEADS, 1, LANES), F32), pltpu.VMEM((1, LANES), F32),
                        pltpu.VMEM((TQ, LANES), F32)],
        compiler_params=_params(("arbitrary", "arbitrary")),
    )(h, h, gw, gb, gn, o_pre, states, dycat, dh)


LN_ROWS = 512


def _resident(shape):
    return pl.BlockSpec(shape, lambda i: (0,) * len(shape), pipeline_mode=pl.Buffered(1))


def _outproj_ln(ycat, w_out, x, g, b, rider=None):
    S = x.shape[0]
    n = S // LN_ROWS
    hop, srcs, bufs = rider if rider is not None else (None, [], [])

    def body(*refs):
        yc_ref, w_ref, x_ref, g_ref, b_ref = refs[:5]
        src_refs = refs[5:5 + len(srcs)]
        outs = refs[5 + len(srcs) + len(bufs):]
        y_ref, yb_ref, yt_ref, xh_ref, rs_ref = outs[:5]
        buf_refs, sems = outs[5:5 + len(bufs)], outs[5 + len(bufs):]
        i = pl.program_id(0)
        if hop is not None:
            @pl.when(i == 0)
            def _():
                hop.start(src_refs, buf_refs, *sems)

        u = ALPHA * x_ref[...] + _dot(yc_ref[...], w_ref[...])
        mu = jnp.mean(u, axis=1, keepdims=True)
        d = u - mu
        rstd = lax.rsqrt(jnp.mean(d * d, axis=1, keepdims=True) + LN_EPS)
        xh = d * rstd
        y = xh * g_ref[...] + b_ref[...]
        y_ref[...] = y
        yb_ref[...] = y.astype(BF16)
        yt_ref[...] = y.T.astype(BF16)
        xh_ref[...] = xh
        rs_ref[...] = rstd
        if hop is not None:
            @pl.when(i == n - 1)
            def _():
                hop.wait(src_refs, buf_refs, *sems)

    row = lambda w: pl.BlockSpec((LN_ROWS, w), lambda i: (i, 0))
    vec = pl.BlockSpec((1, D_MODEL), lambda i: (0, 0))
    n_in = 5 + len(srcs)
    out = pl.pallas_call(
        body,
        name="outproj_ln",
        grid=(n,),
        in_specs=[row(D_MODEL), _resident((D_MODEL, D_MODEL)), row(D_MODEL), vec, vec] + [ANY] * (len(srcs) + len(bufs)),
        out_specs=[row(D_MODEL), row(D_MODEL), pl.BlockSpec((D_MODEL, LN_ROWS), lambda i: (0, i)), row(D_MODEL), row(1)]
        + [ANY] * len(bufs),
        out_shape=[jax.ShapeDtypeStruct((S, D_MODEL), F32), jax.ShapeDtypeStruct((S, D_MODEL), BF16),
                   jax.ShapeDtypeStruct((D_MODEL, S), BF16),
                   jax.ShapeDtypeStruct((S, D_MODEL), F32), jax.ShapeDtypeStruct((S, 1), F32)]
        + [jax.ShapeDtypeStruct(x_.shape, x_.dtype) for x_ in bufs],
        input_output_aliases={n_in + t: 5 + t for t in range(len(bufs))},
        scratch_shapes=hop.sems() if hop is not None else [],
        compiler_params=_params(("parallel",) if hop is None else ("arbitrary",)),
    )(ycat, w_out, x, g, b, *srcs, *bufs)
    return tuple(out[:5]) if hop is None else (tuple(out[:5]), list(out[5:]))


def _ln_bwd_dycat(dy, xhat, rstd, g, w_out, rider=None):
    S = dy.shape[0]
    n = S // LN_ROWS
    hop, srcs, bufs = rider if rider is not None else (None, [], [])

    def body(*refs):
        dy_ref, xh_ref, rs_ref, g_ref, w_ref = refs[:5]
        src_refs = refs[5:5 + len(srcs)]
        outs = refs[5 + len(srcs) + len(bufs):]
        du_ref, dub_ref, dyc_ref, dg_ref, db_ref = outs[:5]
        buf_refs, sems = outs[5:5 + len(bufs)], outs[5 + len(bufs):]
        i = pl.program_id(0)
        if hop is not None:
            @pl.when(i == 0)
            def _():
                hop.start(src_refs, buf_refs, *sems)

        dy_, xh = dy_ref[...], xh_ref[...]
        dyg = dy_ * g_ref[...]
        m1 = jnp.mean(dyg, axis=1, keepdims=True)
        m2 = jnp.mean(dyg * xh, axis=1, keepdims=True)
        du = rs_ref[...] * (dyg - m1 - xh * m2)
        dub = du.astype(BF16)
        du_ref[...] = du
        dub_ref[...] = dub
        dyc_ref[...] = _dot_nt(dub, w_ref[...]).astype(BF16)
        dg = jnp.sum(dy_ * xh, axis=0, keepdims=True)
        db = jnp.sum(dy_, axis=0, keepdims=True)

        @pl.when(i == 0)
        def _():
            dg_ref[...] = dg
            db_ref[...] = db

        @pl.when(i > 0)
        def _():
            dg_ref[...] += dg
            db_ref[...] += db

        if hop is not None:
            @pl.when(i == n - 1)
            def _():
                hop.wait(src_refs, buf_refs, *sems)

    row = lambda w: pl.BlockSpec((LN_ROWS, w), lambda i: (i, 0))
    vec = pl.BlockSpec((1, D_MODEL), lambda i: (0, 0))
    n_in = 5 + len(srcs)
    out = pl.pallas_call(
        body,
        name="ln_bwd_dycat",
        grid=(n,),
        in_specs=[row(D_MODEL), row(D_MODEL), row(1), vec, pl.BlockSpec((D_MODEL, D_MODEL), lambda i: (0, 0))]
        + [ANY] * (len(srcs) + len(bufs)),
        out_specs=[row(D_MODEL), row(D_MODEL), row(D_MODEL), vec, vec] + [ANY] * len(bufs),
        out_shape=[jax.ShapeDtypeStruct((S, D_MODEL), F32), jax.ShapeDtypeStruct((S, D_MODEL), BF16),
                   jax.ShapeDtypeStruct((S, D_MODEL), BF16),
                   jax.ShapeDtypeStruct((1, D_MODEL), F32), jax.ShapeDtypeStruct((1, D_MODEL), F32)]
        + [jax.ShapeDtypeStruct(b.shape, b.dtype) for b in bufs],
        input_output_aliases={n_in + t: 5 + t for t in range(len(bufs))},
        scratch_shapes=hop.sems() if hop is not None else [],
        compiler_params=_params(("arbitrary",)),
    )(dy, xhat, rstd, g, w_out, *srcs, *bufs)
    return tuple(out[:5]) if hop is None else (tuple(out[:5]), list(out[5:]))


def _loss_head(y, target):
    S = y.shape[0]

    def body(y_ref, t_ref, l_ref, dy_ref):
        i = pl.program_id(0)
        err = y_ref[...] - t_ref[...]
        dy_ref[...] = err * (1.0 / D_MODEL)
        part = (0.5 / D_MODEL) * jnp.sum(jnp.sum(err * err, axis=1, keepdims=True), axis=0, keepdims=True)

        @pl.when(i == 0)
        def _():
            l_ref[...] = part

        @pl.when(i > 0)
        def _():
            l_ref[...] += part

    row = pl.BlockSpec((TQ, D_MODEL), lambda i: (i, 0))
    return pl.pallas_call(
        body,
        name="loss_head",
        grid=(S // TQ,),
        in_specs=[row, row],
        out_specs=[pl.BlockSpec((1, 1), lambda i: (0, 0)), row],
        out_shape=[jax.ShapeDtypeStruct((1, 1), F32), jax.ShapeDtypeStruct((S, D_MODEL), F32)],
        compiler_params=_params(("arbitrary",)),
    )(y, target)


def _pad_gate(gate_w, gate_b):
    gw = gate_w.reshape(GATE_RANK, B_HEADS, B_DK)
    gw = jnp.pad(gw, ((LR_LANE, LANES - LR_LANE - GATE_RANK), (0, 0), (0, LANES - B_DK))).reshape(LANES, B_HEADS * LANES)
    gb = jnp.pad(gate_b.reshape(B_HEADS, B_DK), ((0, 0), (0, LANES - B_DK))).reshape(1, B_HEADS * LANES)
    return gw.astype(BF16), gb.astype(F32)


def _layer_fwd(x, xb, xt, mem_b, w_in, w_kv, w_out, u, gw, gb, gn, ln_g, ln_b, rider=None, next_hop=None):
    h = _matmul(xb, w_in, mode="nn", out_dtype=BF16, tm=1024, tn=1792, tk=D_MODEL, name="in_proj", rider=rider)
    if rider is not None:
        h, rode = h
    mkv = _matmul(mem_b, w_kv, mode="nn", out_dtype=BF16, tm=mem_b.shape[0], tn=1024, tk=D_MODEL, name="mem_kv")
    ycat, ycat_t = _band_fwd(h, u)
    ycat, ycat_t, o_pre, states = _gla_fwd(h, gw, gb, gn, ycat, ycat_t)
    ycat, ycat_t = _mem_fwd(h, mkv, ycat, ycat_t)
    if next_hop is None:
        y, ybf, yt, xhat, rstd = _outproj_ln(ycat, w_out, x, ln_g, ln_b)
    else:
        (y, ybf, yt, xhat, rstd), rode = _outproj_ln(ycat, w_out, x, ln_g, ln_b, rider=(next_hop, [], rode))
    saved = (xt, h, mkv, ycat_t, o_pre, states, xhat, rstd)
    return (y, ybf, yt, saved) if rider is None else (y, ybf, yt, saved, rode)


def _layer_bwd(dy, saved, mem_b, w_in, w_out, u, gw, gb, gn, ln_g, reduce=None, own_reduce=None):
    xt, h, mkv, ycat_t, o_pre, states, xhat, rstd = saved

    def riding(**kw):
        if reduce is None:
            return _matmul(**kw)
        out, bufs = _matmul(rider=reduce.rider(), **kw)
        reduce.landed(bufs)
        return out

    if reduce is None:
        du, dub, dycat, d_ln_g, d_ln_b = _ln_bwd_dycat(dy, xhat, rstd, ln_g, w_out)
    else:
        (du, dub, dycat, d_ln_g, d_ln_b), bufs = _ln_bwd_dycat(dy, xhat, rstd, ln_g, w_out, rider=reduce.rider())
        reduce.landed(bufs)
    d_w_out = _matmul(ycat_t, dub, mode="nn", out_dtype=F32, tm=1024, tn=1024, tk=min(4096, dub.shape[0]), name="d_w_out")
    dh, d_u = _band_bwd(h, u, dycat)
    dh, dgw, dgb, dgn = _gla_bwd(h, gw, gb, gn, o_pre, states, dycat, dh)
    dh, dmkv = _mem_bwd(h, mkv, dycat, dh)
    d_w_kv = _matmul(mem_b, dmkv, mode="tn", out_dtype=F32, tm=1024, tn=1024, tk=mem_b.shape[0], name="d_w_kv")
    dx_args = dict(a=dh, b=w_in, mode="nt", out_dtype=F32, tm=1024, tn=1024, tk=3584, name="dx", add=du, add_scale=ALPHA)
    dw_args = dict(a=xt, b=dh, mode="nn", out_dtype=F32, tm=1024, tn=1024, tk=min(4096, dh.shape[0]), name="d_w_in")
    if own_reduce is None:
        dx = riding(**dx_args)
        d_w_in = riding(**dw_args)
        return dx, (d_w_in, d_u, dgw, dgb, dgn, d_w_kv, d_w_out, d_ln_g, d_ln_b)
    d_w_in = riding(**dw_args)
    grads = (d_w_in, d_u, dgw, dgb, dgn, d_w_kv, d_w_out, d_ln_g, d_ln_b)
    own = own_reduce(grads, reduce.finish() if reduce is not None else None)
    own.step()
    dx, bufs = _matmul(rider=own.rider(), **dx_args)
    own.landed(bufs)
    return dx, grads, own


def _unpad_heads(w):
    r = w.shape[0]
    return w.reshape(r, B_HEADS, LANES)[:, :, :B_DK].reshape(r, B_KEY_WIDTH)


def _padded_col_of():
    col, o = np.zeros(IN_WIDTH, np.int64), 0
    for seg in (SEG_Q, SEG_K, SEG_V, SEG_Z):
        for hd in range(A_HEADS):
            col[o:o + LANES] = C_A + (hd // A_HPS) * 4 * A_HW + seg * A_HW + (hd % A_HPS) * LANES + np.arange(LANES)
            o += LANES
    for seg, width in ((SEG_Q, B_DK), (SEG_K, B_DK), (SEG_V, LANES), (SEG_Z, LANES)):
        for hd in range(B_HEADS):
            col[o:o + width] = C_B + hd * 4 * LANES + seg * LANES + np.arange(width)
            o += width
    col[o:o + GATE_RANK] = C_LR + LR_LANE + np.arange(GATE_RANK)
    o += GATE_RANK
    for seg in (0, 1):
        for hd in range(M_HEADS):
            col[o:o + LANES] = C_M + hd * 2 * LANES + seg * LANES + np.arange(LANES)
            o += LANES
    assert o == IN_WIDTH
    return col


def _runs(idx):
    out, start = [], 0
    for k in range(1, len(idx) + 1):
        if k == len(idx) or idx[k] != idx[k - 1] + 1:
            out.append((int(idx[start]), k - start))
            start = k
    return out


def _chip_columns(g, j, n):
    runs = _runs(_padded_col_of()[j * n:(j + 1) * n])
    return jnp.concatenate([g[:, a:a + ln] for a, ln in runs], axis=1)


def _padded_from_shards(shards):
    n = shards[0].shape[1]
    src = np.full(HP, -1, np.int64)
    src[_padded_col_of()] = np.arange(IN_WIDTH)
    parts, k = [], 0
    while k < HP:
        e = k + 1
        if src[k] < 0:
            while e < HP and src[e] < 0:
                e += 1
            parts.append(jnp.zeros((shards[0].shape[0], e - k), shards[0].dtype))
        else:
            while e < HP and src[e] == src[e - 1] + 1 and src[e] // n == src[k] // n:
                e += 1
            parts.append(shards[src[k] // n][:, src[k] % n:src[k] % n + e - k])
        k = e
    return jnp.concatenate(parts, axis=1)


ADAMW_BLOCK_BYTES = 1 << 20


def _adamw(w, g, m, v, name):
    L, R, C = w.shape
    tl, tr = 1, R
    if R * C * 4 <= ADAMW_BLOCK_BYTES:
        tl = max(d for d in range(1, L + 1) if L % d == 0 and d * R * C * 4 <= ADAMW_BLOCK_BYTES)
    else:
        for cand in (256, 128, 64, 32, 16, 8):
            if R % cand == 0 and R > cand:
                tr = cand
                break

    def body(w_ref, g_ref, m_ref, v_ref, d_ref, nm_ref, nv_ref):
        g_ = g_ref[...]
        nm = ADAM_B1 * m_ref[...] + (1.0 - ADAM_B1) * g_
        nv = ADAM_B2 * v_ref[...] + (1.0 - ADAM_B2) * (g_ * g_)
        m_hat = nm / (1.0 - ADAM_B1 ** ADAM_STEP)
        v_hat = nv / (1.0 - ADAM_B2 ** ADAM_STEP)
        d_ref[...] = -ADAM_LR * (m_hat / (jnp.sqrt(v_hat) + ADAM_EPS) + ADAM_WD * w_ref[...])
        nm_ref[...] = nm
        nv_ref[...] = nv

    spec = pl.BlockSpec((tl, tr, C), lambda l, i: (l, i, 0))
    sd = jax.ShapeDtypeStruct((L, R, C), F32)
    return pl.pallas_call(
        body, name=name, grid=(L // tl, R // tr), in_specs=[spec] * 4, out_specs=[spec] * 3, out_shape=[sd] * 3,
        compiler_params=_params(("parallel", "parallel")),
    )(w, g, m, v)


def _adamw_nd(w, g, m, v, name):
    shape = w.shape
    f = (lambda a: a) if w.ndim == 3 else (lambda a: a.reshape(1, shape[0], shape[1]))
    return tuple(o.reshape(shape) for o in _adamw(f(w), f(g), f(m), f(v), name))


ANY = pl.BlockSpec(memory_space=pl.ANY)


def _place():
    x, y, c = lax.axis_index("x"), lax.axis_index("y"), lax.axis_index("c")
    chips = [(1 - x, y), (x, 1 - y), (1 - x, 1 - y)]
    return x, y, c, chips


class _WeightGather:
    def __init__(self, hop, layer, rows):
        self.hop, self.layer, self.rows = hop, layer, rows
        self.n_sem = 3 * len(rows)

    def _copies(self, shard_refs, buf_refs, send, recv, received):
        x, y, c, chips = _place()
        out = []
        for t, R in enumerate(self.rows):
            half = R // 2
            assert half % 16 == 0
            mine = pl.ds(pl.multiple_of(c * half, 16), half)
            other = pl.ds(pl.multiple_of((1 - c) * half, 16), half)
            mine_of_shard = pl.ds(pl.multiple_of(self.layer * R + c * half, 16), half)
            for k, chip in enumerate(chips):
                theirs = buf_refs[t].at[2 * chip[0] + chip[1]]
                if self.hop == "chips":
                    src, dst, to = shard_refs[t].at[mine_of_shard], buf_refs[t].at[2 * x + y, mine], (*chip, c)
                    landed = theirs.at[mine]
                else:
                    src, dst, to = theirs.at[mine], theirs.at[mine], (x, y, 1 - c)
                    landed = theirs.at[other]
                out.append(pltpu.make_async_remote_copy(
                    src_ref=src, dst_ref=landed if received else dst, send_sem=send.at[3 * t + k],
                    recv_sem=recv.at[3 * t + k], device_id=to, device_id_type=MESH))
        return out

    def start(self, shard_refs, buf_refs, send, recv):
        for cp in self._copies(shard_refs, buf_refs, send, recv, False):
            cp.start()

    def wait(self, shard_refs, buf_refs, send, recv):
        for cp in self._copies(shard_refs, buf_refs, send, recv, True):
            cp.wait_recv()
        for cp in self._copies(shard_refs, buf_refs, send, recv, False):
            cp.wait_send()

    def sems(self):
        return [pltpu.SemaphoreType.DMA((self.n_sem,)), pltpu.SemaphoreType.DMA((self.n_sem,))]

    def call(self, srcs, bufs, name):
        ns, nb = len(srcs), len(bufs)

        def body(*refs):
            src_refs, buf_refs, (send, recv) = refs[:ns], refs[ns + nb:ns + 2 * nb], refs[ns + 2 * nb:]
            self.start(src_refs, buf_refs, send, recv)
            self.wait(src_refs, buf_refs, send, recv)

        return pl.pallas_call(
            body, name=name, in_specs=[ANY] * (ns + nb), out_specs=[ANY] * nb,
            out_shape=[jax.ShapeDtypeStruct(b.shape, b.dtype) for b in bufs],
            input_output_aliases={ns + t: t for t in range(nb)},
            scratch_shapes=self.sems(),
        )(*srcs, *bufs)


class _GradHop(_WeightGather):
    def __init__(self, hop, layer, slices):
        self.hop, self.layer, self.slices = hop, layer, slices
        self.n_sem = {"pair": N_CHIPS, "chips": N_CHIPS - 1, "gather": 1}[hop] * len(slices)

    def _copies(self, src_refs, buf_refs, send, recv, received):
        x, y, c, chips = _place()
        me, out = 2 * x + y, []

        def remote(src, dst, to):
            k = len(out)
            out.append(pltpu.make_async_remote_copy(src_ref=src, dst_ref=dst, send_sem=send.at[k], recv_sem=recv.at[k],
                                                    device_id=to, device_id_type=MESH))

        for t, (half, where) in enumerate(self.slices):
            if self.hop == "pair":
                for j, (a, first) in enumerate(where):
                    rows = pl.ds(pl.multiple_of(first + (1 - c) * half, 8), half)
                    remote(src_refs[a].at[rows], buf_refs[t].at[j], (x, y, 1 - c))
            elif self.hop == "chips":
                for chip in chips:
                    slot = 2 * chip[0] + chip[1]
                    remote(src_refs[t].at[slot], buf_refs[t].at[slot if received else me], (*chip, c))
            else:
                mine = buf_refs[t].at[self.layer, c]
                remote(mine, buf_refs[t].at[self.layer, 1 - c] if received else mine, (x, y, 1 - c))
        return out


def _add_halves(parts, got, c_idx, name):
    n, L, half, C = got.shape
    tr = 64

    def body(*refs):
        ins, (got_ref, o_ref) = refs[1:1 + len(parts)], refs[1 + len(parts):]
        for k in range(len(parts)):
            o_ref[k // L, k % L] = (ins[k][...] + got_ref[k // L, k % L]).astype(BF16)

    def rows_of(first):
        assert first % tr == 0 and half % tr == 0
        return lambda i, c: (first // tr + c[0] * (half // tr) + i, 0)

    whole = pl.BlockSpec((n, L, tr, C), lambda i, c: (0, 0, i, 0))
    return pl.pallas_call(
        body, name=name,
        grid_spec=pltpu.PrefetchScalarGridSpec(
            num_scalar_prefetch=1, grid=(half // tr,),
            in_specs=[pl.BlockSpec((tr, C), rows_of(first)) for _, first in parts] + [whole],
            out_specs=whole),
        out_shape=jax.ShapeDtypeStruct((n, L, half, C), BF16),
        compiler_params=_params(("parallel",)),
    )(c_idx, *[a for a, _ in parts], got)


def _add_slots(r, c_idx, dest, layer, name):
    n, half, C = r.shape
    tr = 256

    def body(c_ref, r_ref, dest_ref, o_ref):
        del dest_ref
        acc = r_ref[0].astype(F32)
        for j in range(1, n):
            acc = acc + r_ref[j].astype(F32)
        o_ref[0, 0] = acc

    return pl.pallas_call(
        body, name=name,
        grid_spec=pltpu.PrefetchScalarGridSpec(
            num_scalar_prefetch=1, grid=(half // tr,),
            in_specs=[pl.BlockSpec((n, tr, C), lambda i, c: (0, i, 0)), ANY],
            out_specs=pl.BlockSpec((1, 1, tr, C), lambda i, c: (layer, c[0], i, 0))),
        out_shape=jax.ShapeDtypeStruct(dest.shape, F32),
        input_output_aliases={2: 0},
        compiler_params=_params(("parallel",)),
    )(c_idx, r, dest)


class _LayerReduce:
    def __init__(self, layer, grads, slices, c_idx, chip, dests):
        self.layer, self.grads, self.slices, self.c_idx, self.chip, self.dests = layer, grads, slices, c_idx, chip, dests
        self.widths = [grads[where[0][0]].shape[1] for _, where in slices]
        self.stage = 0

    def _hop(self, kind):
        return _GradHop(kind, self.layer, self.slices)

    def rider(self):
        if self.stage == 0:
            got = [lax.empty((N_CHIPS, half, w), F32) for (half, _), w in zip(self.slices, self.widths)]
            return self._hop("pair"), self.grads, got
        if self.stage == 1:
            q = [lax.empty(p.shape, BF16) for p in self.pair_sums]
            return self._hop("chips"), self.pair_sums, q
        return self._hop("gather"), [], self.dests

    def landed(self, bufs):
        tag = f"{self.layer}"
        if self.stage == 0:
            self.pair_sums = []
            for t, ((half, where), got) in enumerate(zip(self.slices, bufs)):
                parts = [(self.grads[a], first) for a, first in where]
                p = _add_halves(parts, got[:, None], self.c_idx, f"rs_add2_{t}_{tag}")
                self.pair_sums.append(p.reshape(N_CHIPS, half, p.shape[-1]))
        elif self.stage == 1:
            for t, (q, p) in enumerate(zip(bufs, self.pair_sums)):
                q = lax.dynamic_update_slice_in_dim(q, lax.dynamic_slice_in_dim(p, self.chip, 1, axis=0), self.chip, axis=0)
                self.dests[t] = _add_slots(q, self.c_idx, self.dests[t], self.layer, f"rs_add4_{t}_{tag}")
        else:
            self.dests = list(bufs)
        self.stage += 1

    def step(self):
        hop, srcs, bufs = self.rider()
        self.landed(hop.call(srcs, bufs, f"rs_{hop.hop}_{self.layer}"))

    def finish(self):
        while self.stage < 3:
            self.step()
        return self.dests


def _all_reduce_small(buf, name):
    R = buf.shape[0]

    def flipped(k, x, y, c):
        return ((1 - x) if k & 4 else x, (1 - y) if k & 2 else y, (1 - c) if k & 1 else c)

    def body(b_ref, o_ref, land, send, recv):
        x, y, c, _ = _place()
        me = 4 * x + 2 * y + c
        land[me] = b_ref[...]
        cps = []
        for k in range(1, N_DEV):
            peer = flipped(k, x, y, c)
            cps.append(pltpu.make_async_remote_copy(src_ref=b_ref, dst_ref=land.at[me], send_sem=send.at[k - 1],
                                                    recv_sem=recv.at[k - 1], device_id=peer, device_id_type=MESH))
        for cp in cps:
            cp.start()
        for k in range(1, N_DEV):
            peer = flipped(k, x, y, c)
            slot = 4 * peer[0] + 2 * peer[1] + peer[2]
            pltpu.make_async_remote_copy(src_ref=b_ref, dst_ref=land.at[slot], send_sem=send.at[k - 1],
                                         recv_sem=recv.at[k - 1], device_id=peer, device_id_type=MESH).wait_recv()
        for cp in cps:
            cp.wait_send()
        acc = land[0]
        for j in range(1, N_DEV):
            acc = acc + land[j]
        o_ref[...] = acc

    vm = pl.BlockSpec(memory_space=pltpu.VMEM)
    return pl.pallas_call(
        body, name=name, in_specs=[vm], out_specs=vm,
        out_shape=jax.ShapeDtypeStruct((R, LANES), F32),
        scratch_shapes=[pltpu.VMEM((N_DEV, R, LANES), F32), pltpu.SemaphoreType.DMA((N_DEV - 1,)),
                        pltpu.SemaphoreType.DMA((N_DEV - 1,))],
    )(buf)


def kernel(x, mem, w_in, a_rel_bias, b_gate_w, b_gate_b, b_norm_g, w_mem_kv, w_out, ln_g, ln_b, loss_target, m_w_in, m_a_rel_bias, m_b_gate_w, m_b_gate_b, m_b_norm_g, m_w_mem_kv, m_w_out, m_ln_g, m_ln_b, v_w_in, v_a_rel_bias, v_b_gate_w, v_b_gate_b, v_b_norm_g, v_w_mem_kv, v_w_out, v_ln_g, v_ln_b):
    L = w_in.shape[0]
    S = x.shape[1]
    cx, cy, cc = lax.axis_index("x"), lax.axis_index("y"), lax.axis_index("c")
    chip = 2 * cx + cy
    c_idx = jnp.reshape(cc, (1,)).astype(jnp.int32)

    n_in, r_kv, r_out = w_in.shape[2], w_mem_kv.shape[1], w_out.shape[1]
    shards = [w.astype(BF16).reshape(-1, w.shape[2]) for w in (w_in, w_mem_kv, w_out)]
    rows = [D_MODEL, r_kv, r_out]

    def landing(l):
        return [lax.dynamic_update_slice_in_dim(lax.empty((N_CHIPS, r, s.shape[1]), BF16),
                                                s[l * r:(l + 1) * r][None], chip, axis=0)
                for s, r in zip(shards, rows)]

    def assembled(bufs):
        return (_padded_from_shards([bufs[0][j] for j in range(N_CHIPS)]),
                bufs[1].reshape(D_MODEL, bufs[1].shape[2]), bufs[2].reshape(D_MODEL, D_MODEL))

    bufs0 = _WeightGather("chips", 0, rows).call(shards, landing(0), "gather_chips_0")
    weights = [assembled(_WeightGather("pair", 0, rows).call(shards, bufs0, "gather_pair_0"))]

    gw_cols = b_gate_w.shape[2]
    gw_slot = jnp.zeros((N_CHIPS, L, GATE_RANK, gw_cols), F32)
    gw_slot = lax.dynamic_update_slice(gw_slot, (0.5 * b_gate_w)[None], (chip, 0, 0, 0))
    gw_flat = gw_slot.reshape(-1)
    n_gw = gw_flat.shape[0]
    pad = (-n_gw) % (8 * LANES)
    gw_full = _all_reduce_small(jnp.pad(gw_flat, (0, pad)).reshape(-1, LANES), "gather_gate_w").reshape(-1)[:n_gw]
    gw_full = gw_full.reshape(N_CHIPS, L, GATE_RANK, gw_cols).transpose(1, 2, 0, 3).reshape(L, GATE_RANK, B_KEY_WIDTH)

    xs = x.reshape(S, D_MODEL)
    mem_b = mem.reshape(mem.shape[1], D_MODEL).astype(BF16)
    target = loss_target.reshape(S, D_MODEL)

    small_w = []
    for l in range(L):
        gw_l, gb_l = _pad_gate(gw_full[l], b_gate_b[l])
        small_w.append((_bias_by_offset(a_rel_bias[l]), gw_l, gb_l,
                        b_norm_g[l].reshape(1, LANES), ln_g[l].reshape(1, D_MODEL), ln_b[l].reshape(1, D_MODEL)))

    y = xs
    yb, yt = _cast_transpose(xs, "x_t")
    saved = []
    for l in range(L):
        if l + 1 < L:
            rider = (_WeightGather("chips", l + 1, rows), shards, landing(l + 1))
            y, yb, yt, sv, bufs = _layer_fwd(y, yb, yt, mem_b, *weights[l], *small_w[l], rider=rider,
                                             next_hop=_WeightGather("pair", l + 1, rows))
            weights.append(assembled(bufs))
        else:
            y, yb, yt, sv = _layer_fwd(y, yb, yt, mem_b, *weights[l], *small_w[l])
        saved.append(sv)
    layer_w = [(*weights[l], *small_w[l]) for l in range(L)]
    loss_part, dy = _loss_head(y, target)

    halves = [D_MODEL // 2, r_kv // 2, r_out // 2]
    dests = [lax.empty((L, 2, hf, w.shape[2]), F32) for hf, w in zip(halves, (w_in, w_mem_kv, w_out))]
    slices = [(halves[0], [(j, 0) for j in range(N_CHIPS)]),
              (halves[1], [(N_CHIPS, j * r_kv) for j in range(N_CHIPS)]),
              (halves[2], [(N_CHIPS + 1, j * r_out) for j in range(N_CHIPS)])]

    def reduction(l, g, into):
        arrays = [_chip_columns(g[0], j, n_in) for j in range(N_CHIPS)] + [g[5], g[6]]
        return _LayerReduce(l, arrays, slices, c_idx, chip, into)

    grads, reduce = [None] * L, None
    for l in reversed(range(L)):
        w_in_l, w_kv_l, w_out_l, u_l, gw_l, gb_l, gn_l, lg_l, lb_l = layer_w[l]
        args = (dy, saved[l], mem_b, w_in_l, w_out_l, u_l, gw_l, gb_l, gn_l, lg_l)
        if l > 0:
            dy, grads[l] = _layer_bwd(*args, reduce=reduce)
            if reduce is not None:
                dests = reduce.finish()
            reduce = reduction(l, grads[l], dests)
        else:
            own = lambda g, above: reduction(0, g, dests if above is None else above)
            dy, grads[l], reduce = _layer_bwd(*args, reduce=reduce, own_reduce=own)
    r_w_in, r_w_kv, r_w_out = [d.reshape(L, 2 * d.shape[2], d.shape[3]) for d in reduce.finish()]
    grad_x = dy.reshape(x.shape)

    g_rel = jnp.stack([_bias_grad_from_offset(g[1]) for g in grads])
    g_gw = jnp.stack([_unpad_heads(g[2][LR_LANE:LR_LANE + GATE_RANK]) for g in grads])
    g_gb = jnp.stack([_unpad_heads(g[3])[0] for g in grads])
    g_gn = jnp.stack([g[4][0] for g in grads])
    g_lg = jnp.stack([g[7][0] for g in grads])
    g_lb = jnp.stack([g[8][0] for g in grads])

    small = [g_rel, g_gw, g_gb, g_gn, g_lg, g_lb, loss_part]
    flat = jnp.concatenate([s.reshape(-1) for s in small])
    n_small = flat.shape[0]
    pad = (-n_small) % (8 * LANES)
    red = _all_reduce_small(jnp.pad(flat, (0, pad)).reshape(-1, LANES), "all_reduce_small").reshape(-1)
    outs, off = [], 0
    for s in small:
        outs.append(red[off:off + s.size].reshape(s.shape))
        off += s.size
    g_rel, g_gw, g_gb, g_gn, g_lg, g_lb, loss = outs
    loss = loss.reshape(())
    g_gw = lax.dynamic_slice_in_dim(g_gw.reshape(L, GATE_RANK, N_CHIPS, gw_cols), chip, 1, axis=2).reshape(L, GATE_RANK, gw_cols)

    g_list = [r_w_in, g_rel, g_gw, g_gb, g_gn, r_w_kv, r_w_out, g_lg, g_lb]
    w_list = [w_in, a_rel_bias, b_gate_w, b_gate_b, b_norm_g, w_mem_kv, w_out, ln_g, ln_b]
    m_list = [m_w_in, m_a_rel_bias, m_b_gate_w, m_b_gate_b, m_b_norm_g, m_w_mem_kv, m_w_out, m_ln_g, m_ln_b]
    v_list = [v_w_in, v_a_rel_bias, v_b_gate_w, v_b_gate_b, v_b_norm_g, v_w_mem_kv, v_w_out, v_ln_g, v_ln_b]
    names = ["w_in", "rel", "gate_w", "gate_b", "norm_g", "w_kv", "w_out", "ln_g", "ln_b"]
    to_cols = lambda a: jnp.transpose(a, (2, 0, 1))
    upd = [tuple(jnp.transpose(o, (1, 2, 0)) for o in
                 _adamw(to_cols(w_in), to_cols(r_w_in), to_cols(m_w_in), to_cols(v_w_in), "adamw_w_in"))]
    upd += [_adamw_nd(w, g, m, v, "adamw_" + n)
            for w, g, m, v, n in list(zip(w_list, g_list, m_list, v_list, names))[1:]]
    deltas = [u_[0] for u_ in upd]
    new_m = [u_[1] for u_ in upd]
    new_v = [u_[2] for u_ in upd]
    return (loss, grad_x, *g_list, *deltas, *new_m, *new_v)
```

```python
import functools

import numpy as np
import jax
import jax.numpy as jnp
from jax import lax
from jax.experimental import pallas as pl
from jax.experimental.pallas import tpu as pltpu

F32 = jnp.float32
BF16 = jnp.bfloat16
MESH = pl.DeviceIdType.MESH

D_MODEL = 2048
DEPTH = 4
CHUNK = 64
LEFT_CHUNKS = 8
MAX_REL = 128
N_REL = 2 * MAX_REL + 1
A_HEADS = 8
HEAD_DIM = 128
B_HEADS = 4
B_DK = 64
M_HEADS = 4
GATE_RANK = 16
GATE_TAU = 16.0
A_WIDTH = A_HEADS * HEAD_DIM
B_WIDTH = B_HEADS * HEAD_DIM
B_KEY_WIDTH = B_HEADS * B_DK
M_WIDTH = M_HEADS * HEAD_DIM
IN_WIDTH = 4 * A_WIDTH + 2 * B_KEY_WIDTH + 2 * B_WIDTH + GATE_RANK + 2 * M_WIDTH
ALPHA = (2.0 * DEPTH) ** 0.25
LN_EPS = 1e-5
RMS_EPS = 1e-6
NEG_INF = -1e30
ADAM_LR = 0.001
ADAM_B1 = 0.9
ADAM_B2 = 0.999
ADAM_EPS = 1e-08
ADAM_WD = 0.01
ADAM_STEP = 10

LANES = 128
VMEM_LIMIT = 56 * 1024 * 1024

C_A, C_B, C_M = 0, 4096, 6144
HP = 7168
LR_HEAD, LR_LANE = B_HEADS - 1, B_DK
C_LR = C_B + LR_HEAD * 4 * LANES
SEG_Q, SEG_K, SEG_V, SEG_Z = 0, 1, 2, 3
TQ = 512
CPB = TQ // CHUNK
N_CHIPS = 4
N_DEV = 8


def _params(sem, vmem=VMEM_LIMIT):
    return pltpu.CompilerParams(dimension_semantics=sem, vmem_limit_bytes=vmem)


def _dot(a, b):
    return jnp.dot(a, b, preferred_element_type=F32)


def _dot_nt(a, b):
    return lax.dot_general(a, b, (((1,), (1,)), ((), ())), preferred_element_type=F32)


def _dot_tn(a, b):
    return lax.dot_general(a, b, (((0,), (0,)), ((), ())), preferred_element_type=F32)


def _sigmoid(x):
    return 1.0 / (1.0 + jnp.exp(-x))


def _split3(x):
    hi = x.astype(BF16)
    r = x - hi.astype(F32)
    mid = r.astype(BF16)
    lo = (r - mid.astype(F32)).astype(BF16)
    return hi, mid, lo


def _dot3(m_bf, x):
    hi, mid, lo = _split3(x)
    return _dot(m_bf, hi) + _dot(m_bf, mid) + _dot(m_bf, lo)


def _matmul(a, b, *, mode, out_dtype, tm, tn, tk, name, add=None, add_scale=1.0, rider=None):
    if mode == "nn":
        (M, K), (K2, N) = a.shape, b.shape
        a_spec = pl.BlockSpec((tm, tk), lambda i, j, k: (i, k))
        b_spec = pl.BlockSpec((tk, tn), lambda i, j, k: (k, j))
        dot = _dot
    elif mode == "nt":
        (M, K), (N, K2) = a.shape, b.shape
        a_spec = pl.BlockSpec((tm, tk), lambda i, j, k: (i, k))
        b_spec = pl.BlockSpec((tn, tk), lambda i, j, k: (j, k))
        dot = _dot_nt
    else:
        (K, M), (K2, N) = a.shape, b.shape
        a_spec = pl.BlockSpec((tk, tm), lambda i, j, k: (k, i))
        b_spec = pl.BlockSpec((tk, tn), lambda i, j, k: (k, j))
        dot = _dot_tn
    assert K == K2 and M % tm == 0 and N % tn == 0 and K % tk == 0, (a.shape, b.shape, mode)
    nk = K // tk
    has_add = add is not None
    assert nk == 1 or out_dtype == F32
    grid = (M // tm, N // tn, nk)
    hop, srcs, bufs = rider if rider is not None else (None, [], [])
    n_in = 2 + has_add

    def body(*refs):
        a_ref, b_ref = refs[:2]
        add_ref = refs[2] if has_add else None
        src_refs = refs[n_in:n_in + len(srcs)]
        o_ref = refs[n_in + len(srcs) + len(bufs)]
        buf_refs = refs[n_in + len(srcs) + len(bufs) + 1:n_in + len(srcs) + 2 * len(bufs) + 1]
        sems = refs[n_in + len(srcs) + 2 * len(bufs) + 1:]
        i, j, k = pl.program_id(0), pl.program_id(1), pl.program_id(2)
        if hop is not None:
            @pl.when(jnp.logical_and(jnp.logical_and(i == 0, j == 0), k == 0))
            def _():
                hop.start(src_refs, buf_refs, *sems)

        part = dot(a_ref[...].astype(BF16), b_ref[...].astype(BF16))

        @pl.when(k == 0)
        def _():
            first = part + add_scale * add_ref[...] if has_add else part
            o_ref[...] = first.astype(out_dtype)

        if nk > 1:
            @pl.when(k > 0)
            def _():
                o_ref[...] += part

        if hop is not None:
            @pl.when(jnp.logical_and(jnp.logical_and(i == grid[0] - 1, j == grid[1] - 1), k == nk - 1))
            def _():
                hop.wait(src_refs, buf_refs, *sems)

    in_specs = [a_spec, b_spec]
    args = [a, b]
    if has_add:
        in_specs.append(pl.BlockSpec((tm, tn), lambda i, j, k: (i, j)))
        args.append(add)
    out = pl.pallas_call(
        body,
        name=name,
        grid=grid,
        in_specs=in_specs + [ANY] * (len(srcs) + len(bufs)),
        out_specs=[pl.BlockSpec((tm, tn), lambda i, j, k: (i, j))] + [ANY] * len(bufs),
        out_shape=[jax.ShapeDtypeStruct((M, N), out_dtype)] + [jax.ShapeDtypeStruct(x.shape, x.dtype) for x in bufs],
        input_output_aliases={n_in + len(srcs) + t: 1 + t for t in range(len(bufs))},
        scratch_shapes=hop.sems() if hop is not None else [],
        compiler_params=_params(("parallel", "parallel", "arbitrary") if hop is None
                                else ("arbitrary", "arbitrary", "arbitrary")),
    )(*args, *srcs, *bufs)
    return out[0] if hop is None else (out[0], list(out[1:]))


def _cast_transpose(a, name):
    R, C = a.shape
    t = 512

    def body(a_ref, b_ref, t_ref):
        b = a_ref[...].astype(BF16)
        b_ref[...] = b
        t_ref[...] = b.T

    return pl.pallas_call(
        body, name=name, grid=(R // t, C // t),
        in_specs=[pl.BlockSpec((t, t), lambda i, j: (i, j))],
        out_specs=[pl.BlockSpec((t, t), lambda i, j: (i, j)), pl.BlockSpec((t, t), lambda i, j: (j, i))],
        out_shape=[jax.ShapeDtypeStruct((R, C), BF16), jax.ShapeDtypeStruct((C, R), BF16)],
        compiler_params=_params(("parallel", "parallel")),
    )(a)


def _chunk_of(rows):
    return lax.shift_right_logical(rows, CHUNK.bit_length() - 1)


A_HPS = 4
A_HW = A_HPS * LANES


def _band_bias(u_row, first):
    bias = pltpu.roll(jnp.broadcast_to(u_row, (TQ, 2 * TQ)), 0, 1, stride=1, stride_axis=0)
    qc = _chunk_of(lax.broadcasted_iota(jnp.int32, (TQ, 2 * TQ), 0))
    col = lax.broadcasted_iota(jnp.int32, (TQ, 2 * TQ), 1)
    kc = _chunk_of(jnp.bitwise_and(col, TQ - 1))
    ok = jnp.logical_or(jnp.logical_and(col < TQ, kc >= qc), jnp.logical_and(col >= TQ, kc <= qc))
    return jnp.where(ok, bias, NEG_INF) + jnp.where(col < TQ, first * NEG_INF, 0.0)


HQ = TQ // 2
HALVES = ((slice(0, HQ), slice(0, 3 * HQ)),
          (slice(HQ, TQ), slice(HQ, 4 * HQ)))


def _band_probs(q, kcat, bias):
    scale = HEAD_DIM ** -0.5
    out = []
    for rows, cols in HALVES:
        s = _dot_nt(q[rows], kcat[cols]) * scale + bias[rows, cols]
        p = jnp.exp(s - jnp.max(s, axis=1, keepdims=True))
        out.append((p, 1.0 / jnp.sum(p, axis=1, keepdims=True)))
    return out


def _band_specs(nq):
    def col(seg, h):
        return C_A // A_HW + 4 * h + seg

    q_spec = pl.BlockSpec((TQ, A_HW), lambda h, i: (jnp.minimum(i, nq - 1), col(SEG_Q, h)))
    kp_spec = pl.BlockSpec((TQ, A_HW), lambda h, i: (jnp.clip(i - 1, 0, nq - 1), col(SEG_K, h)))
    kc_spec = pl.BlockSpec((TQ, A_HW), lambda h, i: (jnp.minimum(i, nq - 1), col(SEG_K, h)))
    vp_spec = pl.BlockSpec((TQ, A_HW), lambda h, i: (jnp.clip(i - 1, 0, nq - 1), col(SEG_V, h)))
    vc_spec = pl.BlockSpec((TQ, A_HW), lambda h, i: (jnp.minimum(i, nq - 1), col(SEG_V, h)))
    z_spec = pl.BlockSpec((TQ, A_HW), lambda h, i: (jnp.minimum(i, nq - 1), col(SEG_Z, h)))
    u_spec = pl.BlockSpec((A_HPS, 1, 2 * TQ), lambda h, i: (h, 0, 0))
    return q_spec, kp_spec, kc_spec, vp_spec, vc_spec, z_spec, u_spec


def _band_fwd(h, u):
    S = h.shape[0]
    nq = S // TQ

    def body(q_ref, kp_ref, kc_ref, vp_ref, vc_ref, z_ref, u_ref, y_ref, yt_ref, bias_scr):
        i = pl.program_id(1)

        @pl.when(i <= 1)
        def _():
            for hh in range(A_HPS):
                bias_scr[hh] = _band_bias(u_ref[hh], (i == 0).astype(F32))

        for hh in range(A_HPS):
            cs = slice(hh * LANES, (hh + 1) * LANES)
            kcat = jnp.concatenate([kp_ref[:, cs], kc_ref[:, cs]], axis=0)
            vcat = jnp.concatenate([vp_ref[:, cs], vc_ref[:, cs]], axis=0)
            probs = _band_probs(q_ref[:, cs], kcat, bias_scr[hh])
            o = jnp.concatenate([_dot(p.astype(BF16), vcat[cols]) * inv
                                 for (p, inv), (_, cols) in zip(probs, HALVES)], axis=0)
            z = z_ref[:, cs].astype(F32)
            y = o * (z * _sigmoid(z))
            y_ref[:, cs] = y.astype(BF16)
            yt_ref[cs, :] = y.T.astype(BF16)

    specs = _band_specs(nq)
    return pl.pallas_call(
        body,
        name="band_fwd",
        grid=(A_HEADS // A_HPS, nq),
        in_specs=[specs[0], specs[1], specs[2], specs[3], specs[4], specs[5], specs[6]],
        out_specs=[pl.BlockSpec((TQ, A_HW), lambda h, i: (i, h)), pl.BlockSpec((A_HW, TQ), lambda h, i: (h, i))],
        out_shape=[jax.ShapeDtypeStruct((S, D_MODEL), BF16), jax.ShapeDtypeStruct((D_MODEL, S), BF16)],
        scratch_shapes=[pltpu.VMEM((A_HPS, TQ, 2 * TQ), F32)],
        compiler_params=_params(("parallel", "arbitrary")),
    )(h, h, h, h, h, h, u)


def _band_bwd(h, u, dycat):
    S = h.shape[0]
    nq = S // TQ
    scale = HEAD_DIM ** -0.5
    qs, ks, vs, zs = (slice(s * A_HW, (s + 1) * A_HW) for s in (SEG_Q, SEG_K, SEG_V, SEG_Z))

    def body(q_ref, kp_ref, kc_ref, vp_ref, vc_ref, z_ref, u_ref, dy_ref,
             dh_ref, du_ref, bias_scr, db_scr, ckt_scr, cvt_scr, cq_scr, cz_scr):
        i = pl.program_id(1)

        @pl.when(i <= 1)
        def _():
            for hh in range(A_HPS):
                bias_scr[hh] = _band_bias(u_ref[hh], (i == 0).astype(F32))

        @pl.when(i == 0)
        def _():
            db_scr[...] = jnp.zeros_like(db_scr)
            ckt_scr[...] = jnp.zeros_like(ckt_scr)
            cvt_scr[...] = jnp.zeros_like(cvt_scr)
            cq_scr[...] = jnp.zeros_like(cq_scr)
            cz_scr[...] = jnp.zeros_like(cz_scr)

        @pl.when(i < nq)
        def _():
            dh_ref[:, qs] = cq_scr[...]
            dh_ref[:, zs] = cz_scr[...]
            for hh in range(A_HPS):
                cs = slice(hh * LANES, (hh + 1) * LANES)
                q = q_ref[:, cs]
                kcat = jnp.concatenate([kp_ref[:, cs], kc_ref[:, cs]], axis=0)
                vcat = jnp.concatenate([vp_ref[:, cs], vc_ref[:, cs]], axis=0)
                probs = [p * inv for p, inv in _band_probs(q, kcat, bias_scr[hh])]
                o = jnp.concatenate([_dot(p.astype(BF16), vcat[cols]) for p, (_, cols) in zip(probs, HALVES)], axis=0)
                z = z_ref[:, cs].astype(F32)
                sg = _sigmoid(z)
                dy = dy_ref[:, cs].astype(F32)
                do = dy * (z * sg)
                cz_scr[:, cs] = (dy * o * (sg * (1.0 + z * (1.0 - sg)))).astype(BF16)
                dob = do.astype(BF16)
                delta = jnp.sum(do * o, axis=1, keepdims=True)
                qt, dot_ = q.T, dob.T
                dq, dkt, dvt = [], [], []
                for p, (rows, cols) in zip(probs, HALVES):
                    ds = p * (_dot_nt(dob[rows], vcat[cols]) - delta[rows])
                    db_scr[hh, rows, cols] += ds
                    dsb = ds.astype(BF16)
                    dq.append(scale * _dot(dsb, kcat[cols]))
                    dkt.append(scale * _dot(qt[:, rows], dsb))
                    dvt.append(_dot(dot_[:, rows], p.astype(BF16)))
                cq_scr[:, cs] = jnp.concatenate(dq, axis=0).astype(BF16)

                def over_keys(parts):
                    lo, hi = parts
                    prev = jnp.concatenate([lo[:, :HQ], lo[:, HQ:2 * HQ] + hi[:, :HQ]], axis=1)
                    cur = jnp.concatenate([lo[:, 2 * HQ:] + hi[:, HQ:2 * HQ], hi[:, 2 * HQ:]], axis=1)
                    return prev, cur

                (dk_prev, dk_cur), (dv_prev, dv_cur) = over_keys(dkt), over_keys(dvt)
                dh_ref[:, SEG_K * A_HW + hh * LANES:SEG_K * A_HW + (hh + 1) * LANES] = (
                    ckt_scr[cs, :] + dk_prev).T.astype(BF16)
                dh_ref[:, SEG_V * A_HW + hh * LANES:SEG_V * A_HW + (hh + 1) * LANES] = (
                    cvt_scr[cs, :] + dv_prev).T.astype(BF16)
                ckt_scr[cs, :] = dk_cur
                cvt_scr[cs, :] = dv_cur

        @pl.when(i == nq)
        def _():
            dh_ref[:, qs] = cq_scr[...]
            dh_ref[:, zs] = cz_scr[...]
            dh_ref[:, ks] = ckt_scr[...].T.astype(BF16)
            dh_ref[:, vs] = cvt_scr[...].T.astype(BF16)
            r0 = lax.broadcasted_iota(jnp.int32, (TQ, TQ), 0)
            r1 = lax.broadcasted_iota(jnp.int32, (TQ, TQ), 1)
            flip = (r0 + r1 == TQ - 1).astype(BF16)
            for hh in range(A_HPS):
                fl = _dot3(flip, db_scr[hh])
                rolled = pltpu.roll(fl, 0, 1, stride=1, stride_axis=0)
                du_ref[hh] = jnp.sum(rolled, axis=0, keepdims=True)

    specs = _band_specs(nq)
    dy_spec = pl.BlockSpec((TQ, A_HW), lambda h, i: (jnp.minimum(i, nq - 1), h))
    return pl.pallas_call(
        body,
        name="band_bwd",
        grid=(A_HEADS // A_HPS, nq + 1),
        in_specs=[specs[0], specs[1], specs[2], specs[3], specs[4], specs[5], specs[6], dy_spec],
        out_specs=[pl.BlockSpec((TQ, 4 * A_HW), lambda h, i: (jnp.maximum(i - 1, 0), C_A // (4 * A_HW) + h)),
                   pl.BlockSpec((A_HPS, 1, 2 * TQ), lambda h, i: (h, 0, 0))],
        out_shape=[jax.ShapeDtypeStruct((S, HP), BF16), jax.ShapeDtypeStruct((A_HEADS, 1, 2 * TQ), F32)],
        scratch_shapes=[pltpu.VMEM((A_HPS, TQ, 2 * TQ), F32), pltpu.VMEM((A_HPS, TQ, 2 * TQ), F32),
                        pltpu.VMEM((A_HW, TQ), F32), pltpu.VMEM((A_HW, TQ), F32),
                        pltpu.VMEM((TQ, A_HW), BF16), pltpu.VMEM((TQ, A_HW), BF16)],
        compiler_params=_params(("parallel", "arbitrary")),
    )(h, h, h, h, h, h, u, dycat)


def _bias_by_offset(table):
    far = jnp.broadcast_to(table[:, N_REL - 1:], (A_HEADS, TQ - MAX_REL))
    ramp = jnp.flip(table, axis=1)
    rest = jnp.broadcast_to(table[:, :1], (A_HEADS, 2 * TQ - CHUNK - (TQ + MAX_REL + 1)))
    wrap = jnp.broadcast_to(table[:, N_REL - 1:], (A_HEADS, CHUNK))
    return jnp.concatenate([far, ramp, rest, wrap], axis=1)[:, None, :]


def _bias_grad_from_offset(du):
    g = jnp.roll(du[:, 0, :], -(TQ - 1), axis=1)
    far = jnp.sum(g[:, :TQ - MAX_REL], axis=1) + jnp.sum(g[:, 2 * TQ - CHUNK:], axis=1)
    ramp = jnp.flip(g[:, TQ - MAX_REL:TQ + MAX_REL + 1], axis=1)
    return ramp.at[:, N_REL - 1].add(far)


def _mem_probs(q, mk):
    s = _dot_nt(q, mk) * (HEAD_DIM ** -0.5)
    p = jnp.exp(s - jnp.max(s, axis=1, keepdims=True))
    return p * (1.0 / jnp.sum(p, axis=1, keepdims=True))


def _mem_cols(hd, seg):
    return slice((2 * hd + seg) * LANES, (2 * hd + seg + 1) * LANES)


def _mem_fwd(h, mkv, ycat, ycat_t):
    S = h.shape[0]
    nm = mkv.shape[0]
    c0 = (A_WIDTH + B_WIDTH) // LANES

    def body(qz_ref, mkv_ref, yin_ref, ytin_ref, y_ref, yt_ref):
        del yin_ref, ytin_ref
        for hd in range(M_HEADS):
            lane = slice(hd * LANES, (hd + 1) * LANES)
            p = _mem_probs(qz_ref[:, _mem_cols(hd, 0)], mkv_ref[:, lane])
            o = _dot(p.astype(BF16), mkv_ref[:, M_WIDTH + hd * LANES:M_WIDTH + (hd + 1) * LANES])
            z = qz_ref[:, _mem_cols(hd, 1)].astype(F32)
            y = o * (z * _sigmoid(z))
            y_ref[:, lane] = y.astype(BF16)
            yt_ref[lane, :] = y.T.astype(BF16)

    return pl.pallas_call(
        body,
        name="mem_fwd",
        grid=(S // TQ,),
        in_specs=[pl.BlockSpec((TQ, 2 * M_WIDTH), lambda i: (i, C_M // (2 * M_WIDTH))),
                  pl.BlockSpec((nm, 2 * M_WIDTH), lambda i: (0, 0)), ANY, ANY],
        out_specs=[pl.BlockSpec((TQ, M_WIDTH), lambda i: (i, c0 // M_HEADS)),
                   pl.BlockSpec((M_WIDTH, TQ), lambda i: (c0 // M_HEADS, i))],
        out_shape=[jax.ShapeDtypeStruct(ycat.shape, BF16), jax.ShapeDtypeStruct(ycat_t.shape, BF16)],
        input_output_aliases={2: 0, 3: 1},
        compiler_params=_params(("parallel",)),
    )(h, mkv, ycat, ycat_t)


def _mem_bwd(h, mkv, dycat, dh):
    S = h.shape[0]
    nm = mkv.shape[0]
    scale = HEAD_DIM ** -0.5

    def body(qz_ref, mkv_ref, dy_ref, dhin_ref, dh_ref, dmkv_ref):
        del dhin_ref
        i = pl.program_id(0)
        for hd in range(M_HEADS):
            lane = slice(hd * LANES, (hd + 1) * LANES)
            lane_v = slice(M_WIDTH + hd * LANES, M_WIDTH + (hd + 1) * LANES)
            q, mk, mv = qz_ref[:, _mem_cols(hd, 0)], mkv_ref[:, lane], mkv_ref[:, lane_v]
            p = _mem_probs(q, mk)
            pb = p.astype(BF16)
            o = _dot(pb, mv)
            z = qz_ref[:, _mem_cols(hd, 1)].astype(F32)
            sg = _sigmoid(z)
            dy = dy_ref[:, lane].astype(F32)
            do = dy * (z * sg)
            dh_ref[:, _mem_cols(hd, 1)] = (dy * o * (sg * (1.0 + z * (1.0 - sg)))).astype(BF16)
            dob = do.astype(BF16)
            ds = p * (_dot_nt(dob, mv) - jnp.sum(do * o, axis=1, keepdims=True))
            dsb = ds.astype(BF16)
            dh_ref[:, _mem_cols(hd, 0)] = (scale * _dot(dsb, mk)).astype(BF16)
            dmk = (scale * _dot(q.T, dsb)).T
            dmv = _dot(dob.T, pb).T

            @pl.when(i == 0)
            def _():
                dmkv_ref[:, lane] = dmk
                dmkv_ref[:, lane_v] = dmv

            @pl.when(i > 0)
            def _():
                dmkv_ref[:, lane] += dmk
                dmkv_ref[:, lane_v] += dmv

    return pl.pallas_call(
        body,
        name="mem_bwd",
        grid=(S // TQ,),
        in_specs=[pl.BlockSpec((TQ, 2 * M_WIDTH), lambda i: (i, C_M // (2 * M_WIDTH))),
                  pl.BlockSpec((nm, 2 * M_WIDTH), lambda i: (0, 0)),
                  pl.BlockSpec((TQ, M_WIDTH), lambda i: (i, (A_WIDTH + B_WIDTH) // M_WIDTH)), ANY],
        out_specs=[pl.BlockSpec((TQ, 2 * M_WIDTH), lambda i: (i, C_M // (2 * M_WIDTH))),
                   pl.BlockSpec((nm, 2 * M_WIDTH), lambda i: (0, 0))],
        out_shape=[jax.ShapeDtypeStruct(dh.shape, BF16), jax.ShapeDtypeStruct((nm, 2 * M_WIDTH), F32)],
        input_output_aliases={3: 0},
        compiler_params=_params(("arbitrary",)),
    )(h, mkv, dycat, dh)


SUB = 2 * CHUNK
SUBS = [slice(s * SUB, (s + 1) * SUB) for s in range(TQ // SUB)]


def _chunk_masks():
    r = lax.broadcasted_iota(jnp.int32, (SUB, SUB), 0)
    c = lax.broadcasted_iota(jnp.int32, (SUB, SUB), 1)
    same = _chunk_of(r) == _chunk_of(c)
    return jnp.logical_and(same, c <= r), jnp.logical_and(same, c > r), jnp.logical_and(same, c >= r)


def _by_sub(fn):
    return jnp.concatenate([fn(rows) for rows in SUBS], axis=0)


def _gla_gates(lr, gw, gb):
    logit = _dot(lr, gw) + gb
    sg = _sigmoid(logit)
    g = (jnp.minimum(logit, 0.0) - jnp.log(1.0 + jnp.exp(-jnp.abs(logit)))) * (1.0 / GATE_TAU)
    lo = _chunk_masks()[0].astype(BF16)
    return sg, _by_sub(lambda rows: _dot3(lo, g[rows]))


def _gla_factors(q, k, b):
    eb = jnp.exp(b)
    enb = jnp.exp(-b)
    return eb, enb, q * eb, q * enb, k * eb, k * enb


def _gla_intra(qp, qn, kp, kn):
    lo, up, _ = _chunk_masks()
    qp, qn, kp, kn = qp.astype(BF16), qn.astype(BF16), kp.astype(BF16), kn.astype(BF16)
    return [jnp.where(lo, _dot_nt(qp[rows], kn[rows]), 0.0) + jnp.where(up, _dot_nt(qn[rows], kp[rows]), 0.0)
            for rows in SUBS]


B_HPS = 4
B_HW = B_HPS * LANES


def _gla_specs(nb, rev):
    blk = (lambda i: nb - 1 - i) if rev else (lambda i: i)
    qkvz_spec = pl.BlockSpec((TQ, 4 * B_HW), lambda i, p: (blk(i), C_B // (4 * B_HW) + p))
    lr_spec = pl.BlockSpec((TQ, LANES), lambda i, p: (blk(i), C_LR // LANES))
    gw_spec = pl.BlockSpec((LANES, B_HW), lambda i, p: (0, p))
    gb_spec = pl.BlockSpec((1, B_HW), lambda i, p: (0, p))
    gn_spec = pl.BlockSpec((1, LANES), lambda i, p: (0, 0))
    return qkvz_spec, lr_spec, gw_spec, gb_spec, gn_spec, blk


def _head_cols(hh, seg):
    return slice((4 * hh + seg) * LANES, (4 * hh + seg + 1) * LANES)


def _gla_fwd(h, gw, gb, gn, ycat, ycat_t):
    S = h.shape[0]
    nb = S // TQ
    c0 = A_WIDTH // LANES

    def body(qkvz_ref, lr_ref, gw_ref, gb_ref, gn_ref, yin_ref, ytin_ref,
             y_ref, yt_ref, o_ref, st_ref, st_scr):
        del yin_ref, ytin_ref
        i, p = pl.program_id(0), pl.program_id(1)
        for hh in range(B_HPS):
            hd = B_HPS * p + hh
            lane = slice(hh * LANES, (hh + 1) * LANES)

            @pl.when(i == 0)
            def _():
                st_scr[hd] = jnp.zeros((LANES, LANES), F32)

            q = qkvz_ref[:, _head_cols(hh, SEG_Q)].astype(F32) * (B_DK ** -0.5)
            k = qkvz_ref[:, _head_cols(hh, SEG_K)].astype(F32)
            v = qkvz_ref[:, _head_cols(hh, SEG_V)]
            _, b = _gla_gates(lr_ref[...], gw_ref[:, lane], gb_ref[:, lane])
            _, _, qp, qn, kp, kn = _gla_factors(q, k, b)
            o_intra = jnp.concatenate([_dot(a.astype(BF16), v[rows])
                                       for a, rows in zip(_gla_intra(qp, qn, kp, kn), SUBS)], axis=0)
            qpb, knb = qp.astype(BF16), kn.astype(BF16)
            st = st_scr[hd]
            outs = []
            for c in range(CPB):
                rows = slice(c * CHUNK, (c + 1) * CHUNK)
                st_ref[hh, c] = st
                outs.append(_dot_nt(qpb[rows], st.astype(BF16)))
                e_last = jnp.exp(b[(c + 1) * CHUNK - 1:(c + 1) * CHUNK, :])
                st = (st + _dot_tn(v[rows], knb[rows])) * e_last
            st_scr[hd] = st
            o = o_intra + jnp.concatenate(outs, axis=0)
            o_ref[:, lane] = o
            r = lax.rsqrt(jnp.mean(o * o, axis=1, keepdims=True) + RMS_EPS)
            z = qkvz_ref[:, _head_cols(hh, SEG_Z)].astype(F32)
            y = o * r * gn_ref[...] * (z * _sigmoid(z))
            y_ref[:, lane] = y.astype(BF16)
            yt_ref[lane, :] = y.T.astype(BF16)

    qkvz_s, lr_s, gw_s, gb_s, gn_s, _ = _gla_specs(nb, False)
    return pl.pallas_call(
        body,
        name="gla_fwd",
        grid=(nb, B_HEADS // B_HPS),
        in_specs=[qkvz_s, lr_s, gw_s, gb_s, gn_s, ANY, ANY],
        out_specs=[pl.BlockSpec((TQ, B_HW), lambda i, p: (i, c0 // B_HPS + p)),
                   pl.BlockSpec((B_HW, TQ), lambda i, p: (c0 // B_HPS + p, i)),
                   pl.BlockSpec((TQ, B_HW), lambda i, p: (i, p)),
                   pl.BlockSpec((B_HPS, CPB, LANES, LANES), lambda i, p: (p, i, 0, 0))],
        out_shape=[jax.ShapeDtypeStruct(ycat.shape, BF16), jax.ShapeDtypeStruct(ycat_t.shape, BF16),
                   jax.ShapeDtypeStruct((S, B_WIDTH), F32),
                   jax.ShapeDtypeStruct((B_HEADS, S // CHUNK, LANES, LANES), F32)],
        input_output_aliases={5: 0, 6: 1},
        scratch_shapes=[pltpu.VMEM((B_HEADS, LANES, LANES), F32)],
        compiler_params=_params(("arbitrary", "arbitrary")),
    )(h, h, gw, gb, gn, ycat, ycat_t)


def _gla_bwd(h, gw, gb, gn, o_pre, states, dycat, dh):
    S = h.shape[0]
    nb = S // TQ
    n_steps = B_HEADS // B_HPS

    def body(qkvz_ref, lr_ref, gw_ref, gb_ref, gn_ref, o_ref, st_ref, dy_ref, dhin_ref,
             dh_ref, dgw_ref, dgb_ref, dgn_ref,
             dst_scr, dgw_scr, dgb_scr, dgn_scr, dlr_scr):
        del dhin_ref
        i, p = pl.program_id(0), pl.program_id(1)

        @pl.when(jnp.logical_and(i == 0, p == 0))
        def _():
            dgn_scr[...] = jnp.zeros_like(dgn_scr)

        dlr_heads = [one_head(hh, i, p, qkvz_ref, lr_ref, gw_ref, gb_ref, gn_ref, o_ref, st_ref, dy_ref,
                              dh_ref, dgw_ref, dgb_ref, dst_scr, dgw_scr, dgb_scr, dgn_scr) for hh in range(B_HPS)]
        dlr = dlr_heads[0]
        for more in dlr_heads[1:]:
            dlr = dlr + more

        @pl.when(p == 0)
        def _():
            dlr_scr[...] = dlr

        @pl.when(p > 0)
        def _():
            dlr_scr[...] += dlr

        @pl.when(p == n_steps - 1)
        def _():
            cols = _head_cols(B_HPS - 1, SEG_Q)
            dh_ref[:, cols] = (dh_ref[:, cols].astype(F32) + dlr_scr[...]).astype(BF16)

        @pl.when(i == nb - 1)
        def _():
            dgn_ref[...] = dgn_scr[...]

    def one_head(hh, i, p, qkvz_ref, lr_ref, gw_ref, gb_ref, gn_ref, o_ref, st_ref, dy_ref,
                 dh_ref, dgw_ref, dgb_ref, dst_scr, dgw_scr, dgb_scr, dgn_scr):
        hd = B_HPS * p + hh
        lane = slice(hh * LANES, (hh + 1) * LANES)

        @pl.when(i == 0)
        def _():
            dst_scr[hd] = jnp.zeros((LANES, LANES), F32)
            dgw_scr[hd] = jnp.zeros((LANES, LANES), F32)
            dgb_scr[hd] = jnp.zeros((1, LANES), F32)

        q = qkvz_ref[:, _head_cols(hh, SEG_Q)].astype(F32) * (B_DK ** -0.5)
        k = qkvz_ref[:, _head_cols(hh, SEG_K)].astype(F32)
        v = qkvz_ref[:, _head_cols(hh, SEG_V)]
        lr, gwv = lr_ref[...], gw_ref[:, lane]
        sg, b = _gla_gates(lr, gwv, gb_ref[:, lane])
        eb, enb, qp, qn, kp, kn = _gla_factors(q, k, b)
        a = _gla_intra(qp, qn, kp, kn)
        qpb, qnb, kpb, knb = qp.astype(BF16), qn.astype(BF16), kp.astype(BF16), kn.astype(BF16)

        o = o_ref[:, lane]
        gn = gn_ref[...]
        r = lax.rsqrt(jnp.mean(o * o, axis=1, keepdims=True) + RMS_EPS)
        z = qkvz_ref[:, _head_cols(hh, SEG_Z)].astype(F32)
        sz = _sigmoid(z)
        dy = dy_ref[:, lane].astype(F32)
        d_on = dy * (z * sz)
        dh_ref[:, _head_cols(hh, SEG_Z)] = (dy * (o * r * gn) * (sz * (1.0 + z * (1.0 - sz)))).astype(BF16)
        dgn_scr[...] += jnp.sum(d_on * o * r, axis=0, keepdims=True)
        t = d_on * gn
        do = r * t - o * (r * r * r) * jnp.mean(t * o, axis=1, keepdims=True)
        dob = do.astype(BF16)

        lo, up, upper = _chunk_masks()
        dqp, dkn, dqn, dkp, dv = [], [], [], [], []
        for a_s, rows in zip(a, SUBS):
            da = _dot_nt(dob[rows], v[rows])
            dalo = jnp.where(lo, da, 0.0).astype(BF16)
            daup = jnp.where(up, da, 0.0).astype(BF16)
            dqp.append(_dot(dalo, knb[rows]))
            dkn.append(_dot_tn(dalo, qpb[rows]))
            dqn.append(_dot(daup, kpb[rows]))
            dkp.append(_dot_tn(daup, qnb[rows]))
            dv.append(_dot_tn(a_s.astype(BF16), dob[rows]))
        dqp, dkn, dqn, dkp, dv = (jnp.concatenate(x, axis=0) for x in (dqp, dkn, dqn, dkp, dv))

        dst = dst_scr[hd]
        dqp_c, dkn_c, dv_c, dbl_c = [None] * CPB, [None] * CPB, [None] * CPB, [None] * CPB
        for c in reversed(range(CPB)):
            rows = slice(c * CHUNK, (c + 1) * CHUNK)
            st = st_ref[hh, c]
            e_last = jnp.exp(b[(c + 1) * CHUNK - 1:(c + 1) * CHUNK, :])
            if c == CPB - 1:
                st_next = (st + _dot_tn(v[rows], knb[rows])) * e_last
            else:
                st_next = st_ref[hh, c + 1]
            dbl_c[c] = jnp.sum(dst * st_next, axis=0, keepdims=True)
            dtt = (dst * e_last).astype(BF16)
            dv_c[c] = _dot_nt(knb[rows], dtt)
            dkn_c[c] = _dot(v[rows], dtt)
            dqp_c[c] = _dot(dob[rows], st.astype(BF16))
            dst = _dot_tn(dob[rows], qpb[rows]) + dst * e_last
        dst_scr[hd] = dst
        dqp = dqp + jnp.concatenate(dqp_c, axis=0)
        dkn = dkn + jnp.concatenate(dkn_c, axis=0)
        dv = dv + jnp.concatenate(dv_c, axis=0)
        dh_ref[:, _head_cols(hh, SEG_V)] = dv.astype(BF16)
        dh_ref[:, _head_cols(hh, SEG_Q)] = ((dqp * eb + dqn * enb) * (B_DK ** -0.5)).astype(BF16)
        dh_ref[:, _head_cols(hh, SEG_K)] = (dkp * eb + dkn * enb).astype(BF16)

        last = jnp.bitwise_and(lax.broadcasted_iota(jnp.int32, (TQ, 1), 0), CHUNK - 1) == CHUNK - 1
        dbl = jnp.concatenate([jnp.broadcast_to(x, (CHUNK, LANES)) for x in dbl_c], axis=0)
        db = dqp * qp - dqn * qn + dkp * kp - dkn * kn + jnp.where(last, dbl, 0.0)
        upper_b = upper.astype(BF16)
        dlogit = _by_sub(lambda rows: _dot3(upper_b, db[rows])) * (1.0 / GATE_TAU) * (1.0 - sg)
        dlb = dlogit.astype(BF16)
        dgw_scr[hd] += _dot_tn(lr, dlb)
        dgb_scr[hd] += jnp.sum(dlogit, axis=0, keepdims=True)

        @pl.when(i == nb - 1)
        def _():
            dgw_ref[:, lane] = dgw_scr[hd]
            dgb_ref[:, lane] = dgb_scr[hd]

        return _dot_nt(dlb, gwv)

    qkvz_s, lr_s, gw_s, gb_s, gn_s, blk = _gla_specs(nb, True)
    row = pl.BlockSpec((TQ, B_HW), lambda i, p: (blk(i), p))
    dy_spec = pl.BlockSpec((TQ, B_HW), lambda i, p: (blk(i), A_WIDTH // B_HW + p))
    st_spec = pl.BlockSpec((B_HPS, CPB, LANES, LANES), lambda i, p: (p, blk(i), 0, 0))
    return pl.pallas_call(
        body,
        name="gla_bwd",
        grid=(nb, B_HEADS // B_HPS),
        in_specs=[qkvz_s, lr_s, gw_s, gb_s, gn_s, row, st_spec, dy_spec, ANY],
        out_specs=[pl.BlockSpec((TQ, 4 * B_HW), lambda i, p: (blk(i), C_B // (4 * B_HW) + p)),
                   pl.BlockSpec((LANES, B_HW), lambda i, p: (0, jnp.where(i == nb - 1, p, 0))),
                   pl.BlockSpec((1, B_HW), lambda i, p: (0, jnp.where(i == nb - 1, p, 0))),
                   pl.BlockSpec((1, LANES), lambda i, p: (0, 0))],
        out_shape=[jax.ShapeDtypeStruct(dh.shape, BF16),
                   jax.ShapeDtypeStruct((LANES, B_HEADS * LANES), F32),
                   jax.ShapeDtypeStruct((1, B_HEADS * LANES), F32),
                   jax.ShapeDtypeStruct((1, LANES), F32)],
        input_output_aliases={8: 0},
        scratch_shapes=[pltpu.VMEM((B_HEADS, LANES, LANES), F32), pltpu.VMEM((B_HEADS, LANES, LANES), F32),
                        pltpu.VMEM((B_HEADS, 1, LANES), F32), pltpu.VMEM((1, LANES), F32),
                        pltpu.VMEM((TQ, LANES), F32)],
        compiler_params=_params(("arbitrary", "arbitrary")),
    )(h, h, gw, gb, gn, o_pre, states, dycat, dh)


LN_ROWS = 512


def _resident(shape):
    return pl.BlockSpec(shape, lambda i: (0,) * len(shape), pipeline_mode=pl.Buffered(1))


def _outproj_ln(ycat, w_out, x, g, b, rider=None):
    S = x.shape[0]
    n = S // LN_ROWS
    hop, srcs, bufs = rider if rider is not None else (None, [], [])

    def body(*refs):
        yc_ref, w_ref, x_ref, g_ref, b_ref = refs[:5]
        src_refs = refs[5:5 + len(srcs)]
        outs = refs[5 + len(srcs) + len(bufs):]
        y_ref, yb_ref, yt_ref, xh_ref, rs_ref = outs[:5]
        buf_refs, sems = outs[5:5 + len(bufs)], outs[5 + len(bufs):]
        i = pl.program_id(0)
        if hop is not None:
            @pl.when(i == 0)
            def _():
                hop.start(src_refs, buf_refs, *sems)

        u = ALPHA * x_ref[...] + _dot(yc_ref[...], w_ref[...])
        mu = jnp.mean(u, axis=1, keepdims=True)
        d = u - mu
        rstd = lax.rsqrt(jnp.mean(d * d, axis=1, keepdims=True) + LN_EPS)
        xh = d * rstd
        y = xh * g_ref[...] + b_ref[...]
        y_ref[...] = y
        yb_ref[...] = y.astype(BF16)
        yt_ref[...] = y.T.astype(BF16)
        xh_ref[...] = xh
        rs_ref[...] = rstd
        if hop is not None:
            @pl.when(i == n - 1)
            def _():
                hop.wait(src_refs, buf_refs, *sems)

    row = lambda w: pl.BlockSpec((LN_ROWS, w), lambda i: (i, 0))
    vec = pl.BlockSpec((1, D_MODEL), lambda i: (0, 0))
    n_in = 5 + len(srcs)
    out = pl.pallas_call(
        body,
        name="outproj_ln",
        grid=(n,),
        in_specs=[row(D_MODEL), _resident((D_MODEL, D_MODEL)), row(D_MODEL), vec, vec] + [ANY] * (len(srcs) + len(bufs)),
        out_specs=[row(D_MODEL), row(D_MODEL), pl.BlockSpec((D_MODEL, LN_ROWS), lambda i: (0, i)), row(D_MODEL), row(1)]
        + [ANY] * len(bufs),
        out_shape=[jax.ShapeDtypeStruct((S, D_MODEL), F32), jax.ShapeDtypeStruct((S, D_MODEL), BF16),
                   jax.ShapeDtypeStruct((D_MODEL, S), BF16),
                   jax.ShapeDtypeStruct((S, D_MODEL), F32), jax.ShapeDtypeStruct((S, 1), F32)]
        + [jax.ShapeDtypeStruct(x_.shape, x_.dtype) for x_ in bufs],
        input_output_aliases={n_in + t: 5 + t for t in range(len(bufs))},
        scratch_shapes=hop.sems() if hop is not None else [],
        compiler_params=_params(("parallel",) if hop is None else ("arbitrary",)),
    )(ycat, w_out, x, g, b, *srcs, *bufs)
    return tuple(out[:5]) if hop is None else (tuple(out[:5]), list(out[5:]))


def _ln_bwd_dycat(dy, xhat, rstd, g, w_out, rider=None):
    S = dy.shape[0]
    n = S // LN_ROWS
    hop, srcs, bufs = rider if rider is not None else (None, [], [])

    def body(*refs):
        dy_ref, xh_ref, rs_ref, g_ref, w_ref = refs[:5]
        src_refs = refs[5:5 + len(srcs)]
        outs = refs[5 + len(srcs) + len(bufs):]
        du_ref, dub_ref, dyc_ref, dg_ref, db_ref = outs[:5]
        buf_refs, sems = outs[5:5 + len(bufs)], outs[5 + len(bufs):]
        i = pl.program_id(0)
        if hop is not None:
            @pl.when(i == 0)
            def _():
                hop.start(src_refs, buf_refs, *sems)

        dy_, xh = dy_ref[...], xh_ref[...]
        dyg = dy_ * g_ref[...]
        m1 = jnp.mean(dyg, axis=1, keepdims=True)
        m2 = jnp.mean(dyg * xh, axis=1, keepdims=True)
        du = rs_ref[...] * (dyg - m1 - xh * m2)
        dub = du.astype(BF16)
        du_ref[...] = du
        dub_ref[...] = dub
        dyc_ref[...] = _dot_nt(dub, w_ref[...]).astype(BF16)
        dg = jnp.sum(dy_ * xh, axis=0, keepdims=True)
        db = jnp.sum(dy_, axis=0, keepdims=True)

        @pl.when(i == 0)
        def _():
            dg_ref[...] = dg
            db_ref[...] = db

        @pl.when(i > 0)
        def _():
            dg_ref[...] += dg
            db_ref[...] += db

        if hop is not None:
            @pl.when(i == n - 1)
            def _():
                hop.wait(src_refs, buf_refs, *sems)

    row = lambda w: pl.BlockSpec((LN_ROWS, w), lambda i: (i, 0))
    vec = pl.BlockSpec((1, D_MODEL), lambda i: (0, 0))
    n_in = 5 + len(srcs)
    out = pl.pallas_call(
        body,
        name="ln_bwd_dycat",
        grid=(n,),
        in_specs=[row(D_MODEL), row(D_MODEL), row(1), vec, pl.BlockSpec((D_MODEL, D_MODEL), lambda i: (0, 0))]
        + [ANY] * (len(srcs) + len(bufs)),
        out_specs=[row(D_MODEL), row(D_MODEL), row(D_MODEL), vec, vec] + [ANY] * len(bufs),
        out_shape=[jax.ShapeDtypeStruct((S, D_MODEL), F32), jax.ShapeDtypeStruct((S, D_MODEL), BF16),
                   jax.ShapeDtypeStruct((S, D_MODEL), BF16),
                   jax.ShapeDtypeStruct((1, D_MODEL), F32), jax.ShapeDtypeStruct((1, D_MODEL), F32)]
        + [jax.ShapeDtypeStruct(b.shape, b.dtype) for b in bufs],
        input_output_aliases={n_in + t: 5 + t for t in range(len(bufs))},
        scratch_shapes=hop.sems() if hop is not None else [],
        compiler_params=_params(("arbitrary",)),
    )(dy, xhat, rstd, g, w_out, *srcs, *bufs)
    return tuple(out[:5]) if hop is None else (tuple(out[:5]), list(out[5:]))


def _loss_head(y, target):
    S = y.shape[0]

    def body(y_ref, t_ref, l_ref, dy_ref):
        i = pl.program_id(0)
        err = y_ref[...] - t_ref[...]
        dy_ref[...] = err * (1.0 / D_MODEL)
        part = (0.5 / D_MODEL) * jnp.sum(jnp.sum(err * err, axis=1, keepdims=True), axis=0, keepdims=True)

        @pl.when(i == 0)
        def _():
            l_ref[...] = part

        @pl.when(i > 0)
        def _():
            l_ref[...] += part

    row = pl.BlockSpec((TQ, D_MODEL), lambda i: (i, 0))
    return pl.pallas_call(
        body,
        name="loss_head",
        grid=(S // TQ,),
        in_specs=[row, row],
        out_specs=[pl.BlockSpec((1, 1), lambda i: (0, 0)), row],
        out_shape=[jax.ShapeDtypeStruct((1, 1), F32), jax.ShapeDtypeStruct((S, D_MODEL), F32)],
        compiler_params=_params(("arbitrary",)),
    )(y, target)


def _pad_gate(gate_w, gate_b):
    gw = gate_w.reshape(GATE_RANK, B_HEADS, B_DK)
    gw = jnp.pad(gw, ((LR_LANE, LANES - LR_LANE - GATE_RANK), (0, 0), (0, LANES - B_DK))).reshape(LANES, B_HEADS * LANES)
    gb = jnp.pad(gate_b.reshape(B_HEADS, B_DK), ((0, 0), (0, LANES - B_DK))).reshape(1, B_HEADS * LANES)
    return gw.astype(BF16), gb.astype(F32)


def _layer_fwd(x, xb, xt, mem_b, w_in, w_kv, w_out, u, gw, gb, gn, ln_g, ln_b, rider=None, next_hop=None):
    h = _matmul(xb, w_in, mode="nn", out_dtype=BF16, tm=1024, tn=1792, tk=D_MODEL, name="in_proj", rider=rider)
    if rider is not None:
        h, rode = h
    mkv = _matmul(mem_b, w_kv, mode="nn", out_dtype=BF16, tm=mem_b.shape[0], tn=1024, tk=D_MODEL, name="mem_kv")
    ycat, ycat_t = _band_fwd(h, u)
    ycat, ycat_t, o_pre, states = _gla_fwd(h, gw, gb, gn, ycat, ycat_t)
    ycat, ycat_t = _mem_fwd(h, mkv, ycat, ycat_t)
    if next_hop is None:
        y, ybf, yt, xhat, rstd = _outproj_ln(ycat, w_out, x, ln_g, ln_b)
    else:
        (y, ybf, yt, xhat, rstd), rode = _outproj_ln(ycat, w_out, x, ln_g, ln_b, rider=(next_hop, [], rode))
    saved = (xt, h, mkv, ycat_t, o_pre, states, xhat, rstd)
    return (y, ybf, yt, saved) if rider is None else (y, ybf, yt, saved, rode)


def _layer_bwd(dy, saved, mem_b, w_in, w_out, u, gw, gb, gn, ln_g, reduce=None, own_reduce=None):
    xt, h, mkv, ycat_t, o_pre, states, xhat, rstd = saved

    def riding(**kw):
        if reduce is None:
            return _matmul(**kw)
        out, bufs = _matmul(rider=reduce.rider(), **kw)
        reduce.landed(bufs)
        return out

    if reduce is None:
        du, dub, dycat, d_ln_g, d_ln_b = _ln_bwd_dycat(dy, xhat, rstd, ln_g, w_out)
    else:
        (du, dub, dycat, d_ln_g, d_ln_b), bufs = _ln_bwd_dycat(dy, xhat, rstd, ln_g, w_out, rider=reduce.rider())
        reduce.landed(bufs)
    d_w_out = _matmul(ycat_t, dub, mode="nn", out_dtype=F32, tm=1024, tn=1024, tk=min(4096, dub.shape[0]), name="d_w_out")
    dh, d_u = _band_bwd(h, u, dycat)
    dh, dgw, dgb, dgn = _gla_bwd(h, gw, gb, gn, o_pre, states, dycat, dh)
    dh, dmkv = _mem_bwd(h, mkv, dycat, dh)
    d_w_kv = _matmul(mem_b, dmkv, mode="tn", out_dtype=F32, tm=1024, tn=1024, tk=mem_b.shape[0], name="d_w_kv")
    dx_args = dict(a=dh, b=w_in, mode="nt", out_dtype=F32, tm=1024, tn=1024, tk=3584, name="dx", add=du, add_scale=ALPHA)
    dw_args = dict(a=xt, b=dh, mode="nn", out_dtype=F32, tm=1024, tn=1024, tk=min(4096, dh.shape[0]), name="d_w_in")
    if own_reduce is None:
        dx = riding(**dx_args)
        d_w_in = riding(**dw_args)
        return dx, (d_w_in, d_u, dgw, dgb, dgn, d_w_kv, d_w_out, d_ln_g, d_ln_b)
    d_w_in = riding(**dw_args)
    grads = (d_w_in, d_u, dgw, dgb, dgn, d_w_kv, d_w_out, d_ln_g, d_ln_b)
    own = own_reduce(grads, reduce.finish() if reduce is not None else None)
    own.step()
    dx, bufs = _matmul(rider=own.rider(), **dx_args)
    own.landed(bufs)
    return dx, grads, own


def _unpad_heads(w):
    r = w.shape[0]
    return w.reshape(r, B_HEADS, LANES)[:, :, :B_DK].reshape(r, B_KEY_WIDTH)


def _padded_col_of():
    col, o = np.zeros(IN_WIDTH, np.int64), 0
    for seg in (SEG_Q, SEG_K, SEG_V, SEG_Z):
        for hd in range(A_HEADS):
            col[o:o + LANES] = C_A + (hd // A_HPS) * 4 * A_HW + seg * A_HW + (hd % A_HPS) * LANES + np.arange(LANES)
            o += LANES
    for seg, width in ((SEG_Q, B_DK), (SEG_K, B_DK), (SEG_V, LANES), (SEG_Z, LANES)):
        for hd in range(B_HEADS):
            col[o:o + width] = C_B + hd * 4 * LANES + seg * LANES + np.arange(width)
            o += width
    col[o:o + GATE_RANK] = C_LR + LR_LANE + np.arange(GATE_RANK)
    o += GATE_RANK
    for seg in (0, 1):
        for hd in range(M_HEADS):
            col[o:o + LANES] = C_M + hd * 2 * LANES + seg * LANES + np.arange(LANES)
            o += LANES
    assert o == IN_WIDTH
    return col


def _runs(idx):
    out, start = [], 0
    for k in range(1, len(idx) + 1):
        if k == len(idx) or idx[k] != idx[k - 1] + 1:
            out.append((int(idx[start]), k - start))
            start = k
    return out


def _chip_columns(g, j, n):
    runs = _runs(_padded_col_of()[j * n:(j + 1) * n])
    return jnp.concatenate([g[:, a:a + ln] for a, ln in runs], axis=1)


def _padded_from_shards(shards):
    n = shards[0].shape[1]
    src = np.full(HP, -1, np.int64)
    src[_padded_col_of()] = np.arange(IN_WIDTH)
    parts, k = [], 0
    while k < HP:
        e = k + 1
        if src[k] < 0:
            while e < HP and src[e] < 0:
                e += 1
            parts.append(jnp.zeros((shards[0].shape[0], e - k), shards[0].dtype))
        else:
            while e < HP and src[e] == src[e - 1] + 1 and src[e] // n == src[k] // n:
                e += 1
            parts.append(shards[src[k] // n][:, src[k] % n:src[k] % n + e - k])
        k = e
    return jnp.concatenate(parts, axis=1)


ADAMW_BLOCK_BYTES = 1 << 20


def _adamw(w, g, m, v, name):
    L, R, C = w.shape
    tl, tr = 1, R
    if R * C * 4 <= ADAMW_BLOCK_BYTES:
        tl = max(d for d in range(1, L + 1) if L % d == 0 and d * R * C * 4 <= ADAMW_BLOCK_BYTES)
    else:
        for cand in (256, 128, 64, 32, 16, 8):
            if R % cand == 0 and R > cand:
                tr = cand
                break

    def body(w_ref, g_ref, m_ref, v_ref, d_ref, nm_ref, nv_ref):
        g_ = g_ref[...]
        nm = ADAM_B1 * m_ref[...] + (1.0 - ADAM_B1) * g_
        nv = ADAM_B2 * v_ref[...] + (1.0 - ADAM_B2) * (g_ * g_)
        m_hat = nm / (1.0 - ADAM_B1 ** ADAM_STEP)
        v_hat = nv / (1.0 - ADAM_B2 ** ADAM_STEP)
        d_ref[...] = -ADAM_LR * (m_hat / (jnp.sqrt(v_hat) + ADAM_EPS) + ADAM_WD * w_ref[...])
        nm_ref[...] = nm
        nv_ref[...] = nv

    spec = pl.BlockSpec((tl, tr, C), lambda l, i: (l, i, 0))
    sd = jax.ShapeDtypeStruct((L, R, C), F32)
    return pl.pallas_call(
        body, name=name, grid=(L // tl, R // tr), in_specs=[spec] * 4, out_specs=[spec] * 3, out_shape=[sd] * 3,
        compiler_params=_params(("parallel", "parallel")),
    )(w, g, m, v)


def _adamw_nd(w, g, m, v, name):
    shape = w.shape
    f = (lambda a: a) if w.ndim == 3 else (lambda a: a.reshape(1, shape[0], shape[1]))
    return tuple(o.reshape(shape) for o in _adamw(f(w), f(g), f(m), f(v), name))


ANY = pl.BlockSpec(memory_space=pl.ANY)


def _place():
    x, y, c = lax.axis_index("x"), lax.axis_index("y"), lax.axis_index("c")
    chips = [(1 - x, y), (x, 1 - y), (1 - x, 1 - y)]
    return x, y, c, chips


class _WeightGather:
    def __init__(self, hop, layer, rows):
        self.hop, self.layer, self.rows = hop, layer, rows
        self.n_sem = 3 * len(rows)

    def _copies(self, shard_refs, buf_refs, send, recv, received):
        x, y, c, chips = _place()
        out = []
        for t, R in enumerate(self.rows):
            half = R // 2
            assert half % 16 == 0
            mine = pl.ds(pl.multiple_of(c * half, 16), half)
            other = pl.ds(pl.multiple_of((1 - c) * half, 16), half)
            mine_of_shard = pl.ds(pl.multiple_of(self.layer * R + c * half, 16), half)
            for k, chip in enumerate(chips):
                theirs = buf_refs[t].at[2 * chip[0] + chip[1]]
                if self.hop == "chips":
                    src, dst, to = shard_refs[t].at[mine_of_shard], buf_refs[t].at[2 * x + y, mine], (*chip, c)
                    landed = theirs.at[mine]
                else:
                    src, dst, to = theirs.at[mine], theirs.at[mine], (x, y, 1 - c)
                    landed = theirs.at[other]
                out.append(pltpu.make_async_remote_copy(
                    src_ref=src, dst_ref=landed if received else dst, send_sem=send.at[3 * t + k],
                    recv_sem=recv.at[3 * t + k], device_id=to, device_id_type=MESH))
        return out

    def start(self, shard_refs, buf_refs, send, recv):
        for cp in self._copies(shard_refs, buf_refs, send, recv, False):
            cp.start()

    def wait(self, shard_refs, buf_refs, send, recv):
        for cp in self._copies(shard_refs, buf_refs, send, recv, True):
            cp.wait_recv()
        for cp in self._copies(shard_refs, buf_refs, send, recv, False):
            cp.wait_send()

    def sems(self):
        return [pltpu.SemaphoreType.DMA((self.n_sem,)), pltpu.SemaphoreType.DMA((self.n_sem,))]

    def call(self, srcs, bufs, name):
        ns, nb = len(srcs), len(bufs)

        def body(*refs):
            src_refs, buf_refs, (send, recv) = refs[:ns], refs[ns + nb:ns + 2 * nb], refs[ns + 2 * nb:]
            self.start(src_refs, buf_refs, send, recv)
            self.wait(src_refs, buf_refs, send, recv)

        return pl.pallas_call(
            body, name=name, in_specs=[ANY] * (ns + nb), out_specs=[ANY] * nb,
            out_shape=[jax.ShapeDtypeStruct(b.shape, b.dtype) for b in bufs],
            input_output_aliases={ns + t: t for t in range(nb)},
            scratch_shapes=self.sems(),
        )(*srcs, *bufs)


class _GradHop(_WeightGather):
    def __init__(self, hop, layer, slices):
        self.hop, self.layer, self.slices = hop, layer, slices
        self.n_sem = {"pair": N_CHIPS, "chips": N_CHIPS - 1, "gather": 1}[hop] * len(slices)

    def _copies(self, src_refs, buf_refs, send, recv, received):
        x, y, c, chips = _place()
        me, out = 2 * x + y, []

        def remote(src, dst, to):
            k = len(out)
            out.append(pltpu.make_async_remote_copy(src_ref=src, dst_ref=dst, send_sem=send.at[k], recv_sem=recv.at[k],
                                                    device_id=to, device_id_type=MESH))

        for t, (half, where) in enumerate(self.slices):
            if self.hop == "pair":
                for j, (a, first) in enumerate(where):
                    rows = pl.ds(pl.multiple_of(first + (1 - c) * half, 8), half)
                    remote(src_refs[a].at[rows], buf_refs[t].at[j], (x, y, 1 - c))
            elif self.hop == "chips":
                for chip in chips:
                    slot = 2 * chip[0] + chip[1]
                    remote(src_refs[t].at[slot], buf_refs[t].at[slot if received else me], (*chip, c))
            else:
                mine = buf_refs[t].at[self.layer, c]
                remote(mine, buf_refs[t].at[self.layer, 1 - c] if received else mine, (x, y, 1 - c))
        return out


def _add_halves(parts, got, c_idx, name):
    n, L, half, C = got.shape
    tr = 64

    def body(*refs):
        ins, (got_ref, o_ref) = refs[1:1 + len(parts)], refs[1 + len(parts):]
        for k in range(len(parts)):
            o_ref[k // L, k % L] = (ins[k][...] + got_ref[k // L, k % L]).astype(BF16)

    def rows_of(first):
        assert first % tr == 0 and half % tr == 0
        return lambda i, c: (first // tr + c[0] * (half // tr) + i, 0)

    whole = pl.BlockSpec((n, L, tr, C), lambda i, c: (0, 0, i, 0))
    return pl.pallas_call(
        body, name=name,
        grid_spec=pltpu.PrefetchScalarGridSpec(
            num_scalar_prefetch=1, grid=(half // tr,),
            in_specs=[pl.BlockSpec((tr, C), rows_of(first)) for _, first in parts] + [whole],
            out_specs=whole),
        out_shape=jax.ShapeDtypeStruct((n, L, half, C), BF16),
        compiler_params=_params(("parallel",)),
    )(c_idx, *[a for a, _ in parts], got)


def _add_slots(r, c_idx, dest, layer, name):
    n, half, C = r.shape
    tr = 256

    def body(c_ref, r_ref, dest_ref, o_ref):
        del dest_ref
        acc = r_ref[0].astype(F32)
        for j in range(1, n):
            acc = acc + r_ref[j].astype(F32)
        o_ref[0, 0] = acc

    return pl.pallas_call(
        body, name=name,
        grid_spec=pltpu.PrefetchScalarGridSpec(
            num_scalar_prefetch=1, grid=(half // tr,),
            in_specs=[pl.BlockSpec((n, tr, C), lambda i, c: (0, i, 0)), ANY],
            out_specs=pl.BlockSpec((1, 1, tr, C), lambda i, c: (layer, c[0], i, 0))),
        out_shape=jax.ShapeDtypeStruct(dest.shape, F32),
        input_output_aliases={2: 0},
        compiler_params=_params(("parallel",)),
    )(c_idx, r, dest)


class _LayerReduce:
    def __init__(self, layer, grads, slices, c_idx, chip, dests):
        self.layer, self.grads, self.slices, self.c_idx, self.chip, self.dests = layer, grads, slices, c_idx, chip, dests
        self.widths = [grads[where[0][0]].shape[1] for _, where in slices]
        self.stage = 0

    def _hop(self, kind):
        return _GradHop(kind, self.layer, self.slices)

    def rider(self):
        if self.stage == 0:
            got = [lax.empty((N_CHIPS, half, w), F32) for (half, _), w in zip(self.slices, self.widths)]
            return self._hop("pair"), self.grads, got
        if self.stage == 1:
            q = [lax.empty(p.shape, BF16) for p in self.pair_sums]
            return self._hop("chips"), self.pair_sums, q
        return self._hop("gather"), [], self.dests

    def landed(self, bufs):
        tag = f"{self.layer}"
        if self.stage == 0:
            self.pair_sums = []
            for t, ((half, where), got) in enumerate(zip(self.slices, bufs)):
                parts = [(self.grads[a], first) for a, first in where]
                p = _add_halves(parts, got[:, None], self.c_idx, f"rs_add2_{t}_{tag}")
                self.pair_sums.append(p.reshape(N_CHIPS, half, p.shape[-1]))
        elif self.stage == 1:
            for t, (q, p) in enumerate(zip(bufs, self.pair_sums)):
                q = lax.dynamic_update_slice_in_dim(q, lax.dynamic_slice_in_dim(p, self.chip, 1, axis=0), self.chip, axis=0)
                self.dests[t] = _add_slots(q, self.c_idx, self.dests[t], self.layer, f"rs_add4_{t}_{tag}")
        else:
            self.dests = list(bufs)
        self.stage += 1

    def step(self):
        hop, srcs, bufs = self.rider()
        self.landed(hop.call(srcs, bufs, f"rs_{hop.hop}_{self.layer}"))

    def finish(self):
        while self.stage < 3:
            self.step()
        return self.dests


def _all_reduce_small(buf, name):
    R = buf.shape[0]

    def flipped(k, x, y, c):
        return ((1 - x) if k & 4 else x, (1 - y) if k & 2 else y, (1 - c) if k & 1 else c)

    def body(b_ref, o_ref, land, send, recv):
        x, y, c, _ = _place()
        me = 4 * x + 2 * y + c
        land[me] = b_ref[...]
        cps = []
        for k in range(1, N_DEV):
            peer = flipped(k, x, y, c)
            cps.append(pltpu.make_async_remote_copy(src_ref=b_ref, dst_ref=land.at[me], send_sem=send.at[k - 1],
                                                    recv_sem=recv.at[k - 1], device_id=peer, device_id_type=MESH))
        for cp in cps:
            cp.start()
        for k in range(1, N_DEV):
            peer = flipped(k, x, y, c)
            slot = 4 * peer[0] + 2 * peer[1] + peer[2]
            pltpu.make_async_remote_copy(src_ref=b_ref, dst_ref=land.at[slot], send_sem=send.at[k - 1],
                                         recv_sem=recv.at[k - 1], device_id=peer, device_id_type=MESH).wait_recv()
        for cp in cps:
            cp.wait_send()
        acc = land[0]
        for j in range(1, N_DEV):
            acc = acc + land[j]
        o_ref[...] = acc

    vm = pl.BlockSpec(memory_space=pltpu.VMEM)
    return pl.pallas_call(
        body, name=name, in_specs=[vm], out_specs=vm,
        out_shape=jax.ShapeDtypeStruct((R, LANES), F32),
        scratch_shapes=[pltpu.VMEM((N_DEV, R, LANES), F32), pltpu.SemaphoreType.DMA((N_DEV - 1,)),
                        pltpu.SemaphoreType.DMA((N_DEV - 1,))],
    )(buf)


def kernel(x, mem, w_in, a_rel_bias, b_gate_w, b_gate_b, b_norm_g, w_mem_kv, w_out, ln_g, ln_b, loss_target, m_w_in, m_a_rel_bias, m_b_gate_w, m_b_gate_b, m_b_norm_g, m_w_mem_kv, m_w_out, m_ln_g, m_ln_b, v_w_in, v_a_rel_bias, v_b_gate_w, v_b_gate_b, v_b_norm_g, v_w_mem_kv, v_w_out, v_ln_g, v_ln_b):
    L = w_in.shape[0]
    S = x.shape[1]
    cx, cy, cc = lax.axis_index("x"), lax.axis_index("y"), lax.axis_index("c")
    chip = 2 * cx + cy
    c_idx = jnp.reshape(cc, (1,)).astype(jnp.int32)

    n_in, r_kv, r_out = w_in.shape[2], w_mem_kv.shape[1], w_out.shape[1]
    shards = [w.astype(BF16).reshape(-1, w.shape[2]) for w in (w_in, w_mem_kv, w_out)]
    rows = [D_MODEL, r_kv, r_out]

    def landing(l):
        return [lax.dynamic_update_slice_in_dim(lax.empty((N_CHIPS, r, s.shape[1]), BF16),
                                                s[l * r:(l + 1) * r][None], chip, axis=0)
                for s, r in zip(shards, rows)]

    def assembled(bufs):
        return (_padded_from_shards([bufs[0][j] for j in range(N_CHIPS)]),
                bufs[1].reshape(D_MODEL, bufs[1].shape[2]), bufs[2].reshape(D_MODEL, D_MODEL))

    bufs0 = _WeightGather("chips", 0, rows).call(shards, landing(0), "gather_chips_0")
    weights = [assembled(_WeightGather("pair", 0, rows).call(shards, bufs0, "gather_pair_0"))]

    gw_cols = b_gate_w.shape[2]
    gw_slot = jnp.zeros((N_CHIPS, L, GATE_RANK, gw_cols), F32)
    gw_slot = lax.dynamic_update_slice(gw_slot, (0.5 * b_gate_w)[None], (chip, 0, 0, 0))
    gw_flat = gw_slot.reshape(-1)
    n_gw = gw_flat.shape[0]
    pad = (-n_gw) % (8 * LANES)
    gw_full = _all_reduce_small(jnp.pad(gw_flat, (0, pad)).reshape(-1, LANES), "gather_gate_w").reshape(-1)[:n_gw]
    gw_full = gw_full.reshape(N_CHIPS, L, GATE_RANK, gw_cols).transpose(1, 2, 0, 3).reshape(L, GATE_RANK, B_KEY_WIDTH)

    xs = x.reshape(S, D_MODEL)
    mem_b = mem.reshape(mem.shape[1], D_MODEL).astype(BF16)
    target = loss_target.reshape(S, D_MODEL)

    small_w = []
    for l in range(L):
        gw_l, gb_l = _pad_gate(gw_full[l], b_gate_b[l])
        small_w.append((_bias_by_offset(a_rel_bias[l]), gw_l, gb_l,
                        b_norm_g[l].reshape(1, LANES), ln_g[l].reshape(1, D_MODEL), ln_b[l].reshape(1, D_MODEL)))

    y = xs
    yb, yt = _cast_transpose(xs, "x_t")
    saved = []
    for l in range(L):
        if l + 1 < L:
            rider = (_WeightGather("chips", l + 1, rows), shards, landing(l + 1))
            y, yb, yt, sv, bufs = _layer_fwd(y, yb, yt, mem_b, *weights[l], *small_w[l], rider=rider,
                                             next_hop=_WeightGather("pair", l + 1, rows))
            weights.append(assembled(bufs))
        else:
            y, yb, yt, sv = _layer_fwd(y, yb, yt, mem_b, *weights[l], *small_w[l])
        saved.append(sv)
    layer_w = [(*weights[l], *small_w[l]) for l in range(L)]
    loss_part, dy = _loss_head(y, target)

    halves = [D_MODEL // 2, r_kv // 2, r_out // 2]
    dests = [lax.empty((L, 2, hf, w.shape[2]), F32) for hf, w in zip(halves, (w_in, w_mem_kv, w_out))]
    slices = [(halves[0], [(j, 0) for j in range(N_CHIPS)]),
              (halves[1], [(N_CHIPS, j * r_kv) for j in range(N_CHIPS)]),
              (halves[2], [(N_CHIPS + 1, j * r_out) for j in range(N_CHIPS)])]

    def reduction(l, g, into):
        arrays = [_chip_columns(g[0], j, n_in) for j in range(N_CHIPS)] + [g[5], g[6]]
        return _LayerReduce(l, arrays, slices, c_idx, chip, into)

    grads, reduce = [None] * L, None
    for l in reversed(range(L)):
        w_in_l, w_kv_l, w_out_l, u_l, gw_l, gb_l, gn_l, lg_l, lb_l = layer_w[l]
        args = (dy, saved[l], mem_b, w_in_l, w_out_l, u_l, gw_l, gb_l, gn_l, lg_l)
        if l > 0:
            dy, grads[l] = _layer_bwd(*args, reduce=reduce)
            if reduce is not None:
                dests = reduce.finish()
            reduce = reduction(l, grads[l], dests)
        else:
            own = lambda g, above: reduction(0, g, dests if above is None else above)
            dy, grads[l], reduce = _layer_bwd(*args, reduce=reduce, own_reduce=own)
    r_w_in, r_w_kv, r_w_out = [d.reshape(L, 2 * d.shape[2], d.shape[3]) for d in reduce.finish()]
    grad_x = dy.reshape(x.shape)

    g_rel = jnp.stack([_bias_grad_from_offset(g[1]) for g in grads])
    g_gw = jnp.stack([_unpad_heads(g[2][LR_LANE:LR_LANE + GATE_RANK]) for g in grads])
    g_gb = jnp.stack([_unpad_heads(g[3])[0] for g in grads])
    g_gn = jnp.stack([g[4][0] for g in grads])
    g_lg = jnp.stack([g[7][0] for g in grads])
    g_lb = jnp.stack([g[8][0] for g in grads])

    small = [g_rel, g_gw, g_gb, g_gn, g_lg, g_lb, loss_part]
    flat = jnp.concatenate([s.reshape(-1) for s in small])
    n_small = flat.shape[0]
    pad = (-n_small) % (8 * LANES)
    red = _all_reduce_small(jnp.pad(flat, (0, pad)).reshape(-1, LANES), "all_reduce_small").reshape(-1)
    outs, off = [], 0
    for s in small:
        outs.append(red[off:off + s.size].reshape(s.shape))
        off += s.size
    g_rel, g_gw, g_gb, g_gn, g_lg, g_lb, loss = outs
    loss = loss.reshape(())
    g_gw = lax.dynamic_slice_in_dim(g_gw.reshape(L, GATE_RANK, N_CHIPS, gw_cols), chip, 1, axis=2).reshape(L, GATE_RANK, gw_cols)

    g_list = [r_w_in, g_rel, g_gw, g_gb, g_gn, r_w_kv, r_w_out, g_lg, g_lb]
    w_list = [w_in, a_rel_bias, b_gate_w, b_gate_b, b_norm_g, w_mem_kv, w_out, ln_g, ln_b]
    m_list = [m_w_in, m_a_rel_bias, m_b_gate_w, m_b_gate_b, m_b_norm_g, m_w_mem_kv, m_w_out, m_ln_g, m_ln_b]
    v_list = [v_w_in, v_a_rel_bias, v_b_gate_w, v_b_gate_b, v_b_norm_g, v_w_mem_kv, v_w_out, v_ln_g, v_ln_b]
    names = ["w_in", "rel", "gate_w", "gate_b", "norm_g", "w_kv", "w_out", "ln_g", "ln_b"]
    to_cols = lambda a: jnp.transpose(a, (2, 0, 1))
    upd = [tuple(jnp.transpose(o, (1, 2, 0)) for o in
                 _adamw(to_cols(w_in), to_cols(r_w_in), to_cols(m_w_in), to_cols(v_w_in), "adamw_w_in"))]
    upd += [_adamw_nd(w, g, m, v, "adamw_" + n)
            for w, g, m, v, n in list(zip(w_list, g_list, m_list, v_list, names))[1:]]
    deltas = [u_[0] for u_ in upd]
    new_m = [u_[1] for u_ in upd]
    new_v = [u_[2] for u_ in upd]
    return (loss, grad_x, *g_list, *deltas, *new_m, *new_v)
```
